```python
import jax
import jax.numpy as jnp
from jax import lax
import numpy as np

D_MODEL = 1024
BATCH = 8
SEQ = 2048
DEPTH = 2

GRID_W = 64
CTX_LEN = 256
EPS = 1e-6

FNET_WIDTH = 256
FNET_GROUPS = 4
FNET_GDIM = FNET_WIDTH // FNET_GROUPS

GLA_HEADS = 4
GLA_DV = 64
GLA_DK = 32
GLA_WIDTH = GLA_HEADS * GLA_DV
GLA_QK = GLA_HEADS * GLA_DK
GLA_GATE_RANK = 16
GLA_GATE_NORM = 16.0
GLA_CHUNK = 64

ATT_HEADS = 8
ATT_KV_HEADS = 2
ATT_HDIM = 64
ATT_WIDTH = ATT_HEADS * ATT_HDIM
ATT_KV_WIDTH = ATT_KV_HEADS * ATT_HDIM
ROPE_AXIS_DIM = ATT_HDIM // 2
ROPE_FREQS = ROPE_AXIS_DIM // 2
ROPE_THETA = 10000.0
Q_BLOCK = 128

D_MIX = FNET_WIDTH + GLA_WIDTH + ATT_WIDTH
IN_SIZES = (FNET_WIDTH, GLA_QK, GLA_QK, GLA_WIDTH, GLA_WIDTH, GLA_GATE_RANK, GLA_GATE_RANK, ATT_WIDTH, ATT_KV_WIDTH, ATT_KV_WIDTH)
D_IN = FNET_WIDTH + 2 * GLA_QK + 2 * GLA_WIDTH + 2 * GLA_GATE_RANK + ATT_WIDTH + 2 * ATT_KV_WIDTH

N_EXPERTS = 16
N_GROUPS = 4
EXPERTS_PER_GROUP = N_EXPERTS // N_GROUPS
TOP_K = 2
D_EXPERT = 512

kernel_name = 'hybrid_fnet_gla_gqa_moe_dit'


def rms_norm(x, w):
    xf = x.astype(jnp.float32)
    y = xf * lax.rsqrt(jnp.mean(xf * xf, axis=-1, keepdims=True) + EPS)
    return (y * w.astype(jnp.float32)).astype(x.dtype)


def split_proj(p):
    idx = []
    acc = 0
    for s in IN_SIZES[:-1]:
        acc += s
        idx.append(acc)
    return jnp.split(p, idx, axis=-1)


def axial_rope_tables(row, col):
    inv = ROPE_THETA ** (-jnp.arange(ROPE_FREQS, dtype=jnp.float32) * 2.0 / ROPE_AXIS_DIM)
    ang = jnp.stack([row.astype(jnp.float32)[:, None] * inv, col.astype(jnp.float32)[:, None] * inv], axis=1)
    return jnp.cos(ang)[:, None], jnp.sin(ang)[:, None]


def apply_axial_rope(x, cos, sin):
    xf = x.astype(jnp.float32).reshape(x.shape[:-1] + (2, 2, ROPE_FREQS))
    x1 = xf[..., 0, :]
    x2 = xf[..., 1, :]
    out = jnp.stack([x1 * cos - x2 * sin, x2 * cos + x1 * sin], axis=-2)
    return out.reshape(x.shape).astype(x.dtype)


def fourier_mix(u):
    b, n, _ = u.shape
    ug = u.astype(jnp.float32).reshape(b, n, FNET_GROUPS, FNET_GDIM).transpose(0, 2, 1, 3)
    f = jnp.fft.fft2(ug, norm='ortho').real
    return f.transpose(0, 2, 1, 3).reshape(b, n, FNET_WIDTH).astype(u.dtype)


def gla_heads(parts, w_up, b_up):
    gq, gk, gv, gdf, gdb = parts[1], parts[2], parts[3], parts[5], parts[6]
    b, n, _ = gq.shape

    def heads(t, d):
        return t.astype(jnp.float32).reshape(b, n, GLA_HEADS, d).transpose(0, 2, 1, 3)

    q = heads(gq, GLA_DK) * (GLA_DK ** -0.5)
    k = heads(gk, GLA_DK)
    v = heads(gv, GLA_DV)
    la_f = heads(jax.nn.log_sigmoid((gdf @ w_up[0] + b_up[0]).astype(jnp.float32)) / GLA_GATE_NORM, GLA_DK)
    la_b = heads(jax.nn.log_sigmoid((gdb @ w_up[1] + b_up[1]).astype(jnp.float32)) / GLA_GATE_NORM, GLA_DK)
    return q, k, v, la_f, la_b


def gla_chunk_scan(q, k, v, log_a, s0):
    b, h, n, dk = q.shape
    dv = v.shape[-1]
    nc = n // GLA_CHUNK

    def to_chunks(t):
        return t.reshape(b, h, nc, GLA_CHUNK, t.shape[-1]).transpose(2, 0, 1, 3, 4)

    mask = jnp.tril(jnp.ones((GLA_CHUNK, GLA_CHUNK), dtype=bool))

    def step(s, inp):
        qi, ki, vi, ai = inp
        cum = jnp.cumsum(ai, axis=-2)
        last = cum[..., -1:, :]
        q_t = qi * jnp.exp(cum)
        k_t = ki * jnp.exp(-cum)
        att = jnp.where(mask, jnp.einsum('bhik,bhjk->bhij', q_t, k_t), 0.0)
        o = jnp.einsum('bhij,bhjv->bhiv', att, vi) + jnp.einsum('bhik,bhkv->bhiv', q_t, s)
        s_new = jnp.exp(last)[..., 0, :, None] * s + jnp.einsum('bhjk,bhjv->bhkv', ki * jnp.exp(last - cum), vi)
        return s_new, o

    s_fin, oc = lax.scan(step, s0, (to_chunks(q), to_chunks(k), to_chunks(v), to_chunks(log_a)))
    o = oc.transpose(1, 2, 0, 3, 4).reshape(b, h, n, dv)
    return o, s_fin


def gla_bidir(q, k, v, la_f, la_b, s0_f, s0_b):
    o_f, s_f = gla_chunk_scan(q, k, v, la_f, s0_f)
    o_b, s_b = gla_chunk_scan(jnp.flip(q, 2), jnp.flip(k, 2), jnp.flip(v, 2), jnp.flip(la_b, 2), s0_b)
    return o_f + jnp.flip(o_b, 2), s_f, s_b


def gla_out(o, g, w_norm):
    b, h, n, dv = o.shape
    o = rms_norm(o, w_norm).transpose(0, 2, 1, 3).reshape(b, n, h * dv)
    return o.astype(g.dtype) * jax.nn.silu(g)


def attn_heads(parts, qn, kn, rope):
    aq, ak, av = parts[7], parts[8], parts[9]
    b, n, _ = aq.shape
    q = rms_norm(aq.reshape(b, n, ATT_HEADS, ATT_HDIM), qn)
    k = rms_norm(ak.reshape(b, n, ATT_KV_HEADS, ATT_HDIM), kn)
    v = av.reshape(b, n, ATT_KV_HEADS, ATT_HDIM)
    if rope is not None:
        q = apply_axial_rope(q, rope[0], rope[1])
        k = apply_axial_rope(k, rope[0], rope[1])
    return q, k, v


def gqa_blocks(q, k, v):
    b, n, h, hd = q.shape
    g = h // ATT_KV_HEADS
    nb = n // Q_BLOCK
    qb = q.reshape(b, nb, Q_BLOCK, ATT_KV_HEADS, g, hd).transpose(1, 0, 3, 4, 2, 5)
    kt = k.transpose(0, 2, 1, 3)
    vt = v.transpose(0, 2, 1, 3)
    scale = hd ** -0.5

    def one_block(qi):
        s = jnp.einsum('bkgqd,bkmd->bkgqm', qi, kt).astype(jnp.float32) * scale
        p = jax.nn.softmax(s, axis=-1).astype(vt.dtype)
        return jnp.einsum('bkgqm,bkmd->bkgqd', p, vt)

    ob = lax.map(one_block, qb)
    return ob.transpose(1, 0, 4, 2, 3, 5).reshape(b, n, h * hd)


def moe_ffn(h, w_router, b_router, w_gate, w_up, w_down):
    n_tok = h.shape[0]
    s = jax.nn.sigmoid(jnp.matmul(h, w_router).astype(jnp.float32))
    sel = (s + b_router.astype(jnp.float32)).reshape(n_tok, N_GROUPS, EXPERTS_PER_GROUP)
    grp_score = jnp.sum(lax.top_k(sel, TOP_K)[0], axis=-1)
    g_idx = jnp.argmax(grp_score, axis=-1)
    in_grp = sel[jnp.arange(n_tok), g_idx]
    loc = lax.top_k(in_grp, TOP_K)[1]
    e_idx = g_idx[:, None] * EXPERTS_PER_GROUP + loc
    w = jnp.take_along_axis(s, e_idx, axis=-1)
    w = w / jnp.sum(w, axis=-1, keepdims=True)
    gate = jnp.sum(w[..., None] * jax.nn.one_hot(e_idx, N_EXPERTS, dtype=jnp.float32), axis=1).astype(h.dtype)
    out = jnp.zeros_like(h)
    for e in range(N_EXPERTS):
        a = jax.nn.silu(h @ w_gate[e]) * (h @ w_up[e])
        out = out + gate[:, e:e + 1] * (a @ w_down[e])
    return out


def setup_inputs(seed: int = 0) -> dict:
    key = jax.random.key(seed)
    ks = jax.random.split(key, 24)

    def nrm(k, shape, s):
        return jax.random.normal(k, shape, jnp.float32) * s

    def gain(k, shape):
        return 1.0 + 0.02 * jax.random.normal(k, shape, jnp.float32)

    return {
        'x': nrm(ks[0], (BATCH, SEQ, D_MODEL), 1.0),
        'c': nrm(ks[1], (BATCH, D_MODEL), 1.0),
        'ctx': nrm(ks[2], (BATCH, CTX_LEN, D_MODEL), 1.0),
        'c_ctx': nrm(ks[3], (D_MODEL,), 1.0),
        'w_ada': nrm(ks[4], (DEPTH, D_MODEL, 6 * D_MODEL), 0.5 * D_MODEL ** -0.5),
        'b_ada': nrm(ks[5], (DEPTH, 6 * D_MODEL), 0.02),
        'norm_mix': gain(ks[6], (DEPTH, D_MODEL)),
        'norm_ffn': gain(ks[7], (DEPTH, D_MODEL)),
        'w_in': nrm(ks[8], (DEPTH, D_MODEL, D_IN), D_MODEL ** -0.5),
        'w_gla_gate_up': nrm(ks[9], (DEPTH, 2, GLA_GATE_RANK, GLA_QK), GLA_GATE_RANK ** -0.5),
        'b_gla_gate': nrm(ks[10], (DEPTH, 2, GLA_QK), 0.1),
        'gla_norm': gain(ks[11], (DEPTH, GLA_DV)),
        'q_norm': gain(ks[12], (DEPTH, ATT_HDIM)),
        'k_norm': gain(ks[13], (DEPTH, ATT_HDIM)),
        'w_out': nrm(ks[14], (DEPTH, D_MIX, D_MODEL), D_MIX ** -0.5),
        'w_router': nrm(ks[15], (D_MODEL, N_EXPERTS), D_MODEL ** -0.5),
        'b_router': nrm(ks[16], (N_EXPERTS,), 0.01),
        'w_exp_gate': nrm(ks[17], (DEPTH, N_EXPERTS, D_MODEL, D_EXPERT), D_MODEL ** -0.5),
        'w_exp_up': nrm(ks[18], (DEPTH, N_EXPERTS, D_MODEL, D_EXPERT), D_MODEL ** -0.5),
        'w_exp_down': nrm(ks[19], (DEPTH, N_EXPERTS, D_EXPERT, D_MODEL), D_EXPERT ** -0.5),
        'final_norm': gain(ks[20], (D_MODEL,)),
    }


def reference(x, c, ctx, c_ctx, w_ada, b_ada, norm_mix, norm_ffn, w_in, w_gla_gate_up, b_gla_gate, gla_norm, q_norm, k_norm, w_out, w_router, b_router, w_exp_gate, w_exp_up, w_exp_down, final_norm):
    b, n, d = x.shape
    m = ctx.shape[1]
    rows = n // GRID_W
    row = jnp.repeat(jnp.arange(rows), GRID_W)
    col = jnp.tile(jnp.arange(GRID_W), rows)
    rope = axial_rope_tables(row, col)
    silu_c = jax.nn.silu(c)
    silu_cc = jax.nn.silu(c_ctx)
    xc = ctx
    for l in range(DEPTH):
        ctx_out = l < DEPTH - 1
        mod_x = (silu_c @ w_ada[l] + b_ada[l])[:, None, :]
        mod_k = (silu_cc @ w_ada[l] + b_ada[l])[None, None, :]
        shx1, scx1, gx1, shx2, scx2, gx2 = jnp.split(mod_x, 6, axis=-1)
        shk1, sck1, gk1, shk2, sck2, gk2 = jnp.split(mod_k, 6, axis=-1)

        hx = rms_norm(x, norm_mix[l]) * (1.0 + scx1) + shx1
        hk = rms_norm(xc, norm_mix[l]) * (1.0 + sck1) + shk1
        px = split_proj(hx @ w_in[l])
        pk = split_proj(hk @ w_in[l])

        gla_x = gla_heads(px, w_gla_gate_up[l], b_gla_gate[l])
        gla_k = gla_heads(pk, w_gla_gate_up[l], b_gla_gate[l])
        s_zero = jnp.zeros((b, GLA_HEADS, GLA_DK, GLA_DV), jnp.float32)
        ok_gla, s_f, s_b = gla_bidir(*gla_k, s_zero, s_zero)
        ox_gla, _, _ = gla_bidir(*gla_x, s_f, s_b)

        qx, kx, vx = attn_heads(px, q_norm[l], k_norm[l], rope)
        qk_, kk_, vk_ = attn_heads(pk, q_norm[l], k_norm[l], None)
        ox_att = gqa_blocks(qx, jnp.concatenate([kk_, kx], axis=1), jnp.concatenate([vk_, vx], axis=1))

        ox = jnp.concatenate([fourier_mix(px[0]), gla_out(ox_gla, px[4], gla_norm[l]), ox_att], axis=-1) @ w_out[l]
        x = x + gx1 * ox
        if ctx_out:
            ok = jnp.concatenate([fourier_mix(pk[0]), gla_out(ok_gla, pk[4], gla_norm[l]), gqa_blocks(qk_, kk_, vk_)], axis=-1) @ w_out[l]
            xc = xc + gk1 * ok

        hx2 = rms_norm(x, norm_ffn[l]) * (1.0 + scx2) + shx2
        if ctx_out:
            hk2 = rms_norm(xc, norm_ffn[l]) * (1.0 + sck2) + shk2
            tokens = jnp.concatenate([hx2.reshape(b * n, d), hk2.reshape(b * m, d)], axis=0)
            y = moe_ffn(tokens, w_router, b_router, w_exp_gate[l], w_exp_up[l], w_exp_down[l])
            x = x + gx2 * y[:b * n].reshape(b, n, d)
            xc = xc + gk2 * y[b * n:].reshape(b, m, d)
        else:
            y = moe_ffn(hx2.reshape(b * n, d), w_router, b_router, w_exp_gate[l], w_exp_up[l], w_exp_down[l])
            x = x + gx2 * y.reshape(b, n, d)
    return rms_norm(x, final_norm)
```

```python
import functools

import numpy as np
import jax
import jax.numpy as jnp
from jax import lax
from jax.experimental import pallas as pl
from jax.experimental.pallas import tpu as pltpu

F32 = jnp.float32
BF16 = jnp.bfloat16

D_MODEL = 1024
GRID_W = 64
EPS = 1e-6

FNET_WIDTH = 256
FNET_GROUPS = 4
FNET_GDIM = 64

GLA_HEADS = 4
GLA_DV = 64
GLA_DK = 32
GLA_WIDTH = 256
GLA_QK = 128
GLA_GATE_RANK = 16
GLA_GATE_NORM = 16.0
GLA_CHUNK = 64
GLA_PAIR = 2 * GLA_CHUNK

ATT_HEADS = 8
ATT_KV_HEADS = 2
ATT_HDIM = 64
ATT_WIDTH = 512
ROPE_FREQS = 16
ROPE_THETA = 10000.0

N_EXPERTS = 16
N_GROUPS = 4
EXPERTS_PER_GROUP = 4
D_EXPERT = 512

C_U = 0
C_GQ = 256
C_GK = 384
C_GV = 512
C_GG = 768
C_AQ = 1024
C_AK = 1536
C_AV = 1792
C_GD = 2048
W_IN_COLS = 2176

VMEM_LIMIT = 56 * 1024 * 1024


def _cparams(sem):
    return pltpu.CompilerParams(dimension_semantics=sem, vmem_limit_bytes=VMEM_LIMIT)


def _sigmoid(x):
    return 1.0 / (1.0 + jnp.exp(-x))


def _nt_dot(a, b):
    return lax.dot_general(a, b, (((1,), (1,)), ((), ())), preferred_element_type=F32)


@functools.lru_cache(maxsize=None)
def _channel_dft_table():
    j = np.arange(FNET_GDIM)
    ang = 2.0 * np.pi * ((j[:, None] * j[None, :]) % FNET_GDIM) / FNET_GDIM
    c = np.cos(ang) / np.sqrt(FNET_GDIM)
    s = np.sin(ang) / np.sqrt(FNET_GDIM)
    out = np.zeros((FNET_WIDTH, 2 * FNET_WIDTH), np.float64)
    for g in range(FNET_GROUPS):
        sl = slice(g * FNET_GDIM, (g + 1) * FNET_GDIM)
        out[sl, sl] = c
        out[sl, FNET_WIDTH + g * FNET_GDIM:FNET_WIDTH + (g + 1) * FNET_GDIM] = s
    return out.astype(np.float32)


@functools.lru_cache(maxsize=None)
def _seq_dft_table(n):
    j = np.arange(n, dtype=np.int64)
    ang = 2.0 * np.pi * ((j[:, None] * j[None, :]) % n) / n
    return np.concatenate([np.cos(ang), -np.sin(ang)], axis=1).astype(np.float32) / np.float32(np.sqrt(n))


@functools.lru_cache(maxsize=None)
def _rope_tables(n):
    rows = n // GRID_W
    row = np.repeat(np.arange(rows), GRID_W).astype(np.float64)
    col = np.tile(np.arange(GRID_W), rows).astype(np.float64)
    inv = ROPE_THETA ** (-np.arange(ROPE_FREQS, dtype=np.float64) * 2.0 / (2 * ROPE_FREQS))
    ar = row[:, None] * inv[None, :]
    ac = col[:, None] * inv[None, :]
    cos = np.concatenate([np.cos(ar), np.cos(ar), np.cos(ac), np.cos(ac)], axis=1)
    sin = np.concatenate([-np.sin(ar), np.sin(ar), -np.sin(ac), np.sin(ac)], axis=1)
    return (np.tile(cos, (1, 2)).astype(np.float32), np.tile(sin, (1, 2)).astype(np.float32))


@functools.lru_cache(maxsize=None)
def _blockdiag_ones(width, blk):
    i = np.arange(width)
    return (i[:, None] // blk == i[None, :] // blk).astype(np.float32)


def _ada_kernel(cv_ref, w_ref, b_ref, o_ref):
    cv = cv_ref[...]
    a = (cv * _sigmoid(cv)).astype(BF16)
    o_ref[...] = jnp.dot(a, w_ref[...].astype(BF16), preferred_element_type=F32) + b_ref[...]


def _ada_call(cv, w_ada, b_ada):
    depth, d, d6 = w_ada.shape
    tn = 1536
    rows = cv.shape[0]
    return pl.pallas_call(
        _ada_kernel,
        grid=(depth, d6 // tn),
        in_specs=[
            pl.BlockSpec((rows, d), lambda l, j: (0, 0)),
            pl.BlockSpec((None, d, tn), lambda l, j: (l, 0, j)),
            pl.BlockSpec((None, 1, tn), lambda l, j: (l, 0, j)),
        ],
        out_specs=pl.BlockSpec((None, rows, tn), lambda l, j: (l, 0, j)),
        out_shape=jax.ShapeDtypeStruct((depth, rows, d6), F32),
        compiler_params=_cparams(("parallel", "parallel")),
        name="ada_mod",
    )(cv, w_ada, b_ada.reshape(depth, 1, d6))


def _swap16(x):
    lane = lax.broadcasted_iota(jnp.int32, x.shape, 1)
    first = (lane % 32) < 16
    return jnp.where(first, pltpu.roll(x, 112, 1), pltpu.roll(x, 16, 1))


def _head_rms(x, bd, w):
    ms = jnp.dot((x * x).astype(BF16), bd, preferred_element_type=F32) * (1.0 / ATT_HDIM)
    return x * lax.rsqrt(ms + EPS) * w


def _inproj_kernel(*refs, rope):
    if rope:
        (x_ref, mod_ref, nw_ref, w_ref, cs_ref, wup_ref, bup_ref, qn_ref, kn_ref, bd_ref, cos_ref, sin_ref,
         ab_ref, gqk_ref, gv_ref, gvt_ref, gg_ref, la_ref, q_ref, kt_ref, vd_ref) = refs
    else:
        (x_ref, mod_ref, nw_ref, w_ref, cs_ref, wup_ref, bup_ref, qn_ref, kn_ref, bd_ref,
         ab_ref, gqk_ref, gv_ref, gvt_ref, gg_ref, la_ref, q_ref, kt_ref, vd_ref) = refs
    x = x_ref[...]
    ms = jnp.mean(x * x, axis=-1, keepdims=True)
    y = x * lax.rsqrt(ms + EPS) * nw_ref[...]
    h = y * (1.0 + mod_ref[1:2, :]) + mod_ref[0:1, :]
    acc = jnp.dot(h.astype(BF16), w_ref[...], preferred_element_type=F32)

    uab = jnp.dot(acc[:, C_U:C_U + FNET_WIDTH].astype(BF16), cs_ref[...], preferred_element_type=F32)
    ab_ref[0] = uab[:, :FNET_WIDTH].astype(BF16)
    ab_ref[1] = uab[:, FNET_WIDTH:].astype(BF16)

    gqk_ref[...] = acc[:, C_GQ:C_GQ + 2 * GLA_QK]
    gv = acc[:, C_GV:C_GV + GLA_WIDTH]
    gv_ref[...] = gv.astype(BF16)
    gvt_ref[...] = gv.T.astype(BF16)
    gg_ref[...] = acc[:, C_GG:C_GG + GLA_WIDTH].astype(BF16)
    pre = jnp.dot(acc[:, C_GD:C_GD + 128].astype(BF16), wup_ref[...], preferred_element_type=F32) + bup_ref[...]
    la_ref[...] = (jnp.minimum(pre, 0.0) - jnp.log1p(jnp.exp(-jnp.abs(pre)))) * (1.0 / GLA_GATE_NORM)

    bd = bd_ref[...]
    q = _head_rms(acc[:, C_AQ:C_AQ + ATT_WIDTH], bd, qn_ref[...])
    k = _head_rms(acc[:, C_AK:C_AK + 256], bd[:256, :256], kn_ref[...])
    if rope:
        cos = cos_ref[...]
        sin = sin_ref[...]
        q = jnp.concatenate(
            [q[:, s:s + 128] * cos + _swap16(q[:, s:s + 128]) * sin for s in range(0, ATT_WIDTH, 128)], axis=1)
        k = jnp.concatenate(
            [k[:, s:s + 128] * cos + _swap16(k[:, s:s + 128]) * sin for s in range(0, 256, 128)], axis=1)
    q_ref[...] = (q * (ATT_HDIM ** -0.5)).astype(BF16)
    v = acc[:, C_AV:C_AV + 256]
    for g in range(ATT_KV_HEADS):
        kt_ref[g] = k[:, 128 * g:128 * (g + 1)].T.astype(BF16)
        vd_ref[g] = v[:, 128 * g:128 * (g + 1)].astype(BF16)


def _inproj_call(x, mod_l, mod_row, nw, w_perm, cs, wup, bup, qn, kn, bd, rope_tabs):
    b, n, d = x.shape
    tm = min(512, n)
    nt = n // tm
    rope = rope_tabs is not None
    if mod_row is None:
        mod_map = lambda bi, i: (bi, 0, 0)
    else:
        mod_map = lambda bi, i: (mod_row, 0, 0)
    const = lambda bi, i: (0, 0)
    in_specs = [
        pl.BlockSpec((None, tm, d), lambda bi, i: (bi, i, 0)),
        pl.BlockSpec((None, 6, d), mod_map),
        pl.BlockSpec((1, d), const),
        pl.BlockSpec((d, W_IN_COLS), const),
        pl.BlockSpec((FNET_WIDTH, 2 * FNET_WIDTH), const),
        pl.BlockSpec((128, 2 * GLA_QK), const),
        pl.BlockSpec((1, 2 * GLA_QK), const),
        pl.BlockSpec((1, ATT_WIDTH), const),
        pl.BlockSpec((1, 256), const),
        pl.BlockSpec((ATT_WIDTH, ATT_WIDTH), const),
    ]
    args = [x, mod_l, nw, w_perm, cs, wup, bup, qn, kn, bd]
    if rope:
        in_specs += [pl.BlockSpec((tm, 128), lambda bi, i: (i, 0)), pl.BlockSpec((tm, 128), lambda bi, i: (i, 0))]
        args += list(rope_tabs)
    out_shape = (
        jax.ShapeDtypeStruct((2, n, b * FNET_WIDTH), BF16),
        jax.ShapeDtypeStruct((b, n, 2 * GLA_QK), F32),
        jax.ShapeDtypeStruct((b, n, GLA_WIDTH), BF16),
        jax.ShapeDtypeStruct((b, GLA_WIDTH, n), BF16),
        jax.ShapeDtypeStruct((b, n, GLA_WIDTH), BF16),
        jax.ShapeDtypeStruct((b, n, 2 * GLA_QK), F32),
        jax.ShapeDtypeStruct((b, n, ATT_WIDTH), BF16),
        jax.ShapeDtypeStruct((b, ATT_KV_HEADS, 128, n), BF16),
        jax.ShapeDtypeStruct((b, ATT_KV_HEADS, n, 128), BF16),
    )
    out_specs = (
        pl.BlockSpec((2, tm, FNET_WIDTH), lambda bi, i: (0, i, bi)),
        pl.BlockSpec((None, tm, 2 * GLA_QK), lambda bi, i: (bi, i, 0)),
        pl.BlockSpec((None, tm, GLA_WIDTH), lambda bi, i: (bi, i, 0)),
        pl.BlockSpec((None, GLA_WIDTH, tm), lambda bi, i: (bi, 0, i)),
        pl.BlockSpec((None, tm, GLA_WIDTH), lambda bi, i: (bi, i, 0)),
        pl.BlockSpec((None, tm, 2 * GLA_QK), lambda bi, i: (bi, i, 0)),
        pl.BlockSpec((None, tm, ATT_WIDTH), lambda bi, i: (bi, i, 0)),
        pl.BlockSpec((None, ATT_KV_HEADS, 128, tm), lambda bi, i: (bi, 0, 0, i)),
        pl.BlockSpec((None, ATT_KV_HEADS, tm, 128), lambda bi, i: (bi, 0, i, 0)),
    )
    return pl.pallas_call(
        functools.partial(_inproj_kernel, rope=rope),
        grid=(b, nt),
        in_specs=in_specs,
        out_specs=out_specs,
        out_shape=out_shape,
        compiler_params=_cparams(("parallel", "parallel")),
        name="inproj_rope" if rope else "inproj_ctx",
    )(*args)


def _seqdft_kernel(t_ref, ab_ref, o_ref):
    y = jnp.dot(t_ref[...], ab_ref[...], preferred_element_type=F32)
    for bb in range(o_ref.shape[0]):
        o_ref[bb] = y[:, bb * FNET_WIDTH:(bb + 1) * FNET_WIDTH].astype(BF16)


def _seqdft_call(table, ab, b):
    n = table.shape[0]
    tm = min(512, n)
    nb = 4 if b % 4 == 0 else (2 if b % 2 == 0 else 1)
    return pl.pallas_call(
        _seqdft_kernel,
        grid=(b // nb, n // tm),
        in_specs=[
            pl.BlockSpec((tm, 2 * n), lambda c, i: (i, 0)),
            pl.BlockSpec((2 * n, nb * FNET_WIDTH), lambda c, i: (0, c)),
        ],
        out_specs=pl.BlockSpec((nb, tm, FNET_WIDTH), lambda c, i: (c, i, 0)),
        out_shape=jax.ShapeDtypeStruct((b, n, FNET_WIDTH), BF16),
        compiler_params=_cparams(("parallel", "parallel")),
        name="seq_dft",
    )(table, ab)


def _gla_dir(qk, v, vt, a, s_in, fwd):
    p = GLA_PAIR
    r = lax.broadcasted_iota(jnp.int32, (p, p), 0)
    c = lax.broadcasted_iota(jnp.int32, (p, p), 1)
    same = (r // GLA_CHUNK) == (c // GLA_CHUNK)
    tri = same & ((c <= r) if fwd else (c >= r))
    tri_b = jnp.where(tri, 1.0, 0.0).astype(BF16)
    row_lo = r < GLA_CHUNK

    q = qk[:, :GLA_QK] * (GLA_DK ** -0.5)
    k = qk[:, GLA_QK:]
    a1 = a.astype(BF16)
    r1 = a - a1.astype(F32)
    a2 = r1.astype(BF16)
    a3 = (r1 - a2.astype(F32)).astype(BF16)
    cum = (jnp.dot(tri_b, a1, preferred_element_type=F32) + jnp.dot(tri_b, a2, preferred_element_type=F32)
           + jnp.dot(tri_b, a3, preferred_element_type=F32))
    if fwd:
        last0, last1 = cum[GLA_CHUNK - 1:GLA_CHUNK, :], cum[p - 1:p, :]
    else:
        last0, last1 = cum[0:1, :], cum[GLA_CHUNK:GLA_CHUNK + 1, :]
    lastb = jnp.where(row_lo, last0, last1)
    qt = q * jnp.exp(cum)
    kt = k * jnp.exp(-cum)
    kd = k * jnp.exp(lastb - cum)

    lane = c
    qs = jnp.concatenate(
        [jnp.where((lane // GLA_DK) == hh, qt, 0.0) for hh in range(GLA_HEADS)], axis=0).astype(BF16)
    att = _nt_dot(qs, kt.astype(BF16))
    tri4 = jnp.concatenate([tri] * GLA_HEADS, axis=0)
    att = jnp.where(tri4, att, 0.0).astype(BF16)
    of = jnp.dot(att, v, preferred_element_type=F32)
    col = lax.broadcasted_iota(jnp.int32, (p, GLA_WIDTH), 1)
    o_intra = jnp.zeros((p, GLA_WIDTH), F32)
    for hh in range(GLA_HEADS):
        o_intra = o_intra + jnp.where((col // GLA_DV) == hh, of[hh * p:(hh + 1) * p, :], 0.0)

    sr = lax.broadcasted_iota(jnp.int32, (GLA_WIDTH, GLA_QK), 0)
    sc = lax.broadcasted_iota(jnp.int32, (GLA_WIDTH, GLA_QK), 1)
    bdm = (sr // GLA_DV) == (sc // GLA_DK)
    order = (0, 1) if fwd else (1, 0)
    lasts = (last0, last1)
    s_cur = s_in
    o_parts = [None, None]
    for ci in order:
        rows = slice(ci * GLA_CHUNK, (ci + 1) * GLA_CHUNK)
        o_parts[ci] = _nt_dot(qt[rows, :].astype(BF16), s_cur.astype(BF16))
        in_chunk = row_lo if ci == 0 else jnp.logical_not(row_lo)
        kvt = jnp.dot(vt, jnp.where(in_chunk, kd, 0.0).astype(BF16), preferred_element_type=F32)
        s_cur = s_cur * jnp.exp(lasts[ci]) + jnp.where(bdm, kvt, 0.0)
    o_inter = jnp.concatenate(o_parts, axis=0)
    return o_intra + o_inter, s_cur


def _gla_kernel(qkf, vf, vtf, laf, qkb, vb, vtb, lab, s0_ref, of_ref, ob_ref, sfin_ref, s_scr):
    i = pl.program_id(1)

    @pl.when(i == 0)
    def _():
        s_scr[...] = s0_ref[...]

    o1, sf = _gla_dir(qkf[...], vf[...], vtf[...], laf[...], s_scr[0], True)
    o2, sb = _gla_dir(qkb[...], vb[...], vtb[...], lab[...], s_scr[1], False)
    of_ref[...] = o1
    ob_ref[...] = o2
    s_scr[0] = sf
    s_scr[1] = sb

    @pl.when(i == pl.num_programs(1) - 1)
    def _():
        sfin_ref[0] = sf
        sfin_ref[1] = sb


def _gla_call(gqk, gv, gvt, la, s0):
    b, n, _ = gqk.shape
    p = GLA_PAIR
    npair = n // p
    fw = lambda bi, i: (bi, i, 0)
    bw = lambda bi, i: (bi, npair - 1 - i, 0)
    in_specs = [
        pl.BlockSpec((None, p, 2 * GLA_QK), fw),
        pl.BlockSpec((None, p, GLA_WIDTH), fw),
        pl.BlockSpec((None, GLA_WIDTH, p), lambda bi, i: (bi, 0, i)),
        pl.BlockSpec((None, p, GLA_QK), fw),
        pl.BlockSpec((None, p, 2 * GLA_QK), bw),
        pl.BlockSpec((None, p, GLA_WIDTH), bw),
        pl.BlockSpec((None, GLA_WIDTH, p), lambda bi, i: (bi, 0, npair - 1 - i)),
        pl.BlockSpec((None, p, GLA_QK), lambda bi, i: (bi, npair - 1 - i, 1)),
        pl.BlockSpec((None, 2, GLA_WIDTH, GLA_QK), lambda bi, i: (bi, 0, 0, 0)),
    ]
    out_specs = (
        pl.BlockSpec((None, p, GLA_WIDTH), fw),
        pl.BlockSpec((None, p, GLA_WIDTH), bw),
        pl.BlockSpec((None, 2, GLA_WIDTH, GLA_QK), lambda bi, i: (bi, 0, 0, 0)),
    )
    out_shape = (
        jax.ShapeDtypeStruct((b, n, GLA_WIDTH), F32),
        jax.ShapeDtypeStruct((b, n, GLA_WIDTH), F32),
        jax.ShapeDtypeStruct((b, 2, GLA_WIDTH, GLA_QK), F32),
    )
    return pl.pallas_call(
        _gla_kernel,
        grid=(b, npair),
        in_specs=in_specs,
        out_specs=out_specs,
        out_shape=out_shape,
        scratch_shapes=[pltpu.VMEM((2, GLA_WIDTH, GLA_QK), F32)],
        compiler_params=_cparams(("parallel", "arbitrary")),
        name="gla_scan",
    )(gqk, gv, gvt, la, gqk, gv, gvt, la, s0)


def _attn_kernel(*refs, nparts):
    q_ref = refs[0]
    parts = [(refs[1 + 2 * i], refs[2 + 2 * i]) for i in range(nparts)]
    o_ref = refs[1 + 2 * nparts]
    tq = q_ref.shape[0]
    lane = lax.broadcasted_iota(jnp.int32, (tq, 128), 1)
    lo = lane < ATT_HDIM
    for j in range(ATT_HEADS // 2):
        q128 = q_ref[:, 128 * j:128 * (j + 1)]
        g = (2 * j) // (ATT_HEADS // ATT_KV_HEADS)
        outs = []
        for half in range(2):
            qm = jnp.where(lo if half == 0 else jnp.logical_not(lo), q128, jnp.zeros_like(q128))
            ss = [jnp.dot(qm, kt_ref[g], preferred_element_type=F32) for kt_ref, _ in parts]
            m = functools.reduce(jnp.maximum, [jnp.max(s, axis=-1, keepdims=True) for s in ss])
            ps = [jnp.exp(s - m) for s in ss]
            den = functools.reduce(lambda u, w: u + w, [jnp.sum(pp, axis=-1, keepdims=True) for pp in ps])
            pv = functools.reduce(
                lambda u, w: u + w,
                [jnp.dot(pp.astype(BF16), vd_ref[g], preferred_element_type=F32) for pp, (_, vd_ref) in zip(ps, parts)])
            outs.append(pv / den)
        o_ref[:, 128 * j:128 * (j + 1)] = jnp.where(lo, outs[0], outs[1]).astype(BF16)


def _attn_call(q, kv_parts):
    b, n, _ = q.shape
    tq = min(256, n)
    in_specs = [pl.BlockSpec((None, tq, ATT_WIDTH), lambda bi, i: (bi, i, 0))]
    args = [q]
    for kt, vd in kv_parts:
        m = kt.shape[-1]
        in_specs.append(pl.BlockSpec((None, ATT_KV_HEADS, 128, m), lambda bi, i: (bi, 0, 0, 0)))
        in_specs.append(pl.BlockSpec((None, ATT_KV_HEADS, m, 128), lambda bi, i: (bi, 0, 0, 0)))
        args += [kt, vd]
    return pl.pallas_call(
        functools.partial(_attn_kernel, nparts=len(kv_parts)),
        grid=(b, n // tq),
        in_specs=in_specs,
        out_specs=pl.BlockSpec((None, tq, ATT_WIDTH), lambda bi, i: (bi, i, 0)),
        out_shape=jax.ShapeDtypeStruct((b, n, ATT_WIDTH), BF16),
        compiler_params=_cparams(("parallel", "parallel")),
        name="gqa_attn",
    )(*args)


def _outproj_kernel(f_ref, of_ref, ob_ref, gg_ref, a_ref, x_ref, mod_ref, w_ref, gn_ref, bd_ref, nf_ref,
                    wrh_ref, wrl_ref, xn_ref, h2_ref, lg_ref):
    o = of_ref[...] + ob_ref[...]
    ms = jnp.dot((o * o).astype(BF16), bd_ref[...], preferred_element_type=F32) * (1.0 / GLA_DV)
    on = o * lax.rsqrt(ms + EPS) * gn_ref[...]
    g = gg_ref[...].astype(F32)
    gl = (on * (g * _sigmoid(g))).astype(BF16)
    ox = (jnp.dot(f_ref[...], w_ref[0:FNET_WIDTH, :], preferred_element_type=F32)
          + jnp.dot(gl, w_ref[FNET_WIDTH:FNET_WIDTH + GLA_WIDTH, :], preferred_element_type=F32)
          + jnp.dot(a_ref[...], w_ref[FNET_WIDTH + GLA_WIDTH:, :], preferred_element_type=F32))
    xn = x_ref[...] + mod_ref[2:3, :] * ox
    xn_ref[...] = xn
    ms2 = jnp.mean(xn * xn, axis=-1, keepdims=True)
    h2 = xn * lax.rsqrt(ms2 + EPS) * nf_ref[...] * (1.0 + mod_ref[4:5, :]) + mod_ref[3:4, :]
    hh = h2.astype(BF16)
    hl = (h2 - hh.astype(F32)).astype(BF16)
    h2_ref[...] = hh
    lg_ref[...] = _nt_dot(wrh_ref[...], hh) + _nt_dot(wrh_ref[...], hl) + _nt_dot(wrl_ref[...], hh)


def _outproj_call(f, of, ob, gg, a, x, mod_l, mod_row, w_out, gn, bd, nf, wrh, wrl):
    b, n, d = x.shape
    tm = min(512, n)
    nt = n // tm
    if mod_row is None:
        mod_map = lambda bi, i: (bi, 0, 0)
    else:
        mod_map = lambda bi, i: (mod_row, 0, 0)
    const = lambda bi, i: (0, 0)
    tok = lambda w: pl.BlockSpec((None, tm, w), lambda bi, i: (bi, i, 0))
    in_specs = [
        tok(FNET_WIDTH), tok(GLA_WIDTH), tok(GLA_WIDTH), tok(GLA_WIDTH), tok(ATT_WIDTH), tok(d),
        pl.BlockSpec((None, 6, d), mod_map),
        pl.BlockSpec((d, d), const),
        pl.BlockSpec((1, GLA_WIDTH), const),
        pl.BlockSpec((GLA_WIDTH, GLA_WIDTH), const),
        pl.BlockSpec((1, d), const),
        pl.BlockSpec((N_EXPERTS, d), const),
        pl.BlockSpec((N_EXPERTS, d), const),
    ]
    out_specs = (
        tok(d), tok(d),
        pl.BlockSpec((N_EXPERTS, tm), lambda bi, i: (0, bi * nt + i)),
    )
    out_shape = (
        jax.ShapeDtypeStruct((b, n, d), F32),
        jax.ShapeDtypeStruct((b, n, d), BF16),
        jax.ShapeDtypeStruct((N_EXPERTS, b * n), F32),
    )
    return pl.pallas_call(
        _outproj_kernel,
        grid=(b, nt),
        in_specs=in_specs,
        out_specs=out_specs,
        out_shape=out_shape,
        compiler_params=_cparams(("parallel", "parallel")),
        name="outproj_router",
    )(f, of, ob, gg, a, x, mod_l, w_out, gn, bd, nf, wrh, wrl)


def _route_kernel(b_ref, lg_ref, g_ref):
    s = [_sigmoid(lg_ref[e]) for e in range(N_EXPERTS)]
    sel = [s[e] + b_ref[e] for e in range(N_EXPERTS)]
    grp = []
    for g in range(N_GROUPS):
        a, b, c, d = sel[4 * g:4 * g + 4]
        hi1, lo1 = jnp.maximum(a, b), jnp.minimum(a, b)
        hi2, lo2 = jnp.maximum(c, d), jnp.minimum(c, d)
        m1 = jnp.maximum(hi1, hi2)
        m2 = jnp.maximum(jnp.minimum(hi1, hi2), jnp.maximum(lo1, lo2))
        grp.append(m1 + m2)
    one = jnp.ones_like(s[0])
    zero = jnp.zeros_like(s[0])
    w = []
    for g in range(N_GROUPS):
        isg = one
        for g2 in range(N_GROUPS):
            if g2 < g:
                isg = isg * jnp.where(grp[g] > grp[g2], one, zero)
            elif g2 > g:
                isg = isg * jnp.where(grp[g] >= grp[g2], one, zero)
        for li in range(EXPERTS_PER_GROUP):
            e = 4 * g + li
            rank = zero
            for lj in range(EXPERTS_PER_GROUP):
                ej = 4 * g + lj
                if lj < li:
                    rank = rank + jnp.where(sel[ej] >= sel[e], one, zero)
                elif lj > li:
                    rank = rank + jnp.where(sel[ej] > sel[e], one, zero)
            w.append(jnp.where(rank < 2.0, isg, zero) * s[e])
    den = functools.reduce(lambda u, v: u + v, w)
    for e in range(N_EXPERTS):
        g_ref[e] = w[e] / den


def _route_call(lgt, b_router):
    n_tok = lgt.shape[1]
    r = n_tok // 128
    lg3 = lgt.reshape(N_EXPERTS, r, 128)
    out = pl.pallas_call(
        _route_kernel,
        in_specs=[
            pl.BlockSpec(memory_space=pltpu.SMEM),
            pl.BlockSpec((N_EXPERTS, r, 128), lambda: (0, 0, 0)),
        ],
        out_specs=pl.BlockSpec((N_EXPERTS, r, 128), lambda: (0, 0, 0)),
        out_shape=jax.ShapeDtypeStruct((N_EXPERTS, r, 128), F32),
        compiler_params=pltpu.CompilerParams(vmem_limit_bytes=VMEM_LIMIT),
        name="route",
    )(b_router, lg3)
    return out.reshape(N_EXPERTS, n_tok)


def _moe_kernel(h_ref, gt_ref, x_ref, mod_ref, wg_ref, wu_ref, wd_ref, fn_ref, o_ref, acc_ref, *, final):
    e = pl.program_id(2)

    @pl.when(e == 0)
    def _():
        acc_ref[...] = jnp.zeros_like(acc_ref)

    h = h_ref[...]
    a = jnp.dot(h, wg_ref[...], preferred_element_type=F32)
    u = jnp.dot(h, wu_ref[...], preferred_element_type=F32)
    t = ((a * _sigmoid(a)) * u).astype(BF16)
    y = jnp.dot(t, wd_ref[...], preferred_element_type=F32)
    gt = gt_ref[...]
    lane = lax.broadcasted_iota(jnp.int32, gt.shape, 1)
    gcol = jnp.sum(jnp.where(lane == e, gt, 0.0), axis=1, keepdims=True)
    acc_ref[...] += gcol * y

    @pl.when(e == pl.num_programs(2) - 1)
    def _():
        xo = x_ref[...] + mod_ref[5:6, :] * acc_ref[...]
        if final:
            ms = jnp.mean(xo * xo, axis=-1, keepdims=True)
            xo = xo * lax.rsqrt(ms + EPS) * fn_ref[...]
        o_ref[...] = xo


def _moe_call(h2, gates, x, mod_l, mod_row, wg, wu, wd, fn, final):
    b, n, d = x.shape
    tm = min(1024, n)
    nt = n // tm
    if mod_row is None:
        mod_map = lambda bi, i, e: (bi, 0, 0)
    else:
        mod_map = lambda bi, i, e: (mod_row, 0, 0)
    tok = lambda w: pl.BlockSpec((None, tm, w), lambda bi, i, e: (bi, i, 0))
    in_specs = [
        tok(d), tok(N_EXPERTS), tok(d),
        pl.BlockSpec((None, 6, d), mod_map),
        pl.BlockSpec((None, d, D_EXPERT), lambda bi, i, e: (e, 0, 0)),
        pl.BlockSpec((None, d, D_EXPERT), lambda bi, i, e: (e, 0, 0)),
        pl.BlockSpec((None, D_EXPERT, d), lambda bi, i, e: (e, 0, 0)),
        pl.BlockSpec((1, d), lambda bi, i, e: (0, 0)),
    ]
    return pl.pallas_call(
        functools.partial(_moe_kernel, final=final),
        grid=(b, nt, N_EXPERTS),
        in_specs=in_specs,
        out_specs=tok(d),
        out_shape=jax.ShapeDtypeStruct((b, n, d), F32),
        scratch_shapes=[pltpu.VMEM((tm, d), F32)],
        compiler_params=_cparams(("parallel", "parallel", "arbitrary")),
        name="moe_dense",
    )(h2, gates, x, mod_l, wg, wu, wd, fn)


def _permute_w_in(w):
    o = 0
    u = w[:, o:o + 256]; o += 256
    gq = w[:, o:o + 128]; o += 128
    gk = w[:, o:o + 128]; o += 128
    gv = w[:, o:o + 256]; o += 256
    gg = w[:, o:o + 256]; o += 256
    gdf = w[:, o:o + 16]; o += 16
    gdb = w[:, o:o + 16]; o += 16
    aq = w[:, o:o + 512]; o += 512
    ak = w[:, o:o + 128]; o += 128
    av = w[:, o:o + 128]; o += 128
    k0, k1 = ak[:, :64], ak[:, 64:]
    v0, v1 = av[:, :64], av[:, 64:]
    pad = jnp.zeros((w.shape[0], 96), w.dtype)
    return jnp.concatenate([u, gq, gk, gv, gg, aq, k0, k0, k1, k1, v0, v0, v1, v1, gdf, gdb, pad], axis=1).astype(BF16)


def _gate_up_weights(w_up, b_up):
    z = jnp.zeros((GLA_GATE_RANK, GLA_QK), w_up.dtype)
    top = jnp.concatenate([w_up[0], z], axis=1)
    mid = jnp.concatenate([z, w_up[1]], axis=1)
    pad = jnp.zeros((128 - 2 * GLA_GATE_RANK, 2 * GLA_QK), w_up.dtype)
    return jnp.concatenate([top, mid, pad], axis=0).astype(BF16), b_up.reshape(1, 2 * GLA_QK)


def kernel(x, c, ctx, c_ctx, w_ada, b_ada, norm_mix, norm_ffn, w_in, w_gla_gate_up, b_gla_gate, gla_norm, q_norm,
           k_norm, w_out, w_router, b_router, w_exp_gate, w_exp_up, w_exp_down, final_norm):
    b, n, d = x.shape
    m = ctx.shape[1]
    depth = w_ada.shape[0]
    assert d == D_MODEL and n % GLA_PAIR == 0 and m % GLA_PAIR == 0 and n % GRID_W == 0

    rows = ((b + 1 + 7) // 8) * 8
    cv = jnp.concatenate([c, c_ctx[None, :], jnp.zeros((rows - b - 1, d), F32)], axis=0)
    mod = _ada_call(cv, w_ada, b_ada).reshape(depth, rows, 6, d)

    cs = jnp.asarray(_channel_dft_table()).astype(BF16)
    tab_x = jnp.asarray(_seq_dft_table(n)).astype(BF16)
    tab_c = jnp.asarray(_seq_dft_table(m)).astype(BF16)
    rope_tabs = tuple(jnp.asarray(t) for t in _rope_tables(n))
    bd512 = jnp.asarray(_blockdiag_ones(ATT_WIDTH, ATT_HDIM)).astype(BF16)
    bd256 = jnp.asarray(_blockdiag_ones(GLA_WIDTH, GLA_DV)).astype(BF16)
    wr_t = w_router.T
    wrh = wr_t.astype(BF16)
    wrl = (wr_t - wrh.astype(F32)).astype(BF16)
    fn = final_norm.reshape(1, d)

    xc = ctx
    for l in range(depth):
        ctx_out = l < depth - 1
        mod_l = mod[l]
        w_perm = _permute_w_in(w_in[l])
        wup, bup = _gate_up_weights(w_gla_gate_up[l], b_gla_gate[l])
        nw = norm_mix[l].reshape(1, d)
        nf = norm_ffn[l].reshape(1, d)
        qn = jnp.tile(q_norm[l], ATT_HEADS).reshape(1, ATT_WIDTH)
        kn = jnp.tile(k_norm[l], 4).reshape(1, 256)
        gn = jnp.tile(gla_norm[l], GLA_HEADS).reshape(1, GLA_WIDTH)
        wo = w_out[l].astype(BF16)
        wg = w_exp_gate[l].astype(BF16)
        wu = w_exp_up[l].astype(BF16)
        wd = w_exp_down[l].astype(BF16)

        (ab_c, gqk_c, gv_c, gvt_c, gg_c, la_c, q_c, kt_c, vd_c) = _inproj_call(
            xc, mod_l, b, nw, w_perm, cs, wup, bup, qn, kn, bd512, None)
        (ab_x, gqk_x, gv_x, gvt_x, gg_x, la_x, q_x, kt_x, vd_x) = _inproj_call(
            x, mod_l, None, nw, w_perm, cs, wup, bup, qn, kn, bd512, rope_tabs)

        s_zero = jnp.zeros((b, 2, GLA_WIDTH, GLA_QK), F32)
        of_c, ob_c, s_fin = _gla_call(gqk_c, gv_c, gvt_c, la_c, s_zero)
        of_x, ob_x, _ = _gla_call(gqk_x, gv_x, gvt_x, la_x, s_fin)

        f_x = _seqdft_call(tab_x, ab_x.reshape(2 * n, b * FNET_WIDTH), b)
        a_x = _attn_call(q_x, [(kt_c, vd_c), (kt_x, vd_x)])
        x, h2_x, lg_x = _outproj_call(f_x, of_x, ob_x, gg_x, a_x, x, mod_l, None, wo, gn, bd256, nf, wrh, wrl)

        if ctx_out:
            f_c = _seqdft_call(tab_c, ab_c.reshape(2 * m, b * FNET_WIDTH), b)
            a_c = _attn_call(q_c, [(kt_c, vd_c)])
            xc, h2_c, lg_c = _outproj_call(f_c, of_c, ob_c, gg_c, a_c, xc, mod_l, b, wo, gn, bd256, nf, wrh, wrl)

        final = l == depth - 1
        gates_x = _route_call(lg_x, b_router).T.reshape(b, n, N_EXPERTS)
        x = _moe_call(h2_x, gates_x, x, mod_l, None, wg, wu, wd, fn, final)
        if ctx_out:
            gates_c = _route_call(lg_c, b_router).T.reshape(b, m, N_EXPERTS)
            xc = _moe_call(h2_c, gates_c, xc, mod_l, b, wg, wu, wd, fn, False)
    return x
```

```python
import functools

import numpy as np
import jax
import jax.numpy as jnp
from jax import lax
from jax.experimental import pallas as pl
from jax.experimental.pallas import tpu as pltpu
from jax.experimental.pallas import tpu_sc as plsc

F32 = jnp.float32
BF16 = jnp.bfloat16

D_MODEL = 1024
GRID_W = 64
EPS = 1e-6

FNET_WIDTH = 256
FNET_GROUPS = 4
FNET_GDIM = 64

GLA_HEADS = 4
GLA_DV = 64
GLA_DK = 32
GLA_WIDTH = 256
GLA_QK = 128
GLA_GATE_RANK = 16
GLA_GATE_NORM = 16.0
GLA_CHUNK = 64
GLA_PAIR = 2 * GLA_CHUNK

ATT_HEADS = 8
ATT_KV_HEADS = 2
ATT_HDIM = 64
ATT_WIDTH = 512
ROPE_FREQS = 16
ROPE_THETA = 10000.0

N_EXPERTS = 16
N_GROUPS = 4
EXPERTS_PER_GROUP = 4
D_EXPERT = 512

C_U = 0
C_GQ = 256
C_GK = 384
C_GV = 512
C_GG = 768
C_AQ = 1024
C_AK = 1536
C_AV = 1792
C_GD = 2048
W_IN_COLS = 2176

VMEM_LIMIT = 56 * 1024 * 1024


def _cparams(sem):
    return pltpu.CompilerParams(dimension_semantics=sem, vmem_limit_bytes=VMEM_LIMIT)


def _sigmoid(x):
    return 1.0 / (1.0 + jnp.exp(-x))


def _nt_dot(a, b):
    return lax.dot_general(a, b, (((1,), (1,)), ((), ())), preferred_element_type=F32)


@functools.lru_cache(maxsize=None)
def _channel_dft_table():
    j = np.arange(FNET_GDIM)
    ang = 2.0 * np.pi * ((j[:, None] * j[None, :]) % FNET_GDIM) / FNET_GDIM
    c = np.cos(ang) / np.sqrt(FNET_GDIM)
    s = np.sin(ang) / np.sqrt(FNET_GDIM)
    out = np.zeros((FNET_WIDTH, 2 * FNET_WIDTH), np.float64)
    for g in range(FNET_GROUPS):
        sl = slice(g * FNET_GDIM, (g + 1) * FNET_GDIM)
        out[sl, sl] = c
        out[sl, FNET_WIDTH + g * FNET_GDIM:FNET_WIDTH + (g + 1) * FNET_GDIM] = s
    return out.astype(np.float32)


@functools.lru_cache(maxsize=None)
def _seq_dft_table(n):
    j = np.arange(n, dtype=np.int64)
    ang = 2.0 * np.pi * ((j[:, None] * j[None, :]) % n) / n
    return np.concatenate([np.cos(ang), -np.sin(ang)], axis=1).astype(np.float32) / np.float32(np.sqrt(n))


@functools.lru_cache(maxsize=None)
def _rope_tables(n):
    rows = n // GRID_W
    row = np.repeat(np.arange(rows), GRID_W).astype(np.float64)
    col = np.tile(np.arange(GRID_W), rows).astype(np.float64)
    inv = ROPE_THETA ** (-np.arange(ROPE_FREQS, dtype=np.float64) * 2.0 / (2 * ROPE_FREQS))
    ar = row[:, None] * inv[None, :]
    ac = col[:, None] * inv[None, :]
    cos = np.concatenate([np.cos(ar), np.cos(ar), np.cos(ac), np.cos(ac)], axis=1)
    sin = np.concatenate([-np.sin(ar), np.sin(ar), -np.sin(ac), np.sin(ac)], axis=1)
    return (np.tile(cos, (1, 2)).astype(np.float32), np.tile(sin, (1, 2)).astype(np.float32))


@functools.lru_cache(maxsize=None)
def _blockdiag_ones(width, blk):
    i = np.arange(width)
    return (i[:, None] // blk == i[None, :] // blk).astype(np.float32)


def _ada_kernel(cv_ref, w_ref, b_ref, o_ref):
    cv = cv_ref[...]
    a = (cv * _sigmoid(cv)).astype(BF16)
    o_ref[...] = jnp.dot(a, w_ref[...].astype(BF16), preferred_element_type=F32) + b_ref[...]


def _ada_call(cv, w_ada, b_ada):
    depth, d, d6 = w_ada.shape
    tn = 1536
    rows = cv.shape[0]
    return pl.pallas_call(
        _ada_kernel,
        grid=(depth, d6 // tn),
        in_specs=[
            pl.BlockSpec((rows, d), lambda l, j: (0, 0)),
            pl.BlockSpec((None, d, tn), lambda l, j: (l, 0, j)),
            pl.BlockSpec((None, 1, tn), lambda l, j: (l, 0, j)),
        ],
        out_specs=pl.BlockSpec((None, rows, tn), lambda l, j: (l, 0, j)),
        out_shape=jax.ShapeDtypeStruct((depth, rows, d6), F32),
        compiler_params=_cparams(("parallel", "parallel")),
        name="ada_mod",
    )(cv, w_ada, b_ada.reshape(depth, 1, d6))


def _swap16(x):
    lane = lax.broadcasted_iota(jnp.int32, x.shape, 1)
    first = (lane % 32) < 16
    return jnp.where(first, pltpu.roll(x, 112, 1), pltpu.roll(x, 16, 1))


def _head_rms(x, bd, w):
    ms = jnp.dot((x * x).astype(BF16), bd, preferred_element_type=F32) * (1.0 / ATT_HDIM)
    return x * lax.rsqrt(ms + EPS) * w


def _inproj_kernel(*refs, rope):
    if rope:
        (x_ref, mod_ref, nw_ref, w_ref, cs_ref, wup_ref, bup_ref, qn_ref, kn_ref, bd_ref, cos_ref, sin_ref,
         ab_ref, gqk_ref, gv_ref, gvt_ref, gg_ref, la_ref, q_ref, kt_ref, vd_ref) = refs
    else:
        (x_ref, mod_ref, nw_ref, w_ref, cs_ref, wup_ref, bup_ref, qn_ref, kn_ref, bd_ref,
         ab_ref, gqk_ref, gv_ref, gvt_ref, gg_ref, la_ref, q_ref, kt_ref, vd_ref) = refs
    x = x_ref[...]
    ms = jnp.mean(x * x, axis=-1, keepdims=True)
    y = x * lax.rsqrt(ms + EPS) * nw_ref[...]
    h = y * (1.0 + mod_ref[1:2, :]) + mod_ref[0:1, :]
    acc = jnp.dot(h.astype(BF16), w_ref[...], preferred_element_type=F32)

    uab = jnp.dot(acc[:, C_U:C_U + FNET_WIDTH].astype(BF16), cs_ref[...], preferred_element_type=F32)
    ab_ref[0] = uab[:, :FNET_WIDTH].astype(BF16)
    ab_ref[1] = uab[:, FNET_WIDTH:].astype(BF16)

    gqk_ref[...] = acc[:, C_GQ:C_GQ + 2 * GLA_QK]
    gv = acc[:, C_GV:C_GV + GLA_WIDTH]
    gv_ref[...] = gv.astype(BF16)
    gvt_ref[...] = gv.T.astype(BF16)
    gg_ref[...] = acc[:, C_GG:C_GG + GLA_WIDTH].astype(BF16)
    pre = jnp.dot(acc[:, C_GD:C_GD + 128].astype(BF16), wup_ref[...], preferred_element_type=F32) + bup_ref[...]
    la_ref[...] = (jnp.minimum(pre, 0.0) - jnp.log1p(jnp.exp(-jnp.abs(pre)))) * (1.0 / GLA_GATE_NORM)

    bd = bd_ref[...]
    q = _head_rms(acc[:, C_AQ:C_AQ + ATT_WIDTH], bd, qn_ref[...])
    k = _head_rms(acc[:, C_AK:C_AK + 256], bd[:256, :256], kn_ref[...])
    if rope:
        cos = cos_ref[...]
        sin = sin_ref[...]
        q = jnp.concatenate(
            [q[:, s:s + 128] * cos + _swap16(q[:, s:s + 128]) * sin for s in range(0, ATT_WIDTH, 128)], axis=1)
        k = jnp.concatenate(
            [k[:, s:s + 128] * cos + _swap16(k[:, s:s + 128]) * sin for s in range(0, 256, 128)], axis=1)
    q_ref[...] = (q * (ATT_HDIM ** -0.5)).astype(BF16)
    v = acc[:, C_AV:C_AV + 256]
    for g in range(ATT_KV_HEADS):
        kt_ref[g] = k[:, 128 * g:128 * (g + 1)].T.astype(BF16)
        vd_ref[g] = v[:, 128 * g:128 * (g + 1)].astype(BF16)


def _inproj_call(x, mod_l, mod_row, nw, w_perm, cs, wup, bup, qn, kn, bd, rope_tabs):
    b, n, d = x.shape
    tm = min(512, n)
    nt = n // tm
    rope = rope_tabs is not None
    if mod_row is None:
        mod_map = lambda bi, i: (bi, 0, 0)
    else:
        mod_map = lambda bi, i: (mod_row, 0, 0)
    const = lambda bi, i: (0, 0)
    in_specs = [
        pl.BlockSpec((None, tm, d), lambda bi, i: (bi, i, 0)),
        pl.BlockSpec((None, 6, d), mod_map),
        pl.BlockSpec((1, d), const),
        pl.BlockSpec((d, W_IN_COLS), const),
        pl.BlockSpec((FNET_WIDTH, 2 * FNET_WIDTH), const),
        pl.BlockSpec((128, 2 * GLA_QK), const),
        pl.BlockSpec((1, 2 * GLA_QK), const),
        pl.BlockSpec((1, ATT_WIDTH), const),
        pl.BlockSpec((1, 256), const),
        pl.BlockSpec((ATT_WIDTH, ATT_WIDTH), const),
    ]
    args = [x, mod_l, nw, w_perm, cs, wup, bup, qn, kn, bd]
    if rope:
        in_specs += [pl.BlockSpec((tm, 128), lambda bi, i: (i, 0)), pl.BlockSpec((tm, 128), lambda bi, i: (i, 0))]
        args += list(rope_tabs)
    out_shape = (
        jax.ShapeDtypeStruct((2, n, b * FNET_WIDTH), BF16),
        jax.ShapeDtypeStruct((b, n, 2 * GLA_QK), F32),
        jax.ShapeDtypeStruct((b, n, GLA_WIDTH), BF16),
        jax.ShapeDtypeStruct((b, GLA_WIDTH, n), BF16),
        jax.ShapeDtypeStruct((b, n, GLA_WIDTH), BF16),
        jax.ShapeDtypeStruct((b, n, 2 * GLA_QK), F32),
        jax.ShapeDtypeStruct((b, n, ATT_WIDTH), BF16),
        jax.ShapeDtypeStruct((b, ATT_KV_HEADS, 128, n), BF16),
        jax.ShapeDtypeStruct((b, ATT_KV_HEADS, n, 128), BF16),
    )
    out_specs = (
        pl.BlockSpec((2, tm, FNET_WIDTH), lambda bi, i: (0, i, bi)),
        pl.BlockSpec((None, tm, 2 * GLA_QK), lambda bi, i: (bi, i, 0)),
        pl.BlockSpec((None, tm, GLA_WIDTH), lambda bi, i: (bi, i, 0)),
        pl.BlockSpec((None, GLA_WIDTH, tm), lambda bi, i: (bi, 0, i)),
        pl.BlockSpec((None, tm, GLA_WIDTH), lambda bi, i: (bi, i, 0)),
        pl.BlockSpec((None, tm, 2 * GLA_QK), lambda bi, i: (bi, i, 0)),
        pl.BlockSpec((None, tm, ATT_WIDTH), lambda bi, i: (bi, i, 0)),
        pl.BlockSpec((None, ATT_KV_HEADS, 128, tm), lambda bi, i: (bi, 0, 0, i)),
        pl.BlockSpec((None, ATT_KV_HEADS, tm, 128), lambda bi, i: (bi, 0, i, 0)),
    )
    return pl.pallas_call(
        functools.partial(_inproj_kernel, rope=rope),
        grid=(b, nt),
        in_specs=in_specs,
        out_specs=out_specs,
        out_shape=out_shape,
        compiler_params=_cparams(("parallel", "parallel")),
        name="inproj_rope" if rope else "inproj_ctx",
    )(*args)


def _seqdft_kernel(t_ref, ab_ref, o_ref):
    y = jnp.dot(t_ref[...], ab_ref[...], preferred_element_type=F32)
    for bb in range(o_ref.shape[0]):
        o_ref[bb] = y[:, bb * FNET_WIDTH:(bb + 1) * FNET_WIDTH].astype(BF16)


def _seqdft_call(table, ab, b):
    n = table.shape[0]
    tm = min(512, n)
    nb = 4 if b % 4 == 0 else (2 if b % 2 == 0 else 1)
    return pl.pallas_call(
        _seqdft_kernel,
        grid=(b // nb, n // tm),
        in_specs=[
            pl.BlockSpec((tm, 2 * n), lambda c, i: (i, 0)),
            pl.BlockSpec((2 * n, nb * FNET_WIDTH), lambda c, i: (0, c)),
        ],
        out_specs=pl.BlockSpec((nb, tm, FNET_WIDTH), lambda c, i: (c, i, 0)),
        out_shape=jax.ShapeDtypeStruct((b, n, FNET_WIDTH), BF16),
        compiler_params=_cparams(("parallel", "parallel")),
        name="seq_dft",
    )(table, ab)


def _gla_dir(qk, v, vt, a, s_in, fwd):
    p = GLA_PAIR
    r = lax.broadcasted_iota(jnp.int32, (p, p), 0)
    c = lax.broadcasted_iota(jnp.int32, (p, p), 1)
    same = (r // GLA_CHUNK) == (c // GLA_CHUNK)
    tri = same & ((c <= r) if fwd else (c >= r))
    tri_b = jnp.where(tri, 1.0, 0.0).astype(BF16)
    row_lo = r < GLA_CHUNK

    q = qk[:, :GLA_QK] * (GLA_DK ** -0.5)
    k = qk[:, GLA_QK:]
    a1 = a.astype(BF16)
    r1 = a - a1.astype(F32)
    a2 = r1.astype(BF16)
    a3 = (r1 - a2.astype(F32)).astype(BF16)
    cum = (jnp.dot(tri_b, a1, preferred_element_type=F32) + jnp.dot(tri_b, a2, preferred_element_type=F32)
           + jnp.dot(tri_b, a3, preferred_element_type=F32))
    if fwd:
        last0, last1 = cum[GLA_CHUNK - 1:GLA_CHUNK, :], cum[p - 1:p, :]
    else:
        last0, last1 = cum[0:1, :], cum[GLA_CHUNK:GLA_CHUNK + 1, :]
    lastb = jnp.where(row_lo, last0, last1)
    qt = q * jnp.exp(cum)
    kt = k * jnp.exp(-cum)
    kd = k * jnp.exp(lastb - cum)

    lane = c
    qs = jnp.concatenate(
        [jnp.where((lane // GLA_DK) == hh, qt, 0.0) for hh in range(GLA_HEADS)], axis=0).astype(BF16)
    att = _nt_dot(qs, kt.astype(BF16))
    tri4 = jnp.concatenate([tri] * GLA_HEADS, axis=0)
    att = jnp.where(tri4, att, 0.0).astype(BF16)
    of = jnp.dot(att, v, preferred_element_type=F32)
    col = lax.broadcasted_iota(jnp.int32, (p, GLA_WIDTH), 1)
    o_intra = jnp.zeros((p, GLA_WIDTH), F32)
    for hh in range(GLA_HEADS):
        o_intra = o_intra + jnp.where((col // GLA_DV) == hh, of[hh * p:(hh + 1) * p, :], 0.0)

    sr = lax.broadcasted_iota(jnp.int32, (GLA_WIDTH, GLA_QK), 0)
    sc = lax.broadcasted_iota(jnp.int32, (GLA_WIDTH, GLA_QK), 1)
    bdm = (sr // GLA_DV) == (sc // GLA_DK)
    order = (0, 1) if fwd else (1, 0)
    lasts = (last0, last1)
    s_cur = s_in
    o_parts = [None, None]
    for ci in order:
        rows = slice(ci * GLA_CHUNK, (ci + 1) * GLA_CHUNK)
        o_parts[ci] = _nt_dot(qt[rows, :].astype(BF16), s_cur.astype(BF16))
        in_chunk = row_lo if ci == 0 else jnp.logical_not(row_lo)
        kvt = jnp.dot(vt, jnp.where(in_chunk, kd, 0.0).astype(BF16), preferred_element_type=F32)
        s_cur = s_cur * jnp.exp(lasts[ci]) + jnp.where(bdm, kvt, 0.0)
    o_inter = jnp.concatenate(o_parts, axis=0)
    return o_intra + o_inter, s_cur


def _gla_kernel(qkf, vf, vtf, laf, qkb, vb, vtb, lab, s0_ref, of_ref, ob_ref, sfin_ref, s_scr):
    i = pl.program_id(1)

    @pl.when(i == 0)
    def _():
        s_scr[...] = s0_ref[...]

    o1, sf = _gla_dir(qkf[...], vf[...], vtf[...], laf[...], s_scr[0], True)
    o2, sb = _gla_dir(qkb[...], vb[...], vtb[...], lab[...], s_scr[1], False)
    of_ref[...] = o1
    ob_ref[...] = o2
    s_scr[0] = sf
    s_scr[1] = sb

    @pl.when(i == pl.num_programs(1) - 1)
    def _():
        sfin_ref[0] = sf
        sfin_ref[1] = sb


def _gla_call(gqk, gv, gvt, la, s0):
    b, n, _ = gqk.shape
    p = GLA_PAIR
    npair = n // p
    fw = lambda bi, i: (bi, i, 0)
    bw = lambda bi, i: (bi, npair - 1 - i, 0)
    in_specs = [
        pl.BlockSpec((None, p, 2 * GLA_QK), fw),
        pl.BlockSpec((None, p, GLA_WIDTH), fw),
        pl.BlockSpec((None, GLA_WIDTH, p), lambda bi, i: (bi, 0, i)),
        pl.BlockSpec((None, p, GLA_QK), fw),
        pl.BlockSpec((None, p, 2 * GLA_QK), bw),
        pl.BlockSpec((None, p, GLA_WIDTH), bw),
        pl.BlockSpec((None, GLA_WIDTH, p), lambda bi, i: (bi, 0, npair - 1 - i)),
        pl.BlockSpec((None, p, GLA_QK), lambda bi, i: (bi, npair - 1 - i, 1)),
        pl.BlockSpec((None, 2, GLA_WIDTH, GLA_QK), lambda bi, i: (bi, 0, 0, 0)),
    ]
    out_specs = (
        pl.BlockSpec((None, p, GLA_WIDTH), fw),
        pl.BlockSpec((None, p, GLA_WIDTH), bw),
        pl.BlockSpec((None, 2, GLA_WIDTH, GLA_QK), lambda bi, i: (bi, 0, 0, 0)),
    )
    out_shape = (
        jax.ShapeDtypeStruct((b, n, GLA_WIDTH), F32),
        jax.ShapeDtypeStruct((b, n, GLA_WIDTH), F32),
        jax.ShapeDtypeStruct((b, 2, GLA_WIDTH, GLA_QK), F32),
    )
    return pl.pallas_call(
        _gla_kernel,
        grid=(b, npair),
        in_specs=in_specs,
        out_specs=out_specs,
        out_shape=out_shape,
        scratch_shapes=[pltpu.VMEM((2, GLA_WIDTH, GLA_QK), F32)],
        compiler_params=_cparams(("parallel", "arbitrary")),
        name="gla_scan",
    )(gqk, gv, gvt, la, gqk, gv, gvt, la, s0)


def _attn_kernel(*refs, nparts):
    q_ref = refs[0]
    parts = [(refs[1 + 2 * i], refs[2 + 2 * i]) for i in range(nparts)]
    o_ref = refs[1 + 2 * nparts]
    tq = q_ref.shape[0]
    lane = lax.broadcasted_iota(jnp.int32, (tq, 128), 1)
    lo = lane < ATT_HDIM
    for j in range(ATT_HEADS // 2):
        q128 = q_ref[:, 128 * j:128 * (j + 1)]
        g = (2 * j) // (ATT_HEADS // ATT_KV_HEADS)
        outs = []
        for half in range(2):
            qm = jnp.where(lo if half == 0 else jnp.logical_not(lo), q128, jnp.zeros_like(q128))
            ss = [jnp.dot(qm, kt_ref[g], preferred_element_type=F32) for kt_ref, _ in parts]
            m = functools.reduce(jnp.maximum, [jnp.max(s, axis=-1, keepdims=True) for s in ss])
            ps = [jnp.exp(s - m) for s in ss]
            den = functools.reduce(lambda u, w: u + w, [jnp.sum(pp, axis=-1, keepdims=True) for pp in ps])
            pv = functools.reduce(
                lambda u, w: u + w,
                [jnp.dot(pp.astype(BF16), vd_ref[g], preferred_element_type=F32) for pp, (_, vd_ref) in zip(ps, parts)])
            outs.append(pv / den)
        o_ref[:, 128 * j:128 * (j + 1)] = jnp.where(lo, outs[0], outs[1]).astype(BF16)


def _attn_call(q, kv_parts):
    b, n, _ = q.shape
    tq = min(256, n)
    in_specs = [pl.BlockSpec((None, tq, ATT_WIDTH), lambda bi, i: (bi, i, 0))]
    args = [q]
    for kt, vd in kv_parts:
        m = kt.shape[-1]
        in_specs.append(pl.BlockSpec((None, ATT_KV_HEADS, 128, m), lambda bi, i: (bi, 0, 0, 0)))
        in_specs.append(pl.BlockSpec((None, ATT_KV_HEADS, m, 128), lambda bi, i: (bi, 0, 0, 0)))
        args += [kt, vd]
    return pl.pallas_call(
        functools.partial(_attn_kernel, nparts=len(kv_parts)),
        grid=(b, n // tq),
        in_specs=in_specs,
        out_specs=pl.BlockSpec((None, tq, ATT_WIDTH), lambda bi, i: (bi, i, 0)),
        out_shape=jax.ShapeDtypeStruct((b, n, ATT_WIDTH), BF16),
        compiler_params=_cparams(("parallel", "parallel")),
        name="gqa_attn",
    )(*args)


def _outproj_kernel(f_ref, of_ref, ob_ref, gg_ref, a_ref, x_ref, mod_ref, w_ref, gn_ref, bd_ref, nf_ref,
                    wrh_ref, wrl_ref, xn_ref, h2_ref, lg_ref):
    o = of_ref[...] + ob_ref[...]
    ms = jnp.dot((o * o).astype(BF16), bd_ref[...], preferred_element_type=F32) * (1.0 / GLA_DV)
    on = o * lax.rsqrt(ms + EPS) * gn_ref[...]
    g = gg_ref[...].astype(F32)
    gl = (on * (g * _sigmoid(g))).astype(BF16)
    ox = (jnp.dot(f_ref[...], w_ref[0:FNET_WIDTH, :], preferred_element_type=F32)
          + jnp.dot(gl, w_ref[FNET_WIDTH:FNET_WIDTH + GLA_WIDTH, :], preferred_element_type=F32)
          + jnp.dot(a_ref[...], w_ref[FNET_WIDTH + GLA_WIDTH:, :], preferred_element_type=F32))
    xn = x_ref[...] + mod_ref[2:3, :] * ox
    xn_ref[...] = xn
    ms2 = jnp.mean(xn * xn, axis=-1, keepdims=True)
    h2 = xn * lax.rsqrt(ms2 + EPS) * nf_ref[...] * (1.0 + mod_ref[4:5, :]) + mod_ref[3:4, :]
    hh = h2.astype(BF16)
    hl = (h2 - hh.astype(F32)).astype(BF16)
    h2_ref[...] = h2
    lg_ref[...] = _nt_dot(wrh_ref[...], hh) + _nt_dot(wrh_ref[...], hl) + _nt_dot(wrl_ref[...], hh)


def _outproj_call(f, of, ob, gg, a, x, mod_l, mod_row, w_out, gn, bd, nf, wrh, wrl):
    b, n, d = x.shape
    tm = min(512, n)
    nt = n // tm
    if mod_row is None:
        mod_map = lambda bi, i: (bi, 0, 0)
    else:
        mod_map = lambda bi, i: (mod_row, 0, 0)
    const = lambda bi, i: (0, 0)
    tok = lambda w: pl.BlockSpec((None, tm, w), lambda bi, i: (bi, i, 0))
    in_specs = [
        tok(FNET_WIDTH), tok(GLA_WIDTH), tok(GLA_WIDTH), tok(GLA_WIDTH), tok(ATT_WIDTH), tok(d),
        pl.BlockSpec((None, 6, d), mod_map),
        pl.BlockSpec((d, d), const),
        pl.BlockSpec((1, GLA_WIDTH), const),
        pl.BlockSpec((GLA_WIDTH, GLA_WIDTH), const),
        pl.BlockSpec((1, d), const),
        pl.BlockSpec((N_EXPERTS, d), const),
        pl.BlockSpec((N_EXPERTS, d), const),
    ]
    out_specs = (
        tok(d), tok(d),
        pl.BlockSpec((N_EXPERTS, tm), lambda bi, i: (0, bi * nt + i)),
    )
    out_shape = (
        jax.ShapeDtypeStruct((b, n, d), F32),
        jax.ShapeDtypeStruct((b, n, d), F32),
        jax.ShapeDtypeStruct((N_EXPERTS, b * n), F32),
    )
    return pl.pallas_call(
        _outproj_kernel,
        grid=(b, nt),
        in_specs=in_specs,
        out_specs=out_specs,
        out_shape=out_shape,
        compiler_params=_cparams(("parallel", "parallel")),
        name="outproj_router",
    )(f, of, ob, gg, a, x, mod_l, w_out, gn, bd, nf, wrh, wrl)


def _route_kernel(b_ref, lg_ref, pos_ref, wt_ref, te_ref, nv_ref, *, tm):
    r = lg_ref.shape[1]
    s = [_sigmoid(lg_ref[e]) for e in range(N_EXPERTS)]
    sel = [s[e] + b_ref[e] for e in range(N_EXPERTS)]
    grp = []
    for g in range(N_GROUPS):
        a, b, c, d = sel[4 * g:4 * g + 4]
        hi1, lo1 = jnp.maximum(a, b), jnp.minimum(a, b)
        hi2, lo2 = jnp.maximum(c, d), jnp.minimum(c, d)
        m1 = jnp.maximum(hi1, hi2)
        m2 = jnp.maximum(jnp.minimum(hi1, hi2), jnp.maximum(lo1, lo2))
        grp.append(m1 + m2)
    one = jnp.ones_like(s[0])
    zero = jnp.zeros_like(s[0])
    msk = []
    for g in range(N_GROUPS):
        isg = one
        for g2 in range(N_GROUPS):
            if g2 < g:
                isg = isg * jnp.where(grp[g] > grp[g2], one, zero)
            elif g2 > g:
                isg = isg * jnp.where(grp[g] >= grp[g2], one, zero)
        for li in range(EXPERTS_PER_GROUP):
            e = 4 * g + li
            rank = zero
            for lj in range(EXPERTS_PER_GROUP):
                ej = 4 * g + lj
                if lj < li:
                    rank = rank + jnp.where(sel[ej] >= sel[e], one, zero)
                elif lj > li:
                    rank = rank + jnp.where(sel[ej] > sel[e], one, zero)
            msk.append(jnp.where(rank < 2.0, isg, zero))
    den = functools.reduce(lambda u, v: u + v, [msk[e] * s[e] for e in range(N_EXPERTS)])

    li_ = lax.broadcasted_iota(jnp.int32, (128, 128), 0)
    lj_ = lax.broadcasted_iota(jnp.int32, (128, 128), 1)
    upper = jnp.where(li_ < lj_, 1.0, 0.0).astype(BF16)
    ri_ = lax.broadcasted_iota(jnp.int32, (r, r), 0)
    rj_ = lax.broadcasted_iota(jnp.int32, (r, r), 1)
    lower = jnp.where(rj_ < ri_, 1.0, 0.0).astype(BF16)
    tile_start = lax.broadcasted_iota(jnp.int32, te_ref.shape, 1).astype(F32) * float(tm)
    te = jnp.zeros(te_ref.shape, F32)
    off = jnp.zeros((1, 1), F32)
    seen = zero
    pos = [zero, zero]
    wts = [zero, zero]
    for e in range(N_EXPERTS):
        mb = msk[e].astype(BF16)
        lane_pre = jnp.dot(mb, upper, preferred_element_type=F32)
        row_pre = jnp.sum(jnp.dot(lower, mb, preferred_element_type=F32), axis=1, keepdims=True)
        cnt = jnp.sum(jnp.sum(msk[e], axis=1, keepdims=True), axis=0, keepdims=True)
        p_e = off + row_pre + lane_pre
        g_e = s[e] / den
        for kk in range(2):
            hit = msk[e] * jnp.where(seen == float(kk), one, zero)
            pos[kk] = pos[kk] + hit * p_e
            wts[kk] = wts[kk] + hit * g_e
        seen = seen + msk[e]
        off = off + jnp.floor((cnt + float(tm - 1)) * (1.0 / tm)) * float(tm)
        te = te + jnp.where(tile_start >= off, 1.0, 0.0)
    for kk in range(2):
        pos_ref[kk] = pos[kk].astype(jnp.int32)
        wt_ref[kk] = wts[kk]
    te_ref[...] = jnp.minimum(te, float(N_EXPERTS - 1)).astype(jnp.int32)
    nv_ref[...] = jnp.broadcast_to(off * (1.0 / tm), nv_ref.shape).astype(jnp.int32)


def _route_call(lgt, b_router, tm):
    n_tok = lgt.shape[1]
    r = n_tok // 128
    assert r * 128 == n_tok and r % 8 == 0 and 2 * n_tok // tm + N_EXPERTS <= 256
    lg3 = lgt.reshape(N_EXPERTS, r, 128)
    full3 = lambda k: pl.BlockSpec((k, r, 128), lambda: (0, 0, 0))
    pos, wts, te, nv = pl.pallas_call(
        functools.partial(_route_kernel, tm=tm),
        in_specs=[pl.BlockSpec(memory_space=pltpu.SMEM), full3(N_EXPERTS)],
        out_specs=(full3(2), full3(2), pl.BlockSpec((1, 256), lambda: (0, 0)), pl.BlockSpec((1, 128), lambda: (0, 0))),
        out_shape=(
            jax.ShapeDtypeStruct((2, r, 128), jnp.int32),
            jax.ShapeDtypeStruct((2, r, 128), F32),
            jax.ShapeDtypeStruct((1, 256), jnp.int32),
            jax.ShapeDtypeStruct((1, 128), jnp.int32),
        ),
        compiler_params=pltpu.CompilerParams(vmem_limit_bytes=VMEM_LIMIT),
        name="route",
    )(b_router, lg3)
    return pos.reshape(2, n_tok), wts.reshape(2, n_tok), te.reshape(256), nv[0, :1]


SC_CORES = 2
SC_SUBCORES = 16
SC_WORKERS = SC_CORES * SC_SUBCORES
SC_CHUNK = 32


def _sc_mesh():
    return plsc.VectorSubcoreMesh(core_axis_name="c", subcore_axis_name="s",
                                  num_cores=SC_CORES, num_subcores=SC_SUBCORES)


def _sc_steps(n_rows):
    per_w = n_rows // SC_WORKERS
    steps = per_w // SC_CHUNK
    assert per_w * SC_WORKERS == n_rows and steps * SC_CHUNK == per_w and steps % 2 == 0, n_rows
    return per_w, steps


def _sc_gather_rows(table, idx):
    p = idx.shape[0]
    d = table.shape[1]
    per_w, steps = _sc_steps(p)
    idx3 = idx.reshape(SC_WORKERS, steps, SC_CHUNK)

    @functools.partial(
        pl.kernel, mesh=_sc_mesh(),
        out_type=jax.ShapeDtypeStruct((p, d), table.dtype),
        scratch_types=[
            pltpu.VMEM((steps, SC_CHUNK), jnp.int32),
            pltpu.VMEM((SC_CHUNK, d), table.dtype),
            pltpu.VMEM((SC_CHUNK, d), table.dtype),
            pltpu.SemaphoreType.DMA, pltpu.SemaphoreType.DMA,
            pltpu.SemaphoreType.DMA, pltpu.SemaphoreType.DMA,
        ],
        name="sc_gather_rows",
    )
    def k(table_hbm, idx_hbm, out_hbm, idx_v, buf0, buf1, g0, g1, w0, w1):
        wid = lax.axis_index("s") * SC_CORES + lax.axis_index("c")
        base = wid * per_w
        pltpu.sync_copy(idx_hbm.at[wid], idx_v)

        def gather(s, buf, sem):
            return pltpu.make_async_copy(table_hbm.at[idx_v.at[s]], buf, sem)

        def write(s, buf, sem):
            return pltpu.make_async_copy(buf, out_hbm.at[pl.ds(base + s * SC_CHUNK, SC_CHUNK)], sem)

        gather(0, buf0, g0).start()

        @pl.loop(0, steps, step=2)
        def _(s):
            gather(s + 1, buf1, g1).start()
            gather(s, buf0, g0).wait()
            write(s, buf0, w0).start()
            write(s, buf0, w0).wait()

            @pl.when(s + 2 < steps)
            def _():
                gather(s + 2, buf0, g0).start()

            gather(s + 1, buf1, g1).wait()
            write(s + 1, buf1, w1).start()
            write(s + 1, buf1, w1).wait()

    return k(table, idx3)


def _sc_dispatch(srcs, poss, p_rows):
    d = srcs[0].shape[1]
    plans = [_sc_steps(src.shape[0]) for src in srcs]
    idxs = [pos.reshape(2, SC_WORKERS, st, SC_CHUNK) for pos, (_, st) in zip(poss, plans)]
    nseg = len(srcs)
    scratch = [pltpu.VMEM((2, st, SC_CHUNK), jnp.int32) for _, st in plans]
    scratch += [pltpu.VMEM((SC_CHUNK, d), F32), pltpu.VMEM((SC_CHUNK, d), F32)]
    scratch += [pltpu.SemaphoreType.DMA] * 6

    @functools.partial(
        pl.kernel, mesh=_sc_mesh(),
        out_type=jax.ShapeDtypeStruct((p_rows, d), F32),
        scratch_types=scratch,
        name="sc_dispatch",
    )
    def k(*refs):
        src_hbm = refs[:nseg]
        idx_hbm = refs[nseg:2 * nseg]
        out_hbm = refs[2 * nseg]
        idx_v = refs[2 * nseg + 1:3 * nseg + 1]
        buf0, buf1, r0, r1, a0, a1, b0, b1 = refs[3 * nseg + 1:]
        wid = lax.axis_index("s") * SC_CORES + lax.axis_index("c")
        for seg in range(nseg):
            per_w, steps = plans[seg]
            base = wid * per_w
            for kk in range(2):
                pltpu.sync_copy(idx_hbm[seg].at[kk, wid], idx_v[seg].at[kk])

            def read(s, buf, sem, seg=seg, base=base):
                return pltpu.make_async_copy(src_hbm[seg].at[pl.ds(base + s * SC_CHUNK, SC_CHUNK)], buf, sem)

            def scat(kk, s, buf, sem, seg=seg):
                return pltpu.make_async_copy(buf, out_hbm.at[idx_v[seg].at[kk, s]], sem)

            read(0, buf0, r0).start()

            @pl.loop(0, steps, step=2)
            def _(s, read=read, scat=scat, steps=steps):
                read(s + 1, buf1, r1).start()
                read(s, buf0, r0).wait()
                scat(0, s, buf0, a0).start()
                scat(1, s, buf0, b0).start()
                scat(0, s, buf0, a0).wait()
                scat(1, s, buf0, b0).wait()

                @pl.when(s + 2 < steps)
                def _():
                    read(s + 2, buf0, r0).start()

                read(s + 1, buf1, r1).wait()
                scat(0, s + 1, buf1, a1).start()
                scat(1, s + 1, buf1, b1).start()
                scat(0, s + 1, buf1, a1).wait()
                scat(1, s + 1, buf1, b1).wait()

    return k(*srcs, *idxs)


MOE_TM = 256


def _experts_kernel(te_ref, nv_ref, xs_ref, wg_ref, wu_ref, wd_ref, ys_ref, wgb, wub, wdb):
    i = pl.program_id(0)
    valid = i < nv_ref[0]
    changed = jnp.logical_or(i == 0, te_ref[i] != te_ref[jnp.maximum(i - 1, 0)])

    @pl.when(jnp.logical_and(valid, changed))
    def _():
        wgb[...] = wg_ref[...].astype(BF16)
        wub[...] = wu_ref[...].astype(BF16)
        wdb[...] = wd_ref[...].astype(BF16)

    @pl.when(valid)
    def _():
        h = xs_ref[...].astype(BF16)
        a = jnp.dot(h, wgb[...], preferred_element_type=F32)
        u = jnp.dot(h, wub[...], preferred_element_type=F32)
        t = ((a * _sigmoid(a)) * u).astype(BF16)
        ys_ref[...] = jnp.dot(t, wdb[...], preferred_element_type=F32)


def _experts_call(xs, te, nv, wg, wu, wd):
    p_rows, d = xs.shape
    tm = MOE_TM
    nt = p_rows // tm
    row = lambda i, te_r, nv_r: (jnp.minimum(i, nv_r[0] - 1), 0)
    wsel = lambda i, te_r, nv_r: (te_r[jnp.minimum(i, nv_r[0] - 1)], 0, 0)
    grid_spec = pltpu.PrefetchScalarGridSpec(
        num_scalar_prefetch=2,
        grid=(nt,),
        in_specs=[
            pl.BlockSpec((tm, d), row),
            pl.BlockSpec((None, d, D_EXPERT), wsel),
            pl.BlockSpec((None, d, D_EXPERT), wsel),
            pl.BlockSpec((None, D_EXPERT, d), wsel),
        ],
        out_specs=pl.BlockSpec((tm, d), row),
        scratch_shapes=[
            pltpu.VMEM((d, D_EXPERT), BF16),
            pltpu.VMEM((d, D_EXPERT), BF16),
            pltpu.VMEM((D_EXPERT, d), BF16),
        ],
    )
    return pl.pallas_call(
        _experts_kernel,
        grid_spec=grid_spec,
        out_shape=jax.ShapeDtypeStruct((p_rows, d), F32),
        compiler_params=_cparams(("arbitrary",)),
        name="moe_experts",
    )(te, nv, xs, wg, wu, wd)


def _combine_kernel(y_ref, wt_ref, x_ref, mod_ref, fn_ref, o_ref, *, final):
    wt = wt_ref[...]
    y = wt[:, 0:1] * y_ref[0] + wt[:, 1:2] * y_ref[1]
    xo = x_ref[...] + mod_ref[5:6, :] * y
    if final:
        ms = jnp.mean(xo * xo, axis=-1, keepdims=True)
        xo = xo * lax.rsqrt(ms + EPS) * fn_ref[...]
    o_ref[...] = xo


def _combine_call(y2, wts, x, mod_l, mod_row, fn, final):
    b, n, d = x.shape
    tm = min(512, n)
    if mod_row is None:
        mod_map = lambda bi, i: (bi, 0, 0)
    else:
        mod_map = lambda bi, i: (mod_row, 0, 0)
    tok = lambda w: pl.BlockSpec((None, tm, w), lambda bi, i: (bi, i, 0))
    return pl.pallas_call(
        functools.partial(_combine_kernel, final=final),
        grid=(b, n // tm),
        in_specs=[
            pl.BlockSpec((2, None, tm, d), lambda bi, i: (0, bi, i, 0)),
            tok(2), tok(d),
            pl.BlockSpec((None, 6, d), mod_map),
            pl.BlockSpec((1, d), lambda bi, i: (0, 0)),
        ],
        out_specs=tok(d),
        out_shape=jax.ShapeDtypeStruct((b, n, d), F32),
        compiler_params=_cparams(("parallel", "parallel")),
        name="moe_combine",
    )(y2, wts, x, mod_l, fn)


def _moe_sparse(h_list, lg_list, x_list, mod_l, mod_rows, b_router, wg, wu, wd, fn, final):
    d = h_list[0].shape[-1]
    sizes = [h.shape[0] * h.shape[1] for h in h_list]
    n_tok = sum(sizes)
    lgt = lg_list[0] if len(lg_list) == 1 else jnp.concatenate(lg_list, axis=1)
    pos, wts, te, nv = _route_call(lgt, b_router, MOE_TM)
    p_rows = 2 * n_tok + N_EXPERTS * MOE_TM
    offs = np.cumsum([0] + sizes)
    poss = [pos[:, offs[i]:offs[i + 1]] for i in range(len(sizes))]
    xs = _sc_dispatch([h.reshape(-1, d) for h in h_list], poss, p_rows)
    ys = _experts_call(xs, te, nv, wg, wu, wd)
    outs = []
    for i, (h, x) in enumerate(zip(h_list, x_list)):
        bb, nn, _ = x.shape
        y2 = _sc_gather_rows(ys, poss[i].reshape(-1)).reshape(2, bb, nn, d)
        w_i = wts[:, offs[i]:offs[i + 1]].T.reshape(bb, nn, 2)
        outs.append(_combine_call(y2, w_i, x, mod_l, mod_rows[i], fn, final))
    return outs


def _permute_w_in(w):
    o = 0
    u = w[:, o:o + 256]; o += 256
    gq = w[:, o:o + 128]; o += 128
    gk = w[:, o:o + 128]; o += 128
    gv = w[:, o:o + 256]; o += 256
    gg = w[:, o:o + 256]; o += 256
    gdf = w[:, o:o + 16]; o += 16
    gdb = w[:, o:o + 16]; o += 16
    aq = w[:, o:o + 512]; o += 512
    ak = w[:, o:o + 128]; o += 128
    av = w[:, o:o + 128]; o += 128
    k0, k1 = ak[:, :64], ak[:, 64:]
    v0, v1 = av[:, :64], av[:, 64:]
    pad = jnp.zeros((w.shape[0], 96), w.dtype)
    return jnp.concatenate([u, gq, gk, gv, gg, aq, k0, k0, k1, k1, v0, v0, v1, v1, gdf, gdb, pad], axis=1).astype(BF16)


def _gate_up_weights(w_up, b_up):
    z = jnp.zeros((GLA_GATE_RANK, GLA_QK), w_up.dtype)
    top = jnp.concatenate([w_up[0], z], axis=1)
    mid = jnp.concatenate([z, w_up[1]], axis=1)
    pad = jnp.zeros((128 - 2 * GLA_GATE_RANK, 2 * GLA_QK), w_up.dtype)
    return jnp.concatenate([top, mid, pad], axis=0).astype(BF16), b_up.reshape(1, 2 * GLA_QK)


def kernel(x, c, ctx, c_ctx, w_ada, b_ada, norm_mix, norm_ffn, w_in, w_gla_gate_up, b_gla_gate, gla_norm, q_norm,
           k_norm, w_out, w_router, b_router, w_exp_gate, w_exp_up, w_exp_down, final_norm):
    b, n, d = x.shape
    m = ctx.shape[1]
    depth = w_ada.shape[0]
    assert d == D_MODEL and n % GLA_PAIR == 0 and m % GLA_PAIR == 0 and n % GRID_W == 0

    rows = ((b + 1 + 7) // 8) * 8
    cv = jnp.concatenate([c, c_ctx[None, :], jnp.zeros((rows - b - 1, d), F32)], axis=0)
    mod = _ada_call(cv, w_ada, b_ada).reshape(depth, rows, 6, d)

    cs = jnp.asarray(_channel_dft_table()).astype(BF16)
    tab_x = jnp.asarray(_seq_dft_table(n)).astype(BF16)
    tab_c = jnp.asarray(_seq_dft_table(m)).astype(BF16)
    rope_tabs = tuple(jnp.asarray(t) for t in _rope_tables(n))
    bd512 = jnp.asarray(_blockdiag_ones(ATT_WIDTH, ATT_HDIM)).astype(BF16)
    bd256 = jnp.asarray(_blockdiag_ones(GLA_WIDTH, GLA_DV)).astype(BF16)
    wr_t = w_router.T
    wrh = wr_t.astype(BF16)
    wrl = (wr_t - wrh.astype(F32)).astype(BF16)
    fn = final_norm.reshape(1, d)

    xc = ctx
    for l in range(depth):
        ctx_out = l < depth - 1
        mod_l = mod[l]
        w_perm = _permute_w_in(w_in[l])
        wup, bup = _gate_up_weights(w_gla_gate_up[l], b_gla_gate[l])
        nw = norm_mix[l].reshape(1, d)
        nf = norm_ffn[l].reshape(1, d)
        qn = jnp.tile(q_norm[l], ATT_HEADS).reshape(1, ATT_WIDTH)
        kn = jnp.tile(k_norm[l], 4).reshape(1, 256)
        gn = jnp.tile(gla_norm[l], GLA_HEADS).reshape(1, GLA_WIDTH)
        wo = w_out[l].astype(BF16)

        (ab_c, gqk_c, gv_c, gvt_c, gg_c, la_c, q_c, kt_c, vd_c) = _inproj_call(
            xc, mod_l, b, nw, w_perm, cs, wup, bup, qn, kn, bd512, None)
        (ab_x, gqk_x, gv_x, gvt_x, gg_x, la_x, q_x, kt_x, vd_x) = _inproj_call(
            x, mod_l, None, nw, w_perm, cs, wup, bup, qn, kn, bd512, rope_tabs)

        s_zero = jnp.zeros((b, 2, GLA_WIDTH, GLA_QK), F32)
        of_c, ob_c, s_fin = _gla_call(gqk_c, gv_c, gvt_c, la_c, s_zero)
        of_x, ob_x, _ = _gla_call(gqk_x, gv_x, gvt_x, la_x, s_fin)

        f_x = _seqdft_call(tab_x, ab_x.reshape(2 * n, b * FNET_WIDTH), b)
        a_x = _attn_call(q_x, [(kt_c, vd_c), (kt_x, vd_x)])
        x, h2_x, lg_x = _outproj_call(f_x, of_x, ob_x, gg_x, a_x, x, mod_l, None, wo, gn, bd256, nf, wrh, wrl)

        if ctx_out:
            f_c = _seqdft_call(tab_c, ab_c.reshape(2 * m, b * FNET_WIDTH), b)
            a_c = _attn_call(q_c, [(kt_c, vd_c)])
            xc, h2_c, lg_c = _outproj_call(f_c, of_c, ob_c, gg_c, a_c, xc, mod_l, b, wo, gn, bd256, nf, wrh, wrl)

        final = l == depth - 1
        wg, wu, wd = w_exp_gate[l], w_exp_up[l], w_exp_down[l]
        if ctx_out:
            x, xc = _moe_sparse([h2_x, h2_c], [lg_x, lg_c], [x, xc], mod_l, [None, b], b_router, wg, wu, wd, fn, final)
        else:
            (x,) = _moe_sparse([h2_x], [lg_x], [x], mod_l, [None], b_router, wg, wu, wd, fn, final)
    return x
```

```python
import functools

import numpy as np
import jax
import jax.numpy as jnp
from jax import lax
from jax.experimental import pallas as pl
from jax.experimental.pallas import tpu as pltpu
from jax.experimental.pallas import tpu_sc as plsc

F32 = jnp.float32
BF16 = jnp.bfloat16

D_MODEL = 1024
GRID_W = 64
EPS = 1e-6

FNET_WIDTH = 256
FNET_GROUPS = 4
FNET_GDIM = 64

GLA_HEADS = 4
GLA_DV = 64
GLA_DK = 32
GLA_WIDTH = 256
GLA_QK = 128
GLA_GATE_RANK = 16
GLA_GATE_NORM = 16.0
GLA_CHUNK = 64
GLA_PAIR = 2 * GLA_CHUNK

ATT_HEADS = 8
ATT_KV_HEADS = 2
ATT_HDIM = 64
ATT_WIDTH = 512
ROPE_FREQS = 16
ROPE_THETA = 10000.0

N_EXPERTS = 16
N_GROUPS = 4
EXPERTS_PER_GROUP = 4
D_EXPERT = 512

C_U = 0
C_GQ = 256
C_GK = 384
C_GV = 512
C_GG = 768
C_AQ = 1024
C_AK = 1536
C_AV = 1792
C_GD = 2048
W_IN_COLS = 2176

VMEM_LIMIT = 56 * 1024 * 1024


def _cparams(sem):
    return pltpu.CompilerParams(dimension_semantics=sem, vmem_limit_bytes=VMEM_LIMIT)


def _sigmoid(x):
    return 1.0 / (1.0 + jnp.exp(-x))


def _nt_dot(a, b):
    return lax.dot_general(a, b, (((1,), (1,)), ((), ())), preferred_element_type=F32)


@functools.lru_cache(maxsize=None)
def _channel_dft_table():
    j = np.arange(FNET_GDIM)
    ang = 2.0 * np.pi * ((j[:, None] * j[None, :]) % FNET_GDIM) / FNET_GDIM
    c = np.cos(ang) / np.sqrt(FNET_GDIM)
    s = np.sin(ang) / np.sqrt(FNET_GDIM)
    out = np.zeros((FNET_WIDTH, 2 * FNET_WIDTH), np.float64)
    for g in range(FNET_GROUPS):
        sl = slice(g * FNET_GDIM, (g + 1) * FNET_GDIM)
        out[sl, sl] = c
        out[sl, FNET_WIDTH + g * FNET_GDIM:FNET_WIDTH + (g + 1) * FNET_GDIM] = s
    return out.astype(np.float32)


@functools.lru_cache(maxsize=None)
def _seq_dft_table(n):
    j = np.arange(n, dtype=np.int64)
    ang = 2.0 * np.pi * ((j[:, None] * j[None, :]) % n) / n
    return np.concatenate([np.cos(ang), -np.sin(ang)], axis=1).astype(np.float32) / np.float32(np.sqrt(n))


@functools.lru_cache(maxsize=None)
def _rope_tables(n):
    rows = n // GRID_W
    row = np.repeat(np.arange(rows), GRID_W).astype(np.float64)
    col = np.tile(np.arange(GRID_W), rows).astype(np.float64)
    inv = ROPE_THETA ** (-np.arange(ROPE_FREQS, dtype=np.float64) * 2.0 / (2 * ROPE_FREQS))
    ar = row[:, None] * inv[None, :]
    ac = col[:, None] * inv[None, :]
    cos = np.concatenate([np.cos(ar), np.cos(ar), np.cos(ac), np.cos(ac)], axis=1)
    sin = np.concatenate([-np.sin(ar), np.sin(ar), -np.sin(ac), np.sin(ac)], axis=1)
    return (np.tile(cos, (1, 2)).astype(np.float32), np.tile(sin, (1, 2)).astype(np.float32))


@functools.lru_cache(maxsize=None)
def _blockdiag_ones(width, blk):
    i = np.arange(width)
    return (i[:, None] // blk == i[None, :] // blk).astype(np.float32)


def _ada_kernel(cv_ref, w_ref, b_ref, o_ref):
    cv = cv_ref[...]
    a = (cv * _sigmoid(cv)).astype(BF16)
    o_ref[...] = jnp.dot(a, w_ref[...].astype(BF16), preferred_element_type=F32) + b_ref[...]


def _ada_call(cv, w_ada, b_ada):
    depth, d, d6 = w_ada.shape
    tn = 1536
    rows = cv.shape[0]
    return pl.pallas_call(
        _ada_kernel,
        grid=(depth, d6 // tn),
        in_specs=[
            pl.BlockSpec((rows, d), lambda l, j: (0, 0)),
            pl.BlockSpec((None, d, tn), lambda l, j: (l, 0, j)),
            pl.BlockSpec((None, 1, tn), lambda l, j: (l, 0, j)),
        ],
        out_specs=pl.BlockSpec((None, rows, tn), lambda l, j: (l, 0, j)),
        out_shape=jax.ShapeDtypeStruct((depth, rows, d6), F32),
        compiler_params=_cparams(("parallel", "parallel")),
        name="ada_mod",
    )(cv, w_ada, b_ada.reshape(depth, 1, d6))


def _swap16(x):
    lane = lax.broadcasted_iota(jnp.int32, x.shape, 1)
    first = (lane % 32) < 16
    return jnp.where(first, pltpu.roll(x, 112, 1), pltpu.roll(x, 16, 1))


def _head_rms(x, bd, w):
    ms = jnp.dot((x * x).astype(BF16), bd, preferred_element_type=F32) * (1.0 / ATT_HDIM)
    return x * lax.rsqrt(ms + EPS) * w


def _inproj_kernel(*refs, rope):
    if rope:
        (x_ref, mod_ref, nw_ref, w_ref, cs_ref, wup_ref, bup_ref, qn_ref, kn_ref, bd_ref, cos_ref, sin_ref,
         ab_ref, gqk_ref, gv_ref, gvt_ref, gg_ref, la_ref, q_ref, kt_ref, vd_ref) = refs
    else:
        (x_ref, mod_ref, nw_ref, w_ref, cs_ref, wup_ref, bup_ref, qn_ref, kn_ref, bd_ref,
         ab_ref, gqk_ref, gv_ref, gvt_ref, gg_ref, la_ref, q_ref, kt_ref, vd_ref) = refs
    x = x_ref[...]
    ms = jnp.mean(x * x, axis=-1, keepdims=True)
    y = x * lax.rsqrt(ms + EPS) * nw_ref[...]
    h = y * (1.0 + mod_ref[1:2, :]) + mod_ref[0:1, :]
    acc = jnp.dot(h.astype(BF16), w_ref[...], preferred_element_type=F32)

    uab = jnp.dot(acc[:, C_U:C_U + FNET_WIDTH].astype(BF16), cs_ref[...], preferred_element_type=F32)
    ab_ref[0] = uab[:, :FNET_WIDTH].astype(BF16)
    ab_ref[1] = uab[:, FNET_WIDTH:].astype(BF16)

    gqk_ref[...] = acc[:, C_GQ:C_GQ + 2 * GLA_QK]
    gv = acc[:, C_GV:C_GV + GLA_WIDTH]
    gv_ref[...] = gv.astype(BF16)
    gvt_ref[...] = gv.T.astype(BF16)
    gg_ref[...] = acc[:, C_GG:C_GG + GLA_WIDTH].astype(BF16)
    pre = jnp.dot(acc[:, C_GD:C_GD + 128].astype(BF16), wup_ref[...], preferred_element_type=F32) + bup_ref[...]
    la_ref[...] = (jnp.minimum(pre, 0.0) - jnp.log1p(jnp.exp(-jnp.abs(pre)))) * (1.0 / GLA_GATE_NORM)

    bd = bd_ref[...]
    q = _head_rms(acc[:, C_AQ:C_AQ + ATT_WIDTH], bd, qn_ref[...])
    k = _head_rms(acc[:, C_AK:C_AK + 256], bd[:256, :256], kn_ref[...])
    if rope:
        cos = cos_ref[...]
        sin = sin_ref[...]
        q = jnp.concatenate(
            [q[:, s:s + 128] * cos + _swap16(q[:, s:s + 128]) * sin for s in range(0, ATT_WIDTH, 128)], axis=1)
        k = jnp.concatenate(
            [k[:, s:s + 128] * cos + _swap16(k[:, s:s + 128]) * sin for s in range(0, 256, 128)], axis=1)
    q_ref[...] = (q * (ATT_HDIM ** -0.5)).astype(BF16)
    v = acc[:, C_AV:C_AV + 256]
    for g in range(ATT_KV_HEADS):
        kt_ref[g] = k[:, 128 * g:128 * (g + 1)].T.astype(BF16)
        vd_ref[g] = v[:, 128 * g:128 * (g + 1)].astype(BF16)


def _inproj_call(x, mod_l, mod_row, nw, w_perm, cs, wup, bup, qn, kn, bd, rope_tabs):
    b, n, d = x.shape
    tm = min(512, n)
    nt = n // tm
    rope = rope_tabs is not None
    if mod_row is None:
        mod_map = lambda bi, i: (bi, 0, 0)
    else:
        mod_map = lambda bi, i: (mod_row, 0, 0)
    const = lambda bi, i: (0, 0)
    in_specs = [
        pl.BlockSpec((None, tm, d), lambda bi, i: (bi, i, 0)),
        pl.BlockSpec((None, 6, d), mod_map),
        pl.BlockSpec((1, d), const),
        pl.BlockSpec((d, W_IN_COLS), const),
        pl.BlockSpec((FNET_WIDTH, 2 * FNET_WIDTH), const),
        pl.BlockSpec((128, 2 * GLA_QK), const),
        pl.BlockSpec((1, 2 * GLA_QK), const),
        pl.BlockSpec((1, ATT_WIDTH), const),
        pl.BlockSpec((1, 256), const),
        pl.BlockSpec((ATT_WIDTH, ATT_WIDTH), const),
    ]
    args = [x, mod_l, nw, w_perm, cs, wup, bup, qn, kn, bd]
    if rope:
        in_specs += [pl.BlockSpec((tm, 128), lambda bi, i: (i, 0)), pl.BlockSpec((tm, 128), lambda bi, i: (i, 0))]
        args += list(rope_tabs)
    out_shape = (
        jax.ShapeDtypeStruct((2, n, b * FNET_WIDTH), BF16),
        jax.ShapeDtypeStruct((b, n, 2 * GLA_QK), F32),
        jax.ShapeDtypeStruct((b, n, GLA_WIDTH), BF16),
        jax.ShapeDtypeStruct((b, GLA_WIDTH, n), BF16),
        jax.ShapeDtypeStruct((b, n, GLA_WIDTH), BF16),
        jax.ShapeDtypeStruct((b, n, 2 * GLA_QK), F32),
        jax.ShapeDtypeStruct((b, n, ATT_WIDTH), BF16),
        jax.ShapeDtypeStruct((b, ATT_KV_HEADS, 128, n), BF16),
        jax.ShapeDtypeStruct((b, ATT_KV_HEADS, n, 128), BF16),
    )
    out_specs = (
        pl.BlockSpec((2, tm, FNET_WIDTH), lambda bi, i: (0, i, bi)),
        pl.BlockSpec((None, tm, 2 * GLA_QK), lambda bi, i: (bi, i, 0)),
        pl.BlockSpec((None, tm, GLA_WIDTH), lambda bi, i: (bi, i, 0)),
        pl.BlockSpec((None, GLA_WIDTH, tm), lambda bi, i: (bi, 0, i)),
        pl.BlockSpec((None, tm, GLA_WIDTH), lambda bi, i: (bi, i, 0)),
        pl.BlockSpec((None, tm, 2 * GLA_QK), lambda bi, i: (bi, i, 0)),
        pl.BlockSpec((None, tm, ATT_WIDTH), lambda bi, i: (bi, i, 0)),
        pl.BlockSpec((None, ATT_KV_HEADS, 128, tm), lambda bi, i: (bi, 0, 0, i)),
        pl.BlockSpec((None, ATT_KV_HEADS, tm, 128), lambda bi, i: (bi, 0, i, 0)),
    )
    return pl.pallas_call(
        functools.partial(_inproj_kernel, rope=rope),
        grid=(b, nt),
        in_specs=in_specs,
        out_specs=out_specs,
        out_shape=out_shape,
        compiler_params=_cparams(("parallel", "parallel")),
        name="inproj_rope" if rope else "inproj_ctx",
    )(*args)


def _seqdft_kernel(t_ref, ab_ref, o_ref):
    y = jnp.dot(t_ref[...], ab_ref[...], preferred_element_type=F32)
    for bb in range(o_ref.shape[0]):
        o_ref[bb] = y[:, bb * FNET_WIDTH:(bb + 1) * FNET_WIDTH].astype(BF16)


def _seqdft_call(table, ab, b):
    n = table.shape[0]
    tm = min(512, n)
    nb = 4 if b % 4 == 0 else (2 if b % 2 == 0 else 1)
    return pl.pallas_call(
        _seqdft_kernel,
        grid=(b // nb, n // tm),
        in_specs=[
            pl.BlockSpec((tm, 2 * n), lambda c, i: (i, 0)),
            pl.BlockSpec((2 * n, nb * FNET_WIDTH), lambda c, i: (0, c)),
        ],
        out_specs=pl.BlockSpec((nb, tm, FNET_WIDTH), lambda c, i: (c, i, 0)),
        out_shape=jax.ShapeDtypeStruct((b, n, FNET_WIDTH), BF16),
        compiler_params=_cparams(("parallel", "parallel")),
        name="seq_dft",
    )(table, ab)


def _gla_dir(qk, v, vt, a, s_in, fwd):
    p = GLA_PAIR
    r = lax.broadcasted_iota(jnp.int32, (p, p), 0)
    c = lax.broadcasted_iota(jnp.int32, (p, p), 1)
    same = (r // GLA_CHUNK) == (c // GLA_CHUNK)
    tri = same & ((c <= r) if fwd else (c >= r))
    tri_b = jnp.where(tri, 1.0, 0.0).astype(BF16)
    row_lo = r < GLA_CHUNK

    q = qk[:, :GLA_QK] * (GLA_DK ** -0.5)
    k = qk[:, GLA_QK:]
    a1 = a.astype(BF16)
    r1 = a - a1.astype(F32)
    a2 = r1.astype(BF16)
    a3 = (r1 - a2.astype(F32)).astype(BF16)
    cum = (jnp.dot(tri_b, a1, preferred_element_type=F32) + jnp.dot(tri_b, a2, preferred_element_type=F32)
           + jnp.dot(tri_b, a3, preferred_element_type=F32))
    if fwd:
        last0, last1 = cum[GLA_CHUNK - 1:GLA_CHUNK, :], cum[p - 1:p, :]
    else:
        last0, last1 = cum[0:1, :], cum[GLA_CHUNK:GLA_CHUNK + 1, :]
    lastb = jnp.where(row_lo, last0, last1)
    qt = q * jnp.exp(cum)
    kt = k * jnp.exp(-cum)
    kd = k * jnp.exp(lastb - cum)

    lane = c
    qs = jnp.concatenate(
        [jnp.where((lane // GLA_DK) == hh, qt, 0.0) for hh in range(GLA_HEADS)], axis=0).astype(BF16)
    att = _nt_dot(qs, kt.astype(BF16))
    tri4 = jnp.concatenate([tri] * GLA_HEADS, axis=0)
    att = jnp.where(tri4, att, 0.0).astype(BF16)
    of = jnp.dot(att, v, preferred_element_type=F32)
    col = lax.broadcasted_iota(jnp.int32, (p, GLA_WIDTH), 1)
    o_intra = jnp.zeros((p, GLA_WIDTH), F32)
    for hh in range(GLA_HEADS):
        o_intra = o_intra + jnp.where((col // GLA_DV) == hh, of[hh * p:(hh + 1) * p, :], 0.0)

    sr = lax.broadcasted_iota(jnp.int32, (GLA_WIDTH, GLA_QK), 0)
    sc = lax.broadcasted_iota(jnp.int32, (GLA_WIDTH, GLA_QK), 1)
    bdm = (sr // GLA_DV) == (sc // GLA_DK)
    order = (0, 1) if fwd else (1, 0)
    lasts = (last0, last1)
    s_cur = s_in
    o_parts = [None, None]
    for ci in order:
        rows = slice(ci * GLA_CHUNK, (ci + 1) * GLA_CHUNK)
        o_parts[ci] = _nt_dot(qt[rows, :].astype(BF16), s_cur.astype(BF16))
        in_chunk = row_lo if ci == 0 else jnp.logical_not(row_lo)
        kvt = jnp.dot(vt, jnp.where(in_chunk, kd, 0.0).astype(BF16), preferred_element_type=F32)
        s_cur = s_cur * jnp.exp(lasts[ci]) + jnp.where(bdm, kvt, 0.0)
    o_inter = jnp.concatenate(o_parts, axis=0)
    return o_intra + o_inter, s_cur


def _gla_kernel(qkf, vf, vtf, laf, qkb, vb, vtb, lab, s0_ref, of_ref, ob_ref, sfin_ref, s_scr):
    i = pl.program_id(1)

    @pl.when(i == 0)
    def _():
        s_scr[...] = s0_ref[...]

    for gi in range(qkf.shape[0]):
        o1, sf = _gla_dir(qkf[gi], vf[gi], vtf[gi], laf[gi], s_scr[gi, 0], True)
        o2, sb = _gla_dir(qkb[gi], vb[gi], vtb[gi], lab[gi], s_scr[gi, 1], False)
        of_ref[gi] = o1
        ob_ref[gi] = o2
        s_scr[gi, 0] = sf
        s_scr[gi, 1] = sb

    @pl.when(i == pl.num_programs(1) - 1)
    def _():
        sfin_ref[...] = s_scr[...]


def _gla_call(gqk, gv, gvt, la, s0):
    b, n, _ = gqk.shape
    p = GLA_PAIR
    npair = n // p
    gb = 4 if b % 4 == 0 else (2 if b % 2 == 0 else 1)
    fw = lambda bi, i: (bi, i, 0)
    bw = lambda bi, i: (bi, npair - 1 - i, 0)
    in_specs = [
        pl.BlockSpec((gb, p, 2 * GLA_QK), fw),
        pl.BlockSpec((gb, p, GLA_WIDTH), fw),
        pl.BlockSpec((gb, GLA_WIDTH, p), lambda bi, i: (bi, 0, i)),
        pl.BlockSpec((gb, p, GLA_QK), fw),
        pl.BlockSpec((gb, p, 2 * GLA_QK), bw),
        pl.BlockSpec((gb, p, GLA_WIDTH), bw),
        pl.BlockSpec((gb, GLA_WIDTH, p), lambda bi, i: (bi, 0, npair - 1 - i)),
        pl.BlockSpec((gb, p, GLA_QK), lambda bi, i: (bi, npair - 1 - i, 1)),
        pl.BlockSpec((gb, 2, GLA_WIDTH, GLA_QK), lambda bi, i: (bi, 0, 0, 0)),
    ]
    out_specs = (
        pl.BlockSpec((gb, p, GLA_WIDTH), fw),
        pl.BlockSpec((gb, p, GLA_WIDTH), bw),
        pl.BlockSpec((gb, 2, GLA_WIDTH, GLA_QK), lambda bi, i: (bi, 0, 0, 0)),
    )
    out_shape = (
        jax.ShapeDtypeStruct((b, n, GLA_WIDTH), F32),
        jax.ShapeDtypeStruct((b, n, GLA_WIDTH), F32),
        jax.ShapeDtypeStruct((b, 2, GLA_WIDTH, GLA_QK), F32),
    )
    return pl.pallas_call(
        _gla_kernel,
        grid=(b // gb, npair),
        in_specs=in_specs,
        out_specs=out_specs,
        out_shape=out_shape,
        scratch_shapes=[pltpu.VMEM((gb, 2, GLA_WIDTH, GLA_QK), F32)],
        compiler_params=_cparams(("parallel", "arbitrary")),
        name="gla_scan",
    )(gqk, gv, gvt, la, gqk, gv, gvt, la, s0)


def _attn_kernel(*refs, nparts):
    q_ref = refs[0]
    parts = [(refs[1 + 2 * i], refs[2 + 2 * i]) for i in range(nparts)]
    o_ref = refs[1 + 2 * nparts]
    tq = q_ref.shape[0]
    lane = lax.broadcasted_iota(jnp.int32, (tq, 128), 1)
    lo = lane < ATT_HDIM
    for j in range(ATT_HEADS // 2):
        q128 = q_ref[:, 128 * j:128 * (j + 1)]
        g = (2 * j) // (ATT_HEADS // ATT_KV_HEADS)
        outs = []
        for half in range(2):
            qm = jnp.where(lo if half == 0 else jnp.logical_not(lo), q128, jnp.zeros_like(q128))
            ss = [jnp.dot(qm, kt_ref[g], preferred_element_type=F32) for kt_ref, _ in parts]
            m = functools.reduce(jnp.maximum, [jnp.max(s, axis=-1, keepdims=True) for s in ss])
            ps = [jnp.exp(s - m) for s in ss]
            den = functools.reduce(lambda u, w: u + w, [jnp.sum(pp, axis=-1, keepdims=True) for pp in ps])
            pv = functools.reduce(
                lambda u, w: u + w,
                [jnp.dot(pp.astype(BF16), vd_ref[g], preferred_element_type=F32) for pp, (_, vd_ref) in zip(ps, parts)])
            outs.append(pv / den)
        o_ref[:, 128 * j:128 * (j + 1)] = jnp.where(lo, outs[0], outs[1]).astype(BF16)


def _attn_call(q, kv_parts):
    b, n, _ = q.shape
    tq = min(512, n)
    in_specs = [pl.BlockSpec((None, tq, ATT_WIDTH), lambda bi, i: (bi, i, 0))]
    args = [q]
    for kt, vd in kv_parts:
        m = kt.shape[-1]
        in_specs.append(pl.BlockSpec((None, ATT_KV_HEADS, 128, m), lambda bi, i: (bi, 0, 0, 0)))
        in_specs.append(pl.BlockSpec((None, ATT_KV_HEADS, m, 128), lambda bi, i: (bi, 0, 0, 0)))
        args += [kt, vd]
    return pl.pallas_call(
        functools.partial(_attn_kernel, nparts=len(kv_parts)),
        grid=(b, n // tq),
        in_specs=in_specs,
        out_specs=pl.BlockSpec((None, tq, ATT_WIDTH), lambda bi, i: (bi, i, 0)),
        out_shape=jax.ShapeDtypeStruct((b, n, ATT_WIDTH), BF16),
        compiler_params=_cparams(("parallel", "parallel")),
        name="gqa_attn",
    )(*args)


def _outproj_kernel(f_ref, of_ref, ob_ref, gg_ref, a_ref, x_ref, mod_ref, w_ref, gn_ref, bd_ref, nf_ref,
                    wrh_ref, wrl_ref, xn_ref, h2_ref, lg_ref):
    o = of_ref[...] + ob_ref[...]
    ms = jnp.dot((o * o).astype(BF16), bd_ref[...], preferred_element_type=F32) * (1.0 / GLA_DV)
    on = o * lax.rsqrt(ms + EPS) * gn_ref[...]
    g = gg_ref[...].astype(F32)
    gl = (on * (g * _sigmoid(g))).astype(BF16)
    ox = (jnp.dot(f_ref[...], w_ref[0:FNET_WIDTH, :], preferred_element_type=F32)
          + jnp.dot(gl, w_ref[FNET_WIDTH:FNET_WIDTH + GLA_WIDTH, :], preferred_element_type=F32)
          + jnp.dot(a_ref[...], w_ref[FNET_WIDTH + GLA_WIDTH:, :], preferred_element_type=F32))
    xn = x_ref[...] + mod_ref[2:3, :] * ox
    xn_ref[...] = xn
    ms2 = jnp.mean(xn * xn, axis=-1, keepdims=True)
    h2 = xn * lax.rsqrt(ms2 + EPS) * nf_ref[...] * (1.0 + mod_ref[4:5, :]) + mod_ref[3:4, :]
    hh = h2.astype(BF16)
    hl = (h2 - hh.astype(F32)).astype(BF16)
    h2_ref[...] = h2
    lg_ref[...] = _nt_dot(wrh_ref[...], hh) + _nt_dot(wrh_ref[...], hl) + _nt_dot(wrl_ref[...], hh)


def _outproj_call(f, of, ob, gg, a, x, mod_l, mod_row, w_out, gn, bd, nf, wrh, wrl):
    b, n, d = x.shape
    tm = min(512, n)
    nt = n // tm
    if mod_row is None:
        mod_map = lambda bi, i: (bi, 0, 0)
    else:
        mod_map = lambda bi, i: (mod_row, 0, 0)
    const = lambda bi, i: (0, 0)
    tok = lambda w: pl.BlockSpec((None, tm, w), lambda bi, i: (bi, i, 0))
    in_specs = [
        tok(FNET_WIDTH), tok(GLA_WIDTH), tok(GLA_WIDTH), tok(GLA_WIDTH), tok(ATT_WIDTH), tok(d),
        pl.BlockSpec((None, 6, d), mod_map),
        pl.BlockSpec((d, d), const),
        pl.BlockSpec((1, GLA_WIDTH), const),
        pl.BlockSpec((GLA_WIDTH, GLA_WIDTH), const),
        pl.BlockSpec((1, d), const),
        pl.BlockSpec((N_EXPERTS, d), const),
        pl.BlockSpec((N_EXPERTS, d), const),
    ]
    out_specs = (
        tok(d), tok(d),
        pl.BlockSpec((N_EXPERTS, tm), lambda bi, i: (0, bi * nt + i)),
    )
    out_shape = (
        jax.ShapeDtypeStruct((b, n, d), F32),
        jax.ShapeDtypeStruct((b, n, d), F32),
        jax.ShapeDtypeStruct((N_EXPERTS, b * n), F32),
    )
    return pl.pallas_call(
        _outproj_kernel,
        grid=(b, nt),
        in_specs=in_specs,
        out_specs=out_specs,
        out_shape=out_shape,
        compiler_params=_cparams(("parallel", "parallel")),
        name="outproj_router",
    )(f, of, ob, gg, a, x, mod_l, w_out, gn, bd, nf, wrh, wrl)


def _route_kernel(b_ref, lg_ref, pos_ref, wt_ref, te_ref, nv_ref, *, tm):
    r = lg_ref.shape[1]
    s = [_sigmoid(lg_ref[e]) for e in range(N_EXPERTS)]
    sel = [s[e] + b_ref[e] for e in range(N_EXPERTS)]
    grp = []
    for g in range(N_GROUPS):
        a, b, c, d = sel[4 * g:4 * g + 4]
        hi1, lo1 = jnp.maximum(a, b), jnp.minimum(a, b)
        hi2, lo2 = jnp.maximum(c, d), jnp.minimum(c, d)
        m1 = jnp.maximum(hi1, hi2)
        m2 = jnp.maximum(jnp.minimum(hi1, hi2), jnp.maximum(lo1, lo2))
        grp.append(m1 + m2)
    one = jnp.ones_like(s[0])
    zero = jnp.zeros_like(s[0])
    msk = []
    for g in range(N_GROUPS):
        isg = one
        for g2 in range(N_GROUPS):
            if g2 < g:
                isg = isg * jnp.where(grp[g] > grp[g2], one, zero)
            elif g2 > g:
                isg = isg * jnp.where(grp[g] >= grp[g2], one, zero)
        for li in range(EXPERTS_PER_GROUP):
            e = 4 * g + li
            rank = zero
            for lj in range(EXPERTS_PER_GROUP):
                ej = 4 * g + lj
                if lj < li:
                    rank = rank + jnp.where(sel[ej] >= sel[e], one, zero)
                elif lj > li:
                    rank = rank + jnp.where(sel[ej] > sel[e], one, zero)
            msk.append(jnp.where(rank < 2.0, isg, zero))
    den = functools.reduce(lambda u, v: u + v, [msk[e] * s[e] for e in range(N_EXPERTS)])

    li_ = lax.broadcasted_iota(jnp.int32, (128, 128), 0)
    lj_ = lax.broadcasted_iota(jnp.int32, (128, 128), 1)
    upper = jnp.where(li_ < lj_, 1.0, 0.0).astype(BF16)
    ri_ = lax.broadcasted_iota(jnp.int32, (r, r), 0)
    rj_ = lax.broadcasted_iota(jnp.int32, (r, r), 1)
    lower = jnp.where(rj_ < ri_, 1.0, 0.0).astype(BF16)
    tile_start = lax.broadcasted_iota(jnp.int32, te_ref.shape, 1).astype(F32) * float(tm)
    te = jnp.zeros(te_ref.shape, F32)
    off = jnp.zeros((1, 1), F32)
    seen = zero
    pos = [zero, zero]
    wts = [zero, zero]
    for e in range(N_EXPERTS):
        mb = msk[e].astype(BF16)
        lane_pre = jnp.dot(mb, upper, preferred_element_type=F32)
        row_pre = jnp.sum(jnp.dot(lower, mb, preferred_element_type=F32), axis=1, keepdims=True)
        cnt = jnp.sum(jnp.sum(msk[e], axis=1, keepdims=True), axis=0, keepdims=True)
        p_e = off + row_pre + lane_pre
        g_e = s[e] / den
        for kk in range(2):
            hit = msk[e] * jnp.where(seen == float(kk), one, zero)
            pos[kk] = pos[kk] + hit * p_e
            wts[kk] = wts[kk] + hit * g_e
        seen = seen + msk[e]
        off = off + jnp.floor((cnt + float(tm - 1)) * (1.0 / tm)) * float(tm)
        te = te + jnp.where(tile_start >= off, 1.0, 0.0)
    for kk in range(2):
        pos_ref[kk] = pos[kk].astype(jnp.int32)
        wt_ref[kk] = wts[kk]
    te_ref[...] = jnp.minimum(te, float(N_EXPERTS - 1)).astype(jnp.int32)
    nv_ref[...] = jnp.broadcast_to(off * (1.0 / tm), nv_ref.shape).astype(jnp.int32)


def _route_call(lgt, b_router, tm):
    n_tok = lgt.shape[1]
    r = n_tok // 128
    assert r * 128 == n_tok and r % 8 == 0 and 2 * n_tok // tm + N_EXPERTS <= 256
    lg3 = lgt.reshape(N_EXPERTS, r, 128)
    full3 = lambda k: pl.BlockSpec((k, r, 128), lambda: (0, 0, 0))
    pos, wts, te, nv = pl.pallas_call(
        functools.partial(_route_kernel, tm=tm),
        in_specs=[pl.BlockSpec(memory_space=pltpu.SMEM), full3(N_EXPERTS)],
        out_specs=(full3(2), full3(2), pl.BlockSpec((1, 256), lambda: (0, 0)), pl.BlockSpec((1, 128), lambda: (0, 0))),
        out_shape=(
            jax.ShapeDtypeStruct((2, r, 128), jnp.int32),
            jax.ShapeDtypeStruct((2, r, 128), F32),
            jax.ShapeDtypeStruct((1, 256), jnp.int32),
            jax.ShapeDtypeStruct((1, 128), jnp.int32),
        ),
        compiler_params=pltpu.CompilerParams(vmem_limit_bytes=VMEM_LIMIT),
        name="route",
    )(b_router, lg3)
    return pos.reshape(2, n_tok), wts.reshape(2, n_tok), te.reshape(256), nv[0, :1]


SC_CORES = 2
SC_SUBCORES = 16
SC_WORKERS = SC_CORES * SC_SUBCORES
SC_CHUNK = 32


def _sc_mesh():
    return plsc.VectorSubcoreMesh(core_axis_name="c", subcore_axis_name="s",
                                  num_cores=SC_CORES, num_subcores=SC_SUBCORES)


def _sc_steps(n_rows):
    per_w = n_rows // SC_WORKERS
    steps = per_w // SC_CHUNK
    assert per_w * SC_WORKERS == n_rows and steps * SC_CHUNK == per_w and steps % 2 == 0, n_rows
    return per_w, steps


def _sc_gather_rows(table, idx):
    p = idx.shape[0]
    d = table.shape[1]
    per_w, steps = _sc_steps(p)
    idx3 = idx.reshape(SC_WORKERS, steps, SC_CHUNK)

    @functools.partial(
        pl.kernel, mesh=_sc_mesh(),
        out_type=jax.ShapeDtypeStruct((p, d), table.dtype),
        scratch_types=[
            pltpu.VMEM((steps, SC_CHUNK), jnp.int32),
            pltpu.VMEM((SC_CHUNK, d), table.dtype),
            pltpu.VMEM((SC_CHUNK, d), table.dtype),
            pltpu.SemaphoreType.DMA, pltpu.SemaphoreType.DMA,
            pltpu.SemaphoreType.DMA, pltpu.SemaphoreType.DMA,
        ],
        name="sc_gather_rows",
    )
    def k(table_hbm, idx_hbm, out_hbm, idx_v, buf0, buf1, g0, g1, w0, w1):
        wid = lax.axis_index("s") * SC_CORES + lax.axis_index("c")
        base = wid * per_w
        pltpu.sync_copy(idx_hbm.at[wid], idx_v)

        def gather(s, buf, sem):
            return pltpu.make_async_copy(table_hbm.at[idx_v.at[s]], buf, sem)

        def write(s, buf, sem):
            return pltpu.make_async_copy(buf, out_hbm.at[pl.ds(base + s * SC_CHUNK, SC_CHUNK)], sem)

        gather(0, buf0, g0).start()

        @pl.loop(0, steps, step=2)
        def _(s):
            gather(s + 1, buf1, g1).start()
            gather(s, buf0, g0).wait()
            write(s, buf0, w0).start()
            write(s, buf0, w0).wait()

            @pl.when(s + 2 < steps)
            def _():
                gather(s + 2, buf0, g0).start()

            gather(s + 1, buf1, g1).wait()
            write(s + 1, buf1, w1).start()
            write(s + 1, buf1, w1).wait()

    return k(table, idx3)


def _sc_dispatch(srcs, poss, p_rows):
    d = srcs[0].shape[1]
    plans = [_sc_steps(src.shape[0]) for src in srcs]
    idxs = [pos.reshape(2, SC_WORKERS, st, SC_CHUNK) for pos, (_, st) in zip(poss, plans)]
    nseg = len(srcs)
    scratch = [pltpu.VMEM((2, st, SC_CHUNK), jnp.int32) for _, st in plans]
    scratch += [pltpu.VMEM((SC_CHUNK, d), F32), pltpu.VMEM((SC_CHUNK, d), F32)]
    scratch += [pltpu.SemaphoreType.DMA] * 6

    @functools.partial(
        pl.kernel, mesh=_sc_mesh(),
        out_type=jax.ShapeDtypeStruct((p_rows, d), F32),
        scratch_types=scratch,
        name="sc_dispatch",
    )
    def k(*refs):
        src_hbm = refs[:nseg]
        idx_hbm = refs[nseg:2 * nseg]
        out_hbm = refs[2 * nseg]
        idx_v = refs[2 * nseg + 1:3 * nseg + 1]
        buf0, buf1, r0, r1, a0, a1, b0, b1 = refs[3 * nseg + 1:]
        wid = lax.axis_index("s") * SC_CORES + lax.axis_index("c")
        for seg in range(nseg):
            per_w, steps = plans[seg]
            base = wid * per_w
            for kk in range(2):
                pltpu.sync_copy(idx_hbm[seg].at[kk, wid], idx_v[seg].at[kk])

            def read(s, buf, sem, seg=seg, base=base):
                return pltpu.make_async_copy(src_hbm[seg].at[pl.ds(base + s * SC_CHUNK, SC_CHUNK)], buf, sem)

            def scat(kk, s, buf, sem, seg=seg):
                return pltpu.make_async_copy(buf, out_hbm.at[idx_v[seg].at[kk, s]], sem)

            read(0, buf0, r0).start()

            @pl.loop(0, steps, step=2)
            def _(s, read=read, scat=scat, steps=steps):
                read(s + 1, buf1, r1).start()
                read(s, buf0, r0).wait()
                scat(0, s, buf0, a0).start()
                scat(1, s, buf0, b0).start()
                scat(0, s, buf0, a0).wait()
                scat(1, s, buf0, b0).wait()

                @pl.when(s + 2 < steps)
                def _():
                    read(s + 2, buf0, r0).start()

                read(s + 1, buf1, r1).wait()
                scat(0, s + 1, buf1, a1).start()
                scat(1, s + 1, buf1, b1).start()
                scat(0, s + 1, buf1, a1).wait()
                scat(1, s + 1, buf1, b1).wait()

    return k(*srcs, *idxs)


MOE_TM = 512


def _experts_kernel(te_ref, nv_ref, xs_ref, wg_ref, wu_ref, wd_ref, ys_ref, wgb, wub, wdb):
    i = pl.program_id(0)
    valid = i < nv_ref[0]
    changed = jnp.logical_or(i == 0, te_ref[i] != te_ref[jnp.maximum(i - 1, 0)])

    @pl.when(jnp.logical_and(valid, changed))
    def _():
        wgb[...] = wg_ref[...].astype(BF16)
        wub[...] = wu_ref[...].astype(BF16)
        wdb[...] = wd_ref[...].astype(BF16)

    @pl.when(valid)
    def _():
        h = xs_ref[...].astype(BF16)
        a = jnp.dot(h, wgb[...], preferred_element_type=F32)
        u = jnp.dot(h, wub[...], preferred_element_type=F32)
        t = ((a * _sigmoid(a)) * u).astype(BF16)
        ys_ref[...] = jnp.dot(t, wdb[...], preferred_element_type=F32)


def _experts_call(xs, te, nv, wg, wu, wd, layer):
    p_rows, d = xs.shape
    tm = MOE_TM
    nt = p_rows // tm
    row = lambda i, te_r, nv_r: (jnp.minimum(i, nv_r[0] - 1), 0)
    wsel = lambda i, te_r, nv_r: (layer, te_r[jnp.minimum(i, nv_r[0] - 1)], 0, 0)
    grid_spec = pltpu.PrefetchScalarGridSpec(
        num_scalar_prefetch=2,
        grid=(nt,),
        in_specs=[
            pl.BlockSpec((tm, d), row),
            pl.BlockSpec((None, None, d, D_EXPERT), wsel),
            pl.BlockSpec((None, None, d, D_EXPERT), wsel),
            pl.BlockSpec((None, None, D_EXPERT, d), wsel),
        ],
        out_specs=pl.BlockSpec((tm, d), row),
        scratch_shapes=[
            pltpu.VMEM((d, D_EXPERT), BF16),
            pltpu.VMEM((d, D_EXPERT), BF16),
            pltpu.VMEM((D_EXPERT, d), BF16),
        ],
    )
    return pl.pallas_call(
        _experts_kernel,
        grid_spec=grid_spec,
        out_shape=jax.ShapeDtypeStruct((p_rows, d), F32),
        compiler_params=_cparams(("arbitrary",)),
        name="moe_experts",
    )(te, nv, xs, wg, wu, wd)


def _combine_kernel(y_ref, wt_ref, x_ref, mod_ref, fn_ref, o_ref, *, final):
    wt = wt_ref[...]
    y = wt[:, 0:1] * y_ref[0] + wt[:, 1:2] * y_ref[1]
    xo = x_ref[...] + mod_ref[5:6, :] * y
    if final:
        ms = jnp.mean(xo * xo, axis=-1, keepdims=True)
        xo = xo * lax.rsqrt(ms + EPS) * fn_ref[...]
    o_ref[...] = xo


def _combine_call(y2, wts, x, mod_l, mod_row, fn, final):
    b, n, d = x.shape
    tm = min(512, n)
    if mod_row is None:
        mod_map = lambda bi, i: (bi, 0, 0)
    else:
        mod_map = lambda bi, i: (mod_row, 0, 0)
    tok = lambda w: pl.BlockSpec((None, tm, w), lambda bi, i: (bi, i, 0))
    return pl.pallas_call(
        functools.partial(_combine_kernel, final=final),
        grid=(b, n // tm),
        in_specs=[
            pl.BlockSpec((2, None, tm, d), lambda bi, i: (0, bi, i, 0)),
            tok(2), tok(d),
            pl.BlockSpec((None, 6, d), mod_map),
            pl.BlockSpec((1, d), lambda bi, i: (0, 0)),
        ],
        out_specs=tok(d),
        out_shape=jax.ShapeDtypeStruct((b, n, d), F32),
        compiler_params=_cparams(("parallel", "parallel")),
        name="moe_combine",
    )(y2, wts, x, mod_l, fn)


def _moe_sparse(h_list, lg_list, x_list, mod_l, mod_rows, b_router, wg, wu, wd, layer, fn, final):
    d = h_list[0].shape[-1]
    sizes = [h.shape[0] * h.shape[1] for h in h_list]
    n_tok = sum(sizes)
    lgt = lg_list[0] if len(lg_list) == 1 else jnp.concatenate(lg_list, axis=1)
    pos, wts, te, nv = _route_call(lgt, b_router, MOE_TM)
    p_rows = 2 * n_tok + N_EXPERTS * MOE_TM
    offs = np.cumsum([0] + sizes)
    poss = [pos[:, offs[i]:offs[i + 1]] for i in range(len(sizes))]
    xs = _sc_dispatch([h.reshape(-1, d) for h in h_list], poss, p_rows)
    ys = _experts_call(xs, te, nv, wg, wu, wd, layer)
    outs = []
    for i, (h, x) in enumerate(zip(h_list, x_list)):
        bb, nn, _ = x.shape
        y2 = _sc_gather_rows(ys, poss[i].reshape(-1)).reshape(2, bb, nn, d)
        w_i = wts[:, offs[i]:offs[i + 1]].T.reshape(bb, nn, 2)
        outs.append(_combine_call(y2, w_i, x, mod_l, mod_rows[i], fn, final))
    return outs


def _permute_w_in(w):
    o = 0
    u = w[:, o:o + 256]; o += 256
    gq = w[:, o:o + 128]; o += 128
    gk = w[:, o:o + 128]; o += 128
    gv = w[:, o:o + 256]; o += 256
    gg = w[:, o:o + 256]; o += 256
    gdf = w[:, o:o + 16]; o += 16
    gdb = w[:, o:o + 16]; o += 16
    aq = w[:, o:o + 512]; o += 512
    ak = w[:, o:o + 128]; o += 128
    av = w[:, o:o + 128]; o += 128
    k0, k1 = ak[:, :64], ak[:, 64:]
    v0, v1 = av[:, :64], av[:, 64:]
    pad = jnp.zeros((w.shape[0], 96), w.dtype)
    return jnp.concatenate([u, gq, gk, gv, gg, aq, k0, k0, k1, k1, v0, v0, v1, v1, gdf, gdb, pad], axis=1).astype(BF16)


def _gate_up_weights(w_up, b_up):
    z = jnp.zeros((GLA_GATE_RANK, GLA_QK), w_up.dtype)
    top = jnp.concatenate([w_up[0], z], axis=1)
    mid = jnp.concatenate([z, w_up[1]], axis=1)
    pad = jnp.zeros((128 - 2 * GLA_GATE_RANK, 2 * GLA_QK), w_up.dtype)
    return jnp.concatenate([top, mid, pad], axis=0).astype(BF16), b_up.reshape(1, 2 * GLA_QK)


def kernel(x, c, ctx, c_ctx, w_ada, b_ada, norm_mix, norm_ffn, w_in, w_gla_gate_up, b_gla_gate, gla_norm, q_norm,
           k_norm, w_out, w_router, b_router, w_exp_gate, w_exp_up, w_exp_down, final_norm):
    b, n, d = x.shape
    m = ctx.shape[1]
    depth = w_ada.shape[0]
    assert d == D_MODEL and n % GLA_PAIR == 0 and m % GLA_PAIR == 0 and n % GRID_W == 0

    rows = ((b + 1 + 7) // 8) * 8
    cv = jnp.concatenate([c, c_ctx[None, :], jnp.zeros((rows - b - 1, d), F32)], axis=0)
    mod = _ada_call(cv, w_ada, b_ada).reshape(depth, rows, 6, d)

    cs = jnp.asarray(_channel_dft_table()).astype(BF16)
    tab_x = jnp.asarray(_seq_dft_table(n)).astype(BF16)
    tab_c = jnp.asarray(_seq_dft_table(m)).astype(BF16)
    rope_tabs = tuple(jnp.asarray(t) for t in _rope_tables(n))
    bd512 = jnp.asarray(_blockdiag_ones(ATT_WIDTH, ATT_HDIM)).astype(BF16)
    bd256 = jnp.asarray(_blockdiag_ones(GLA_WIDTH, GLA_DV)).astype(BF16)
    wr_t = w_router.T
    wrh = wr_t.astype(BF16)
    wrl = (wr_t - wrh.astype(F32)).astype(BF16)
    fn = final_norm.reshape(1, d)

    xc = ctx
    for l in range(depth):
        ctx_out = l < depth - 1
        mod_l = mod[l]
        w_perm = _permute_w_in(w_in[l])
        wup, bup = _gate_up_weights(w_gla_gate_up[l], b_gla_gate[l])
        nw = norm_mix[l].reshape(1, d)
        nf = norm_ffn[l].reshape(1, d)
        qn = jnp.tile(q_norm[l], ATT_HEADS).reshape(1, ATT_WIDTH)
        kn = jnp.tile(k_norm[l], 4).reshape(1, 256)
        gn = jnp.tile(gla_norm[l], GLA_HEADS).reshape(1, GLA_WIDTH)
        wo = w_out[l].astype(BF16)

        (ab_c, gqk_c, gv_c, gvt_c, gg_c, la_c, q_c, kt_c, vd_c) = _inproj_call(
            xc, mod_l, b, nw, w_perm, cs, wup, bup, qn, kn, bd512, None)
        (ab_x, gqk_x, gv_x, gvt_x, gg_x, la_x, q_x, kt_x, vd_x) = _inproj_call(
            x, mod_l, None, nw, w_perm, cs, wup, bup, qn, kn, bd512, rope_tabs)

        s_zero = jnp.zeros((b, 2, GLA_WIDTH, GLA_QK), F32)
        of_c, ob_c, s_fin = _gla_call(gqk_c, gv_c, gvt_c, la_c, s_zero)
        of_x, ob_x, _ = _gla_call(gqk_x, gv_x, gvt_x, la_x, s_fin)

        f_x = _seqdft_call(tab_x, ab_x.reshape(2 * n, b * FNET_WIDTH), b)
        a_x = _attn_call(q_x, [(kt_c, vd_c), (kt_x, vd_x)])
        x, h2_x, lg_x = _outproj_call(f_x, of_x, ob_x, gg_x, a_x, x, mod_l, None, wo, gn, bd256, nf, wrh, wrl)

        if ctx_out:
            f_c = _seqdft_call(tab_c, ab_c.reshape(2 * m, b * FNET_WIDTH), b)
            a_c = _attn_call(q_c, [(kt_c, vd_c)])
            xc, h2_c, lg_c = _outproj_call(f_c, of_c, ob_c, gg_c, a_c, xc, mod_l, b, wo, gn, bd256, nf, wrh, wrl)

        final = l == depth - 1
        wexp = (w_exp_gate, w_exp_up, w_exp_down, l)
        if ctx_out:
            x, xc = _moe_sparse([h2_x, h2_c], [lg_x, lg_c], [x, xc], mod_l, [None, b], b_router, *wexp, fn, final)
        else:
            (x,) = _moe_sparse([h2_x], [lg_x], [x], mod_l, [None], b_router, *wexp, fn, final)
    return x
```

```python
import functools

import numpy as np
import jax
import jax.numpy as jnp
from jax import lax
from jax.experimental import pallas as pl
from jax.experimental.pallas import tpu as pltpu
from jax.experimental.pallas import tpu_sc as plsc

F32 = jnp.float32
BF16 = jnp.bfloat16

D_MODEL = 1024
GRID_W = 64
EPS = 1e-6
LOG2E = 1.4426950408889634

FNET_WIDTH = 256
FNET_GROUPS = 4
FNET_GDIM = 64

GLA_HEADS = 4
GLA_DV = 64
GLA_DK = 32
GLA_WIDTH = 256
GLA_QK = 128
GLA_GATE_RANK = 16
GLA_GATE_NORM = 16.0
GLA_CHUNK = 64
GLA_PAIR = 2 * GLA_CHUNK

ATT_HEADS = 8
ATT_KV_HEADS = 2
ATT_HDIM = 64
ATT_WIDTH = 512
ROPE_FREQS = 16
ROPE_THETA = 10000.0

N_EXPERTS = 16
N_GROUPS = 4
EXPERTS_PER_GROUP = 4
D_EXPERT = 512

C_U = 0
C_GQ = 256
C_GK = 384
C_GV = 512
C_GG = 768
C_AQ = 1024
C_AK = 1536
C_AV = 1792
C_GD = 2048
W_IN_COLS = 2176

VMEM_LIMIT = 56 * 1024 * 1024


def _cparams(sem):
    return pltpu.CompilerParams(dimension_semantics=sem, vmem_limit_bytes=VMEM_LIMIT)


def _sigmoid(x):
    return 1.0 / (1.0 + jnp.exp(-x))


def _pack_bf16_pairs(x):
    w = x.shape[1] // 2
    lo = lax.bitcast_convert_type(x[:, :w].astype(BF16).astype(F32), jnp.uint32)
    hi = lax.bitcast_convert_type(x[:, w:].astype(BF16).astype(F32), jnp.uint32)
    return (lo >> 16) | (hi & jnp.uint32(0xFFFF0000))


def _unpack_bf16_pairs(p):
    lo = lax.bitcast_convert_type(p << 16, F32)
    hi = lax.bitcast_convert_type(p & jnp.uint32(0xFFFF0000), F32)
    return jnp.concatenate([lo, hi], axis=1)


def _nt_dot(a, b):
    return lax.dot_general(a, b, (((1,), (1,)), ((), ())), preferred_element_type=F32)


@functools.lru_cache(maxsize=None)
def _channel_dft_table():
    j = np.arange(FNET_GDIM)
    ang = 2.0 * np.pi * ((j[:, None] * j[None, :]) % FNET_GDIM) / FNET_GDIM
    c = np.cos(ang) / np.sqrt(FNET_GDIM)
    s = np.sin(ang) / np.sqrt(FNET_GDIM)
    out = np.zeros((FNET_WIDTH, 2 * FNET_WIDTH), np.float64)
    for g in range(FNET_GROUPS):
        sl = slice(g * FNET_GDIM, (g + 1) * FNET_GDIM)
        out[sl, sl] = c
        out[sl, FNET_WIDTH + g * FNET_GDIM:FNET_WIDTH + (g + 1) * FNET_GDIM] = s
    return out.astype(np.float32)


@functools.lru_cache(maxsize=None)
def _seq_dft_table(n):
    j = np.arange(n, dtype=np.int64)
    ang = 2.0 * np.pi * ((j[:, None] * j[None, :]) % n) / n
    return np.concatenate([np.cos(ang), -np.sin(ang)], axis=1).astype(np.float32) / np.float32(np.sqrt(n))


@functools.lru_cache(maxsize=None)
def _rope_tables(n):
    rows = n // GRID_W
    row = np.repeat(np.arange(rows), GRID_W).astype(np.float64)
    col = np.tile(np.arange(GRID_W), rows).astype(np.float64)
    inv = ROPE_THETA ** (-np.arange(ROPE_FREQS, dtype=np.float64) * 2.0 / (2 * ROPE_FREQS))
    ar = row[:, None] * inv[None, :]
    ac = col[:, None] * inv[None, :]
    cos = np.concatenate([np.cos(ar), np.cos(ar), np.cos(ac), np.cos(ac)], axis=1)
    sin = np.concatenate([-np.sin(ar), np.sin(ar), -np.sin(ac), np.sin(ac)], axis=1)
    return (np.tile(cos, (1, 2)).astype(np.float32), np.tile(sin, (1, 2)).astype(np.float32))


@functools.lru_cache(maxsize=None)
def _blockdiag_ones(width, blk):
    i = np.arange(width)
    return (i[:, None] // blk == i[None, :] // blk).astype(np.float32)


def _ada_kernel(cv_ref, w_ref, b_ref, o_ref):
    cv = cv_ref[...]
    a = (cv * _sigmoid(cv)).astype(BF16)
    o_ref[...] = jnp.dot(a, w_ref[...].astype(BF16), preferred_element_type=F32) + b_ref[...]


def _ada_call(cv, w_ada, b_ada):
    depth, d, d6 = w_ada.shape
    tn = 1536
    rows = cv.shape[0]
    return pl.pallas_call(
        _ada_kernel,
        grid=(depth, d6 // tn),
        in_specs=[
            pl.BlockSpec((rows, d), lambda l, j: (0, 0)),
            pl.BlockSpec((None, d, tn), lambda l, j: (l, 0, j)),
            pl.BlockSpec((None, 1, tn), lambda l, j: (l, 0, j)),
        ],
        out_specs=pl.BlockSpec((None, rows, tn), lambda l, j: (l, 0, j)),
        out_shape=jax.ShapeDtypeStruct((depth, rows, d6), F32),
        compiler_params=_cparams(("parallel", "parallel")),
        name="ada_mod",
    )(cv, w_ada, b_ada.reshape(depth, 1, d6))


def _swap16(x):
    lane = lax.broadcasted_iota(jnp.int32, x.shape, 1)
    first = (lane % 32) < 16
    return jnp.where(first, pltpu.roll(x, 112, 1), pltpu.roll(x, 16, 1))


def _head_rms(x, bd, w):
    ms = jnp.dot((x * x).astype(BF16), bd, preferred_element_type=F32) * (1.0 / ATT_HDIM)
    return x * lax.rsqrt(ms + EPS) * w


def _inproj_kernel(*refs, rope):
    if rope:
        (x_ref, mod_ref, nw_ref, w_ref, cs_ref, wup_ref, bup_ref, qn_ref, kn_ref, bd_ref, cos_ref, sin_ref,
         ab_ref, gqk_ref, gv_ref, gvt_ref, gg_ref, la_ref, q_ref, kt_ref, vd_ref) = refs
    else:
        (x_ref, mod_ref, nw_ref, w_ref, cs_ref, wup_ref, bup_ref, qn_ref, kn_ref, bd_ref,
         ab_ref, gqk_ref, gv_ref, gvt_ref, gg_ref, la_ref, q_ref, kt_ref, vd_ref) = refs
    x = x_ref[...]
    ms = jnp.mean(x * x, axis=-1, keepdims=True)
    y = x * lax.rsqrt(ms + EPS) * nw_ref[...]
    h = y * (1.0 + mod_ref[1:2, :]) + mod_ref[0:1, :]
    hb = h.astype(BF16)

    def proj(c0, width):
        return jnp.dot(hb, w_ref[:, c0:c0 + width], preferred_element_type=F32)

    uab = jnp.dot(proj(C_U, FNET_WIDTH).astype(BF16), cs_ref[...], preferred_element_type=F32)
    ab_ref[0] = uab[:, :FNET_WIDTH].astype(BF16)
    ab_ref[1] = uab[:, FNET_WIDTH:].astype(BF16)

    gqk_ref[...] = proj(C_GQ, 2 * GLA_QK)
    gv = proj(C_GV, GLA_WIDTH)
    gv_ref[...] = gv.astype(BF16)
    gvt_ref[...] = gv.T.astype(BF16)
    gg_ref[...] = proj(C_GG, GLA_WIDTH).astype(BF16)
    pre = jnp.dot(proj(C_GD, 128).astype(BF16), wup_ref[...], preferred_element_type=F32) + bup_ref[...]
    la_ref[...] = (jnp.minimum(pre, 0.0) - jnp.log1p(jnp.exp(-jnp.abs(pre)))) * (1.0 / GLA_GATE_NORM)

    bd = bd_ref[...]
    q = _head_rms(proj(C_AQ, ATT_WIDTH), bd, qn_ref[...])
    k = _head_rms(proj(C_AK, 256), bd[:256, :256], kn_ref[...])
    if rope:
        cos = cos_ref[...]
        sin = sin_ref[...]
        q = jnp.concatenate(
            [q[:, s:s + 128] * cos + _swap16(q[:, s:s + 128]) * sin for s in range(0, ATT_WIDTH, 128)], axis=1)
        k = jnp.concatenate(
            [k[:, s:s + 128] * cos + _swap16(k[:, s:s + 128]) * sin for s in range(0, 256, 128)], axis=1)
    q_ref[...] = (q * (ATT_HDIM ** -0.5 * LOG2E)).astype(BF16)
    v = proj(C_AV, 256)
    lane = lax.broadcasted_iota(jnp.int32, (x.shape[0], 128), 1)
    for g in range(ATT_KV_HEADS):
        kt_ref[g] = k[:, 128 * g:128 * (g + 1)].T.astype(BF16)
        vv = v[:, 128 * g:128 * (g + 1)]
        vd_ref[2 * g] = jnp.where(lane < ATT_HDIM, vv, 1.0).astype(BF16)
        vd_ref[2 * g + 1] = jnp.where(lane < ATT_HDIM, 1.0, vv).astype(BF16)


def _inproj_call(x, mod_l, mod_row, nw, w_perm, cs, wup, bup, qn, kn, bd, rope_tabs):
    b, n, d = x.shape
    tm = min(512, n)
    nt = n // tm
    rope = rope_tabs is not None
    if mod_row is None:
        mod_map = lambda bi, i: (bi, 0, 0)
    else:
        mod_map = lambda bi, i: (mod_row, 0, 0)
    const = lambda bi, i: (0, 0)
    in_specs = [
        pl.BlockSpec((None, tm, d), lambda bi, i: (bi, i, 0)),
        pl.BlockSpec((None, 6, d), mod_map),
        pl.BlockSpec((1, d), const),
        pl.BlockSpec((d, W_IN_COLS), const),
        pl.BlockSpec((FNET_WIDTH, 2 * FNET_WIDTH), const),
        pl.BlockSpec((128, 2 * GLA_QK), const),
        pl.BlockSpec((1, 2 * GLA_QK), const),
        pl.BlockSpec((1, ATT_WIDTH), const),
        pl.BlockSpec((1, 256), const),
        pl.BlockSpec((ATT_WIDTH, ATT_WIDTH), const),
    ]
    args = [x, mod_l, nw, w_perm, cs, wup, bup, qn, kn, bd]
    if rope:
        in_specs += [pl.BlockSpec((tm, 128), lambda bi, i: (i, 0)), pl.BlockSpec((tm, 128), lambda bi, i: (i, 0))]
        args += list(rope_tabs)
    out_shape = (
        jax.ShapeDtypeStruct((2, n, b * FNET_WIDTH), BF16),
        jax.ShapeDtypeStruct((b, n, 2 * GLA_QK), F32),
        jax.ShapeDtypeStruct((b, n, GLA_WIDTH), BF16),
        jax.ShapeDtypeStruct((b, GLA_WIDTH, n), BF16),
        jax.ShapeDtypeStruct((b, n, GLA_WIDTH), BF16),
        jax.ShapeDtypeStruct((b, n, 2 * GLA_QK), F32),
        jax.ShapeDtypeStruct((b, n, ATT_WIDTH), BF16),
        jax.ShapeDtypeStruct((b, ATT_KV_HEADS, 128, n), BF16),
        jax.ShapeDtypeStruct((b, 2 * ATT_KV_HEADS, n, 128), BF16),
    )
    out_specs = (
        pl.BlockSpec((2, tm, FNET_WIDTH), lambda bi, i: (0, i, bi)),
        pl.BlockSpec((None, tm, 2 * GLA_QK), lambda bi, i: (bi, i, 0)),
        pl.BlockSpec((None, tm, GLA_WIDTH), lambda bi, i: (bi, i, 0)),
        pl.BlockSpec((None, GLA_WIDTH, tm), lambda bi, i: (bi, 0, i)),
        pl.BlockSpec((None, tm, GLA_WIDTH), lambda bi, i: (bi, i, 0)),
        pl.BlockSpec((None, tm, 2 * GLA_QK), lambda bi, i: (bi, i, 0)),
        pl.BlockSpec((None, tm, ATT_WIDTH), lambda bi, i: (bi, i, 0)),
        pl.BlockSpec((None, ATT_KV_HEADS, 128, tm), lambda bi, i: (bi, 0, 0, i)),
        pl.BlockSpec((None, 2 * ATT_KV_HEADS, tm, 128), lambda bi, i: (bi, 0, i, 0)),
    )
    return pl.pallas_call(
        functools.partial(_inproj_kernel, rope=rope),
        grid=(b, nt),
        in_specs=in_specs,
        out_specs=out_specs,
        out_shape=out_shape,
        compiler_params=_cparams(("parallel", "parallel")),
        name="inproj_rope" if rope else "inproj_ctx",
    )(*args)


def _seqdft_kernel(t_ref, ab_ref, o_ref):
    y = jnp.dot(t_ref[...], ab_ref[...], preferred_element_type=F32)
    for bb in range(o_ref.shape[0]):
        o_ref[bb] = y[:, bb * FNET_WIDTH:(bb + 1) * FNET_WIDTH].astype(BF16)


def _seqdft_call(table, ab, b):
    n = table.shape[0]
    tm = min(512, n)
    nb = 4 if b % 4 == 0 else (2 if b % 2 == 0 else 1)
    return pl.pallas_call(
        _seqdft_kernel,
        grid=(b // nb, n // tm),
        in_specs=[
            pl.BlockSpec((tm, 2 * n), lambda c, i: (i, 0)),
            pl.BlockSpec((2 * n, nb * FNET_WIDTH), lambda c, i: (0, c)),
        ],
        out_specs=pl.BlockSpec((nb, tm, FNET_WIDTH), lambda c, i: (c, i, 0)),
        out_shape=jax.ShapeDtypeStruct((b, n, FNET_WIDTH), BF16),
        compiler_params=_cparams(("parallel", "parallel")),
        name="seq_dft",
    )(table, ab)


def _gla_dir(qk, v, vt, a, s_in, fwd):
    p = GLA_PAIR
    r = lax.broadcasted_iota(jnp.int32, (p, p), 0)
    c = lax.broadcasted_iota(jnp.int32, (p, p), 1)
    same = (r // GLA_CHUNK) == (c // GLA_CHUNK)
    tri = same & ((c <= r) if fwd else (c >= r))
    row_lo = r < GLA_CHUNK
    rin = r % GLA_CHUNK

    q = qk[:, :GLA_QK] * (GLA_DK ** -0.5)
    k = qk[:, GLA_QK:]
    cum = a
    sh = 1
    while sh < GLA_CHUNK:
        if fwd:
            cum = cum + jnp.where(rin >= sh, pltpu.roll(cum, sh, 0), 0.0)
        else:
            cum = cum + jnp.where(rin < GLA_CHUNK - sh, pltpu.roll(cum, p - sh, 0), 0.0)
        sh *= 2
    if fwd:
        last0, last1 = cum[GLA_CHUNK - 1:GLA_CHUNK, :], cum[p - 1:p, :]
    else:
        last0, last1 = cum[0:1, :], cum[GLA_CHUNK:GLA_CHUNK + 1, :]
    lastb = jnp.where(row_lo, last0, last1)
    qt = q * jnp.exp(cum)
    kt = k * jnp.exp(-cum)
    kd = k * jnp.exp(lastb - cum)

    kt_b = kt.astype(BF16)
    zk = jnp.zeros_like(kt_b)
    ks = jnp.concatenate([jnp.where((c // GLA_DK) == hh, kt_b, zk) for hh in range(GLA_HEADS)], axis=0)
    att = _nt_dot(qt.astype(BF16), ks)
    tri4 = jnp.concatenate([tri] * GLA_HEADS, axis=1)
    att = jnp.where(tri4, att, 0.0).astype(BF16)
    col = lax.broadcasted_iota(jnp.int32, (p, GLA_WIDTH), 1)
    zv = jnp.zeros_like(v)
    vs = jnp.concatenate([jnp.where((col // GLA_DV) == hh, v, zv) for hh in range(GLA_HEADS)], axis=0)
    o_intra = jnp.dot(att, vs, preferred_element_type=F32)

    sr = lax.broadcasted_iota(jnp.int32, (GLA_WIDTH, GLA_QK), 0)
    sc = lax.broadcasted_iota(jnp.int32, (GLA_WIDTH, GLA_QK), 1)
    bdm = (sr // GLA_DV) == (sc // GLA_DK)
    first, second = (0, 1) if fwd else (1, 0)
    lasts = (last0, last1)
    in_chunk = (row_lo, jnp.logical_not(row_lo))
    kd2 = jnp.concatenate([jnp.where(in_chunk[0], kd, 0.0), jnp.where(in_chunk[1], kd, 0.0)], axis=1).astype(BF16)
    kvt2 = jnp.dot(vt, kd2, preferred_element_type=F32)
    kvt = (kvt2[:, :GLA_QK], kvt2[:, GLA_QK:])
    s_a = s_in
    s_b = s_a * jnp.exp(lasts[first]) + jnp.where(bdm, kvt[first], 0.0)
    s_c = s_b * jnp.exp(lasts[second]) + jnp.where(bdm, kvt[second], 0.0)
    q2 = jnp.concatenate([jnp.where(in_chunk[first], qt, 0.0), jnp.where(in_chunk[second], qt, 0.0)], axis=1)
    s2 = jnp.concatenate([s_a, s_b], axis=1).astype(BF16)
    o_inter = _nt_dot(q2.astype(BF16), s2)
    return o_intra + o_inter, s_c


def _gla_kernel(qkf, vf, vtf, laf, qkb, vb, vtb, lab, s0_ref, of_ref, ob_ref, sfin_ref, s_scr):
    i = pl.program_id(1)

    @pl.when(i == 0)
    def _():
        s_scr[...] = s0_ref[...]

    for gi in range(qkf.shape[0]):
        o1, sf = _gla_dir(qkf[gi], vf[gi], vtf[gi], laf[gi], s_scr[gi, 0], True)
        o2, sb = _gla_dir(qkb[gi], vb[gi], vtb[gi], lab[gi], s_scr[gi, 1], False)
        of_ref[gi] = o1
        ob_ref[gi] = o2
        s_scr[gi, 0] = sf
        s_scr[gi, 1] = sb

    @pl.when(i == pl.num_programs(1) - 1)
    def _():
        sfin_ref[...] = s_scr[...]


def _gla_call(gqk, gv, gvt, la, s0):
    b, n, _ = gqk.shape
    p = GLA_PAIR
    npair = n // p
    gb = 4 if b % 4 == 0 else (2 if b % 2 == 0 else 1)
    fw = lambda bi, i: (bi, i, 0)
    bw = lambda bi, i: (bi, npair - 1 - i, 0)
    in_specs = [
        pl.BlockSpec((gb, p, 2 * GLA_QK), fw),
        pl.BlockSpec((gb, p, GLA_WIDTH), fw),
        pl.BlockSpec((gb, GLA_WIDTH, p), lambda bi, i: (bi, 0, i)),
        pl.BlockSpec((gb, p, GLA_QK), fw),
        pl.BlockSpec((gb, p, 2 * GLA_QK), bw),
        pl.BlockSpec((gb, p, GLA_WIDTH), bw),
        pl.BlockSpec((gb, GLA_WIDTH, p), lambda bi, i: (bi, 0, npair - 1 - i)),
        pl.BlockSpec((gb, p, GLA_QK), lambda bi, i: (bi, npair - 1 - i, 1)),
        pl.BlockSpec((gb, 2, GLA_WIDTH, GLA_QK), lambda bi, i: (bi, 0, 0, 0)),
    ]
    out_specs = (
        pl.BlockSpec((gb, p, GLA_WIDTH), fw),
        pl.BlockSpec((gb, p, GLA_WIDTH), bw),
        pl.BlockSpec((gb, 2, GLA_WIDTH, GLA_QK), lambda bi, i: (bi, 0, 0, 0)),
    )
    out_shape = (
        jax.ShapeDtypeStruct((b, n, GLA_WIDTH), F32),
        jax.ShapeDtypeStruct((b, n, GLA_WIDTH), F32),
        jax.ShapeDtypeStruct((b, 2, GLA_WIDTH, GLA_QK), F32),
    )
    return pl.pallas_call(
        _gla_kernel,
        grid=(b // gb, npair),
        in_specs=in_specs,
        out_specs=out_specs,
        out_shape=out_shape,
        scratch_shapes=[pltpu.VMEM((gb, 2, GLA_WIDTH, GLA_QK), F32)],
        compiler_params=_cparams(("parallel", "arbitrary")),
        name="gla_scan",
    )(gqk, gv, gvt, la, gqk, gv, gvt, la, s0)


def _attn_kernel(*refs, nparts):
    q_ref = refs[0]
    parts = [(refs[1 + 2 * i], refs[2 + 2 * i]) for i in range(nparts)]
    o_ref = refs[1 + 2 * nparts]
    tq = q_ref.shape[0]
    lane = lax.broadcasted_iota(jnp.int32, (tq, 128), 1)
    lo = lane < ATT_HDIM
    for j in range(ATT_HEADS // 2):
        q128 = q_ref[:, 128 * j:128 * (j + 1)]
        g = (2 * j) // (ATT_HEADS // ATT_KV_HEADS)
        outs = []
        for half in range(2):
            qm = jnp.where(lo if half == 0 else jnp.logical_not(lo), q128, jnp.zeros_like(q128))
            ss = [jnp.dot(qm, kt_ref[g], preferred_element_type=F32) for kt_ref, _ in parts]
            m = functools.reduce(jnp.maximum, [jnp.max(s, axis=-1, keepdims=True) for s in ss])
            ps = [jnp.exp2(s - m).astype(BF16) for s in ss]
            pv = functools.reduce(
                lambda u, w: u + w,
                [jnp.dot(pp, vd_ref[2 * g + half], preferred_element_type=F32) for pp, (_, vd_ref) in zip(ps, parts)])
            den = pv[:, ATT_HDIM:ATT_HDIM + 1] if half == 0 else pv[:, 0:1]
            outs.append(pv / den)
        o_ref[:, 128 * j:128 * (j + 1)] = jnp.where(lo, outs[0], outs[1]).astype(BF16)


def _attn_call(q, kv_parts):
    b, n, _ = q.shape
    tq = min(512, n)
    in_specs = [pl.BlockSpec((None, tq, ATT_WIDTH), lambda bi, i: (bi, i, 0))]
    args = [q]
    for kt, vd in kv_parts:
        m = kt.shape[-1]
        in_specs.append(pl.BlockSpec((None, ATT_KV_HEADS, 128, m), lambda bi, i: (bi, 0, 0, 0)))
        in_specs.append(pl.BlockSpec((None, 2 * ATT_KV_HEADS, m, 128), lambda bi, i: (bi, 0, 0, 0)))
        args += [kt, vd]
    return pl.pallas_call(
        functools.partial(_attn_kernel, nparts=len(kv_parts)),
        grid=(b, n // tq),
        in_specs=in_specs,
        out_specs=pl.BlockSpec((None, tq, ATT_WIDTH), lambda bi, i: (bi, i, 0)),
        out_shape=jax.ShapeDtypeStruct((b, n, ATT_WIDTH), BF16),
        compiler_params=_cparams(("parallel", "parallel")),
        name="gqa_attn",
    )(*args)


def _outproj_kernel(f_ref, of_ref, ob_ref, gg_ref, a_ref, x_ref, mod_ref, w_ref, gn_ref, bd_ref, nf_ref,
                    wrh_ref, wrl_ref, xn_ref, h2_ref, lg_ref):
    o = of_ref[...] + ob_ref[...]
    ms = jnp.dot((o * o).astype(BF16), bd_ref[...], preferred_element_type=F32) * (1.0 / GLA_DV)
    on = o * lax.rsqrt(ms + EPS) * gn_ref[...]
    g = gg_ref[...].astype(F32)
    gl = (on * (g * _sigmoid(g))).astype(BF16)
    ox = (jnp.dot(f_ref[...], w_ref[0:FNET_WIDTH, :], preferred_element_type=F32)
          + jnp.dot(gl, w_ref[FNET_WIDTH:FNET_WIDTH + GLA_WIDTH, :], preferred_element_type=F32)
          + jnp.dot(a_ref[...], w_ref[FNET_WIDTH + GLA_WIDTH:, :], preferred_element_type=F32))
    xn = x_ref[...] + mod_ref[2:3, :] * ox
    xn_ref[...] = xn
    ms2 = jnp.mean(xn * xn, axis=-1, keepdims=True)
    h2 = xn * lax.rsqrt(ms2 + EPS) * nf_ref[...] * (1.0 + mod_ref[4:5, :]) + mod_ref[3:4, :]
    hh = h2.astype(BF16)
    hl = (h2 - hh.astype(F32)).astype(BF16)
    h2_ref[...] = _pack_bf16_pairs(h2)
    lg_ref[...] = _nt_dot(wrh_ref[...], hh) + _nt_dot(wrh_ref[...], hl) + _nt_dot(wrl_ref[...], hh)


def _outproj_call(f, of, ob, gg, a, x, mod_l, mod_row, w_out, gn, bd, nf, wrh, wrl):
    b, n, d = x.shape
    tm = min(512, n)
    nt = n // tm
    if mod_row is None:
        mod_map = lambda bi, i: (bi, 0, 0)
    else:
        mod_map = lambda bi, i: (mod_row, 0, 0)
    const = lambda bi, i: (0, 0)
    tok = lambda w: pl.BlockSpec((None, tm, w), lambda bi, i: (bi, i, 0))
    in_specs = [
        tok(FNET_WIDTH), tok(GLA_WIDTH), tok(GLA_WIDTH), tok(GLA_WIDTH), tok(ATT_WIDTH), tok(d),
        pl.BlockSpec((None, 6, d), mod_map),
        pl.BlockSpec((d, d), const),
        pl.BlockSpec((1, GLA_WIDTH), const),
        pl.BlockSpec((GLA_WIDTH, GLA_WIDTH), const),
        pl.BlockSpec((1, d), const),
        pl.BlockSpec((N_EXPERTS, d), const),
        pl.BlockSpec((N_EXPERTS, d), const),
    ]
    out_specs = (
        tok(d), tok(d // 2),
        pl.BlockSpec((N_EXPERTS, tm), lambda bi, i: (0, bi * nt + i)),
    )
    out_shape = (
        jax.ShapeDtypeStruct((b, n, d), F32),
        jax.ShapeDtypeStruct((b, n, d // 2), jnp.uint32),
        jax.ShapeDtypeStruct((N_EXPERTS, b * n), F32),
    )
    return pl.pallas_call(
        _outproj_kernel,
        grid=(b, nt),
        in_specs=in_specs,
        out_specs=out_specs,
        out_shape=out_shape,
        compiler_params=_cparams(("parallel", "parallel")),
        name="outproj_router",
    )(f, of, ob, gg, a, x, mod_l, w_out, gn, bd, nf, wrh, wrl)


def _route_kernel(b_ref, lg_ref, pos_ref, wt_ref, te_ref, nv_ref, *, tm):
    r = lg_ref.shape[1]
    s = [_sigmoid(lg_ref[e]) for e in range(N_EXPERTS)]
    sel = [s[e] + b_ref[e] for e in range(N_EXPERTS)]
    grp = []
    for g in range(N_GROUPS):
        a, b, c, d = sel[4 * g:4 * g + 4]
        hi1, lo1 = jnp.maximum(a, b), jnp.minimum(a, b)
        hi2, lo2 = jnp.maximum(c, d), jnp.minimum(c, d)
        m1 = jnp.maximum(hi1, hi2)
        m2 = jnp.maximum(jnp.minimum(hi1, hi2), jnp.maximum(lo1, lo2))
        grp.append(m1 + m2)
    one = jnp.ones_like(s[0])
    zero = jnp.zeros_like(s[0])
    msk = []
    for g in range(N_GROUPS):
        isg = one
        for g2 in range(N_GROUPS):
            if g2 < g:
                isg = isg * jnp.where(grp[g] > grp[g2], one, zero)
            elif g2 > g:
                isg = isg * jnp.where(grp[g] >= grp[g2], one, zero)
        for li in range(EXPERTS_PER_GROUP):
            e = 4 * g + li
            rank = zero
            for lj in range(EXPERTS_PER_GROUP):
                ej = 4 * g + lj
                if lj < li:
                    rank = rank + jnp.where(sel[ej] >= sel[e], one, zero)
                elif lj > li:
                    rank = rank + jnp.where(sel[ej] > sel[e], one, zero)
            msk.append(jnp.where(rank < 2.0, isg, zero))
    den = functools.reduce(lambda u, v: u + v, [msk[e] * s[e] for e in range(N_EXPERTS)])

    li_ = lax.broadcasted_iota(jnp.int32, (128, 128), 0)
    lj_ = lax.broadcasted_iota(jnp.int32, (128, 128), 1)
    upper = jnp.where(li_ < lj_, 1.0, 0.0).astype(BF16)
    ri_ = lax.broadcasted_iota(jnp.int32, (r, r), 0)
    rj_ = lax.broadcasted_iota(jnp.int32, (r, r), 1)
    lower = jnp.where(rj_ < ri_, 1.0, 0.0).astype(BF16)
    tile_start = lax.broadcasted_iota(jnp.int32, te_ref.shape, 1).astype(F32) * float(tm)
    te = jnp.zeros(te_ref.shape, F32)
    off = jnp.zeros((1, 1), F32)
    seen = zero
    pos = [zero, zero]
    wts = [zero, zero]
    for e in range(N_EXPERTS):
        mb = msk[e].astype(BF16)
        lane_pre = jnp.dot(mb, upper, preferred_element_type=F32)
        row_pre = jnp.sum(jnp.dot(lower, mb, preferred_element_type=F32), axis=1, keepdims=True)
        cnt = jnp.sum(jnp.sum(msk[e], axis=1, keepdims=True), axis=0, keepdims=True)
        p_e = off + row_pre + lane_pre
        g_e = s[e] / den
        for kk in range(2):
            hit = msk[e] * jnp.where(seen == float(kk), one, zero)
            pos[kk] = pos[kk] + hit * p_e
            wts[kk] = wts[kk] + hit * g_e
        seen = seen + msk[e]
        off = off + jnp.floor((cnt + float(tm - 1)) * (1.0 / tm)) * float(tm)
        te = te + jnp.where(tile_start >= off, 1.0, 0.0)
    for kk in range(2):
        pos_ref[kk] = pos[kk].astype(jnp.int32)
        wt_ref[kk] = wts[kk]
    te_ref[...] = jnp.minimum(te, float(N_EXPERTS - 1)).astype(jnp.int32)
    nv_ref[...] = jnp.broadcast_to(off * (1.0 / tm), nv_ref.shape).astype(jnp.int32)


def _route_call(lgt, b_router, tm):
    n_tok = lgt.shape[1]
    r = n_tok // 128
    assert r * 128 == n_tok and r % 8 == 0 and 2 * n_tok // tm + N_EXPERTS <= 256
    lg3 = lgt.reshape(N_EXPERTS, r, 128)
    full3 = lambda k: pl.BlockSpec((k, r, 128), lambda: (0, 0, 0))
    pos, wts, te, nv = pl.pallas_call(
        functools.partial(_route_kernel, tm=tm),
        in_specs=[pl.BlockSpec(memory_space=pltpu.SMEM), full3(N_EXPERTS)],
        out_specs=(full3(2), full3(2), pl.BlockSpec((1, 256), lambda: (0, 0)), pl.BlockSpec((1, 128), lambda: (0, 0))),
        out_shape=(
            jax.ShapeDtypeStruct((2, r, 128), jnp.int32),
            jax.ShapeDtypeStruct((2, r, 128), F32),
            jax.ShapeDtypeStruct((1, 256), jnp.int32),
            jax.ShapeDtypeStruct((1, 128), jnp.int32),
        ),
        compiler_params=pltpu.CompilerParams(vmem_limit_bytes=VMEM_LIMIT),
        name="route",
    )(b_router, lg3)
    return pos.reshape(2, n_tok), wts.reshape(2, n_tok), te.reshape(256), nv[0, :1]


SC_CORES = 2
SC_SUBCORES = 16
SC_WORKERS = SC_CORES * SC_SUBCORES
SC_CHUNK = 32


def _sc_mesh():
    return plsc.VectorSubcoreMesh(core_axis_name="c", subcore_axis_name="s",
                                  num_cores=SC_CORES, num_subcores=SC_SUBCORES)


def _sc_steps(n_rows):
    per_w = n_rows // SC_WORKERS
    steps = per_w // SC_CHUNK
    assert per_w * SC_WORKERS == n_rows and steps * SC_CHUNK == per_w and steps % 2 == 0, n_rows
    return per_w, steps


def _sc_gather_rows(table, idx):
    p = idx.shape[0]
    d = table.shape[1]
    per_w, steps = _sc_steps(p)
    idx3 = idx.reshape(SC_WORKERS, steps, SC_CHUNK)

    @functools.partial(
        pl.kernel, mesh=_sc_mesh(),
        out_type=jax.ShapeDtypeStruct((p, d), table.dtype),
        scratch_types=[
            pltpu.VMEM((steps, SC_CHUNK), jnp.int32),
            pltpu.VMEM((SC_CHUNK, d), table.dtype),
            pltpu.VMEM((SC_CHUNK, d), table.dtype),
            pltpu.SemaphoreType.DMA, pltpu.SemaphoreType.DMA,
            pltpu.SemaphoreType.DMA, pltpu.SemaphoreType.DMA,
        ],
        name="sc_gather_rows",
    )
    def k(table_hbm, idx_hbm, out_hbm, idx_v, buf0, buf1, g0, g1, w0, w1):
        wid = lax.axis_index("s") * SC_CORES + lax.axis_index("c")
        base = wid * per_w
        pltpu.sync_copy(idx_hbm.at[wid], idx_v)

        def gather(s, buf, sem):
            return pltpu.make_async_copy(table_hbm.at[idx_v.at[s]], buf, sem)

        def write(s, buf, sem):
            return pltpu.make_async_copy(buf, out_hbm.at[pl.ds(base + s * SC_CHUNK, SC_CHUNK)], sem)

        gather(0, buf0, g0).start()

        @pl.loop(0, steps, step=2)
        def _(s):
            gather(s + 1, buf1, g1).start()
            gather(s, buf0, g0).wait()
            write(s, buf0, w0).start()
            write(s, buf0, w0).wait()

            @pl.when(s + 2 < steps)
            def _():
                gather(s + 2, buf0, g0).start()

            gather(s + 1, buf1, g1).wait()
            write(s + 1, buf1, w1).start()
            write(s + 1, buf1, w1).wait()

    return k(table, idx3)


def _sc_dispatch(srcs, poss, p_rows):
    d = srcs[0].shape[1]
    dt = srcs[0].dtype
    plans = [_sc_steps(src.shape[0]) for src in srcs]
    idxs = [pos.reshape(2, SC_WORKERS, st, SC_CHUNK) for pos, (_, st) in zip(poss, plans)]
    nseg = len(srcs)
    scratch = [pltpu.VMEM((2, st, SC_CHUNK), jnp.int32) for _, st in plans]
    scratch += [pltpu.VMEM((SC_CHUNK, d), dt), pltpu.VMEM((SC_CHUNK, d), dt)]
    scratch += [pltpu.SemaphoreType.DMA] * 6

    @functools.partial(
        pl.kernel, mesh=_sc_mesh(),
        out_type=jax.ShapeDtypeStruct((p_rows, d), dt),
        scratch_types=scratch,
        name="sc_dispatch",
    )
    def k(*refs):
        src_hbm = refs[:nseg]
        idx_hbm = refs[nseg:2 * nseg]
        out_hbm = refs[2 * nseg]
        idx_v = refs[2 * nseg + 1:3 * nseg + 1]
        buf0, buf1, r0, r1, a0, a1, b0, b1 = refs[3 * nseg + 1:]
        wid = lax.axis_index("s") * SC_CORES + lax.axis_index("c")
        for seg in range(nseg):
            per_w, steps = plans[seg]
            base = wid * per_w
            for kk in range(2):
                pltpu.sync_copy(idx_hbm[seg].at[kk, wid], idx_v[seg].at[kk])

            def read(s, buf, sem, seg=seg, base=base):
                return pltpu.make_async_copy(src_hbm[seg].at[pl.ds(base + s * SC_CHUNK, SC_CHUNK)], buf, sem)

            def scat(kk, s, buf, sem, seg=seg):
                return pltpu.make_async_copy(buf, out_hbm.at[idx_v[seg].at[kk, s]], sem)

            read(0, buf0, r0).start()

            @pl.loop(0, steps, step=2)
            def _(s, read=read, scat=scat, steps=steps):
                read(s + 1, buf1, r1).start()
                read(s, buf0, r0).wait()
                scat(0, s, buf0, a0).start()
                scat(1, s, buf0, b0).start()
                scat(0, s, buf0, a0).wait()
                scat(1, s, buf0, b0).wait()

                @pl.when(s + 2 < steps)
                def _():
                    read(s + 2, buf0, r0).start()

                read(s + 1, buf1, r1).wait()
                scat(0, s + 1, buf1, a1).start()
                scat(1, s + 1, buf1, b1).start()
                scat(0, s + 1, buf1, a1).wait()
                scat(1, s + 1, buf1, b1).wait()

    return k(*srcs, *idxs)


MOE_TM = 512


def _experts_kernel(te_ref, nv_ref, xs_ref, wg_ref, wu_ref, wd_ref, ys_ref):
    @pl.when(pl.program_id(0) < nv_ref[0])
    def _():
        h = _unpack_bf16_pairs(xs_ref[...]).astype(BF16)
        half = D_EXPERT // 2
        y = None
        for j in range(2):
            sl = slice(j * half, (j + 1) * half)
            a = jnp.dot(h, wg_ref[:, sl].astype(BF16), preferred_element_type=F32)
            u = jnp.dot(h, wu_ref[:, sl].astype(BF16), preferred_element_type=F32)
            t = ((a * _sigmoid(a)) * u).astype(BF16)
            yj = jnp.dot(t, wd_ref[sl, :].astype(BF16), preferred_element_type=F32)
            y = yj if y is None else y + yj
        ys_ref[...] = _pack_bf16_pairs(y)


def _experts_call(xs, te, nv, wg, wu, wd, layer):
    p_rows, dh = xs.shape
    d = 2 * dh
    tm = MOE_TM
    nt = p_rows // tm
    row = lambda i, te_r, nv_r: (jnp.minimum(i, nv_r[0] - 1), 0)
    wsel = lambda i, te_r, nv_r: (layer, te_r[jnp.minimum(i, nv_r[0] - 1)], 0, 0)
    grid_spec = pltpu.PrefetchScalarGridSpec(
        num_scalar_prefetch=2,
        grid=(nt,),
        in_specs=[
            pl.BlockSpec((tm, dh), row),
            pl.BlockSpec((None, None, d, D_EXPERT), wsel),
            pl.BlockSpec((None, None, d, D_EXPERT), wsel),
            pl.BlockSpec((None, None, D_EXPERT, d), wsel),
        ],
        out_specs=pl.BlockSpec((tm, dh), row),
    )
    return pl.pallas_call(
        _experts_kernel,
        grid_spec=grid_spec,
        out_shape=jax.ShapeDtypeStruct((p_rows, dh), jnp.uint32),
        compiler_params=_cparams(("arbitrary",)),
        name="moe_experts",
    )(te, nv, xs, wg, wu, wd)


def _combine_kernel(y_ref, wt_ref, x_ref, mod_ref, fn_ref, o_ref, *, final):
    wt = wt_ref[...]
    y = wt[:, 0:1] * _unpack_bf16_pairs(y_ref[0]) + wt[:, 1:2] * _unpack_bf16_pairs(y_ref[1])
    xo = x_ref[...] + mod_ref[5:6, :] * y
    if final:
        ms = jnp.mean(xo * xo, axis=-1, keepdims=True)
        xo = xo * lax.rsqrt(ms + EPS) * fn_ref[...]
    o_ref[...] = xo


def _combine_call(y2, wts, x, mod_l, mod_row, fn, final):
    b, n, d = x.shape
    tm = min(512, n)
    if mod_row is None:
        mod_map = lambda bi, i: (bi, 0, 0)
    else:
        mod_map = lambda bi, i: (mod_row, 0, 0)
    tok = lambda w: pl.BlockSpec((None, tm, w), lambda bi, i: (bi, i, 0))
    return pl.pallas_call(
        functools.partial(_combine_kernel, final=final),
        grid=(b, n // tm),
        in_specs=[
            pl.BlockSpec((2, None, tm, d // 2), lambda bi, i: (0, bi, i, 0)),
            tok(2), tok(d),
            pl.BlockSpec((None, 6, d), mod_map),
            pl.BlockSpec((1, d), lambda bi, i: (0, 0)),
        ],
        out_specs=tok(d),
        out_shape=jax.ShapeDtypeStruct((b, n, d), F32),
        compiler_params=_cparams(("parallel", "parallel")),
        name="moe_combine",
    )(y2, wts, x, mod_l, fn)


def _moe_sparse(h_list, lg_list, x_list, mod_l, mod_rows, b_router, wg, wu, wd, layer, fn, final):
    d = h_list[0].shape[-1]
    sizes = [h.shape[0] * h.shape[1] for h in h_list]
    n_tok = sum(sizes)
    lgt = lg_list[0] if len(lg_list) == 1 else jnp.concatenate(lg_list, axis=1)
    pos, wts, te, nv = _route_call(lgt, b_router, MOE_TM)
    p_rows = 2 * n_tok + N_EXPERTS * MOE_TM
    offs = np.cumsum([0] + sizes)
    poss = [pos[:, offs[i]:offs[i + 1]] for i in range(len(sizes))]
    xs = _sc_dispatch([h.reshape(-1, d) for h in h_list], poss, p_rows)
    ys = _experts_call(xs, te, nv, wg, wu, wd, layer)
    outs = []
    for i, (h, x) in enumerate(zip(h_list, x_list)):
        bb, nn, _ = x.shape
        y2 = _sc_gather_rows(ys, poss[i].reshape(-1)).reshape(2, bb, nn, d)
        w_i = wts[:, offs[i]:offs[i + 1]].T.reshape(bb, nn, 2)
        outs.append(_combine_call(y2, w_i, x, mod_l, mod_rows[i], fn, final))
    return outs


def _permute_w_in(w):
    o = 0
    u = w[:, o:o + 256]; o += 256
    gq = w[:, o:o + 128]; o += 128
    gk = w[:, o:o + 128]; o += 128
    gv = w[:, o:o + 256]; o += 256
    gg = w[:, o:o + 256]; o += 256
    gdf = w[:, o:o + 16]; o += 16
    gdb = w[:, o:o + 16]; o += 16
    aq = w[:, o:o + 512]; o += 512
    ak = w[:, o:o + 128]; o += 128
    av = w[:, o:o + 128]; o += 128
    k0, k1 = ak[:, :64], ak[:, 64:]
    v0, v1 = av[:, :64], av[:, 64:]
    pad = jnp.zeros((w.shape[0], 96), w.dtype)
    return jnp.concatenate([u, gq, gk, gv, gg, aq, k0, k0, k1, k1, v0, v0, v1, v1, gdf, gdb, pad], axis=1).astype(BF16)


def _gate_up_weights(w_up, b_up):
    z = jnp.zeros((GLA_GATE_RANK, GLA_QK), w_up.dtype)
    top = jnp.concatenate([w_up[0], z], axis=1)
    mid = jnp.concatenate([z, w_up[1]], axis=1)
    pad = jnp.zeros((128 - 2 * GLA_GATE_RANK, 2 * GLA_QK), w_up.dtype)
    return jnp.concatenate([top, mid, pad], axis=0).astype(BF16), b_up.reshape(1, 2 * GLA_QK)


def kernel(x, c, ctx, c_ctx, w_ada, b_ada, norm_mix, norm_ffn, w_in, w_gla_gate_up, b_gla_gate, gla_norm, q_norm,
           k_norm, w_out, w_router, b_router, w_exp_gate, w_exp_up, w_exp_down, final_norm):
    b, n, d = x.shape
    m = ctx.shape[1]
    depth = w_ada.shape[0]
    assert d == D_MODEL and n % GLA_PAIR == 0 and m % GLA_PAIR == 0 and n % GRID_W == 0

    rows = ((b + 1 + 7) // 8) * 8
    cv = jnp.concatenate([c, c_ctx[None, :], jnp.zeros((rows - b - 1, d), F32)], axis=0)
    mod = _ada_call(cv, w_ada, b_ada).reshape(depth, rows, 6, d)

    cs = jnp.asarray(_channel_dft_table()).astype(BF16)
    tab_x = jnp.asarray(_seq_dft_table(n)).astype(BF16)
    tab_c = jnp.asarray(_seq_dft_table(m)).astype(BF16)
    rope_tabs = tuple(jnp.asarray(t) for t in _rope_tables(n))
    bd512 = jnp.asarray(_blockdiag_ones(ATT_WIDTH, ATT_HDIM)).astype(BF16)
    bd256 = jnp.asarray(_blockdiag_ones(GLA_WIDTH, GLA_DV)).astype(BF16)
    wr_t = w_router.T
    wrh = wr_t.astype(BF16)
    wrl = (wr_t - wrh.astype(F32)).astype(BF16)
    fn = final_norm.reshape(1, d)

    xc = ctx
    for l in range(depth):
        ctx_out = l < depth - 1
        mod_l = mod[l]
        w_perm = _permute_w_in(w_in[l])
        wup, bup = _gate_up_weights(w_gla_gate_up[l], b_gla_gate[l])
        nw = norm_mix[l].reshape(1, d)
        nf = norm_ffn[l].reshape(1, d)
        qn = jnp.tile(q_norm[l], ATT_HEADS).reshape(1, ATT_WIDTH)
        kn = jnp.tile(k_norm[l], 4).reshape(1, 256)
        gn = jnp.tile(gla_norm[l], GLA_HEADS).reshape(1, GLA_WIDTH)
        wo = w_out[l].astype(BF16)

        (ab_c, gqk_c, gv_c, gvt_c, gg_c, la_c, q_c, kt_c, vd_c) = _inproj_call(
            xc, mod_l, b, nw, w_perm, cs, wup, bup, qn, kn, bd512, None)
        (ab_x, gqk_x, gv_x, gvt_x, gg_x, la_x, q_x, kt_x, vd_x) = _inproj_call(
            x, mod_l, None, nw, w_perm, cs, wup, bup, qn, kn, bd512, rope_tabs)

        s_zero = jnp.zeros((b, 2, GLA_WIDTH, GLA_QK), F32)
        of_c, ob_c, s_fin = _gla_call(gqk_c, gv_c, gvt_c, la_c, s_zero)
        of_x, ob_x, _ = _gla_call(gqk_x, gv_x, gvt_x, la_x, s_fin)

        f_x = _seqdft_call(tab_x, ab_x.reshape(2 * n, b * FNET_WIDTH), b)
        a_x = _attn_call(q_x, [(kt_c, vd_c), (kt_x, vd_x)])
        x, h2_x, lg_x = _outproj_call(f_x, of_x, ob_x, gg_x, a_x, x, mod_l, None, wo, gn, bd256, nf, wrh, wrl)

        if ctx_out:
            f_c = _seqdft_call(tab_c, ab_c.reshape(2 * m, b * FNET_WIDTH), b)
            a_c = _attn_call(q_c, [(kt_c, vd_c)])
            xc, h2_c, lg_c = _outproj_call(f_c, of_c, ob_c, gg_c, a_c, xc, mod_l, b, wo, gn, bd256, nf, wrh, wrl)

        final = l == depth - 1
        wexp = (w_exp_gate, w_exp_up, w_exp_down, l)
        if ctx_out:
            x, xc = _moe_sparse([h2_x, h2_c], [lg_x, lg_c], [x, xc], mod_l, [None, b], b_router, *wexp, fn, final)
        else:
            (x,) = _moe_sparse([h2_x], [lg_x], [x], mod_l, [None], b_router, *wexp, fn, final)
    return x
```

```python
import functools

import numpy as np
import jax
import jax.numpy as jnp
from jax import lax
from jax.experimental import pallas as pl
from jax.experimental.pallas import tpu as pltpu
from jax.experimental.pallas import tpu_sc as plsc

F32 = jnp.float32
BF16 = jnp.bfloat16

D_MODEL = 1024
GRID_W = 64
EPS = 1e-6
LOG2E = 1.4426950408889634

FNET_WIDTH = 256
FNET_GROUPS = 4
FNET_GDIM = 64

GLA_HEADS = 4
GLA_DV = 64
GLA_DK = 32
GLA_WIDTH = 256
GLA_QK = 128
GLA_GATE_RANK = 16
GLA_GATE_NORM = 16.0
GLA_CHUNK = 64
GLA_PAIR = 2 * GLA_CHUNK

ATT_HEADS = 8
ATT_KV_HEADS = 2
ATT_HDIM = 64
ATT_WIDTH = 512
ROPE_FREQS = 16
ROPE_THETA = 10000.0

N_EXPERTS = 16
N_GROUPS = 4
EXPERTS_PER_GROUP = 4
D_EXPERT = 512

C_U = 0
C_GQ = 256
C_GK = 384
C_GV = 512
C_GG = 768
C_AQ = 1024
C_AK = 1536
C_GD = 1792
W_IN_COLS = 1920
W_IN_REF_COLS = 1824

VMEM_LIMIT = 56 * 1024 * 1024


def _cparams(sem):
    return pltpu.CompilerParams(dimension_semantics=sem, vmem_limit_bytes=VMEM_LIMIT)


def _sigmoid(x):
    return 1.0 / (1.0 + jnp.exp(-x))


def _pack_bf16_pairs(x):
    w = x.shape[1] // 2
    lo = lax.bitcast_convert_type(x[:, :w].astype(BF16).astype(F32), jnp.uint32)
    hi = lax.bitcast_convert_type(x[:, w:].astype(BF16).astype(F32), jnp.uint32)
    return (lo >> 16) | (hi & jnp.uint32(0xFFFF0000))


def _unpack_bf16_pairs(p):
    lo = lax.bitcast_convert_type(p << 16, F32)
    hi = lax.bitcast_convert_type(p & jnp.uint32(0xFFFF0000), F32)
    return jnp.concatenate([lo, hi], axis=1)


def _token_columns(w):
    nr = w.shape[0]
    tm = nr * 128
    lane = lax.broadcasted_iota(jnp.int32, (tm, 128), 1)
    row = lax.broadcasted_iota(jnp.int32, (tm, 128), 0)
    wb = jnp.concatenate([jnp.broadcast_to(w[r:r + 1, :], (128, 128)) for r in range(nr)], axis=0)
    return jnp.sum(jnp.where(lane == (row % 128), wb, 0.0), axis=1, keepdims=True)


def _moe_mix(y_ref, w0, w1):
    return (_token_columns(w0) * _unpack_bf16_pairs(y_ref[0])
            + _token_columns(w1) * _unpack_bf16_pairs(y_ref[1]))


def _nt_dot(a, b):
    return lax.dot_general(a, b, (((1,), (1,)), ((), ())), preferred_element_type=F32)


@functools.lru_cache(maxsize=None)
def _channel_dft_table():
    j = np.arange(FNET_GDIM)
    ang = 2.0 * np.pi * ((j[:, None] * j[None, :]) % FNET_GDIM) / FNET_GDIM
    c = np.cos(ang) / np.sqrt(FNET_GDIM)
    s = np.sin(ang) / np.sqrt(FNET_GDIM)
    out = np.zeros((FNET_WIDTH, 2 * FNET_WIDTH), np.float64)
    for g in range(FNET_GROUPS):
        sl = slice(g * FNET_GDIM, (g + 1) * FNET_GDIM)
        out[sl, sl] = c
        out[sl, FNET_WIDTH + g * FNET_GDIM:FNET_WIDTH + (g + 1) * FNET_GDIM] = s
    return out.astype(np.float32)


@functools.lru_cache(maxsize=None)
def _seq_dft_table(n):
    j = np.arange(n, dtype=np.int64)
    ang = 2.0 * np.pi * ((j[:, None] * j[None, :]) % n) / n
    return np.concatenate([np.cos(ang), -np.sin(ang)], axis=1).astype(np.float32) / np.float32(np.sqrt(n))


@functools.lru_cache(maxsize=None)
def _rope_tables(n):
    rows = n // GRID_W
    row = np.repeat(np.arange(rows), GRID_W).astype(np.float64)
    col = np.tile(np.arange(GRID_W), rows).astype(np.float64)
    inv = ROPE_THETA ** (-np.arange(ROPE_FREQS, dtype=np.float64) * 2.0 / (2 * ROPE_FREQS))
    ar = row[:, None] * inv[None, :]
    ac = col[:, None] * inv[None, :]
    cos = np.concatenate([np.cos(ar), np.cos(ar), np.cos(ac), np.cos(ac)], axis=1)
    sin = np.concatenate([-np.sin(ar), np.sin(ar), -np.sin(ac), np.sin(ac)], axis=1)
    return (np.tile(cos, (1, 2)).astype(np.float32), np.tile(sin, (1, 2)).astype(np.float32))


@functools.lru_cache(maxsize=None)
def _blockdiag_ones(width, blk):
    i = np.arange(width)
    return (i[:, None] // blk == i[None, :] // blk).astype(np.float32)


def _ada_kernel(cv_ref, w_ref, b_ref, o_ref):
    cv = cv_ref[...]
    a = (cv * _sigmoid(cv)).astype(BF16)
    o_ref[...] = jnp.dot(a, w_ref[...].astype(BF16), preferred_element_type=F32) + b_ref[...]


def _ada_call(cv, w_ada, b_ada):
    depth, d, d6 = w_ada.shape
    tn = 1536
    rows = cv.shape[0]
    return pl.pallas_call(
        _ada_kernel,
        grid=(depth, d6 // tn),
        in_specs=[
            pl.BlockSpec((rows, d), lambda l, j: (0, 0)),
            pl.BlockSpec((None, d, tn), lambda l, j: (l, 0, j)),
            pl.BlockSpec((None, 1, tn), lambda l, j: (l, 0, j)),
        ],
        out_specs=pl.BlockSpec((None, rows, tn), lambda l, j: (l, 0, j)),
        out_shape=jax.ShapeDtypeStruct((depth, rows, d6), F32),
        compiler_params=_cparams(("parallel", "parallel")),
        name="ada_mod",
    )(cv, w_ada, b_ada.reshape(depth, 1, d6))


def _swap16(x):
    lane = lax.broadcasted_iota(jnp.int32, x.shape, 1)
    first = (lane % 32) < 16
    return jnp.where(first, pltpu.roll(x, 112, 1), pltpu.roll(x, 16, 1))


def _head_rms(x, bd, w):
    ms = jnp.dot((x * x).astype(BF16), bd, preferred_element_type=F32) * (1.0 / ATT_HDIM)
    return x * lax.rsqrt(ms + EPS) * w


def _inproj_kernel(*refs, rope, pending_rows):
    refs = list(refs)
    x_ref, mod_ref, nw_ref, w_ref, cs_ref, wup_ref, bup_ref, qn_ref, kn_ref, bd_ref = refs[:10]
    del refs[:10]
    if rope:
        cos_ref, sin_ref = refs[:2]
        del refs[:2]
    if pending_rows is not None:
        y_ref, wt_ref, modp_ref = refs[:3]
        del refs[:3]
    ab_ref, gqk_ref, gv_ref, gvt_ref, gg_ref, la_ref, q_ref, kt_ref, vd_ref = refs[:9]
    x = x_ref[...]
    if pending_rows is not None:
        xnew_ref = refs[9]
        row0, rows_per_sample = pending_rows
        nr = x.shape[0] // 128
        r = row0 + pl.program_id(0) * rows_per_sample + pl.program_id(1) * nr
        sub = lax.rem(r, 8)
        w = [wt_ref[kk, 0:nr, :] for kk in range(2)]
        for blk in range(1, 8 // nr):
            w = [jnp.where(sub == blk * nr, wt_ref[kk, blk * nr:(blk + 1) * nr, :], w[kk]) for kk in range(2)]
        x = x + modp_ref[5:6, :] * _moe_mix(y_ref, w[0], w[1])
        xnew_ref[...] = x
    ms = jnp.mean(x * x, axis=-1, keepdims=True)
    y = x * lax.rsqrt(ms + EPS) * nw_ref[...]
    h = y * (1.0 + mod_ref[1:2, :]) + mod_ref[0:1, :]
    hb = h.astype(BF16)

    def proj(c0, width):
        return jnp.dot(hb, w_ref[:, c0:c0 + width], preferred_element_type=F32)

    uab = jnp.dot(proj(C_U, FNET_WIDTH).astype(BF16), cs_ref[...], preferred_element_type=F32)
    ab_ref[0] = uab[:, :FNET_WIDTH].astype(BF16)
    ab_ref[1] = uab[:, FNET_WIDTH:].astype(BF16)

    gqk_ref[...] = proj(C_GQ, 2 * GLA_QK)
    gv = proj(C_GV, GLA_WIDTH)
    gv_ref[...] = gv.astype(BF16)
    gvt_ref[...] = gv.T.astype(BF16)
    gg_ref[...] = proj(C_GG, GLA_WIDTH).astype(BF16)
    pre = jnp.dot(proj(C_GD, 128).astype(BF16), wup_ref[...], preferred_element_type=F32) + bup_ref[...]
    la_ref[...] = (jnp.minimum(pre, 0.0) - jnp.log1p(jnp.exp(-jnp.abs(pre)))) * (1.0 / GLA_GATE_NORM)

    bd = bd_ref[...]
    q = _head_rms(proj(C_AQ, ATT_WIDTH), bd, qn_ref[...])
    kv = proj(C_AK, 256)
    k = _head_rms(kv[:, :128], bd[:128, :128], kn_ref[...])
    if rope:
        cos = cos_ref[...]
        sin = sin_ref[...]
        q = jnp.concatenate(
            [q[:, s:s + 128] * cos + _swap16(q[:, s:s + 128]) * sin for s in range(0, ATT_WIDTH, 128)], axis=1)
        k = k * cos + _swap16(k) * sin
    q_ref[...] = (q * (ATT_HDIM ** -0.5 * LOG2E)).astype(BF16)
    v = kv[:, 128:]
    lo = lax.broadcasted_iota(jnp.int32, k.shape, 1) < ATT_HDIM
    k_sw = pltpu.roll(k, ATT_HDIM, 1)
    v_sw = pltpu.roll(v, ATT_HDIM, 1)
    kt_ref[0] = jnp.where(lo, k, k_sw).T.astype(BF16)
    kt_ref[1] = jnp.where(lo, k_sw, k).T.astype(BF16)
    vd_ref[0] = jnp.where(lo, v, 1.0).astype(BF16)
    vd_ref[1] = jnp.where(lo, 1.0, v_sw).astype(BF16)
    vd_ref[2] = jnp.where(lo, v_sw, 1.0).astype(BF16)
    vd_ref[3] = jnp.where(lo, 1.0, v).astype(BF16)


def _inproj_call(x, mod_l, mod_row, nw, w_perm, cs, wup, bup, qn, kn, bd, rope_tabs, pending=None):
    b, n, d = x.shape
    tm = min(512, n)
    nt = n // tm
    rope = rope_tabs is not None
    if mod_row is None:
        mod_map = lambda bi, i: (bi, 0, 0)
    else:
        mod_map = lambda bi, i: (mod_row, 0, 0)
    const = lambda bi, i: (0, 0)
    in_specs = [
        pl.BlockSpec((None, tm, d), lambda bi, i: (bi, i, 0)),
        pl.BlockSpec((None, 6, d), mod_map),
        pl.BlockSpec((1, d), const),
        pl.BlockSpec((d, W_IN_COLS), const),
        pl.BlockSpec((FNET_WIDTH, 2 * FNET_WIDTH), const),
        pl.BlockSpec((128, 2 * GLA_QK), const),
        pl.BlockSpec((1, 2 * GLA_QK), const),
        pl.BlockSpec((1, ATT_WIDTH), const),
        pl.BlockSpec((1, 128), const),
        pl.BlockSpec((ATT_WIDTH, ATT_WIDTH), const),
    ]
    args = [x, mod_l, nw, w_perm, cs, wup, bup, qn, kn, bd]
    if rope:
        in_specs += [pl.BlockSpec((tm, 128), lambda bi, i: (i, 0)), pl.BlockSpec((tm, 128), lambda bi, i: (i, 0))]
        args += list(rope_tabs)
    pending_rows = None
    if pending is not None:
        y2, wts3, row0, mod_prev = pending
        nr = tm // 128
        rps = n // 128
        assert 8 % nr == 0 and row0 % nr == 0 and rps % nr == 0
        pending_rows = (row0, rps)
        in_specs += [
            pl.BlockSpec((2, None, tm, d // 2), lambda bi, i: (0, bi, i, 0)),
            pl.BlockSpec((2, 8, 128), lambda bi, i: (0, (row0 + bi * rps + i * nr) // 8, 0)),
            pl.BlockSpec((None, 6, d), mod_map),
        ]
        args += [y2.reshape(2, b, n, d // 2), wts3, mod_prev]
    out_shape = (
        jax.ShapeDtypeStruct((2, n, b * FNET_WIDTH), BF16),
        jax.ShapeDtypeStruct((b, n, 2 * GLA_QK), F32),
        jax.ShapeDtypeStruct((b, n, GLA_WIDTH), BF16),
        jax.ShapeDtypeStruct((b, GLA_WIDTH, n), BF16),
        jax.ShapeDtypeStruct((b, n, GLA_WIDTH), BF16),
        jax.ShapeDtypeStruct((b, n, 2 * GLA_QK), F32),
        jax.ShapeDtypeStruct((b, n, ATT_WIDTH), BF16),
        jax.ShapeDtypeStruct((b, ATT_KV_HEADS, 128, n), BF16),
        jax.ShapeDtypeStruct((b, 2 * ATT_KV_HEADS, n, 128), BF16),
    )
    out_specs = (
        pl.BlockSpec((2, tm, FNET_WIDTH), lambda bi, i: (0, i, bi)),
        pl.BlockSpec((None, tm, 2 * GLA_QK), lambda bi, i: (bi, i, 0)),
        pl.BlockSpec((None, tm, GLA_WIDTH), lambda bi, i: (bi, i, 0)),
        pl.BlockSpec((None, GLA_WIDTH, tm), lambda bi, i: (bi, 0, i)),
        pl.BlockSpec((None, tm, GLA_WIDTH), lambda bi, i: (bi, i, 0)),
        pl.BlockSpec((None, tm, 2 * GLA_QK), lambda bi, i: (bi, i, 0)),
        pl.BlockSpec((None, tm, ATT_WIDTH), lambda bi, i: (bi, i, 0)),
        pl.BlockSpec((None, ATT_KV_HEADS, 128, tm), lambda bi, i: (bi, 0, 0, i)),
        pl.BlockSpec((None, 2 * ATT_KV_HEADS, tm, 128), lambda bi, i: (bi, 0, i, 0)),
    )
    if pending is not None:
        out_shape += (jax.ShapeDtypeStruct((b, n, d), F32),)
        out_specs += (pl.BlockSpec((None, tm, d), lambda bi, i: (bi, i, 0)),)
    return pl.pallas_call(
        functools.partial(_inproj_kernel, rope=rope, pending_rows=pending_rows),
        grid=(b, nt),
        in_specs=in_specs,
        out_specs=out_specs,
        out_shape=out_shape,
        compiler_params=_cparams(("parallel", "parallel")),
        name="inproj_rope" if rope else "inproj_ctx",
    )(*args)


def _seqdft_kernel(t_ref, ab_ref, o_ref):
    y = jnp.dot(t_ref[...], ab_ref[...], preferred_element_type=F32)
    for bb in range(o_ref.shape[0]):
        o_ref[bb] = y[:, bb * FNET_WIDTH:(bb + 1) * FNET_WIDTH].astype(BF16)


def _seqdft_call(table, ab, b):
    n = table.shape[0]
    tm = min(512, n)
    nb = 4 if b % 4 == 0 else (2 if b % 2 == 0 else 1)
    return pl.pallas_call(
        _seqdft_kernel,
        grid=(b // nb, n // tm),
        in_specs=[
            pl.BlockSpec((tm, 2 * n), lambda c, i: (i, 0)),
            pl.BlockSpec((2 * n, nb * FNET_WIDTH), lambda c, i: (0, c)),
        ],
        out_specs=pl.BlockSpec((nb, tm, FNET_WIDTH), lambda c, i: (c, i, 0)),
        out_shape=jax.ShapeDtypeStruct((b, n, FNET_WIDTH), BF16),
        compiler_params=_cparams(("parallel", "parallel")),
        name="seq_dft",
    )(table, ab)


def _gla_dir(qk, v, vt, a, s_in, fwd):
    p = GLA_PAIR
    r = lax.broadcasted_iota(jnp.int32, (p, p), 0)
    c = lax.broadcasted_iota(jnp.int32, (p, p), 1)
    same = (r // GLA_CHUNK) == (c // GLA_CHUNK)
    tri = same & ((c <= r) if fwd else (c >= r))
    row_lo = r < GLA_CHUNK
    rin = r % GLA_CHUNK

    q = qk[:, :GLA_QK] * (GLA_DK ** -0.5)
    k = qk[:, GLA_QK:]
    cum = a
    sh = 1
    while sh < GLA_CHUNK:
        if fwd:
            cum = cum + jnp.where(rin >= sh, pltpu.roll(cum, sh, 0), 0.0)
        else:
            cum = cum + jnp.where(rin < GLA_CHUNK - sh, pltpu.roll(cum, p - sh, 0), 0.0)
        sh *= 2
    if fwd:
        last0, last1 = cum[GLA_CHUNK - 1:GLA_CHUNK, :], cum[p - 1:p, :]
    else:
        last0, last1 = cum[0:1, :], cum[GLA_CHUNK:GLA_CHUNK + 1, :]
    lastb = jnp.where(row_lo, last0, last1)
    qt = q * jnp.exp(cum)
    kt = k * jnp.exp(-cum)
    kd = k * jnp.exp(lastb - cum)

    kt_b = kt.astype(BF16)
    zk = jnp.zeros_like(kt_b)
    ks = jnp.concatenate([jnp.where((c // GLA_DK) == hh, kt_b, zk) for hh in range(GLA_HEADS)], axis=0)
    att = _nt_dot(qt.astype(BF16), ks)
    tri4 = jnp.concatenate([tri] * GLA_HEADS, axis=1)
    att = jnp.where(tri4, att, 0.0).astype(BF16)
    col = lax.broadcasted_iota(jnp.int32, (p, GLA_WIDTH), 1)
    zv = jnp.zeros_like(v)
    vs = jnp.concatenate([jnp.where((col // GLA_DV) == hh, v, zv) for hh in range(GLA_HEADS)], axis=0)
    o_intra = jnp.dot(att, vs, preferred_element_type=F32)

    sr = lax.broadcasted_iota(jnp.int32, (GLA_WIDTH, GLA_QK), 0)
    sc = lax.broadcasted_iota(jnp.int32, (GLA_WIDTH, GLA_QK), 1)
    bdm = (sr // GLA_DV) == (sc // GLA_DK)
    first, second = (0, 1) if fwd else (1, 0)
    lasts = (last0, last1)
    in_chunk = (row_lo, jnp.logical_not(row_lo))
    kd2 = jnp.concatenate([jnp.where(in_chunk[0], kd, 0.0), jnp.where(in_chunk[1], kd, 0.0)], axis=1).astype(BF16)
    kvt2 = jnp.dot(vt, kd2, preferred_element_type=F32)
    kvt = (kvt2[:, :GLA_QK], kvt2[:, GLA_QK:])
    s_a = s_in
    s_b = s_a * jnp.exp(lasts[first]) + jnp.where(bdm, kvt[first], 0.0)
    s_c = s_b * jnp.exp(lasts[second]) + jnp.where(bdm, kvt[second], 0.0)
    q2 = jnp.concatenate([jnp.where(in_chunk[first], qt, 0.0), jnp.where(in_chunk[second], qt, 0.0)], axis=1)
    s2 = jnp.concatenate([s_a, s_b], axis=1).astype(BF16)
    o_inter = _nt_dot(q2.astype(BF16), s2)
    return o_intra + o_inter, s_c


def _gla_kernel(qkf, vf, vtf, laf, qkb, vb, vtb, lab, s0_ref, of_ref, ob_ref, sfin_ref, s_scr):
    i = pl.program_id(1)

    @pl.when(i == 0)
    def _():
        s_scr[...] = s0_ref[...]

    for gi in range(qkf.shape[0]):
        o1, sf = _gla_dir(qkf[gi], vf[gi], vtf[gi], laf[gi], s_scr[gi, 0], True)
        o2, sb = _gla_dir(qkb[gi], vb[gi], vtb[gi], lab[gi], s_scr[gi, 1], False)
        of_ref[gi] = o1.astype(of_ref.dtype)
        ob_ref[gi] = o2.astype(ob_ref.dtype)
        s_scr[gi, 0] = sf
        s_scr[gi, 1] = sb

    @pl.when(i == pl.num_programs(1) - 1)
    def _():
        sfin_ref[...] = s_scr[...]


def _gla_call(gqk, gv, gvt, la, s0):
    b, n, _ = gqk.shape
    p = GLA_PAIR
    npair = n // p
    gb = 4 if b % 4 == 0 else (2 if b % 2 == 0 else 1)
    fw = lambda bi, i: (bi, i, 0)
    bw = lambda bi, i: (bi, npair - 1 - i, 0)
    in_specs = [
        pl.BlockSpec((gb, p, 2 * GLA_QK), fw),
        pl.BlockSpec((gb, p, GLA_WIDTH), fw),
        pl.BlockSpec((gb, GLA_WIDTH, p), lambda bi, i: (bi, 0, i)),
        pl.BlockSpec((gb, p, GLA_QK), fw),
        pl.BlockSpec((gb, p, 2 * GLA_QK), bw),
        pl.BlockSpec((gb, p, GLA_WIDTH), bw),
        pl.BlockSpec((gb, GLA_WIDTH, p), lambda bi, i: (bi, 0, npair - 1 - i)),
        pl.BlockSpec((gb, p, GLA_QK), lambda bi, i: (bi, npair - 1 - i, 1)),
        pl.BlockSpec((gb, 2, GLA_WIDTH, GLA_QK), lambda bi, i: (bi, 0, 0, 0)),
    ]
    out_specs = (
        pl.BlockSpec((gb, p, GLA_WIDTH), fw),
        pl.BlockSpec((gb, p, GLA_WIDTH), bw),
        pl.BlockSpec((gb, 2, GLA_WIDTH, GLA_QK), lambda bi, i: (bi, 0, 0, 0)),
    )
    out_shape = (
        jax.ShapeDtypeStruct((b, n, GLA_WIDTH), BF16),
        jax.ShapeDtypeStruct((b, n, GLA_WIDTH), BF16),
        jax.ShapeDtypeStruct((b, 2, GLA_WIDTH, GLA_QK), F32),
    )
    return pl.pallas_call(
        _gla_kernel,
        grid=(b // gb, npair),
        in_specs=in_specs,
        out_specs=out_specs,
        out_shape=out_shape,
        scratch_shapes=[pltpu.VMEM((gb, 2, GLA_WIDTH, GLA_QK), F32)],
        compiler_params=_cparams(("parallel", "arbitrary")),
        name="gla_scan",
    )(gqk, gv, gvt, la, gqk, gv, gvt, la, s0)


def _attn_kernel(*refs, nparts):
    q_ref = refs[0]
    parts = [(refs[1 + 2 * i], refs[2 + 2 * i]) for i in range(nparts)]
    o_ref = refs[1 + 2 * nparts]
    tq = q_ref.shape[0]
    lane = lax.broadcasted_iota(jnp.int32, (tq, 128), 1)
    lo = lane < ATT_HDIM
    for j in range(ATT_HEADS // 2):
        q128 = q_ref[:, 128 * j:128 * (j + 1)]
        g = (2 * j) // (ATT_HEADS // ATT_KV_HEADS)
        outs = []
        for half in range(2):
            qm = jnp.where(lo if half == 0 else jnp.logical_not(lo), q128, jnp.zeros_like(q128))
            ss = [jnp.dot(qm, kt_ref[g], preferred_element_type=F32) for kt_ref, _ in parts]
            m = functools.reduce(jnp.maximum, [jnp.max(s, axis=-1, keepdims=True) for s in ss])
            ps = [jnp.exp2(s - m).astype(BF16) for s in ss]
            pv = functools.reduce(
                lambda u, w: u + w,
                [jnp.dot(pp, vd_ref[2 * g + half], preferred_element_type=F32) for pp, (_, vd_ref) in zip(ps, parts)])
            den = pv[:, ATT_HDIM:ATT_HDIM + 1] if half == 0 else pv[:, 0:1]
            outs.append(pv / den)
        o_ref[:, 128 * j:128 * (j + 1)] = jnp.where(lo, outs[0], outs[1]).astype(BF16)


def _attn_call(q, kv_parts):
    b, n, _ = q.shape
    tq = min(1024, n)
    in_specs = [pl.BlockSpec((None, tq, ATT_WIDTH), lambda bi, i: (bi, i, 0))]
    args = [q]
    for kt, vd in kv_parts:
        m = kt.shape[-1]
        in_specs.append(pl.BlockSpec((None, ATT_KV_HEADS, 128, m), lambda bi, i: (bi, 0, 0, 0)))
        in_specs.append(pl.BlockSpec((None, 2 * ATT_KV_HEADS, m, 128), lambda bi, i: (bi, 0, 0, 0)))
        args += [kt, vd]
    return pl.pallas_call(
        functools.partial(_attn_kernel, nparts=len(kv_parts)),
        grid=(b, n // tq),
        in_specs=in_specs,
        out_specs=pl.BlockSpec((None, tq, ATT_WIDTH), lambda bi, i: (bi, i, 0)),
        out_shape=jax.ShapeDtypeStruct((b, n, ATT_WIDTH), BF16),
        compiler_params=_cparams(("parallel", "parallel")),
        name="gqa_attn",
    )(*args)


def _outproj_kernel(f_ref, of_ref, ob_ref, gg_ref, a_ref, x_ref, mod_ref, w_ref, gn_ref, bd_ref, nf_ref,
                    wr_ref, xn_ref, h2_ref, lg_ref):
    o = of_ref[...].astype(F32) + ob_ref[...].astype(F32)
    ms = jnp.dot((o * o).astype(BF16), bd_ref[...], preferred_element_type=F32) * (1.0 / GLA_DV)
    on = o * lax.rsqrt(ms + EPS) * gn_ref[...]
    g = gg_ref[...].astype(F32)
    gl = (on * (g * _sigmoid(g))).astype(BF16)
    ox = (jnp.dot(f_ref[...], w_ref[0:FNET_WIDTH, :], preferred_element_type=F32)
          + jnp.dot(gl, w_ref[FNET_WIDTH:FNET_WIDTH + GLA_WIDTH, :], preferred_element_type=F32)
          + jnp.dot(a_ref[...], w_ref[FNET_WIDTH + GLA_WIDTH:, :], preferred_element_type=F32))
    xn = x_ref[...] + mod_ref[2:3, :] * ox
    xn_ref[...] = xn
    ms2 = jnp.mean(xn * xn, axis=-1, keepdims=True)
    h2 = xn * lax.rsqrt(ms2 + EPS) * nf_ref[...] * (1.0 + mod_ref[4:5, :]) + mod_ref[3:4, :]
    h2_ref[...] = _pack_bf16_pairs(h2)
    lg2 = _nt_dot(wr_ref[...], h2.astype(BF16))
    lg_ref[...] = lg2[:N_EXPERTS, :] + lg2[N_EXPERTS:, :]


def _outproj_call(f, of, ob, gg, a, x, mod_l, mod_row, w_out, gn, bd, nf, wr):
    b, n, d = x.shape
    tm = min(512, n)
    nt = n // tm
    if mod_row is None:
        mod_map = lambda bi, i: (bi, 0, 0)
    else:
        mod_map = lambda bi, i: (mod_row, 0, 0)
    const = lambda bi, i: (0, 0)
    tok = lambda w: pl.BlockSpec((None, tm, w), lambda bi, i: (bi, i, 0))
    in_specs = [
        tok(FNET_WIDTH), tok(GLA_WIDTH), tok(GLA_WIDTH), tok(GLA_WIDTH), tok(ATT_WIDTH), tok(d),
        pl.BlockSpec((None, 6, d), mod_map),
        pl.BlockSpec((d, d), const),
        pl.BlockSpec((1, GLA_WIDTH), const),
        pl.BlockSpec((GLA_WIDTH, GLA_WIDTH), const),
        pl.BlockSpec((1, d), const),
        pl.BlockSpec((2 * N_EXPERTS, d), const),
    ]
    out_specs = (
        tok(d), tok(d // 2),
        pl.BlockSpec((N_EXPERTS, tm), lambda bi, i: (0, bi * nt + i)),
    )
    out_shape = (
        jax.ShapeDtypeStruct((b, n, d), F32),
        jax.ShapeDtypeStruct((b, n, d // 2), jnp.uint32),
        jax.ShapeDtypeStruct((N_EXPERTS, b * n), F32),
    )
    return pl.pallas_call(
        _outproj_kernel,
        grid=(b, nt),
        in_specs=in_specs,
        out_specs=out_specs,
        out_shape=out_shape,
        compiler_params=_cparams(("parallel", "parallel")),
        name="outproj_router",
    )(f, of, ob, gg, a, x, mod_l, w_out, gn, bd, nf, wr)


def _route_kernel(b_ref, lg_ref, pos_ref, wt_ref, te_ref, nv_ref, *, tm):
    r = lg_ref.shape[1]
    s = [_sigmoid(lg_ref[e]) for e in range(N_EXPERTS)]
    sel = [s[e] + b_ref[e] for e in range(N_EXPERTS)]
    grp = []
    for g in range(N_GROUPS):
        a, b, c, d = sel[4 * g:4 * g + 4]
        hi1, lo1 = jnp.maximum(a, b), jnp.minimum(a, b)
        hi2, lo2 = jnp.maximum(c, d), jnp.minimum(c, d)
        m1 = jnp.maximum(hi1, hi2)
        m2 = jnp.maximum(jnp.minimum(hi1, hi2), jnp.maximum(lo1, lo2))
        grp.append(m1 + m2)
    one = jnp.ones_like(s[0])
    zero = jnp.zeros_like(s[0])
    msk = []
    for g in range(N_GROUPS):
        isg = one
        for g2 in range(N_GROUPS):
            if g2 < g:
                isg = isg * jnp.where(grp[g] > grp[g2], one, zero)
            elif g2 > g:
                isg = isg * jnp.where(grp[g] >= grp[g2], one, zero)
        for li in range(EXPERTS_PER_GROUP):
            e = 4 * g + li
            rank = zero
            for lj in range(EXPERTS_PER_GROUP):
                ej = 4 * g + lj
                if lj < li:
                    rank = rank + jnp.where(sel[ej] >= sel[e], one, zero)
                elif lj > li:
                    rank = rank + jnp.where(sel[ej] > sel[e], one, zero)
            msk.append(jnp.where(rank < 2.0, isg, zero))
    den = functools.reduce(lambda u, v: u + v, [msk[e] * s[e] for e in range(N_EXPERTS)])

    li_ = lax.broadcasted_iota(jnp.int32, (128, 128), 0)
    lj_ = lax.broadcasted_iota(jnp.int32, (128, 128), 1)
    upper = jnp.where(li_ < lj_, 1.0, 0.0).astype(BF16)
    ri_ = lax.broadcasted_iota(jnp.int32, (r, r), 0)
    rj_ = lax.broadcasted_iota(jnp.int32, (r, r), 1)
    lower = jnp.where(rj_ < ri_, 1.0, 0.0).astype(BF16)
    tile_start = lax.broadcasted_iota(jnp.int32, te_ref.shape, 1).astype(F32) * float(tm)
    te = jnp.zeros(te_ref.shape, F32)
    off = jnp.zeros((1, 1), F32)
    seen = zero
    pos = [zero, zero]
    wts = [zero, zero]
    for e in range(N_EXPERTS):
        mb = msk[e].astype(BF16)
        lane_pre = jnp.dot(mb, upper, preferred_element_type=F32)
        row_pre = jnp.sum(jnp.dot(lower, mb, preferred_element_type=F32), axis=1, keepdims=True)
        cnt = jnp.sum(jnp.sum(msk[e], axis=1, keepdims=True), axis=0, keepdims=True)
        p_e = off + row_pre + lane_pre
        g_e = s[e] / den
        for kk in range(2):
            hit = msk[e] * jnp.where(seen == float(kk), one, zero)
            pos[kk] = pos[kk] + hit * p_e
            wts[kk] = wts[kk] + hit * g_e
        seen = seen + msk[e]
        off = off + jnp.floor((cnt + float(tm - 1)) * (1.0 / tm)) * float(tm)
        te = te + jnp.where(tile_start >= off, 1.0, 0.0)
    for kk in range(2):
        pos_ref[kk] = pos[kk].astype(jnp.int32)
        wt_ref[kk] = wts[kk]
    te_ref[...] = jnp.minimum(te, float(N_EXPERTS - 1)).astype(jnp.int32)
    nv_ref[...] = jnp.broadcast_to(off * (1.0 / tm), nv_ref.shape).astype(jnp.int32)


def _route_call(lgt, b_router, tm):
    n_tok = lgt.shape[1]
    r = n_tok // 128
    assert r * 128 == n_tok and r % 8 == 0 and 2 * n_tok // tm + N_EXPERTS <= 256
    lg3 = lgt.reshape(N_EXPERTS, r, 128)
    full3 = lambda k: pl.BlockSpec((k, r, 128), lambda: (0, 0, 0))
    pos, wts, te, nv = pl.pallas_call(
        functools.partial(_route_kernel, tm=tm),
        in_specs=[pl.BlockSpec(memory_space=pltpu.SMEM), full3(N_EXPERTS)],
        out_specs=(full3(2), full3(2), pl.BlockSpec((1, 256), lambda: (0, 0)), pl.BlockSpec((1, 128), lambda: (0, 0))),
        out_shape=(
            jax.ShapeDtypeStruct((2, r, 128), jnp.int32),
            jax.ShapeDtypeStruct((2, r, 128), F32),
            jax.ShapeDtypeStruct((1, 256), jnp.int32),
            jax.ShapeDtypeStruct((1, 128), jnp.int32),
        ),
        compiler_params=pltpu.CompilerParams(vmem_limit_bytes=VMEM_LIMIT),
        name="route",
    )(b_router, lg3)
    return pos.reshape(2, n_tok), wts, te.reshape(256), nv[0, :1]


SC_CORES = 2
SC_SUBCORES = 16
SC_WORKERS = SC_CORES * SC_SUBCORES
SC_CHUNK = 32


def _sc_mesh():
    return plsc.VectorSubcoreMesh(core_axis_name="c", subcore_axis_name="s",
                                  num_cores=SC_CORES, num_subcores=SC_SUBCORES)


def _sc_steps(n_rows):
    per_w = n_rows // SC_WORKERS
    steps = per_w // SC_CHUNK
    assert per_w * SC_WORKERS == n_rows and steps * SC_CHUNK == per_w and steps % 2 == 0, n_rows
    return per_w, steps


def _sc_gather_rows(table, idx):
    p = idx.shape[0]
    d = table.shape[1]
    per_w, steps = _sc_steps(p)
    idx3 = idx.reshape(SC_WORKERS, steps, SC_CHUNK)

    @functools.partial(
        pl.kernel, mesh=_sc_mesh(),
        out_type=jax.ShapeDtypeStruct((p, d), table.dtype),
        scratch_types=[
            pltpu.VMEM((steps, SC_CHUNK), jnp.int32),
            pltpu.VMEM((SC_CHUNK, d), table.dtype),
            pltpu.VMEM((SC_CHUNK, d), table.dtype),
            pltpu.SemaphoreType.DMA, pltpu.SemaphoreType.DMA,
            pltpu.SemaphoreType.DMA, pltpu.SemaphoreType.DMA,
        ],
        name="sc_gather_rows",
    )
    def k(table_hbm, idx_hbm, out_hbm, idx_v, buf0, buf1, g0, g1, w0, w1):
        wid = lax.axis_index("s") * SC_CORES + lax.axis_index("c")
        base = wid * per_w
        pltpu.sync_copy(idx_hbm.at[wid], idx_v)

        def gather(s, buf, sem):
            return pltpu.make_async_copy(table_hbm.at[idx_v.at[s]], buf, sem)

        def write(s, buf, sem):
            return pltpu.make_async_copy(buf, out_hbm.at[pl.ds(base + s * SC_CHUNK, SC_CHUNK)], sem)

        gather(0, buf0, g0).start()

        @pl.loop(0, steps, step=2)
        def _(s):
            gather(s + 1, buf1, g1).start()
            gather(s, buf0, g0).wait()
            write(s, buf0, w0).start()
            write(s, buf0, w0).wait()

            @pl.when(s + 2 < steps)
            def _():
                gather(s + 2, buf0, g0).start()

            gather(s + 1, buf1, g1).wait()
            write(s + 1, buf1, w1).start()
            write(s + 1, buf1, w1).wait()

    return k(table, idx3)


def _sc_dispatch(srcs, poss, p_rows):
    d = srcs[0].shape[1]
    dt = srcs[0].dtype
    plans = [_sc_steps(src.shape[0]) for src in srcs]
    idxs = [pos.reshape(2, SC_WORKERS, st, SC_CHUNK) for pos, (_, st) in zip(poss, plans)]
    nseg = len(srcs)
    scratch = [pltpu.VMEM((2, st, SC_CHUNK), jnp.int32) for _, st in plans]
    scratch += [pltpu.VMEM((SC_CHUNK, d), dt), pltpu.VMEM((SC_CHUNK, d), dt)]
    scratch += [pltpu.SemaphoreType.DMA] * 6

    @functools.partial(
        pl.kernel, mesh=_sc_mesh(),
        out_type=jax.ShapeDtypeStruct((p_rows, d), dt),
        scratch_types=scratch,
        name="sc_dispatch",
    )
    def k(*refs):
        src_hbm = refs[:nseg]
        idx_hbm = refs[nseg:2 * nseg]
        out_hbm = refs[2 * nseg]
        idx_v = refs[2 * nseg + 1:3 * nseg + 1]
        buf0, buf1, r0, r1, a0, a1, b0, b1 = refs[3 * nseg + 1:]
        wid = lax.axis_index("s") * SC_CORES + lax.axis_index("c")
        for seg in range(nseg):
            per_w, steps = plans[seg]
            base = wid * per_w
            for kk in range(2):
                pltpu.sync_copy(idx_hbm[seg].at[kk, wid], idx_v[seg].at[kk])

            def read(s, buf, sem, seg=seg, base=base):
                return pltpu.make_async_copy(src_hbm[seg].at[pl.ds(base + s * SC_CHUNK, SC_CHUNK)], buf, sem)

            def scat(kk, s, buf, sem, seg=seg):
                return pltpu.make_async_copy(buf, out_hbm.at[idx_v[seg].at[kk, s]], sem)

            read(0, buf0, r0).start()

            @pl.loop(0, steps, step=2)
            def _(s, read=read, scat=scat, steps=steps):
                read(s + 1, buf1, r1).start()
                read(s, buf0, r0).wait()
                scat(0, s, buf0, a0).start()
                scat(1, s, buf0, b0).start()
                scat(0, s, buf0, a0).wait()
                scat(1, s, buf0, b0).wait()

                @pl.when(s + 2 < steps)
                def _():
                    read(s + 2, buf0, r0).start()

                read(s + 1, buf1, r1).wait()
                scat(0, s + 1, buf1, a1).start()
                scat(1, s + 1, buf1, b1).start()
                scat(0, s + 1, buf1, a1).wait()
                scat(1, s + 1, buf1, b1).wait()

    return k(*srcs, *idxs)


MOE_TM = 512


def _experts_kernel(te_ref, nv_ref, xs_ref, wg_ref, wu_ref, wd_ref, ys_ref):
    @pl.when(pl.program_id(0) < nv_ref[0])
    def _():
        h = _unpack_bf16_pairs(xs_ref[...]).astype(BF16)
        half = D_EXPERT // 2
        y = None
        for j in range(2):
            sl = slice(j * half, (j + 1) * half)
            a = jnp.dot(h, wg_ref[:, sl].astype(BF16), preferred_element_type=F32)
            u = jnp.dot(h, wu_ref[:, sl].astype(BF16), preferred_element_type=F32)
            t = ((a * _sigmoid(a)) * u).astype(BF16)
            yj = jnp.dot(t, wd_ref[sl, :].astype(BF16), preferred_element_type=F32)
            y = yj if y is None else y + yj
        ys_ref[...] = _pack_bf16_pairs(y)


def _experts_call(xs, te, nv, wg, wu, wd, layer):
    p_rows, dh = xs.shape
    d = 2 * dh
    tm = MOE_TM
    nt = p_rows // tm
    row = lambda i, te_r, nv_r: (jnp.minimum(i, nv_r[0] - 1), 0)
    wsel = lambda i, te_r, nv_r: (layer, te_r[jnp.minimum(i, nv_r[0] - 1)], 0, 0)
    grid_spec = pltpu.PrefetchScalarGridSpec(
        num_scalar_prefetch=2,
        grid=(nt,),
        in_specs=[
            pl.BlockSpec((tm, dh), row),
            pl.BlockSpec((None, None, d, D_EXPERT), wsel),
            pl.BlockSpec((None, None, d, D_EXPERT), wsel),
            pl.BlockSpec((None, None, D_EXPERT, d), wsel),
        ],
        out_specs=pl.BlockSpec((tm, dh), row),
    )
    return pl.pallas_call(
        _experts_kernel,
        grid_spec=grid_spec,
        out_shape=jax.ShapeDtypeStruct((p_rows, dh), jnp.uint32),
        compiler_params=_cparams(("arbitrary",)),
        name="moe_experts",
    )(te, nv, xs, wg, wu, wd)


COMBINE_TM = 1024


def _combine_kernel(y_ref, wt_ref, x_ref, mod_ref, fn_ref, o_ref, *, final):
    xo = x_ref[...] + mod_ref[5:6, :] * _moe_mix(y_ref, wt_ref[0], wt_ref[1])
    if final:
        ms = jnp.mean(xo * xo, axis=-1, keepdims=True)
        xo = xo * lax.rsqrt(ms + EPS) * fn_ref[...]
    o_ref[...] = xo


def _combine_call(y2, wts3, row0, x, mod_l, mod_row, fn, final):
    b, n, d = x.shape
    n_tok = b * n
    tm = COMBINE_TM
    assert n_tok % tm == 0 and row0 % 8 == 0 and (mod_row is not None or n % tm == 0)
    if mod_row is None:
        mod_map = lambda i: ((i * tm) // n, 0, 0)
    else:
        mod_map = lambda i: (mod_row, 0, 0)
    out = pl.pallas_call(
        functools.partial(_combine_kernel, final=final),
        grid=(n_tok // tm,),
        in_specs=[
            pl.BlockSpec((2, tm, d // 2), lambda i: (0, i, 0)),
            pl.BlockSpec((2, tm // 128, 128), lambda i: (0, row0 // 8 + i, 0)),
            pl.BlockSpec((tm, d), lambda i: (i, 0)),
            pl.BlockSpec((None, 6, d), mod_map),
            pl.BlockSpec((1, d), lambda i: (0, 0)),
        ],
        out_specs=pl.BlockSpec((tm, d), lambda i: (i, 0)),
        out_shape=jax.ShapeDtypeStruct((n_tok, d), F32),
        compiler_params=_cparams(("parallel",)),
        name="moe_combine",
    )(y2, wts3, x.reshape(n_tok, d), mod_l, fn)
    return out.reshape(b, n, d)


def _moe_sparse(h_list, lg_list, x_list, mod_l, mod_rows, b_router, wg, wu, wd, layer, fn, final):
    d = h_list[0].shape[-1]
    sizes = [h.shape[0] * h.shape[1] for h in h_list]
    n_tok = sum(sizes)
    lgt = lg_list[0] if len(lg_list) == 1 else jnp.concatenate(lg_list, axis=1)
    pos, wts, te, nv = _route_call(lgt, b_router, MOE_TM)
    p_rows = 2 * n_tok + N_EXPERTS * MOE_TM
    offs = np.cumsum([0] + sizes)
    poss = [pos[:, offs[i]:offs[i + 1]] for i in range(len(sizes))]
    xs = _sc_dispatch([h.reshape(-1, d) for h in h_list], poss, p_rows)
    ys = _experts_call(xs, te, nv, wg, wu, wd, layer)
    outs = []
    for i, x in enumerate(x_list):
        y2 = _sc_gather_rows(ys, poss[i].reshape(-1)).reshape(2, sizes[i], d)
        row0 = int(offs[i]) // 128
        if final:
            outs.append(_combine_call(y2, wts, row0, x, mod_l, mod_rows[i], fn, True))
        else:
            outs.append((y2, wts, row0, mod_l))
    return outs


def _winprep_kernel(w_ref, o_ref):
    gd0 = C_AQ
    gdw = 2 * GLA_GATE_RANK
    tail = W_IN_REF_COLS - gd0 - gdw
    o_ref[:, 0:gd0] = w_ref[:, 0:gd0].astype(BF16)
    o_ref[:, gd0:gd0 + tail] = w_ref[:, gd0 + gdw:W_IN_REF_COLS].astype(BF16)
    o_ref[:, C_GD:C_GD + gdw] = w_ref[:, gd0:gd0 + gdw].astype(BF16)
    o_ref[:, C_GD + gdw:] = jnp.zeros((o_ref.shape[0], W_IN_COLS - C_GD - gdw), BF16)


def _winprep_call(w_in):
    depth, d, cols = w_in.shape
    assert cols == W_IN_REF_COLS and C_GD == C_AQ + cols - C_AQ - 2 * GLA_GATE_RANK
    return pl.pallas_call(
        _winprep_kernel,
        grid=(depth,),
        in_specs=[pl.BlockSpec((None, d, cols), lambda l: (l, 0, 0))],
        out_specs=pl.BlockSpec((None, d, W_IN_COLS), lambda l: (l, 0, 0)),
        out_shape=jax.ShapeDtypeStruct((depth, d, W_IN_COLS), BF16),
        compiler_params=_cparams(("parallel",)),
        name="w_in_prep",
    )(w_in)


def _gate_up_weights(w_up, b_up):
    z = jnp.zeros((GLA_GATE_RANK, GLA_QK), w_up.dtype)
    top = jnp.concatenate([w_up[0], z], axis=1)
    mid = jnp.concatenate([z, w_up[1]], axis=1)
    pad = jnp.zeros((128 - 2 * GLA_GATE_RANK, 2 * GLA_QK), w_up.dtype)
    return jnp.concatenate([top, mid, pad], axis=0).astype(BF16), b_up.reshape(1, 2 * GLA_QK)


def kernel(x, c, ctx, c_ctx, w_ada, b_ada, norm_mix, norm_ffn, w_in, w_gla_gate_up, b_gla_gate, gla_norm, q_norm,
           k_norm, w_out, w_router, b_router, w_exp_gate, w_exp_up, w_exp_down, final_norm):
    b, n, d = x.shape
    m = ctx.shape[1]
    depth = w_ada.shape[0]
    assert d == D_MODEL and n % GLA_PAIR == 0 and m % GLA_PAIR == 0 and n % GRID_W == 0

    rows = ((b + 1 + 7) // 8) * 8
    cv = jnp.concatenate([c, c_ctx[None, :], jnp.zeros((rows - b - 1, d), F32)], axis=0)
    mod = _ada_call(cv, w_ada, b_ada).reshape(depth, rows, 6, d)

    cs = jnp.asarray(_channel_dft_table()).astype(BF16)
    tab_x = jnp.asarray(_seq_dft_table(n)).astype(BF16)
    tab_c = jnp.asarray(_seq_dft_table(m)).astype(BF16)
    rope_tabs = tuple(jnp.asarray(t) for t in _rope_tables(n))
    bd512 = jnp.asarray(_blockdiag_ones(ATT_WIDTH, ATT_HDIM)).astype(BF16)
    bd256 = jnp.asarray(_blockdiag_ones(GLA_WIDTH, GLA_DV)).astype(BF16)
    wr_t = w_router.T
    wr_hi = wr_t.astype(BF16)
    wrh = jnp.concatenate([wr_hi, (wr_t - wr_hi.astype(F32)).astype(BF16)], axis=0)
    fn = final_norm.reshape(1, d)

    w_in_perm = _winprep_call(w_in)

    xc = ctx
    pend_x = pend_c = None
    for l in range(depth):
        ctx_out = l < depth - 1
        mod_l = mod[l]
        w_perm = w_in_perm[l]
        wup, bup = _gate_up_weights(w_gla_gate_up[l], b_gla_gate[l])
        nw = norm_mix[l].reshape(1, d)
        nf = norm_ffn[l].reshape(1, d)
        qn = jnp.tile(q_norm[l], ATT_HEADS).reshape(1, ATT_WIDTH)
        kn = jnp.tile(k_norm[l], ATT_KV_HEADS).reshape(1, 128)
        gn = jnp.tile(gla_norm[l], GLA_HEADS).reshape(1, GLA_WIDTH)
        wo = w_out[l].astype(BF16)

        outs_c = _inproj_call(xc, mod_l, b, nw, w_perm, cs, wup, bup, qn, kn, bd512, None, pend_c)
        outs_x = _inproj_call(x, mod_l, None, nw, w_perm, cs, wup, bup, qn, kn, bd512, rope_tabs, pend_x)
        (ab_c, gqk_c, gv_c, gvt_c, gg_c, la_c, q_c, kt_c, vd_c) = outs_c[:9]
        (ab_x, gqk_x, gv_x, gvt_x, gg_x, la_x, q_x, kt_x, vd_x) = outs_x[:9]
        if pend_x is not None:
            x, xc = outs_x[9], outs_c[9]

        s_zero = jnp.zeros((b, 2, GLA_WIDTH, GLA_QK), F32)
        of_c, ob_c, s_fin = _gla_call(gqk_c, gv_c, gvt_c, la_c, s_zero)
        of_x, ob_x, _ = _gla_call(gqk_x, gv_x, gvt_x, la_x, s_fin)

        f_x = _seqdft_call(tab_x, ab_x.reshape(2 * n, b * FNET_WIDTH), b)
        a_x = _attn_call(q_x, [(kt_c, vd_c), (kt_x, vd_x)])
        x, h2_x, lg_x = _outproj_call(f_x, of_x, ob_x, gg_x, a_x, x, mod_l, None, wo, gn, bd256, nf, wrh)

        if ctx_out:
            f_c = _seqdft_call(tab_c, ab_c.reshape(2 * m, b * FNET_WIDTH), b)
            a_c = _attn_call(q_c, [(kt_c, vd_c)])
            xc, h2_c, lg_c = _outproj_call(f_c, of_c, ob_c, gg_c, a_c, xc, mod_l, b, wo, gn, bd256, nf, wrh)

        final = l == depth - 1
        wexp = (w_exp_gate, w_exp_up, w_exp_down, l)
        if ctx_out:
            res = _moe_sparse([h2_x, h2_c], [lg_x, lg_c], [x, xc], mod_l, [None, b], b_router, *wexp, fn, final)
        else:
            res = _moe_sparse([h2_x], [lg_x], [x], mod_l, [None], b_router, *wexp, fn, final)
        if final:
            x = res[0]
        else:
            pend_x, pend_c = res
    return x
```

```python
import functools

import numpy as np
import jax
import jax.numpy as jnp
from jax import lax
from jax.experimental import pallas as pl
from jax.experimental.pallas import tpu as pltpu
from jax.experimental.pallas import tpu_sc as plsc

F32 = jnp.float32
BF16 = jnp.bfloat16

D_MODEL = 1024
GRID_W = 64
EPS = 1e-6
LOG2E = 1.4426950408889634

FNET_WIDTH = 256
FNET_GROUPS = 4
FNET_GDIM = 64

GLA_HEADS = 4
GLA_DV = 64
GLA_DK = 32
GLA_WIDTH = 256
GLA_QK = 128
GLA_GATE_RANK = 16
GLA_GATE_NORM = 16.0
GLA_CHUNK = 64
GLA_PAIR = 2 * GLA_CHUNK

ATT_HEADS = 8
ATT_KV_HEADS = 2
ATT_HDIM = 64
ATT_WIDTH = 512
ROPE_FREQS = 16
ROPE_THETA = 10000.0

N_EXPERTS = 16
N_GROUPS = 4
EXPERTS_PER_GROUP = 4
D_EXPERT = 512

C_U = 0
C_GQ = 256
C_GK = 384
C_GV = 512
C_GG = 768
C_AQ = 1024
C_AK = 1536
C_GD = 1792
W_IN_COLS = 1920
W_IN_REF_COLS = 1824

VMEM_LIMIT = 56 * 1024 * 1024


def _cparams(sem):
    return pltpu.CompilerParams(dimension_semantics=sem, vmem_limit_bytes=VMEM_LIMIT)


def _sigmoid(x):
    return 1.0 / (1.0 + jnp.exp(-x))


def _pack_bf16_pairs(x):
    w = x.shape[1] // 2
    lo = lax.bitcast_convert_type(x[:, :w].astype(BF16).astype(F32), jnp.uint32)
    hi = lax.bitcast_convert_type(x[:, w:].astype(BF16).astype(F32), jnp.uint32)
    return (lo >> 16) | (hi & jnp.uint32(0xFFFF0000))


def _unpack_bf16_pairs(p):
    lo = lax.bitcast_convert_type(p << 16, F32)
    hi = lax.bitcast_convert_type(p & jnp.uint32(0xFFFF0000), F32)
    return jnp.concatenate([lo, hi], axis=1)


def _token_columns(w):
    nr = w.shape[0]
    tm = nr * 128
    lane = lax.broadcasted_iota(jnp.int32, (tm, 128), 1)
    row = lax.broadcasted_iota(jnp.int32, (tm, 128), 0)
    wb = jnp.concatenate([jnp.broadcast_to(w[r:r + 1, :], (128, 128)) for r in range(nr)], axis=0)
    return jnp.sum(jnp.where(lane == (row % 128), wb, 0.0), axis=1, keepdims=True)


def _moe_mix(y_ref, w0, w1):
    return (_token_columns(w0) * _unpack_bf16_pairs(y_ref[0])
            + _token_columns(w1) * _unpack_bf16_pairs(y_ref[1]))


def _nt_dot(a, b):
    return lax.dot_general(a, b, (((1,), (1,)), ((), ())), preferred_element_type=F32)


@functools.lru_cache(maxsize=None)
def _channel_dft_table():
    j = np.arange(FNET_GDIM)
    ang = 2.0 * np.pi * ((j[:, None] * j[None, :]) % FNET_GDIM) / FNET_GDIM
    c = np.cos(ang) / np.sqrt(FNET_GDIM)
    s = np.sin(ang) / np.sqrt(FNET_GDIM)
    out = np.zeros((FNET_WIDTH, 2 * FNET_WIDTH), np.float64)
    for g in range(FNET_GROUPS):
        sl = slice(g * FNET_GDIM, (g + 1) * FNET_GDIM)
        out[sl, sl] = c
        out[sl, FNET_WIDTH + g * FNET_GDIM:FNET_WIDTH + (g + 1) * FNET_GDIM] = s
    return out.astype(np.float32)


@functools.lru_cache(maxsize=None)
def _seq_dft_table(n):
    j = np.arange(n, dtype=np.int64)
    ang = 2.0 * np.pi * ((j[:, None] * j[None, :]) % n) / n
    return np.concatenate([np.cos(ang), -np.sin(ang)], axis=1).astype(np.float32) / np.float32(np.sqrt(n))


@functools.lru_cache(maxsize=None)
def _rope_tables(n):
    rows = n // GRID_W
    row = np.repeat(np.arange(rows), GRID_W).astype(np.float64)
    col = np.tile(np.arange(GRID_W), rows).astype(np.float64)
    inv = ROPE_THETA ** (-np.arange(ROPE_FREQS, dtype=np.float64) * 2.0 / (2 * ROPE_FREQS))
    ar = row[:, None] * inv[None, :]
    ac = col[:, None] * inv[None, :]
    cos = np.concatenate([np.cos(ar), np.cos(ar), np.cos(ac), np.cos(ac)], axis=1)
    sin = np.concatenate([-np.sin(ar), np.sin(ar), -np.sin(ac), np.sin(ac)], axis=1)
    return (np.tile(cos, (1, 2)).astype(np.float32), np.tile(sin, (1, 2)).astype(np.float32))


@functools.lru_cache(maxsize=None)
def _blockdiag_ones(width, blk):
    i = np.arange(width)
    return (i[:, None] // blk == i[None, :] // blk).astype(np.float32)


def _ada_kernel(cv_ref, w_ref, b_ref, o_ref):
    cv = cv_ref[...]
    a = (cv * _sigmoid(cv)).astype(BF16)
    o_ref[...] = jnp.dot(a, w_ref[...].astype(BF16), preferred_element_type=F32) + b_ref[...]


def _ada_call(cv, w_ada, b_ada):
    depth, d, d6 = w_ada.shape
    tn = 1536
    rows = cv.shape[0]
    return pl.pallas_call(
        _ada_kernel,
        grid=(depth, d6 // tn),
        in_specs=[
            pl.BlockSpec((rows, d), lambda l, j: (0, 0)),
            pl.BlockSpec((None, d, tn), lambda l, j: (l, 0, j)),
            pl.BlockSpec((None, 1, tn), lambda l, j: (l, 0, j)),
        ],
        out_specs=pl.BlockSpec((None, rows, tn), lambda l, j: (l, 0, j)),
        out_shape=jax.ShapeDtypeStruct((depth, rows, d6), F32),
        compiler_params=_cparams(("parallel", "parallel")),
        name="ada_mod",
    )(cv, w_ada, b_ada.reshape(depth, 1, d6))


def _swap16(x):
    lane = lax.broadcasted_iota(jnp.int32, x.shape, 1)
    first = (lane % 32) < 16
    return jnp.where(first, pltpu.roll(x, 112, 1), pltpu.roll(x, 16, 1))


def _head_rms(x, bd, w):
    ms = jnp.dot((x * x).astype(BF16), bd, preferred_element_type=F32) * (1.0 / ATT_HDIM)
    return x * lax.rsqrt(ms + EPS) * w


def _inproj_kernel(*refs, rope, pending_rows):
    refs = list(refs)
    x_ref, mod_ref, nw_ref, w_ref, cs_ref, wup_ref, bup_ref, qn_ref, kn_ref, bd_ref = refs[:10]
    del refs[:10]
    if rope:
        cos_ref, sin_ref = refs[:2]
        del refs[:2]
    if pending_rows is not None:
        y_ref, wt_ref, modp_ref = refs[:3]
        del refs[:3]
    ab_ref, gqk_ref, gv_ref, gvt_ref, gg_ref, la_ref, q_ref, kt_ref, vd_ref = refs[:9]
    x = x_ref[...]
    if pending_rows is not None:
        xnew_ref = refs[9]
        row0, rows_per_sample = pending_rows
        nr = x.shape[0] // 128
        r = row0 + pl.program_id(0) * rows_per_sample + pl.program_id(1) * nr
        sub = lax.rem(r, 8)
        w = [wt_ref[kk, 0:nr, :] for kk in range(2)]
        for blk in range(1, 8 // nr):
            w = [jnp.where(sub == blk * nr, wt_ref[kk, blk * nr:(blk + 1) * nr, :], w[kk]) for kk in range(2)]
        x = x + modp_ref[5:6, :] * _moe_mix(y_ref, w[0], w[1])
        xnew_ref[...] = x
    ms = jnp.mean(x * x, axis=-1, keepdims=True)
    y = x * lax.rsqrt(ms + EPS) * nw_ref[...]
    h = y * (1.0 + mod_ref[1:2, :]) + mod_ref[0:1, :]
    hb = h.astype(BF16)

    def proj(c0, width):
        return jnp.dot(hb, w_ref[:, c0:c0 + width], preferred_element_type=F32)

    uab = jnp.dot(proj(C_U, FNET_WIDTH).astype(BF16), cs_ref[...], preferred_element_type=F32)
    ab_ref[0] = uab[:, :FNET_WIDTH].astype(BF16)
    ab_ref[1] = uab[:, FNET_WIDTH:].astype(BF16)

    gqk_ref[...] = proj(C_GQ, 2 * GLA_QK)
    gv = proj(C_GV, GLA_WIDTH)
    gv_ref[...] = gv.astype(BF16)
    gvt_ref[...] = gv.T.astype(BF16)
    gg_ref[...] = proj(C_GG, GLA_WIDTH).astype(BF16)
    pre = jnp.dot(proj(C_GD, 128).astype(BF16), wup_ref[...], preferred_element_type=F32) + bup_ref[...]
    la_ref[...] = (jnp.minimum(pre, 0.0) - jnp.log1p(jnp.exp(-jnp.abs(pre)))) * (1.0 / GLA_GATE_NORM)

    bd = bd_ref[...]
    q = _head_rms(proj(C_AQ, ATT_WIDTH), bd, qn_ref[...])
    kv = proj(C_AK, 256)
    k = _head_rms(kv[:, :128], bd[:128, :128], kn_ref[...])
    if rope:
        cos = cos_ref[...]
        sin = sin_ref[...]
        q = jnp.concatenate(
            [q[:, s:s + 128] * cos + _swap16(q[:, s:s + 128]) * sin for s in range(0, ATT_WIDTH, 128)], axis=1)
        k = k * cos + _swap16(k) * sin
    q_ref[...] = (q * (ATT_HDIM ** -0.5 * LOG2E)).astype(BF16)
    v = kv[:, 128:]
    lo = lax.broadcasted_iota(jnp.int32, k.shape, 1) < ATT_HDIM
    k_sw = pltpu.roll(k, ATT_HDIM, 1)
    v_sw = pltpu.roll(v, ATT_HDIM, 1)
    kt_ref[0] = jnp.where(lo, k, k_sw).T.astype(BF16)
    kt_ref[1] = jnp.where(lo, k_sw, k).T.astype(BF16)
    vd_ref[0] = jnp.where(lo, v, 1.0).astype(BF16)
    vd_ref[1] = jnp.where(lo, 1.0, v_sw).astype(BF16)
    vd_ref[2] = jnp.where(lo, v_sw, 1.0).astype(BF16)
    vd_ref[3] = jnp.where(lo, 1.0, v).astype(BF16)


def _inproj_call(x, mod_l, mod_row, nw, w_perm, cs, wup, bup, qn, kn, bd, rope_tabs, pending=None):
    b, n, d = x.shape
    tm = min(512, n)
    nt = n // tm
    rope = rope_tabs is not None
    if mod_row is None:
        mod_map = lambda bi, i: (bi, 0, 0)
    else:
        mod_map = lambda bi, i: (mod_row, 0, 0)
    const = lambda bi, i: (0, 0)
    in_specs = [
        pl.BlockSpec((None, tm, d), lambda bi, i: (bi, i, 0)),
        pl.BlockSpec((None, 6, d), mod_map),
        pl.BlockSpec((1, d), const),
        pl.BlockSpec((d, W_IN_COLS), const),
        pl.BlockSpec((FNET_WIDTH, 2 * FNET_WIDTH), const),
        pl.BlockSpec((128, 2 * GLA_QK), const),
        pl.BlockSpec((1, 2 * GLA_QK), const),
        pl.BlockSpec((1, ATT_WIDTH), const),
        pl.BlockSpec((1, 128), const),
        pl.BlockSpec((ATT_WIDTH, ATT_WIDTH), const),
    ]
    args = [x, mod_l, nw, w_perm, cs, wup, bup, qn, kn, bd]
    if rope:
        in_specs += [pl.BlockSpec((tm, 128), lambda bi, i: (i, 0)), pl.BlockSpec((tm, 128), lambda bi, i: (i, 0))]
        args += list(rope_tabs)
    pending_rows = None
    if pending is not None:
        y2, wts3, row0, mod_prev = pending
        nr = tm // 128
        rps = n // 128
        assert 8 % nr == 0 and row0 % nr == 0 and rps % nr == 0
        pending_rows = (row0, rps)
        in_specs += [
            pl.BlockSpec((2, None, tm, d // 2), lambda bi, i: (0, bi, i, 0)),
            pl.BlockSpec((2, 8, 128), lambda bi, i: (0, (row0 + bi * rps + i * nr) // 8, 0)),
            pl.BlockSpec((None, 6, d), mod_map),
        ]
        args += [y2.reshape(2, b, n, d // 2), wts3, mod_prev]
    out_shape = (
        jax.ShapeDtypeStruct((2, n, b * FNET_WIDTH), BF16),
        jax.ShapeDtypeStruct((b, n, 2 * GLA_QK), F32),
        jax.ShapeDtypeStruct((b, n, GLA_WIDTH), BF16),
        jax.ShapeDtypeStruct((b, GLA_WIDTH, n), BF16),
        jax.ShapeDtypeStruct((b, n, GLA_WIDTH), BF16),
        jax.ShapeDtypeStruct((b, n, 2 * GLA_QK), F32),
        jax.ShapeDtypeStruct((b, n, ATT_WIDTH), BF16),
        jax.ShapeDtypeStruct((b, ATT_KV_HEADS, 128, n), BF16),
        jax.ShapeDtypeStruct((b, 2 * ATT_KV_HEADS, n, 128), BF16),
    )
    out_specs = (
        pl.BlockSpec((2, tm, FNET_WIDTH), lambda bi, i: (0, i, bi)),
        pl.BlockSpec((None, tm, 2 * GLA_QK), lambda bi, i: (bi, i, 0)),
        pl.BlockSpec((None, tm, GLA_WIDTH), lambda bi, i: (bi, i, 0)),
        pl.BlockSpec((None, GLA_WIDTH, tm), lambda bi, i: (bi, 0, i)),
        pl.BlockSpec((None, tm, GLA_WIDTH), lambda bi, i: (bi, i, 0)),
        pl.BlockSpec((None, tm, 2 * GLA_QK), lambda bi, i: (bi, i, 0)),
        pl.BlockSpec((None, tm, ATT_WIDTH), lambda bi, i: (bi, i, 0)),
        pl.BlockSpec((None, ATT_KV_HEADS, 128, tm), lambda bi, i: (bi, 0, 0, i)),
        pl.BlockSpec((None, 2 * ATT_KV_HEADS, tm, 128), lambda bi, i: (bi, 0, i, 0)),
    )
    if pending is not None:
        out_shape += (jax.ShapeDtypeStruct((b, n, d), F32),)
        out_specs += (pl.BlockSpec((None, tm, d), lambda bi, i: (bi, i, 0)),)
    return pl.pallas_call(
        functools.partial(_inproj_kernel, rope=rope, pending_rows=pending_rows),
        grid=(b, nt),
        in_specs=in_specs,
        out_specs=out_specs,
        out_shape=out_shape,
        compiler_params=_cparams(("parallel", "parallel")),
        name="inproj_rope" if rope else "inproj_ctx",
    )(*args)


def _seqdft_kernel(t_ref, ab_ref, o_ref):
    y = jnp.dot(t_ref[...], ab_ref[...], preferred_element_type=F32)
    for bb in range(o_ref.shape[0]):
        o_ref[bb] = y[:, bb * FNET_WIDTH:(bb + 1) * FNET_WIDTH].astype(BF16)


def _seqdft_call(table, ab, b):
    n = table.shape[0]
    tm = min(512, n)
    nb = 4 if b % 4 == 0 else (2 if b % 2 == 0 else 1)
    return pl.pallas_call(
        _seqdft_kernel,
        grid=(b // nb, n // tm),
        in_specs=[
            pl.BlockSpec((tm, 2 * n), lambda c, i: (i, 0)),
            pl.BlockSpec((2 * n, nb * FNET_WIDTH), lambda c, i: (0, c)),
        ],
        out_specs=pl.BlockSpec((nb, tm, FNET_WIDTH), lambda c, i: (c, i, 0)),
        out_shape=jax.ShapeDtypeStruct((b, n, FNET_WIDTH), BF16),
        compiler_params=_cparams(("parallel", "parallel")),
        name="seq_dft",
    )(table, ab)


def _gla_dir(qk, v, vt, a, s_in, fwd):
    p = GLA_PAIR
    r = lax.broadcasted_iota(jnp.int32, (p, p), 0)
    c = lax.broadcasted_iota(jnp.int32, (p, p), 1)
    same = (r // GLA_CHUNK) == (c // GLA_CHUNK)
    tri = same & ((c <= r) if fwd else (c >= r))
    row_lo = r < GLA_CHUNK
    rin = r % GLA_CHUNK

    q = qk[:, :GLA_QK] * (GLA_DK ** -0.5)
    k = qk[:, GLA_QK:]
    cum = a
    sh = 1
    while sh < GLA_CHUNK:
        if fwd:
            cum = cum + jnp.where(rin >= sh, pltpu.roll(cum, sh, 0), 0.0)
        else:
            cum = cum + jnp.where(rin < GLA_CHUNK - sh, pltpu.roll(cum, p - sh, 0), 0.0)
        sh *= 2
    if fwd:
        last0, last1 = cum[GLA_CHUNK - 1:GLA_CHUNK, :], cum[p - 1:p, :]
    else:
        last0, last1 = cum[0:1, :], cum[GLA_CHUNK:GLA_CHUNK + 1, :]
    lastb = jnp.where(row_lo, last0, last1)
    qt = q * jnp.exp(cum)
    kt = k * jnp.exp(-cum)
    kd = k * jnp.exp(lastb - cum)

    kt_b = kt.astype(BF16)
    zk = jnp.zeros_like(kt_b)
    ks = jnp.concatenate([jnp.where((c // GLA_DK) == hh, kt_b, zk) for hh in range(GLA_HEADS)], axis=0)
    att = _nt_dot(qt.astype(BF16), ks)
    tri4 = jnp.concatenate([tri] * GLA_HEADS, axis=1)
    att = jnp.where(tri4, att, 0.0).astype(BF16)
    col = lax.broadcasted_iota(jnp.int32, (p, GLA_WIDTH), 1)
    zv = jnp.zeros_like(v)
    vs = jnp.concatenate([jnp.where((col // GLA_DV) == hh, v, zv) for hh in range(GLA_HEADS)], axis=0)
    o_intra = jnp.dot(att, vs, preferred_element_type=F32)

    sr = lax.broadcasted_iota(jnp.int32, (GLA_WIDTH, GLA_QK), 0)
    sc = lax.broadcasted_iota(jnp.int32, (GLA_WIDTH, GLA_QK), 1)
    bdm = (sr // GLA_DV) == (sc // GLA_DK)
    first, second = (0, 1) if fwd else (1, 0)
    lasts = (last0, last1)
    in_chunk = (row_lo, jnp.logical_not(row_lo))
    kd2 = jnp.concatenate([jnp.where(in_chunk[0], kd, 0.0), jnp.where(in_chunk[1], kd, 0.0)], axis=1).astype(BF16)
    kvt2 = jnp.dot(vt, kd2, preferred_element_type=F32)
    kvt = (kvt2[:, :GLA_QK], kvt2[:, GLA_QK:])
    s_a = s_in
    s_b = s_a * jnp.exp(lasts[first]) + jnp.where(bdm, kvt[first], 0.0)
    s_c = s_b * jnp.exp(lasts[second]) + jnp.where(bdm, kvt[second], 0.0)
    q2 = jnp.concatenate([jnp.where(in_chunk[first], qt, 0.0), jnp.where(in_chunk[second], qt, 0.0)], axis=1)
    s2 = jnp.concatenate([s_a, s_b], axis=1).astype(BF16)
    o_inter = _nt_dot(q2.astype(BF16), s2)
    return o_intra + o_inter, s_c


def _gla_kernel(qkf, vf, vtf, laf, qkb, vb, vtb, lab, s0_ref, of_ref, ob_ref, sfin_ref, s_scr):
    i = pl.program_id(1)

    @pl.when(i == 0)
    def _():
        s_scr[...] = s0_ref[...]

    for gi in range(qkf.shape[0]):
        o1, sf = _gla_dir(qkf[gi], vf[gi], vtf[gi], laf[gi], s_scr[gi, 0], True)
        o2, sb = _gla_dir(qkb[gi], vb[gi], vtb[gi], lab[gi], s_scr[gi, 1], False)
        of_ref[gi] = o1.astype(of_ref.dtype)
        ob_ref[gi] = o2.astype(ob_ref.dtype)
        s_scr[gi, 0] = sf
        s_scr[gi, 1] = sb

    @pl.when(i == pl.num_programs(1) - 1)
    def _():
        sfin_ref[...] = s_scr[...]


def _gla_call(gqk, gv, gvt, la, s0):
    b, n, _ = gqk.shape
    p = GLA_PAIR
    npair = n // p
    gb = 4 if b % 4 == 0 else (2 if b % 2 == 0 else 1)
    fw = lambda bi, i: (bi, i, 0)
    bw = lambda bi, i: (bi, npair - 1 - i, 0)
    in_specs = [
        pl.BlockSpec((gb, p, 2 * GLA_QK), fw),
        pl.BlockSpec((gb, p, GLA_WIDTH), fw),
        pl.BlockSpec((gb, GLA_WIDTH, p), lambda bi, i: (bi, 0, i)),
        pl.BlockSpec((gb, p, GLA_QK), fw),
        pl.BlockSpec((gb, p, 2 * GLA_QK), bw),
        pl.BlockSpec((gb, p, GLA_WIDTH), bw),
        pl.BlockSpec((gb, GLA_WIDTH, p), lambda bi, i: (bi, 0, npair - 1 - i)),
        pl.BlockSpec((gb, p, GLA_QK), lambda bi, i: (bi, npair - 1 - i, 1)),
        pl.BlockSpec((gb, 2, GLA_WIDTH, GLA_QK), lambda bi, i: (bi, 0, 0, 0)),
    ]
    out_specs = (
        pl.BlockSpec((gb, p, GLA_WIDTH), fw),
        pl.BlockSpec((gb, p, GLA_WIDTH), bw),
        pl.BlockSpec((gb, 2, GLA_WIDTH, GLA_QK), lambda bi, i: (bi, 0, 0, 0)),
    )
    out_shape = (
        jax.ShapeDtypeStruct((b, n, GLA_WIDTH), BF16),
        jax.ShapeDtypeStruct((b, n, GLA_WIDTH), BF16),
        jax.ShapeDtypeStruct((b, 2, GLA_WIDTH, GLA_QK), F32),
    )
    return pl.pallas_call(
        _gla_kernel,
        grid=(b // gb, npair),
        in_specs=in_specs,
        out_specs=out_specs,
        out_shape=out_shape,
        scratch_shapes=[pltpu.VMEM((gb, 2, GLA_WIDTH, GLA_QK), F32)],
        compiler_params=_cparams(("parallel", "arbitrary")),
        name="gla_scan",
    )(gqk, gv, gvt, la, gqk, gv, gvt, la, s0)


def _attn_heads(q_ref, parts):
    tq = q_ref.shape[0]
    lane = lax.broadcasted_iota(jnp.int32, (tq, 128), 1)
    lo = lane < ATT_HDIM
    blocks = []
    for j in range(ATT_HEADS // 2):
        q128 = q_ref[:, 128 * j:128 * (j + 1)]
        g = (2 * j) // (ATT_HEADS // ATT_KV_HEADS)
        outs = []
        for half in range(2):
            qm = jnp.where(lo if half == 0 else jnp.logical_not(lo), q128, jnp.zeros_like(q128))
            ss = [jnp.dot(qm, kt_ref[g], preferred_element_type=F32) for kt_ref, _ in parts]
            m = functools.reduce(jnp.maximum, [jnp.max(s, axis=-1, keepdims=True) for s in ss])
            ps = [jnp.exp2(s - m).astype(BF16) for s in ss]
            pv = functools.reduce(
                lambda u, w: u + w,
                [jnp.dot(pp, vd_ref[2 * g + half], preferred_element_type=F32) for pp, (_, vd_ref) in zip(ps, parts)])
            den = pv[:, ATT_HDIM:ATT_HDIM + 1] if half == 0 else pv[:, 0:1]
            outs.append(pv / den)
        blocks.append(jnp.where(lo, outs[0], outs[1]).astype(BF16))
    return jnp.concatenate(blocks, axis=1)


def _attn_outproj_kernel(*refs, nparts):
    q_ref = refs[0]
    parts = [(refs[1 + 2 * i], refs[2 + 2 * i]) for i in range(nparts)]
    (f_ref, of_ref, ob_ref, gg_ref, x_ref, mod_ref, w_ref, gn_ref, bd_ref, nf_ref, wr_ref,
     xn_ref, h2_ref, lg_ref) = refs[1 + 2 * nparts:]
    att = _attn_heads(q_ref, parts)
    o = of_ref[...].astype(F32) + ob_ref[...].astype(F32)
    ms = jnp.dot((o * o).astype(BF16), bd_ref[...], preferred_element_type=F32) * (1.0 / GLA_DV)
    on = o * lax.rsqrt(ms + EPS) * gn_ref[...]
    g = gg_ref[...].astype(F32)
    gl = (on * (g * _sigmoid(g))).astype(BF16)
    ox = (jnp.dot(f_ref[...], w_ref[0:FNET_WIDTH, :], preferred_element_type=F32)
          + jnp.dot(gl, w_ref[FNET_WIDTH:FNET_WIDTH + GLA_WIDTH, :], preferred_element_type=F32)
          + jnp.dot(att, w_ref[FNET_WIDTH + GLA_WIDTH:, :], preferred_element_type=F32))
    xn = x_ref[...] + mod_ref[2:3, :] * ox
    xn_ref[...] = xn
    ms2 = jnp.mean(xn * xn, axis=-1, keepdims=True)
    h2 = xn * lax.rsqrt(ms2 + EPS) * nf_ref[...] * (1.0 + mod_ref[4:5, :]) + mod_ref[3:4, :]
    h2_ref[...] = _pack_bf16_pairs(h2)
    lg2 = _nt_dot(wr_ref[...], h2.astype(BF16))
    lg_ref[...] = lg2[:N_EXPERTS, :] + lg2[N_EXPERTS:, :]


def _attn_outproj_call(q, kv_parts, f, of, ob, gg, x, mod_l, mod_row, w_out, gn, bd, nf, wr):
    b, n, d = x.shape
    tm = min(512, n)
    nt = n // tm
    if mod_row is None:
        mod_map = lambda bi, i: (bi, 0, 0)
    else:
        mod_map = lambda bi, i: (mod_row, 0, 0)
    const = lambda bi, i: (0, 0)
    tok = lambda w: pl.BlockSpec((None, tm, w), lambda bi, i: (bi, i, 0))
    in_specs = [tok(ATT_WIDTH)]
    args = [q]
    for kt, vd in kv_parts:
        m = kt.shape[-1]
        in_specs.append(pl.BlockSpec((None, ATT_KV_HEADS, 128, m), lambda bi, i: (bi, 0, 0, 0)))
        in_specs.append(pl.BlockSpec((None, 2 * ATT_KV_HEADS, m, 128), lambda bi, i: (bi, 0, 0, 0)))
        args += [kt, vd]
    in_specs += [
        tok(FNET_WIDTH), tok(GLA_WIDTH), tok(GLA_WIDTH), tok(GLA_WIDTH), tok(d),
        pl.BlockSpec((None, 6, d), mod_map),
        pl.BlockSpec((d, d), const),
        pl.BlockSpec((1, GLA_WIDTH), const),
        pl.BlockSpec((GLA_WIDTH, GLA_WIDTH), const),
        pl.BlockSpec((1, d), const),
        pl.BlockSpec((2 * N_EXPERTS, d), const),
    ]
    args += [f, of, ob, gg, x, mod_l, w_out, gn, bd, nf, wr]
    out_specs = (
        tok(d), tok(d // 2),
        pl.BlockSpec((N_EXPERTS, tm), lambda bi, i: (0, bi * nt + i)),
    )
    out_shape = (
        jax.ShapeDtypeStruct((b, n, d), F32),
        jax.ShapeDtypeStruct((b, n, d // 2), jnp.uint32),
        jax.ShapeDtypeStruct((N_EXPERTS, b * n), F32),
    )
    return pl.pallas_call(
        functools.partial(_attn_outproj_kernel, nparts=len(kv_parts)),
        grid=(b, nt),
        in_specs=in_specs,
        out_specs=out_specs,
        out_shape=out_shape,
        compiler_params=_cparams(("parallel", "parallel")),
        name="attn_outproj",
    )(*args)


def _route_kernel(b_ref, lg_ref, pos_ref, wt_ref, te_ref, nv_ref, *, tm):
    r = lg_ref.shape[1]
    s = [_sigmoid(lg_ref[e]) for e in range(N_EXPERTS)]
    sel = [s[e] + b_ref[e] for e in range(N_EXPERTS)]
    grp = []
    for g in range(N_GROUPS):
        a, b, c, d = sel[4 * g:4 * g + 4]
        hi1, lo1 = jnp.maximum(a, b), jnp.minimum(a, b)
        hi2, lo2 = jnp.maximum(c, d), jnp.minimum(c, d)
        m1 = jnp.maximum(hi1, hi2)
        m2 = jnp.maximum(jnp.minimum(hi1, hi2), jnp.maximum(lo1, lo2))
        grp.append(m1 + m2)
    one = jnp.ones_like(s[0])
    zero = jnp.zeros_like(s[0])
    msk = []
    for g in range(N_GROUPS):
        isg = one
        for g2 in range(N_GROUPS):
            if g2 < g:
                isg = isg * jnp.where(grp[g] > grp[g2], one, zero)
            elif g2 > g:
                isg = isg * jnp.where(grp[g] >= grp[g2], one, zero)
        for li in range(EXPERTS_PER_GROUP):
            e = 4 * g + li
            rank = zero
            for lj in range(EXPERTS_PER_GROUP):
                ej = 4 * g + lj
                if lj < li:
                    rank = rank + jnp.where(sel[ej] >= sel[e], one, zero)
                elif lj > li:
                    rank = rank + jnp.where(sel[ej] > sel[e], one, zero)
            msk.append(jnp.where(rank < 2.0, isg, zero))
    den = functools.reduce(lambda u, v: u + v, [msk[e] * s[e] for e in range(N_EXPERTS)])

    li_ = lax.broadcasted_iota(jnp.int32, (128, 128), 0)
    lj_ = lax.broadcasted_iota(jnp.int32, (128, 128), 1)
    upper = jnp.where(li_ < lj_, 1.0, 0.0).astype(BF16)
    ri_ = lax.broadcasted_iota(jnp.int32, (r, r), 0)
    rj_ = lax.broadcasted_iota(jnp.int32, (r, r), 1)
    lower = jnp.where(rj_ < ri_, 1.0, 0.0).astype(BF16)
    tile_start = lax.broadcasted_iota(jnp.int32, te_ref.shape, 1).astype(F32) * float(tm)
    te = jnp.zeros(te_ref.shape, F32)
    off = jnp.zeros((1, 1), F32)
    seen = zero
    pos = [zero, zero]
    wts = [zero, zero]
    for e in range(N_EXPERTS):
        mb = msk[e].astype(BF16)
        lane_pre = jnp.dot(mb, upper, preferred_element_type=F32)
        row_pre = jnp.sum(jnp.dot(lower, mb, preferred_element_type=F32), axis=1, keepdims=True)
        cnt = jnp.sum(jnp.sum(msk[e], axis=1, keepdims=True), axis=0, keepdims=True)
        p_e = off + row_pre + lane_pre
        g_e = s[e] / den
        for kk in range(2):
            hit = msk[e] * jnp.where(seen == float(kk), one, zero)
            pos[kk] = pos[kk] + hit * p_e
            wts[kk] = wts[kk] + hit * g_e
        seen = seen + msk[e]
        off = off + jnp.floor((cnt + float(tm - 1)) * (1.0 / tm)) * float(tm)
        te = te + jnp.where(tile_start >= off, 1.0, 0.0)
    for kk in range(2):
        pos_ref[kk] = pos[kk].astype(jnp.int32)
        wt_ref[kk] = wts[kk]
    te_ref[...] = jnp.minimum(te, float(N_EXPERTS - 1)).astype(jnp.int32)
    nv_ref[...] = jnp.broadcast_to(off * (1.0 / tm), nv_ref.shape).astype(jnp.int32)


def _route_call(lgt, b_router, tm):
    n_tok = lgt.shape[1]
    r = n_tok // 128
    assert r * 128 == n_tok and r % 8 == 0 and 2 * n_tok // tm + N_EXPERTS <= 256
    lg3 = lgt.reshape(N_EXPERTS, r, 128)
    full3 = lambda k: pl.BlockSpec((k, r, 128), lambda: (0, 0, 0))
    pos, wts, te, nv = pl.pallas_call(
        functools.partial(_route_kernel, tm=tm),
        in_specs=[pl.BlockSpec(memory_space=pltpu.SMEM), full3(N_EXPERTS)],
        out_specs=(full3(2), full3(2), pl.BlockSpec((1, 256), lambda: (0, 0)), pl.BlockSpec((1, 128), lambda: (0, 0))),
        out_shape=(
            jax.ShapeDtypeStruct((2, r, 128), jnp.int32),
            jax.ShapeDtypeStruct((2, r, 128), F32),
            jax.ShapeDtypeStruct((1, 256), jnp.int32),
            jax.ShapeDtypeStruct((1, 128), jnp.int32),
        ),
        compiler_params=pltpu.CompilerParams(vmem_limit_bytes=VMEM_LIMIT),
        name="route",
    )(b_router, lg3)
    return pos.reshape(2, n_tok), wts, te.reshape(256), nv[0, :1]


SC_CORES = 2
SC_SUBCORES = 16
SC_WORKERS = SC_CORES * SC_SUBCORES
SC_CHUNK = 32


def _sc_mesh():
    return plsc.VectorSubcoreMesh(core_axis_name="c", subcore_axis_name="s",
                                  num_cores=SC_CORES, num_subcores=SC_SUBCORES)


def _sc_steps(n_rows):
    per_w = n_rows // SC_WORKERS
    steps = per_w // SC_CHUNK
    assert per_w * SC_WORKERS == n_rows and steps * SC_CHUNK == per_w and steps % 2 == 0, n_rows
    return per_w, steps


def _sc_gather_rows(table, idx):
    p = idx.shape[0]
    d = table.shape[1]
    per_w, steps = _sc_steps(p)
    idx3 = idx.reshape(SC_WORKERS, steps, SC_CHUNK)

    @functools.partial(
        pl.kernel, mesh=_sc_mesh(),
        out_type=jax.ShapeDtypeStruct((p, d), table.dtype),
        scratch_types=[
            pltpu.VMEM((steps, SC_CHUNK), jnp.int32),
            pltpu.VMEM((SC_CHUNK, d), table.dtype),
            pltpu.VMEM((SC_CHUNK, d), table.dtype),
            pltpu.SemaphoreType.DMA, pltpu.SemaphoreType.DMA,
            pltpu.SemaphoreType.DMA, pltpu.SemaphoreType.DMA,
        ],
        name="sc_gather_rows",
    )
    def k(table_hbm, idx_hbm, out_hbm, idx_v, buf0, buf1, g0, g1, w0, w1):
        wid = lax.axis_index("s") * SC_CORES + lax.axis_index("c")
        base = wid * per_w
        pltpu.sync_copy(idx_hbm.at[wid], idx_v)

        def gather(s, buf, sem):
            return pltpu.make_async_copy(table_hbm.at[idx_v.at[s]], buf, sem)

        def write(s, buf, sem):
            return pltpu.make_async_copy(buf, out_hbm.at[pl.ds(base + s * SC_CHUNK, SC_CHUNK)], sem)

        gather(0, buf0, g0).start()

        @pl.loop(0, steps, step=2)
        def _(s):
            gather(s + 1, buf1, g1).start()
            gather(s, buf0, g0).wait()
            write(s, buf0, w0).start()
            write(s, buf0, w0).wait()

            @pl.when(s + 2 < steps)
            def _():
                gather(s + 2, buf0, g0).start()

            gather(s + 1, buf1, g1).wait()
            write(s + 1, buf1, w1).start()
            write(s + 1, buf1, w1).wait()

    return k(table, idx3)


def _sc_dispatch(srcs, poss, p_rows):
    d = srcs[0].shape[1]
    dt = srcs[0].dtype
    plans = [_sc_steps(src.shape[0]) for src in srcs]
    idxs = [pos.reshape(2, SC_WORKERS, st, SC_CHUNK) for pos, (_, st) in zip(poss, plans)]
    nseg = len(srcs)
    scratch = [pltpu.VMEM((2, st, SC_CHUNK), jnp.int32) for _, st in plans]
    scratch += [pltpu.VMEM((SC_CHUNK, d), dt), pltpu.VMEM((SC_CHUNK, d), dt)]
    scratch += [pltpu.SemaphoreType.DMA] * 6

    @functools.partial(
        pl.kernel, mesh=_sc_mesh(),
        out_type=jax.ShapeDtypeStruct((p_rows, d), dt),
        scratch_types=scratch,
        name="sc_dispatch",
    )
    def k(*refs):
        src_hbm = refs[:nseg]
        idx_hbm = refs[nseg:2 * nseg]
        out_hbm = refs[2 * nseg]
        idx_v = refs[2 * nseg + 1:3 * nseg + 1]
        buf0, buf1, r0, r1, a0, a1, b0, b1 = refs[3 * nseg + 1:]
        wid = lax.axis_index("s") * SC_CORES + lax.axis_index("c")
        for seg in range(nseg):
            per_w, steps = plans[seg]
            base = wid * per_w
            for kk in range(2):
                pltpu.sync_copy(idx_hbm[seg].at[kk, wid], idx_v[seg].at[kk])

            def read(s, buf, sem, seg=seg, base=base):
                return pltpu.make_async_copy(src_hbm[seg].at[pl.ds(base + s * SC_CHUNK, SC_CHUNK)], buf, sem)

            def scat(kk, s, buf, sem, seg=seg):
                return pltpu.make_async_copy(buf, out_hbm.at[idx_v[seg].at[kk, s]], sem)

            read(0, buf0, r0).start()

            @pl.loop(0, steps, step=2)
            def _(s, read=read, scat=scat, steps=steps):
                read(s + 1, buf1, r1).start()
                read(s, buf0, r0).wait()
                scat(0, s, buf0, a0).start()
                scat(1, s, buf0, b0).start()
                scat(0, s, buf0, a0).wait()
                scat(1, s, buf0, b0).wait()

                @pl.when(s + 2 < steps)
                def _():
                    read(s + 2, buf0, r0).start()

                read(s + 1, buf1, r1).wait()
                scat(0, s + 1, buf1, a1).start()
                scat(1, s + 1, buf1, b1).start()
                scat(0, s + 1, buf1, a1).wait()
                scat(1, s + 1, buf1, b1).wait()

    return k(*srcs, *idxs)


MOE_TM = 512


def _experts_kernel(te_ref, nv_ref, xs_ref, wg_ref, wu_ref, wd_ref, ys_ref):
    @pl.when(pl.program_id(0) < nv_ref[0])
    def _():
        h = _unpack_bf16_pairs(xs_ref[...]).astype(BF16)
        half = D_EXPERT // 2
        y = None
        for j in range(2):
            sl = slice(j * half, (j + 1) * half)
            a = jnp.dot(h, wg_ref[:, sl].astype(BF16), preferred_element_type=F32)
            u = jnp.dot(h, wu_ref[:, sl].astype(BF16), preferred_element_type=F32)
            t = ((a * _sigmoid(a)) * u).astype(BF16)
            yj = jnp.dot(t, wd_ref[sl, :].astype(BF16), preferred_element_type=F32)
            y = yj if y is None else y + yj
        ys_ref[...] = _pack_bf16_pairs(y)


def _experts_call(xs, te, nv, wg, wu, wd, layer):
    p_rows, dh = xs.shape
    d = 2 * dh
    tm = MOE_TM
    nt = p_rows // tm
    row = lambda i, te_r, nv_r: (jnp.minimum(i, nv_r[0] - 1), 0)
    wsel = lambda i, te_r, nv_r: (layer, te_r[jnp.minimum(i, nv_r[0] - 1)], 0, 0)
    grid_spec = pltpu.PrefetchScalarGridSpec(
        num_scalar_prefetch=2,
        grid=(nt,),
        in_specs=[
            pl.BlockSpec((tm, dh), row),
            pl.BlockSpec((None, None, d, D_EXPERT), wsel),
            pl.BlockSpec((None, None, d, D_EXPERT), wsel),
            pl.BlockSpec((None, None, D_EXPERT, d), wsel),
        ],
        out_specs=pl.BlockSpec((tm, dh), row),
    )
    return pl.pallas_call(
        _experts_kernel,
        grid_spec=grid_spec,
        out_shape=jax.ShapeDtypeStruct((p_rows, dh), jnp.uint32),
        compiler_params=_cparams(("arbitrary",)),
        name="moe_experts",
    )(te, nv, xs, wg, wu, wd)


COMBINE_TM = 1024


def _combine_kernel(y_ref, wt_ref, x_ref, mod_ref, fn_ref, o_ref, *, final):
    xo = x_ref[...] + mod_ref[5:6, :] * _moe_mix(y_ref, wt_ref[0], wt_ref[1])
    if final:
        ms = jnp.mean(xo * xo, axis=-1, keepdims=True)
        xo = xo * lax.rsqrt(ms + EPS) * fn_ref[...]
    o_ref[...] = xo


def _combine_call(y2, wts3, row0, x, mod_l, mod_row, fn, final):
    b, n, d = x.shape
    n_tok = b * n
    tm = COMBINE_TM
    assert n_tok % tm == 0 and row0 % 8 == 0 and (mod_row is not None or n % tm == 0)
    if mod_row is None:
        mod_map = lambda i: ((i * tm) // n, 0, 0)
    else:
        mod_map = lambda i: (mod_row, 0, 0)
    out = pl.pallas_call(
        functools.partial(_combine_kernel, final=final),
        grid=(n_tok // tm,),
        in_specs=[
            pl.BlockSpec((2, tm, d // 2), lambda i: (0, i, 0)),
            pl.BlockSpec((2, tm // 128, 128), lambda i: (0, row0 // 8 + i, 0)),
            pl.BlockSpec((tm, d), lambda i: (i, 0)),
            pl.BlockSpec((None, 6, d), mod_map),
            pl.BlockSpec((1, d), lambda i: (0, 0)),
        ],
        out_specs=pl.BlockSpec((tm, d), lambda i: (i, 0)),
        out_shape=jax.ShapeDtypeStruct((n_tok, d), F32),
        compiler_params=_cparams(("parallel",)),
        name="moe_combine",
    )(y2, wts3, x.reshape(n_tok, d), mod_l, fn)
    return out.reshape(b, n, d)


def _moe_sparse(h_list, lg_list, x_list, mod_l, mod_rows, b_router, wg, wu, wd, layer, fn, final):
    d = h_list[0].shape[-1]
    sizes = [h.shape[0] * h.shape[1] for h in h_list]
    n_tok = sum(sizes)
    lgt = lg_list[0] if len(lg_list) == 1 else jnp.concatenate(lg_list, axis=1)
    pos, wts, te, nv = _route_call(lgt, b_router, MOE_TM)
    p_rows = 2 * n_tok + N_EXPERTS * MOE_TM
    offs = np.cumsum([0] + sizes)
    poss = [pos[:, offs[i]:offs[i + 1]] for i in range(len(sizes))]
    xs = _sc_dispatch([h.reshape(-1, d) for h in h_list], poss, p_rows)
    ys = _experts_call(xs, te, nv, wg, wu, wd, layer)
    outs = []
    for i, x in enumerate(x_list):
        y2 = _sc_gather_rows(ys, poss[i].reshape(-1)).reshape(2, sizes[i], d)
        row0 = int(offs[i]) // 128
        if final:
            outs.append(_combine_call(y2, wts, row0, x, mod_l, mod_rows[i], fn, True))
        else:
            outs.append((y2, wts, row0, mod_l))
    return outs


def _winprep_kernel(w_ref, o_ref):
    gd0 = C_AQ
    gdw = 2 * GLA_GATE_RANK
    tail = W_IN_REF_COLS - gd0 - gdw
    o_ref[:, 0:gd0] = w_ref[:, 0:gd0].astype(BF16)
    o_ref[:, gd0:gd0 + tail] = w_ref[:, gd0 + gdw:W_IN_REF_COLS].astype(BF16)
    o_ref[:, C_GD:C_GD + gdw] = w_ref[:, gd0:gd0 + gdw].astype(BF16)
    o_ref[:, C_GD + gdw:] = jnp.zeros((o_ref.shape[0], W_IN_COLS - C_GD - gdw), BF16)


def _winprep_call(w_in):
    depth, d, cols = w_in.shape
    assert cols == W_IN_REF_COLS and C_GD == C_AQ + cols - C_AQ - 2 * GLA_GATE_RANK
    return pl.pallas_call(
        _winprep_kernel,
        grid=(depth,),
        in_specs=[pl.BlockSpec((None, d, cols), lambda l: (l, 0, 0))],
        out_specs=pl.BlockSpec((None, d, W_IN_COLS), lambda l: (l, 0, 0)),
        out_shape=jax.ShapeDtypeStruct((depth, d, W_IN_COLS), BF16),
        compiler_params=_cparams(("parallel",)),
        name="w_in_prep",
    )(w_in)


def _gate_up_weights(w_up, b_up):
    z = jnp.zeros((GLA_GATE_RANK, GLA_QK), w_up.dtype)
    top = jnp.concatenate([w_up[0], z], axis=1)
    mid = jnp.concatenate([z, w_up[1]], axis=1)
    pad = jnp.zeros((128 - 2 * GLA_GATE_RANK, 2 * GLA_QK), w_up.dtype)
    return jnp.concatenate([top, mid, pad], axis=0).astype(BF16), b_up.reshape(1, 2 * GLA_QK)


def kernel(x, c, ctx, c_ctx, w_ada, b_ada, norm_mix, norm_ffn, w_in, w_gla_gate_up, b_gla_gate, gla_norm, q_norm,
           k_norm, w_out, w_router, b_router, w_exp_gate, w_exp_up, w_exp_down, final_norm):
    b, n, d = x.shape
    m = ctx.shape[1]
    depth = w_ada.shape[0]
    assert d == D_MODEL and n % GLA_PAIR == 0 and m % GLA_PAIR == 0 and n % GRID_W == 0

    rows = ((b + 1 + 7) // 8) * 8
    cv = jnp.concatenate([c, c_ctx[None, :], jnp.zeros((rows - b - 1, d), F32)], axis=0)
    mod = _ada_call(cv, w_ada, b_ada).reshape(depth, rows, 6, d)

    cs = jnp.asarray(_channel_dft_table()).astype(BF16)
    tab_x = jnp.asarray(_seq_dft_table(n)).astype(BF16)
    tab_c = jnp.asarray(_seq_dft_table(m)).astype(BF16)
    rope_tabs = tuple(jnp.asarray(t) for t in _rope_tables(n))
    bd512 = jnp.asarray(_blockdiag_ones(ATT_WIDTH, ATT_HDIM)).astype(BF16)
    bd256 = jnp.asarray(_blockdiag_ones(GLA_WIDTH, GLA_DV)).astype(BF16)
    wr_t = w_router.T
    wr_hi = wr_t.astype(BF16)
    wrh = jnp.concatenate([wr_hi, (wr_t - wr_hi.astype(F32)).astype(BF16)], axis=0)
    fn = final_norm.reshape(1, d)

    w_in_perm = _winprep_call(w_in)

    xc = ctx
    pend_x = pend_c = None
    for l in range(depth):
        ctx_out = l < depth - 1
        mod_l = mod[l]
        w_perm = w_in_perm[l]
        wup, bup = _gate_up_weights(w_gla_gate_up[l], b_gla_gate[l])
        nw = norm_mix[l].reshape(1, d)
        nf = norm_ffn[l].reshape(1, d)
        qn = jnp.tile(q_norm[l], ATT_HEADS).reshape(1, ATT_WIDTH)
        kn = jnp.tile(k_norm[l], ATT_KV_HEADS).reshape(1, 128)
        gn = jnp.tile(gla_norm[l], GLA_HEADS).reshape(1, GLA_WIDTH)
        wo = w_out[l].astype(BF16)

        outs_c = _inproj_call(xc, mod_l, b, nw, w_perm, cs, wup, bup, qn, kn, bd512, None, pend_c)
        outs_x = _inproj_call(x, mod_l, None, nw, w_perm, cs, wup, bup, qn, kn, bd512, rope_tabs, pend_x)
        (ab_c, gqk_c, gv_c, gvt_c, gg_c, la_c, q_c, kt_c, vd_c) = outs_c[:9]
        (ab_x, gqk_x, gv_x, gvt_x, gg_x, la_x, q_x, kt_x, vd_x) = outs_x[:9]
        if pend_x is not None:
            x, xc = outs_x[9], outs_c[9]

        s_zero = jnp.zeros((b, 2, GLA_WIDTH, GLA_QK), F32)
        of_c, ob_c, s_fin = _gla_call(gqk_c, gv_c, gvt_c, la_c, s_zero)
        of_x, ob_x, _ = _gla_call(gqk_x, gv_x, gvt_x, la_x, s_fin)

        f_x = _seqdft_call(tab_x, ab_x.reshape(2 * n, b * FNET_WIDTH), b)
        x, h2_x, lg_x = _attn_outproj_call(q_x, [(kt_c, vd_c), (kt_x, vd_x)], f_x, of_x, ob_x, gg_x, x,
                                           mod_l, None, wo, gn, bd256, nf, wrh)

        if ctx_out:
            f_c = _seqdft_call(tab_c, ab_c.reshape(2 * m, b * FNET_WIDTH), b)
            xc, h2_c, lg_c = _attn_outproj_call(q_c, [(kt_c, vd_c)], f_c, of_c, ob_c, gg_c, xc,
                                                mod_l, b, wo, gn, bd256, nf, wrh)

        final = l == depth - 1
        wexp = (w_exp_gate, w_exp_up, w_exp_down, l)
        if ctx_out:
            res = _moe_sparse([h2_x, h2_c], [lg_x, lg_c], [x, xc], mod_l, [None, b], b_router, *wexp, fn, final)
        else:
            res = _moe_sparse([h2_x], [lg_x], [x], mod_l, [None], b_router, *wexp, fn, final)
        if final:
            x = res[0]
        else:
            pend_x, pend_c = res
    return x
```

```python
import functools

import numpy as np
import jax
import jax.numpy as jnp
from jax import lax
from jax.experimental import pallas as pl
from jax.experimental.pallas import tpu as pltpu
from jax.experimental.pallas import tpu_sc as plsc

F32 = jnp.float32
BF16 = jnp.bfloat16

D_MODEL = 1024
GRID_W = 64
EPS = 1e-6
LOG2E = 1.4426950408889634

FNET_WIDTH = 256
FNET_GROUPS = 4
FNET_GDIM = 64

GLA_HEADS = 4
GLA_DV = 64
GLA_DK = 32
GLA_WIDTH = 256
GLA_QK = 128
GLA_GATE_RANK = 16
GLA_GATE_NORM = 16.0
GLA_CHUNK = 64
GLA_PAIR = 2 * GLA_CHUNK

ATT_HEADS = 8
ATT_KV_HEADS = 2
ATT_HDIM = 64
ATT_WIDTH = 512
ROPE_FREQS = 16
ROPE_THETA = 10000.0

N_EXPERTS = 16
N_GROUPS = 4
EXPERTS_PER_GROUP = 4
D_EXPERT = 512

C_U = 0
C_GQ = 256
C_GK = 384
C_GV = 512
C_GG = 768
C_AQ = 1024
C_AK = 1536
C_GD = 1792
W_IN_COLS = 1920
W_IN_REF_COLS = 1824

VMEM_LIMIT = 56 * 1024 * 1024


def _cparams(sem):
    return pltpu.CompilerParams(dimension_semantics=sem, vmem_limit_bytes=VMEM_LIMIT)


def _sigmoid(x):
    return 1.0 / (1.0 + jnp.exp(-x))


def _pack_bf16_pairs(x):
    blocks = []
    for t in range(x.shape[1] // 256):
        lo = lax.bitcast_convert_type(x[:, 256 * t:256 * t + 128].astype(BF16).astype(F32), jnp.uint32)
        hi = lax.bitcast_convert_type(x[:, 256 * t + 128:256 * t + 256].astype(BF16).astype(F32), jnp.uint32)
        blocks.append((lo >> 16) | (hi & jnp.uint32(0xFFFF0000)))
    return lax.bitcast_convert_type(jnp.concatenate(blocks, axis=1), jnp.int32)


def _unpack_bf16_pairs(p):
    u = lax.bitcast_convert_type(p, jnp.uint32)
    blocks = []
    for t in range(p.shape[1] // 128):
        word = u[:, 128 * t:128 * (t + 1)]
        blocks += [lax.bitcast_convert_type(word << 16, F32),
                   lax.bitcast_convert_type(word & jnp.uint32(0xFFFF0000), F32)]
    return jnp.concatenate(blocks, axis=1)


def _token_columns(w):
    nr = w.shape[0]
    tm = nr * 128
    lane = lax.broadcasted_iota(jnp.int32, (tm, 128), 1)
    row = lax.broadcasted_iota(jnp.int32, (tm, 128), 0)
    wb = jnp.concatenate([jnp.broadcast_to(w[r:r + 1, :], (128, 128)) for r in range(nr)], axis=0)
    return jnp.sum(jnp.where(lane == (row % 128), wb, 0.0), axis=1, keepdims=True)


def _moe_mix(y_ref, w0, w1):
    return (_token_columns(w0) * _unpack_bf16_pairs(y_ref[0])
            + _token_columns(w1) * _unpack_bf16_pairs(y_ref[1]))


def _nt_dot(a, b):
    return lax.dot_general(a, b, (((1,), (1,)), ((), ())), preferred_element_type=F32)


@functools.lru_cache(maxsize=None)
def _channel_dft_table():
    j = np.arange(FNET_GDIM)
    ang = 2.0 * np.pi * ((j[:, None] * j[None, :]) % FNET_GDIM) / FNET_GDIM
    c = np.cos(ang) / np.sqrt(FNET_GDIM)
    s = np.sin(ang) / np.sqrt(FNET_GDIM)
    out = np.zeros((FNET_WIDTH, 2 * FNET_WIDTH), np.float64)
    for g in range(FNET_GROUPS):
        sl = slice(g * FNET_GDIM, (g + 1) * FNET_GDIM)
        out[sl, sl] = c
        out[sl, FNET_WIDTH + g * FNET_GDIM:FNET_WIDTH + (g + 1) * FNET_GDIM] = s
    return out.astype(np.float32)


@functools.lru_cache(maxsize=None)
def _seq_dft_table(n):
    j = np.arange(n, dtype=np.int64)
    ang = 2.0 * np.pi * ((j[:, None] * j[None, :]) % n) / n
    return np.concatenate([np.cos(ang), -np.sin(ang)], axis=1).astype(np.float32) / np.float32(np.sqrt(n))


@functools.lru_cache(maxsize=None)
def _rope_tables(n):
    rows = n // GRID_W
    row = np.repeat(np.arange(rows), GRID_W).astype(np.float64)
    col = np.tile(np.arange(GRID_W), rows).astype(np.float64)
    inv = ROPE_THETA ** (-np.arange(ROPE_FREQS, dtype=np.float64) * 2.0 / (2 * ROPE_FREQS))
    ar = row[:, None] * inv[None, :]
    ac = col[:, None] * inv[None, :]
    cos = np.concatenate([np.cos(ar), np.cos(ar), np.cos(ac), np.cos(ac)], axis=1)
    sin = np.concatenate([-np.sin(ar), np.sin(ar), -np.sin(ac), np.sin(ac)], axis=1)
    return (np.tile(cos, (1, 2)).astype(np.float32), np.tile(sin, (1, 2)).astype(np.float32))


@functools.lru_cache(maxsize=None)
def _blockdiag_ones(width, blk):
    i = np.arange(width)
    return (i[:, None] // blk == i[None, :] // blk).astype(np.float32)


def _ada_kernel(cv_ref, w_ref, b_ref, o_ref):
    cv = cv_ref[...]
    a = (cv * _sigmoid(cv)).astype(BF16)
    o_ref[...] = jnp.dot(a, w_ref[...].astype(BF16), preferred_element_type=F32) + b_ref[...]


def _ada_call(cv, w_ada, b_ada):
    depth, d, d6 = w_ada.shape
    tn = 1536
    rows = cv.shape[0]
    return pl.pallas_call(
        _ada_kernel,
        grid=(depth, d6 // tn),
        in_specs=[
            pl.BlockSpec((rows, d), lambda l, j: (0, 0)),
            pl.BlockSpec((None, d, tn), lambda l, j: (l, 0, j)),
            pl.BlockSpec((None, 1, tn), lambda l, j: (l, 0, j)),
        ],
        out_specs=pl.BlockSpec((None, rows, tn), lambda l, j: (l, 0, j)),
        out_shape=jax.ShapeDtypeStruct((depth, rows, d6), F32),
        compiler_params=_cparams(("parallel", "parallel")),
        name="ada_mod",
    )(cv, w_ada, b_ada.reshape(depth, 1, d6))


def _swap16(x):
    lane = lax.broadcasted_iota(jnp.int32, x.shape, 1)
    first = (lane % 32) < 16
    return jnp.where(first, pltpu.roll(x, 112, 1), pltpu.roll(x, 16, 1))


def _head_rms(x, bd, w):
    ms = jnp.dot((x * x).astype(BF16), bd, preferred_element_type=F32) * (1.0 / ATT_HDIM)
    return x * lax.rsqrt(ms + EPS) * w


def _inproj_kernel(*refs, rope, pending_rows):
    refs = list(refs)
    x_ref, mod_ref, nw_ref, w_ref, cs_ref, wup_ref, bup_ref, qn_ref, kn_ref, bd_ref = refs[:10]
    del refs[:10]
    if rope:
        cos_ref, sin_ref = refs[:2]
        del refs[:2]
    if pending_rows is not None:
        y_ref, wt_ref, modp_ref = refs[:3]
        del refs[:3]
    ab_ref, gqk_ref, gv_ref, gvt_ref, gg_ref, la_ref, q_ref, kt_ref, vd_ref = refs[:9]
    x = x_ref[...]
    if pending_rows is not None:
        xnew_ref = refs[9]
        row0, rows_per_sample = pending_rows
        nr = x.shape[0] // 128
        r = row0 + pl.program_id(0) * rows_per_sample + pl.program_id(1) * nr
        sub = lax.rem(r, 8)
        w = [wt_ref[kk, 0:nr, :] for kk in range(2)]
        for blk in range(1, 8 // nr):
            w = [jnp.where(sub == blk * nr, wt_ref[kk, blk * nr:(blk + 1) * nr, :], w[kk]) for kk in range(2)]
        x = x + modp_ref[5:6, :] * _moe_mix(y_ref, w[0], w[1])
        xnew_ref[...] = x
    ms = jnp.mean(x * x, axis=-1, keepdims=True)
    y = x * lax.rsqrt(ms + EPS) * nw_ref[...]
    h = y * (1.0 + mod_ref[1:2, :]) + mod_ref[0:1, :]
    hb = h.astype(BF16)

    def proj(c0, width):
        return jnp.dot(hb, w_ref[:, c0:c0 + width], preferred_element_type=F32)

    uab = jnp.dot(proj(C_U, FNET_WIDTH).astype(BF16), cs_ref[...], preferred_element_type=F32)
    ab_ref[0] = uab[:, :FNET_WIDTH].astype(BF16)
    ab_ref[1] = uab[:, FNET_WIDTH:].astype(BF16)

    gqk_ref[...] = proj(C_GQ, 2 * GLA_QK)
    gv = proj(C_GV, GLA_WIDTH)
    gv_ref[...] = gv.astype(BF16)
    gvt_ref[...] = gv.T.astype(BF16)
    gg_ref[...] = proj(C_GG, GLA_WIDTH).astype(BF16)
    pre = jnp.dot(proj(C_GD, 128).astype(BF16), wup_ref[...], preferred_element_type=F32) + bup_ref[...]
    la_ref[...] = (jnp.minimum(pre, 0.0) - jnp.log1p(jnp.exp(-jnp.abs(pre)))) * (1.0 / GLA_GATE_NORM)

    bd = bd_ref[...]
    q = _head_rms(proj(C_AQ, ATT_WIDTH), bd, qn_ref[...])
    kv = proj(C_AK, 256)
    k = _head_rms(kv[:, :128], bd[:128, :128], kn_ref[...])
    if rope:
        cos = cos_ref[...]
        sin = sin_ref[...]
        q = jnp.concatenate(
            [q[:, s:s + 128] * cos + _swap16(q[:, s:s + 128]) * sin for s in range(0, ATT_WIDTH, 128)], axis=1)
        k = k * cos + _swap16(k) * sin
    q_ref[...] = (q * (ATT_HDIM ** -0.5 * LOG2E)).astype(BF16)
    v = kv[:, 128:]
    lo = lax.broadcasted_iota(jnp.int32, k.shape, 1) < ATT_HDIM
    k_sw = pltpu.roll(k, ATT_HDIM, 1)
    v_sw = pltpu.roll(v, ATT_HDIM, 1)
    kt_ref[0] = jnp.where(lo, k, k_sw).T.astype(BF16)
    kt_ref[1] = jnp.where(lo, k_sw, k).T.astype(BF16)
    vd_ref[0] = jnp.where(lo, v, 1.0).astype(BF16)
    vd_ref[1] = jnp.where(lo, 1.0, v_sw).astype(BF16)
    vd_ref[2] = jnp.where(lo, v_sw, 1.0).astype(BF16)
    vd_ref[3] = jnp.where(lo, 1.0, v).astype(BF16)


def _inproj_call(x, mod_l, mod_row, nw, w_perm, cs, wup, bup, qn, kn, bd, rope_tabs, pending=None):
    b, n, d = x.shape
    tm = min(512, n)
    nt = n // tm
    rope = rope_tabs is not None
    if mod_row is None:
        mod_map = lambda bi, i: (bi, 0, 0)
    else:
        mod_map = lambda bi, i: (mod_row, 0, 0)
    const = lambda bi, i: (0, 0)
    in_specs = [
        pl.BlockSpec((None, tm, d), lambda bi, i: (bi, i, 0)),
        pl.BlockSpec((None, 6, d), mod_map),
        pl.BlockSpec((1, d), const),
        pl.BlockSpec((d, W_IN_COLS), const),
        pl.BlockSpec((FNET_WIDTH, 2 * FNET_WIDTH), const),
        pl.BlockSpec((128, 2 * GLA_QK), const),
        pl.BlockSpec((1, 2 * GLA_QK), const),
        pl.BlockSpec((1, ATT_WIDTH), const),
        pl.BlockSpec((1, 128), const),
        pl.BlockSpec((ATT_WIDTH, ATT_WIDTH), const),
    ]
    args = [x, mod_l, nw, w_perm, cs, wup, bup, qn, kn, bd]
    if rope:
        in_specs += [pl.BlockSpec((tm, 128), lambda bi, i: (i, 0)), pl.BlockSpec((tm, 128), lambda bi, i: (i, 0))]
        args += list(rope_tabs)
    pending_rows = None
    if pending is not None:
        y2, wts3, row0, mod_prev = pending
        nr = tm // 128
        rps = n // 128
        assert 8 % nr == 0 and row0 % nr == 0 and rps % nr == 0
        pending_rows = (row0, rps)
        in_specs += [
            pl.BlockSpec((2, None, tm, d // 2), lambda bi, i: (0, bi, i, 0)),
            pl.BlockSpec((2, 8, 128), lambda bi, i: (0, (row0 + bi * rps + i * nr) // 8, 0)),
            pl.BlockSpec((None, 6, d), mod_map),
        ]
        args += [y2.reshape(2, b, n, d // 2), wts3, mod_prev]
    out_shape = (
        jax.ShapeDtypeStruct((2, n, b * FNET_WIDTH), BF16),
        jax.ShapeDtypeStruct((b, n, 2 * GLA_QK), F32),
        jax.ShapeDtypeStruct((b, n, GLA_WIDTH), BF16),
        jax.ShapeDtypeStruct((b, GLA_WIDTH, n), BF16),
        jax.ShapeDtypeStruct((b, n, GLA_WIDTH), BF16),
        jax.ShapeDtypeStruct((b, n, 2 * GLA_QK), F32),
        jax.ShapeDtypeStruct((b, n, ATT_WIDTH), BF16),
        jax.ShapeDtypeStruct((b, ATT_KV_HEADS, 128, n), BF16),
        jax.ShapeDtypeStruct((b, 2 * ATT_KV_HEADS, n, 128), BF16),
    )
    out_specs = (
        pl.BlockSpec((2, tm, FNET_WIDTH), lambda bi, i: (0, i, bi)),
        pl.BlockSpec((None, tm, 2 * GLA_QK), lambda bi, i: (bi, i, 0)),
        pl.BlockSpec((None, tm, GLA_WIDTH), lambda bi, i: (bi, i, 0)),
        pl.BlockSpec((None, GLA_WIDTH, tm), lambda bi, i: (bi, 0, i)),
        pl.BlockSpec((None, tm, GLA_WIDTH), lambda bi, i: (bi, i, 0)),
        pl.BlockSpec((None, tm, 2 * GLA_QK), lambda bi, i: (bi, i, 0)),
        pl.BlockSpec((None, tm, ATT_WIDTH), lambda bi, i: (bi, i, 0)),
        pl.BlockSpec((None, ATT_KV_HEADS, 128, tm), lambda bi, i: (bi, 0, 0, i)),
        pl.BlockSpec((None, 2 * ATT_KV_HEADS, tm, 128), lambda bi, i: (bi, 0, i, 0)),
    )
    if pending is not None:
        out_shape += (jax.ShapeDtypeStruct((b, n, d), F32),)
        out_specs += (pl.BlockSpec((None, tm, d), lambda bi, i: (bi, i, 0)),)
    return pl.pallas_call(
        functools.partial(_inproj_kernel, rope=rope, pending_rows=pending_rows),
        grid=(b, nt),
        in_specs=in_specs,
        out_specs=out_specs,
        out_shape=out_shape,
        compiler_params=_cparams(("parallel", "parallel")),
        name="inproj_rope" if rope else "inproj_ctx",
    )(*args)


def _seqdft_kernel(t_ref, ab_ref, o_ref):
    y = jnp.dot(t_ref[...], ab_ref[...], preferred_element_type=F32)
    for bb in range(o_ref.shape[0]):
        o_ref[bb] = y[:, bb * FNET_WIDTH:(bb + 1) * FNET_WIDTH].astype(BF16)


def _seqdft_call(table, ab, b):
    n = table.shape[0]
    tm = min(512, n)
    nb = 4 if b % 4 == 0 else (2 if b % 2 == 0 else 1)
    return pl.pallas_call(
        _seqdft_kernel,
        grid=(b // nb, n // tm),
        in_specs=[
            pl.BlockSpec((tm, 2 * n), lambda c, i: (i, 0)),
            pl.BlockSpec((2 * n, nb * FNET_WIDTH), lambda c, i: (0, c)),
        ],
        out_specs=pl.BlockSpec((nb, tm, FNET_WIDTH), lambda c, i: (c, i, 0)),
        out_shape=jax.ShapeDtypeStruct((b, n, FNET_WIDTH), BF16),
        compiler_params=_cparams(("parallel", "parallel")),
        name="seq_dft",
    )(table, ab)


def _gla_dir(qk, v, vt, a, s_in, fwd):
    p = GLA_PAIR
    r = lax.broadcasted_iota(jnp.int32, (p, p), 0)
    c = lax.broadcasted_iota(jnp.int32, (p, p), 1)
    same = (r // GLA_CHUNK) == (c // GLA_CHUNK)
    tri = same & ((c <= r) if fwd else (c >= r))
    row_lo = r < GLA_CHUNK
    rin = r % GLA_CHUNK

    q = qk[:, :GLA_QK] * (GLA_DK ** -0.5)
    k = qk[:, GLA_QK:]
    cum = a
    sh = 1
    while sh < GLA_CHUNK:
        if fwd:
            cum = cum + jnp.where(rin >= sh, pltpu.roll(cum, sh, 0), 0.0)
        else:
            cum = cum + jnp.where(rin < GLA_CHUNK - sh, pltpu.roll(cum, p - sh, 0), 0.0)
        sh *= 2
    if fwd:
        last0, last1 = cum[GLA_CHUNK - 1:GLA_CHUNK, :], cum[p - 1:p, :]
    else:
        last0, last1 = cum[0:1, :], cum[GLA_CHUNK:GLA_CHUNK + 1, :]
    lastb = jnp.where(row_lo, last0, last1)
    qt = q * jnp.exp(cum)
    kt = k * jnp.exp(-cum)
    kd = k * jnp.exp(lastb - cum)

    kt_b = kt.astype(BF16)
    zk = jnp.zeros_like(kt_b)
    ks = jnp.concatenate([jnp.where((c // GLA_DK) == hh, kt_b, zk) for hh in range(GLA_HEADS)], axis=0)
    att = _nt_dot(qt.astype(BF16), ks)
    tri4 = jnp.concatenate([tri] * GLA_HEADS, axis=1)
    att = jnp.where(tri4, att, 0.0).astype(BF16)
    col = lax.broadcasted_iota(jnp.int32, (p, GLA_WIDTH), 1)
    zv = jnp.zeros_like(v)
    vs = jnp.concatenate([jnp.where((col // GLA_DV) == hh, v, zv) for hh in range(GLA_HEADS)], axis=0)
    o_intra = jnp.dot(att, vs, preferred_element_type=F32)

    sr = lax.broadcasted_iota(jnp.int32, (GLA_WIDTH, GLA_QK), 0)
    sc = lax.broadcasted_iota(jnp.int32, (GLA_WIDTH, GLA_QK), 1)
    bdm = (sr // GLA_DV) == (sc // GLA_DK)
    first, second = (0, 1) if fwd else (1, 0)
    lasts = (last0, last1)
    in_chunk = (row_lo, jnp.logical_not(row_lo))
    kd2 = jnp.concatenate([jnp.where(in_chunk[0], kd, 0.0), jnp.where(in_chunk[1], kd, 0.0)], axis=1).astype(BF16)
    kvt2 = jnp.dot(vt, kd2, preferred_element_type=F32)
    kvt = (kvt2[:, :GLA_QK], kvt2[:, GLA_QK:])
    s_a = s_in
    s_b = s_a * jnp.exp(lasts[first]) + jnp.where(bdm, kvt[first], 0.0)
    s_c = s_b * jnp.exp(lasts[second]) + jnp.where(bdm, kvt[second], 0.0)
    q2 = jnp.concatenate([jnp.where(in_chunk[first], qt, 0.0), jnp.where(in_chunk[second], qt, 0.0)], axis=1)
    s2 = jnp.concatenate([s_a, s_b], axis=1).astype(BF16)
    o_inter = _nt_dot(q2.astype(BF16), s2)
    return o_intra + o_inter, s_c


def _gla_kernel(qkf, vf, vtf, laf, qkb, vb, vtb, lab, s0_ref, of_ref, ob_ref, sfin_ref, s_scr):
    i = pl.program_id(1)

    @pl.when(i == 0)
    def _():
        s_scr[...] = s0_ref[...]

    for gi in range(qkf.shape[0]):
        o1, sf = _gla_dir(qkf[gi], vf[gi], vtf[gi], laf[gi], s_scr[gi, 0], True)
        o2, sb = _gla_dir(qkb[gi], vb[gi], vtb[gi], lab[gi], s_scr[gi, 1], False)
        of_ref[gi] = o1.astype(of_ref.dtype)
        ob_ref[gi] = o2.astype(ob_ref.dtype)
        s_scr[gi, 0] = sf
        s_scr[gi, 1] = sb

    @pl.when(i == pl.num_programs(1) - 1)
    def _():
        sfin_ref[...] = s_scr[...]


def _gla_call(gqk, gv, gvt, la, s0):
    b, n, _ = gqk.shape
    p = GLA_PAIR
    npair = n // p
    gb = 4 if b % 4 == 0 else (2 if b % 2 == 0 else 1)
    fw = lambda bi, i: (bi, i, 0)
    bw = lambda bi, i: (bi, npair - 1 - i, 0)
    in_specs = [
        pl.BlockSpec((gb, p, 2 * GLA_QK), fw),
        pl.BlockSpec((gb, p, GLA_WIDTH), fw),
        pl.BlockSpec((gb, GLA_WIDTH, p), lambda bi, i: (bi, 0, i)),
        pl.BlockSpec((gb, p, GLA_QK), fw),
        pl.BlockSpec((gb, p, 2 * GLA_QK), bw),
        pl.BlockSpec((gb, p, GLA_WIDTH), bw),
        pl.BlockSpec((gb, GLA_WIDTH, p), lambda bi, i: (bi, 0, npair - 1 - i)),
        pl.BlockSpec((gb, p, GLA_QK), lambda bi, i: (bi, npair - 1 - i, 1)),
        pl.BlockSpec((gb, 2, GLA_WIDTH, GLA_QK), lambda bi, i: (bi, 0, 0, 0)),
    ]
    out_specs = (
        pl.BlockSpec((gb, p, GLA_WIDTH), fw),
        pl.BlockSpec((gb, p, GLA_WIDTH), bw),
        pl.BlockSpec((gb, 2, GLA_WIDTH, GLA_QK), lambda bi, i: (bi, 0, 0, 0)),
    )
    out_shape = (
        jax.ShapeDtypeStruct((b, n, GLA_WIDTH), BF16),
        jax.ShapeDtypeStruct((b, n, GLA_WIDTH), BF16),
        jax.ShapeDtypeStruct((b, 2, GLA_WIDTH, GLA_QK), F32),
    )
    return pl.pallas_call(
        _gla_kernel,
        grid=(b // gb, npair),
        in_specs=in_specs,
        out_specs=out_specs,
        out_shape=out_shape,
        scratch_shapes=[pltpu.VMEM((gb, 2, GLA_WIDTH, GLA_QK), F32)],
        compiler_params=_cparams(("parallel", "arbitrary")),
        name="gla_scan",
    )(gqk, gv, gvt, la, gqk, gv, gvt, la, s0)


def _attn_heads(q_ref, parts):
    tq = q_ref.shape[0]
    lane = lax.broadcasted_iota(jnp.int32, (tq, 128), 1)
    lo = lane < ATT_HDIM
    blocks = []
    for j in range(ATT_HEADS // 2):
        q128 = q_ref[:, 128 * j:128 * (j + 1)]
        g = (2 * j) // (ATT_HEADS // ATT_KV_HEADS)
        outs = []
        for half in range(2):
            qm = jnp.where(lo if half == 0 else jnp.logical_not(lo), q128, jnp.zeros_like(q128))
            ss = [jnp.dot(qm, kt_ref[g], preferred_element_type=F32) for kt_ref, _ in parts]
            m = functools.reduce(jnp.maximum, [jnp.max(s, axis=-1, keepdims=True) for s in ss])
            ps = [jnp.exp2(s - m).astype(BF16) for s in ss]
            pv = functools.reduce(
                lambda u, w: u + w,
                [jnp.dot(pp, vd_ref[2 * g + half], preferred_element_type=F32) for pp, (_, vd_ref) in zip(ps, parts)])
            den = pv[:, ATT_HDIM:ATT_HDIM + 1] if half == 0 else pv[:, 0:1]
            outs.append(pv / den)
        blocks.append(jnp.where(lo, outs[0], outs[1]).astype(BF16))
    return jnp.concatenate(blocks, axis=1)


def _attn_outproj_kernel(*refs, nparts):
    q_ref = refs[0]
    parts = [(refs[1 + 2 * i], refs[2 + 2 * i]) for i in range(nparts)]
    (f_ref, of_ref, ob_ref, gg_ref, x_ref, mod_ref, w_ref, gn_ref, bd_ref, nf_ref, wr_ref,
     xn_ref, h2_ref, lg_ref) = refs[1 + 2 * nparts:]
    att = _attn_heads(q_ref, parts)
    o = of_ref[...].astype(F32) + ob_ref[...].astype(F32)
    ms = jnp.dot((o * o).astype(BF16), bd_ref[...], preferred_element_type=F32) * (1.0 / GLA_DV)
    on = o * lax.rsqrt(ms + EPS) * gn_ref[...]
    g = gg_ref[...].astype(F32)
    gl = (on * (g * _sigmoid(g))).astype(BF16)
    ox = (jnp.dot(f_ref[...], w_ref[0:FNET_WIDTH, :], preferred_element_type=F32)
          + jnp.dot(gl, w_ref[FNET_WIDTH:FNET_WIDTH + GLA_WIDTH, :], preferred_element_type=F32)
          + jnp.dot(att, w_ref[FNET_WIDTH + GLA_WIDTH:, :], preferred_element_type=F32))
    xn = x_ref[...] + mod_ref[2:3, :] * ox
    xn_ref[...] = xn
    ms2 = jnp.mean(xn * xn, axis=-1, keepdims=True)
    h2 = xn * lax.rsqrt(ms2 + EPS) * nf_ref[...] * (1.0 + mod_ref[4:5, :]) + mod_ref[3:4, :]
    h2_ref[...] = _pack_bf16_pairs(h2)
    lg2 = _nt_dot(wr_ref[...], h2.astype(BF16))
    lg_ref[...] = lg2[:N_EXPERTS, :] + lg2[N_EXPERTS:, :]


def _attn_outproj_call(q, kv_parts, f, of, ob, gg, x, mod_l, mod_row, w_out, gn, bd, nf, wr):
    b, n, d = x.shape
    tm = min(512, n)
    nt = n // tm
    if mod_row is None:
        mod_map = lambda bi, i: (bi, 0, 0)
    else:
        mod_map = lambda bi, i: (mod_row, 0, 0)
    const = lambda bi, i: (0, 0)
    tok = lambda w: pl.BlockSpec((None, tm, w), lambda bi, i: (bi, i, 0))
    in_specs = [tok(ATT_WIDTH)]
    args = [q]
    for kt, vd in kv_parts:
        m = kt.shape[-1]
        in_specs.append(pl.BlockSpec((None, ATT_KV_HEADS, 128, m), lambda bi, i: (bi, 0, 0, 0)))
        in_specs.append(pl.BlockSpec((None, 2 * ATT_KV_HEADS, m, 128), lambda bi, i: (bi, 0, 0, 0)))
        args += [kt, vd]
    in_specs += [
        tok(FNET_WIDTH), tok(GLA_WIDTH), tok(GLA_WIDTH), tok(GLA_WIDTH), tok(d),
        pl.BlockSpec((None, 6, d), mod_map),
        pl.BlockSpec((d, d), const),
        pl.BlockSpec((1, GLA_WIDTH), const),
        pl.BlockSpec((GLA_WIDTH, GLA_WIDTH), const),
        pl.BlockSpec((1, d), const),
        pl.BlockSpec((2 * N_EXPERTS, d), const),
    ]
    args += [f, of, ob, gg, x, mod_l, w_out, gn, bd, nf, wr]
    out_specs = (
        tok(d), tok(d // 2),
        pl.BlockSpec((N_EXPERTS, tm), lambda bi, i: (0, bi * nt + i)),
    )
    out_shape = (
        jax.ShapeDtypeStruct((b, n, d), F32),
        jax.ShapeDtypeStruct((b, n, d // 2), jnp.int32),
        jax.ShapeDtypeStruct((N_EXPERTS, b * n), F32),
    )
    return pl.pallas_call(
        functools.partial(_attn_outproj_kernel, nparts=len(kv_parts)),
        grid=(b, nt),
        in_specs=in_specs,
        out_specs=out_specs,
        out_shape=out_shape,
        compiler_params=_cparams(("parallel", "parallel")),
        name="attn_outproj",
    )(*args)


def _route_kernel(b_ref, lg_ref, pos_ref, wt_ref, te_ref, nv_ref, *, tm):
    r = lg_ref.shape[1]
    s = [_sigmoid(lg_ref[e]) for e in range(N_EXPERTS)]
    sel = [s[e] + b_ref[e] for e in range(N_EXPERTS)]
    grp = []
    for g in range(N_GROUPS):
        a, b, c, d = sel[4 * g:4 * g + 4]
        hi1, lo1 = jnp.maximum(a, b), jnp.minimum(a, b)
        hi2, lo2 = jnp.maximum(c, d), jnp.minimum(c, d)
        m1 = jnp.maximum(hi1, hi2)
        m2 = jnp.maximum(jnp.minimum(hi1, hi2), jnp.maximum(lo1, lo2))
        grp.append(m1 + m2)
    one = jnp.ones_like(s[0])
    zero = jnp.zeros_like(s[0])
    msk = []
    for g in range(N_GROUPS):
        isg = one
        for g2 in range(N_GROUPS):
            if g2 < g:
                isg = isg * jnp.where(grp[g] > grp[g2], one, zero)
            elif g2 > g:
                isg = isg * jnp.where(grp[g] >= grp[g2], one, zero)
        for li in range(EXPERTS_PER_GROUP):
            e = 4 * g + li
            rank = zero
            for lj in range(EXPERTS_PER_GROUP):
                ej = 4 * g + lj
                if lj < li:
                    rank = rank + jnp.where(sel[ej] >= sel[e], one, zero)
                elif lj > li:
                    rank = rank + jnp.where(sel[ej] > sel[e], one, zero)
            msk.append(jnp.where(rank < 2.0, isg, zero))
    den = functools.reduce(lambda u, v: u + v, [msk[e] * s[e] for e in range(N_EXPERTS)])

    li_ = lax.broadcasted_iota(jnp.int32, (128, 128), 0)
    lj_ = lax.broadcasted_iota(jnp.int32, (128, 128), 1)
    upper = jnp.where(li_ < lj_, 1.0, 0.0).astype(BF16)
    ri_ = lax.broadcasted_iota(jnp.int32, (r, r), 0)
    rj_ = lax.broadcasted_iota(jnp.int32, (r, r), 1)
    lower = jnp.where(rj_ < ri_, 1.0, 0.0).astype(BF16)
    tiles = (1, te_ref.shape[1])
    tile_start = lax.broadcasted_iota(jnp.int32, tiles, 1).astype(F32) * float(tm)
    te = jnp.zeros(tiles, F32)
    seg = jnp.zeros(tiles, F32)
    nonempty = []
    off = jnp.zeros((1, 1), F32)
    seen = zero
    pos = [zero, zero]
    wts = [zero, zero]
    for e in range(N_EXPERTS):
        mb = msk[e].astype(BF16)
        lane_pre = jnp.dot(mb, upper, preferred_element_type=F32)
        row_pre = jnp.sum(jnp.dot(lower, mb, preferred_element_type=F32), axis=1, keepdims=True)
        cnt = jnp.sum(jnp.sum(msk[e], axis=1, keepdims=True), axis=0, keepdims=True)
        p_e = off + row_pre + lane_pre
        g_e = s[e] / den
        for kk in range(2):
            hit = msk[e] * jnp.where(seen == float(kk), one, zero)
            pos[kk] = pos[kk] + hit * p_e
            wts[kk] = wts[kk] + hit * g_e
        seen = seen + msk[e]
        off = off + jnp.floor((cnt + float(tm - 1)) * (1.0 / tm)) * float(tm)
        passed = jnp.where(tile_start >= off, 1.0, 0.0)
        te = te + passed
        nonempty.append(jnp.where(cnt > 0.0, 1.0, 0.0))
        seg = seg + nonempty[e] * passed
    for kk in range(2):
        pos_ref[kk] = pos[kk].astype(jnp.int32)
        wt_ref[kk] = wts[kk]
    te = jnp.minimum(te, float(N_EXPERTS - 1))
    nxt = jnp.full(tiles, -1.0, F32)
    for e in reversed(range(N_EXPERTS)):
        nxt = jnp.where(jnp.logical_and(nonempty[e] > 0.0, te < float(e)), float(e), nxt)
    te_ref[0:1, :] = te.astype(jnp.int32)
    te_ref[1:2, :] = seg.astype(jnp.int32)
    te_ref[2:3, :] = nxt.astype(jnp.int32)
    te_ref[3:, :] = jnp.zeros((te_ref.shape[0] - 3, te_ref.shape[1]), jnp.int32)
    nv_ref[...] = jnp.broadcast_to(off * (1.0 / tm), nv_ref.shape).astype(jnp.int32)


def _route_call(lgt, b_router, tm):
    n_tok = lgt.shape[1]
    r = n_tok // 128
    assert r * 128 == n_tok and r % 8 == 0 and 2 * n_tok // tm + N_EXPERTS <= 256
    lg3 = lgt.reshape(N_EXPERTS, r, 128)
    full3 = lambda k: pl.BlockSpec((k, r, 128), lambda: (0, 0, 0))
    pos, wts, te, nv = pl.pallas_call(
        functools.partial(_route_kernel, tm=tm),
        in_specs=[pl.BlockSpec(memory_space=pltpu.SMEM), full3(N_EXPERTS)],
        out_specs=(full3(2), full3(2), pl.BlockSpec((8, 256), lambda: (0, 0)), pl.BlockSpec((1, 128), lambda: (0, 0))),
        out_shape=(
            jax.ShapeDtypeStruct((2, r, 128), jnp.int32),
            jax.ShapeDtypeStruct((2, r, 128), F32),
            jax.ShapeDtypeStruct((8, 256), jnp.int32),
            jax.ShapeDtypeStruct((1, 128), jnp.int32),
        ),
        compiler_params=pltpu.CompilerParams(vmem_limit_bytes=VMEM_LIMIT),
        name="route",
    )(b_router, lg3)
    return pos.reshape(2, n_tok), wts, te[:3].reshape(3 * 256), nv[0, :1]


SC_CORES = 2
SC_SUBCORES = 16
SC_WORKERS = SC_CORES * SC_SUBCORES
SC_CHUNK = 32


def _sc_mesh():
    return plsc.VectorSubcoreMesh(core_axis_name="c", subcore_axis_name="s",
                                  num_cores=SC_CORES, num_subcores=SC_SUBCORES)


def _sc_steps(n_rows):
    per_w = n_rows // SC_WORKERS
    steps = per_w // SC_CHUNK
    assert per_w * SC_WORKERS == n_rows and steps * SC_CHUNK == per_w and steps % 2 == 0, n_rows
    return per_w, steps


def _sc_gather_rows(table, idx):
    p = idx.shape[0]
    d = table.shape[1]
    per_w, steps = _sc_steps(p)
    idx3 = idx.reshape(SC_WORKERS, steps, SC_CHUNK)

    @functools.partial(
        pl.kernel, mesh=_sc_mesh(),
        out_type=jax.ShapeDtypeStruct((p, d), table.dtype),
        scratch_types=[
            pltpu.VMEM((steps, SC_CHUNK), jnp.int32),
            pltpu.VMEM((SC_CHUNK, d), table.dtype),
            pltpu.VMEM((SC_CHUNK, d), table.dtype),
            pltpu.SemaphoreType.DMA, pltpu.SemaphoreType.DMA,
            pltpu.SemaphoreType.DMA, pltpu.SemaphoreType.DMA,
        ],
        name="sc_gather_rows",
    )
    def k(table_hbm, idx_hbm, out_hbm, idx_v, buf0, buf1, g0, g1, w0, w1):
        wid = lax.axis_index("s") * SC_CORES + lax.axis_index("c")
        base = wid * per_w
        pltpu.sync_copy(idx_hbm.at[wid], idx_v)

        def gather(s, buf, sem):
            return pltpu.make_async_copy(table_hbm.at[idx_v.at[s]], buf, sem)

        def write(s, buf, sem):
            return pltpu.make_async_copy(buf, out_hbm.at[pl.ds(base + s * SC_CHUNK, SC_CHUNK)], sem)

        gather(0, buf0, g0).start()

        @pl.loop(0, steps, step=2)
        def _(s):
            gather(s + 1, buf1, g1).start()
            gather(s, buf0, g0).wait()
            write(s, buf0, w0).start()
            write(s, buf0, w0).wait()

            @pl.when(s + 2 < steps)
            def _():
                gather(s + 2, buf0, g0).start()

            gather(s + 1, buf1, g1).wait()
            write(s + 1, buf1, w1).start()
            write(s + 1, buf1, w1).wait()

    return k(table, idx3)


def _sc_dispatch(srcs, poss, p_rows):
    d = srcs[0].shape[1]
    dt = srcs[0].dtype
    plans = [_sc_steps(src.shape[0]) for src in srcs]
    idxs = [pos.reshape(2, SC_WORKERS, st, SC_CHUNK) for pos, (_, st) in zip(poss, plans)]
    nseg = len(srcs)
    scratch = [pltpu.VMEM((2, st, SC_CHUNK), jnp.int32) for _, st in plans]
    scratch += [pltpu.VMEM((SC_CHUNK, d), dt), pltpu.VMEM((SC_CHUNK, d), dt)]
    scratch += [pltpu.SemaphoreType.DMA] * 6

    @functools.partial(
        pl.kernel, mesh=_sc_mesh(),
        out_type=jax.ShapeDtypeStruct((p_rows, d), dt),
        scratch_types=scratch,
        name="sc_dispatch",
    )
    def k(*refs):
        src_hbm = refs[:nseg]
        idx_hbm = refs[nseg:2 * nseg]
        out_hbm = refs[2 * nseg]
        idx_v = refs[2 * nseg + 1:3 * nseg + 1]
        buf0, buf1, r0, r1, a0, a1, b0, b1 = refs[3 * nseg + 1:]
        wid = lax.axis_index("s") * SC_CORES + lax.axis_index("c")
        for seg in range(nseg):
            per_w, steps = plans[seg]
            base = wid * per_w
            for kk in range(2):
                pltpu.sync_copy(idx_hbm[seg].at[kk, wid], idx_v[seg].at[kk])

            def read(s, buf, sem, seg=seg, base=base):
                return pltpu.make_async_copy(src_hbm[seg].at[pl.ds(base + s * SC_CHUNK, SC_CHUNK)], buf, sem)

            def scat(kk, s, buf, sem, seg=seg):
                return pltpu.make_async_copy(buf, out_hbm.at[idx_v[seg].at[kk, s]], sem)

            read(0, buf0, r0).start()

            @pl.loop(0, steps, step=2)
            def _(s, read=read, scat=scat, steps=steps):
                read(s + 1, buf1, r1).start()
                read(s, buf0, r0).wait()
                scat(0, s, buf0, a0).start()
                scat(1, s, buf0, b0).start()
                scat(0, s, buf0, a0).wait()
                scat(1, s, buf0, b0).wait()

                @pl.when(s + 2 < steps)
                def _():
                    read(s + 2, buf0, r0).start()

                read(s + 1, buf1, r1).wait()
                scat(0, s + 1, buf1, a1).start()
                scat(1, s + 1, buf1, b1).start()
                scat(0, s + 1, buf1, a1).wait()
                scat(1, s + 1, buf1, b1).wait()

    return k(*srcs, *idxs)


MOE_TM = 512


MOE_TILE_SLOTS = 256


def _experts_kernel(te_ref, nv_ref, xs_ref, wg_hbm, wu_hbm, wd_hbm, ys_ref, wg_v, wu_v, wd_v, sem, *, layer):
    i = pl.program_id(0)
    valid = i < nv_ref[0]
    e = te_ref[i]
    first = jnp.logical_or(i == 0, e != te_ref[jnp.maximum(i - 1, 0)])
    slot = lax.rem(te_ref[MOE_TILE_SLOTS + i], 2)
    nxt = te_ref[2 * MOE_TILE_SLOTS + i]

    def weight_copies(expert, s):
        return (pltpu.make_async_copy(wg_hbm.at[layer, expert], wg_v.at[s], sem.at[s, 0]),
                pltpu.make_async_copy(wu_hbm.at[layer, expert], wu_v.at[s], sem.at[s, 1]),
                pltpu.make_async_copy(wd_hbm.at[layer, expert], wd_v.at[s], sem.at[s, 2]))

    @pl.when(jnp.logical_and(valid, i == 0))
    def _():
        for cp in weight_copies(e, slot):
            cp.start()

    @pl.when(jnp.logical_and(valid, first))
    def _():
        for cp in weight_copies(e, slot):
            cp.wait()

        @pl.when(nxt >= 0)
        def _():
            for cp in weight_copies(nxt, 1 - slot):
                cp.start()

    @pl.when(valid)
    def _():
        h = _unpack_bf16_pairs(xs_ref[...]).astype(BF16)
        half = D_EXPERT // 2
        y = None
        for j in range(2):
            sl = slice(j * half, (j + 1) * half)
            a = jnp.dot(h, wg_v[slot, :, sl].astype(BF16), preferred_element_type=F32)
            u = jnp.dot(h, wu_v[slot, :, sl].astype(BF16), preferred_element_type=F32)
            t = ((a * _sigmoid(a)) * u).astype(BF16)
            yj = jnp.dot(t, wd_v[slot, sl, :].astype(BF16), preferred_element_type=F32)
            y = yj if y is None else y + yj
        ys_ref[...] = _pack_bf16_pairs(y)


def _experts_call(xs, te, nv, wg, wu, wd, layer):
    p_rows, dh = xs.shape
    d = 2 * dh
    tm = MOE_TM
    nt = p_rows // tm
    assert nt <= MOE_TILE_SLOTS and te.shape == (3 * MOE_TILE_SLOTS,)
    row = lambda i, te_r, nv_r: (jnp.minimum(i, nv_r[0] - 1), 0)
    grid_spec = pltpu.PrefetchScalarGridSpec(
        num_scalar_prefetch=2,
        grid=(nt,),
        in_specs=[
            pl.BlockSpec((tm, dh), row),
            pl.BlockSpec(memory_space=pl.ANY),
            pl.BlockSpec(memory_space=pl.ANY),
            pl.BlockSpec(memory_space=pl.ANY),
        ],
        out_specs=pl.BlockSpec((tm, dh), row),
        scratch_shapes=[
            pltpu.VMEM((2, d, D_EXPERT), F32),
            pltpu.VMEM((2, d, D_EXPERT), F32),
            pltpu.VMEM((2, D_EXPERT, d), F32),
            pltpu.SemaphoreType.DMA((2, 3)),
        ],
    )
    return pl.pallas_call(
        functools.partial(_experts_kernel, layer=layer),
        grid_spec=grid_spec,
        out_shape=jax.ShapeDtypeStruct((p_rows, dh), jnp.int32),
        compiler_params=_cparams(("arbitrary",)),
        name="moe_experts",
    )(te, nv, xs, wg, wu, wd)


COMBINE_TM = 1024


def _combine_kernel(y_ref, wt_ref, x_ref, mod_ref, fn_ref, o_ref, *, final):
    xo = x_ref[...] + mod_ref[5:6, :] * _moe_mix(y_ref, wt_ref[0], wt_ref[1])
    if final:
        ms = jnp.mean(xo * xo, axis=-1, keepdims=True)
        xo = xo * lax.rsqrt(ms + EPS) * fn_ref[...]
    o_ref[...] = xo


def _combine_call(y2, wts3, row0, x, mod_l, mod_row, fn, final):
    b, n, d = x.shape
    n_tok = b * n
    tm = COMBINE_TM
    assert n_tok % tm == 0 and row0 % 8 == 0 and (mod_row is not None or n % tm == 0)
    if mod_row is None:
        mod_map = lambda i: ((i * tm) // n, 0, 0)
    else:
        mod_map = lambda i: (mod_row, 0, 0)
    out = pl.pallas_call(
        functools.partial(_combine_kernel, final=final),
        grid=(n_tok // tm,),
        in_specs=[
            pl.BlockSpec((2, tm, d // 2), lambda i: (0, i, 0)),
            pl.BlockSpec((2, tm // 128, 128), lambda i: (0, row0 // 8 + i, 0)),
            pl.BlockSpec((tm, d), lambda i: (i, 0)),
            pl.BlockSpec((None, 6, d), mod_map),
            pl.BlockSpec((1, d), lambda i: (0, 0)),
        ],
        out_specs=pl.BlockSpec((tm, d), lambda i: (i, 0)),
        out_shape=jax.ShapeDtypeStruct((n_tok, d), F32),
        compiler_params=_cparams(("parallel",)),
        name="moe_combine",
    )(y2, wts3, x.reshape(n_tok, d), mod_l, fn)
    return out.reshape(b, n, d)


def _moe_sparse(h_list, lg_list, x_list, mod_l, mod_rows, b_router, wg, wu, wd, layer, fn, final):
    d = h_list[0].shape[-1]
    sizes = [h.shape[0] * h.shape[1] for h in h_list]
    n_tok = sum(sizes)
    lgt = lg_list[0] if len(lg_list) == 1 else jnp.concatenate(lg_list, axis=1)
    pos, wts, te, nv = _route_call(lgt, b_router, MOE_TM)
    p_rows = 2 * n_tok + N_EXPERTS * MOE_TM
    offs = np.cumsum([0] + sizes)
    poss = [pos[:, offs[i]:offs[i + 1]] for i in range(len(sizes))]
    xs = _sc_dispatch([h.reshape(-1, d) for h in h_list], poss, p_rows)
    ys = _experts_call(xs, te, nv, wg, wu, wd, layer)
    outs = []
    for i, x in enumerate(x_list):
        y2 = _sc_gather_rows(ys, poss[i].reshape(-1)).reshape(2, sizes[i], d)
        row0 = int(offs[i]) // 128
        if final:
            outs.append(_combine_call(y2, wts, row0, x, mod_l, mod_rows[i], fn, True))
        else:
            outs.append((y2, wts, row0, mod_l))
    return outs


def _winprep_kernel(w_ref, o_ref):
    gd0 = C_AQ
    gdw = 2 * GLA_GATE_RANK
    tail = W_IN_REF_COLS - gd0 - gdw
    o_ref[:, 0:gd0] = w_ref[:, 0:gd0].astype(BF16)
    o_ref[:, gd0:gd0 + tail] = w_ref[:, gd0 + gdw:W_IN_REF_COLS].astype(BF16)
    o_ref[:, C_GD:C_GD + gdw] = w_ref[:, gd0:gd0 + gdw].astype(BF16)
    o_ref[:, C_GD + gdw:] = jnp.zeros((o_ref.shape[0], W_IN_COLS - C_GD - gdw), BF16)


def _winprep_call(w_in):
    depth, d, cols = w_in.shape
    assert cols == W_IN_REF_COLS and C_GD == C_AQ + cols - C_AQ - 2 * GLA_GATE_RANK
    return pl.pallas_call(
        _winprep_kernel,
        grid=(depth,),
        in_specs=[pl.BlockSpec((None, d, cols), lambda l: (l, 0, 0))],
        out_specs=pl.BlockSpec((None, d, W_IN_COLS), lambda l: (l, 0, 0)),
        out_shape=jax.ShapeDtypeStruct((depth, d, W_IN_COLS), BF16),
        compiler_params=_cparams(("parallel",)),
        name="w_in_prep",
    )(w_in)


def _gate_up_weights(w_up, b_up):
    z = jnp.zeros((GLA_GATE_RANK, GLA_QK), w_up.dtype)
    top = jnp.concatenate([w_up[0], z], axis=1)
    mid = jnp.concatenate([z, w_up[1]], axis=1)
    pad = jnp.zeros((128 - 2 * GLA_GATE_RANK, 2 * GLA_QK), w_up.dtype)
    return jnp.concatenate([top, mid, pad], axis=0).astype(BF16), b_up.reshape(1, 2 * GLA_QK)


def kernel(x, c, ctx, c_ctx, w_ada, b_ada, norm_mix, norm_ffn, w_in, w_gla_gate_up, b_gla_gate, gla_norm, q_norm,
           k_norm, w_out, w_router, b_router, w_exp_gate, w_exp_up, w_exp_down, final_norm):
    b, n, d = x.shape
    m = ctx.shape[1]
    depth = w_ada.shape[0]
    assert d == D_MODEL and n % GLA_PAIR == 0 and m % GLA_PAIR == 0 and n % GRID_W == 0

    rows = ((b + 1 + 7) // 8) * 8
    cv = jnp.concatenate([c, c_ctx[None, :], jnp.zeros((rows - b - 1, d), F32)], axis=0)
    mod = _ada_call(cv, w_ada, b_ada).reshape(depth, rows, 6, d)

    cs = jnp.asarray(_channel_dft_table()).astype(BF16)
    tab_x = jnp.asarray(_seq_dft_table(n)).astype(BF16)
    tab_c = jnp.asarray(_seq_dft_table(m)).astype(BF16)
    rope_tabs = tuple(jnp.asarray(t) for t in _rope_tables(n))
    bd512 = jnp.asarray(_blockdiag_ones(ATT_WIDTH, ATT_HDIM)).astype(BF16)
    bd256 = jnp.asarray(_blockdiag_ones(GLA_WIDTH, GLA_DV)).astype(BF16)
    wr_t = w_router.T
    wr_hi = wr_t.astype(BF16)
    wrh = jnp.concatenate([wr_hi, (wr_t - wr_hi.astype(F32)).astype(BF16)], axis=0)
    fn = final_norm.reshape(1, d)

    w_in_perm = _winprep_call(w_in)

    xc = ctx
    pend_x = pend_c = None
    for l in range(depth):
        ctx_out = l < depth - 1
        mod_l = mod[l]
        w_perm = w_in_perm[l]
        wup, bup = _gate_up_weights(w_gla_gate_up[l], b_gla_gate[l])
        nw = norm_mix[l].reshape(1, d)
        nf = norm_ffn[l].reshape(1, d)
        qn = jnp.tile(q_norm[l], ATT_HEADS).reshape(1, ATT_WIDTH)
        kn = jnp.tile(k_norm[l], ATT_KV_HEADS).reshape(1, 128)
        gn = jnp.tile(gla_norm[l], GLA_HEADS).reshape(1, GLA_WIDTH)
        wo = w_out[l].astype(BF16)

        outs_c = _inproj_call(xc, mod_l, b, nw, w_perm, cs, wup, bup, qn, kn, bd512, None, pend_c)
        outs_x = _inproj_call(x, mod_l, None, nw, w_perm, cs, wup, bup, qn, kn, bd512, rope_tabs, pend_x)
        (ab_c, gqk_c, gv_c, gvt_c, gg_c, la_c, q_c, kt_c, vd_c) = outs_c[:9]
        (ab_x, gqk_x, gv_x, gvt_x, gg_x, la_x, q_x, kt_x, vd_x) = outs_x[:9]
        if pend_x is not None:
            x, xc = outs_x[9], outs_c[9]

        s_zero = jnp.zeros((b, 2, GLA_WIDTH, GLA_QK), F32)
        of_c, ob_c, s_fin = _gla_call(gqk_c, gv_c, gvt_c, la_c, s_zero)
        of_x, ob_x, _ = _gla_call(gqk_x, gv_x, gvt_x, la_x, s_fin)

        f_x = _seqdft_call(tab_x, ab_x.reshape(2 * n, b * FNET_WIDTH), b)
        x, h2_x, lg_x = _attn_outproj_call(q_x, [(kt_c, vd_c), (kt_x, vd_x)], f_x, of_x, ob_x, gg_x, x,
                                           mod_l, None, wo, gn, bd256, nf, wrh)

        if ctx_out:
            f_c = _seqdft_call(tab_c, ab_c.reshape(2 * m, b * FNET_WIDTH), b)
            xc, h2_c, lg_c = _attn_outproj_call(q_c, [(kt_c, vd_c)], f_c, of_c, ob_c, gg_c, xc,
                                                mod_l, b, wo, gn, bd256, nf, wrh)

        final = l == depth - 1
        wexp = (w_exp_gate, w_exp_up, w_exp_down, l)
        if ctx_out:
            res = _moe_sparse([h2_x, h2_c], [lg_x, lg_c], [x, xc], mod_l, [None, b], b_router, *wexp, fn, final)
        else:
            res = _moe_sparse([h2_x], [lg_x], [x], mod_l, [None], b_router, *wexp, fn, final)
        if final:
            x = res[0]
        else:
            pend_x, pend_c = res
    return x
```

```python
import functools

import numpy as np
import jax
import jax.numpy as jnp
from jax import lax
from jax.experimental import pallas as pl
from jax.experimental.pallas import tpu as pltpu
from jax.experimental.pallas import tpu_sc as plsc

F32 = jnp.float32
BF16 = jnp.bfloat16

D_MODEL = 1024
GRID_W = 64
EPS = 1e-6
LOG2E = 1.4426950408889634

FNET_WIDTH = 256
FNET_GROUPS = 4
FNET_GDIM = 64

GLA_HEADS = 4
GLA_DV = 64
GLA_DK = 32
GLA_WIDTH = 256
GLA_QK = 128
GLA_GATE_RANK = 16
GLA_GATE_NORM = 16.0
GLA_CHUNK = 64
GLA_PAIR = 2 * GLA_CHUNK

ATT_HEADS = 8
ATT_KV_HEADS = 2
ATT_HDIM = 64
ATT_WIDTH = 512
ROPE_FREQS = 16
ROPE_THETA = 10000.0

N_EXPERTS = 16
N_GROUPS = 4
EXPERTS_PER_GROUP = 4
D_EXPERT = 512

C_U = 0
C_GQ = 256
C_GK = 384
C_GV = 512
C_GG = 768
C_AQ = 1024
C_AK = 1536
C_GD = 1792
W_IN_COLS = 1920
W_IN_REF_COLS = 1824

VMEM_LIMIT = 56 * 1024 * 1024


def _cparams(sem):
    return pltpu.CompilerParams(dimension_semantics=sem, vmem_limit_bytes=VMEM_LIMIT)


def _sigmoid(x):
    return 1.0 / (1.0 + jnp.exp(-x))


def _pack_bf16_pairs(x):
    blocks = []
    for t in range(x.shape[1] // 256):
        lo = lax.bitcast_convert_type(x[:, 256 * t:256 * t + 128].astype(BF16).astype(F32), jnp.uint32)
        hi = lax.bitcast_convert_type(x[:, 256 * t + 128:256 * t + 256].astype(BF16).astype(F32), jnp.uint32)
        blocks.append((lo >> 16) | (hi & jnp.uint32(0xFFFF0000)))
    return lax.bitcast_convert_type(jnp.concatenate(blocks, axis=1), jnp.int32)


def _unpack_bf16_pairs(p):
    u = lax.bitcast_convert_type(p, jnp.uint32)
    blocks = []
    for t in range(p.shape[1] // 128):
        word = u[:, 128 * t:128 * (t + 1)]
        blocks += [lax.bitcast_convert_type(word << 16, F32),
                   lax.bitcast_convert_type(word & jnp.uint32(0xFFFF0000), F32)]
    return jnp.concatenate(blocks, axis=1)


def _token_columns(w):
    nr = w.shape[0]
    tm = nr * 128
    lane = lax.broadcasted_iota(jnp.int32, (tm, 128), 1)
    row = lax.broadcasted_iota(jnp.int32, (tm, 128), 0)
    wb = jnp.concatenate([jnp.broadcast_to(w[r:r + 1, :], (128, 128)) for r in range(nr)], axis=0)
    return jnp.sum(jnp.where(lane == (row % 128), wb, 0.0), axis=1, keepdims=True)


def _moe_mix(y_ref, w0, w1):
    return (_token_columns(w0) * _unpack_bf16_pairs(y_ref[0])
            + _token_columns(w1) * _unpack_bf16_pairs(y_ref[1]))


def _nt_dot(a, b):
    return lax.dot_general(a, b, (((1,), (1,)), ((), ())), preferred_element_type=F32)


@functools.lru_cache(maxsize=None)
def _channel_dft_table():
    j = np.arange(FNET_GDIM)
    ang = 2.0 * np.pi * ((j[:, None] * j[None, :]) % FNET_GDIM) / FNET_GDIM
    c = np.cos(ang) / np.sqrt(FNET_GDIM)
    s = np.sin(ang) / np.sqrt(FNET_GDIM)
    out = np.zeros((FNET_WIDTH, 2 * FNET_WIDTH), np.float64)
    for g in range(FNET_GROUPS):
        sl = slice(g * FNET_GDIM, (g + 1) * FNET_GDIM)
        out[sl, sl] = c
        out[sl, FNET_WIDTH + g * FNET_GDIM:FNET_WIDTH + (g + 1) * FNET_GDIM] = s
    return out.astype(np.float32)


@functools.lru_cache(maxsize=None)
def _seq_dft_table(n):
    j = np.arange(n, dtype=np.int64)
    ang = 2.0 * np.pi * ((j[:, None] * j[None, :]) % n) / n
    return np.concatenate([np.cos(ang), -np.sin(ang)], axis=1).astype(np.float32) / np.float32(np.sqrt(n))


@functools.lru_cache(maxsize=None)
def _rope_tables(n):
    rows = n // GRID_W
    row = np.repeat(np.arange(rows), GRID_W).astype(np.float64)
    col = np.tile(np.arange(GRID_W), rows).astype(np.float64)
    inv = ROPE_THETA ** (-np.arange(ROPE_FREQS, dtype=np.float64) * 2.0 / (2 * ROPE_FREQS))
    ar = row[:, None] * inv[None, :]
    ac = col[:, None] * inv[None, :]
    cos = np.concatenate([np.cos(ar), np.cos(ar), np.cos(ac), np.cos(ac)], axis=1)
    sin = np.concatenate([-np.sin(ar), np.sin(ar), -np.sin(ac), np.sin(ac)], axis=1)
    return (np.tile(cos, (1, 2)).astype(np.float32), np.tile(sin, (1, 2)).astype(np.float32))


@functools.lru_cache(maxsize=None)
def _blockdiag_ones(width, blk):
    i = np.arange(width)
    return (i[:, None] // blk == i[None, :] // blk).astype(np.float32)


def _ada_kernel(cv_ref, w_ref, b_ref, o_ref):
    cv = cv_ref[...]
    a = (cv * _sigmoid(cv)).astype(BF16)
    o_ref[...] = jnp.dot(a, w_ref[...].astype(BF16), preferred_element_type=F32) + b_ref[...]


def _ada_call(cv, w_ada, b_ada):
    depth, d, d6 = w_ada.shape
    tn = 1536
    rows = cv.shape[0]
    return pl.pallas_call(
        _ada_kernel,
        grid=(depth, d6 // tn),
        in_specs=[
            pl.BlockSpec((rows, d), lambda l, j: (0, 0)),
            pl.BlockSpec((None, d, tn), lambda l, j: (l, 0, j)),
            pl.BlockSpec((None, 1, tn), lambda l, j: (l, 0, j)),
        ],
        out_specs=pl.BlockSpec((None, rows, tn), lambda l, j: (l, 0, j)),
        out_shape=jax.ShapeDtypeStruct((depth, rows, d6), F32),
        compiler_params=_cparams(("parallel", "parallel")),
        name="ada_mod",
    )(cv, w_ada, b_ada.reshape(depth, 1, d6))


def _swap16(x):
    lane = lax.broadcasted_iota(jnp.int32, x.shape, 1)
    first = (lane % 32) < 16
    return jnp.where(first, pltpu.roll(x, 112, 1), pltpu.roll(x, 16, 1))


def _head_rms(x, bd, w):
    ms = jnp.dot((x * x).astype(BF16), bd, preferred_element_type=F32) * (1.0 / ATT_HDIM)
    return x * lax.rsqrt(ms + EPS) * w


def _inproj_kernel(*refs, rope, pending_rows):
    refs = list(refs)
    x_ref, mod_ref, nw_ref, w_ref, cs_ref, wup_ref, bup_ref, qn_ref, kn_ref, bd_ref = refs[:10]
    del refs[:10]
    if rope:
        cos_ref, sin_ref = refs[:2]
        del refs[:2]
    if pending_rows is not None:
        y_ref, wt_ref, modp_ref = refs[:3]
        del refs[:3]
    ab_ref, gqk_ref, gv_ref, gvt_ref, gg_ref, la_ref, q_ref, kt_ref, vd_ref = refs[:9]
    x = x_ref[...]
    if pending_rows is not None:
        xnew_ref = refs[9]
        row0, rows_per_sample = pending_rows
        nr = x.shape[0] // 128
        r = row0 + pl.program_id(0) * rows_per_sample + pl.program_id(1) * nr
        sub = lax.rem(r, 8)
        w = [wt_ref[kk, 0:nr, :] for kk in range(2)]
        for blk in range(1, 8 // nr):
            w = [jnp.where(sub == blk * nr, wt_ref[kk, blk * nr:(blk + 1) * nr, :], w[kk]) for kk in range(2)]
        x = x + modp_ref[5:6, :] * _moe_mix(y_ref, w[0], w[1])
        xnew_ref[...] = x
    ms = jnp.mean(x * x, axis=-1, keepdims=True)
    y = x * lax.rsqrt(ms + EPS) * nw_ref[...]
    h = y * (1.0 + mod_ref[1:2, :]) + mod_ref[0:1, :]
    hb = h.astype(BF16)

    def proj(c0, width):
        return jnp.dot(hb, w_ref[:, c0:c0 + width], preferred_element_type=F32)

    uab = jnp.dot(proj(C_U, FNET_WIDTH).astype(BF16), cs_ref[...], preferred_element_type=F32)
    ab_ref[0] = uab[:, :FNET_WIDTH].astype(BF16)
    ab_ref[1] = uab[:, FNET_WIDTH:].astype(BF16)

    gqk_ref[...] = proj(C_GQ, 2 * GLA_QK)
    gv = proj(C_GV, GLA_WIDTH)
    gv_ref[...] = gv.astype(BF16)
    gvt_ref[...] = gv.T.astype(BF16)
    gg_ref[...] = proj(C_GG, GLA_WIDTH).astype(BF16)
    pre = jnp.dot(proj(C_GD, 128).astype(BF16), wup_ref[...], preferred_element_type=F32) + bup_ref[...]
    la_ref[...] = (jnp.minimum(pre, 0.0) - jnp.log1p(jnp.exp(-jnp.abs(pre)))) * (1.0 / GLA_GATE_NORM)

    bd = bd_ref[...]
    q = _head_rms(proj(C_AQ, ATT_WIDTH), bd, qn_ref[...])
    kv = proj(C_AK, 256)
    k = _head_rms(kv[:, :128], bd[:128, :128], kn_ref[...])
    if rope:
        cos = cos_ref[...]
        sin = sin_ref[...]
        q = jnp.concatenate(
            [q[:, s:s + 128] * cos + _swap16(q[:, s:s + 128]) * sin for s in range(0, ATT_WIDTH, 128)], axis=1)
        k = k * cos + _swap16(k) * sin
    q_ref[...] = (q * (ATT_HDIM ** -0.5 * LOG2E)).astype(BF16)
    v = kv[:, 128:]
    lo = lax.broadcasted_iota(jnp.int32, k.shape, 1) < ATT_HDIM
    k_sw = pltpu.roll(k, ATT_HDIM, 1)
    v_sw = pltpu.roll(v, ATT_HDIM, 1)
    kt_ref[0] = jnp.where(lo, k, k_sw).T.astype(BF16)
    kt_ref[1] = jnp.where(lo, k_sw, k).T.astype(BF16)
    vd_ref[0] = jnp.where(lo, v, 1.0).astype(BF16)
    vd_ref[1] = jnp.where(lo, 1.0, v_sw).astype(BF16)
    vd_ref[2] = jnp.where(lo, v_sw, 1.0).astype(BF16)
    vd_ref[3] = jnp.where(lo, 1.0, v).astype(BF16)


def _inproj_call(x, mod_l, mod_row, nw, w_perm, cs, wup, bup, qn, kn, bd, rope_tabs, pending=None):
    b, n, d = x.shape
    tm = min(1024, n)
    nt = n // tm
    rope = rope_tabs is not None
    if mod_row is None:
        mod_map = lambda bi, i: (bi, 0, 0)
    else:
        mod_map = lambda bi, i: (mod_row, 0, 0)
    const = lambda bi, i: (0, 0)
    in_specs = [
        pl.BlockSpec((None, tm, d), lambda bi, i: (bi, i, 0)),
        pl.BlockSpec((None, 6, d), mod_map),
        pl.BlockSpec((1, d), const),
        pl.BlockSpec((d, W_IN_COLS), const),
        pl.BlockSpec((FNET_WIDTH, 2 * FNET_WIDTH), const),
        pl.BlockSpec((128, 2 * GLA_QK), const),
        pl.BlockSpec((1, 2 * GLA_QK), const),
        pl.BlockSpec((1, ATT_WIDTH), const),
        pl.BlockSpec((1, 128), const),
        pl.BlockSpec((ATT_WIDTH, ATT_WIDTH), const),
    ]
    args = [x, mod_l, nw, w_perm, cs, wup, bup, qn, kn, bd]
    if rope:
        in_specs += [pl.BlockSpec((tm, 128), lambda bi, i: (i, 0)), pl.BlockSpec((tm, 128), lambda bi, i: (i, 0))]
        args += list(rope_tabs)
    pending_rows = None
    if pending is not None:
        y2, wts3, row0, mod_prev = pending
        nr = tm // 128
        rps = n // 128
        assert 8 % nr == 0 and row0 % nr == 0 and rps % nr == 0
        pending_rows = (row0, rps)
        in_specs += [
            pl.BlockSpec((2, None, tm, d // 2), lambda bi, i: (0, bi, i, 0)),
            pl.BlockSpec((2, 8, 128), lambda bi, i: (0, (row0 + bi * rps + i * nr) // 8, 0)),
            pl.BlockSpec((None, 6, d), mod_map),
        ]
        args += [y2.reshape(2, b, n, d // 2), wts3, mod_prev]
    out_shape = (
        jax.ShapeDtypeStruct((2, n, b * FNET_WIDTH), BF16),
        jax.ShapeDtypeStruct((b, n, 2 * GLA_QK), F32),
        jax.ShapeDtypeStruct((b, n, GLA_WIDTH), BF16),
        jax.ShapeDtypeStruct((b, GLA_WIDTH, n), BF16),
        jax.ShapeDtypeStruct((b, n, GLA_WIDTH), BF16),
        jax.ShapeDtypeStruct((b, n, 2 * GLA_QK), F32),
        jax.ShapeDtypeStruct((b, n, ATT_WIDTH), BF16),
        jax.ShapeDtypeStruct((b, ATT_KV_HEADS, 128, n), BF16),
        jax.ShapeDtypeStruct((b, 2 * ATT_KV_HEADS, n, 128), BF16),
    )
    out_specs = (
        pl.BlockSpec((2, tm, FNET_WIDTH), lambda bi, i: (0, i, bi)),
        pl.BlockSpec((None, tm, 2 * GLA_QK), lambda bi, i: (bi, i, 0)),
        pl.BlockSpec((None, tm, GLA_WIDTH), lambda bi, i: (bi, i, 0)),
        pl.BlockSpec((None, GLA_WIDTH, tm), lambda bi, i: (bi, 0, i)),
        pl.BlockSpec((None, tm, GLA_WIDTH), lambda bi, i: (bi, i, 0)),
        pl.BlockSpec((None, tm, 2 * GLA_QK), lambda bi, i: (bi, i, 0)),
        pl.BlockSpec((None, tm, ATT_WIDTH), lambda bi, i: (bi, i, 0)),
        pl.BlockSpec((None, ATT_KV_HEADS, 128, tm), lambda bi, i: (bi, 0, 0, i)),
        pl.BlockSpec((None, 2 * ATT_KV_HEADS, tm, 128), lambda bi, i: (bi, 0, i, 0)),
    )
    if pending is not None:
        out_shape += (jax.ShapeDtypeStruct((b, n, d), F32),)
        out_specs += (pl.BlockSpec((None, tm, d), lambda bi, i: (bi, i, 0)),)
    return pl.pallas_call(
        functools.partial(_inproj_kernel, rope=rope, pending_rows=pending_rows),
        grid=(b, nt),
        in_specs=in_specs,
        out_specs=out_specs,
        out_shape=out_shape,
        compiler_params=_cparams(("parallel", "parallel")),
        name="inproj_rope" if rope else "inproj_ctx",
    )(*args)


def _seqdft_kernel(t_ref, ab_ref, o_ref):
    y = jnp.dot(t_ref[...], ab_ref[...], preferred_element_type=F32)
    for bb in range(o_ref.shape[0]):
        o_ref[bb] = y[:, bb * FNET_WIDTH:(bb + 1) * FNET_WIDTH].astype(BF16)


def _seqdft_call(table, ab, b):
    n = table.shape[0]
    tm = min(512, n)
    nb = 4 if b % 4 == 0 else (2 if b % 2 == 0 else 1)
    return pl.pallas_call(
        _seqdft_kernel,
        grid=(b // nb, n // tm),
        in_specs=[
            pl.BlockSpec((tm, 2 * n), lambda c, i: (i, 0)),
            pl.BlockSpec((2 * n, nb * FNET_WIDTH), lambda c, i: (0, c)),
        ],
        out_specs=pl.BlockSpec((nb, tm, FNET_WIDTH), lambda c, i: (c, i, 0)),
        out_shape=jax.ShapeDtypeStruct((b, n, FNET_WIDTH), BF16),
        compiler_params=_cparams(("parallel", "parallel")),
        name="seq_dft",
    )(table, ab)


def _gla_dir(qk, v, vt, a, s_in, fwd):
    p = GLA_PAIR
    r = lax.broadcasted_iota(jnp.int32, (p, p), 0)
    c = lax.broadcasted_iota(jnp.int32, (p, p), 1)
    same = (r // GLA_CHUNK) == (c // GLA_CHUNK)
    tri = same & ((c <= r) if fwd else (c >= r))
    row_lo = r < GLA_CHUNK
    rin = r % GLA_CHUNK

    q = qk[:, :GLA_QK] * (GLA_DK ** -0.5)
    k = qk[:, GLA_QK:]
    cum = a
    sh = 1
    while sh < GLA_CHUNK:
        if fwd:
            cum = cum + jnp.where(rin >= sh, pltpu.roll(cum, sh, 0), 0.0)
        else:
            cum = cum + jnp.where(rin < GLA_CHUNK - sh, pltpu.roll(cum, p - sh, 0), 0.0)
        sh *= 2
    if fwd:
        last0, last1 = cum[GLA_CHUNK - 1:GLA_CHUNK, :], cum[p - 1:p, :]
    else:
        last0, last1 = cum[0:1, :], cum[GLA_CHUNK:GLA_CHUNK + 1, :]
    lastb = jnp.where(row_lo, last0, last1)
    qt = q * jnp.exp(cum)
    kt = k * jnp.exp(-cum)
    kd = k * jnp.exp(lastb - cum)

    kt_b = kt.astype(BF16)
    zk = jnp.zeros_like(kt_b)
    ks = jnp.concatenate([jnp.where((c // GLA_DK) == hh, kt_b, zk) for hh in range(GLA_HEADS)], axis=0)
    att = _nt_dot(qt.astype(BF16), ks)
    tri4 = jnp.concatenate([tri] * GLA_HEADS, axis=1)
    att = jnp.where(tri4, att, 0.0).astype(BF16)
    col = lax.broadcasted_iota(jnp.int32, (p, GLA_WIDTH), 1)
    zv = jnp.zeros_like(v)
    vs = jnp.concatenate([jnp.where((col // GLA_DV) == hh, v, zv) for hh in range(GLA_HEADS)], axis=0)
    o_intra = jnp.dot(att, vs, preferred_element_type=F32)

    sr = lax.broadcasted_iota(jnp.int32, (GLA_WIDTH, GLA_QK), 0)
    sc = lax.broadcasted_iota(jnp.int32, (GLA_WIDTH, GLA_QK), 1)
    bdm = (sr // GLA_DV) == (sc // GLA_DK)
    first, second = (0, 1) if fwd else (1, 0)
    lasts = (last0, last1)
    in_chunk = (row_lo, jnp.logical_not(row_lo))
    kd2 = jnp.concatenate([jnp.where(in_chunk[0], kd, 0.0), jnp.where(in_chunk[1], kd, 0.0)], axis=1).astype(BF16)
    kvt2 = jnp.dot(vt, kd2, preferred_element_type=F32)
    kvt = (kvt2[:, :GLA_QK], kvt2[:, GLA_QK:])
    s_a = s_in
    s_b = s_a * jnp.exp(lasts[first]) + jnp.where(bdm, kvt[first], 0.0)
    s_c = s_b * jnp.exp(lasts[second]) + jnp.where(bdm, kvt[second], 0.0)
    q2 = jnp.concatenate([jnp.where(in_chunk[first], qt, 0.0), jnp.where(in_chunk[second], qt, 0.0)], axis=1)
    s2 = jnp.concatenate([s_a, s_b], axis=1).astype(BF16)
    o_inter = _nt_dot(q2.astype(BF16), s2)
    return o_intra + o_inter, s_c


def _gla_kernel(qkf, vf, vtf, laf, qkb, vb, vtb, lab, s0_ref, of_ref, ob_ref, sfin_ref, s_scr):
    i = pl.program_id(1)

    @pl.when(i == 0)
    def _():
        s_scr[...] = s0_ref[...]

    for gi in range(qkf.shape[0]):
        o1, sf = _gla_dir(qkf[gi], vf[gi], vtf[gi], laf[gi], s_scr[gi, 0], True)
        o2, sb = _gla_dir(qkb[gi], vb[gi], vtb[gi], lab[gi], s_scr[gi, 1], False)
        of_ref[gi] = o1.astype(of_ref.dtype)
        ob_ref[gi] = o2.astype(ob_ref.dtype)
        s_scr[gi, 0] = sf
        s_scr[gi, 1] = sb

    @pl.when(i == pl.num_programs(1) - 1)
    def _():
        sfin_ref[...] = s_scr[...]


def _gla_call(gqk, gv, gvt, la, s0):
    b, n, _ = gqk.shape
    p = GLA_PAIR
    npair = n // p
    gb = 4 if b % 4 == 0 else (2 if b % 2 == 0 else 1)
    fw = lambda bi, i: (bi, i, 0)
    bw = lambda bi, i: (bi, npair - 1 - i, 0)
    in_specs = [
        pl.BlockSpec((gb, p, 2 * GLA_QK), fw),
        pl.BlockSpec((gb, p, GLA_WIDTH), fw),
        pl.BlockSpec((gb, GLA_WIDTH, p), lambda bi, i: (bi, 0, i)),
        pl.BlockSpec((gb, p, GLA_QK), fw),
        pl.BlockSpec((gb, p, 2 * GLA_QK), bw),
        pl.BlockSpec((gb, p, GLA_WIDTH), bw),
        pl.BlockSpec((gb, GLA_WIDTH, p), lambda bi, i: (bi, 0, npair - 1 - i)),
        pl.BlockSpec((gb, p, GLA_QK), lambda bi, i: (bi, npair - 1 - i, 1)),
        pl.BlockSpec((gb, 2, GLA_WIDTH, GLA_QK), lambda bi, i: (bi, 0, 0, 0)),
    ]
    out_specs = (
        pl.BlockSpec((gb, p, GLA_WIDTH), fw),
        pl.BlockSpec((gb, p, GLA_WIDTH), bw),
        pl.BlockSpec((gb, 2, GLA_WIDTH, GLA_QK), lambda bi, i: (bi, 0, 0, 0)),
    )
    out_shape = (
        jax.ShapeDtypeStruct((b, n, GLA_WIDTH), BF16),
        jax.ShapeDtypeStruct((b, n, GLA_WIDTH), BF16),
        jax.ShapeDtypeStruct((b, 2, GLA_WIDTH, GLA_QK), F32),
    )
    return pl.pallas_call(
        _gla_kernel,
        grid=(b // gb, npair),
        in_specs=in_specs,
        out_specs=out_specs,
        out_shape=out_shape,
        scratch_shapes=[pltpu.VMEM((gb, 2, GLA_WIDTH, GLA_QK), F32)],
        compiler_params=_cparams(("parallel", "arbitrary")),
        name="gla_scan",
    )(gqk, gv, gvt, la, gqk, gv, gvt, la, s0)


def _attn_heads(q_ref, parts):
    tq = q_ref.shape[0]
    lane = lax.broadcasted_iota(jnp.int32, (tq, 128), 1)
    lo = lane < ATT_HDIM
    blocks = []
    for j in range(ATT_HEADS // 2):
        q128 = q_ref[:, 128 * j:128 * (j + 1)]
        g = (2 * j) // (ATT_HEADS // ATT_KV_HEADS)
        outs = []
        for half in range(2):
            qm = jnp.where(lo if half == 0 else jnp.logical_not(lo), q128, jnp.zeros_like(q128))
            ss = [jnp.dot(qm, kt_ref[g], preferred_element_type=F32) for kt_ref, _ in parts]
            m = functools.reduce(jnp.maximum, [jnp.max(s, axis=-1, keepdims=True) for s in ss])
            ps = [jnp.exp2(s - m).astype(BF16) for s in ss]
            pv = functools.reduce(
                lambda u, w: u + w,
                [jnp.dot(pp, vd_ref[2 * g + half], preferred_element_type=F32) for pp, (_, vd_ref) in zip(ps, parts)])
            den = pv[:, ATT_HDIM:ATT_HDIM + 1] if half == 0 else pv[:, 0:1]
            outs.append(pv / den)
        blocks.append(jnp.where(lo, outs[0], outs[1]).astype(BF16))
    return jnp.concatenate(blocks, axis=1)


def _attn_outproj_kernel(*refs, nparts):
    q_ref = refs[0]
    parts = [(refs[1 + 2 * i], refs[2 + 2 * i]) for i in range(nparts)]
    (f_ref, of_ref, ob_ref, gg_ref, x_ref, mod_ref, w_ref, gn_ref, bd_ref, nf_ref, wr_ref,
     xn_ref, h2_ref, lg_ref) = refs[1 + 2 * nparts:]
    att = _attn_heads(q_ref, parts)
    o = of_ref[...].astype(F32) + ob_ref[...].astype(F32)
    ms = jnp.dot((o * o).astype(BF16), bd_ref[...], preferred_element_type=F32) * (1.0 / GLA_DV)
    on = o * lax.rsqrt(ms + EPS) * gn_ref[...]
    g = gg_ref[...].astype(F32)
    gl = (on * (g * _sigmoid(g))).astype(BF16)
    ox = (jnp.dot(f_ref[...], w_ref[0:FNET_WIDTH, :], preferred_element_type=F32)
          + jnp.dot(gl, w_ref[FNET_WIDTH:FNET_WIDTH + GLA_WIDTH, :], preferred_element_type=F32)
          + jnp.dot(att, w_ref[FNET_WIDTH + GLA_WIDTH:, :], preferred_element_type=F32))
    xn = x_ref[...] + mod_ref[2:3, :] * ox
    xn_ref[...] = xn
    ms2 = jnp.mean(xn * xn, axis=-1, keepdims=True)
    h2 = xn * lax.rsqrt(ms2 + EPS) * nf_ref[...] * (1.0 + mod_ref[4:5, :]) + mod_ref[3:4, :]
    h2_ref[...] = _pack_bf16_pairs(h2)
    lg2 = _nt_dot(wr_ref[...], h2.astype(BF16))
    lg_ref[...] = lg2[:N_EXPERTS, :] + lg2[N_EXPERTS:, :]


def _attn_outproj_call(q, kv_parts, f, of, ob, gg, x, mod_l, mod_row, w_out, gn, bd, nf, wr):
    b, n, d = x.shape
    tm = min(1024, n)
    nt = n // tm
    if mod_row is None:
        mod_map = lambda bi, i: (bi, 0, 0)
    else:
        mod_map = lambda bi, i: (mod_row, 0, 0)
    const = lambda bi, i: (0, 0)
    tok = lambda w: pl.BlockSpec((None, tm, w), lambda bi, i: (bi, i, 0))
    in_specs = [tok(ATT_WIDTH)]
    args = [q]
    for kt, vd in kv_parts:
        m = kt.shape[-1]
        in_specs.append(pl.BlockSpec((None, ATT_KV_HEADS, 128, m), lambda bi, i: (bi, 0, 0, 0)))
        in_specs.append(pl.BlockSpec((None, 2 * ATT_KV_HEADS, m, 128), lambda bi, i: (bi, 0, 0, 0)))
        args += [kt, vd]
    in_specs += [
        tok(FNET_WIDTH), tok(GLA_WIDTH), tok(GLA_WIDTH), tok(GLA_WIDTH), tok(d),
        pl.BlockSpec((None, 6, d), mod_map),
        pl.BlockSpec((d, d), const),
        pl.BlockSpec((1, GLA_WIDTH), const),
        pl.BlockSpec((GLA_WIDTH, GLA_WIDTH), const),
        pl.BlockSpec((1, d), const),
        pl.BlockSpec((2 * N_EXPERTS, d), const),
    ]
    args += [f, of, ob, gg, x, mod_l, w_out, gn, bd, nf, wr]
    out_specs = (
        tok(d), tok(d // 2),
        pl.BlockSpec((N_EXPERTS, tm), lambda bi, i: (0, bi * nt + i)),
    )
    out_shape = (
        jax.ShapeDtypeStruct((b, n, d), F32),
        jax.ShapeDtypeStruct((b, n, d // 2), jnp.int32),
        jax.ShapeDtypeStruct((N_EXPERTS, b * n), F32),
    )
    return pl.pallas_call(
        functools.partial(_attn_outproj_kernel, nparts=len(kv_parts)),
        grid=(b, nt),
        in_specs=in_specs,
        out_specs=out_specs,
        out_shape=out_shape,
        compiler_params=_cparams(("parallel", "parallel")),
        name="attn_outproj",
    )(*args)


def _route_kernel(b_ref, lg_ref, pos_ref, wt_ref, te_ref, nv_ref, *, tm):
    r = lg_ref.shape[1]
    s = [_sigmoid(lg_ref[e]) for e in range(N_EXPERTS)]
    sel = [s[e] + b_ref[e] for e in range(N_EXPERTS)]
    grp = []
    for g in range(N_GROUPS):
        a, b, c, d = sel[4 * g:4 * g + 4]
        hi1, lo1 = jnp.maximum(a, b), jnp.minimum(a, b)
        hi2, lo2 = jnp.maximum(c, d), jnp.minimum(c, d)
        m1 = jnp.maximum(hi1, hi2)
        m2 = jnp.maximum(jnp.minimum(hi1, hi2), jnp.maximum(lo1, lo2))
        grp.append(m1 + m2)
    one = jnp.ones_like(s[0])
    zero = jnp.zeros_like(s[0])
    msk = []
    for g in range(N_GROUPS):
        isg = one
        for g2 in range(N_GROUPS):
            if g2 < g:
                isg = isg * jnp.where(grp[g] > grp[g2], one, zero)
            elif g2 > g:
                isg = isg * jnp.where(grp[g] >= grp[g2], one, zero)
        for li in range(EXPERTS_PER_GROUP):
            e = 4 * g + li
            rank = zero
            for lj in range(EXPERTS_PER_GROUP):
                ej = 4 * g + lj
                if lj < li:
                    rank = rank + jnp.where(sel[ej] >= sel[e], one, zero)
                elif lj > li:
                    rank = rank + jnp.where(sel[ej] > sel[e], one, zero)
            msk.append(jnp.where(rank < 2.0, isg, zero))
    den = functools.reduce(lambda u, v: u + v, [msk[e] * s[e] for e in range(N_EXPERTS)])

    li_ = lax.broadcasted_iota(jnp.int32, (128, 128), 0)
    lj_ = lax.broadcasted_iota(jnp.int32, (128, 128), 1)
    upper = jnp.where(li_ < lj_, 1.0, 0.0).astype(BF16)
    ri_ = lax.broadcasted_iota(jnp.int32, (r, r), 0)
    rj_ = lax.broadcasted_iota(jnp.int32, (r, r), 1)
    lower = jnp.where(rj_ < ri_, 1.0, 0.0).astype(BF16)
    tiles = (1, te_ref.shape[1])
    tile_start = lax.broadcasted_iota(jnp.int32, tiles, 1).astype(F32) * float(tm)
    te = jnp.zeros(tiles, F32)
    seg = jnp.zeros(tiles, F32)
    nonempty = []
    off = jnp.zeros((1, 1), F32)
    seen = zero
    pos = [zero, zero]
    wts = [zero, zero]
    for e in range(N_EXPERTS):
        mb = msk[e].astype(BF16)
        lane_pre = jnp.dot(mb, upper, preferred_element_type=F32)
        row_pre = jnp.sum(jnp.dot(lower, mb, preferred_element_type=F32), axis=1, keepdims=True)
        cnt = jnp.sum(jnp.sum(msk[e], axis=1, keepdims=True), axis=0, keepdims=True)
        p_e = off + row_pre + lane_pre
        g_e = s[e] / den
        for kk in range(2):
            hit = msk[e] * jnp.where(seen == float(kk), one, zero)
            pos[kk] = pos[kk] + hit * p_e
            wts[kk] = wts[kk] + hit * g_e
        seen = seen + msk[e]
        off = off + jnp.floor((cnt + float(tm - 1)) * (1.0 / tm)) * float(tm)
        passed = jnp.where(tile_start >= off, 1.0, 0.0)
        te = te + passed
        nonempty.append(jnp.where(cnt > 0.0, 1.0, 0.0))
        seg = seg + nonempty[e] * passed
    for kk in range(2):
        pos_ref[kk] = pos[kk].astype(jnp.int32)
        wt_ref[kk] = wts[kk]
    te = jnp.minimum(te, float(N_EXPERTS - 1))
    nxt = jnp.full(tiles, -1.0, F32)
    for e in reversed(range(N_EXPERTS)):
        nxt = jnp.where(jnp.logical_and(nonempty[e] > 0.0, te < float(e)), float(e), nxt)
    te_ref[0:1, :] = te.astype(jnp.int32)
    te_ref[1:2, :] = seg.astype(jnp.int32)
    te_ref[2:3, :] = nxt.astype(jnp.int32)
    te_ref[3:, :] = jnp.zeros((te_ref.shape[0] - 3, te_ref.shape[1]), jnp.int32)
    nv_ref[...] = jnp.broadcast_to(off * (1.0 / tm), nv_ref.shape).astype(jnp.int32)


def _route_call(lgt, b_router, tm):
    n_tok = lgt.shape[1]
    r = n_tok // 128
    assert r * 128 == n_tok and r % 8 == 0 and 2 * n_tok // tm + N_EXPERTS <= 256
    lg3 = lgt.reshape(N_EXPERTS, r, 128)
    full3 = lambda k: pl.BlockSpec((k, r, 128), lambda: (0, 0, 0))
    pos, wts, te, nv = pl.pallas_call(
        functools.partial(_route_kernel, tm=tm),
        in_specs=[pl.BlockSpec(memory_space=pltpu.SMEM), full3(N_EXPERTS)],
        out_specs=(full3(2), full3(2), pl.BlockSpec((8, 256), lambda: (0, 0)), pl.BlockSpec((1, 128), lambda: (0, 0))),
        out_shape=(
            jax.ShapeDtypeStruct((2, r, 128), jnp.int32),
            jax.ShapeDtypeStruct((2, r, 128), F32),
            jax.ShapeDtypeStruct((8, 256), jnp.int32),
            jax.ShapeDtypeStruct((1, 128), jnp.int32),
        ),
        compiler_params=pltpu.CompilerParams(vmem_limit_bytes=VMEM_LIMIT),
        name="route",
    )(b_router, lg3)
    return pos.reshape(2, n_tok), wts, te[:3].reshape(3 * 256), nv[0, :1]


SC_CORES = 2
SC_SUBCORES = 16
SC_WORKERS = SC_CORES * SC_SUBCORES
SC_CHUNK = 32


def _sc_mesh():
    return plsc.VectorSubcoreMesh(core_axis_name="c", subcore_axis_name="s",
                                  num_cores=SC_CORES, num_subcores=SC_SUBCORES)


def _sc_steps(n_rows):
    per_w = n_rows // SC_WORKERS
    steps = per_w // SC_CHUNK
    assert per_w * SC_WORKERS == n_rows and steps * SC_CHUNK == per_w and steps % 2 == 0, n_rows
    return per_w, steps


def _sc_gather_rows(table, idx):
    p = idx.shape[0]
    d = table.shape[1]
    per_w, steps = _sc_steps(p)
    idx3 = idx.reshape(SC_WORKERS, steps, SC_CHUNK)

    @functools.partial(
        pl.kernel, mesh=_sc_mesh(),
        out_type=jax.ShapeDtypeStruct((p, d), table.dtype),
        scratch_types=[
            pltpu.VMEM((steps, SC_CHUNK), jnp.int32),
            pltpu.VMEM((SC_CHUNK, d), table.dtype),
            pltpu.VMEM((SC_CHUNK, d), table.dtype),
            pltpu.SemaphoreType.DMA, pltpu.SemaphoreType.DMA,
            pltpu.SemaphoreType.DMA, pltpu.SemaphoreType.DMA,
        ],
        name="sc_gather_rows",
    )
    def k(table_hbm, idx_hbm, out_hbm, idx_v, buf0, buf1, g0, g1, w0, w1):
        wid = lax.axis_index("s") * SC_CORES + lax.axis_index("c")
        base = wid * per_w
        pltpu.sync_copy(idx_hbm.at[wid], idx_v)

        def gather(s, buf, sem):
            return pltpu.make_async_copy(table_hbm.at[idx_v.at[s]], buf, sem)

        def write(s, buf, sem):
            return pltpu.make_async_copy(buf, out_hbm.at[pl.ds(base + s * SC_CHUNK, SC_CHUNK)], sem)

        gather(0, buf0, g0).start()

        @pl.loop(0, steps, step=2)
        def _(s):
            gather(s + 1, buf1, g1).start()
            gather(s, buf0, g0).wait()
            write(s, buf0, w0).start()
            write(s, buf0, w0).wait()

            @pl.when(s + 2 < steps)
            def _():
                gather(s + 2, buf0, g0).start()

            gather(s + 1, buf1, g1).wait()
            write(s + 1, buf1, w1).start()
            write(s + 1, buf1, w1).wait()

    return k(table, idx3)


def _sc_dispatch(srcs, poss, p_rows):
    d = srcs[0].shape[1]
    dt = srcs[0].dtype
    plans = [_sc_steps(src.shape[0]) for src in srcs]
    idxs = [pos.reshape(2, SC_WORKERS, st, SC_CHUNK) for pos, (_, st) in zip(poss, plans)]
    nseg = len(srcs)
    scratch = [pltpu.VMEM((2, st, SC_CHUNK), jnp.int32) for _, st in plans]
    scratch += [pltpu.VMEM((SC_CHUNK, d), dt), pltpu.VMEM((SC_CHUNK, d), dt)]
    scratch += [pltpu.SemaphoreType.DMA] * 6

    @functools.partial(
        pl.kernel, mesh=_sc_mesh(),
        out_type=jax.ShapeDtypeStruct((p_rows, d), dt),
        scratch_types=scratch,
        name="sc_dispatch",
    )
    def k(*refs):
        src_hbm = refs[:nseg]
        idx_hbm = refs[nseg:2 * nseg]
        out_hbm = refs[2 * nseg]
        idx_v = refs[2 * nseg + 1:3 * nseg + 1]
        buf0, buf1, r0, r1, a0, a1, b0, b1 = refs[3 * nseg + 1:]
        wid = lax.axis_index("s") * SC_CORES + lax.axis_index("c")
        for seg in range(nseg):
            per_w, steps = plans[seg]
            base = wid * per_w
            for kk in range(2):
                pltpu.sync_copy(idx_hbm[seg].at[kk, wid], idx_v[seg].at[kk])

            def read(s, buf, sem, seg=seg, base=base):
                return pltpu.make_async_copy(src_hbm[seg].at[pl.ds(base + s * SC_CHUNK, SC_CHUNK)], buf, sem)

            def scat(kk, s, buf, sem, seg=seg):
                return pltpu.make_async_copy(buf, out_hbm.at[idx_v[seg].at[kk, s]], sem)

            read(0, buf0, r0).start()

            @pl.loop(0, steps, step=2)
            def _(s, read=read, scat=scat, steps=steps):
                read(s + 1, buf1, r1).start()
                read(s, buf0, r0).wait()
                scat(0, s, buf0, a0).start()
                scat(1, s, buf0, b0).start()
                scat(0, s, buf0, a0).wait()
                scat(1, s, buf0, b0).wait()

                @pl.when(s + 2 < steps)
                def _():
                    read(s + 2, buf0, r0).start()

                read(s + 1, buf1, r1).wait()
                scat(0, s + 1, buf1, a1).start()
                scat(1, s + 1, buf1, b1).start()
                scat(0, s + 1, buf1, a1).wait()
                scat(1, s + 1, buf1, b1).wait()

    return k(*srcs, *idxs)


MOE_TM = 512


MOE_TILE_SLOTS = 256


def _experts_kernel(te_ref, nv_ref, xs_ref, wg_hbm, wu_hbm, wd_hbm, ys_ref, wg_v, wu_v, wd_v, sem, *, layer):
    i = pl.program_id(0)
    valid = i < nv_ref[0]
    e = te_ref[i]
    first = jnp.logical_or(i == 0, e != te_ref[jnp.maximum(i - 1, 0)])
    slot = lax.rem(te_ref[MOE_TILE_SLOTS + i], 2)
    nxt = te_ref[2 * MOE_TILE_SLOTS + i]

    def weight_copies(expert, s):
        return (pltpu.make_async_copy(wg_hbm.at[layer, expert], wg_v.at[s], sem.at[s, 0]),
                pltpu.make_async_copy(wu_hbm.at[layer, expert], wu_v.at[s], sem.at[s, 1]),
                pltpu.make_async_copy(wd_hbm.at[layer, expert], wd_v.at[s], sem.at[s, 2]))

    @pl.when(jnp.logical_and(valid, i == 0))
    def _():
        for cp in weight_copies(e, slot):
            cp.start()

    @pl.when(jnp.logical_and(valid, first))
    def _():
        for cp in weight_copies(e, slot):
            cp.wait()

        @pl.when(nxt >= 0)
        def _():
            for cp in weight_copies(nxt, 1 - slot):
                cp.start()

    @pl.when(valid)
    def _():
        h = _unpack_bf16_pairs(xs_ref[...]).astype(BF16)
        half = D_EXPERT // 2
        y = None
        for j in range(2):
            sl = slice(j * half, (j + 1) * half)
            a = jnp.dot(h, wg_v[slot, :, sl].astype(BF16), preferred_element_type=F32)
            u = jnp.dot(h, wu_v[slot, :, sl].astype(BF16), preferred_element_type=F32)
            t = ((a * _sigmoid(a)) * u).astype(BF16)
            yj = jnp.dot(t, wd_v[slot, sl, :].astype(BF16), preferred_element_type=F32)
            y = yj if y is None else y + yj
        ys_ref[...] = _pack_bf16_pairs(y)


def _experts_call(xs, te, nv, wg, wu, wd, layer):
    p_rows, dh = xs.shape
    d = 2 * dh
    tm = MOE_TM
    nt = p_rows // tm
    assert nt <= MOE_TILE_SLOTS and te.shape == (3 * MOE_TILE_SLOTS,)
    row = lambda i, te_r, nv_r: (jnp.minimum(i, nv_r[0] - 1), 0)
    grid_spec = pltpu.PrefetchScalarGridSpec(
        num_scalar_prefetch=2,
        grid=(nt,),
        in_specs=[
            pl.BlockSpec((tm, dh), row),
            pl.BlockSpec(memory_space=pl.ANY),
            pl.BlockSpec(memory_space=pl.ANY),
            pl.BlockSpec(memory_space=pl.ANY),
        ],
        out_specs=pl.BlockSpec((tm, dh), row),
        scratch_shapes=[
            pltpu.VMEM((2, d, D_EXPERT), F32),
            pltpu.VMEM((2, d, D_EXPERT), F32),
            pltpu.VMEM((2, D_EXPERT, d), F32),
            pltpu.SemaphoreType.DMA((2, 3)),
        ],
    )
    return pl.pallas_call(
        functools.partial(_experts_kernel, layer=layer),
        grid_spec=grid_spec,
        out_shape=jax.ShapeDtypeStruct((p_rows, dh), jnp.int32),
        compiler_params=_cparams(("arbitrary",)),
        name="moe_experts",
    )(te, nv, xs, wg, wu, wd)


COMBINE_TM = 1024


def _combine_kernel(y_ref, wt_ref, x_ref, mod_ref, fn_ref, o_ref, *, final):
    xo = x_ref[...] + mod_ref[5:6, :] * _moe_mix(y_ref, wt_ref[0], wt_ref[1])
    if final:
        ms = jnp.mean(xo * xo, axis=-1, keepdims=True)
        xo = xo * lax.rsqrt(ms + EPS) * fn_ref[...]
    o_ref[...] = xo


def _combine_call(y2, wts3, row0, x, mod_l, mod_row, fn, final):
    b, n, d = x.shape
    n_tok = b * n
    tm = COMBINE_TM
    assert n_tok % tm == 0 and row0 % 8 == 0 and (mod_row is not None or n % tm == 0)
    if mod_row is None:
        mod_map = lambda i: ((i * tm) // n, 0, 0)
    else:
        mod_map = lambda i: (mod_row, 0, 0)
    out = pl.pallas_call(
        functools.partial(_combine_kernel, final=final),
        grid=(n_tok // tm,),
        in_specs=[
            pl.BlockSpec((2, tm, d // 2), lambda i: (0, i, 0)),
            pl.BlockSpec((2, tm // 128, 128), lambda i: (0, row0 // 8 + i, 0)),
            pl.BlockSpec((tm, d), lambda i: (i, 0)),
            pl.BlockSpec((None, 6, d), mod_map),
            pl.BlockSpec((1, d), lambda i: (0, 0)),
        ],
        out_specs=pl.BlockSpec((tm, d), lambda i: (i, 0)),
        out_shape=jax.ShapeDtypeStruct((n_tok, d), F32),
        compiler_params=_cparams(("parallel",)),
        name="moe_combine",
    )(y2, wts3, x.reshape(n_tok, d), mod_l, fn)
    return out.reshape(b, n, d)


def _moe_sparse(h_list, lg_list, x_list, mod_l, mod_rows, b_router, wg, wu, wd, layer, fn, final):
    d = h_list[0].shape[-1]
    sizes = [h.shape[0] * h.shape[1] for h in h_list]
    n_tok = sum(sizes)
    lgt = lg_list[0] if len(lg_list) == 1 else jnp.concatenate(lg_list, axis=1)
    pos, wts, te, nv = _route_call(lgt, b_router, MOE_TM)
    p_rows = 2 * n_tok + N_EXPERTS * MOE_TM
    offs = np.cumsum([0] + sizes)
    poss = [pos[:, offs[i]:offs[i + 1]] for i in range(len(sizes))]
    xs = _sc_dispatch([h.reshape(-1, d) for h in h_list], poss, p_rows)
    ys = _experts_call(xs, te, nv, wg, wu, wd, layer)
    outs = []
    for i, x in enumerate(x_list):
        y2 = _sc_gather_rows(ys, poss[i].reshape(-1)).reshape(2, sizes[i], d)
        row0 = int(offs[i]) // 128
        if final:
            outs.append(_combine_call(y2, wts, row0, x, mod_l, mod_rows[i], fn, True))
        else:
            outs.append((y2, wts, row0, mod_l))
    return outs


def _winprep_kernel(w_ref, o_ref):
    gd0 = C_AQ
    gdw = 2 * GLA_GATE_RANK
    tail = W_IN_REF_COLS - gd0 - gdw
    o_ref[:, 0:gd0] = w_ref[:, 0:gd0].astype(BF16)
    o_ref[:, gd0:gd0 + tail] = w_ref[:, gd0 + gdw:W_IN_REF_COLS].astype(BF16)
    o_ref[:, C_GD:C_GD + gdw] = w_ref[:, gd0:gd0 + gdw].astype(BF16)
    o_ref[:, C_GD + gdw:] = jnp.zeros((o_ref.shape[0], W_IN_COLS - C_GD - gdw), BF16)


def _winprep_call(w_in):
    depth, d, cols = w_in.shape
    assert cols == W_IN_REF_COLS and C_GD == C_AQ + cols - C_AQ - 2 * GLA_GATE_RANK
    return pl.pallas_call(
        _winprep_kernel,
        grid=(depth,),
        in_specs=[pl.BlockSpec((None, d, cols), lambda l: (l, 0, 0))],
        out_specs=pl.BlockSpec((None, d, W_IN_COLS), lambda l: (l, 0, 0)),
        out_shape=jax.ShapeDtypeStruct((depth, d, W_IN_COLS), BF16),
        compiler_params=_cparams(("parallel",)),
        name="w_in_prep",
    )(w_in)


def _gate_up_weights(w_up, b_up):
    z = jnp.zeros((GLA_GATE_RANK, GLA_QK), w_up.dtype)
    top = jnp.concatenate([w_up[0], z], axis=1)
    mid = jnp.concatenate([z, w_up[1]], axis=1)
    pad = jnp.zeros((128 - 2 * GLA_GATE_RANK, 2 * GLA_QK), w_up.dtype)
    return jnp.concatenate([top, mid, pad], axis=0).astype(BF16), b_up.reshape(1, 2 * GLA_QK)


def kernel(x, c, ctx, c_ctx, w_ada, b_ada, norm_mix, norm_ffn, w_in, w_gla_gate_up, b_gla_gate, gla_norm, q_norm,
           k_norm, w_out, w_router, b_router, w_exp_gate, w_exp_up, w_exp_down, final_norm):
    b, n, d = x.shape
    m = ctx.shape[1]
    depth = w_ada.shape[0]
    assert d == D_MODEL and n % GLA_PAIR == 0 and m % GLA_PAIR == 0 and n % GRID_W == 0

    rows = ((b + 1 + 7) // 8) * 8
    cv = jnp.concatenate([c, c_ctx[None, :], jnp.zeros((rows - b - 1, d), F32)], axis=0)
    mod = _ada_call(cv, w_ada, b_ada).reshape(depth, rows, 6, d)

    cs = jnp.asarray(_channel_dft_table()).astype(BF16)
    tab_x = jnp.asarray(_seq_dft_table(n)).astype(BF16)
    tab_c = jnp.asarray(_seq_dft_table(m)).astype(BF16)
    rope_tabs = tuple(jnp.asarray(t) for t in _rope_tables(n))
    bd512 = jnp.asarray(_blockdiag_ones(ATT_WIDTH, ATT_HDIM)).astype(BF16)
    bd256 = jnp.asarray(_blockdiag_ones(GLA_WIDTH, GLA_DV)).astype(BF16)
    wr_t = w_router.T
    wr_hi = wr_t.astype(BF16)
    wrh = jnp.concatenate([wr_hi, (wr_t - wr_hi.astype(F32)).astype(BF16)], axis=0)
    fn = final_norm.reshape(1, d)

    w_in_perm = _winprep_call(w_in)

    xc = ctx
    pend_x = pend_c = None
    for l in range(depth):
        ctx_out = l < depth - 1
        mod_l = mod[l]
        w_perm = w_in_perm[l]
        wup, bup = _gate_up_weights(w_gla_gate_up[l], b_gla_gate[l])
        nw = norm_mix[l].reshape(1, d)
        nf = norm_ffn[l].reshape(1, d)
        qn = jnp.tile(q_norm[l], ATT_HEADS).reshape(1, ATT_WIDTH)
        kn = jnp.tile(k_norm[l], ATT_KV_HEADS).reshape(1, 128)
        gn = jnp.tile(gla_norm[l], GLA_HEADS).reshape(1, GLA_WIDTH)
        wo = w_out[l].astype(BF16)

        outs_c = _inproj_call(xc, mod_l, b, nw, w_perm, cs, wup, bup, qn, kn, bd512, None, pend_c)
        outs_x = _inproj_call(x, mod_l, None, nw, w_perm, cs, wup, bup, qn, kn, bd512, rope_tabs, pend_x)
        (ab_c, gqk_c, gv_c, gvt_c, gg_c, la_c, q_c, kt_c, vd_c) = outs_c[:9]
        (ab_x, gqk_x, gv_x, gvt_x, gg_x, la_x, q_x, kt_x, vd_x) = outs_x[:9]
        if pend_x is not None:
            x, xc = outs_x[9], outs_c[9]

        s_zero = jnp.zeros((b, 2, GLA_WIDTH, GLA_QK), F32)
        of_c, ob_c, s_fin = _gla_call(gqk_c, gv_c, gvt_c, la_c, s_zero)
        of_x, ob_x, _ = _gla_call(gqk_x, gv_x, gvt_x, la_x, s_fin)

        f_x = _seqdft_call(tab_x, ab_x.reshape(2 * n, b * FNET_WIDTH), b)
        x, h2_x, lg_x = _attn_outproj_call(q_x, [(kt_c, vd_c), (kt_x, vd_x)], f_x, of_x, ob_x, gg_x, x,
                                           mod_l, None, wo, gn, bd256, nf, wrh)

        if ctx_out:
            f_c = _seqdft_call(tab_c, ab_c.reshape(2 * m, b * FNET_WIDTH), b)
            xc, h2_c, lg_c = _attn_outproj_call(q_c, [(kt_c, vd_c)], f_c, of_c, ob_c, gg_c, xc,
                                                mod_l, b, wo, gn, bd256, nf, wrh)

        final = l == depth - 1
        wexp = (w_exp_gate, w_exp_up, w_exp_down, l)
        if ctx_out:
            res = _moe_sparse([h2_x, h2_c], [lg_x, lg_c], [x, xc], mod_l, [None, b], b_router, *wexp, fn, final)
        else:
            res = _moe_sparse([h2_x], [lg_x], [x], mod_l, [None], b_router, *wexp, fn, final)
        if final:
            x = res[0]
        else:
            pend_x, pend_c = res
    return x
```

```python
import functools

import numpy as np
import jax
import jax.numpy as jnp
from jax import lax
from jax.experimental import pallas as pl
from jax.experimental.pallas import tpu as pltpu
from jax.experimental.pallas import tpu_sc as plsc

F32 = jnp.float32
BF16 = jnp.bfloat16

D_MODEL = 1024
GRID_W = 64
EPS = 1e-6
LOG2E = 1.4426950408889634

FNET_WIDTH = 256
FNET_GROUPS = 4
FNET_GDIM = 64

GLA_HEADS = 4
GLA_DV = 64
GLA_DK = 32
GLA_WIDTH = 256
GLA_QK = 128
GLA_GATE_RANK = 16
GLA_GATE_NORM = 16.0
GLA_CHUNK = 64
GLA_PAIR = 2 * GLA_CHUNK

ATT_HEADS = 8
ATT_KV_HEADS = 2
ATT_HDIM = 64
ATT_WIDTH = 512
ROPE_FREQS = 16
ROPE_THETA = 10000.0

N_EXPERTS = 16
N_GROUPS = 4
EXPERTS_PER_GROUP = 4
D_EXPERT = 512

C_U = 0
C_GQ = 256
C_GK = 384
C_GV = 512
C_GG = 768
C_AQ = 1024
C_AK = 1536
C_GD = 1792
W_IN_COLS = 1920
W_IN_REF_COLS = 1824

VMEM_LIMIT = 56 * 1024 * 1024


def _cparams(sem):
    return pltpu.CompilerParams(dimension_semantics=sem, vmem_limit_bytes=VMEM_LIMIT)


def _sigmoid(x):
    return 1.0 / (1.0 + jnp.exp(-x))


def _pack_bf16_pairs(x):
    blocks = []
    for t in range(x.shape[1] // 256):
        lo = lax.bitcast_convert_type(x[:, 256 * t:256 * t + 128].astype(BF16).astype(F32), jnp.uint32)
        hi = lax.bitcast_convert_type(x[:, 256 * t + 128:256 * t + 256].astype(BF16).astype(F32), jnp.uint32)
        blocks.append((lo >> 16) | (hi & jnp.uint32(0xFFFF0000)))
    return lax.bitcast_convert_type(jnp.concatenate(blocks, axis=1), jnp.int32)


def _unpack_bf16_pairs(p):
    u = lax.bitcast_convert_type(p, jnp.uint32)
    blocks = []
    for t in range(p.shape[1] // 128):
        word = u[:, 128 * t:128 * (t + 1)]
        blocks += [lax.bitcast_convert_type(word << 16, F32),
                   lax.bitcast_convert_type(word & jnp.uint32(0xFFFF0000), F32)]
    return jnp.concatenate(blocks, axis=1)


def _token_columns(w):
    nr = w.shape[0]
    tm = nr * 128
    lane = lax.broadcasted_iota(jnp.int32, (tm, 128), 1)
    row = lax.broadcasted_iota(jnp.int32, (tm, 128), 0)
    wb = jnp.concatenate([jnp.broadcast_to(w[r:r + 1, :], (128, 128)) for r in range(nr)], axis=0)
    return jnp.sum(jnp.where(lane == (row % 128), wb, 0.0), axis=1, keepdims=True)


def _moe_mix(y_ref, w0, w1):
    return (_token_columns(w0) * _unpack_bf16_pairs(y_ref[0])
            + _token_columns(w1) * _unpack_bf16_pairs(y_ref[1]))


def _nt_dot(a, b):
    return lax.dot_general(a, b, (((1,), (1,)), ((), ())), preferred_element_type=F32)


@functools.lru_cache(maxsize=None)
def _channel_dft_table():
    j = np.arange(FNET_GDIM)
    ang = 2.0 * np.pi * ((j[:, None] * j[None, :]) % FNET_GDIM) / FNET_GDIM
    c = np.cos(ang) / np.sqrt(FNET_GDIM)
    s = np.sin(ang) / np.sqrt(FNET_GDIM)
    out = np.zeros((FNET_WIDTH, 2 * FNET_WIDTH), np.float64)
    for g in range(FNET_GROUPS):
        sl = slice(g * FNET_GDIM, (g + 1) * FNET_GDIM)
        out[sl, sl] = c
        out[sl, FNET_WIDTH + g * FNET_GDIM:FNET_WIDTH + (g + 1) * FNET_GDIM] = s
    return out.astype(np.float32)


@functools.lru_cache(maxsize=None)
def _seq_dft_table(n):
    j = np.arange(n, dtype=np.int64)
    ang = 2.0 * np.pi * ((j[:, None] * j[None, :]) % n) / n
    return np.concatenate([np.cos(ang), -np.sin(ang)], axis=1).astype(np.float32) / np.float32(np.sqrt(n))


@functools.lru_cache(maxsize=None)
def _rope_tables(n):
    rows = n // GRID_W
    row = np.repeat(np.arange(rows), GRID_W).astype(np.float64)
    col = np.tile(np.arange(GRID_W), rows).astype(np.float64)
    inv = ROPE_THETA ** (-np.arange(ROPE_FREQS, dtype=np.float64) * 2.0 / (2 * ROPE_FREQS))
    ar = row[:, None] * inv[None, :]
    ac = col[:, None] * inv[None, :]
    cos = np.concatenate([np.cos(ar), np.cos(ar), np.cos(ac), np.cos(ac)], axis=1)
    sin = np.concatenate([-np.sin(ar), np.sin(ar), -np.sin(ac), np.sin(ac)], axis=1)
    return (np.tile(cos, (1, 2)).astype(np.float32), np.tile(sin, (1, 2)).astype(np.float32))


@functools.lru_cache(maxsize=None)
def _blockdiag_ones(width, blk):
    i = np.arange(width)
    return (i[:, None] // blk == i[None, :] // blk).astype(np.float32)


def _ada_kernel(cv_ref, w_ref, b_ref, o_ref):
    cv = cv_ref[...]
    a = (cv * _sigmoid(cv)).astype(BF16)
    o_ref[...] = jnp.dot(a, w_ref[...].astype(BF16), preferred_element_type=F32) + b_ref[...]


def _ada_call(cv, w_ada, b_ada):
    depth, d, d6 = w_ada.shape
    tn = 1536
    rows = cv.shape[0]
    return pl.pallas_call(
        _ada_kernel,
        grid=(depth, d6 // tn),
        in_specs=[
            pl.BlockSpec((rows, d), lambda l, j: (0, 0)),
            pl.BlockSpec((None, d, tn), lambda l, j: (l, 0, j)),
            pl.BlockSpec((None, 1, tn), lambda l, j: (l, 0, j)),
        ],
        out_specs=pl.BlockSpec((None, rows, tn), lambda l, j: (l, 0, j)),
        out_shape=jax.ShapeDtypeStruct((depth, rows, d6), F32),
        compiler_params=_cparams(("parallel", "parallel")),
        name="ada_mod",
    )(cv, w_ada, b_ada.reshape(depth, 1, d6))


def _swap16(x):
    lane = lax.broadcasted_iota(jnp.int32, x.shape, 1)
    first = (lane % 32) < 16
    return jnp.where(first, pltpu.roll(x, 112, 1), pltpu.roll(x, 16, 1))


def _head_rms(x, bd, w):
    ms = jnp.dot((x * x).astype(BF16), bd, preferred_element_type=F32) * (1.0 / ATT_HDIM)
    return x * lax.rsqrt(ms + EPS) * w


def _inproj_kernel(*refs, rope, pending_rows):
    refs = list(refs)
    x_ref, mod_ref, nw_ref, w_ref, cs_ref, wup_ref, bup_ref, qn_ref, kn_ref, bd_ref = refs[:10]
    del refs[:10]
    if rope:
        cos_ref, sin_ref = refs[:2]
        del refs[:2]
    if pending_rows is not None:
        y_ref, wt_ref, modp_ref = refs[:3]
        del refs[:3]
    ab_ref, gqk_ref, gv_ref, gvt_ref, gg_ref, la_ref, q_ref, kt_ref, vd_ref = refs[:9]
    x = x_ref[...]
    if pending_rows is not None:
        xnew_ref = refs[9]
        row0, rows_per_sample = pending_rows
        nr = x.shape[0] // 128
        r = row0 + pl.program_id(0) * rows_per_sample + pl.program_id(1) * nr
        sub = lax.rem(r, 8)
        w = [wt_ref[kk, 0:nr, :] for kk in range(2)]
        for blk in range(1, 8 // nr):
            w = [jnp.where(sub == blk * nr, wt_ref[kk, blk * nr:(blk + 1) * nr, :], w[kk]) for kk in range(2)]
        x = x + modp_ref[5:6, :] * _moe_mix(y_ref, w[0], w[1])
        xnew_ref[...] = x
    ms = jnp.mean(x * x, axis=-1, keepdims=True)
    y = x * lax.rsqrt(ms + EPS) * nw_ref[...]
    h = y * (1.0 + mod_ref[1:2, :]) + mod_ref[0:1, :]
    hb = h.astype(BF16)

    def proj(c0, width):
        return jnp.dot(hb, w_ref[:, c0:c0 + width], preferred_element_type=F32)

    uab = jnp.dot(proj(C_U, FNET_WIDTH).astype(BF16), cs_ref[...], preferred_element_type=F32)
    ab_ref[0] = uab[:, :FNET_WIDTH].astype(BF16)
    ab_ref[1] = uab[:, FNET_WIDTH:].astype(BF16)

    gqk_ref[...] = proj(C_GQ, 2 * GLA_QK)
    gv = proj(C_GV, GLA_WIDTH)
    gv_ref[...] = gv.astype(BF16)
    gvt_ref[...] = gv.T.astype(BF16)
    gg_ref[...] = proj(C_GG, GLA_WIDTH).astype(BF16)
    pre = jnp.dot(proj(C_GD, 128).astype(BF16), wup_ref[...], preferred_element_type=F32) + bup_ref[...]
    la_ref[...] = (jnp.minimum(pre, 0.0) - jnp.log1p(jnp.exp(-jnp.abs(pre)))) * (1.0 / GLA_GATE_NORM)

    bd = bd_ref[...]
    q = _head_rms(proj(C_AQ, ATT_WIDTH), bd, qn_ref[...])
    kv = proj(C_AK, 256)
    k = _head_rms(kv[:, :128], bd[:128, :128], kn_ref[...])
    if rope:
        cos = cos_ref[...]
        sin = sin_ref[...]
        q = jnp.concatenate(
            [q[:, s:s + 128] * cos + _swap16(q[:, s:s + 128]) * sin for s in range(0, ATT_WIDTH, 128)], axis=1)
        k = k * cos + _swap16(k) * sin
    q_ref[...] = (q * (ATT_HDIM ** -0.5 * LOG2E)).astype(BF16)
    v = kv[:, 128:]
    lo = lax.broadcasted_iota(jnp.int32, k.shape, 1) < ATT_HDIM
    k_sw = pltpu.roll(k, ATT_HDIM, 1)
    v_sw = pltpu.roll(v, ATT_HDIM, 1)
    kt_ref[0] = jnp.where(lo, k, k_sw).T.astype(BF16)
    kt_ref[1] = jnp.where(lo, k_sw, k).T.astype(BF16)
    vd_ref[0] = jnp.where(lo, v, 1.0).astype(BF16)
    vd_ref[1] = jnp.where(lo, 1.0, v_sw).astype(BF16)
    vd_ref[2] = jnp.where(lo, v_sw, 1.0).astype(BF16)
    vd_ref[3] = jnp.where(lo, 1.0, v).astype(BF16)


def _inproj_call(x, mod_l, mod_row, nw, w_perm, cs, wup, bup, qn, kn, bd, rope_tabs, pending=None):
    b, n, d = x.shape
    tm = min(1024, n)
    nt = n // tm
    rope = rope_tabs is not None
    if mod_row is None:
        mod_map = lambda bi, i: (bi, 0, 0)
    else:
        mod_map = lambda bi, i: (mod_row, 0, 0)
    const = lambda bi, i: (0, 0)
    in_specs = [
        pl.BlockSpec((None, tm, d), lambda bi, i: (bi, i, 0)),
        pl.BlockSpec((None, 6, d), mod_map),
        pl.BlockSpec((1, d), const),
        pl.BlockSpec((d, W_IN_COLS), const),
        pl.BlockSpec((FNET_WIDTH, 2 * FNET_WIDTH), const),
        pl.BlockSpec((128, 2 * GLA_QK), const),
        pl.BlockSpec((1, 2 * GLA_QK), const),
        pl.BlockSpec((1, ATT_WIDTH), const),
        pl.BlockSpec((1, 128), const),
        pl.BlockSpec((ATT_WIDTH, ATT_WIDTH), const),
    ]
    args = [x, mod_l, nw, w_perm, cs, wup, bup, qn, kn, bd]
    if rope:
        in_specs += [pl.BlockSpec((tm, 128), lambda bi, i: (i, 0)), pl.BlockSpec((tm, 128), lambda bi, i: (i, 0))]
        args += list(rope_tabs)
    pending_rows = None
    if pending is not None:
        y2, wts3, row0, mod_prev = pending
        nr = tm // 128
        rps = n // 128
        assert 8 % nr == 0 and row0 % nr == 0 and rps % nr == 0
        pending_rows = (row0, rps)
        in_specs += [
            pl.BlockSpec((2, None, tm, d // 2), lambda bi, i: (0, bi, i, 0)),
            pl.BlockSpec((2, 8, 128), lambda bi, i: (0, (row0 + bi * rps + i * nr) // 8, 0)),
            pl.BlockSpec((None, 6, d), mod_map),
        ]
        args += [y2.reshape(2, b, n, d // 2), wts3, mod_prev]
    out_shape = (
        jax.ShapeDtypeStruct((2, n, b * FNET_WIDTH), BF16),
        jax.ShapeDtypeStruct((b, n, 2 * GLA_QK), F32),
        jax.ShapeDtypeStruct((b, n, GLA_WIDTH), BF16),
        jax.ShapeDtypeStruct((b, GLA_WIDTH, n), BF16),
        jax.ShapeDtypeStruct((b, n, GLA_WIDTH), BF16),
        jax.ShapeDtypeStruct((b, n, 2 * GLA_QK), F32),
        jax.ShapeDtypeStruct((b, n, ATT_WIDTH), BF16),
        jax.ShapeDtypeStruct((b, ATT_KV_HEADS, 128, n), BF16),
        jax.ShapeDtypeStruct((b, 2 * ATT_KV_HEADS, n, 128), BF16),
    )
    out_specs = (
        pl.BlockSpec((2, tm, FNET_WIDTH), lambda bi, i: (0, i, bi)),
        pl.BlockSpec((None, tm, 2 * GLA_QK), lambda bi, i: (bi, i, 0)),
        pl.BlockSpec((None, tm, GLA_WIDTH), lambda bi, i: (bi, i, 0)),
        pl.BlockSpec((None, GLA_WIDTH, tm), lambda bi, i: (bi, 0, i)),
        pl.BlockSpec((None, tm, GLA_WIDTH), lambda bi, i: (bi, i, 0)),
        pl.BlockSpec((None, tm, 2 * GLA_QK), lambda bi, i: (bi, i, 0)),
        pl.BlockSpec((None, tm, ATT_WIDTH), lambda bi, i: (bi, i, 0)),
        pl.BlockSpec((None, ATT_KV_HEADS, 128, tm), lambda bi, i: (bi, 0, 0, i)),
        pl.BlockSpec((None, 2 * ATT_KV_HEADS, tm, 128), lambda bi, i: (bi, 0, i, 0)),
    )
    if pending is not None:
        out_shape += (jax.ShapeDtypeStruct((b, n, d), F32),)
        out_specs += (pl.BlockSpec((None, tm, d), lambda bi, i: (bi, i, 0)),)
    return pl.pallas_call(
        functools.partial(_inproj_kernel, rope=rope, pending_rows=pending_rows),
        grid=(b, nt),
        in_specs=in_specs,
        out_specs=out_specs,
        out_shape=out_shape,
        compiler_params=_cparams(("parallel", "parallel")),
        name="inproj_rope" if rope else "inproj_ctx",
    )(*args)


def _seqdft_kernel(t_ref, ab_ref, o_ref):
    y = jnp.dot(t_ref[...], ab_ref[...], preferred_element_type=F32)
    for bb in range(o_ref.shape[0]):
        o_ref[bb] = y[:, bb * FNET_WIDTH:(bb + 1) * FNET_WIDTH].astype(BF16)


def _seqdft_call(table, ab, b):
    n = table.shape[0]
    tm = min(512, n)
    nb = 4 if b % 4 == 0 else (2 if b % 2 == 0 else 1)
    return pl.pallas_call(
        _seqdft_kernel,
        grid=(b // nb, n // tm),
        in_specs=[
            pl.BlockSpec((tm, 2 * n), lambda c, i: (i, 0)),
            pl.BlockSpec((2 * n, nb * FNET_WIDTH), lambda c, i: (0, c)),
        ],
        out_specs=pl.BlockSpec((nb, tm, FNET_WIDTH), lambda c, i: (c, i, 0)),
        out_shape=jax.ShapeDtypeStruct((b, n, FNET_WIDTH), BF16),
        compiler_params=_cparams(("parallel", "parallel")),
        name="seq_dft",
    )(table, ab)


def _gla_dir(qk, v, vt, a, s_in, fwd):
    p = GLA_PAIR
    r = lax.broadcasted_iota(jnp.int32, (p, p), 0)
    c = lax.broadcasted_iota(jnp.int32, (p, p), 1)
    same = (r // GLA_CHUNK) == (c // GLA_CHUNK)
    tri = same & ((c <= r) if fwd else (c >= r))
    row_lo = r < GLA_CHUNK
    rin = r % GLA_CHUNK

    q = qk[:, :GLA_QK] * (GLA_DK ** -0.5)
    k = qk[:, GLA_QK:]
    cum = a
    sh = 1
    while sh < GLA_CHUNK:
        if fwd:
            cum = cum + jnp.where(rin >= sh, pltpu.roll(cum, sh, 0), 0.0)
        else:
            cum = cum + jnp.where(rin < GLA_CHUNK - sh, pltpu.roll(cum, p - sh, 0), 0.0)
        sh *= 2
    if fwd:
        last0, last1 = cum[GLA_CHUNK - 1:GLA_CHUNK, :], cum[p - 1:p, :]
    else:
        last0, last1 = cum[0:1, :], cum[GLA_CHUNK:GLA_CHUNK + 1, :]
    lastb = jnp.where(row_lo, last0, last1)
    qt = q * jnp.exp(cum)
    kt = k * jnp.exp(-cum)
    kd = k * jnp.exp(lastb - cum)

    kt_b = kt.astype(BF16)
    zk = jnp.zeros_like(kt_b)
    ks = jnp.concatenate([jnp.where((c // GLA_DK) == hh, kt_b, zk) for hh in range(GLA_HEADS)], axis=0)
    att = _nt_dot(qt.astype(BF16), ks)
    tri4 = jnp.concatenate([tri] * GLA_HEADS, axis=1)
    att = jnp.where(tri4, att, 0.0).astype(BF16)
    col = lax.broadcasted_iota(jnp.int32, (p, GLA_WIDTH), 1)
    zv = jnp.zeros_like(v)
    vs = jnp.concatenate([jnp.where((col // GLA_DV) == hh, v, zv) for hh in range(GLA_HEADS)], axis=0)
    o_intra = jnp.dot(att, vs, preferred_element_type=F32)

    sr = lax.broadcasted_iota(jnp.int32, (GLA_WIDTH, GLA_QK), 0)
    sc = lax.broadcasted_iota(jnp.int32, (GLA_WIDTH, GLA_QK), 1)
    bdm = (sr // GLA_DV) == (sc // GLA_DK)
    first, second = (0, 1) if fwd else (1, 0)
    lasts = (last0, last1)
    in_chunk = (row_lo, jnp.logical_not(row_lo))
    kd2 = jnp.concatenate([jnp.where(in_chunk[0], kd, 0.0), jnp.where(in_chunk[1], kd, 0.0)], axis=1).astype(BF16)
    kvt2 = jnp.dot(vt, kd2, preferred_element_type=F32)
    kvt = (kvt2[:, :GLA_QK], kvt2[:, GLA_QK:])
    s_a = s_in
    s_b = s_a * jnp.exp(lasts[first]) + jnp.where(bdm, kvt[first], 0.0)
    s_c = s_b * jnp.exp(lasts[second]) + jnp.where(bdm, kvt[second], 0.0)
    q2 = jnp.concatenate([jnp.where(in_chunk[first], qt, 0.0), jnp.where(in_chunk[second], qt, 0.0)], axis=1)
    s2 = jnp.concatenate([s_a, s_b], axis=1).astype(BF16)
    o_inter = _nt_dot(q2.astype(BF16), s2)
    return o_intra + o_inter, s_c


def _gla_kernel(qkf, vf, vtf, laf, qkb, vb, vtb, lab, s0_ref, of_ref, ob_ref, sfin_ref, s_scr):
    i = pl.program_id(1)

    @pl.when(i == 0)
    def _():
        s_scr[...] = s0_ref[...]

    pairs = qkf.shape[1] // GLA_PAIR
    for gi in range(qkf.shape[0]):
        sf = s_scr[gi, 0]
        sb = s_scr[gi, 1]
        for pi in range(pairs):
            rf = slice(pi * GLA_PAIR, (pi + 1) * GLA_PAIR)
            rb = slice((pairs - 1 - pi) * GLA_PAIR, (pairs - pi) * GLA_PAIR)
            o1, sf = _gla_dir(qkf[gi, rf, :], vf[gi, rf, :], vtf[gi, :, rf], laf[gi, rf, :], sf, True)
            o2, sb = _gla_dir(qkb[gi, rb, :], vb[gi, rb, :], vtb[gi, :, rb], lab[gi, rb, :], sb, False)
            of_ref[gi, rf, :] = o1.astype(of_ref.dtype)
            ob_ref[gi, rb, :] = o2.astype(ob_ref.dtype)
        s_scr[gi, 0] = sf
        s_scr[gi, 1] = sb

    @pl.when(i == pl.num_programs(1) - 1)
    def _():
        sfin_ref[...] = s_scr[...]


def _gla_call(gqk, gv, gvt, la, s0):
    b, n, _ = gqk.shape
    p = GLA_PAIR * (2 if n % (2 * GLA_PAIR) == 0 else 1)
    npair = n // p
    gb = 4 if b % 4 == 0 else (2 if b % 2 == 0 else 1)
    fw = lambda bi, i: (bi, i, 0)
    bw = lambda bi, i: (bi, npair - 1 - i, 0)
    in_specs = [
        pl.BlockSpec((gb, p, 2 * GLA_QK), fw),
        pl.BlockSpec((gb, p, GLA_WIDTH), fw),
        pl.BlockSpec((gb, GLA_WIDTH, p), lambda bi, i: (bi, 0, i)),
        pl.BlockSpec((gb, p, GLA_QK), fw),
        pl.BlockSpec((gb, p, 2 * GLA_QK), bw),
        pl.BlockSpec((gb, p, GLA_WIDTH), bw),
        pl.BlockSpec((gb, GLA_WIDTH, p), lambda bi, i: (bi, 0, npair - 1 - i)),
        pl.BlockSpec((gb, p, GLA_QK), lambda bi, i: (bi, npair - 1 - i, 1)),
        pl.BlockSpec((gb, 2, GLA_WIDTH, GLA_QK), lambda bi, i: (bi, 0, 0, 0)),
    ]
    out_specs = (
        pl.BlockSpec((gb, p, GLA_WIDTH), fw),
        pl.BlockSpec((gb, p, GLA_WIDTH), bw),
        pl.BlockSpec((gb, 2, GLA_WIDTH, GLA_QK), lambda bi, i: (bi, 0, 0, 0)),
    )
    out_shape = (
        jax.ShapeDtypeStruct((b, n, GLA_WIDTH), BF16),
        jax.ShapeDtypeStruct((b, n, GLA_WIDTH), BF16),
        jax.ShapeDtypeStruct((b, 2, GLA_WIDTH, GLA_QK), F32),
    )
    return pl.pallas_call(
        _gla_kernel,
        grid=(b // gb, npair),
        in_specs=in_specs,
        out_specs=out_specs,
        out_shape=out_shape,
        scratch_shapes=[pltpu.VMEM((gb, 2, GLA_WIDTH, GLA_QK), F32)],
        compiler_params=_cparams(("parallel", "arbitrary")),
        name="gla_scan",
    )(gqk, gv, gvt, la, gqk, gv, gvt, la, s0)


def _attn_heads(q_ref, parts):
    tq = q_ref.shape[0]
    lane = lax.broadcasted_iota(jnp.int32, (tq, 128), 1)
    lo = lane < ATT_HDIM
    blocks = []
    for j in range(ATT_HEADS // 2):
        q128 = q_ref[:, 128 * j:128 * (j + 1)]
        g = (2 * j) // (ATT_HEADS // ATT_KV_HEADS)
        outs = []
        for half in range(2):
            qm = jnp.where(lo if half == 0 else jnp.logical_not(lo), q128, jnp.zeros_like(q128))
            ss = [jnp.dot(qm, kt_ref[g], preferred_element_type=F32) for kt_ref, _ in parts]
            m = functools.reduce(jnp.maximum, [jnp.max(s, axis=-1, keepdims=True) for s in ss])
            ps = [jnp.exp2(s - m).astype(BF16) for s in ss]
            pv = functools.reduce(
                lambda u, w: u + w,
                [jnp.dot(pp, vd_ref[2 * g + half], preferred_element_type=F32) for pp, (_, vd_ref) in zip(ps, parts)])
            den = pv[:, ATT_HDIM:ATT_HDIM + 1] if half == 0 else pv[:, 0:1]
            outs.append(pv / den)
        blocks.append(jnp.where(lo, outs[0], outs[1]).astype(BF16))
    return jnp.concatenate(blocks, axis=1)


def _attn_outproj_kernel(*refs, nparts):
    q_ref = refs[0]
    parts = [(refs[1 + 2 * i], refs[2 + 2 * i]) for i in range(nparts)]
    (f_ref, of_ref, ob_ref, gg_ref, x_ref, mod_ref, w_ref, gn_ref, bd_ref, nf_ref, wr_ref,
     xn_ref, h2_ref, lg_ref) = refs[1 + 2 * nparts:]
    att = _attn_heads(q_ref, parts)
    o = of_ref[...].astype(F32) + ob_ref[...].astype(F32)
    ms = jnp.dot((o * o).astype(BF16), bd_ref[...], preferred_element_type=F32) * (1.0 / GLA_DV)
    on = o * lax.rsqrt(ms + EPS) * gn_ref[...]
    g = gg_ref[...].astype(F32)
    gl = (on * (g * _sigmoid(g))).astype(BF16)
    ox = (jnp.dot(f_ref[...], w_ref[0:FNET_WIDTH, :], preferred_element_type=F32)
          + jnp.dot(gl, w_ref[FNET_WIDTH:FNET_WIDTH + GLA_WIDTH, :], preferred_element_type=F32)
          + jnp.dot(att, w_ref[FNET_WIDTH + GLA_WIDTH:, :], preferred_element_type=F32))
    xn = x_ref[...] + mod_ref[2:3, :] * ox
    xn_ref[...] = xn
    ms2 = jnp.mean(xn * xn, axis=-1, keepdims=True)
    h2 = xn * lax.rsqrt(ms2 + EPS) * nf_ref[...] * (1.0 + mod_ref[4:5, :]) + mod_ref[3:4, :]
    h2_ref[...] = _pack_bf16_pairs(h2)
    lg2 = _nt_dot(wr_ref[...], h2.astype(BF16))
    lg_ref[...] = lg2[:N_EXPERTS, :] + lg2[N_EXPERTS:, :]


def _attn_outproj_call(q, kv_parts, f, of, ob, gg, x, mod_l, mod_row, w_out, gn, bd, nf, wr):
    b, n, d = x.shape
    tm = min(1024, n)
    nt = n // tm
    if mod_row is None:
        mod_map = lambda bi, i: (bi, 0, 0)
    else:
        mod_map = lambda bi, i: (mod_row, 0, 0)
    const = lambda bi, i: (0, 0)
    tok = lambda w: pl.BlockSpec((None, tm, w), lambda bi, i: (bi, i, 0))
    in_specs = [tok(ATT_WIDTH)]
    args = [q]
    for kt, vd in kv_parts:
        m = kt.shape[-1]
        in_specs.append(pl.BlockSpec((None, ATT_KV_HEADS, 128, m), lambda bi, i: (bi, 0, 0, 0)))
        in_specs.append(pl.BlockSpec((None, 2 * ATT_KV_HEADS, m, 128), lambda bi, i: (bi, 0, 0, 0)))
        args += [kt, vd]
    in_specs += [
        tok(FNET_WIDTH), tok(GLA_WIDTH), tok(GLA_WIDTH), tok(GLA_WIDTH), tok(d),
        pl.BlockSpec((None, 6, d), mod_map),
        pl.BlockSpec((d, d), const),
        pl.BlockSpec((1, GLA_WIDTH), const),
        pl.BlockSpec((GLA_WIDTH, GLA_WIDTH), const),
        pl.BlockSpec((1, d), const),
        pl.BlockSpec((2 * N_EXPERTS, d), const),
    ]
    args += [f, of, ob, gg, x, mod_l, w_out, gn, bd, nf, wr]
    out_specs = (
        tok(d), tok(d // 2),
        pl.BlockSpec((N_EXPERTS, tm), lambda bi, i: (0, bi * nt + i)),
    )
    out_shape = (
        jax.ShapeDtypeStruct((b, n, d), F32),
        jax.ShapeDtypeStruct((b, n, d // 2), jnp.int32),
        jax.ShapeDtypeStruct((N_EXPERTS, b * n), F32),
    )
    return pl.pallas_call(
        functools.partial(_attn_outproj_kernel, nparts=len(kv_parts)),
        grid=(b, nt),
        in_specs=in_specs,
        out_specs=out_specs,
        out_shape=out_shape,
        compiler_params=_cparams(("parallel", "parallel")),
        name="attn_outproj",
    )(*args)


def _route_kernel(b_ref, lg_ref, pos_ref, wt_ref, te_ref, nv_ref, *, tm):
    r = lg_ref.shape[1]
    s = [_sigmoid(lg_ref[e]) for e in range(N_EXPERTS)]
    sel = [s[e] + b_ref[e] for e in range(N_EXPERTS)]
    grp = []
    for g in range(N_GROUPS):
        a, b, c, d = sel[4 * g:4 * g + 4]
        hi1, lo1 = jnp.maximum(a, b), jnp.minimum(a, b)
        hi2, lo2 = jnp.maximum(c, d), jnp.minimum(c, d)
        m1 = jnp.maximum(hi1, hi2)
        m2 = jnp.maximum(jnp.minimum(hi1, hi2), jnp.maximum(lo1, lo2))
        grp.append(m1 + m2)
    one = jnp.ones_like(s[0])
    zero = jnp.zeros_like(s[0])
    msk = []
    for g in range(N_GROUPS):
        isg = one
        for g2 in range(N_GROUPS):
            if g2 < g:
                isg = isg * jnp.where(grp[g] > grp[g2], one, zero)
            elif g2 > g:
                isg = isg * jnp.where(grp[g] >= grp[g2], one, zero)
        for li in range(EXPERTS_PER_GROUP):
            e = 4 * g + li
            rank = zero
            for lj in range(EXPERTS_PER_GROUP):
                ej = 4 * g + lj
                if lj < li:
                    rank = rank + jnp.where(sel[ej] >= sel[e], one, zero)
                elif lj > li:
                    rank = rank + jnp.where(sel[ej] > sel[e], one, zero)
            msk.append(jnp.where(rank < 2.0, isg, zero))
    den = functools.reduce(lambda u, v: u + v, [msk[e] * s[e] for e in range(N_EXPERTS)])

    li_ = lax.broadcasted_iota(jnp.int32, (128, 128), 0)
    lj_ = lax.broadcasted_iota(jnp.int32, (128, 128), 1)
    upper = jnp.where(li_ < lj_, 1.0, 0.0).astype(BF16)
    ri_ = lax.broadcasted_iota(jnp.int32, (r, r), 0)
    rj_ = lax.broadcasted_iota(jnp.int32, (r, r), 1)
    lower = jnp.where(rj_ < ri_, 1.0, 0.0).astype(BF16)
    tiles = (1, te_ref.shape[1])
    tile_start = lax.broadcasted_iota(jnp.int32, tiles, 1).astype(F32) * float(tm)
    te = jnp.zeros(tiles, F32)
    seg = jnp.zeros(tiles, F32)
    nonempty = []
    off = jnp.zeros((1, 1), F32)
    seen = zero
    pos = [zero, zero]
    wts = [zero, zero]
    for e in range(N_EXPERTS):
        mb = msk[e].astype(BF16)
        lane_pre = jnp.dot(mb, upper, preferred_element_type=F32)
        row_pre = jnp.sum(jnp.dot(lower, mb, preferred_element_type=F32), axis=1, keepdims=True)
        cnt = jnp.sum(jnp.sum(msk[e], axis=1, keepdims=True), axis=0, keepdims=True)
        p_e = off + row_pre + lane_pre
        g_e = s[e] / den
        for kk in range(2):
            hit = msk[e] * jnp.where(seen == float(kk), one, zero)
            pos[kk] = pos[kk] + hit * p_e
            wts[kk] = wts[kk] + hit * g_e
        seen = seen + msk[e]
        off = off + jnp.floor((cnt + float(tm - 1)) * (1.0 / tm)) * float(tm)
        passed = jnp.where(tile_start >= off, 1.0, 0.0)
        te = te + passed
        nonempty.append(jnp.where(cnt > 0.0, 1.0, 0.0))
        seg = seg + nonempty[e] * passed
    for kk in range(2):
        pos_ref[kk] = pos[kk].astype(jnp.int32)
        wt_ref[kk] = wts[kk]
    te = jnp.minimum(te, float(N_EXPERTS - 1))
    nxt = jnp.full(tiles, -1.0, F32)
    for e in reversed(range(N_EXPERTS)):
        nxt = jnp.where(jnp.logical_and(nonempty[e] > 0.0, te < float(e)), float(e), nxt)
    te_ref[0:1, :] = te.astype(jnp.int32)
    te_ref[1:2, :] = seg.astype(jnp.int32)
    te_ref[2:3, :] = nxt.astype(jnp.int32)
    te_ref[3:, :] = jnp.zeros((te_ref.shape[0] - 3, te_ref.shape[1]), jnp.int32)
    nv_ref[...] = jnp.broadcast_to(off * (1.0 / tm), nv_ref.shape).astype(jnp.int32)


def _route_call(lgt, b_router, tm):
    n_tok = lgt.shape[1]
    r = n_tok // 128
    assert r * 128 == n_tok and r % 8 == 0 and 2 * n_tok // tm + N_EXPERTS <= 256
    lg3 = lgt.reshape(N_EXPERTS, r, 128)
    full3 = lambda k: pl.BlockSpec((k, r, 128), lambda: (0, 0, 0))
    pos, wts, te, nv = pl.pallas_call(
        functools.partial(_route_kernel, tm=tm),
        in_specs=[pl.BlockSpec(memory_space=pltpu.SMEM), full3(N_EXPERTS)],
        out_specs=(full3(2), full3(2), pl.BlockSpec((8, 256), lambda: (0, 0)), pl.BlockSpec((1, 128), lambda: (0, 0))),
        out_shape=(
            jax.ShapeDtypeStruct((2, r, 128), jnp.int32),
            jax.ShapeDtypeStruct((2, r, 128), F32),
            jax.ShapeDtypeStruct((8, 256), jnp.int32),
            jax.ShapeDtypeStruct((1, 128), jnp.int32),
        ),
        compiler_params=pltpu.CompilerParams(vmem_limit_bytes=VMEM_LIMIT),
        name="route",
    )(b_router, lg3)
    return pos.reshape(2, n_tok), wts, te[:3].reshape(3 * 256), nv[0, :1]


SC_CORES = 2
SC_SUBCORES = 16
SC_WORKERS = SC_CORES * SC_SUBCORES
SC_CHUNK = 32


def _sc_mesh():
    return plsc.VectorSubcoreMesh(core_axis_name="c", subcore_axis_name="s",
                                  num_cores=SC_CORES, num_subcores=SC_SUBCORES)


def _sc_steps(n_rows):
    per_w = n_rows // SC_WORKERS
    steps = per_w // SC_CHUNK
    assert per_w * SC_WORKERS == n_rows and steps * SC_CHUNK == per_w and steps % 2 == 0, n_rows
    return per_w, steps


def _sc_gather_rows(table, idx):
    p = idx.shape[0]
    d = table.shape[1]
    per_w, steps = _sc_steps(p)
    idx3 = idx.reshape(SC_WORKERS, steps, SC_CHUNK)

    @functools.partial(
        pl.kernel, mesh=_sc_mesh(),
        out_type=jax.ShapeDtypeStruct((p, d), table.dtype),
        scratch_types=[
            pltpu.VMEM((steps, SC_CHUNK), jnp.int32),
            pltpu.VMEM((SC_CHUNK, d), table.dtype),
            pltpu.VMEM((SC_CHUNK, d), table.dtype),
            pltpu.SemaphoreType.DMA, pltpu.SemaphoreType.DMA,
            pltpu.SemaphoreType.DMA, pltpu.SemaphoreType.DMA,
        ],
        name="sc_gather_rows",
    )
    def k(table_hbm, idx_hbm, out_hbm, idx_v, buf0, buf1, g0, g1, w0, w1):
        wid = lax.axis_index("s") * SC_CORES + lax.axis_index("c")
        base = wid * per_w
        pltpu.sync_copy(idx_hbm.at[wid], idx_v)

        def gather(s, buf, sem):
            return pltpu.make_async_copy(table_hbm.at[idx_v.at[s]], buf, sem)

        def write(s, buf, sem):
            return pltpu.make_async_copy(buf, out_hbm.at[pl.ds(base + s * SC_CHUNK, SC_CHUNK)], sem)

        gather(0, buf0, g0).start()

        @pl.loop(0, steps, step=2)
        def _(s):
            gather(s + 1, buf1, g1).start()
            gather(s, buf0, g0).wait()
            write(s, buf0, w0).start()
            write(s, buf0, w0).wait()

            @pl.when(s + 2 < steps)
            def _():
                gather(s + 2, buf0, g0).start()

            gather(s + 1, buf1, g1).wait()
            write(s + 1, buf1, w1).start()
            write(s + 1, buf1, w1).wait()

    return k(table, idx3)


def _sc_dispatch(srcs, poss, p_rows):
    d = srcs[0].shape[1]
    dt = srcs[0].dtype
    plans = [_sc_steps(src.shape[0]) for src in srcs]
    idxs = [pos.reshape(2, SC_WORKERS, st, SC_CHUNK) for pos, (_, st) in zip(poss, plans)]
    nseg = len(srcs)
    scratch = [pltpu.VMEM((2, st, SC_CHUNK), jnp.int32) for _, st in plans]
    scratch += [pltpu.VMEM((SC_CHUNK, d), dt), pltpu.VMEM((SC_CHUNK, d), dt)]
    scratch += [pltpu.SemaphoreType.DMA] * 6

    @functools.partial(
        pl.kernel, mesh=_sc_mesh(),
        out_type=jax.ShapeDtypeStruct((p_rows, d), dt),
        scratch_types=scratch,
        name="sc_dispatch",
    )
    def k(*refs):
        src_hbm = refs[:nseg]
        idx_hbm = refs[nseg:2 * nseg]
        out_hbm = refs[2 * nseg]
        idx_v = refs[2 * nseg + 1:3 * nseg + 1]
        buf0, buf1, r0, r1, a0, a1, b0, b1 = refs[3 * nseg + 1:]
        wid = lax.axis_index("s") * SC_CORES + lax.axis_index("c")
        for seg in range(nseg):
            per_w, steps = plans[seg]
            base = wid * per_w
            for kk in range(2):
                pltpu.sync_copy(idx_hbm[seg].at[kk, wid], idx_v[seg].at[kk])

            def read(s, buf, sem, seg=seg, base=base):
                return pltpu.make_async_copy(src_hbm[seg].at[pl.ds(base + s * SC_CHUNK, SC_CHUNK)], buf, sem)

            def scat(kk, s, buf, sem, seg=seg):
                return pltpu.make_async_copy(buf, out_hbm.at[idx_v[seg].at[kk, s]], sem)

            read(0, buf0, r0).start()

            @pl.loop(0, steps, step=2)
            def _(s, read=read, scat=scat, steps=steps):
                read(s + 1, buf1, r1).start()
                read(s, buf0, r0).wait()
                scat(0, s, buf0, a0).start()
                scat(1, s, buf0, b0).start()
                scat(0, s, buf0, a0).wait()
                scat(1, s, buf0, b0).wait()

                @pl.when(s + 2 < steps)
                def _():
                    read(s + 2, buf0, r0).start()

                read(s + 1, buf1, r1).wait()
                scat(0, s + 1, buf1, a1).start()
                scat(1, s + 1, buf1, b1).start()
                scat(0, s + 1, buf1, a1).wait()
                scat(1, s + 1, buf1, b1).wait()

    return k(*srcs, *idxs)


MOE_TM = 512


MOE_TILE_SLOTS = 256


def _experts_kernel(te_ref, nv_ref, xs_ref, wg_hbm, wu_hbm, wd_hbm, ys_ref, wg_v, wu_v, wd_v, sem, *, layer):
    i = pl.program_id(0)
    valid = i < nv_ref[0]
    e = te_ref[i]
    first = jnp.logical_or(i == 0, e != te_ref[jnp.maximum(i - 1, 0)])
    slot = lax.rem(te_ref[MOE_TILE_SLOTS + i], 2)
    nxt = te_ref[2 * MOE_TILE_SLOTS + i]

    def weight_copies(expert, s):
        return (pltpu.make_async_copy(wg_hbm.at[layer, expert], wg_v.at[s], sem.at[s, 0]),
                pltpu.make_async_copy(wu_hbm.at[layer, expert], wu_v.at[s], sem.at[s, 1]),
                pltpu.make_async_copy(wd_hbm.at[layer, expert], wd_v.at[s], sem.at[s, 2]))

    @pl.when(jnp.logical_and(valid, i == 0))
    def _():
        for cp in weight_copies(e, slot):
            cp.start()

    @pl.when(jnp.logical_and(valid, first))
    def _():
        for cp in weight_copies(e, slot):
            cp.wait()

        @pl.when(nxt >= 0)
        def _():
            for cp in weight_copies(nxt, 1 - slot):
                cp.start()

    @pl.when(valid)
    def _():
        h = _unpack_bf16_pairs(xs_ref[...]).astype(BF16)
        half = D_EXPERT // 2
        y = None
        for j in range(2):
            sl = slice(j * half, (j + 1) * half)
            a = jnp.dot(h, wg_v[slot, :, sl].astype(BF16), preferred_element_type=F32)
            u = jnp.dot(h, wu_v[slot, :, sl].astype(BF16), preferred_element_type=F32)
            t = ((a * _sigmoid(a)) * u).astype(BF16)
            yj = jnp.dot(t, wd_v[slot, sl, :].astype(BF16), preferred_element_type=F32)
            y = yj if y is None else y + yj
        ys_ref[...] = _pack_bf16_pairs(y)


def _experts_call(xs, te, nv, wg, wu, wd, layer):
    p_rows, dh = xs.shape
    d = 2 * dh
    tm = MOE_TM
    nt = p_rows // tm
    assert nt <= MOE_TILE_SLOTS and te.shape == (3 * MOE_TILE_SLOTS,)
    row = lambda i, te_r, nv_r: (jnp.minimum(i, nv_r[0] - 1), 0)
    grid_spec = pltpu.PrefetchScalarGridSpec(
        num_scalar_prefetch=2,
        grid=(nt,),
        in_specs=[
            pl.BlockSpec((tm, dh), row),
            pl.BlockSpec(memory_space=pl.ANY),
            pl.BlockSpec(memory_space=pl.ANY),
            pl.BlockSpec(memory_space=pl.ANY),
        ],
        out_specs=pl.BlockSpec((tm, dh), row),
        scratch_shapes=[
            pltpu.VMEM((2, d, D_EXPERT), F32),
            pltpu.VMEM((2, d, D_EXPERT), F32),
            pltpu.VMEM((2, D_EXPERT, d), F32),
            pltpu.SemaphoreType.DMA((2, 3)),
        ],
    )
    return pl.pallas_call(
        functools.partial(_experts_kernel, layer=layer),
        grid_spec=grid_spec,
        out_shape=jax.ShapeDtypeStruct((p_rows, dh), jnp.int32),
        compiler_params=_cparams(("arbitrary",)),
        name="moe_experts",
    )(te, nv, xs, wg, wu, wd)


COMBINE_TM = 1024


def _combine_kernel(y_ref, wt_ref, x_ref, mod_ref, fn_ref, o_ref, *, final):
    xo = x_ref[...] + mod_ref[5:6, :] * _moe_mix(y_ref, wt_ref[0], wt_ref[1])
    if final:
        ms = jnp.mean(xo * xo, axis=-1, keepdims=True)
        xo = xo * lax.rsqrt(ms + EPS) * fn_ref[...]
    o_ref[...] = xo


def _combine_call(y2, wts3, row0, x, mod_l, mod_row, fn, final):
    b, n, d = x.shape
    n_tok = b * n
    tm = COMBINE_TM
    assert n_tok % tm == 0 and row0 % 8 == 0 and (mod_row is not None or n % tm == 0)
    if mod_row is None:
        mod_map = lambda i: ((i * tm) // n, 0, 0)
    else:
        mod_map = lambda i: (mod_row, 0, 0)
    out = pl.pallas_call(
        functools.partial(_combine_kernel, final=final),
        grid=(n_tok // tm,),
        in_specs=[
            pl.BlockSpec((2, tm, d // 2), lambda i: (0, i, 0)),
            pl.BlockSpec((2, tm // 128, 128), lambda i: (0, row0 // 8 + i, 0)),
            pl.BlockSpec((tm, d), lambda i: (i, 0)),
            pl.BlockSpec((None, 6, d), mod_map),
            pl.BlockSpec((1, d), lambda i: (0, 0)),
        ],
        out_specs=pl.BlockSpec((tm, d), lambda i: (i, 0)),
        out_shape=jax.ShapeDtypeStruct((n_tok, d), F32),
        compiler_params=_cparams(("parallel",)),
        name="moe_combine",
    )(y2, wts3, x.reshape(n_tok, d), mod_l, fn)
    return out.reshape(b, n, d)


def _moe_sparse(h_list, lg_list, x_list, mod_l, mod_rows, b_router, wg, wu, wd, layer, fn, final):
    d = h_list[0].shape[-1]
    sizes = [h.shape[0] * h.shape[1] for h in h_list]
    n_tok = sum(sizes)
    lgt = lg_list[0] if len(lg_list) == 1 else jnp.concatenate(lg_list, axis=1)
    pos, wts, te, nv = _route_call(lgt, b_router, MOE_TM)
    p_rows = 2 * n_tok + N_EXPERTS * MOE_TM
    offs = np.cumsum([0] + sizes)
    poss = [pos[:, offs[i]:offs[i + 1]] for i in range(len(sizes))]
    xs = _sc_dispatch([h.reshape(-1, d) for h in h_list], poss, p_rows)
    ys = _experts_call(xs, te, nv, wg, wu, wd, layer)
    outs = []
    for i, x in enumerate(x_list):
        y2 = _sc_gather_rows(ys, poss[i].reshape(-1)).reshape(2, sizes[i], d)
        row0 = int(offs[i]) // 128
        if final:
            outs.append(_combine_call(y2, wts, row0, x, mod_l, mod_rows[i], fn, True))
        else:
            outs.append((y2, wts, row0, mod_l))
    return outs


def _winprep_kernel(wt_ref, o_ref):
    gd0 = C_AQ
    gdw = 2 * GLA_GATE_RANK
    tail = W_IN_REF_COLS - gd0 - gdw
    o_ref[:, 0:gd0] = wt_ref[0:gd0, :].T.astype(BF16)
    o_ref[:, gd0:gd0 + tail] = wt_ref[gd0 + gdw:W_IN_REF_COLS, :].T.astype(BF16)
    gd = wt_ref[gd0:gd0 + 128, :].T
    lane = lax.broadcasted_iota(jnp.int32, gd.shape, 1)
    o_ref[:, C_GD:] = jnp.where(lane < gdw, gd, 0.0).astype(BF16)


def _winprep_call(w_in):
    depth, d, cols = w_in.shape
    assert cols == W_IN_REF_COLS and C_GD == cols - 2 * GLA_GATE_RANK and W_IN_COLS - C_GD == 128
    return pl.pallas_call(
        _winprep_kernel,
        grid=(depth,),
        in_specs=[pl.BlockSpec((None, cols, d), lambda l: (l, 0, 0))],
        out_specs=pl.BlockSpec((None, d, W_IN_COLS), lambda l: (l, 0, 0)),
        out_shape=jax.ShapeDtypeStruct((depth, d, W_IN_COLS), BF16),
        compiler_params=_cparams(("parallel",)),
        name="w_in_prep",
    )(jnp.swapaxes(w_in, 1, 2))


def _gate_up_weights(w_up, b_up):
    z = jnp.zeros((GLA_GATE_RANK, GLA_QK), w_up.dtype)
    top = jnp.concatenate([w_up[0], z], axis=1)
    mid = jnp.concatenate([z, w_up[1]], axis=1)
    pad = jnp.zeros((128 - 2 * GLA_GATE_RANK, 2 * GLA_QK), w_up.dtype)
    return jnp.concatenate([top, mid, pad], axis=0).astype(BF16), b_up.reshape(1, 2 * GLA_QK)


def kernel(x, c, ctx, c_ctx, w_ada, b_ada, norm_mix, norm_ffn, w_in, w_gla_gate_up, b_gla_gate, gla_norm, q_norm,
           k_norm, w_out, w_router, b_router, w_exp_gate, w_exp_up, w_exp_down, final_norm):
    b, n, d = x.shape
    m = ctx.shape[1]
    depth = w_ada.shape[0]
    assert d == D_MODEL and n % GLA_PAIR == 0 and m % GLA_PAIR == 0 and n % GRID_W == 0

    rows = ((b + 1 + 7) // 8) * 8
    cv = jnp.concatenate([c, c_ctx[None, :], jnp.zeros((rows - b - 1, d), F32)], axis=0)
    mod = _ada_call(cv, w_ada, b_ada).reshape(depth, rows, 6, d)

    cs = jnp.asarray(_channel_dft_table()).astype(BF16)
    tab_x = jnp.asarray(_seq_dft_table(n)).astype(BF16)
    tab_c = jnp.asarray(_seq_dft_table(m)).astype(BF16)
    rope_tabs = tuple(jnp.asarray(t) for t in _rope_tables(n))
    bd512 = jnp.asarray(_blockdiag_ones(ATT_WIDTH, ATT_HDIM)).astype(BF16)
    bd256 = jnp.asarray(_blockdiag_ones(GLA_WIDTH, GLA_DV)).astype(BF16)
    wr_t = w_router.T
    wr_hi = wr_t.astype(BF16)
    wrh = jnp.concatenate([wr_hi, (wr_t - wr_hi.astype(F32)).astype(BF16)], axis=0)
    fn = final_norm.reshape(1, d)

    w_in_perm = _winprep_call(w_in)

    xc = ctx
    pend_x = pend_c = None
    for l in range(depth):
        ctx_out = l < depth - 1
        mod_l = mod[l]
        w_perm = w_in_perm[l]
        wup, bup = _gate_up_weights(w_gla_gate_up[l], b_gla_gate[l])
        nw = norm_mix[l].reshape(1, d)
        nf = norm_ffn[l].reshape(1, d)
        qn = jnp.tile(q_norm[l], ATT_HEADS).reshape(1, ATT_WIDTH)
        kn = jnp.tile(k_norm[l], ATT_KV_HEADS).reshape(1, 128)
        gn = jnp.tile(gla_norm[l], GLA_HEADS).reshape(1, GLA_WIDTH)
        wo = w_out[l].astype(BF16)

        outs_c = _inproj_call(xc, mod_l, b, nw, w_perm, cs, wup, bup, qn, kn, bd512, None, pend_c)
        outs_x = _inproj_call(x, mod_l, None, nw, w_perm, cs, wup, bup, qn, kn, bd512, rope_tabs, pend_x)
        (ab_c, gqk_c, gv_c, gvt_c, gg_c, la_c, q_c, kt_c, vd_c) = outs_c[:9]
        (ab_x, gqk_x, gv_x, gvt_x, gg_x, la_x, q_x, kt_x, vd_x) = outs_x[:9]
        if pend_x is not None:
            x, xc = outs_x[9], outs_c[9]

        s_zero = jnp.zeros((b, 2, GLA_WIDTH, GLA_QK), F32)
        of_c, ob_c, s_fin = _gla_call(gqk_c, gv_c, gvt_c, la_c, s_zero)
        of_x, ob_x, _ = _gla_call(gqk_x, gv_x, gvt_x, la_x, s_fin)

        f_x = _seqdft_call(tab_x, ab_x.reshape(2 * n, b * FNET_WIDTH), b)
        x, h2_x, lg_x = _attn_outproj_call(q_x, [(kt_c, vd_c), (kt_x, vd_x)], f_x, of_x, ob_x, gg_x, x,
                                           mod_l, None, wo, gn, bd256, nf, wrh)

        if ctx_out:
            f_c = _seqdft_call(tab_c, ab_c.reshape(2 * m, b * FNET_WIDTH), b)
            xc, h2_c, lg_c = _attn_outproj_call(q_c, [(kt_c, vd_c)], f_c, of_c, ob_c, gg_c, xc,
                                                mod_l, b, wo, gn, bd256, nf, wrh)

        final = l == depth - 1
        wexp = (w_exp_gate, w_exp_up, w_exp_down, l)
        if ctx_out:
            res = _moe_sparse([h2_x, h2_c], [lg_x, lg_c], [x, xc], mod_l, [None, b], b_router, *wexp, fn, final)
        else:
            res = _moe_sparse([h2_x], [lg_x], [x], mod_l, [None], b_router, *wexp, fn, final)
        if final:
            x = res[0]
        else:
            pend_x, pend_c = res
    return x
```

```python
import functools

import numpy as np
import jax
import jax.numpy as jnp
from jax import lax
from jax.experimental import pallas as pl
from jax.experimental.pallas import tpu as pltpu
from jax.experimental.pallas import tpu_sc as plsc

F32 = jnp.float32
BF16 = jnp.bfloat16

D_MODEL = 1024
GRID_W = 64
EPS = 1e-6
LOG2E = 1.4426950408889634

FNET_WIDTH = 256
FNET_GROUPS = 4
FNET_GDIM = 64

GLA_HEADS = 4
GLA_DV = 64
GLA_DK = 32
GLA_WIDTH = 256
GLA_QK = 128
GLA_GATE_RANK = 16
GLA_GATE_NORM = 16.0
GLA_CHUNK = 64
GLA_PAIR = 2 * GLA_CHUNK

ATT_HEADS = 8
ATT_KV_HEADS = 2
ATT_HDIM = 64
ATT_WIDTH = 512
ROPE_FREQS = 16
ROPE_THETA = 10000.0

N_EXPERTS = 16
N_GROUPS = 4
EXPERTS_PER_GROUP = 4
D_EXPERT = 512

C_U = 0
C_GQ = 256
C_GK = 384
C_GV = 512
C_GG = 768
C_AQ = 1024
C_AK = 1536
C_GD = 1792
W_IN_COLS = 1920
W_IN_REF_COLS = 1824

VMEM_LIMIT = 56 * 1024 * 1024


def _cparams(sem):
    return pltpu.CompilerParams(dimension_semantics=sem, vmem_limit_bytes=VMEM_LIMIT)


def _sigmoid(x):
    return 1.0 / (1.0 + jnp.exp(-x))


def _pack_bf16_pairs(x):
    blocks = []
    for t in range(x.shape[1] // 256):
        lo = lax.bitcast_convert_type(x[:, 256 * t:256 * t + 128].astype(BF16).astype(F32), jnp.uint32)
        hi = lax.bitcast_convert_type(x[:, 256 * t + 128:256 * t + 256].astype(BF16).astype(F32), jnp.uint32)
        blocks.append((lo >> 16) | (hi & jnp.uint32(0xFFFF0000)))
    return lax.bitcast_convert_type(jnp.concatenate(blocks, axis=1), jnp.int32)


def _unpack_bf16_pairs(p):
    u = lax.bitcast_convert_type(p, jnp.uint32)
    blocks = []
    for t in range(p.shape[1] // 128):
        word = u[:, 128 * t:128 * (t + 1)]
        blocks += [lax.bitcast_convert_type(word << 16, F32),
                   lax.bitcast_convert_type(word & jnp.uint32(0xFFFF0000), F32)]
    return jnp.concatenate(blocks, axis=1)


def _token_columns(w):
    nr = w.shape[0]
    tm = nr * 128
    lane = lax.broadcasted_iota(jnp.int32, (tm, 128), 1)
    row = lax.broadcasted_iota(jnp.int32, (tm, 128), 0)
    wb = jnp.concatenate([jnp.broadcast_to(w[r:r + 1, :], (128, 128)) for r in range(nr)], axis=0)
    return jnp.sum(jnp.where(lane == (row % 128), wb, 0.0), axis=1, keepdims=True)


def _moe_mix(y_ref, w0, w1):
    return (_token_columns(w0) * _unpack_bf16_pairs(y_ref[0])
            + _token_columns(w1) * _unpack_bf16_pairs(y_ref[1]))


def _nt_dot(a, b):
    return lax.dot_general(a, b, (((1,), (1,)), ((), ())), preferred_element_type=F32)


@functools.lru_cache(maxsize=None)
def _channel_dft_table():
    j = np.arange(FNET_GDIM)
    ang = 2.0 * np.pi * ((j[:, None] * j[None, :]) % FNET_GDIM) / FNET_GDIM
    c = np.cos(ang) / np.sqrt(FNET_GDIM)
    s = np.sin(ang) / np.sqrt(FNET_GDIM)
    out = np.zeros((FNET_WIDTH, 2 * FNET_WIDTH), np.float64)
    for g in range(FNET_GROUPS):
        sl = slice(g * FNET_GDIM, (g + 1) * FNET_GDIM)
        out[sl, sl] = c
        out[sl, FNET_WIDTH + g * FNET_GDIM:FNET_WIDTH + (g + 1) * FNET_GDIM] = s
    return out.astype(np.float32)


@functools.lru_cache(maxsize=None)
def _seq_dft_table(n):
    j = np.arange(n, dtype=np.int64)
    ang = 2.0 * np.pi * ((j[:, None] * j[None, :]) % n) / n
    return np.concatenate([np.cos(ang), -np.sin(ang)], axis=1).astype(np.float32) / np.float32(np.sqrt(n))


@functools.lru_cache(maxsize=None)
def _rope_tables(n):
    rows = n // GRID_W
    row = np.repeat(np.arange(rows), GRID_W).astype(np.float64)
    col = np.tile(np.arange(GRID_W), rows).astype(np.float64)
    inv = ROPE_THETA ** (-np.arange(ROPE_FREQS, dtype=np.float64) * 2.0 / (2 * ROPE_FREQS))
    ar = row[:, None] * inv[None, :]
    ac = col[:, None] * inv[None, :]
    cos = np.concatenate([np.cos(ar), np.cos(ar), np.cos(ac), np.cos(ac)], axis=1)
    sin = np.concatenate([-np.sin(ar), np.sin(ar), -np.sin(ac), np.sin(ac)], axis=1)
    return (np.tile(cos, (1, 2)).astype(np.float32), np.tile(sin, (1, 2)).astype(np.float32))


@functools.lru_cache(maxsize=None)
def _blockdiag_ones(width, blk):
    i = np.arange(width)
    return (i[:, None] // blk == i[None, :] // blk).astype(np.float32)


def _ada_kernel(cv_ref, w_ref, b_ref, o_ref):
    cv = cv_ref[...]
    a = (cv * _sigmoid(cv)).astype(BF16)
    o_ref[...] = jnp.dot(a, w_ref[...].astype(BF16), preferred_element_type=F32) + b_ref[...]


def _ada_call(cv, w_ada, b_ada):
    depth, d, d6 = w_ada.shape
    tn = 1536
    rows = cv.shape[0]
    return pl.pallas_call(
        _ada_kernel,
        grid=(depth, d6 // tn),
        in_specs=[
            pl.BlockSpec((rows, d), lambda l, j: (0, 0)),
            pl.BlockSpec((None, d, tn), lambda l, j: (l, 0, j)),
            pl.BlockSpec((None, 1, tn), lambda l, j: (l, 0, j)),
        ],
        out_specs=pl.BlockSpec((None, rows, tn), lambda l, j: (l, 0, j)),
        out_shape=jax.ShapeDtypeStruct((depth, rows, d6), F32),
        compiler_params=_cparams(("parallel", "parallel")),
        name="ada_mod",
    )(cv, w_ada, b_ada.reshape(depth, 1, d6))


def _swap16(x):
    lane = lax.broadcasted_iota(jnp.int32, x.shape, 1)
    first = (lane % 32) < 16
    return jnp.where(first, pltpu.roll(x, 112, 1), pltpu.roll(x, 16, 1))


def _head_rms(x, bd, w):
    ms = jnp.dot((x * x).astype(BF16), bd, preferred_element_type=F32) * (1.0 / ATT_HDIM)
    return x * lax.rsqrt(ms + EPS) * w


def _inproj_kernel(*refs, rope, pending_rows, names):
    refs = list(refs)
    x_ref, mod_ref, nw_ref, w_ref, cs_ref, wup_ref, bup_ref, qn_ref, kn_ref, bd_ref = refs[:10]
    del refs[:10]
    if rope:
        cos_ref, sin_ref = refs[:2]
        del refs[:2]
    if pending_rows is not None:
        y_ref, wt_ref, modp_ref = refs[:3]
        del refs[:3]
    out = dict(zip(names, refs))
    full = "q" in out
    x = x_ref[...]
    if pending_rows is not None:
        xnew_ref = out["x"]
        row0, rows_per_sample = pending_rows
        nr = x.shape[0] // 128
        r = row0 + pl.program_id(0) * rows_per_sample + pl.program_id(1) * nr
        sub = lax.rem(r, 8)
        w = [wt_ref[kk, 0:nr, :] for kk in range(2)]
        for blk in range(1, 8 // nr):
            w = [jnp.where(sub == blk * nr, wt_ref[kk, blk * nr:(blk + 1) * nr, :], w[kk]) for kk in range(2)]
        x = x + modp_ref[5:6, :] * _moe_mix(y_ref, w[0], w[1])
        xnew_ref[...] = x
    ms = jnp.mean(x * x, axis=-1, keepdims=True)
    y = x * lax.rsqrt(ms + EPS) * nw_ref[...]
    h = y * (1.0 + mod_ref[1:2, :]) + mod_ref[0:1, :]
    hb = h.astype(BF16)

    def proj(c0, width):
        return jnp.dot(hb, w_ref[:, c0:c0 + width], preferred_element_type=F32)

    if full:
        uab = jnp.dot(proj(C_U, FNET_WIDTH).astype(BF16), cs_ref[...], preferred_element_type=F32)
        out["ab"][0] = uab[:, :FNET_WIDTH].astype(BF16)
        out["ab"][1] = uab[:, FNET_WIDTH:].astype(BF16)

    out["gqk"][...] = proj(C_GQ, 2 * GLA_QK)
    gv = proj(C_GV, GLA_WIDTH)
    out["gv"][...] = gv.astype(BF16)
    out["gvt"][...] = gv.T.astype(BF16)
    if full:
        out["gg"][...] = proj(C_GG, GLA_WIDTH).astype(BF16)
    pre = jnp.dot(proj(C_GD, 128).astype(BF16), wup_ref[...], preferred_element_type=F32) + bup_ref[...]
    out["la"][...] = (jnp.minimum(pre, 0.0) - jnp.log1p(jnp.exp(-jnp.abs(pre)))) * (1.0 / GLA_GATE_NORM)

    bd = bd_ref[...]
    kv = proj(C_AK, 256)
    k = _head_rms(kv[:, :128], bd[:128, :128], kn_ref[...])
    if rope:
        cos = cos_ref[...]
        sin = sin_ref[...]
        k = k * cos + _swap16(k) * sin
    if full:
        q = _head_rms(proj(C_AQ, ATT_WIDTH), bd, qn_ref[...])
        if rope:
            q = jnp.concatenate(
                [q[:, s:s + 128] * cos + _swap16(q[:, s:s + 128]) * sin for s in range(0, ATT_WIDTH, 128)], axis=1)
        out["q"][...] = (q * (ATT_HDIM ** -0.5 * LOG2E)).astype(BF16)
    kt_ref, vd_ref = out["kt"], out["vd"]
    v = kv[:, 128:]
    lo = lax.broadcasted_iota(jnp.int32, k.shape, 1) < ATT_HDIM
    k_sw = pltpu.roll(k, ATT_HDIM, 1)
    v_sw = pltpu.roll(v, ATT_HDIM, 1)
    kt_ref[0] = jnp.where(lo, k, k_sw).T.astype(BF16)
    kt_ref[1] = jnp.where(lo, k_sw, k).T.astype(BF16)
    vd_ref[0] = jnp.where(lo, v, 1.0).astype(BF16)
    vd_ref[1] = jnp.where(lo, 1.0, v_sw).astype(BF16)
    vd_ref[2] = jnp.where(lo, v_sw, 1.0).astype(BF16)
    vd_ref[3] = jnp.where(lo, 1.0, v).astype(BF16)


def _inproj_call(x, mod_l, mod_row, nw, w_perm, cs, wup, bup, qn, kn, bd, rope_tabs, pending=None, full=True):
    b, n, d = x.shape
    tm = min(1024, n)
    nt = n // tm
    rope = rope_tabs is not None
    if mod_row is None:
        mod_map = lambda bi, i: (bi, 0, 0)
    else:
        mod_map = lambda bi, i: (mod_row, 0, 0)
    const = lambda bi, i: (0, 0)
    in_specs = [
        pl.BlockSpec((None, tm, d), lambda bi, i: (bi, i, 0)),
        pl.BlockSpec((None, 6, d), mod_map),
        pl.BlockSpec((1, d), const),
        pl.BlockSpec((None, d, W_IN_COLS), lambda bi, i: (w_perm[1], 0, 0)),
        pl.BlockSpec((FNET_WIDTH, 2 * FNET_WIDTH), const),
        pl.BlockSpec((128, 2 * GLA_QK), const),
        pl.BlockSpec((1, 2 * GLA_QK), const),
        pl.BlockSpec((1, ATT_WIDTH), const),
        pl.BlockSpec((1, 128), const),
        pl.BlockSpec((ATT_WIDTH, ATT_WIDTH), const),
    ]
    args = [x, mod_l, nw, w_perm[0], cs, wup, bup, qn, kn, bd]
    if rope:
        in_specs += [pl.BlockSpec((tm, 128), lambda bi, i: (i, 0)), pl.BlockSpec((tm, 128), lambda bi, i: (i, 0))]
        args += list(rope_tabs)
    pending_rows = None
    if pending is not None:
        y2, wts3, row0, mod_prev = pending
        nr = tm // 128
        rps = n // 128
        assert 8 % nr == 0 and row0 % nr == 0 and rps % nr == 0
        pending_rows = (row0, rps)
        in_specs += [
            pl.BlockSpec((2, None, tm, d // 2), lambda bi, i: (0, bi, i, 0)),
            pl.BlockSpec((2, 8, 128), lambda bi, i: (0, (row0 + bi * rps + i * nr) // 8, 0)),
            pl.BlockSpec((None, 6, d), mod_map),
        ]
        args += [y2.reshape(2, b, n, d // 2), wts3, mod_prev]
    tok = lambda w: pl.BlockSpec((None, tm, w), lambda bi, i: (bi, i, 0))
    outs = []
    if full:
        outs.append(("ab", jax.ShapeDtypeStruct((2, n, b * FNET_WIDTH), BF16),
                     pl.BlockSpec((2, tm, FNET_WIDTH), lambda bi, i: (0, i, bi))))
    outs += [
        ("gqk", jax.ShapeDtypeStruct((b, n, 2 * GLA_QK), F32), tok(2 * GLA_QK)),
        ("gv", jax.ShapeDtypeStruct((b, n, GLA_WIDTH), BF16), tok(GLA_WIDTH)),
        ("gvt", jax.ShapeDtypeStruct((b, GLA_WIDTH, n), BF16),
         pl.BlockSpec((None, GLA_WIDTH, tm), lambda bi, i: (bi, 0, i))),
    ]
    if full:
        outs.append(("gg", jax.ShapeDtypeStruct((b, n, GLA_WIDTH), BF16), tok(GLA_WIDTH)))
    outs.append(("la", jax.ShapeDtypeStruct((b, n, 2 * GLA_QK), F32), tok(2 * GLA_QK)))
    if full:
        outs.append(("q", jax.ShapeDtypeStruct((b, n, ATT_WIDTH), BF16), tok(ATT_WIDTH)))
    outs += [
        ("kt", jax.ShapeDtypeStruct((b, ATT_KV_HEADS, 128, n), BF16),
         pl.BlockSpec((None, ATT_KV_HEADS, 128, tm), lambda bi, i: (bi, 0, 0, i))),
        ("vd", jax.ShapeDtypeStruct((b, 2 * ATT_KV_HEADS, n, 128), BF16),
         pl.BlockSpec((None, 2 * ATT_KV_HEADS, tm, 128), lambda bi, i: (bi, 0, i, 0))),
    ]
    if pending is not None:
        outs.append(("x", jax.ShapeDtypeStruct((b, n, d), F32), tok(d)))
    names = tuple(o[0] for o in outs)
    res = pl.pallas_call(
        functools.partial(_inproj_kernel, rope=rope, pending_rows=pending_rows, names=names),
        grid=(b, nt),
        in_specs=in_specs,
        out_specs=tuple(o[2] for o in outs),
        out_shape=tuple(o[1] for o in outs),
        compiler_params=_cparams(("parallel", "parallel")),
        name="inproj_rope" if rope else "inproj_ctx",
    )(*args)
    return dict(zip(names, res))


def _seqdft_kernel(t_ref, ab_ref, o_ref):
    y = jnp.dot(t_ref[...], ab_ref[...], preferred_element_type=F32)
    for bb in range(o_ref.shape[0]):
        o_ref[bb] = y[:, bb * FNET_WIDTH:(bb + 1) * FNET_WIDTH].astype(BF16)


def _seqdft_call(table, ab, b):
    n = table.shape[0]
    tm = min(512, n)
    nb = 4 if b % 4 == 0 else (2 if b % 2 == 0 else 1)
    return pl.pallas_call(
        _seqdft_kernel,
        grid=(b // nb, n // tm),
        in_specs=[
            pl.BlockSpec((tm, 2 * n), lambda c, i: (i, 0)),
            pl.BlockSpec((2 * n, nb * FNET_WIDTH), lambda c, i: (0, c)),
        ],
        out_specs=pl.BlockSpec((nb, tm, FNET_WIDTH), lambda c, i: (c, i, 0)),
        out_shape=jax.ShapeDtypeStruct((b, n, FNET_WIDTH), BF16),
        compiler_params=_cparams(("parallel", "parallel")),
        name="seq_dft",
    )(table, ab)


def _gla_dir(qk, v, vt, a, s_in, fwd):
    p = GLA_PAIR
    r = lax.broadcasted_iota(jnp.int32, (p, p), 0)
    c = lax.broadcasted_iota(jnp.int32, (p, p), 1)
    same = (r // GLA_CHUNK) == (c // GLA_CHUNK)
    tri = same & ((c <= r) if fwd else (c >= r))
    row_lo = r < GLA_CHUNK
    rin = r % GLA_CHUNK

    q = qk[:, :GLA_QK] * (GLA_DK ** -0.5)
    k = qk[:, GLA_QK:]
    cum = a
    sh = 1
    while sh < GLA_CHUNK:
        if fwd:
            cum = cum + jnp.where(rin >= sh, pltpu.roll(cum, sh, 0), 0.0)
        else:
            cum = cum + jnp.where(rin < GLA_CHUNK - sh, pltpu.roll(cum, p - sh, 0), 0.0)
        sh *= 2
    if fwd:
        last0, last1 = cum[GLA_CHUNK - 1:GLA_CHUNK, :], cum[p - 1:p, :]
    else:
        last0, last1 = cum[0:1, :], cum[GLA_CHUNK:GLA_CHUNK + 1, :]
    lastb = jnp.where(row_lo, last0, last1)
    qt = q * jnp.exp(cum)
    kt = k * jnp.exp(-cum)
    kd = k * jnp.exp(lastb - cum)

    kt_b = kt.astype(BF16)
    zk = jnp.zeros_like(kt_b)
    ks = jnp.concatenate([jnp.where((c // GLA_DK) == hh, kt_b, zk) for hh in range(GLA_HEADS)], axis=0)
    att = _nt_dot(qt.astype(BF16), ks)
    tri4 = jnp.concatenate([tri] * GLA_HEADS, axis=1)
    att = jnp.where(tri4, att, 0.0).astype(BF16)
    col = lax.broadcasted_iota(jnp.int32, (p, GLA_WIDTH), 1)
    zv = jnp.zeros_like(v)
    vs = jnp.concatenate([jnp.where((col // GLA_DV) == hh, v, zv) for hh in range(GLA_HEADS)], axis=0)
    o_intra = jnp.dot(att, vs, preferred_element_type=F32)

    sr = lax.broadcasted_iota(jnp.int32, (GLA_WIDTH, GLA_QK), 0)
    sc = lax.broadcasted_iota(jnp.int32, (GLA_WIDTH, GLA_QK), 1)
    bdm = (sr // GLA_DV) == (sc // GLA_DK)
    first, second = (0, 1) if fwd else (1, 0)
    lasts = (last0, last1)
    in_chunk = (row_lo, jnp.logical_not(row_lo))
    kd2 = jnp.concatenate([jnp.where(in_chunk[0], kd, 0.0), jnp.where(in_chunk[1], kd, 0.0)], axis=1).astype(BF16)
    kvt2 = jnp.dot(vt, kd2, preferred_element_type=F32)
    kvt = (kvt2[:, :GLA_QK], kvt2[:, GLA_QK:])
    s_a = s_in
    s_b = s_a * jnp.exp(lasts[first]) + jnp.where(bdm, kvt[first], 0.0)
    s_c = s_b * jnp.exp(lasts[second]) + jnp.where(bdm, kvt[second], 0.0)
    q2 = jnp.concatenate([jnp.where(in_chunk[first], qt, 0.0), jnp.where(in_chunk[second], qt, 0.0)], axis=1)
    s2 = jnp.concatenate([s_a, s_b], axis=1).astype(BF16)
    o_inter = _nt_dot(q2.astype(BF16), s2)
    return o_intra + o_inter, s_c


def _gla_kernel(qkf, vf, vtf, laf, qkb, vb, vtb, lab, s0_ref, of_ref, ob_ref, sfin_ref, s_scr):
    i = pl.program_id(1)

    @pl.when(i == 0)
    def _():
        s_scr[...] = s0_ref[...]

    pairs = qkf.shape[1] // GLA_PAIR
    for gi in range(qkf.shape[0]):
        sf = s_scr[gi, 0]
        sb = s_scr[gi, 1]
        for pi in range(pairs):
            rf = slice(pi * GLA_PAIR, (pi + 1) * GLA_PAIR)
            rb = slice((pairs - 1 - pi) * GLA_PAIR, (pairs - pi) * GLA_PAIR)
            o1, sf = _gla_dir(qkf[gi, rf, :], vf[gi, rf, :], vtf[gi, :, rf], laf[gi, rf, :], sf, True)
            o2, sb = _gla_dir(qkb[gi, rb, :], vb[gi, rb, :], vtb[gi, :, rb], lab[gi, rb, :], sb, False)
            of_ref[gi, rf, :] = o1.astype(of_ref.dtype)
            ob_ref[gi, rb, :] = o2.astype(ob_ref.dtype)
        s_scr[gi, 0] = sf
        s_scr[gi, 1] = sb

    @pl.when(i == pl.num_programs(1) - 1)
    def _():
        sfin_ref[...] = s_scr[...]


def _gla_call(gqk, gv, gvt, la, s0):
    b, n, _ = gqk.shape
    p = GLA_PAIR * (2 if n % (2 * GLA_PAIR) == 0 else 1)
    npair = n // p
    gb = 4 if b % 4 == 0 else (2 if b % 2 == 0 else 1)
    fw = lambda bi, i: (bi, i, 0)
    bw = lambda bi, i: (bi, npair - 1 - i, 0)
    in_specs = [
        pl.BlockSpec((gb, p, 2 * GLA_QK), fw),
        pl.BlockSpec((gb, p, GLA_WIDTH), fw),
        pl.BlockSpec((gb, GLA_WIDTH, p), lambda bi, i: (bi, 0, i)),
        pl.BlockSpec((gb, p, GLA_QK), fw),
        pl.BlockSpec((gb, p, 2 * GLA_QK), bw),
        pl.BlockSpec((gb, p, GLA_WIDTH), bw),
        pl.BlockSpec((gb, GLA_WIDTH, p), lambda bi, i: (bi, 0, npair - 1 - i)),
        pl.BlockSpec((gb, p, GLA_QK), lambda bi, i: (bi, npair - 1 - i, 1)),
        pl.BlockSpec((gb, 2, GLA_WIDTH, GLA_QK), lambda bi, i: (bi, 0, 0, 0)),
    ]
    out_specs = (
        pl.BlockSpec((gb, p, GLA_WIDTH), fw),
        pl.BlockSpec((gb, p, GLA_WIDTH), bw),
        pl.BlockSpec((gb, 2, GLA_WIDTH, GLA_QK), lambda bi, i: (bi, 0, 0, 0)),
    )
    out_shape = (
        jax.ShapeDtypeStruct((b, n, GLA_WIDTH), BF16),
        jax.ShapeDtypeStruct((b, n, GLA_WIDTH), BF16),
        jax.ShapeDtypeStruct((b, 2, GLA_WIDTH, GLA_QK), F32),
    )
    return pl.pallas_call(
        _gla_kernel,
        grid=(b // gb, npair),
        in_specs=in_specs,
        out_specs=out_specs,
        out_shape=out_shape,
        scratch_shapes=[pltpu.VMEM((gb, 2, GLA_WIDTH, GLA_QK), F32)],
        compiler_params=_cparams(("parallel", "arbitrary")),
        name="gla_scan",
    )(gqk, gv, gvt, la, gqk, gv, gvt, la, s0)


def _attn_heads(q_ref, parts):
    tq = q_ref.shape[0]
    lane = lax.broadcasted_iota(jnp.int32, (tq, 128), 1)
    lo = lane < ATT_HDIM
    blocks = []
    for j in range(ATT_HEADS // 2):
        q128 = q_ref[:, 128 * j:128 * (j + 1)]
        g = (2 * j) // (ATT_HEADS // ATT_KV_HEADS)
        outs = []
        for half in range(2):
            qm = jnp.where(lo if half == 0 else jnp.logical_not(lo), q128, jnp.zeros_like(q128))
            ss = [jnp.dot(qm, kt_ref[g], preferred_element_type=F32) for kt_ref, _ in parts]
            m = functools.reduce(jnp.maximum, [jnp.max(s, axis=-1, keepdims=True) for s in ss])
            ps = [jnp.exp2(s - m).astype(BF16) for s in ss]
            pv = functools.reduce(
                lambda u, w: u + w,
                [jnp.dot(pp, vd_ref[2 * g + half], preferred_element_type=F32) for pp, (_, vd_ref) in zip(ps, parts)])
            den = pv[:, ATT_HDIM:ATT_HDIM + 1] if half == 0 else pv[:, 0:1]
            outs.append(pv / den)
        blocks.append(jnp.where(lo, outs[0], outs[1]).astype(BF16))
    return jnp.concatenate(blocks, axis=1)


def _attn_outproj_kernel(*refs, nparts):
    q_ref = refs[0]
    parts = [(refs[1 + 2 * i], refs[2 + 2 * i]) for i in range(nparts)]
    (f_ref, of_ref, ob_ref, gg_ref, x_ref, mod_ref, w_ref, gn_ref, bd_ref, nf_ref, wr_ref,
     xn_ref, h2_ref, lg_ref) = refs[1 + 2 * nparts:]
    att = _attn_heads(q_ref, parts)
    o = of_ref[...].astype(F32) + ob_ref[...].astype(F32)
    ms = jnp.dot((o * o).astype(BF16), bd_ref[...], preferred_element_type=F32) * (1.0 / GLA_DV)
    on = o * lax.rsqrt(ms + EPS) * gn_ref[...]
    g = gg_ref[...].astype(F32)
    gl = (on * (g * _sigmoid(g))).astype(BF16)
    ox = (jnp.dot(f_ref[...], w_ref[0:FNET_WIDTH, :], preferred_element_type=F32)
          + jnp.dot(gl, w_ref[FNET_WIDTH:FNET_WIDTH + GLA_WIDTH, :], preferred_element_type=F32)
          + jnp.dot(att, w_ref[FNET_WIDTH + GLA_WIDTH:, :], preferred_element_type=F32))
    xn = x_ref[...] + mod_ref[2:3, :] * ox
    xn_ref[...] = xn
    ms2 = jnp.mean(xn * xn, axis=-1, keepdims=True)
    h2 = xn * lax.rsqrt(ms2 + EPS) * nf_ref[...] * (1.0 + mod_ref[4:5, :]) + mod_ref[3:4, :]
    h2_ref[...] = _pack_bf16_pairs(h2)
    lg2 = _nt_dot(wr_ref[...], h2.astype(BF16))
    lg_ref[...] = lg2[:N_EXPERTS, :] + lg2[N_EXPERTS:, :]


def _attn_outproj_call(q, kv_parts, f, of, ob, gg, x, mod_l, mod_row, w_out, gn, bd, nf, wr):
    b, n, d = x.shape
    tm = min(1024, n)
    nt = n // tm
    if mod_row is None:
        mod_map = lambda bi, i: (bi, 0, 0)
    else:
        mod_map = lambda bi, i: (mod_row, 0, 0)
    const = lambda bi, i: (0, 0)
    tok = lambda w: pl.BlockSpec((None, tm, w), lambda bi, i: (bi, i, 0))
    in_specs = [tok(ATT_WIDTH)]
    args = [q]
    for kt, vd in kv_parts:
        m = kt.shape[-1]
        in_specs.append(pl.BlockSpec((None, ATT_KV_HEADS, 128, m), lambda bi, i: (bi, 0, 0, 0)))
        in_specs.append(pl.BlockSpec((None, 2 * ATT_KV_HEADS, m, 128), lambda bi, i: (bi, 0, 0, 0)))
        args += [kt, vd]
    in_specs += [
        tok(FNET_WIDTH), tok(GLA_WIDTH), tok(GLA_WIDTH), tok(GLA_WIDTH), tok(d),
        pl.BlockSpec((None, 6, d), mod_map),
        pl.BlockSpec((None, d, d), lambda bi, i: (w_out[1], 0, 0)),
        pl.BlockSpec((1, GLA_WIDTH), const),
        pl.BlockSpec((GLA_WIDTH, GLA_WIDTH), const),
        pl.BlockSpec((1, d), const),
        pl.BlockSpec((2 * N_EXPERTS, d), const),
    ]
    args += [f, of, ob, gg, x, mod_l, w_out[0], gn, bd, nf, wr]
    out_specs = (
        tok(d), tok(d // 2),
        pl.BlockSpec((N_EXPERTS, tm), lambda bi, i: (0, bi * nt + i)),
    )
    out_shape = (
        jax.ShapeDtypeStruct((b, n, d), F32),
        jax.ShapeDtypeStruct((b, n, d // 2), jnp.int32),
        jax.ShapeDtypeStruct((N_EXPERTS, b * n), F32),
    )
    return pl.pallas_call(
        functools.partial(_attn_outproj_kernel, nparts=len(kv_parts)),
        grid=(b, nt),
        in_specs=in_specs,
        out_specs=out_specs,
        out_shape=out_shape,
        compiler_params=_cparams(("parallel", "parallel")),
        name="attn_outproj",
    )(*args)


def _route_kernel(b_ref, lg_ref, pos_ref, wt_ref, te_ref, nv_ref, *, tm):
    r = lg_ref.shape[1]
    s = [_sigmoid(lg_ref[e]) for e in range(N_EXPERTS)]
    sel = [s[e] + b_ref[e] for e in range(N_EXPERTS)]
    grp = []
    for g in range(N_GROUPS):
        a, b, c, d = sel[4 * g:4 * g + 4]
        hi1, lo1 = jnp.maximum(a, b), jnp.minimum(a, b)
        hi2, lo2 = jnp.maximum(c, d), jnp.minimum(c, d)
        m1 = jnp.maximum(hi1, hi2)
        m2 = jnp.maximum(jnp.minimum(hi1, hi2), jnp.maximum(lo1, lo2))
        grp.append(m1 + m2)
    one = jnp.ones_like(s[0])
    zero = jnp.zeros_like(s[0])
    msk = []
    for g in range(N_GROUPS):
        isg = one
        for g2 in range(N_GROUPS):
            if g2 < g:
                isg = isg * jnp.where(grp[g] > grp[g2], one, zero)
            elif g2 > g:
                isg = isg * jnp.where(grp[g] >= grp[g2], one, zero)
        for li in range(EXPERTS_PER_GROUP):
            e = 4 * g + li
            rank = zero
            for lj in range(EXPERTS_PER_GROUP):
                ej = 4 * g + lj
                if lj < li:
                    rank = rank + jnp.where(sel[ej] >= sel[e], one, zero)
                elif lj > li:
                    rank = rank + jnp.where(sel[ej] > sel[e], one, zero)
            msk.append(jnp.where(rank < 2.0, isg, zero))
    den = functools.reduce(lambda u, v: u + v, [msk[e] * s[e] for e in range(N_EXPERTS)])

    li_ = lax.broadcasted_iota(jnp.int32, (128, 128), 0)
    lj_ = lax.broadcasted_iota(jnp.int32, (128, 128), 1)
    upper = jnp.where(li_ < lj_, 1.0, 0.0).astype(BF16)
    ri_ = lax.broadcasted_iota(jnp.int32, (r, r), 0)
    rj_ = lax.broadcasted_iota(jnp.int32, (r, r), 1)
    lower = jnp.where(rj_ < ri_, 1.0, 0.0).astype(BF16)
    tiles = (1, te_ref.shape[1])
    tile_start = lax.broadcasted_iota(jnp.int32, tiles, 1).astype(F32) * float(tm)
    te = jnp.zeros(tiles, F32)
    seg = jnp.zeros(tiles, F32)
    nonempty = []
    off = jnp.zeros((1, 1), F32)
    seen = zero
    pos = [zero, zero]
    wts = [zero, zero]
    for e in range(N_EXPERTS):
        mb = msk[e].astype(BF16)
        lane_pre = jnp.dot(mb, upper, preferred_element_type=F32)
        row_pre = jnp.sum(jnp.dot(lower, mb, preferred_element_type=F32), axis=1, keepdims=True)
        cnt = jnp.sum(jnp.sum(msk[e], axis=1, keepdims=True), axis=0, keepdims=True)
        p_e = off + row_pre + lane_pre
        g_e = s[e] / den
        for kk in range(2):
            hit = msk[e] * jnp.where(seen == float(kk), one, zero)
            pos[kk] = pos[kk] + hit * p_e
            wts[kk] = wts[kk] + hit * g_e
        seen = seen + msk[e]
        off = off + jnp.floor((cnt + float(tm - 1)) * (1.0 / tm)) * float(tm)
        passed = jnp.where(tile_start >= off, 1.0, 0.0)
        te = te + passed
        nonempty.append(jnp.where(cnt > 0.0, 1.0, 0.0))
        seg = seg + nonempty[e] * passed
    for kk in range(2):
        pos_ref[kk] = pos[kk].astype(jnp.int32)
        wt_ref[kk] = wts[kk]
    te = jnp.minimum(te, float(N_EXPERTS - 1))
    nxt = jnp.full(tiles, -1.0, F32)
    for e in reversed(range(N_EXPERTS)):
        nxt = jnp.where(jnp.logical_and(nonempty[e] > 0.0, te < float(e)), float(e), nxt)
    te_ref[0:1, :] = te.astype(jnp.int32)
    te_ref[1:2, :] = seg.astype(jnp.int32)
    te_ref[2:3, :] = nxt.astype(jnp.int32)
    te_ref[3:, :] = jnp.zeros((te_ref.shape[0] - 3, te_ref.shape[1]), jnp.int32)
    nv_ref[...] = jnp.broadcast_to(off * (1.0 / tm), nv_ref.shape).astype(jnp.int32)


def _route_call(lgt, b_router, tm):
    n_tok = lgt.shape[1]
    r = n_tok // 128
    assert r * 128 == n_tok and r % 8 == 0 and 2 * n_tok // tm + N_EXPERTS <= 256
    lg3 = lgt.reshape(N_EXPERTS, r, 128)
    full3 = lambda k: pl.BlockSpec((k, r, 128), lambda: (0, 0, 0))
    pos, wts, te, nv = pl.pallas_call(
        functools.partial(_route_kernel, tm=tm),
        in_specs=[pl.BlockSpec(memory_space=pltpu.SMEM), full3(N_EXPERTS)],
        out_specs=(full3(2), full3(2), pl.BlockSpec((8, 256), lambda: (0, 0)), pl.BlockSpec((1, 128), lambda: (0, 0))),
        out_shape=(
            jax.ShapeDtypeStruct((2, r, 128), jnp.int32),
            jax.ShapeDtypeStruct((2, r, 128), F32),
            jax.ShapeDtypeStruct((8, 256), jnp.int32),
            jax.ShapeDtypeStruct((1, 128), jnp.int32),
        ),
        compiler_params=pltpu.CompilerParams(vmem_limit_bytes=VMEM_LIMIT),
        name="route",
    )(b_router, lg3)
    return pos.reshape(2, n_tok), wts, te[:3].reshape(3 * 256), nv[0, :1]


SC_CORES = 2
SC_SUBCORES = 16
SC_WORKERS = SC_CORES * SC_SUBCORES
SC_CHUNK = 32


def _sc_mesh():
    return plsc.VectorSubcoreMesh(core_axis_name="c", subcore_axis_name="s",
                                  num_cores=SC_CORES, num_subcores=SC_SUBCORES)


def _sc_steps(n_rows):
    per_w = n_rows // SC_WORKERS
    steps = per_w // SC_CHUNK
    assert per_w * SC_WORKERS == n_rows and steps * SC_CHUNK == per_w and steps % 2 == 0, n_rows
    return per_w, steps


def _sc_gather_rows(table, idx):
    p = idx.shape[0]
    d = table.shape[1]
    per_w, steps = _sc_steps(p)
    idx3 = idx.reshape(SC_WORKERS, steps, SC_CHUNK)

    @functools.partial(
        pl.kernel, mesh=_sc_mesh(),
        out_type=jax.ShapeDtypeStruct((p, d), table.dtype),
        scratch_types=[
            pltpu.VMEM((steps, SC_CHUNK), jnp.int32),
            pltpu.VMEM((SC_CHUNK, d), table.dtype),
            pltpu.VMEM((SC_CHUNK, d), table.dtype),
            pltpu.SemaphoreType.DMA, pltpu.SemaphoreType.DMA,
            pltpu.SemaphoreType.DMA, pltpu.SemaphoreType.DMA,
        ],
        name="sc_gather_rows",
    )
    def k(table_hbm, idx_hbm, out_hbm, idx_v, buf0, buf1, g0, g1, w0, w1):
        wid = lax.axis_index("s") * SC_CORES + lax.axis_index("c")
        base = wid * per_w
        pltpu.sync_copy(idx_hbm.at[wid], idx_v)

        def gather(s, buf, sem):
            return pltpu.make_async_copy(table_hbm.at[idx_v.at[s]], buf, sem)

        def write(s, buf, sem):
            return pltpu.make_async_copy(buf, out_hbm.at[pl.ds(base + s * SC_CHUNK, SC_CHUNK)], sem)

        gather(0, buf0, g0).start()

        @pl.loop(0, steps, step=2)
        def _(s):
            gather(s + 1, buf1, g1).start()
            gather(s, buf0, g0).wait()
            write(s, buf0, w0).start()
            write(s, buf0, w0).wait()

            @pl.when(s + 2 < steps)
            def _():
                gather(s + 2, buf0, g0).start()

            gather(s + 1, buf1, g1).wait()
            write(s + 1, buf1, w1).start()
            write(s + 1, buf1, w1).wait()

    return k(table, idx3)


def _sc_dispatch(srcs, poss, p_rows):
    d = srcs[0].shape[1]
    dt = srcs[0].dtype
    plans = [_sc_steps(src.shape[0]) for src in srcs]
    idxs = [pos.reshape(2, SC_WORKERS, st, SC_CHUNK) for pos, (_, st) in zip(poss, plans)]
    nseg = len(srcs)
    scratch = [pltpu.VMEM((2, st, SC_CHUNK), jnp.int32) for _, st in plans]
    scratch += [pltpu.VMEM((SC_CHUNK, d), dt), pltpu.VMEM((SC_CHUNK, d), dt)]
    scratch += [pltpu.SemaphoreType.DMA] * 6

    @functools.partial(
        pl.kernel, mesh=_sc_mesh(),
        out_type=jax.ShapeDtypeStruct((p_rows, d), dt),
        scratch_types=scratch,
        name="sc_dispatch",
    )
    def k(*refs):
        src_hbm = refs[:nseg]
        idx_hbm = refs[nseg:2 * nseg]
        out_hbm = refs[2 * nseg]
        idx_v = refs[2 * nseg + 1:3 * nseg + 1]
        buf0, buf1, r0, r1, a0, a1, b0, b1 = refs[3 * nseg + 1:]
        wid = lax.axis_index("s") * SC_CORES + lax.axis_index("c")
        for seg in range(nseg):
            per_w, steps = plans[seg]
            base = wid * per_w
            for kk in range(2):
                pltpu.sync_copy(idx_hbm[seg].at[kk, wid], idx_v[seg].at[kk])

            def read(s, buf, sem, seg=seg, base=base):
                return pltpu.make_async_copy(src_hbm[seg].at[pl.ds(base + s * SC_CHUNK, SC_CHUNK)], buf, sem)

            def scat(kk, s, buf, sem, seg=seg):
                return pltpu.make_async_copy(buf, out_hbm.at[idx_v[seg].at[kk, s]], sem)

            read(0, buf0, r0).start()

            @pl.loop(0, steps, step=2)
            def _(s, read=read, scat=scat, steps=steps):
                read(s + 1, buf1, r1).start()
                read(s, buf0, r0).wait()
                scat(0, s, buf0, a0).start()
                scat(1, s, buf0, b0).start()
                scat(0, s, buf0, a0).wait()
                scat(1, s, buf0, b0).wait()

                @pl.when(s + 2 < steps)
                def _():
                    read(s + 2, buf0, r0).start()

                read(s + 1, buf1, r1).wait()
                scat(0, s + 1, buf1, a1).start()
                scat(1, s + 1, buf1, b1).start()
                scat(0, s + 1, buf1, a1).wait()
                scat(1, s + 1, buf1, b1).wait()

    return k(*srcs, *idxs)


MOE_TM = 512


MOE_TILE_SLOTS = 256


def _experts_kernel(te_ref, nv_ref, xs_ref, wg_hbm, wu_hbm, wd_hbm, ys_ref, wg_v, wu_v, wd_v, sem, *, layer):
    i = pl.program_id(0)
    valid = i < nv_ref[0]
    e = te_ref[i]
    first = jnp.logical_or(i == 0, e != te_ref[jnp.maximum(i - 1, 0)])
    slot = lax.rem(te_ref[MOE_TILE_SLOTS + i], 2)
    nxt = te_ref[2 * MOE_TILE_SLOTS + i]

    def weight_copies(expert, s):
        return (pltpu.make_async_copy(wg_hbm.at[layer, expert], wg_v.at[s], sem.at[s, 0]),
                pltpu.make_async_copy(wu_hbm.at[layer, expert], wu_v.at[s], sem.at[s, 1]),
                pltpu.make_async_copy(wd_hbm.at[layer, expert], wd_v.at[s], sem.at[s, 2]))

    @pl.when(jnp.logical_and(valid, i == 0))
    def _():
        for cp in weight_copies(e, slot):
            cp.start()

    @pl.when(jnp.logical_and(valid, first))
    def _():
        for cp in weight_copies(e, slot):
            cp.wait()

        @pl.when(nxt >= 0)
        def _():
            for cp in weight_copies(nxt, 1 - slot):
                cp.start()

    @pl.when(valid)
    def _():
        h = _unpack_bf16_pairs(xs_ref[...]).astype(BF16)
        half = D_EXPERT // 2
        y = None
        for j in range(2):
            sl = slice(j * half, (j + 1) * half)
            a = jnp.dot(h, wg_v[slot, :, sl].astype(BF16), preferred_element_type=F32)
            u = jnp.dot(h, wu_v[slot, :, sl].astype(BF16), preferred_element_type=F32)
            t = ((a * _sigmoid(a)) * u).astype(BF16)
            yj = jnp.dot(t, wd_v[slot, sl, :].astype(BF16), preferred_element_type=F32)
            y = yj if y is None else y + yj
        ys_ref[...] = _pack_bf16_pairs(y)


def _experts_call(xs, te, nv, wg, wu, wd, layer):
    p_rows, dh = xs.shape
    d = 2 * dh
    tm = MOE_TM
    nt = p_rows // tm
    assert nt <= MOE_TILE_SLOTS and te.shape == (3 * MOE_TILE_SLOTS,)
    row = lambda i, te_r, nv_r: (jnp.minimum(i, nv_r[0] - 1), 0)
    grid_spec = pltpu.PrefetchScalarGridSpec(
        num_scalar_prefetch=2,
        grid=(nt,),
        in_specs=[
            pl.BlockSpec((tm, dh), row),
            pl.BlockSpec(memory_space=pl.ANY),
            pl.BlockSpec(memory_space=pl.ANY),
            pl.BlockSpec(memory_space=pl.ANY),
        ],
        out_specs=pl.BlockSpec((tm, dh), row),
        scratch_shapes=[
            pltpu.VMEM((2, d, D_EXPERT), F32),
            pltpu.VMEM((2, d, D_EXPERT), F32),
            pltpu.VMEM((2, D_EXPERT, d), F32),
            pltpu.SemaphoreType.DMA((2, 3)),
        ],
    )
    return pl.pallas_call(
        functools.partial(_experts_kernel, layer=layer),
        grid_spec=grid_spec,
        out_shape=jax.ShapeDtypeStruct((p_rows, dh), jnp.int32),
        compiler_params=_cparams(("arbitrary",)),
        name="moe_experts",
    )(te, nv, xs, wg, wu, wd)


COMBINE_TM = 1024


def _combine_kernel(y_ref, wt_ref, x_ref, mod_ref, fn_ref, o_ref, *, final):
    xo = x_ref[...] + mod_ref[5:6, :] * _moe_mix(y_ref, wt_ref[0], wt_ref[1])
    if final:
        ms = jnp.mean(xo * xo, axis=-1, keepdims=True)
        xo = xo * lax.rsqrt(ms + EPS) * fn_ref[...]
    o_ref[...] = xo


def _combine_call(y2, wts3, row0, x, mod_l, mod_row, fn, final):
    b, n, d = x.shape
    n_tok = b * n
    tm = COMBINE_TM
    assert n_tok % tm == 0 and row0 % 8 == 0 and (mod_row is not None or n % tm == 0)
    if mod_row is None:
        mod_map = lambda i: ((i * tm) // n, 0, 0)
    else:
        mod_map = lambda i: (mod_row, 0, 0)
    out = pl.pallas_call(
        functools.partial(_combine_kernel, final=final),
        grid=(n_tok // tm,),
        in_specs=[
            pl.BlockSpec((2, tm, d // 2), lambda i: (0, i, 0)),
            pl.BlockSpec((2, tm // 128, 128), lambda i: (0, row0 // 8 + i, 0)),
            pl.BlockSpec((tm, d), lambda i: (i, 0)),
            pl.BlockSpec((None, 6, d), mod_map),
            pl.BlockSpec((1, d), lambda i: (0, 0)),
        ],
        out_specs=pl.BlockSpec((tm, d), lambda i: (i, 0)),
        out_shape=jax.ShapeDtypeStruct((n_tok, d), F32),
        compiler_params=_cparams(("parallel",)),
        name="moe_combine",
    )(y2, wts3, x.reshape(n_tok, d), mod_l, fn)
    return out.reshape(b, n, d)


def _moe_sparse(h_list, lg_list, x_list, mod_l, mod_rows, b_router, wg, wu, wd, layer, fn, final):
    d = h_list[0].shape[-1]
    sizes = [h.shape[0] * h.shape[1] for h in h_list]
    n_tok = sum(sizes)
    lgt = lg_list[0] if len(lg_list) == 1 else jnp.concatenate(lg_list, axis=1)
    pos, wts, te, nv = _route_call(lgt, b_router, MOE_TM)
    p_rows = 2 * n_tok + N_EXPERTS * MOE_TM
    offs = np.cumsum([0] + sizes)
    poss = [pos[:, offs[i]:offs[i + 1]] for i in range(len(sizes))]
    xs = _sc_dispatch([h.reshape(-1, d) for h in h_list], poss, p_rows)
    ys = _experts_call(xs, te, nv, wg, wu, wd, layer)
    outs = []
    for i, x in enumerate(x_list):
        y2 = _sc_gather_rows(ys, poss[i].reshape(-1)).reshape(2, sizes[i], d)
        row0 = int(offs[i]) // 128
        if final:
            outs.append(_combine_call(y2, wts, row0, x, mod_l, mod_rows[i], fn, True))
        else:
            outs.append((y2, wts, row0, mod_l))
    return outs


def _winprep_kernel(wt_ref, o_ref):
    gd0 = C_AQ
    gdw = 2 * GLA_GATE_RANK
    tail = W_IN_REF_COLS - gd0 - gdw
    o_ref[:, 0:gd0] = wt_ref[0:gd0, :].T.astype(BF16)
    o_ref[:, gd0:gd0 + tail] = wt_ref[gd0 + gdw:W_IN_REF_COLS, :].T.astype(BF16)
    gd = wt_ref[gd0:gd0 + 128, :].T
    lane = lax.broadcasted_iota(jnp.int32, gd.shape, 1)
    o_ref[:, C_GD:] = jnp.where(lane < gdw, gd, 0.0).astype(BF16)


def _winprep_call(w_in):
    depth, d, cols = w_in.shape
    assert cols == W_IN_REF_COLS and C_GD == cols - 2 * GLA_GATE_RANK and W_IN_COLS - C_GD == 128
    return pl.pallas_call(
        _winprep_kernel,
        grid=(depth,),
        in_specs=[pl.BlockSpec((None, cols, d), lambda l: (l, 0, 0))],
        out_specs=pl.BlockSpec((None, d, W_IN_COLS), lambda l: (l, 0, 0)),
        out_shape=jax.ShapeDtypeStruct((depth, d, W_IN_COLS), BF16),
        compiler_params=_cparams(("parallel",)),
        name="w_in_prep",
    )(jnp.swapaxes(w_in, 1, 2))


def _gate_up_weights(w_up, b_up):
    z = jnp.zeros((GLA_GATE_RANK, GLA_QK), w_up.dtype)
    top = jnp.concatenate([w_up[0], z], axis=1)
    mid = jnp.concatenate([z, w_up[1]], axis=1)
    pad = jnp.zeros((128 - 2 * GLA_GATE_RANK, 2 * GLA_QK), w_up.dtype)
    return jnp.concatenate([top, mid, pad], axis=0).astype(BF16), b_up.reshape(1, 2 * GLA_QK)


def kernel(x, c, ctx, c_ctx, w_ada, b_ada, norm_mix, norm_ffn, w_in, w_gla_gate_up, b_gla_gate, gla_norm, q_norm,
           k_norm, w_out, w_router, b_router, w_exp_gate, w_exp_up, w_exp_down, final_norm):
    b, n, d = x.shape
    m = ctx.shape[1]
    depth = w_ada.shape[0]
    assert d == D_MODEL and n % GLA_PAIR == 0 and m % GLA_PAIR == 0 and n % GRID_W == 0

    rows = ((b + 1 + 7) // 8) * 8
    cv = jnp.concatenate([c, c_ctx[None, :], jnp.zeros((rows - b - 1, d), F32)], axis=0)
    mod = _ada_call(cv, w_ada, b_ada).reshape(depth, rows, 6, d)

    cs = jnp.asarray(_channel_dft_table()).astype(BF16)
    tab_x = jnp.asarray(_seq_dft_table(n)).astype(BF16)
    tab_c = jnp.asarray(_seq_dft_table(m)).astype(BF16)
    rope_tabs = tuple(jnp.asarray(t) for t in _rope_tables(n))
    bd512 = jnp.asarray(_blockdiag_ones(ATT_WIDTH, ATT_HDIM)).astype(BF16)
    bd256 = jnp.asarray(_blockdiag_ones(GLA_WIDTH, GLA_DV)).astype(BF16)
    wr_t = w_router.T
    wr_hi = wr_t.astype(BF16)
    wrh = jnp.concatenate([wr_hi, (wr_t - wr_hi.astype(F32)).astype(BF16)], axis=0)
    fn = final_norm.reshape(1, d)

    w_in_perm = _winprep_call(w_in)
    w_out_bf = w_out.astype(BF16)

    xc = ctx
    pend_x = pend_c = None
    for l in range(depth):
        ctx_out = l < depth - 1
        mod_l = mod[l]
        w_perm = (w_in_perm, l)
        wup, bup = _gate_up_weights(w_gla_gate_up[l], b_gla_gate[l])
        nw = norm_mix[l].reshape(1, d)
        nf = norm_ffn[l].reshape(1, d)
        qn = jnp.tile(q_norm[l], ATT_HEADS).reshape(1, ATT_WIDTH)
        kn = jnp.tile(k_norm[l], ATT_KV_HEADS).reshape(1, 128)
        gn = jnp.tile(gla_norm[l], GLA_HEADS).reshape(1, GLA_WIDTH)
        wo = (w_out_bf, l)

        pc = _inproj_call(xc, mod_l, b, nw, w_perm, cs, wup, bup, qn, kn, bd512, None, pend_c, full=ctx_out)
        px = _inproj_call(x, mod_l, None, nw, w_perm, cs, wup, bup, qn, kn, bd512, rope_tabs, pend_x)
        if pend_x is not None:
            x, xc = px["x"], pc["x"]

        s_zero = jnp.zeros((b, 2, GLA_WIDTH, GLA_QK), F32)
        of_c, ob_c, s_fin = _gla_call(pc["gqk"], pc["gv"], pc["gvt"], pc["la"], s_zero)
        of_x, ob_x, _ = _gla_call(px["gqk"], px["gv"], px["gvt"], px["la"], s_fin)

        kv_c = (pc["kt"], pc["vd"])
        f_x = _seqdft_call(tab_x, px["ab"].reshape(2 * n, b * FNET_WIDTH), b)
        x, h2_x, lg_x = _attn_outproj_call(px["q"], [kv_c, (px["kt"], px["vd"])], f_x, of_x, ob_x, px["gg"], x,
                                           mod_l, None, wo, gn, bd256, nf, wrh)

        if ctx_out:
            f_c = _seqdft_call(tab_c, pc["ab"].reshape(2 * m, b * FNET_WIDTH), b)
            xc, h2_c, lg_c = _attn_outproj_call(pc["q"], [kv_c], f_c, of_c, ob_c, pc["gg"], xc,
                                                mod_l, b, wo, gn, bd256, nf, wrh)

        final = l == depth - 1
        wexp = (w_exp_gate, w_exp_up, w_exp_down, l)
        if ctx_out:
            res = _moe_sparse([h2_x, h2_c], [lg_x, lg_c], [x, xc], mod_l, [None, b], b_router, *wexp, fn, final)
        else:
            res = _moe_sparse([h2_x], [lg_x], [x], mod_l, [None], b_router, *wexp, fn, final)
        if final:
            x = res[0]
        else:
            pend_x, pend_c = res
    return x
```

```python
import functools

import numpy as np
import jax
import jax.numpy as jnp
from jax import lax
from jax.experimental import pallas as pl
from jax.experimental.pallas import tpu as pltpu
from jax.experimental.pallas import tpu_sc as plsc

F32 = jnp.float32
BF16 = jnp.bfloat16

D_MODEL = 1024
GRID_W = 64
EPS = 1e-6
LOG2E = 1.4426950408889634

FNET_WIDTH = 256
FNET_GROUPS = 4
FNET_GDIM = 64

GLA_HEADS = 4
GLA_DV = 64
GLA_DK = 32
GLA_WIDTH = 256
GLA_QK = 128
GLA_GATE_RANK = 16
GLA_GATE_NORM = 16.0
GLA_CHUNK = 64
GLA_PAIR = 2 * GLA_CHUNK

ATT_HEADS = 8
ATT_KV_HEADS = 2
ATT_HDIM = 64
ATT_WIDTH = 512
ROPE_FREQS = 16
ROPE_THETA = 10000.0

N_EXPERTS = 16
N_GROUPS = 4
EXPERTS_PER_GROUP = 4
D_EXPERT = 512

C_U = 0
C_GQ = 256
C_GK = 384
C_GV = 512
C_GG = 768
C_AQ = 1024
C_AK = 1536
C_GD = 1792
W_IN_COLS = 1920
W_IN_REF_COLS = 1824

VMEM_LIMIT = 56 * 1024 * 1024


def _cparams(sem):
    return pltpu.CompilerParams(dimension_semantics=sem, vmem_limit_bytes=VMEM_LIMIT)


def _sigmoid(x):
    return 1.0 / (1.0 + jnp.exp(-x))


def _pack_bf16_pairs(x):
    blocks = []
    for t in range(x.shape[1] // 256):
        lo = lax.bitcast_convert_type(x[:, 256 * t:256 * t + 128].astype(BF16).astype(F32), jnp.uint32)
        hi = lax.bitcast_convert_type(x[:, 256 * t + 128:256 * t + 256].astype(BF16).astype(F32), jnp.uint32)
        blocks.append((lo >> 16) | (hi & jnp.uint32(0xFFFF0000)))
    return lax.bitcast_convert_type(jnp.concatenate(blocks, axis=1), jnp.int32)


def _unpack_bf16_pairs(p):
    u = lax.bitcast_convert_type(p, jnp.uint32)
    blocks = []
    for t in range(p.shape[1] // 128):
        word = u[:, 128 * t:128 * (t + 1)]
        blocks += [lax.bitcast_convert_type(word << 16, F32),
                   lax.bitcast_convert_type(word & jnp.uint32(0xFFFF0000), F32)]
    return jnp.concatenate(blocks, axis=1)


def _token_columns(w):
    nr = w.shape[0]
    tm = nr * 128
    lane = lax.broadcasted_iota(jnp.int32, (tm, 128), 1)
    row = lax.broadcasted_iota(jnp.int32, (tm, 128), 0)
    wb = jnp.concatenate([jnp.broadcast_to(w[r:r + 1, :], (128, 128)) for r in range(nr)], axis=0)
    return jnp.sum(jnp.where(lane == (row % 128), wb, 0.0), axis=1, keepdims=True)


def _moe_mix(y_ref, w0, w1):
    return (_token_columns(w0) * _unpack_bf16_pairs(y_ref[0])
            + _token_columns(w1) * _unpack_bf16_pairs(y_ref[1]))


def _nt_dot(a, b):
    return lax.dot_general(a, b, (((1,), (1,)), ((), ())), preferred_element_type=F32)


@functools.lru_cache(maxsize=None)
def _channel_dft_table():
    j = np.arange(FNET_GDIM)
    ang = 2.0 * np.pi * ((j[:, None] * j[None, :]) % FNET_GDIM) / FNET_GDIM
    c = np.cos(ang) / np.sqrt(FNET_GDIM)
    s = np.sin(ang) / np.sqrt(FNET_GDIM)
    out = np.zeros((FNET_WIDTH, 2 * FNET_WIDTH), np.float64)
    for g in range(FNET_GROUPS):
        sl = slice(g * FNET_GDIM, (g + 1) * FNET_GDIM)
        out[sl, sl] = c
        out[sl, FNET_WIDTH + g * FNET_GDIM:FNET_WIDTH + (g + 1) * FNET_GDIM] = s
    return out.astype(np.float32)


@functools.lru_cache(maxsize=None)
def _seq_dft_table(n):
    j = np.arange(n, dtype=np.int64)
    ang = 2.0 * np.pi * ((j[:, None] * j[None, :]) % n) / n
    return np.concatenate([np.cos(ang), -np.sin(ang)], axis=1).astype(np.float32) / np.float32(np.sqrt(n))


@functools.lru_cache(maxsize=None)
def _rope_tables(n):
    rows = n // GRID_W
    row = np.repeat(np.arange(rows), GRID_W).astype(np.float64)
    col = np.tile(np.arange(GRID_W), rows).astype(np.float64)
    inv = ROPE_THETA ** (-np.arange(ROPE_FREQS, dtype=np.float64) * 2.0 / (2 * ROPE_FREQS))
    ar = row[:, None] * inv[None, :]
    ac = col[:, None] * inv[None, :]
    cos = np.concatenate([np.cos(ar), np.cos(ar), np.cos(ac), np.cos(ac)], axis=1)
    sin = np.concatenate([-np.sin(ar), np.sin(ar), -np.sin(ac), np.sin(ac)], axis=1)
    return (np.tile(cos, (1, 2)).astype(np.float32), np.tile(sin, (1, 2)).astype(np.float32))


@functools.lru_cache(maxsize=None)
def _blockdiag_ones(width, blk):
    i = np.arange(width)
    return (i[:, None] // blk == i[None, :] // blk).astype(np.float32)


def _ada_kernel(cv_ref, w_ref, b_ref, o_ref):
    cv = cv_ref[...]
    a = (cv * _sigmoid(cv)).astype(BF16)
    o_ref[...] = jnp.dot(a, w_ref[...].astype(BF16), preferred_element_type=F32) + b_ref[...]


def _ada_call(cv, w_ada, b_ada):
    depth, d, d6 = w_ada.shape
    tn = 1536
    rows = cv.shape[0]
    return pl.pallas_call(
        _ada_kernel,
        grid=(depth, d6 // tn),
        in_specs=[
            pl.BlockSpec((rows, d), lambda l, j: (0, 0)),
            pl.BlockSpec((None, d, tn), lambda l, j: (l, 0, j)),
            pl.BlockSpec((None, 1, tn), lambda l, j: (l, 0, j)),
        ],
        out_specs=pl.BlockSpec((None, rows, tn), lambda l, j: (l, 0, j)),
        out_shape=jax.ShapeDtypeStruct((depth, rows, d6), F32),
        compiler_params=_cparams(("parallel", "parallel")),
        name="ada_mod",
    )(cv, w_ada, b_ada.reshape(depth, 1, d6))


def _swap16(x):
    lane = lax.broadcasted_iota(jnp.int32, x.shape, 1)
    first = (lane % 32) < 16
    return jnp.where(first, pltpu.roll(x, 112, 1), pltpu.roll(x, 16, 1))


def _head_rms(x, bd, w):
    ms = jnp.dot((x * x).astype(BF16), bd, preferred_element_type=F32) * (1.0 / ATT_HDIM)
    return x * lax.rsqrt(ms + EPS) * w


def _inproj_kernel(*refs, rope, pending_rows, names):
    refs = list(refs)
    x_ref, mod_ref, nw_ref, w_ref, cs_ref, wup_ref, bup_ref, qn_ref, kn_ref, bd_ref = refs[:10]
    del refs[:10]
    if rope:
        cos_ref, sin_ref = refs[:2]
        del refs[:2]
    if pending_rows is not None:
        y_ref, wt_ref, modp_ref = refs[:3]
        del refs[:3]
    out = dict(zip(names, refs))
    full = "q" in out
    x = x_ref[...]
    if pending_rows is not None:
        xnew_ref = out["x"]
        row0, rows_per_sample = pending_rows
        nr = x.shape[0] // 128
        r = row0 + pl.program_id(0) * rows_per_sample + pl.program_id(1) * nr
        sub = lax.rem(r, 8)
        w = [wt_ref[kk, 0:nr, :] for kk in range(2)]
        for blk in range(1, 8 // nr):
            w = [jnp.where(sub == blk * nr, wt_ref[kk, blk * nr:(blk + 1) * nr, :], w[kk]) for kk in range(2)]
        x = x + modp_ref[5:6, :] * _moe_mix(y_ref, w[0], w[1])
        xnew_ref[...] = x
    ms = jnp.mean(x * x, axis=-1, keepdims=True)
    y = x * lax.rsqrt(ms + EPS) * nw_ref[...]
    h = y * (1.0 + mod_ref[1:2, :]) + mod_ref[0:1, :]
    hb = h.astype(BF16)

    def proj(c0, width):
        return jnp.dot(hb, w_ref[:, c0:c0 + width], preferred_element_type=F32)

    if full:
        uab = jnp.dot(proj(C_U, FNET_WIDTH).astype(BF16), cs_ref[...], preferred_element_type=F32)
        out["ab"][0] = uab[:, :FNET_WIDTH].astype(BF16)
        out["ab"][1] = uab[:, FNET_WIDTH:].astype(BF16)

    out["gqk"][...] = proj(C_GQ, 2 * GLA_QK)
    gv = proj(C_GV, GLA_WIDTH)
    out["gv"][...] = gv.astype(BF16)
    out["gvt"][...] = gv.T.astype(BF16)
    if full:
        out["gg"][...] = proj(C_GG, GLA_WIDTH).astype(BF16)
    pre = jnp.dot(proj(C_GD, 128).astype(BF16), wup_ref[...], preferred_element_type=F32) + bup_ref[...]
    out["la"][...] = (jnp.minimum(pre, 0.0) - jnp.log1p(jnp.exp(-jnp.abs(pre)))) * (1.0 / GLA_GATE_NORM)

    bd = bd_ref[...]
    kv = proj(C_AK, 256)
    k = _head_rms(kv[:, :128], bd[:128, :128], kn_ref[...])
    if rope:
        cos = cos_ref[...]
        sin = sin_ref[...]
        k = k * cos + _swap16(k) * sin
    if full:
        q = _head_rms(proj(C_AQ, ATT_WIDTH), bd, qn_ref[...])
        if rope:
            q = jnp.concatenate(
                [q[:, s:s + 128] * cos + _swap16(q[:, s:s + 128]) * sin for s in range(0, ATT_WIDTH, 128)], axis=1)
        out["q"][...] = (q * (ATT_HDIM ** -0.5 * LOG2E)).astype(BF16)
    kt_ref, vd_ref = out["kt"], out["vd"]
    v = kv[:, 128:]
    lo = lax.broadcasted_iota(jnp.int32, k.shape, 1) < ATT_HDIM
    k_sw = pltpu.roll(k, ATT_HDIM, 1)
    v_sw = pltpu.roll(v, ATT_HDIM, 1)
    kt_ref[0] = jnp.where(lo, k, k_sw).T.astype(BF16)
    kt_ref[1] = jnp.where(lo, k_sw, k).T.astype(BF16)
    vd_ref[0] = jnp.where(lo, v, 1.0).astype(BF16)
    vd_ref[1] = jnp.where(lo, 1.0, v_sw).astype(BF16)
    vd_ref[2] = jnp.where(lo, v_sw, 1.0).astype(BF16)
    vd_ref[3] = jnp.where(lo, 1.0, v).astype(BF16)


def _inproj_call(x, mod_l, mod_row, nw, w_perm, cs, wup, bup, qn, kn, bd, rope_tabs, pending=None, full=True):
    b, n, d = x.shape
    tm = min(1024, n)
    nt = n // tm
    rope = rope_tabs is not None
    if mod_row is None:
        mod_map = lambda bi, i: (bi, 0, 0)
    else:
        mod_map = lambda bi, i: (mod_row, 0, 0)
    const = lambda bi, i: (0, 0)
    in_specs = [
        pl.BlockSpec((None, tm, d), lambda bi, i: (bi, i, 0)),
        pl.BlockSpec((None, 6, d), mod_map),
        pl.BlockSpec((1, d), const),
        pl.BlockSpec((None, d, W_IN_COLS), lambda bi, i: (w_perm[1], 0, 0)),
        pl.BlockSpec((FNET_WIDTH, 2 * FNET_WIDTH), const),
        pl.BlockSpec((128, 2 * GLA_QK), const),
        pl.BlockSpec((1, 2 * GLA_QK), const),
        pl.BlockSpec((1, ATT_WIDTH), const),
        pl.BlockSpec((1, 128), const),
        pl.BlockSpec((ATT_WIDTH, ATT_WIDTH), const),
    ]
    args = [x, mod_l, nw, w_perm[0], cs, wup, bup, qn, kn, bd]
    if rope:
        in_specs += [pl.BlockSpec((tm, 128), lambda bi, i: (i, 0)), pl.BlockSpec((tm, 128), lambda bi, i: (i, 0))]
        args += list(rope_tabs)
    pending_rows = None
    if pending is not None:
        y2, wts3, row0, mod_prev = pending
        nr = tm // 128
        rps = n // 128
        assert 8 % nr == 0 and row0 % nr == 0 and rps % nr == 0
        pending_rows = (row0, rps)
        in_specs += [
            pl.BlockSpec((2, None, tm, d // 2), lambda bi, i: (0, bi, i, 0)),
            pl.BlockSpec((2, 8, 128), lambda bi, i: (0, (row0 + bi * rps + i * nr) // 8, 0)),
            pl.BlockSpec((None, 6, d), mod_map),
        ]
        args += [y2.reshape(2, b, n, d // 2), wts3, mod_prev]
    tok = lambda w: pl.BlockSpec((None, tm, w), lambda bi, i: (bi, i, 0))
    outs = []
    if full:
        outs.append(("ab", jax.ShapeDtypeStruct((2, n, b * FNET_WIDTH), BF16),
                     pl.BlockSpec((2, tm, FNET_WIDTH), lambda bi, i: (0, i, bi))))
    outs += [
        ("gqk", jax.ShapeDtypeStruct((b, n, 2 * GLA_QK), F32), tok(2 * GLA_QK)),
        ("gv", jax.ShapeDtypeStruct((b, n, GLA_WIDTH), BF16), tok(GLA_WIDTH)),
        ("gvt", jax.ShapeDtypeStruct((b, GLA_WIDTH, n), BF16),
         pl.BlockSpec((None, GLA_WIDTH, tm), lambda bi, i: (bi, 0, i))),
    ]
    if full:
        outs.append(("gg", jax.ShapeDtypeStruct((b, n, GLA_WIDTH), BF16), tok(GLA_WIDTH)))
    outs.append(("la", jax.ShapeDtypeStruct((b, n, 2 * GLA_QK), F32), tok(2 * GLA_QK)))
    if full:
        outs.append(("q", jax.ShapeDtypeStruct((b, n, ATT_WIDTH), BF16), tok(ATT_WIDTH)))
    outs += [
        ("kt", jax.ShapeDtypeStruct((b, ATT_KV_HEADS, 128, n), BF16),
         pl.BlockSpec((None, ATT_KV_HEADS, 128, tm), lambda bi, i: (bi, 0, 0, i))),
        ("vd", jax.ShapeDtypeStruct((b, 2 * ATT_KV_HEADS, n, 128), BF16),
         pl.BlockSpec((None, 2 * ATT_KV_HEADS, tm, 128), lambda bi, i: (bi, 0, i, 0))),
    ]
    if pending is not None:
        outs.append(("x", jax.ShapeDtypeStruct((b, n, d), F32), tok(d)))
    names = tuple(o[0] for o in outs)
    res = pl.pallas_call(
        functools.partial(_inproj_kernel, rope=rope, pending_rows=pending_rows, names=names),
        grid=(b, nt),
        in_specs=in_specs,
        out_specs=tuple(o[2] for o in outs),
        out_shape=tuple(o[1] for o in outs),
        compiler_params=_cparams(("parallel", "parallel")),
        name="inproj_rope" if rope else "inproj_ctx",
    )(*args)
    return dict(zip(names, res))


def _seqdft_kernel(t_ref, ab_ref, o_ref):
    y = jnp.dot(t_ref[...], ab_ref[...], preferred_element_type=F32)
    for bb in range(o_ref.shape[0]):
        o_ref[bb] = y[:, bb * FNET_WIDTH:(bb + 1) * FNET_WIDTH].astype(BF16)


def _seqdft_call(table, ab, b):
    n = table.shape[0]
    tm = min(512, n)
    nb = 4 if b % 4 == 0 else (2 if b % 2 == 0 else 1)
    return pl.pallas_call(
        _seqdft_kernel,
        grid=(b // nb, n // tm),
        in_specs=[
            pl.BlockSpec((tm, 2 * n), lambda c, i: (i, 0)),
            pl.BlockSpec((2 * n, nb * FNET_WIDTH), lambda c, i: (0, c)),
        ],
        out_specs=pl.BlockSpec((nb, tm, FNET_WIDTH), lambda c, i: (c, i, 0)),
        out_shape=jax.ShapeDtypeStruct((b, n, FNET_WIDTH), BF16),
        compiler_params=_cparams(("parallel", "parallel")),
        name="seq_dft",
    )(table, ab)


def _gla_dir(qk, v, vt, a, s_in, fwd):
    p = GLA_PAIR
    r = lax.broadcasted_iota(jnp.int32, (p, p), 0)
    c = lax.broadcasted_iota(jnp.int32, (p, p), 1)
    same = (r // GLA_CHUNK) == (c // GLA_CHUNK)
    tri = same & ((c <= r) if fwd else (c >= r))
    row_lo = r < GLA_CHUNK
    rin = r % GLA_CHUNK

    q = qk[:, :GLA_QK] * (GLA_DK ** -0.5)
    k = qk[:, GLA_QK:]
    cum = a
    sh = 1
    while sh < GLA_CHUNK:
        if fwd:
            cum = cum + jnp.where(rin >= sh, pltpu.roll(cum, sh, 0), 0.0)
        else:
            cum = cum + jnp.where(rin < GLA_CHUNK - sh, pltpu.roll(cum, p - sh, 0), 0.0)
        sh *= 2
    if fwd:
        last0, last1 = cum[GLA_CHUNK - 1:GLA_CHUNK, :], cum[p - 1:p, :]
    else:
        last0, last1 = cum[0:1, :], cum[GLA_CHUNK:GLA_CHUNK + 1, :]
    lastb = jnp.where(row_lo, last0, last1)
    qt = q * jnp.exp(cum)
    kt = k * jnp.exp(-cum)
    kd = k * jnp.exp(lastb - cum)

    kt_b = kt.astype(BF16)
    zk = jnp.zeros_like(kt_b)
    ks = jnp.concatenate([jnp.where((c // GLA_DK) == hh, kt_b, zk) for hh in range(GLA_HEADS)], axis=0)
    att = _nt_dot(qt.astype(BF16), ks)
    tri4 = jnp.concatenate([tri] * GLA_HEADS, axis=1)
    att = jnp.where(tri4, att, 0.0).astype(BF16)
    col = lax.broadcasted_iota(jnp.int32, (p, GLA_WIDTH), 1)
    zv = jnp.zeros_like(v)
    vs = jnp.concatenate([jnp.where((col // GLA_DV) == hh, v, zv) for hh in range(GLA_HEADS)], axis=0)
    o_intra = jnp.dot(att, vs, preferred_element_type=F32)

    sr = lax.broadcasted_iota(jnp.int32, (GLA_WIDTH, GLA_QK), 0)
    sc = lax.broadcasted_iota(jnp.int32, (GLA_WIDTH, GLA_QK), 1)
    bdm = (sr // GLA_DV) == (sc // GLA_DK)
    first, second = (0, 1) if fwd else (1, 0)
    lasts = (last0, last1)
    in_chunk = (row_lo, jnp.logical_not(row_lo))
    kd2 = jnp.concatenate([jnp.where(in_chunk[0], kd, 0.0), jnp.where(in_chunk[1], kd, 0.0)], axis=1).astype(BF16)
    kvt2 = jnp.dot(vt, kd2, preferred_element_type=F32)
    kvt = (kvt2[:, :GLA_QK], kvt2[:, GLA_QK:])
    s_a = s_in
    s_b = s_a * jnp.exp(lasts[first]) + jnp.where(bdm, kvt[first], 0.0)
    s_c = s_b * jnp.exp(lasts[second]) + jnp.where(bdm, kvt[second], 0.0)
    q2 = jnp.concatenate([jnp.where(in_chunk[first], qt, 0.0), jnp.where(in_chunk[second], qt, 0.0)], axis=1)
    s2 = jnp.concatenate([s_a, s_b], axis=1).astype(BF16)
    o_inter = _nt_dot(q2.astype(BF16), s2)
    return o_intra + o_inter, s_c


def _gla_kernel(qkf, vf, vtf, laf, qkb, vb, vtb, lab, s0_ref, of_ref, ob_ref, sfin_ref, s_scr):
    i = pl.program_id(1)

    @pl.when(i == 0)
    def _():
        s_scr[...] = s0_ref[...]

    pairs = qkf.shape[1] // GLA_PAIR
    for gi in range(qkf.shape[0]):
        sf = s_scr[gi, 0]
        sb = s_scr[gi, 1]
        for pi in range(pairs):
            rf = slice(pi * GLA_PAIR, (pi + 1) * GLA_PAIR)
            rb = slice((pairs - 1 - pi) * GLA_PAIR, (pairs - pi) * GLA_PAIR)
            o1, sf = _gla_dir(qkf[gi, rf, :], vf[gi, rf, :], vtf[gi, :, rf], laf[gi, rf, :], sf, True)
            o2, sb = _gla_dir(qkb[gi, rb, :], vb[gi, rb, :], vtb[gi, :, rb], lab[gi, rb, :], sb, False)
            of_ref[gi, rf, :] = o1.astype(of_ref.dtype)
            ob_ref[gi, rb, :] = o2.astype(ob_ref.dtype)
        s_scr[gi, 0] = sf
        s_scr[gi, 1] = sb

    @pl.when(i == pl.num_programs(1) - 1)
    def _():
        sfin_ref[...] = s_scr[...]


def _gla_call(gqk, gv, gvt, la, s0):
    b, n, _ = gqk.shape
    p = GLA_PAIR * (2 if n % (2 * GLA_PAIR) == 0 else 1)
    npair = n // p
    gb = 4 if b % 4 == 0 else (2 if b % 2 == 0 else 1)
    fw = lambda bi, i: (bi, i, 0)
    bw = lambda bi, i: (bi, npair - 1 - i, 0)
    in_specs = [
        pl.BlockSpec((gb, p, 2 * GLA_QK), fw),
        pl.BlockSpec((gb, p, GLA_WIDTH), fw),
        pl.BlockSpec((gb, GLA_WIDTH, p), lambda bi, i: (bi, 0, i)),
        pl.BlockSpec((gb, p, GLA_QK), fw),
        pl.BlockSpec((gb, p, 2 * GLA_QK), bw),
        pl.BlockSpec((gb, p, GLA_WIDTH), bw),
        pl.BlockSpec((gb, GLA_WIDTH, p), lambda bi, i: (bi, 0, npair - 1 - i)),
        pl.BlockSpec((gb, p, GLA_QK), lambda bi, i: (bi, npair - 1 - i, 1)),
        pl.BlockSpec((gb, 2, GLA_WIDTH, GLA_QK), lambda bi, i: (bi, 0, 0, 0)),
    ]
    out_specs = (
        pl.BlockSpec((gb, p, GLA_WIDTH), fw),
        pl.BlockSpec((gb, p, GLA_WIDTH), bw),
        pl.BlockSpec((gb, 2, GLA_WIDTH, GLA_QK), lambda bi, i: (bi, 0, 0, 0)),
    )
    out_shape = (
        jax.ShapeDtypeStruct((b, n, GLA_WIDTH), BF16),
        jax.ShapeDtypeStruct((b, n, GLA_WIDTH), BF16),
        jax.ShapeDtypeStruct((b, 2, GLA_WIDTH, GLA_QK), F32),
    )
    return pl.pallas_call(
        _gla_kernel,
        grid=(b // gb, npair),
        in_specs=in_specs,
        out_specs=out_specs,
        out_shape=out_shape,
        scratch_shapes=[pltpu.VMEM((gb, 2, GLA_WIDTH, GLA_QK), F32)],
        compiler_params=_cparams(("parallel", "arbitrary")),
        name="gla_scan",
    )(gqk, gv, gvt, la, gqk, gv, gvt, la, s0)


def _attn_heads(q_ref, parts):
    tq = q_ref.shape[0]
    lane = lax.broadcasted_iota(jnp.int32, (tq, 128), 1)
    lo = lane < ATT_HDIM
    blocks = []
    for j in range(ATT_HEADS // 2):
        q128 = q_ref[:, 128 * j:128 * (j + 1)]
        g = (2 * j) // (ATT_HEADS // ATT_KV_HEADS)
        outs = []
        for half in range(2):
            qm = jnp.where(lo if half == 0 else jnp.logical_not(lo), q128, jnp.zeros_like(q128))
            ss = [jnp.dot(qm, kt_ref[g], preferred_element_type=F32) for kt_ref, _ in parts]
            m = functools.reduce(jnp.maximum, [jnp.max(s, axis=-1, keepdims=True) for s in ss])
            ps = [jnp.exp2(s - m).astype(BF16) for s in ss]
            pv = functools.reduce(
                lambda u, w: u + w,
                [jnp.dot(pp, vd_ref[2 * g + half], preferred_element_type=F32) for pp, (_, vd_ref) in zip(ps, parts)])
            den = pv[:, ATT_HDIM:ATT_HDIM + 1] if half == 0 else pv[:, 0:1]
            outs.append(pv / den)
        blocks.append(jnp.where(lo, outs[0], outs[1]).astype(BF16))
    return jnp.concatenate(blocks, axis=1)


def _attn_outproj_kernel(*refs, nparts):
    q_ref = refs[0]
    parts = [(refs[1 + 2 * i], refs[2 + 2 * i]) for i in range(nparts)]
    (f_ref, of_ref, ob_ref, gg_ref, x_ref, mod_ref, w_ref, gn_ref, bd_ref, nf_ref, wr_ref,
     xn_ref, h2_ref, lg_ref) = refs[1 + 2 * nparts:]
    att = _attn_heads(q_ref, parts)
    o = of_ref[...].astype(F32) + ob_ref[...].astype(F32)
    ms = jnp.dot((o * o).astype(BF16), bd_ref[...], preferred_element_type=F32) * (1.0 / GLA_DV)
    on = o * lax.rsqrt(ms + EPS) * gn_ref[...]
    g = gg_ref[...].astype(F32)
    gl = (on * (g * _sigmoid(g))).astype(BF16)
    ox = (jnp.dot(f_ref[...], w_ref[0:FNET_WIDTH, :], preferred_element_type=F32)
          + jnp.dot(gl, w_ref[FNET_WIDTH:FNET_WIDTH + GLA_WIDTH, :], preferred_element_type=F32)
          + jnp.dot(att, w_ref[FNET_WIDTH + GLA_WIDTH:, :], preferred_element_type=F32))
    xn = x_ref[...] + mod_ref[2:3, :] * ox
    xn_ref[...] = xn
    ms2 = jnp.mean(xn * xn, axis=-1, keepdims=True)
    h2 = xn * lax.rsqrt(ms2 + EPS) * nf_ref[...] * (1.0 + mod_ref[4:5, :]) + mod_ref[3:4, :]
    h2_ref[...] = _pack_bf16_pairs(h2)
    lg2 = _nt_dot(wr_ref[...], h2.astype(BF16))
    lg_ref[...] = lg2[:N_EXPERTS, :] + lg2[N_EXPERTS:, :]


def _attn_outproj_call(q, kv_parts, f, of, ob, gg, x, mod_l, mod_row, w_out, gn, bd, nf, wr):
    b, n, d = x.shape
    tm = min(1024, n)
    nt = n // tm
    if mod_row is None:
        mod_map = lambda bi, i: (bi, 0, 0)
    else:
        mod_map = lambda bi, i: (mod_row, 0, 0)
    const = lambda bi, i: (0, 0)
    tok = lambda w: pl.BlockSpec((None, tm, w), lambda bi, i: (bi, i, 0))
    in_specs = [tok(ATT_WIDTH)]
    args = [q]
    for kt, vd in kv_parts:
        m = kt.shape[-1]
        in_specs.append(pl.BlockSpec((None, ATT_KV_HEADS, 128, m), lambda bi, i: (bi, 0, 0, 0)))
        in_specs.append(pl.BlockSpec((None, 2 * ATT_KV_HEADS, m, 128), lambda bi, i: (bi, 0, 0, 0)))
        args += [kt, vd]
    in_specs += [
        tok(FNET_WIDTH), tok(GLA_WIDTH), tok(GLA_WIDTH), tok(GLA_WIDTH), tok(d),
        pl.BlockSpec((None, 6, d), mod_map),
        pl.BlockSpec((None, d, d), lambda bi, i: (w_out[1], 0, 0)),
        pl.BlockSpec((1, GLA_WIDTH), const),
        pl.BlockSpec((GLA_WIDTH, GLA_WIDTH), const),
        pl.BlockSpec((1, d), const),
        pl.BlockSpec((2 * N_EXPERTS, d), const),
    ]
    args += [f, of, ob, gg, x, mod_l, w_out[0], gn, bd, nf, wr]
    out_specs = (
        tok(d), tok(d // 2),
        pl.BlockSpec((N_EXPERTS, tm), lambda bi, i: (0, bi * nt + i)),
    )
    out_shape = (
        jax.ShapeDtypeStruct((b, n, d), F32),
        jax.ShapeDtypeStruct((b, n, d // 2), jnp.int32),
        jax.ShapeDtypeStruct((N_EXPERTS, b * n), F32),
    )
    return pl.pallas_call(
        functools.partial(_attn_outproj_kernel, nparts=len(kv_parts)),
        grid=(b, nt),
        in_specs=in_specs,
        out_specs=out_specs,
        out_shape=out_shape,
        compiler_params=_cparams(("parallel", "parallel")),
        name="attn_outproj",
    )(*args)


def _route_kernel(b_ref, lg_ref, pos_ref, wt_ref, te_ref, nv_ref, *, tm):
    r = lg_ref.shape[1]
    s = [_sigmoid(lg_ref[e]) for e in range(N_EXPERTS)]
    sel = [s[e] + b_ref[e] for e in range(N_EXPERTS)]
    grp = []
    for g in range(N_GROUPS):
        a, b, c, d = sel[4 * g:4 * g + 4]
        hi1, lo1 = jnp.maximum(a, b), jnp.minimum(a, b)
        hi2, lo2 = jnp.maximum(c, d), jnp.minimum(c, d)
        m1 = jnp.maximum(hi1, hi2)
        m2 = jnp.maximum(jnp.minimum(hi1, hi2), jnp.maximum(lo1, lo2))
        grp.append(m1 + m2)
    one = jnp.ones_like(s[0])
    zero = jnp.zeros_like(s[0])
    msk = []
    for g in range(N_GROUPS):
        isg = one
        for g2 in range(N_GROUPS):
            if g2 < g:
                isg = isg * jnp.where(grp[g] > grp[g2], one, zero)
            elif g2 > g:
                isg = isg * jnp.where(grp[g] >= grp[g2], one, zero)
        for li in range(EXPERTS_PER_GROUP):
            e = 4 * g + li
            rank = zero
            for lj in range(EXPERTS_PER_GROUP):
                ej = 4 * g + lj
                if lj < li:
                    rank = rank + jnp.where(sel[ej] >= sel[e], one, zero)
                elif lj > li:
                    rank = rank + jnp.where(sel[ej] > sel[e], one, zero)
            msk.append(jnp.where(rank < 2.0, isg, zero))
    den = functools.reduce(lambda u, v: u + v, [msk[e] * s[e] for e in range(N_EXPERTS)])

    li_ = lax.broadcasted_iota(jnp.int32, (128, 128), 0)
    lj_ = lax.broadcasted_iota(jnp.int32, (128, 128), 1)
    upper = jnp.where(li_ < lj_, 1.0, 0.0).astype(BF16)
    ri_ = lax.broadcasted_iota(jnp.int32, (r, r), 0)
    rj_ = lax.broadcasted_iota(jnp.int32, (r, r), 1)
    lower = jnp.where(rj_ < ri_, 1.0, 0.0).astype(BF16)
    tiles = (1, te_ref.shape[1])
    tile_start = lax.broadcasted_iota(jnp.int32, tiles, 1).astype(F32) * float(tm)
    te = jnp.zeros(tiles, F32)
    seg = jnp.zeros(tiles, F32)
    nonempty = []
    off = jnp.zeros((1, 1), F32)
    seen = zero
    pos = [zero, zero]
    wts = [zero, zero]
    for e in range(N_EXPERTS):
        mb = msk[e].astype(BF16)
        lane_pre = jnp.dot(mb, upper, preferred_element_type=F32)
        row_pre = jnp.sum(jnp.dot(lower, mb, preferred_element_type=F32), axis=1, keepdims=True)
        cnt = jnp.sum(jnp.sum(msk[e], axis=1, keepdims=True), axis=0, keepdims=True)
        p_e = off + row_pre + lane_pre
        g_e = s[e] / den
        for kk in range(2):
            hit = msk[e] * jnp.where(seen == float(kk), one, zero)
            pos[kk] = pos[kk] + hit * p_e
            wts[kk] = wts[kk] + hit * g_e
        seen = seen + msk[e]
        off = off + jnp.floor((cnt + float(tm - 1)) * (1.0 / tm)) * float(tm)
        passed = jnp.where(tile_start >= off, 1.0, 0.0)
        te = te + passed
        nonempty.append(jnp.where(cnt > 0.0, 1.0, 0.0))
        seg = seg + nonempty[e] * passed
    for kk in range(2):
        pos_ref[kk] = pos[kk].astype(jnp.int32)
        wt_ref[kk] = wts[kk]
    te = jnp.minimum(te, float(N_EXPERTS - 1))
    nxt = jnp.full(tiles, -1.0, F32)
    for e in reversed(range(N_EXPERTS)):
        nxt = jnp.where(jnp.logical_and(nonempty[e] > 0.0, te < float(e)), float(e), nxt)
    te_ref[0:1, :] = te.astype(jnp.int32)
    te_ref[1:2, :] = seg.astype(jnp.int32)
    te_ref[2:3, :] = nxt.astype(jnp.int32)
    te_ref[3:, :] = jnp.zeros((te_ref.shape[0] - 3, te_ref.shape[1]), jnp.int32)
    nv_ref[...] = jnp.broadcast_to(off * (1.0 / tm), nv_ref.shape).astype(jnp.int32)


def _route_call(lgt, b_router, tm):
    n_tok = lgt.shape[1]
    r = n_tok // 128
    assert r * 128 == n_tok and r % 8 == 0 and 2 * n_tok // tm + N_EXPERTS <= 256
    lg3 = lgt.reshape(N_EXPERTS, r, 128)
    full3 = lambda k: pl.BlockSpec((k, r, 128), lambda: (0, 0, 0))
    pos, wts, te, nv = pl.pallas_call(
        functools.partial(_route_kernel, tm=tm),
        in_specs=[pl.BlockSpec(memory_space=pltpu.SMEM), full3(N_EXPERTS)],
        out_specs=(full3(2), full3(2), pl.BlockSpec((8, 256), lambda: (0, 0)), pl.BlockSpec((1, 128), lambda: (0, 0))),
        out_shape=(
            jax.ShapeDtypeStruct((2, r, 128), jnp.int32),
            jax.ShapeDtypeStruct((2, r, 128), F32),
            jax.ShapeDtypeStruct((8, 256), jnp.int32),
            jax.ShapeDtypeStruct((1, 128), jnp.int32),
        ),
        compiler_params=pltpu.CompilerParams(vmem_limit_bytes=VMEM_LIMIT),
        name="route",
    )(b_router, lg3)
    return pos.reshape(2, n_tok), wts, te[:3].reshape(3 * 256), nv[0, :1]


SC_CORES = 2
SC_SUBCORES = 16
SC_WORKERS = SC_CORES * SC_SUBCORES
SC_CHUNK = 32


def _sc_mesh():
    return plsc.VectorSubcoreMesh(core_axis_name="c", subcore_axis_name="s",
                                  num_cores=SC_CORES, num_subcores=SC_SUBCORES)


def _sc_steps(n_rows):
    per_w = n_rows // SC_WORKERS
    steps = per_w // SC_CHUNK
    assert per_w * SC_WORKERS == n_rows and steps * SC_CHUNK == per_w and steps % 2 == 0, n_rows
    return per_w, steps


def _sc_gather_rows(table, idx):
    p = idx.shape[0]
    d = table.shape[1]
    per_w, steps = _sc_steps(p)
    idx3 = idx.reshape(SC_WORKERS, steps, SC_CHUNK)

    @functools.partial(
        pl.kernel, mesh=_sc_mesh(),
        out_type=jax.ShapeDtypeStruct((p, d), table.dtype),
        scratch_types=[
            pltpu.VMEM((steps, SC_CHUNK), jnp.int32),
            pltpu.VMEM((SC_CHUNK, d), table.dtype),
            pltpu.VMEM((SC_CHUNK, d), table.dtype),
            pltpu.SemaphoreType.DMA, pltpu.SemaphoreType.DMA,
            pltpu.SemaphoreType.DMA, pltpu.SemaphoreType.DMA,
        ],
        name="sc_gather_rows",
    )
    def k(table_hbm, idx_hbm, out_hbm, idx_v, buf0, buf1, g0, g1, w0, w1):
        wid = lax.axis_index("s") * SC_CORES + lax.axis_index("c")
        base = wid * per_w
        pltpu.sync_copy(idx_hbm.at[wid], idx_v)

        def gather(s, buf, sem):
            return pltpu.make_async_copy(table_hbm.at[idx_v.at[s]], buf, sem)

        def write(s, buf, sem):
            return pltpu.make_async_copy(buf, out_hbm.at[pl.ds(base + s * SC_CHUNK, SC_CHUNK)], sem)

        gather(0, buf0, g0).start()

        @pl.loop(0, steps, step=2)
        def _(s):
            gather(s + 1, buf1, g1).start()
            gather(s, buf0, g0).wait()
            write(s, buf0, w0).start()
            write(s, buf0, w0).wait()

            @pl.when(s + 2 < steps)
            def _():
                gather(s + 2, buf0, g0).start()

            gather(s + 1, buf1, g1).wait()
            write(s + 1, buf1, w1).start()
            write(s + 1, buf1, w1).wait()

    return k(table, idx3)


def _sc_dispatch(srcs, poss, p_rows):
    d = srcs[0].shape[1]
    dt = srcs[0].dtype
    plans = [_sc_steps(src.shape[0]) for src in srcs]
    idxs = [pos.reshape(2, SC_WORKERS, st, SC_CHUNK) for pos, (_, st) in zip(poss, plans)]
    nseg = len(srcs)
    scratch = [pltpu.VMEM((2, st, SC_CHUNK), jnp.int32) for _, st in plans]
    scratch += [pltpu.VMEM((SC_CHUNK, d), dt), pltpu.VMEM((SC_CHUNK, d), dt)]
    scratch += [pltpu.SemaphoreType.DMA] * 6

    @functools.partial(
        pl.kernel, mesh=_sc_mesh(),
        out_type=jax.ShapeDtypeStruct((p_rows, d), dt),
        scratch_types=scratch,
        name="sc_dispatch",
    )
    def k(*refs):
        src_hbm = refs[:nseg]
        idx_hbm = refs[nseg:2 * nseg]
        out_hbm = refs[2 * nseg]
        idx_v = refs[2 * nseg + 1:3 * nseg + 1]
        buf0, buf1, r0, r1, a0, a1, b0, b1 = refs[3 * nseg + 1:]
        wid = lax.axis_index("s") * SC_CORES + lax.axis_index("c")
        for seg in range(nseg):
            per_w, steps = plans[seg]
            base = wid * per_w
            for kk in range(2):
                pltpu.sync_copy(idx_hbm[seg].at[kk, wid], idx_v[seg].at[kk])

            def read(s, buf, sem, seg=seg, base=base):
                return pltpu.make_async_copy(src_hbm[seg].at[pl.ds(base + s * SC_CHUNK, SC_CHUNK)], buf, sem)

            def scat(kk, s, buf, sem, seg=seg):
                return pltpu.make_async_copy(buf, out_hbm.at[idx_v[seg].at[kk, s]], sem)

            read(0, buf0, r0).start()

            @pl.loop(0, steps, step=2)
            def _(s, read=read, scat=scat, steps=steps):
                read(s + 1, buf1, r1).start()
                read(s, buf0, r0).wait()
                scat(0, s, buf0, a0).start()
                scat(1, s, buf0, b0).start()
                scat(0, s, buf0, a0).wait()
                scat(1, s, buf0, b0).wait()

                @pl.when(s + 2 < steps)
                def _():
                    read(s + 2, buf0, r0).start()

                read(s + 1, buf1, r1).wait()
                scat(0, s + 1, buf1, a1).start()
                scat(1, s + 1, buf1, b1).start()
                scat(0, s + 1, buf1, a1).wait()
                scat(1, s + 1, buf1, b1).wait()

    return k(*srcs, *idxs)


MOE_TM = 512


MOE_TILE_SLOTS = 256


def _experts_kernel(te_ref, nv_ref, xs_ref, wg_hbm, wu_hbm, wd_hbm, ys_ref, wg_v, wu_v, wd_v, sem, *, layer):
    i = pl.program_id(0)
    valid = i < nv_ref[0]
    e = te_ref[i]
    first = jnp.logical_or(i == 0, e != te_ref[jnp.maximum(i - 1, 0)])
    slot = lax.rem(te_ref[MOE_TILE_SLOTS + i], 2)
    nxt = te_ref[2 * MOE_TILE_SLOTS + i]

    def weight_copies(expert, s):
        return (pltpu.make_async_copy(wg_hbm.at[layer, expert], wg_v.at[s], sem.at[s, 0]),
                pltpu.make_async_copy(wu_hbm.at[layer, expert], wu_v.at[s], sem.at[s, 1]),
                pltpu.make_async_copy(wd_hbm.at[layer, expert], wd_v.at[s], sem.at[s, 2]))

    @pl.when(jnp.logical_and(valid, i == 0))
    def _():
        for cp in weight_copies(e, slot):
            cp.start()

    @pl.when(jnp.logical_and(valid, first))
    def _():
        for cp in weight_copies(e, slot):
            cp.wait()

        @pl.when(nxt >= 0)
        def _():
            for cp in weight_copies(nxt, 1 - slot):
                cp.start()

    @pl.when(valid)
    def _():
        h = _unpack_bf16_pairs(xs_ref[...]).astype(BF16)
        half = D_EXPERT // 2
        y = None
        for j in range(2):
            sl = slice(j * half, (j + 1) * half)
            a = jnp.dot(h, wg_v[slot, :, sl].astype(BF16), preferred_element_type=F32)
            u = jnp.dot(h, wu_v[slot, :, sl].astype(BF16), preferred_element_type=F32)
            t = ((a * _sigmoid(a)) * u).astype(BF16)
            yj = jnp.dot(t, wd_v[slot, sl, :].astype(BF16), preferred_element_type=F32)
            y = yj if y is None else y + yj
        ys_ref[...] = _pack_bf16_pairs(y)


def _experts_call(xs, te, nv, wg, wu, wd, layer):
    p_rows, dh = xs.shape
    d = 2 * dh
    tm = MOE_TM
    nt = p_rows // tm
    assert nt <= MOE_TILE_SLOTS and te.shape == (3 * MOE_TILE_SLOTS,)
    row = lambda i, te_r, nv_r: (jnp.minimum(i, nv_r[0] - 1), 0)
    grid_spec = pltpu.PrefetchScalarGridSpec(
        num_scalar_prefetch=2,
        grid=(nt,),
        in_specs=[
            pl.BlockSpec((tm, dh), row),
            pl.BlockSpec(memory_space=pl.ANY),
            pl.BlockSpec(memory_space=pl.ANY),
            pl.BlockSpec(memory_space=pl.ANY),
        ],
        out_specs=pl.BlockSpec((tm, dh), row),
        scratch_shapes=[
            pltpu.VMEM((2, d, D_EXPERT), F32),
            pltpu.VMEM((2, d, D_EXPERT), F32),
            pltpu.VMEM((2, D_EXPERT, d), F32),
            pltpu.SemaphoreType.DMA((2, 3)),
        ],
    )
    return pl.pallas_call(
        functools.partial(_experts_kernel, layer=layer),
        grid_spec=grid_spec,
        out_shape=jax.ShapeDtypeStruct((p_rows, dh), jnp.int32),
        compiler_params=_cparams(("arbitrary",)),
        name="moe_experts",
    )(te, nv, xs, wg, wu, wd)


COMBINE_TM = 1024


def _combine_kernel(y_ref, wt_ref, x_ref, mod_ref, fn_ref, o_ref, *, final):
    xo = x_ref[...] + mod_ref[5:6, :] * _moe_mix(y_ref, wt_ref[0], wt_ref[1])
    if final:
        ms = jnp.mean(xo * xo, axis=-1, keepdims=True)
        xo = xo * lax.rsqrt(ms + EPS) * fn_ref[...]
    o_ref[...] = xo


def _combine_call(y2, wts3, row0, x, mod_l, mod_row, fn, final):
    b, n, d = x.shape
    n_tok = b * n
    tm = COMBINE_TM
    assert n_tok % tm == 0 and row0 % 8 == 0 and (mod_row is not None or n % tm == 0)
    if mod_row is None:
        mod_map = lambda i: ((i * tm) // n, 0, 0)
    else:
        mod_map = lambda i: (mod_row, 0, 0)
    out = pl.pallas_call(
        functools.partial(_combine_kernel, final=final),
        grid=(n_tok // tm,),
        in_specs=[
            pl.BlockSpec((2, tm, d // 2), lambda i: (0, i, 0)),
            pl.BlockSpec((2, tm // 128, 128), lambda i: (0, row0 // 8 + i, 0)),
            pl.BlockSpec((tm, d), lambda i: (i, 0)),
            pl.BlockSpec((None, 6, d), mod_map),
            pl.BlockSpec((1, d), lambda i: (0, 0)),
        ],
        out_specs=pl.BlockSpec((tm, d), lambda i: (i, 0)),
        out_shape=jax.ShapeDtypeStruct((n_tok, d), F32),
        compiler_params=_cparams(("parallel",)),
        name="moe_combine",
    )(y2, wts3, x.reshape(n_tok, d), mod_l, fn)
    return out.reshape(b, n, d)


def _moe_sparse(h_list, lg_list, x_list, mod_l, mod_rows, b_router, wg, wu, wd, layer, fn, final):
    d = h_list[0].shape[-1]
    sizes = [h.shape[0] * h.shape[1] for h in h_list]
    n_tok = sum(sizes)
    lgt = lg_list[0] if len(lg_list) == 1 else jnp.concatenate(lg_list, axis=1)
    pos, wts, te, nv = _route_call(lgt, b_router, MOE_TM)
    p_rows = 2 * n_tok + N_EXPERTS * MOE_TM
    offs = np.cumsum([0] + sizes)
    poss = [pos[:, offs[i]:offs[i + 1]] for i in range(len(sizes))]
    xs = _sc_dispatch([h.reshape(-1, d) for h in h_list], poss, p_rows)
    ys = _experts_call(xs, te, nv, wg, wu, wd, layer)
    outs = []
    for i, x in enumerate(x_list):
        y2 = _sc_gather_rows(ys, poss[i].reshape(-1)).reshape(2, sizes[i], d)
        row0 = int(offs[i]) // 128
        if final:
            outs.append(_combine_call(y2, wts, row0, x, mod_l, mod_rows[i], fn, True))
        else:
            outs.append((y2, wts, row0, mod_l))
    return outs


def _winprep_kernel(wt_ref, o_ref):
    gd0 = C_AQ
    gdw = 2 * GLA_GATE_RANK
    tail = W_IN_REF_COLS - gd0 - gdw
    o_ref[:, 0:gd0] = wt_ref[0:gd0, :].T.astype(BF16)
    o_ref[:, gd0:gd0 + tail] = wt_ref[gd0 + gdw:W_IN_REF_COLS, :].T.astype(BF16)
    gd = wt_ref[gd0:gd0 + 128, :].T
    lane = lax.broadcasted_iota(jnp.int32, gd.shape, 1)
    o_ref[:, C_GD:] = jnp.where(lane < gdw, gd, 0.0).astype(BF16)


def _winprep_call(w_in):
    depth, d, cols = w_in.shape
    assert cols == W_IN_REF_COLS and C_GD == cols - 2 * GLA_GATE_RANK and W_IN_COLS - C_GD == 128
    return pl.pallas_call(
        _winprep_kernel,
        grid=(depth,),
        in_specs=[pl.BlockSpec((None, cols, d), lambda l: (l, 0, 0))],
        out_specs=pl.BlockSpec((None, d, W_IN_COLS), lambda l: (l, 0, 0)),
        out_shape=jax.ShapeDtypeStruct((depth, d, W_IN_COLS), BF16),
        compiler_params=_cparams(("parallel",)),
        name="w_in_prep",
    )(jnp.swapaxes(w_in, 1, 2))


def _gate_up_weights(w_up, b_up):
    z = jnp.zeros((GLA_GATE_RANK, GLA_QK), w_up.dtype)
    top = jnp.concatenate([w_up[0], z], axis=1)
    mid = jnp.concatenate([z, w_up[1]], axis=1)
    pad = jnp.zeros((128 - 2 * GLA_GATE_RANK, 2 * GLA_QK), w_up.dtype)
    return jnp.concatenate([top, mid, pad], axis=0).astype(BF16), b_up.reshape(1, 2 * GLA_QK)


def kernel(x, c, ctx, c_ctx, w_ada, b_ada, norm_mix, norm_ffn, w_in, w_gla_gate_up, b_gla_gate, gla_norm, q_norm,
           k_norm, w_out, w_router, b_router, w_exp_gate, w_exp_up, w_exp_down, final_norm):
    b, n, d = x.shape
    m = ctx.shape[1]
    depth = w_ada.shape[0]
    assert d == D_MODEL and n % GLA_PAIR == 0 and m % GLA_PAIR == 0 and n % GRID_W == 0

    rows = ((b + 1 + 7) // 8) * 8
    cv = jnp.concatenate([c, c_ctx[None, :], jnp.zeros((rows - b - 1, d), F32)], axis=0)
    mod = _ada_call(cv, w_ada, b_ada).reshape(depth, rows, 6, d)

    cs = jnp.asarray(_channel_dft_table()).astype(BF16)
    tab_x = jnp.asarray(_seq_dft_table(n)).astype(BF16)
    tab_c = jnp.asarray(_seq_dft_table(m)).astype(BF16)
    rope_tabs = tuple(jnp.asarray(t) for t in _rope_tables(n))
    bd512 = jnp.asarray(_blockdiag_ones(ATT_WIDTH, ATT_HDIM)).astype(BF16)
    bd256 = jnp.asarray(_blockdiag_ones(GLA_WIDTH, GLA_DV)).astype(BF16)
    wr_t = w_router.T
    wr_hi = wr_t.astype(BF16)
    wrh = jnp.concatenate([wr_hi, (wr_t - wr_hi.astype(F32)).astype(BF16)], axis=0)
    fn = final_norm.reshape(1, d)

    w_in_perm = _winprep_call(w_in)
    w_out_bf = w_out.astype(BF16)

    xc = ctx
    pend_x = pend_c = None
    for l in range(depth):
        ctx_out = l < depth - 1
        mod_l = mod[l]
        w_perm = (w_in_perm, l)
        wup, bup = _gate_up_weights(w_gla_gate_up[l], b_gla_gate[l])
        nw = norm_mix[l].reshape(1, d)
        nf = norm_ffn[l].reshape(1, d)
        qn = jnp.tile(q_norm[l], ATT_HEADS).reshape(1, ATT_WIDTH)
        kn = jnp.tile(k_norm[l], ATT_KV_HEADS).reshape(1, 128)
        gn = jnp.tile(gla_norm[l], GLA_HEADS).reshape(1, GLA_WIDTH)
        wo = (w_out_bf, l)

        pc = _inproj_call(xc, mod_l, b, nw, w_perm, cs, wup, bup, qn, kn, bd512, None, pend_c, full=ctx_out)
        px = _inproj_call(x, mod_l, None, nw, w_perm, cs, wup, bup, qn, kn, bd512, rope_tabs, pend_x)
        if pend_x is not None:
            x, xc = px["x"], pc["x"]

        s_zero = jnp.zeros((b, 2, GLA_WIDTH, GLA_QK), F32)
        of_c, ob_c, s_fin = _gla_call(pc["gqk"], pc["gv"], pc["gvt"], pc["la"], s_zero)
        of_x, ob_x, _ = _gla_call(px["gqk"], px["gv"], px["gvt"], px["la"], s_fin)

        kv_c = (pc["kt"], pc["vd"])
        f_x = _seqdft_call(tab_x, px["ab"].reshape(2 * n, b * FNET_WIDTH), b)
        x, h2_x, lg_x = _attn_outproj_call(px["q"], [kv_c, (px["kt"], px["vd"])], f_x, of_x, ob_x, px["gg"], x,
                                           mod_l, None, wo, gn, bd256, nf, wrh)

        final = l == depth - 1
        wexp = (w_exp_gate, w_exp_up, w_exp_down, l)
        (res_x,) = _moe_sparse([h2_x], [lg_x], [x], mod_l, [None], b_router, *wexp, fn, final)
        if final:
            x = res_x
        else:
            pend_x = res_x

        if ctx_out:
            f_c = _seqdft_call(tab_c, pc["ab"].reshape(2 * m, b * FNET_WIDTH), b)
            xc, h2_c, lg_c = _attn_outproj_call(pc["q"], [kv_c], f_c, of_c, ob_c, pc["gg"], xc,
                                                mod_l, b, wo, gn, bd256, nf, wrh)
            (pend_c,) = _moe_sparse([h2_c], [lg_c], [xc], mod_l, [b], b_router, *wexp, fn, False)
    return x
```

```python
import functools

import numpy as np
import jax
import jax.numpy as jnp
from jax import lax
from jax.experimental import pallas as pl
from jax.experimental.pallas import tpu as pltpu
from jax.experimental.pallas import tpu_sc as plsc

F32 = jnp.float32
BF16 = jnp.bfloat16

D_MODEL = 1024
GRID_W = 64
EPS = 1e-6
LOG2E = 1.4426950408889634

FNET_WIDTH = 256
FNET_GROUPS = 4
FNET_GDIM = 64

GLA_HEADS = 4
GLA_DV = 64
GLA_DK = 32
GLA_WIDTH = 256
GLA_QK = 128
GLA_GATE_RANK = 16
GLA_GATE_NORM = 16.0
GLA_CHUNK = 64
GLA_PAIR = 2 * GLA_CHUNK

ATT_HEADS = 8
ATT_KV_HEADS = 2
ATT_HDIM = 64
ATT_WIDTH = 512
ROPE_FREQS = 16
ROPE_THETA = 10000.0

N_EXPERTS = 16
N_GROUPS = 4
EXPERTS_PER_GROUP = 4
D_EXPERT = 512

C_U = 0
C_GQ = 256
C_GK = 384
C_GV = 512
C_GG = 768
C_AQ = 1024
C_AK = 1536
C_GD = 1792
W_IN_COLS = 1920
W_IN_REF_COLS = 1824

VMEM_LIMIT = 56 * 1024 * 1024


def _cparams(sem):
    return pltpu.CompilerParams(dimension_semantics=sem, vmem_limit_bytes=VMEM_LIMIT)


def _sigmoid(x):
    return 1.0 / (1.0 + jnp.exp(-x))


def _pack_bf16_pairs(x):
    blocks = []
    for t in range(x.shape[1] // 256):
        lo = lax.bitcast_convert_type(x[:, 256 * t:256 * t + 128].astype(BF16).astype(F32), jnp.uint32)
        hi = lax.bitcast_convert_type(x[:, 256 * t + 128:256 * t + 256].astype(BF16).astype(F32), jnp.uint32)
        blocks.append((lo >> 16) | (hi & jnp.uint32(0xFFFF0000)))
    return lax.bitcast_convert_type(jnp.concatenate(blocks, axis=1), jnp.int32)


def _unpack_bf16_pairs(p):
    u = lax.bitcast_convert_type(p, jnp.uint32)
    blocks = []
    for t in range(p.shape[1] // 128):
        word = u[:, 128 * t:128 * (t + 1)]
        blocks += [lax.bitcast_convert_type(word << 16, F32),
                   lax.bitcast_convert_type(word & jnp.uint32(0xFFFF0000), F32)]
    return jnp.concatenate(blocks, axis=1)


def _token_columns(w):
    nr = w.shape[0]
    tm = nr * 128
    lane = lax.broadcasted_iota(jnp.int32, (tm, 128), 1)
    row = lax.broadcasted_iota(jnp.int32, (tm, 128), 0)
    wb = jnp.concatenate([jnp.broadcast_to(w[r:r + 1, :], (128, 128)) for r in range(nr)], axis=0)
    return jnp.sum(jnp.where(lane == (row % 128), wb, 0.0), axis=1, keepdims=True)


def _moe_mix(y_ref, w0, w1):
    return (_token_columns(w0) * _unpack_bf16_pairs(y_ref[0])
            + _token_columns(w1) * _unpack_bf16_pairs(y_ref[1]))


def _nt_dot(a, b):
    return lax.dot_general(a, b, (((1,), (1,)), ((), ())), preferred_element_type=F32)


@functools.lru_cache(maxsize=None)
def _channel_dft_table():
    j = np.arange(FNET_GDIM)
    ang = 2.0 * np.pi * ((j[:, None] * j[None, :]) % FNET_GDIM) / FNET_GDIM
    c = np.cos(ang) / np.sqrt(FNET_GDIM)
    s = np.sin(ang) / np.sqrt(FNET_GDIM)
    out = np.zeros((FNET_WIDTH, 2 * FNET_WIDTH), np.float64)
    for g in range(FNET_GROUPS):
        sl = slice(g * FNET_GDIM, (g + 1) * FNET_GDIM)
        out[sl, sl] = c
        out[sl, FNET_WIDTH + g * FNET_GDIM:FNET_WIDTH + (g + 1) * FNET_GDIM] = s
    return out.astype(np.float32)


@functools.lru_cache(maxsize=None)
def _seq_dft_table(n):
    j = np.arange(n, dtype=np.int64)
    ang = 2.0 * np.pi * ((j[:, None] * j[None, :]) % n) / n
    return np.concatenate([np.cos(ang), -np.sin(ang)], axis=1).astype(np.float32) / np.float32(np.sqrt(n))


@functools.lru_cache(maxsize=None)
def _rope_tables(n):
    rows = n // GRID_W
    row = np.repeat(np.arange(rows), GRID_W).astype(np.float64)
    col = np.tile(np.arange(GRID_W), rows).astype(np.float64)
    inv = ROPE_THETA ** (-np.arange(ROPE_FREQS, dtype=np.float64) * 2.0 / (2 * ROPE_FREQS))
    ar = row[:, None] * inv[None, :]
    ac = col[:, None] * inv[None, :]
    cos = np.concatenate([np.cos(ar), np.cos(ar), np.cos(ac), np.cos(ac)], axis=1)
    sin = np.concatenate([-np.sin(ar), np.sin(ar), -np.sin(ac), np.sin(ac)], axis=1)
    return (np.tile(cos, (1, 2)).astype(np.float32), np.tile(sin, (1, 2)).astype(np.float32))


@functools.lru_cache(maxsize=None)
def _blockdiag_ones(width, blk):
    i = np.arange(width)
    return (i[:, None] // blk == i[None, :] // blk).astype(np.float32)


def _ada_kernel(cv_ref, w_ref, b_ref, o_ref):
    cv = cv_ref[...]
    a = (cv * _sigmoid(cv)).astype(BF16)
    o_ref[...] = jnp.dot(a, w_ref[...].astype(BF16), preferred_element_type=F32) + b_ref[...]


def _ada_call(cv, w_ada, b_ada):
    depth, d, d6 = w_ada.shape
    tn = 1536
    rows = cv.shape[0]
    return pl.pallas_call(
        _ada_kernel,
        grid=(depth, d6 // tn),
        in_specs=[
            pl.BlockSpec((rows, d), lambda l, j: (0, 0)),
            pl.BlockSpec((None, d, tn), lambda l, j: (l, 0, j)),
            pl.BlockSpec((None, 1, tn), lambda l, j: (l, 0, j)),
        ],
        out_specs=pl.BlockSpec((None, rows, tn), lambda l, j: (l, 0, j)),
        out_shape=jax.ShapeDtypeStruct((depth, rows, d6), F32),
        compiler_params=_cparams(("parallel", "parallel")),
        name="ada_mod",
    )(cv, w_ada, b_ada.reshape(depth, 1, d6))


def _swap16(x):
    lane = lax.broadcasted_iota(jnp.int32, x.shape, 1)
    first = (lane % 32) < 16
    return jnp.where(first, pltpu.roll(x, 112, 1), pltpu.roll(x, 16, 1))


def _head_rms(x, bd, w):
    ms = jnp.dot((x * x).astype(BF16), bd, preferred_element_type=F32) * (1.0 / ATT_HDIM)
    return x * lax.rsqrt(ms + EPS) * w


def _inproj_kernel(*refs, rope, pending_rows, names):
    refs = list(refs)
    x_ref, mod_ref, nw_ref, w_ref, cs_ref, wup_ref, bup_ref, qn_ref, kn_ref, bd_ref = refs[:10]
    del refs[:10]
    if rope:
        cos_ref, sin_ref = refs[:2]
        del refs[:2]
    if pending_rows is not None:
        y_ref, wt_ref, modp_ref = refs[:3]
        del refs[:3]
    out = dict(zip(names, refs))
    full = "q" in out
    x = x_ref[...]
    if pending_rows is not None:
        xnew_ref = out["x"]
        row0, rows_per_sample = pending_rows
        nr = x.shape[0] // 128
        r = row0 + pl.program_id(0) * rows_per_sample + pl.program_id(1) * nr
        sub = lax.rem(r, 8)
        w = [wt_ref[kk, 0:nr, :] for kk in range(2)]
        for blk in range(1, 8 // nr):
            w = [jnp.where(sub == blk * nr, wt_ref[kk, blk * nr:(blk + 1) * nr, :], w[kk]) for kk in range(2)]
        x = x + modp_ref[5:6, :] * _moe_mix(y_ref, w[0], w[1])
        xnew_ref[...] = x
    ms = jnp.mean(x * x, axis=-1, keepdims=True)
    y = x * lax.rsqrt(ms + EPS) * nw_ref[...]
    h = y * (1.0 + mod_ref[1:2, :]) + mod_ref[0:1, :]
    hb = h.astype(BF16)

    def proj(c0, width):
        return jnp.dot(hb, w_ref[:, c0:c0 + width], preferred_element_type=F32)

    if full:
        uab = jnp.dot(proj(C_U, FNET_WIDTH).astype(BF16), cs_ref[...], preferred_element_type=F32)
        out["ab"][0] = uab[:, :FNET_WIDTH].astype(BF16)
        out["ab"][1] = uab[:, FNET_WIDTH:].astype(BF16)

    out["gqk"][...] = proj(C_GQ, 2 * GLA_QK)
    gv = proj(C_GV, GLA_WIDTH)
    out["gv"][...] = gv.astype(BF16)
    out["gvt"][...] = gv.T.astype(BF16)
    if full:
        out["gg"][...] = proj(C_GG, GLA_WIDTH).astype(BF16)
    pre = jnp.dot(proj(C_GD, 128).astype(BF16), wup_ref[...], preferred_element_type=F32) + bup_ref[...]
    out["la"][...] = (jnp.minimum(pre, 0.0) - jnp.log1p(jnp.exp(-jnp.abs(pre)))) * (1.0 / GLA_GATE_NORM)

    bd = bd_ref[...]
    kv = proj(C_AK, 256)
    k = _head_rms(kv[:, :128], bd[:128, :128], kn_ref[...])
    if rope:
        cos = cos_ref[...]
        sin = sin_ref[...]
        k = k * cos + _swap16(k) * sin
    if full:
        q = _head_rms(proj(C_AQ, ATT_WIDTH), bd, qn_ref[...])
        if rope:
            q = jnp.concatenate(
                [q[:, s:s + 128] * cos + _swap16(q[:, s:s + 128]) * sin for s in range(0, ATT_WIDTH, 128)], axis=1)
        out["q"][...] = (q * (ATT_HDIM ** -0.5 * LOG2E)).astype(BF16)
    kt_ref, vd_ref = out["kt"], out["vd"]
    v = kv[:, 128:]
    lo = lax.broadcasted_iota(jnp.int32, k.shape, 1) < ATT_HDIM
    k_sw = pltpu.roll(k, ATT_HDIM, 1)
    v_sw = pltpu.roll(v, ATT_HDIM, 1)
    kt_ref[0] = jnp.where(lo, k, k_sw).T.astype(BF16)
    kt_ref[1] = jnp.where(lo, k_sw, k).T.astype(BF16)
    vd_ref[0] = jnp.where(lo, v, 1.0).astype(BF16)
    vd_ref[1] = jnp.where(lo, 1.0, v_sw).astype(BF16)
    vd_ref[2] = jnp.where(lo, v_sw, 1.0).astype(BF16)
    vd_ref[3] = jnp.where(lo, 1.0, v).astype(BF16)


def _inproj_call(x, mod_l, mod_row, nw, w_perm, cs, wup, bup, qn, kn, bd, rope_tabs, pending=None, full=True):
    b, n, d = x.shape
    tm = min(1024, n)
    nt = n // tm
    rope = rope_tabs is not None
    if mod_row is None:
        mod_map = lambda bi, i: (bi, 0, 0)
    else:
        mod_map = lambda bi, i: (mod_row, 0, 0)
    const = lambda bi, i: (0, 0)
    in_specs = [
        pl.BlockSpec((None, tm, d), lambda bi, i: (bi, i, 0)),
        pl.BlockSpec((None, 6, d), mod_map),
        pl.BlockSpec((1, d), const),
        pl.BlockSpec((None, d, W_IN_COLS), lambda bi, i: (w_perm[1], 0, 0)),
        pl.BlockSpec((FNET_WIDTH, 2 * FNET_WIDTH), const),
        pl.BlockSpec((128, 2 * GLA_QK), const),
        pl.BlockSpec((1, 2 * GLA_QK), const),
        pl.BlockSpec((1, ATT_WIDTH), const),
        pl.BlockSpec((1, 128), const),
        pl.BlockSpec((ATT_WIDTH, ATT_WIDTH), const),
    ]
    args = [x, mod_l, nw, w_perm[0], cs, wup, bup, qn, kn, bd]
    if rope:
        in_specs += [pl.BlockSpec((tm, 128), lambda bi, i: (i, 0)), pl.BlockSpec((tm, 128), lambda bi, i: (i, 0))]
        args += list(rope_tabs)
    pending_rows = None
    if pending is not None:
        y2, wts3, row0, mod_prev = pending
        nr = tm // 128
        rps = n // 128
        assert 8 % nr == 0 and row0 % nr == 0 and rps % nr == 0
        pending_rows = (row0, rps)
        in_specs += [
            pl.BlockSpec((2, None, tm, d // 2), lambda bi, i: (0, bi, i, 0)),
            pl.BlockSpec((2, 8, 128), lambda bi, i: (0, (row0 + bi * rps + i * nr) // 8, 0)),
            pl.BlockSpec((None, 6, d), mod_map),
        ]
        args += [y2.reshape(2, b, n, d // 2), wts3, mod_prev]
    tok = lambda w: pl.BlockSpec((None, tm, w), lambda bi, i: (bi, i, 0))
    outs = []
    if full:
        outs.append(("ab", jax.ShapeDtypeStruct((2, n, b * FNET_WIDTH), BF16),
                     pl.BlockSpec((2, tm, FNET_WIDTH), lambda bi, i: (0, i, bi))))
    outs += [
        ("gqk", jax.ShapeDtypeStruct((b, n, 2 * GLA_QK), F32), tok(2 * GLA_QK)),
        ("gv", jax.ShapeDtypeStruct((b, n, GLA_WIDTH), BF16), tok(GLA_WIDTH)),
        ("gvt", jax.ShapeDtypeStruct((b, GLA_WIDTH, n), BF16),
         pl.BlockSpec((None, GLA_WIDTH, tm), lambda bi, i: (bi, 0, i))),
    ]
    if full:
        outs.append(("gg", jax.ShapeDtypeStruct((b, n, GLA_WIDTH), BF16), tok(GLA_WIDTH)))
    outs.append(("la", jax.ShapeDtypeStruct((b, n, 2 * GLA_QK), F32), tok(2 * GLA_QK)))
    if full:
        outs.append(("q", jax.ShapeDtypeStruct((b, n, ATT_WIDTH), BF16), tok(ATT_WIDTH)))
    outs += [
        ("kt", jax.ShapeDtypeStruct((b, ATT_KV_HEADS, 128, n), BF16),
         pl.BlockSpec((None, ATT_KV_HEADS, 128, tm), lambda bi, i: (bi, 0, 0, i))),
        ("vd", jax.ShapeDtypeStruct((b, 2 * ATT_KV_HEADS, n, 128), BF16),
         pl.BlockSpec((None, 2 * ATT_KV_HEADS, tm, 128), lambda bi, i: (bi, 0, i, 0))),
    ]
    if pending is not None:
        outs.append(("x", jax.ShapeDtypeStruct((b, n, d), F32), tok(d)))
    names = tuple(o[0] for o in outs)
    res = pl.pallas_call(
        functools.partial(_inproj_kernel, rope=rope, pending_rows=pending_rows, names=names),
        grid=(b, nt),
        in_specs=in_specs,
        out_specs=tuple(o[2] for o in outs),
        out_shape=tuple(o[1] for o in outs),
        compiler_params=_cparams(("parallel", "parallel")),
        name="inproj_rope" if rope else "inproj_ctx",
    )(*args)
    return dict(zip(names, res))


def _seqdft_kernel(t_ref, ab_ref, o_ref):
    y = jnp.dot(t_ref[...], ab_ref[...], preferred_element_type=F32)
    for bb in range(o_ref.shape[0]):
        o_ref[bb] = y[:, bb * FNET_WIDTH:(bb + 1) * FNET_WIDTH].astype(BF16)


def _seqdft_call(table, ab, b):
    n = table.shape[0]
    tm = min(512, n)
    nb = 4 if b % 4 == 0 else (2 if b % 2 == 0 else 1)
    return pl.pallas_call(
        _seqdft_kernel,
        grid=(b // nb, n // tm),
        in_specs=[
            pl.BlockSpec((tm, 2 * n), lambda c, i: (i, 0)),
            pl.BlockSpec((2 * n, nb * FNET_WIDTH), lambda c, i: (0, c)),
        ],
        out_specs=pl.BlockSpec((nb, tm, FNET_WIDTH), lambda c, i: (c, i, 0)),
        out_shape=jax.ShapeDtypeStruct((b, n, FNET_WIDTH), BF16),
        compiler_params=_cparams(("parallel", "parallel")),
        name="seq_dft",
    )(table, ab)


def _gla_dir(qk, v, vt, a, s_in, fwd):
    p = GLA_PAIR
    r = lax.broadcasted_iota(jnp.int32, (p, p), 0)
    c = lax.broadcasted_iota(jnp.int32, (p, p), 1)
    same = (r // GLA_CHUNK) == (c // GLA_CHUNK)
    tri = same & ((c <= r) if fwd else (c >= r))
    row_lo = r < GLA_CHUNK
    rin = r % GLA_CHUNK

    q = qk[:, :GLA_QK] * (GLA_DK ** -0.5)
    k = qk[:, GLA_QK:]
    cum = a
    sh = 1
    while sh < GLA_CHUNK:
        if fwd:
            cum = cum + jnp.where(rin >= sh, pltpu.roll(cum, sh, 0), 0.0)
        else:
            cum = cum + jnp.where(rin < GLA_CHUNK - sh, pltpu.roll(cum, p - sh, 0), 0.0)
        sh *= 2
    if fwd:
        last0, last1 = cum[GLA_CHUNK - 1:GLA_CHUNK, :], cum[p - 1:p, :]
    else:
        last0, last1 = cum[0:1, :], cum[GLA_CHUNK:GLA_CHUNK + 1, :]
    lastb = jnp.where(row_lo, last0, last1)
    qt = q * jnp.exp(cum)
    kt = k * jnp.exp(-cum)
    kd = k * jnp.exp(lastb - cum)

    kt_b = kt.astype(BF16)
    zk = jnp.zeros_like(kt_b)
    ks = jnp.concatenate([jnp.where((c // GLA_DK) == hh, kt_b, zk) for hh in range(GLA_HEADS)], axis=0)
    att = _nt_dot(qt.astype(BF16), ks)
    tri4 = jnp.concatenate([tri] * GLA_HEADS, axis=1)
    att = jnp.where(tri4, att, 0.0).astype(BF16)
    col = lax.broadcasted_iota(jnp.int32, (p, GLA_WIDTH), 1)
    zv = jnp.zeros_like(v)
    vs = jnp.concatenate([jnp.where((col // GLA_DV) == hh, v, zv) for hh in range(GLA_HEADS)], axis=0)
    o_intra = jnp.dot(att, vs, preferred_element_type=F32)

    sr = lax.broadcasted_iota(jnp.int32, (GLA_WIDTH, GLA_QK), 0)
    sc = lax.broadcasted_iota(jnp.int32, (GLA_WIDTH, GLA_QK), 1)
    bdm = (sr // GLA_DV) == (sc // GLA_DK)
    first, second = (0, 1) if fwd else (1, 0)
    lasts = (last0, last1)
    in_chunk = (row_lo, jnp.logical_not(row_lo))
    kd2 = jnp.concatenate([jnp.where(in_chunk[0], kd, 0.0), jnp.where(in_chunk[1], kd, 0.0)], axis=1).astype(BF16)
    kvt2 = jnp.dot(vt, kd2, preferred_element_type=F32)
    kvt = (kvt2[:, :GLA_QK], kvt2[:, GLA_QK:])
    s_a = s_in
    s_b = s_a * jnp.exp(lasts[first]) + jnp.where(bdm, kvt[first], 0.0)
    s_c = s_b * jnp.exp(lasts[second]) + jnp.where(bdm, kvt[second], 0.0)
    q2 = jnp.concatenate([jnp.where(in_chunk[first], qt, 0.0), jnp.where(in_chunk[second], qt, 0.0)], axis=1)
    s2 = jnp.concatenate([s_a, s_b], axis=1).astype(BF16)
    o_inter = _nt_dot(q2.astype(BF16), s2)
    return o_intra + o_inter, s_c


def _gla_kernel(qkf, vf, vtf, laf, qkb, vb, vtb, lab, s0_ref, of_ref, ob_ref, sfin_ref, s_scr):
    i = pl.program_id(1)

    @pl.when(i == 0)
    def _():
        s_scr[...] = s0_ref[...]

    pairs = qkf.shape[1] // GLA_PAIR
    for gi in range(qkf.shape[0]):
        sf = s_scr[gi, 0]
        sb = s_scr[gi, 1]
        for pi in range(pairs):
            rf = slice(pi * GLA_PAIR, (pi + 1) * GLA_PAIR)
            rb = slice((pairs - 1 - pi) * GLA_PAIR, (pairs - pi) * GLA_PAIR)
            o1, sf = _gla_dir(qkf[gi, rf, :], vf[gi, rf, :], vtf[gi, :, rf], laf[gi, rf, :], sf, True)
            o2, sb = _gla_dir(qkb[gi, rb, :], vb[gi, rb, :], vtb[gi, :, rb], lab[gi, rb, :], sb, False)
            of_ref[gi, rf, :] = o1.astype(of_ref.dtype)
            ob_ref[gi, rb, :] = o2.astype(ob_ref.dtype)
        s_scr[gi, 0] = sf
        s_scr[gi, 1] = sb

    @pl.when(i == pl.num_programs(1) - 1)
    def _():
        sfin_ref[...] = s_scr[...]


def _gla_call(gqk, gv, gvt, la, s0):
    b, n, _ = gqk.shape
    p = GLA_PAIR * (2 if n % (2 * GLA_PAIR) == 0 else 1)
    npair = n // p
    gb = 4 if b % 4 == 0 else (2 if b % 2 == 0 else 1)
    fw = lambda bi, i: (bi, i, 0)
    bw = lambda bi, i: (bi, npair - 1 - i, 0)
    in_specs = [
        pl.BlockSpec((gb, p, 2 * GLA_QK), fw),
        pl.BlockSpec((gb, p, GLA_WIDTH), fw),
        pl.BlockSpec((gb, GLA_WIDTH, p), lambda bi, i: (bi, 0, i)),
        pl.BlockSpec((gb, p, GLA_QK), fw),
        pl.BlockSpec((gb, p, 2 * GLA_QK), bw),
        pl.BlockSpec((gb, p, GLA_WIDTH), bw),
        pl.BlockSpec((gb, GLA_WIDTH, p), lambda bi, i: (bi, 0, npair - 1 - i)),
        pl.BlockSpec((gb, p, GLA_QK), lambda bi, i: (bi, npair - 1 - i, 1)),
        pl.BlockSpec((gb, 2, GLA_WIDTH, GLA_QK), lambda bi, i: (bi, 0, 0, 0)),
    ]
    out_specs = (
        pl.BlockSpec((gb, p, GLA_WIDTH), fw),
        pl.BlockSpec((gb, p, GLA_WIDTH), bw),
        pl.BlockSpec((gb, 2, GLA_WIDTH, GLA_QK), lambda bi, i: (bi, 0, 0, 0)),
    )
    out_shape = (
        jax.ShapeDtypeStruct((b, n, GLA_WIDTH), BF16),
        jax.ShapeDtypeStruct((b, n, GLA_WIDTH), BF16),
        jax.ShapeDtypeStruct((b, 2, GLA_WIDTH, GLA_QK), F32),
    )
    return pl.pallas_call(
        _gla_kernel,
        grid=(b // gb, npair),
        in_specs=in_specs,
        out_specs=out_specs,
        out_shape=out_shape,
        scratch_shapes=[pltpu.VMEM((gb, 2, GLA_WIDTH, GLA_QK), F32)],
        compiler_params=_cparams(("parallel", "arbitrary")),
        name="gla_scan",
    )(gqk, gv, gvt, la, gqk, gv, gvt, la, s0)


def _attn_heads(q_ref, parts):
    tq = q_ref.shape[0]
    lane = lax.broadcasted_iota(jnp.int32, (tq, 128), 1)
    lo = lane < ATT_HDIM
    blocks = []
    for j in range(ATT_HEADS // 2):
        q128 = q_ref[:, 128 * j:128 * (j + 1)]
        g = (2 * j) // (ATT_HEADS // ATT_KV_HEADS)
        outs = []
        for half in range(2):
            qm = jnp.where(lo if half == 0 else jnp.logical_not(lo), q128, jnp.zeros_like(q128))
            ss = [jnp.dot(qm, kt_ref[g], preferred_element_type=F32) for kt_ref, _ in parts]
            m = functools.reduce(jnp.maximum, [jnp.max(s, axis=-1, keepdims=True) for s in ss])
            ps = [jnp.exp2(s - m).astype(BF16) for s in ss]
            pv = functools.reduce(
                lambda u, w: u + w,
                [jnp.dot(pp, vd_ref[2 * g + half], preferred_element_type=F32) for pp, (_, vd_ref) in zip(ps, parts)])
            den = pv[:, ATT_HDIM:ATT_HDIM + 1] if half == 0 else pv[:, 0:1]
            outs.append(pv / den)
        blocks.append(jnp.where(lo, outs[0], outs[1]).astype(BF16))
    return jnp.concatenate(blocks, axis=1)


def _attn_outproj_kernel(*refs, nparts):
    q_ref = refs[0]
    parts = [(refs[1 + 2 * i], refs[2 + 2 * i]) for i in range(nparts)]
    (f_ref, of_ref, ob_ref, gg_ref, x_ref, mod_ref, w_ref, gn_ref, bd_ref, nf_ref, wr_ref,
     xn_ref, h2_ref, lg_ref) = refs[1 + 2 * nparts:]
    att = _attn_heads(q_ref, parts)
    o = of_ref[...].astype(F32) + ob_ref[...].astype(F32)
    ms = jnp.dot((o * o).astype(BF16), bd_ref[...], preferred_element_type=F32) * (1.0 / GLA_DV)
    on = o * lax.rsqrt(ms + EPS) * gn_ref[...]
    g = gg_ref[...].astype(F32)
    gl = (on * (g * _sigmoid(g))).astype(BF16)
    ox = (jnp.dot(f_ref[...], w_ref[0:FNET_WIDTH, :], preferred_element_type=F32)
          + jnp.dot(gl, w_ref[FNET_WIDTH:FNET_WIDTH + GLA_WIDTH, :], preferred_element_type=F32)
          + jnp.dot(att, w_ref[FNET_WIDTH + GLA_WIDTH:, :], preferred_element_type=F32))
    xn = x_ref[...] + mod_ref[2:3, :] * ox
    xn_ref[...] = xn
    ms2 = jnp.mean(xn * xn, axis=-1, keepdims=True)
    h2 = xn * lax.rsqrt(ms2 + EPS) * nf_ref[...] * (1.0 + mod_ref[4:5, :]) + mod_ref[3:4, :]
    h2_ref[...] = _pack_bf16_pairs(h2)
    lg2 = _nt_dot(wr_ref[...], h2.astype(BF16))
    lg_ref[...] = lg2[:N_EXPERTS, :] + lg2[N_EXPERTS:, :]


def _attn_outproj_call(q, kv_parts, f, of, ob, gg, x, mod_l, mod_row, w_out, gn, bd, nf, wr):
    b, n, d = x.shape
    tm = min(1024, n)
    nt = n // tm
    if mod_row is None:
        mod_map = lambda bi, i: (bi, 0, 0)
    else:
        mod_map = lambda bi, i: (mod_row, 0, 0)
    const = lambda bi, i: (0, 0)
    tok = lambda w: pl.BlockSpec((None, tm, w), lambda bi, i: (bi, i, 0))
    in_specs = [tok(ATT_WIDTH)]
    args = [q]
    for kt, vd in kv_parts:
        m = kt.shape[-1]
        in_specs.append(pl.BlockSpec((None, ATT_KV_HEADS, 128, m), lambda bi, i: (bi, 0, 0, 0)))
        in_specs.append(pl.BlockSpec((None, 2 * ATT_KV_HEADS, m, 128), lambda bi, i: (bi, 0, 0, 0)))
        args += [kt, vd]
    in_specs += [
        tok(FNET_WIDTH), tok(GLA_WIDTH), tok(GLA_WIDTH), tok(GLA_WIDTH), tok(d),
        pl.BlockSpec((None, 6, d), mod_map),
        pl.BlockSpec((None, d, d), lambda bi, i: (w_out[1], 0, 0)),
        pl.BlockSpec((1, GLA_WIDTH), const),
        pl.BlockSpec((GLA_WIDTH, GLA_WIDTH), const),
        pl.BlockSpec((1, d), const),
        pl.BlockSpec((2 * N_EXPERTS, d), const),
    ]
    args += [f, of, ob, gg, x, mod_l, w_out[0], gn, bd, nf, wr]
    out_specs = (
        tok(d), tok(d // 2),
        pl.BlockSpec((N_EXPERTS, tm), lambda bi, i: (0, bi * nt + i)),
    )
    out_shape = (
        jax.ShapeDtypeStruct((b, n, d), F32),
        jax.ShapeDtypeStruct((b, n, d // 2), jnp.int32),
        jax.ShapeDtypeStruct((N_EXPERTS, b * n), F32),
    )
    return pl.pallas_call(
        functools.partial(_attn_outproj_kernel, nparts=len(kv_parts)),
        grid=(b, nt),
        in_specs=in_specs,
        out_specs=out_specs,
        out_shape=out_shape,
        compiler_params=_cparams(("parallel", "parallel")),
        name="attn_outproj",
    )(*args)


MOE_TILE_SLOTS = 256
MOE_TILE_ROWS = 8

def _route_kernel(b_ref, lg_ref, pos_ref, wt_ref, te_ref, nv_ref, *, tm):
    r = lg_ref.shape[1]
    s = [_sigmoid(lg_ref[e]) for e in range(N_EXPERTS)]
    sel = [s[e] + b_ref[e] for e in range(N_EXPERTS)]
    grp = []
    for g in range(N_GROUPS):
        a, b, c, d = sel[4 * g:4 * g + 4]
        hi1, lo1 = jnp.maximum(a, b), jnp.minimum(a, b)
        hi2, lo2 = jnp.maximum(c, d), jnp.minimum(c, d)
        m1 = jnp.maximum(hi1, hi2)
        m2 = jnp.maximum(jnp.minimum(hi1, hi2), jnp.maximum(lo1, lo2))
        grp.append(m1 + m2)
    one = jnp.ones_like(s[0])
    zero = jnp.zeros_like(s[0])
    msk = []
    for g in range(N_GROUPS):
        isg = one
        for g2 in range(N_GROUPS):
            if g2 < g:
                isg = isg * jnp.where(grp[g] > grp[g2], one, zero)
            elif g2 > g:
                isg = isg * jnp.where(grp[g] >= grp[g2], one, zero)
        for li in range(EXPERTS_PER_GROUP):
            e = 4 * g + li
            rank = zero
            for lj in range(EXPERTS_PER_GROUP):
                ej = 4 * g + lj
                if lj < li:
                    rank = rank + jnp.where(sel[ej] >= sel[e], one, zero)
                elif lj > li:
                    rank = rank + jnp.where(sel[ej] > sel[e], one, zero)
            msk.append(jnp.where(rank < 2.0, isg, zero))
    den = functools.reduce(lambda u, v: u + v, [msk[e] * s[e] for e in range(N_EXPERTS)])

    li_ = lax.broadcasted_iota(jnp.int32, (128, 128), 0)
    lj_ = lax.broadcasted_iota(jnp.int32, (128, 128), 1)
    upper = jnp.where(li_ < lj_, 1.0, 0.0).astype(BF16)
    ri_ = lax.broadcasted_iota(jnp.int32, (r, r), 0)
    rj_ = lax.broadcasted_iota(jnp.int32, (r, r), 1)
    lower = jnp.where(rj_ < ri_, 1.0, 0.0).astype(BF16)
    tiles = (1, te_ref.shape[1])
    tile_start = lax.broadcasted_iota(jnp.int32, tiles, 1).astype(F32) * float(tm)
    te = jnp.zeros(tiles, F32)
    seg = jnp.zeros(tiles, F32)
    nonempty = []
    off = jnp.zeros((1, 1), F32)
    seen = zero
    pos = [zero, zero]
    wts = [zero, zero]
    for e in range(N_EXPERTS):
        mb = msk[e].astype(BF16)
        lane_pre = jnp.dot(mb, upper, preferred_element_type=F32)
        row_pre = jnp.sum(jnp.dot(lower, mb, preferred_element_type=F32), axis=1, keepdims=True)
        cnt = jnp.sum(jnp.sum(msk[e], axis=1, keepdims=True), axis=0, keepdims=True)
        p_e = off + row_pre + lane_pre
        g_e = s[e] / den
        for kk in range(2):
            hit = msk[e] * jnp.where(seen == float(kk), one, zero)
            pos[kk] = pos[kk] + hit * p_e
            wts[kk] = wts[kk] + hit * g_e
        seen = seen + msk[e]
        off = off + jnp.floor((cnt + float(tm - 1)) * (1.0 / tm)) * float(tm)
        passed = jnp.where(tile_start >= off, 1.0, 0.0)
        te = te + passed
        nonempty.append(jnp.where(cnt > 0.0, 1.0, 0.0))
        seg = seg + nonempty[e] * passed
    for kk in range(2):
        pos_ref[kk] = pos[kk].astype(jnp.int32)
        wt_ref[kk] = wts[kk]
    te = jnp.minimum(te, float(N_EXPERTS - 1))
    nxt = jnp.full(tiles, -1.0, F32)
    for e in reversed(range(N_EXPERTS)):
        nxt = jnp.where(jnp.logical_and(nonempty[e] > 0.0, te < float(e)), float(e), nxt)
    te_ref[0:1, :] = te.astype(jnp.int32)
    te_ref[1:2, :] = seg.astype(jnp.int32)
    te_ref[2:3, :] = nxt.astype(jnp.int32)
    te_ref[3:, :] = jnp.zeros((te_ref.shape[0] - 3, te_ref.shape[1]), jnp.int32)
    nv_ref[...] = jnp.broadcast_to(off * (1.0 / tm), nv_ref.shape).astype(jnp.int32)


def _route_call(lgt, b_router, tm):
    n_tok = lgt.shape[1]
    r = n_tok // 128
    assert r * 128 == n_tok and r % 8 == 0 and 2 * n_tok // tm + N_EXPERTS <= MOE_TILE_SLOTS
    lg3 = lgt.reshape(N_EXPERTS, r, 128)
    full3 = lambda k: pl.BlockSpec((k, r, 128), lambda: (0, 0, 0))
    pos, wts, te, nv = pl.pallas_call(
        functools.partial(_route_kernel, tm=tm),
        in_specs=[pl.BlockSpec(memory_space=pltpu.SMEM), full3(N_EXPERTS)],
        out_specs=(full3(2), full3(2), pl.BlockSpec((MOE_TILE_ROWS, MOE_TILE_SLOTS), lambda: (0, 0)), pl.BlockSpec((1, 128), lambda: (0, 0))),
        out_shape=(
            jax.ShapeDtypeStruct((2, r, 128), jnp.int32),
            jax.ShapeDtypeStruct((2, r, 128), F32),
            jax.ShapeDtypeStruct((MOE_TILE_ROWS, MOE_TILE_SLOTS), jnp.int32),
            jax.ShapeDtypeStruct((1, 128), jnp.int32),
        ),
        compiler_params=pltpu.CompilerParams(vmem_limit_bytes=VMEM_LIMIT),
        name="route",
    )(b_router, lg3)
    return pos.reshape(2, n_tok), wts, te[:3].reshape(3 * MOE_TILE_SLOTS), nv[0, :1]


SC_CORES = 2
SC_SUBCORES = 16
SC_WORKERS = SC_CORES * SC_SUBCORES
SC_CHUNK = 32


def _sc_mesh():
    return plsc.VectorSubcoreMesh(core_axis_name="c", subcore_axis_name="s",
                                  num_cores=SC_CORES, num_subcores=SC_SUBCORES)


def _sc_steps(n_rows):
    per_w = n_rows // SC_WORKERS
    steps = per_w // SC_CHUNK
    assert per_w * SC_WORKERS == n_rows and steps * SC_CHUNK == per_w and steps % 2 == 0, n_rows
    return per_w, steps


def _sc_gather_rows(table, idx):
    p = idx.shape[0]
    d = table.shape[1]
    per_w, steps = _sc_steps(p)
    idx3 = idx.reshape(SC_WORKERS, steps, SC_CHUNK)

    @functools.partial(
        pl.kernel, mesh=_sc_mesh(),
        out_type=jax.ShapeDtypeStruct((p, d), table.dtype),
        scratch_types=[
            pltpu.VMEM((steps, SC_CHUNK), jnp.int32),
            pltpu.VMEM((SC_CHUNK, d), table.dtype),
            pltpu.VMEM((SC_CHUNK, d), table.dtype),
            pltpu.SemaphoreType.DMA, pltpu.SemaphoreType.DMA,
            pltpu.SemaphoreType.DMA, pltpu.SemaphoreType.DMA,
        ],
        name="sc_gather_rows",
    )
    def k(table_hbm, idx_hbm, out_hbm, idx_v, buf0, buf1, g0, g1, w0, w1):
        wid = lax.axis_index("s") * SC_CORES + lax.axis_index("c")
        base = wid * per_w
        pltpu.sync_copy(idx_hbm.at[wid], idx_v)

        def gather(s, buf, sem):
            return pltpu.make_async_copy(table_hbm.at[idx_v.at[s]], buf, sem)

        def write(s, buf, sem):
            return pltpu.make_async_copy(buf, out_hbm.at[pl.ds(base + s * SC_CHUNK, SC_CHUNK)], sem)

        gather(0, buf0, g0).start()

        @pl.loop(0, steps, step=2)
        def _(s):
            gather(s + 1, buf1, g1).start()
            gather(s, buf0, g0).wait()
            write(s, buf0, w0).start()
            write(s, buf0, w0).wait()

            @pl.when(s + 2 < steps)
            def _():
                gather(s + 2, buf0, g0).start()

            gather(s + 1, buf1, g1).wait()
            write(s + 1, buf1, w1).start()
            write(s + 1, buf1, w1).wait()

    return k(table, idx3)


def _sc_dispatch(srcs, poss, p_rows):
    d = srcs[0].shape[1]
    dt = srcs[0].dtype
    plans = [_sc_steps(src.shape[0]) for src in srcs]
    idxs = [pos.reshape(2, SC_WORKERS, st, SC_CHUNK) for pos, (_, st) in zip(poss, plans)]
    nseg = len(srcs)
    scratch = [pltpu.VMEM((2, st, SC_CHUNK), jnp.int32) for _, st in plans]
    scratch += [pltpu.VMEM((SC_CHUNK, d), dt), pltpu.VMEM((SC_CHUNK, d), dt)]
    scratch += [pltpu.SemaphoreType.DMA] * 6

    @functools.partial(
        pl.kernel, mesh=_sc_mesh(),
        out_type=jax.ShapeDtypeStruct((p_rows, d), dt),
        scratch_types=scratch,
        name="sc_dispatch",
    )
    def k(*refs):
        src_hbm = refs[:nseg]
        idx_hbm = refs[nseg:2 * nseg]
        out_hbm = refs[2 * nseg]
        idx_v = refs[2 * nseg + 1:3 * nseg + 1]
        buf0, buf1, r0, r1, a0, a1, b0, b1 = refs[3 * nseg + 1:]
        wid = lax.axis_index("s") * SC_CORES + lax.axis_index("c")
        for seg in range(nseg):
            per_w, steps = plans[seg]
            base = wid * per_w
            for kk in range(2):
                pltpu.sync_copy(idx_hbm[seg].at[kk, wid], idx_v[seg].at[kk])

            def read(s, buf, sem, seg=seg, base=base):
                return pltpu.make_async_copy(src_hbm[seg].at[pl.ds(base + s * SC_CHUNK, SC_CHUNK)], buf, sem)

            def scat(kk, s, buf, sem, seg=seg):
                return pltpu.make_async_copy(buf, out_hbm.at[idx_v[seg].at[kk, s]], sem)

            read(0, buf0, r0).start()

            @pl.loop(0, steps, step=2)
            def _(s, read=read, scat=scat, steps=steps):
                read(s + 1, buf1, r1).start()
                read(s, buf0, r0).wait()
                scat(0, s, buf0, a0).start()
                scat(1, s, buf0, b0).start()
                scat(0, s, buf0, a0).wait()
                scat(1, s, buf0, b0).wait()

                @pl.when(s + 2 < steps)
                def _():
                    read(s + 2, buf0, r0).start()

                read(s + 1, buf1, r1).wait()
                scat(0, s + 1, buf1, a1).start()
                scat(1, s + 1, buf1, b1).start()
                scat(0, s + 1, buf1, a1).wait()
                scat(1, s + 1, buf1, b1).wait()

    return k(*srcs, *idxs)


MOE_TM = 512


def _experts_kernel(te_ref, nv_ref, xs_ref, wg_hbm, wu_hbm, wd_hbm, ys_ref, wg_v, wu_v, wd_v, sem, *, layer):
    i = pl.program_id(0)
    valid = i < nv_ref[0]
    e = te_ref[i]
    first = jnp.logical_or(i == 0, e != te_ref[jnp.maximum(i - 1, 0)])
    slot = lax.rem(te_ref[MOE_TILE_SLOTS + i], 2)
    nxt = te_ref[2 * MOE_TILE_SLOTS + i]

    def weight_copies(expert, s):
        return (pltpu.make_async_copy(wg_hbm.at[layer, expert], wg_v.at[s], sem.at[s, 0]),
                pltpu.make_async_copy(wu_hbm.at[layer, expert], wu_v.at[s], sem.at[s, 1]),
                pltpu.make_async_copy(wd_hbm.at[layer, expert], wd_v.at[s], sem.at[s, 2]))

    @pl.when(jnp.logical_and(valid, i == 0))
    def _():
        for cp in weight_copies(e, slot):
            cp.start()

    @pl.when(jnp.logical_and(valid, first))
    def _():
        for cp in weight_copies(e, slot):
            cp.wait()

        @pl.when(nxt >= 0)
        def _():
            for cp in weight_copies(nxt, 1 - slot):
                cp.start()

    @pl.when(valid)
    def _():
        h = _unpack_bf16_pairs(xs_ref[...]).astype(BF16)
        half = D_EXPERT // 2
        y = None
        for j in range(2):
            sl = slice(j * half, (j + 1) * half)
            a = jnp.dot(h, wg_v[slot, :, sl].astype(BF16), preferred_element_type=F32)
            u = jnp.dot(h, wu_v[slot, :, sl].astype(BF16), preferred_element_type=F32)
            t = ((a * _sigmoid(a)) * u).astype(BF16)
            yj = jnp.dot(t, wd_v[slot, sl, :].astype(BF16), preferred_element_type=F32)
            y = yj if y is None else y + yj
        ys_ref[...] = _pack_bf16_pairs(y)


def _experts_call(xs, te, nv, wg, wu, wd, layer):
    p_rows, dh = xs.shape
    d = 2 * dh
    tm = MOE_TM
    nt = p_rows // tm
    assert nt <= MOE_TILE_SLOTS and te.shape == (3 * MOE_TILE_SLOTS,)
    row = lambda i, te_r, nv_r: (jnp.minimum(i, nv_r[0] - 1), 0)
    grid_spec = pltpu.PrefetchScalarGridSpec(
        num_scalar_prefetch=2,
        grid=(nt,),
        in_specs=[
            pl.BlockSpec((tm, dh), row),
            pl.BlockSpec(memory_space=pl.ANY),
            pl.BlockSpec(memory_space=pl.ANY),
            pl.BlockSpec(memory_space=pl.ANY),
        ],
        out_specs=pl.BlockSpec((tm, dh), row),
        scratch_shapes=[
            pltpu.VMEM((2, d, D_EXPERT), F32),
            pltpu.VMEM((2, d, D_EXPERT), F32),
            pltpu.VMEM((2, D_EXPERT, d), F32),
            pltpu.SemaphoreType.DMA((2, 3)),
        ],
    )
    return pl.pallas_call(
        functools.partial(_experts_kernel, layer=layer),
        grid_spec=grid_spec,
        out_shape=jax.ShapeDtypeStruct((p_rows, dh), jnp.int32),
        compiler_params=_cparams(("arbitrary",)),
        name="moe_experts",
    )(te, nv, xs, wg, wu, wd)


COMBINE_TM = 1024


def _combine_kernel(y_ref, wt_ref, x_ref, mod_ref, fn_ref, o_ref, *, final):
    xo = x_ref[...] + mod_ref[5:6, :] * _moe_mix(y_ref, wt_ref[0], wt_ref[1])
    if final:
        ms = jnp.mean(xo * xo, axis=-1, keepdims=True)
        xo = xo * lax.rsqrt(ms + EPS) * fn_ref[...]
    o_ref[...] = xo


def _combine_call(y2, wts3, row0, x, mod_l, mod_row, fn, final):
    b, n, d = x.shape
    n_tok = b * n
    tm = COMBINE_TM
    assert n_tok % tm == 0 and row0 % 8 == 0 and (mod_row is not None or n % tm == 0)
    if mod_row is None:
        mod_map = lambda i: ((i * tm) // n, 0, 0)
    else:
        mod_map = lambda i: (mod_row, 0, 0)
    out = pl.pallas_call(
        functools.partial(_combine_kernel, final=final),
        grid=(n_tok // tm,),
        in_specs=[
            pl.BlockSpec((2, tm, d // 2), lambda i: (0, i, 0)),
            pl.BlockSpec((2, tm // 128, 128), lambda i: (0, row0 // 8 + i, 0)),
            pl.BlockSpec((tm, d), lambda i: (i, 0)),
            pl.BlockSpec((None, 6, d), mod_map),
            pl.BlockSpec((1, d), lambda i: (0, 0)),
        ],
        out_specs=pl.BlockSpec((tm, d), lambda i: (i, 0)),
        out_shape=jax.ShapeDtypeStruct((n_tok, d), F32),
        compiler_params=_cparams(("parallel",)),
        name="moe_combine",
    )(y2, wts3, x.reshape(n_tok, d), mod_l, fn)
    return out.reshape(b, n, d)


def _moe_sparse(h_list, lg_list, x_list, mod_l, mod_rows, b_router, wg, wu, wd, layer, fn, final):
    d = h_list[0].shape[-1]
    sizes = [h.shape[0] * h.shape[1] for h in h_list]
    n_tok = sum(sizes)
    lgt = lg_list[0] if len(lg_list) == 1 else jnp.concatenate(lg_list, axis=1)
    pos, wts, te, nv = _route_call(lgt, b_router, MOE_TM)
    p_rows = 2 * n_tok + N_EXPERTS * MOE_TM
    offs = np.cumsum([0] + sizes)
    poss = [pos[:, offs[i]:offs[i + 1]] for i in range(len(sizes))]
    xs = _sc_dispatch([h.reshape(-1, d) for h in h_list], poss, p_rows)
    ys = _experts_call(xs, te, nv, wg, wu, wd, layer)
    outs = []
    for i, x in enumerate(x_list):
        y2 = _sc_gather_rows(ys, poss[i].reshape(-1)).reshape(2, sizes[i], d)
        row0 = int(offs[i]) // 128
        if final:
            outs.append(_combine_call(y2, wts, row0, x, mod_l, mod_rows[i], fn, True))
        else:
            outs.append((y2, wts, row0, mod_l))
    return outs


def _winprep_kernel(wt_ref, o_ref):
    gd0 = C_AQ
    gdw = 2 * GLA_GATE_RANK
    tail = W_IN_REF_COLS - gd0 - gdw
    o_ref[:, 0:gd0] = wt_ref[0:gd0, :].T.astype(BF16)
    o_ref[:, gd0:gd0 + tail] = wt_ref[gd0 + gdw:W_IN_REF_COLS, :].T.astype(BF16)
    gd = wt_ref[gd0:gd0 + 128, :].T
    lane = lax.broadcasted_iota(jnp.int32, gd.shape, 1)
    o_ref[:, C_GD:] = jnp.where(lane < gdw, gd, 0.0).astype(BF16)


def _winprep_call(w_in):
    depth, d, cols = w_in.shape
    assert cols == W_IN_REF_COLS and C_GD == cols - 2 * GLA_GATE_RANK and W_IN_COLS - C_GD == 128
    return pl.pallas_call(
        _winprep_kernel,
        grid=(depth,),
        in_specs=[pl.BlockSpec((None, cols, d), lambda l: (l, 0, 0))],
        out_specs=pl.BlockSpec((None, d, W_IN_COLS), lambda l: (l, 0, 0)),
        out_shape=jax.ShapeDtypeStruct((depth, d, W_IN_COLS), BF16),
        compiler_params=_cparams(("parallel",)),
        name="w_in_prep",
    )(jnp.swapaxes(w_in, 1, 2))


def _gate_up_weights(w_up, b_up):
    z = jnp.zeros((GLA_GATE_RANK, GLA_QK), w_up.dtype)
    top = jnp.concatenate([w_up[0], z], axis=1)
    mid = jnp.concatenate([z, w_up[1]], axis=1)
    pad = jnp.zeros((128 - 2 * GLA_GATE_RANK, 2 * GLA_QK), w_up.dtype)
    return jnp.concatenate([top, mid, pad], axis=0).astype(BF16), b_up.reshape(1, 2 * GLA_QK)


def kernel(x, c, ctx, c_ctx, w_ada, b_ada, norm_mix, norm_ffn, w_in, w_gla_gate_up, b_gla_gate, gla_norm, q_norm,
           k_norm, w_out, w_router, b_router, w_exp_gate, w_exp_up, w_exp_down, final_norm):
    b, n, d = x.shape
    m = ctx.shape[1]
    depth = w_ada.shape[0]
    assert d == D_MODEL and n % GLA_PAIR == 0 and m % GLA_PAIR == 0 and n % GRID_W == 0

    rows = ((b + 1 + 7) // 8) * 8
    cv = jnp.concatenate([c, c_ctx[None, :], jnp.zeros((rows - b - 1, d), F32)], axis=0)
    mod = _ada_call(cv, w_ada, b_ada).reshape(depth, rows, 6, d)

    cs = jnp.asarray(_channel_dft_table()).astype(BF16)
    tab_x = jnp.asarray(_seq_dft_table(n)).astype(BF16)
    tab_c = jnp.asarray(_seq_dft_table(m)).astype(BF16)
    rope_tabs = tuple(jnp.asarray(t) for t in _rope_tables(n))
    bd512 = jnp.asarray(_blockdiag_ones(ATT_WIDTH, ATT_HDIM)).astype(BF16)
    bd256 = jnp.asarray(_blockdiag_ones(GLA_WIDTH, GLA_DV)).astype(BF16)
    wr_t = w_router.T
    wr_hi = wr_t.astype(BF16)
    wrh = jnp.concatenate([wr_hi, (wr_t - wr_hi.astype(F32)).astype(BF16)], axis=0)
    fn = final_norm.reshape(1, d)

    w_in_perm = _winprep_call(w_in)
    w_out_bf = w_out.astype(BF16)

    xc = ctx
    pend_x = pend_c = None
    for l in range(depth):
        ctx_out = l < depth - 1
        mod_l = mod[l]
        w_perm = (w_in_perm, l)
        wup, bup = _gate_up_weights(w_gla_gate_up[l], b_gla_gate[l])
        nw = norm_mix[l].reshape(1, d)
        nf = norm_ffn[l].reshape(1, d)
        qn = jnp.tile(q_norm[l], ATT_HEADS).reshape(1, ATT_WIDTH)
        kn = jnp.tile(k_norm[l], ATT_KV_HEADS).reshape(1, 128)
        gn = jnp.tile(gla_norm[l], GLA_HEADS).reshape(1, GLA_WIDTH)
        wo = (w_out_bf, l)

        pc = _inproj_call(xc, mod_l, b, nw, w_perm, cs, wup, bup, qn, kn, bd512, None, pend_c, full=ctx_out)
        px = _inproj_call(x, mod_l, None, nw, w_perm, cs, wup, bup, qn, kn, bd512, rope_tabs, pend_x)
        if pend_x is not None:
            x, xc = px["x"], pc["x"]

        s_zero = jnp.zeros((b, 2, GLA_WIDTH, GLA_QK), F32)
        of_c, ob_c, s_fin = _gla_call(pc["gqk"], pc["gv"], pc["gvt"], pc["la"], s_zero)
        of_x, ob_x, _ = _gla_call(px["gqk"], px["gv"], px["gvt"], px["la"], s_fin)

        kv_c = (pc["kt"], pc["vd"])
        f_x = _seqdft_call(tab_x, px["ab"].reshape(2 * n, b * FNET_WIDTH), b)
        x, h2_x, lg_x = _attn_outproj_call(px["q"], [kv_c, (px["kt"], px["vd"])], f_x, of_x, ob_x, px["gg"], x,
                                           mod_l, None, wo, gn, bd256, nf, wrh)

        if ctx_out:
            f_c = _seqdft_call(tab_c, pc["ab"].reshape(2 * m, b * FNET_WIDTH), b)
            xc, h2_c, lg_c = _attn_outproj_call(pc["q"], [kv_c], f_c, of_c, ob_c, pc["gg"], xc,
                                                mod_l, b, wo, gn, bd256, nf, wrh)

        final = l == depth - 1
        wexp = (w_exp_gate, w_exp_up, w_exp_down, l)
        if ctx_out:
            res = _moe_sparse([h2_x, h2_c], [lg_x, lg_c], [x, xc], mod_l, [None, b], b_router, *wexp, fn, final)
        else:
            res = _moe_sparse([h2_x], [lg_x], [x], mod_l, [None], b_router, *wexp, fn, final)
        if final:
            x = res[0]
        else:
            pend_x, pend_c = res
    return x
```

```python
import functools

import numpy as np
import jax
import jax.numpy as jnp
from jax import lax
from jax.experimental import pallas as pl
from jax.experimental.pallas import tpu as pltpu
from jax.experimental.pallas import tpu_sc as plsc

F32 = jnp.float32
BF16 = jnp.bfloat16

D_MODEL = 1024
GRID_W = 64
EPS = 1e-6
LOG2E = 1.4426950408889634

FNET_WIDTH = 256
FNET_GROUPS = 4
FNET_GDIM = 64

GLA_HEADS = 4
GLA_DV = 64
GLA_DK = 32
GLA_WIDTH = 256
GLA_QK = 128
GLA_GATE_RANK = 16
GLA_GATE_NORM = 16.0
GLA_CHUNK = 64
GLA_PAIR = 2 * GLA_CHUNK

ATT_HEADS = 8
ATT_KV_HEADS = 2
ATT_HDIM = 64
ATT_WIDTH = 512
ROPE_FREQS = 16
ROPE_THETA = 10000.0

N_EXPERTS = 16
N_GROUPS = 4
EXPERTS_PER_GROUP = 4
D_EXPERT = 512

C_U = 0
C_GQ = 256
C_GK = 384
C_GV = 512
C_GG = 768
C_AQ = 1024
C_AK = 1536
C_GD = 1792
W_IN_COLS = 1920
W_IN_REF_COLS = 1824

VMEM_LIMIT = 56 * 1024 * 1024


def _cparams(sem):
    return pltpu.CompilerParams(dimension_semantics=sem, vmem_limit_bytes=VMEM_LIMIT)


def _sigmoid(x):
    return 1.0 / (1.0 + jnp.exp(-x))


def _pack_bf16_pairs(x):
    blocks = []
    for t in range(x.shape[1] // 256):
        lo = lax.bitcast_convert_type(x[:, 256 * t:256 * t + 128].astype(BF16).astype(F32), jnp.uint32)
        hi = lax.bitcast_convert_type(x[:, 256 * t + 128:256 * t + 256].astype(BF16).astype(F32), jnp.uint32)
        blocks.append((lo >> 16) | (hi & jnp.uint32(0xFFFF0000)))
    return lax.bitcast_convert_type(jnp.concatenate(blocks, axis=1), jnp.int32)


def _unpack_bf16_pairs(p):
    u = lax.bitcast_convert_type(p, jnp.uint32)
    blocks = []
    for t in range(p.shape[1] // 128):
        word = u[:, 128 * t:128 * (t + 1)]
        blocks += [lax.bitcast_convert_type(word << 16, F32),
                   lax.bitcast_convert_type(word & jnp.uint32(0xFFFF0000), F32)]
    return jnp.concatenate(blocks, axis=1)


def _token_columns(w):
    nr = w.shape[0]
    tm = nr * 128
    lane = lax.broadcasted_iota(jnp.int32, (tm, 128), 1)
    row = lax.broadcasted_iota(jnp.int32, (tm, 128), 0)
    wb = jnp.concatenate([jnp.broadcast_to(w[r:r + 1, :], (128, 128)) for r in range(nr)], axis=0)
    return jnp.sum(jnp.where(lane == (row % 128), wb, 0.0), axis=1, keepdims=True)


def _moe_mix(y_ref, w0, w1):
    return (_token_columns(w0) * _unpack_bf16_pairs(y_ref[0])
            + _token_columns(w1) * _unpack_bf16_pairs(y_ref[1]))


def _nt_dot(a, b):
    return lax.dot_general(a, b, (((1,), (1,)), ((), ())), preferred_element_type=F32)


@functools.lru_cache(maxsize=None)
def _channel_dft_table():
    j = np.arange(FNET_GDIM)
    ang = 2.0 * np.pi * ((j[:, None] * j[None, :]) % FNET_GDIM) / FNET_GDIM
    c = np.cos(ang) / np.sqrt(FNET_GDIM)
    s = np.sin(ang) / np.sqrt(FNET_GDIM)
    out = np.zeros((FNET_WIDTH, 2 * FNET_WIDTH), np.float64)
    for g in range(FNET_GROUPS):
        sl = slice(g * FNET_GDIM, (g + 1) * FNET_GDIM)
        out[sl, sl] = c
        out[sl, FNET_WIDTH + g * FNET_GDIM:FNET_WIDTH + (g + 1) * FNET_GDIM] = s
    return out.astype(np.float32)


@functools.lru_cache(maxsize=None)
def _seq_dft_table(n):
    j = np.arange(n, dtype=np.int64)
    ang = 2.0 * np.pi * ((j[:, None] * j[None, :]) % n) / n
    return np.concatenate([np.cos(ang), -np.sin(ang)], axis=1).astype(np.float32) / np.float32(np.sqrt(n))


@functools.lru_cache(maxsize=None)
def _rope_tables(n):
    rows = n // GRID_W
    row = np.repeat(np.arange(rows), GRID_W).astype(np.float64)
    col = np.tile(np.arange(GRID_W), rows).astype(np.float64)
    inv = ROPE_THETA ** (-np.arange(ROPE_FREQS, dtype=np.float64) * 2.0 / (2 * ROPE_FREQS))
    ar = row[:, None] * inv[None, :]
    ac = col[:, None] * inv[None, :]
    cos = np.concatenate([np.cos(ar), np.cos(ar), np.cos(ac), np.cos(ac)], axis=1)
    sin = np.concatenate([-np.sin(ar), np.sin(ar), -np.sin(ac), np.sin(ac)], axis=1)
    return (np.tile(cos, (1, 2)).astype(np.float32), np.tile(sin, (1, 2)).astype(np.float32))


@functools.lru_cache(maxsize=None)
def _blockdiag_ones(width, blk):
    i = np.arange(width)
    return (i[:, None] // blk == i[None, :] // blk).astype(np.float32)


def _ada_kernel(cv_ref, w_ref, b_ref, o_ref):
    cv = cv_ref[...]
    a = (cv * _sigmoid(cv)).astype(BF16)
    o_ref[...] = jnp.dot(a, w_ref[...].astype(BF16), preferred_element_type=F32) + b_ref[...]


def _ada_call(cv, w_ada, b_ada):
    depth, d, d6 = w_ada.shape
    tn = 1536
    rows = cv.shape[0]
    return pl.pallas_call(
        _ada_kernel,
        grid=(depth, d6 // tn),
        in_specs=[
            pl.BlockSpec((rows, d), lambda l, j: (0, 0)),
            pl.BlockSpec((None, d, tn), lambda l, j: (l, 0, j)),
            pl.BlockSpec((None, 1, tn), lambda l, j: (l, 0, j)),
        ],
        out_specs=pl.BlockSpec((None, rows, tn), lambda l, j: (l, 0, j)),
        out_shape=jax.ShapeDtypeStruct((depth, rows, d6), F32),
        compiler_params=_cparams(("parallel", "parallel")),
        name="ada_mod",
    )(cv, w_ada, b_ada.reshape(depth, 1, d6))


def _swap16(x):
    lane = lax.broadcasted_iota(jnp.int32, x.shape, 1)
    first = (lane % 32) < 16
    return jnp.where(first, pltpu.roll(x, 112, 1), pltpu.roll(x, 16, 1))


def _head_rms(x, bd, w):
    ms = jnp.dot((x * x).astype(BF16), bd, preferred_element_type=F32) * (1.0 / ATT_HDIM)
    return x * lax.rsqrt(ms + EPS) * w


def _inproj_kernel(*refs, rope, pending_rows, names):
    refs = list(refs)
    x_ref, mod_ref, nw_ref, w_ref, cs_ref, wup_ref, bup_ref, qn_ref, kn_ref, bd_ref = refs[:10]
    del refs[:10]
    if rope:
        cos_ref, sin_ref = refs[:2]
        del refs[:2]
    if pending_rows is not None:
        y_ref, wt_ref, modp_ref = refs[:3]
        del refs[:3]
    out = dict(zip(names, refs))
    full = "q" in out
    x = x_ref[...]
    if pending_rows is not None:
        xnew_ref = out["x"]
        row0, rows_per_sample = pending_rows
        nr = x.shape[0] // 128
        r = row0 + pl.program_id(0) * rows_per_sample + pl.program_id(1) * nr
        sub = lax.rem(r, 8)
        w = [wt_ref[kk, 0:nr, :] for kk in range(2)]
        for blk in range(1, 8 // nr):
            w = [jnp.where(sub == blk * nr, wt_ref[kk, blk * nr:(blk + 1) * nr, :], w[kk]) for kk in range(2)]
        x = x + modp_ref[5:6, :] * _moe_mix(y_ref, w[0], w[1])
        xnew_ref[...] = x
    ms = jnp.mean(x * x, axis=-1, keepdims=True)
    y = x * lax.rsqrt(ms + EPS) * nw_ref[...]
    h = y * (1.0 + mod_ref[1:2, :]) + mod_ref[0:1, :]
    hb = h.astype(BF16)

    def proj(c0, width):
        return jnp.dot(hb, w_ref[:, c0:c0 + width], preferred_element_type=F32)

    if full:
        uab = jnp.dot(proj(C_U, FNET_WIDTH).astype(BF16), cs_ref[...], preferred_element_type=F32)
        out["ab"][0] = uab[:, :FNET_WIDTH].astype(BF16)
        out["ab"][1] = uab[:, FNET_WIDTH:].astype(BF16)

    out["gqk"][...] = proj(C_GQ, 2 * GLA_QK)
    gv = proj(C_GV, GLA_WIDTH)
    out["gv"][...] = gv.astype(BF16)
    out["gvt"][...] = gv.T.astype(BF16)
    if full:
        out["gg"][...] = proj(C_GG, GLA_WIDTH).astype(BF16)
    pre = jnp.dot(proj(C_GD, 128).astype(BF16), wup_ref[...], preferred_element_type=F32) + bup_ref[...]
    out["la"][...] = (jnp.minimum(pre, 0.0) - jnp.log1p(jnp.exp(-jnp.abs(pre)))) * (1.0 / GLA_GATE_NORM)

    bd = bd_ref[...]
    kv = proj(C_AK, 256)
    k = _head_rms(kv[:, :128], bd[:128, :128], kn_ref[...])
    if rope:
        cos = cos_ref[...]
        sin = sin_ref[...]
        k = k * cos + _swap16(k) * sin
    if full:
        q = _head_rms(proj(C_AQ, ATT_WIDTH), bd, qn_ref[...])
        if rope:
            q = jnp.concatenate(
                [q[:, s:s + 128] * cos + _swap16(q[:, s:s + 128]) * sin for s in range(0, ATT_WIDTH, 128)], axis=1)
        out["q"][...] = (q * (ATT_HDIM ** -0.5 * LOG2E)).astype(BF16)
    kt_ref, vd_ref = out["kt"], out["vd"]
    v = kv[:, 128:]
    lo = lax.broadcasted_iota(jnp.int32, k.shape, 1) < ATT_HDIM
    k_sw = pltpu.roll(k, ATT_HDIM, 1)
    v_sw = pltpu.roll(v, ATT_HDIM, 1)
    kt_ref[0] = jnp.where(lo, k, k_sw).T.astype(BF16)
    kt_ref[1] = jnp.where(lo, k_sw, k).T.astype(BF16)
    vd_ref[0] = jnp.where(lo, v, 1.0).astype(BF16)
    vd_ref[1] = jnp.where(lo, 1.0, v_sw).astype(BF16)
    vd_ref[2] = jnp.where(lo, v_sw, 1.0).astype(BF16)
    vd_ref[3] = jnp.where(lo, 1.0, v).astype(BF16)


def _inproj_call(x, mod_l, mod_row, nw, w_perm, cs, wup, bup, qn, kn, bd, rope_tabs, pending=None, full=True):
    b, n, d = x.shape
    tm = min(1024, n)
    nt = n // tm
    rope = rope_tabs is not None
    if mod_row is None:
        mod_map = lambda bi, i: (bi, 0, 0)
    else:
        mod_map = lambda bi, i: (mod_row, 0, 0)
    const = lambda bi, i: (0, 0)
    in_specs = [
        pl.BlockSpec((None, tm, d), lambda bi, i: (bi, i, 0)),
        pl.BlockSpec((None, 6, d), mod_map),
        pl.BlockSpec((1, d), const),
        pl.BlockSpec((None, d, W_IN_COLS), lambda bi, i: (w_perm[1], 0, 0)),
        pl.BlockSpec((FNET_WIDTH, 2 * FNET_WIDTH), const),
        pl.BlockSpec((128, 2 * GLA_QK), const),
        pl.BlockSpec((1, 2 * GLA_QK), const),
        pl.BlockSpec((1, ATT_WIDTH), const),
        pl.BlockSpec((1, 128), const),
        pl.BlockSpec((ATT_WIDTH, ATT_WIDTH), const),
    ]
    args = [x, mod_l, nw, w_perm[0], cs, wup, bup, qn, kn, bd]
    if rope:
        in_specs += [pl.BlockSpec((tm, 128), lambda bi, i: (i, 0)), pl.BlockSpec((tm, 128), lambda bi, i: (i, 0))]
        args += list(rope_tabs)
    pending_rows = None
    if pending is not None:
        y2, wts3, row0, mod_prev = pending
        nr = tm // 128
        rps = n // 128
        assert 8 % nr == 0 and row0 % nr == 0 and rps % nr == 0
        pending_rows = (row0, rps)
        in_specs += [
            pl.BlockSpec((2, None, tm, d // 2), lambda bi, i: (0, bi, i, 0)),
            pl.BlockSpec((2, 8, 128), lambda bi, i: (0, (row0 + bi * rps + i * nr) // 8, 0)),
            pl.BlockSpec((None, 6, d), mod_map),
        ]
        args += [y2.reshape(2, b, n, d // 2), wts3, mod_prev]
    tok = lambda w: pl.BlockSpec((None, tm, w), lambda bi, i: (bi, i, 0))
    outs = []
    if full:
        outs.append(("ab", jax.ShapeDtypeStruct((2, n, b * FNET_WIDTH), BF16),
                     pl.BlockSpec((2, tm, FNET_WIDTH), lambda bi, i: (0, i, bi))))
    outs += [
        ("gqk", jax.ShapeDtypeStruct((b, n, 2 * GLA_QK), F32), tok(2 * GLA_QK)),
        ("gv", jax.ShapeDtypeStruct((b, n, GLA_WIDTH), BF16), tok(GLA_WIDTH)),
        ("gvt", jax.ShapeDtypeStruct((b, GLA_WIDTH, n), BF16),
         pl.BlockSpec((None, GLA_WIDTH, tm), lambda bi, i: (bi, 0, i))),
    ]
    if full:
        outs.append(("gg", jax.ShapeDtypeStruct((b, n, GLA_WIDTH), BF16), tok(GLA_WIDTH)))
    outs.append(("la", jax.ShapeDtypeStruct((b, n, 2 * GLA_QK), F32), tok(2 * GLA_QK)))
    if full:
        outs.append(("q", jax.ShapeDtypeStruct((b, n, ATT_WIDTH), BF16), tok(ATT_WIDTH)))
    outs += [
        ("kt", jax.ShapeDtypeStruct((b, ATT_KV_HEADS, 128, n), BF16),
         pl.BlockSpec((None, ATT_KV_HEADS, 128, tm), lambda bi, i: (bi, 0, 0, i))),
        ("vd", jax.ShapeDtypeStruct((b, 2 * ATT_KV_HEADS, n, 128), BF16),
         pl.BlockSpec((None, 2 * ATT_KV_HEADS, tm, 128), lambda bi, i: (bi, 0, i, 0))),
    ]
    if pending is not None:
        outs.append(("x", jax.ShapeDtypeStruct((b, n, d), F32), tok(d)))
    names = tuple(o[0] for o in outs)
    res = pl.pallas_call(
        functools.partial(_inproj_kernel, rope=rope, pending_rows=pending_rows, names=names),
        grid=(b, nt),
        in_specs=in_specs,
        out_specs=tuple(o[2] for o in outs),
        out_shape=tuple(o[1] for o in outs),
        compiler_params=_cparams(("parallel", "parallel")),
        name="inproj_rope" if rope else "inproj_ctx",
    )(*args)
    return dict(zip(names, res))


def _seqdft_kernel(t_ref, ab_ref, o_ref):
    y = jnp.dot(t_ref[...], ab_ref[...], preferred_element_type=F32)
    for bb in range(o_ref.shape[0]):
        o_ref[bb] = y[:, bb * FNET_WIDTH:(bb + 1) * FNET_WIDTH].astype(BF16)


def _seqdft_call(table, ab, b):
    n = table.shape[0]
    tm = min(512, n)
    nb = 4 if b % 4 == 0 else (2 if b % 2 == 0 else 1)
    return pl.pallas_call(
        _seqdft_kernel,
        grid=(b // nb, n // tm),
        in_specs=[
            pl.BlockSpec((tm, 2 * n), lambda c, i: (i, 0)),
            pl.BlockSpec((2 * n, nb * FNET_WIDTH), lambda c, i: (0, c)),
        ],
        out_specs=pl.BlockSpec((nb, tm, FNET_WIDTH), lambda c, i: (c, i, 0)),
        out_shape=jax.ShapeDtypeStruct((b, n, FNET_WIDTH), BF16),
        compiler_params=_cparams(("parallel", "parallel")),
        name="seq_dft",
    )(table, ab)


def _gla_dir(qk, v, vt, a, s_in, fwd):
    p = GLA_PAIR
    r = lax.broadcasted_iota(jnp.int32, (p, p), 0)
    c = lax.broadcasted_iota(jnp.int32, (p, p), 1)
    same = (r // GLA_CHUNK) == (c // GLA_CHUNK)
    tri = same & ((c <= r) if fwd else (c >= r))
    row_lo = r < GLA_CHUNK
    rin = r % GLA_CHUNK

    q = qk[:, :GLA_QK] * (GLA_DK ** -0.5)
    k = qk[:, GLA_QK:]
    cum = a
    sh = 1
    while sh < GLA_CHUNK:
        if fwd:
            cum = cum + jnp.where(rin >= sh, pltpu.roll(cum, sh, 0), 0.0)
        else:
            cum = cum + jnp.where(rin < GLA_CHUNK - sh, pltpu.roll(cum, p - sh, 0), 0.0)
        sh *= 2
    if fwd:
        last0, last1 = cum[GLA_CHUNK - 1:GLA_CHUNK, :], cum[p - 1:p, :]
    else:
        last0, last1 = cum[0:1, :], cum[GLA_CHUNK:GLA_CHUNK + 1, :]
    lastb = jnp.where(row_lo, last0, last1)
    qt = q * jnp.exp(cum)
    kt = k * jnp.exp(-cum)
    kd = k * jnp.exp(lastb - cum)

    kt_b = kt.astype(BF16)
    zk = jnp.zeros_like(kt_b)
    ks = jnp.concatenate([jnp.where((c // GLA_DK) == hh, kt_b, zk) for hh in range(GLA_HEADS)], axis=0)
    att = _nt_dot(qt.astype(BF16), ks)
    tri4 = jnp.concatenate([tri] * GLA_HEADS, axis=1)
    att = jnp.where(tri4, att, 0.0).astype(BF16)
    col = lax.broadcasted_iota(jnp.int32, (p, GLA_WIDTH), 1)
    zv = jnp.zeros_like(v)
    vs = jnp.concatenate([jnp.where((col // GLA_DV) == hh, v, zv) for hh in range(GLA_HEADS)], axis=0)
    o_intra = jnp.dot(att, vs, preferred_element_type=F32)

    sr = lax.broadcasted_iota(jnp.int32, (GLA_WIDTH, GLA_QK), 0)
    sc = lax.broadcasted_iota(jnp.int32, (GLA_WIDTH, GLA_QK), 1)
    bdm = (sr // GLA_DV) == (sc // GLA_DK)
    first, second = (0, 1) if fwd else (1, 0)
    lasts = (last0, last1)
    in_chunk = (row_lo, jnp.logical_not(row_lo))
    kd2 = jnp.concatenate([jnp.where(in_chunk[0], kd, 0.0), jnp.where(in_chunk[1], kd, 0.0)], axis=1).astype(BF16)
    kvt2 = jnp.dot(vt, kd2, preferred_element_type=F32)
    kvt = (kvt2[:, :GLA_QK], kvt2[:, GLA_QK:])
    s_a = s_in
    s_b = s_a * jnp.exp(lasts[first]) + jnp.where(bdm, kvt[first], 0.0)
    s_c = s_b * jnp.exp(lasts[second]) + jnp.where(bdm, kvt[second], 0.0)
    q2 = jnp.concatenate([jnp.where(in_chunk[first], qt, 0.0), jnp.where(in_chunk[second], qt, 0.0)], axis=1)
    s2 = jnp.concatenate([s_a, s_b], axis=1).astype(BF16)
    o_inter = _nt_dot(q2.astype(BF16), s2)
    return o_intra + o_inter, s_c


def _gla_kernel(qkf, vf, vtf, laf, qkb, vb, vtb, lab, s0_ref, of_ref, ob_ref, sfin_ref, s_scr):
    i = pl.program_id(1)

    @pl.when(i == 0)
    def _():
        s_scr[...] = s0_ref[...]

    pairs = qkf.shape[1] // GLA_PAIR
    for gi in range(qkf.shape[0]):
        sf = s_scr[gi, 0]
        sb = s_scr[gi, 1]
        for pi in range(pairs):
            rf = slice(pi * GLA_PAIR, (pi + 1) * GLA_PAIR)
            rb = slice((pairs - 1 - pi) * GLA_PAIR, (pairs - pi) * GLA_PAIR)
            o1, sf = _gla_dir(qkf[gi, rf, :], vf[gi, rf, :], vtf[gi, :, rf], laf[gi, rf, :], sf, True)
            o2, sb = _gla_dir(qkb[gi, rb, :], vb[gi, rb, :], vtb[gi, :, rb], lab[gi, rb, :], sb, False)
            of_ref[gi, rf, :] = o1.astype(of_ref.dtype)
            ob_ref[gi, rb, :] = o2.astype(ob_ref.dtype)
        s_scr[gi, 0] = sf
        s_scr[gi, 1] = sb

    @pl.when(i == pl.num_programs(1) - 1)
    def _():
        sfin_ref[...] = s_scr[...]


def _gla_call(gqk, gv, gvt, la, s0):
    b, n, _ = gqk.shape
    p = GLA_PAIR * (2 if n % (2 * GLA_PAIR) == 0 else 1)
    npair = n // p
    gb = 4 if b % 4 == 0 else (2 if b % 2 == 0 else 1)
    fw = lambda bi, i: (bi, i, 0)
    bw = lambda bi, i: (bi, npair - 1 - i, 0)
    in_specs = [
        pl.BlockSpec((gb, p, 2 * GLA_QK), fw),
        pl.BlockSpec((gb, p, GLA_WIDTH), fw),
        pl.BlockSpec((gb, GLA_WIDTH, p), lambda bi, i: (bi, 0, i)),
        pl.BlockSpec((gb, p, GLA_QK), fw),
        pl.BlockSpec((gb, p, 2 * GLA_QK), bw),
        pl.BlockSpec((gb, p, GLA_WIDTH), bw),
        pl.BlockSpec((gb, GLA_WIDTH, p), lambda bi, i: (bi, 0, npair - 1 - i)),
        pl.BlockSpec((gb, p, GLA_QK), lambda bi, i: (bi, npair - 1 - i, 1)),
        pl.BlockSpec((gb, 2, GLA_WIDTH, GLA_QK), lambda bi, i: (bi, 0, 0, 0)),
    ]
    out_specs = (
        pl.BlockSpec((gb, p, GLA_WIDTH), fw),
        pl.BlockSpec((gb, p, GLA_WIDTH), bw),
        pl.BlockSpec((gb, 2, GLA_WIDTH, GLA_QK), lambda bi, i: (bi, 0, 0, 0)),
    )
    out_shape = (
        jax.ShapeDtypeStruct((b, n, GLA_WIDTH), BF16),
        jax.ShapeDtypeStruct((b, n, GLA_WIDTH), BF16),
        jax.ShapeDtypeStruct((b, 2, GLA_WIDTH, GLA_QK), F32),
    )
    return pl.pallas_call(
        _gla_kernel,
        grid=(b // gb, npair),
        in_specs=in_specs,
        out_specs=out_specs,
        out_shape=out_shape,
        scratch_shapes=[pltpu.VMEM((gb, 2, GLA_WIDTH, GLA_QK), F32)],
        compiler_params=_cparams(("parallel", "arbitrary")),
        name="gla_scan",
    )(gqk, gv, gvt, la, gqk, gv, gvt, la, s0)


def _attn_heads(q_ref, parts):
    tq = q_ref.shape[0]
    lane = lax.broadcasted_iota(jnp.int32, (tq, 128), 1)
    lo = lane < ATT_HDIM
    blocks = []
    for j in range(ATT_HEADS // 2):
        q128 = q_ref[:, 128 * j:128 * (j + 1)]
        g = (2 * j) // (ATT_HEADS // ATT_KV_HEADS)
        outs = []
        for half in range(2):
            qm = jnp.where(lo if half == 0 else jnp.logical_not(lo), q128, jnp.zeros_like(q128))
            ss = [jnp.dot(qm, kt_ref[g], preferred_element_type=F32) for kt_ref, _ in parts]
            m = functools.reduce(jnp.maximum, [jnp.max(s, axis=-1, keepdims=True) for s in ss])
            ps = [jnp.exp2(s - m).astype(BF16) for s in ss]
            pv = functools.reduce(
                lambda u, w: u + w,
                [jnp.dot(pp, vd_ref[2 * g + half], preferred_element_type=F32) for pp, (_, vd_ref) in zip(ps, parts)])
            den = pv[:, ATT_HDIM:ATT_HDIM + 1] if half == 0 else pv[:, 0:1]
            outs.append(pv / den)
        blocks.append(jnp.where(lo, outs[0], outs[1]).astype(BF16))
    return jnp.concatenate(blocks, axis=1)


def _attn_outproj_kernel(*refs, nparts):
    q_ref = refs[0]
    parts = [(refs[1 + 2 * i], refs[2 + 2 * i]) for i in range(nparts)]
    (f_ref, of_ref, ob_ref, gg_ref, x_ref, mod_ref, w_ref, gn_ref, bd_ref, nf_ref, wr_ref,
     xn_ref, h2_ref, lg_ref) = refs[1 + 2 * nparts:]
    att = _attn_heads(q_ref, parts)
    o = of_ref[...].astype(F32) + ob_ref[...].astype(F32)
    ms = jnp.dot((o * o).astype(BF16), bd_ref[...], preferred_element_type=F32) * (1.0 / GLA_DV)
    on = o * lax.rsqrt(ms + EPS) * gn_ref[...]
    g = gg_ref[...].astype(F32)
    gl = (on * (g * _sigmoid(g))).astype(BF16)
    ox = (jnp.dot(f_ref[...], w_ref[0:FNET_WIDTH, :], preferred_element_type=F32)
          + jnp.dot(gl, w_ref[FNET_WIDTH:FNET_WIDTH + GLA_WIDTH, :], preferred_element_type=F32)
          + jnp.dot(att, w_ref[FNET_WIDTH + GLA_WIDTH:, :], preferred_element_type=F32))
    xn = x_ref[...] + mod_ref[2:3, :] * ox
    xn_ref[...] = xn
    ms2 = jnp.mean(xn * xn, axis=-1, keepdims=True)
    h2 = xn * lax.rsqrt(ms2 + EPS) * nf_ref[...] * (1.0 + mod_ref[4:5, :]) + mod_ref[3:4, :]
    h2_ref[...] = _pack_bf16_pairs(h2)
    lg2 = _nt_dot(wr_ref[...], h2.astype(BF16))
    lg_ref[...] = lg2[:N_EXPERTS, :] + lg2[N_EXPERTS:, :]


def _attn_outproj_call(q, kv_parts, f, of, ob, gg, x, mod_l, mod_row, w_out, gn, bd, nf, wr):
    b, n, d = x.shape
    tm = min(1024, n)
    nt = n // tm
    if mod_row is None:
        mod_map = lambda bi, i: (bi, 0, 0)
    else:
        mod_map = lambda bi, i: (mod_row, 0, 0)
    const = lambda bi, i: (0, 0)
    tok = lambda w: pl.BlockSpec((None, tm, w), lambda bi, i: (bi, i, 0))
    in_specs = [tok(ATT_WIDTH)]
    args = [q]
    for kt, vd in kv_parts:
        m = kt.shape[-1]
        in_specs.append(pl.BlockSpec((None, ATT_KV_HEADS, 128, m), lambda bi, i: (bi, 0, 0, 0)))
        in_specs.append(pl.BlockSpec((None, 2 * ATT_KV_HEADS, m, 128), lambda bi, i: (bi, 0, 0, 0)))
        args += [kt, vd]
    in_specs += [
        tok(FNET_WIDTH), tok(GLA_WIDTH), tok(GLA_WIDTH), tok(GLA_WIDTH), tok(d),
        pl.BlockSpec((None, 6, d), mod_map),
        pl.BlockSpec((None, d, d), lambda bi, i: (w_out[1], 0, 0)),
        pl.BlockSpec((1, GLA_WIDTH), const),
        pl.BlockSpec((GLA_WIDTH, GLA_WIDTH), const),
        pl.BlockSpec((1, d), const),
        pl.BlockSpec((2 * N_EXPERTS, d), const),
    ]
    args += [f, of, ob, gg, x, mod_l, w_out[0], gn, bd, nf, wr]
    out_specs = (
        tok(d), tok(d // 2),
        pl.BlockSpec((N_EXPERTS, tm), lambda bi, i: (0, bi * nt + i)),
    )
    out_shape = (
        jax.ShapeDtypeStruct((b, n, d), F32),
        jax.ShapeDtypeStruct((b, n, d // 2), jnp.int32),
        jax.ShapeDtypeStruct((N_EXPERTS, b * n), F32),
    )
    return pl.pallas_call(
        functools.partial(_attn_outproj_kernel, nparts=len(kv_parts)),
        grid=(b, nt),
        in_specs=in_specs,
        out_specs=out_specs,
        out_shape=out_shape,
        compiler_params=_cparams(("parallel", "parallel")),
        name="attn_outproj",
    )(*args)


MOE_TILE_SLOTS = 256
MOE_TILE_ROWS = 8

def _route_kernel(b_ref, lg_ref, pos_ref, wt_ref, te_ref, nv_ref, *, tm):
    r = lg_ref.shape[1]
    s = [_sigmoid(lg_ref[e]) for e in range(N_EXPERTS)]
    sel = [s[e] + b_ref[e] for e in range(N_EXPERTS)]
    grp = []
    for g in range(N_GROUPS):
        a, b, c, d = sel[4 * g:4 * g + 4]
        hi1, lo1 = jnp.maximum(a, b), jnp.minimum(a, b)
        hi2, lo2 = jnp.maximum(c, d), jnp.minimum(c, d)
        m1 = jnp.maximum(hi1, hi2)
        m2 = jnp.maximum(jnp.minimum(hi1, hi2), jnp.maximum(lo1, lo2))
        grp.append(m1 + m2)
    one = jnp.ones_like(s[0])
    zero = jnp.zeros_like(s[0])
    msk = []
    for g in range(N_GROUPS):
        isg = one
        for g2 in range(N_GROUPS):
            if g2 < g:
                isg = isg * jnp.where(grp[g] > grp[g2], one, zero)
            elif g2 > g:
                isg = isg * jnp.where(grp[g] >= grp[g2], one, zero)
        for li in range(EXPERTS_PER_GROUP):
            e = 4 * g + li
            rank = zero
            for lj in range(EXPERTS_PER_GROUP):
                ej = 4 * g + lj
                if lj < li:
                    rank = rank + jnp.where(sel[ej] >= sel[e], one, zero)
                elif lj > li:
                    rank = rank + jnp.where(sel[ej] > sel[e], one, zero)
            msk.append(jnp.where(rank < 2.0, isg, zero))
    den = functools.reduce(lambda u, v: u + v, [msk[e] * s[e] for e in range(N_EXPERTS)])

    li_ = lax.broadcasted_iota(jnp.int32, (128, 128), 0)
    lj_ = lax.broadcasted_iota(jnp.int32, (128, 128), 1)
    upper = jnp.where(li_ < lj_, 1.0, 0.0).astype(BF16)
    ri_ = lax.broadcasted_iota(jnp.int32, (r, r), 0)
    rj_ = lax.broadcasted_iota(jnp.int32, (r, r), 1)
    lower = jnp.where(rj_ < ri_, 1.0, 0.0).astype(BF16)
    tiles = (1, te_ref.shape[1])
    tile_start = lax.broadcasted_iota(jnp.int32, tiles, 1).astype(F32) * float(tm)
    te = jnp.zeros(tiles, F32)
    seg = jnp.zeros(tiles, F32)
    nonempty = []
    off = jnp.zeros((1, 1), F32)
    seen = zero
    pos = [zero, zero]
    wts = [zero, zero]
    for e in range(N_EXPERTS):
        mb = msk[e].astype(BF16)
        lane_pre = jnp.dot(mb, upper, preferred_element_type=F32)
        row_pre = jnp.sum(jnp.dot(lower, mb, preferred_element_type=F32), axis=1, keepdims=True)
        cnt = jnp.sum(jnp.sum(msk[e], axis=1, keepdims=True), axis=0, keepdims=True)
        p_e = off + row_pre + lane_pre
        g_e = s[e] / den
        for kk in range(2):
            hit = msk[e] * jnp.where(seen == float(kk), one, zero)
            pos[kk] = pos[kk] + hit * p_e
            wts[kk] = wts[kk] + hit * g_e
        seen = seen + msk[e]
        off = off + jnp.floor((cnt + float(tm - 1)) * (1.0 / tm)) * float(tm)
        passed = jnp.where(tile_start >= off, 1.0, 0.0)
        te = te + passed
        nonempty.append(jnp.where(cnt > 0.0, 1.0, 0.0))
        seg = seg + nonempty[e] * passed
    for kk in range(2):
        pos_ref[kk] = pos[kk].astype(jnp.int32)
        wt_ref[kk] = wts[kk]
    te = jnp.minimum(te, float(N_EXPERTS - 1))
    nxt = jnp.full(tiles, -1.0, F32)
    for e in reversed(range(N_EXPERTS)):
        nxt = jnp.where(jnp.logical_and(nonempty[e] > 0.0, te < float(e)), float(e), nxt)
    te_ref[0:1, :] = te.astype(jnp.int32)
    te_ref[1:2, :] = seg.astype(jnp.int32)
    te_ref[2:3, :] = nxt.astype(jnp.int32)
    te_ref[3:, :] = jnp.zeros((te_ref.shape[0] - 3, te_ref.shape[1]), jnp.int32)
    nv_ref[...] = jnp.broadcast_to(off * (1.0 / tm), nv_ref.shape).astype(jnp.int32)


def _route_call(lgt, b_router, tm):
    n_tok = lgt.shape[1]
    r = n_tok // 128
    assert r * 128 == n_tok and r % 8 == 0 and 2 * n_tok // tm + N_EXPERTS <= MOE_TILE_SLOTS
    lg3 = lgt.reshape(N_EXPERTS, r, 128)
    full3 = lambda k: pl.BlockSpec((k, r, 128), lambda: (0, 0, 0))
    pos, wts, te, nv = pl.pallas_call(
        functools.partial(_route_kernel, tm=tm),
        in_specs=[pl.BlockSpec(memory_space=pltpu.SMEM), full3(N_EXPERTS)],
        out_specs=(full3(2), full3(2), pl.BlockSpec((MOE_TILE_ROWS, MOE_TILE_SLOTS), lambda: (0, 0)), pl.BlockSpec((1, 128), lambda: (0, 0))),
        out_shape=(
            jax.ShapeDtypeStruct((2, r, 128), jnp.int32),
            jax.ShapeDtypeStruct((2, r, 128), F32),
            jax.ShapeDtypeStruct((MOE_TILE_ROWS, MOE_TILE_SLOTS), jnp.int32),
            jax.ShapeDtypeStruct((1, 128), jnp.int32),
        ),
        compiler_params=pltpu.CompilerParams(vmem_limit_bytes=VMEM_LIMIT),
        name="route",
    )(b_router, lg3)
    return pos.reshape(2, n_tok), wts, te[:3].reshape(3 * MOE_TILE_SLOTS), nv[0, :1]


SC_CORES = 2
SC_SUBCORES = 16
SC_WORKERS = SC_CORES * SC_SUBCORES
SC_CHUNK = 32


def _sc_mesh():
    return plsc.VectorSubcoreMesh(core_axis_name="c", subcore_axis_name="s",
                                  num_cores=SC_CORES, num_subcores=SC_SUBCORES)


def _sc_steps(n_rows):
    per_w = n_rows // SC_WORKERS
    steps = per_w // SC_CHUNK
    assert per_w * SC_WORKERS == n_rows and steps * SC_CHUNK == per_w and steps % 2 == 0, n_rows
    return per_w, steps


def _sc_gather_rows(table, idx):
    p = idx.shape[0]
    d = table.shape[1]
    per_w, steps = _sc_steps(p)
    idx3 = idx.reshape(SC_WORKERS, steps, SC_CHUNK)

    @functools.partial(
        pl.kernel, mesh=_sc_mesh(),
        out_type=jax.ShapeDtypeStruct((p, d), table.dtype),
        scratch_types=[
            pltpu.VMEM((steps, SC_CHUNK), jnp.int32),
            pltpu.VMEM((SC_CHUNK, d), table.dtype),
            pltpu.VMEM((SC_CHUNK, d), table.dtype),
            pltpu.SemaphoreType.DMA, pltpu.SemaphoreType.DMA,
            pltpu.SemaphoreType.DMA, pltpu.SemaphoreType.DMA,
        ],
        name="sc_gather_rows",
    )
    def k(table_hbm, idx_hbm, out_hbm, idx_v, buf0, buf1, g0, g1, w0, w1):
        wid = lax.axis_index("s") * SC_CORES + lax.axis_index("c")
        base = wid * per_w
        pltpu.sync_copy(idx_hbm.at[wid], idx_v)

        def gather(s, buf, sem):
            return pltpu.make_async_copy(table_hbm.at[idx_v.at[s]], buf, sem)

        def write(s, buf, sem):
            return pltpu.make_async_copy(buf, out_hbm.at[pl.ds(base + s * SC_CHUNK, SC_CHUNK)], sem)

        gather(0, buf0, g0).start()

        @pl.loop(0, steps, step=2)
        def _(s):
            gather(s + 1, buf1, g1).start()
            gather(s, buf0, g0).wait()
            write(s, buf0, w0).start()
            write(s, buf0, w0).wait()

            @pl.when(s + 2 < steps)
            def _():
                gather(s + 2, buf0, g0).start()

            gather(s + 1, buf1, g1).wait()
            write(s + 1, buf1, w1).start()
            write(s + 1, buf1, w1).wait()

    return k(table, idx3)


def _sc_dispatch(srcs, poss, p_rows):
    d = srcs[0].shape[1]
    dt = srcs[0].dtype
    plans = [_sc_steps(src.shape[0]) for src in srcs]
    idxs = [pos.reshape(2, SC_WORKERS, st, SC_CHUNK) for pos, (_, st) in zip(poss, plans)]
    nseg = len(srcs)
    scratch = [pltpu.VMEM((2, st, SC_CHUNK), jnp.int32) for _, st in plans]
    scratch += [pltpu.VMEM((SC_CHUNK, d), dt), pltpu.VMEM((SC_CHUNK, d), dt)]
    scratch += [pltpu.SemaphoreType.DMA] * 6

    @functools.partial(
        pl.kernel, mesh=_sc_mesh(),
        out_type=jax.ShapeDtypeStruct((p_rows, d), dt),
        scratch_types=scratch,
        name="sc_dispatch",
    )
    def k(*refs):
        src_hbm = refs[:nseg]
        idx_hbm = refs[nseg:2 * nseg]
        out_hbm = refs[2 * nseg]
        idx_v = refs[2 * nseg + 1:3 * nseg + 1]
        buf0, buf1, r0, r1, a0, a1, b0, b1 = refs[3 * nseg + 1:]
        wid = lax.axis_index("s") * SC_CORES + lax.axis_index("c")
        for seg in range(nseg):
            per_w, steps = plans[seg]
            base = wid * per_w
            for kk in range(2):
                pltpu.sync_copy(idx_hbm[seg].at[kk, wid], idx_v[seg].at[kk])

            def read(s, buf, sem, seg=seg, base=base):
                return pltpu.make_async_copy(src_hbm[seg].at[pl.ds(base + s * SC_CHUNK, SC_CHUNK)], buf, sem)

            def scat(kk, s, buf, sem, seg=seg):
                return pltpu.make_async_copy(buf, out_hbm.at[idx_v[seg].at[kk, s]], sem)

            read(0, buf0, r0).start()

            @pl.loop(0, steps, step=2)
            def _(s, read=read, scat=scat, steps=steps):
                read(s + 1, buf1, r1).start()
                read(s, buf0, r0).wait()
                scat(0, s, buf0, a0).start()
                scat(1, s, buf0, b0).start()
                scat(0, s, buf0, a0).wait()
                scat(1, s, buf0, b0).wait()

                @pl.when(s + 2 < steps)
                def _():
                    read(s + 2, buf0, r0).start()

                read(s + 1, buf1, r1).wait()
                scat(0, s + 1, buf1, a1).start()
                scat(1, s + 1, buf1, b1).start()
                scat(0, s + 1, buf1, a1).wait()
                scat(1, s + 1, buf1, b1).wait()

    return k(*srcs, *idxs)


MOE_TM = 512


def _experts_kernel(te_ref, nv_ref, xs_ref, wg_hbm, wu_hbm, wd_hbm, ys_ref, wg_v, wu_v, wd_v, sem, *, layer):
    i = pl.program_id(0)
    valid = i < nv_ref[0]
    e = te_ref[i]
    first = jnp.logical_or(i == 0, e != te_ref[jnp.maximum(i - 1, 0)])
    slot = lax.rem(te_ref[MOE_TILE_SLOTS + i], 2)
    nxt = te_ref[2 * MOE_TILE_SLOTS + i]

    def weight_copies(expert, s):
        return (pltpu.make_async_copy(wg_hbm.at[layer, expert], wg_v.at[s], sem.at[s, 0]),
                pltpu.make_async_copy(wu_hbm.at[layer, expert], wu_v.at[s], sem.at[s, 1]),
                pltpu.make_async_copy(wd_hbm.at[layer, expert], wd_v.at[s], sem.at[s, 2]))

    @pl.when(jnp.logical_and(valid, i == 0))
    def _():
        for cp in weight_copies(e, slot):
            cp.start()

    @pl.when(jnp.logical_and(valid, first))
    def _():
        for cp in weight_copies(e, slot):
            cp.wait()

        @pl.when(nxt >= 0)
        def _():
            for cp in weight_copies(nxt, 1 - slot):
                cp.start()

    @pl.when(valid)
    def _():
        h = _unpack_bf16_pairs(xs_ref[...]).astype(BF16)
        half = D_EXPERT // 2
        y = None
        for j in range(2):
            sl = slice(j * half, (j + 1) * half)
            a = jnp.dot(h, wg_v[slot, :, sl].astype(BF16), preferred_element_type=F32)
            u = jnp.dot(h, wu_v[slot, :, sl].astype(BF16), preferred_element_type=F32)
            t = ((a * _sigmoid(a)) * u).astype(BF16)
            yj = jnp.dot(t, wd_v[slot, sl, :].astype(BF16), preferred_element_type=F32)
            y = yj if y is None else y + yj
        ys_ref[...] = _pack_bf16_pairs(y)


def _experts_call(xs, te, nv, wg, wu, wd, layer):
    p_rows, dh = xs.shape
    d = 2 * dh
    tm = MOE_TM
    nt = p_rows // tm
    assert nt <= MOE_TILE_SLOTS and te.shape == (3 * MOE_TILE_SLOTS,)
    row = lambda i, te_r, nv_r: (jnp.minimum(i, nv_r[0] - 1), 0)
    grid_spec = pltpu.PrefetchScalarGridSpec(
        num_scalar_prefetch=2,
        grid=(nt,),
        in_specs=[
            pl.BlockSpec((tm, dh), row),
            pl.BlockSpec(memory_space=pl.ANY),
            pl.BlockSpec(memory_space=pl.ANY),
            pl.BlockSpec(memory_space=pl.ANY),
        ],
        out_specs=pl.BlockSpec((tm, dh), row),
        scratch_shapes=[
            pltpu.VMEM((2, d, D_EXPERT), F32),
            pltpu.VMEM((2, d, D_EXPERT), F32),
            pltpu.VMEM((2, D_EXPERT, d), F32),
            pltpu.SemaphoreType.DMA((2, 3)),
        ],
    )
    return pl.pallas_call(
        functools.partial(_experts_kernel, layer=layer),
        grid_spec=grid_spec,
        out_shape=jax.ShapeDtypeStruct((p_rows, dh), jnp.int32),
        compiler_params=_cparams(("arbitrary",)),
        name="moe_experts",
    )(te, nv, xs, wg, wu, wd)


COMBINE_TM = 1024
FINAL_PARTS = 2


def _combine_kernel(y_ref, wt_ref, x_ref, mod_ref, fn_ref, o_ref, *, final):
    xo = x_ref[...] + mod_ref[5:6, :] * _moe_mix(y_ref, wt_ref[0], wt_ref[1])
    if final:
        ms = jnp.mean(xo * xo, axis=-1, keepdims=True)
        xo = xo * lax.rsqrt(ms + EPS) * fn_ref[...]
    o_ref[...] = xo


def _combine_kernel_into(y_ref, wt_ref, x_ref, mod_ref, fn_ref, prev_ref, o_ref, *, final):
    del prev_ref
    _combine_kernel(y_ref, wt_ref, x_ref, mod_ref, fn_ref, o_ref, final=final)


def _combine_call(y2, wts3, row0, x, mod_l, mod_row, fn, final, tok0=0, prev=None):
    b, n, d = x.shape
    n_tok = b * n
    part = y2.shape[1]
    tm = COMBINE_TM
    assert part % tm == 0 and tok0 % tm == 0 and row0 % 8 == 0 and (mod_row is not None or n % tm == 0)
    t0 = tok0 // tm
    if mod_row is None:
        mod_map = lambda i: (((t0 + i) * tm) // n, 0, 0)
    else:
        mod_map = lambda i: (mod_row, 0, 0)
    in_specs = [
        pl.BlockSpec((2, tm, d // 2), lambda i: (0, i, 0)),
        pl.BlockSpec((2, tm // 128, 128), lambda i: (0, row0 // 8 + t0 + i, 0)),
        pl.BlockSpec((tm, d), lambda i: (t0 + i, 0)),
        pl.BlockSpec((None, 6, d), mod_map),
        pl.BlockSpec((1, d), lambda i: (0, 0)),
    ]
    args = [y2, wts3, x.reshape(n_tok, d), mod_l, fn]
    body, aliases = _combine_kernel, {}
    if prev is not None:
        in_specs.append(pl.BlockSpec(memory_space=pl.ANY))
        args.append(prev.reshape(n_tok, d))
        body, aliases = _combine_kernel_into, {len(args) - 1: 0}
    out = pl.pallas_call(
        functools.partial(body, final=final),
        grid=(part // tm,),
        in_specs=in_specs,
        out_specs=pl.BlockSpec((tm, d), lambda i: (t0 + i, 0)),
        out_shape=jax.ShapeDtypeStruct((n_tok, d), F32),
        input_output_aliases=aliases,
        compiler_params=_cparams(("parallel",)),
        name="moe_combine",
    )(*args)
    return out.reshape(b, n, d)


def _moe_sparse(h_list, lg_list, x_list, mod_l, mod_rows, b_router, wg, wu, wd, layer, fn, final):
    d = h_list[0].shape[-1]
    sizes = [h.shape[0] * h.shape[1] for h in h_list]
    n_tok = sum(sizes)
    lgt = lg_list[0] if len(lg_list) == 1 else jnp.concatenate(lg_list, axis=1)
    pos, wts, te, nv = _route_call(lgt, b_router, MOE_TM)
    p_rows = 2 * n_tok + N_EXPERTS * MOE_TM
    offs = np.cumsum([0] + sizes)
    poss = [pos[:, offs[i]:offs[i + 1]] for i in range(len(sizes))]
    xs = _sc_dispatch([h.reshape(-1, d) for h in h_list], poss, p_rows)
    ys = _experts_call(xs, te, nv, wg, wu, wd, layer)
    outs = []
    for i, x in enumerate(x_list):
        row0 = int(offs[i]) // 128
        if not final:
            y2 = _sc_gather_rows(ys, poss[i].reshape(-1)).reshape(2, sizes[i], d)
            outs.append((y2, wts, row0, mod_l))
            continue
        nparts = FINAL_PARTS if sizes[i] % (FINAL_PARTS * SC_WORKERS * SC_CHUNK * 2) == 0 else 1
        part = sizes[i] // nparts
        y2s = [_sc_gather_rows(ys, poss[i][:, k * part:(k + 1) * part].reshape(-1)).reshape(2, part, d)
               for k in range(nparts)]
        out = None
        for k in range(nparts):
            out = _combine_call(y2s[k], wts, row0, x, mod_l, mod_rows[i], fn, True, tok0=k * part, prev=out)
        outs.append(out)
    return outs


def _winprep_kernel(wt_ref, o_ref):
    gd0 = C_AQ
    gdw = 2 * GLA_GATE_RANK
    tail = W_IN_REF_COLS - gd0 - gdw
    o_ref[:, 0:gd0] = wt_ref[0:gd0, :].T.astype(BF16)
    o_ref[:, gd0:gd0 + tail] = wt_ref[gd0 + gdw:W_IN_REF_COLS, :].T.astype(BF16)
    gd = wt_ref[gd0:gd0 + 128, :].T
    lane = lax.broadcasted_iota(jnp.int32, gd.shape, 1)
    o_ref[:, C_GD:] = jnp.where(lane < gdw, gd, 0.0).astype(BF16)


def _winprep_call(w_in):
    depth, d, cols = w_in.shape
    assert cols == W_IN_REF_COLS and C_GD == cols - 2 * GLA_GATE_RANK and W_IN_COLS - C_GD == 128
    return pl.pallas_call(
        _winprep_kernel,
        grid=(depth,),
        in_specs=[pl.BlockSpec((None, cols, d), lambda l: (l, 0, 0))],
        out_specs=pl.BlockSpec((None, d, W_IN_COLS), lambda l: (l, 0, 0)),
        out_shape=jax.ShapeDtypeStruct((depth, d, W_IN_COLS), BF16),
        compiler_params=_cparams(("parallel",)),
        name="w_in_prep",
    )(jnp.swapaxes(w_in, 1, 2))


def _gate_up_weights(w_up, b_up):
    z = jnp.zeros((GLA_GATE_RANK, GLA_QK), w_up.dtype)
    top = jnp.concatenate([w_up[0], z], axis=1)
    mid = jnp.concatenate([z, w_up[1]], axis=1)
    pad = jnp.zeros((128 - 2 * GLA_GATE_RANK, 2 * GLA_QK), w_up.dtype)
    return jnp.concatenate([top, mid, pad], axis=0).astype(BF16), b_up.reshape(1, 2 * GLA_QK)


def kernel(x, c, ctx, c_ctx, w_ada, b_ada, norm_mix, norm_ffn, w_in, w_gla_gate_up, b_gla_gate, gla_norm, q_norm,
           k_norm, w_out, w_router, b_router, w_exp_gate, w_exp_up, w_exp_down, final_norm):
    b, n, d = x.shape
    m = ctx.shape[1]
    depth = w_ada.shape[0]
    assert d == D_MODEL and n % GLA_PAIR == 0 and m % GLA_PAIR == 0 and n % GRID_W == 0

    rows = ((b + 1 + 7) // 8) * 8
    cv = jnp.concatenate([c, c_ctx[None, :], jnp.zeros((rows - b - 1, d), F32)], axis=0)
    mod = _ada_call(cv, w_ada, b_ada).reshape(depth, rows, 6, d)

    cs = jnp.asarray(_channel_dft_table()).astype(BF16)
    tab_x = jnp.asarray(_seq_dft_table(n)).astype(BF16)
    tab_c = jnp.asarray(_seq_dft_table(m)).astype(BF16)
    rope_tabs = tuple(jnp.asarray(t) for t in _rope_tables(n))
    bd512 = jnp.asarray(_blockdiag_ones(ATT_WIDTH, ATT_HDIM)).astype(BF16)
    bd256 = jnp.asarray(_blockdiag_ones(GLA_WIDTH, GLA_DV)).astype(BF16)
    wr_t = w_router.T
    wr_hi = wr_t.astype(BF16)
    wrh = jnp.concatenate([wr_hi, (wr_t - wr_hi.astype(F32)).astype(BF16)], axis=0)
    fn = final_norm.reshape(1, d)

    w_in_perm = _winprep_call(w_in)
    w_out_bf = w_out.astype(BF16)

    xc = ctx
    pend_x = pend_c = None
    for l in range(depth):
        ctx_out = l < depth - 1
        mod_l = mod[l]
        w_perm = (w_in_perm, l)
        wup, bup = _gate_up_weights(w_gla_gate_up[l], b_gla_gate[l])
        nw = norm_mix[l].reshape(1, d)
        nf = norm_ffn[l].reshape(1, d)
        qn = jnp.tile(q_norm[l], ATT_HEADS).reshape(1, ATT_WIDTH)
        kn = jnp.tile(k_norm[l], ATT_KV_HEADS).reshape(1, 128)
        gn = jnp.tile(gla_norm[l], GLA_HEADS).reshape(1, GLA_WIDTH)
        wo = (w_out_bf, l)

        pc = _inproj_call(xc, mod_l, b, nw, w_perm, cs, wup, bup, qn, kn, bd512, None, pend_c, full=ctx_out)
        px = _inproj_call(x, mod_l, None, nw, w_perm, cs, wup, bup, qn, kn, bd512, rope_tabs, pend_x)
        if pend_x is not None:
            x, xc = px["x"], pc["x"]

        s_zero = jnp.zeros((b, 2, GLA_WIDTH, GLA_QK), F32)
        of_c, ob_c, s_fin = _gla_call(pc["gqk"], pc["gv"], pc["gvt"], pc["la"], s_zero)
        of_x, ob_x, _ = _gla_call(px["gqk"], px["gv"], px["gvt"], px["la"], s_fin)

        kv_c = (pc["kt"], pc["vd"])
        f_x = _seqdft_call(tab_x, px["ab"].reshape(2 * n, b * FNET_WIDTH), b)
        x, h2_x, lg_x = _attn_outproj_call(px["q"], [kv_c, (px["kt"], px["vd"])], f_x, of_x, ob_x, px["gg"], x,
                                           mod_l, None, wo, gn, bd256, nf, wrh)

        if ctx_out:
            f_c = _seqdft_call(tab_c, pc["ab"].reshape(2 * m, b * FNET_WIDTH), b)
            xc, h2_c, lg_c = _attn_outproj_call(pc["q"], [kv_c], f_c, of_c, ob_c, pc["gg"], xc,
                                                mod_l, b, wo, gn, bd256, nf, wrh)

        final = l == depth - 1
        wexp = (w_exp_gate, w_exp_up, w_exp_down, l)
        if ctx_out:
            res = _moe_sparse([h2_x, h2_c], [lg_x, lg_c], [x, xc], mod_l, [None, b], b_router, *wexp, fn, final)
        else:
            res = _moe_sparse([h2_x], [lg_x], [x], mod_l, [None], b_router, *wexp, fn, final)
        if final:
            x = res[0]
        else:
            pend_x, pend_c = res
    return x
```

```python
import functools

import numpy as np
import jax
import jax.numpy as jnp
from jax import lax
from jax.experimental import pallas as pl
from jax.experimental.pallas import tpu as pltpu
from jax.experimental.pallas import tpu_sc as plsc

F32 = jnp.float32
BF16 = jnp.bfloat16

D_MODEL = 1024
GRID_W = 64
EPS = 1e-6
LOG2E = 1.4426950408889634

FNET_WIDTH = 256
FNET_GROUPS = 4
FNET_GDIM = 64

GLA_HEADS = 4
GLA_DV = 64
GLA_DK = 32
GLA_WIDTH = 256
GLA_QK = 128
GLA_GATE_RANK = 16
GLA_GATE_NORM = 16.0
GLA_CHUNK = 64
GLA_PAIR = 2 * GLA_CHUNK

ATT_HEADS = 8
ATT_KV_HEADS = 2
ATT_HDIM = 64
ATT_WIDTH = 512
ROPE_FREQS = 16
ROPE_THETA = 10000.0

N_EXPERTS = 16
N_GROUPS = 4
EXPERTS_PER_GROUP = 4
D_EXPERT = 512

C_U = 0
C_GQ = 256
C_GK = 384
C_GV = 512
C_GG = 768
C_AQ = 1024
C_AK = 1536
C_GD = 1792
W_IN_COLS = 1920
W_IN_REF_COLS = 1824

VMEM_LIMIT = 56 * 1024 * 1024


def _cparams(sem):
    return pltpu.CompilerParams(dimension_semantics=sem, vmem_limit_bytes=VMEM_LIMIT)


def _sigmoid(x):
    return 1.0 / (1.0 + jnp.exp(-x))


def _pack_bf16_pairs(x):
    blocks = []
    for t in range(x.shape[1] // 256):
        lo = lax.bitcast_convert_type(x[:, 256 * t:256 * t + 128].astype(BF16).astype(F32), jnp.uint32)
        hi = lax.bitcast_convert_type(x[:, 256 * t + 128:256 * t + 256].astype(BF16).astype(F32), jnp.uint32)
        blocks.append((lo >> 16) | (hi & jnp.uint32(0xFFFF0000)))
    return lax.bitcast_convert_type(jnp.concatenate(blocks, axis=1), jnp.int32)


def _unpack_bf16_pairs(p):
    u = lax.bitcast_convert_type(p, jnp.uint32)
    blocks = []
    for t in range(p.shape[1] // 128):
        word = u[:, 128 * t:128 * (t + 1)]
        blocks += [lax.bitcast_convert_type(word << 16, F32),
                   lax.bitcast_convert_type(word & jnp.uint32(0xFFFF0000), F32)]
    return jnp.concatenate(blocks, axis=1)


def _token_columns(w):
    nr = w.shape[0]
    tm = nr * 128
    lane = lax.broadcasted_iota(jnp.int32, (tm, 128), 1)
    row = lax.broadcasted_iota(jnp.int32, (tm, 128), 0)
    wb = jnp.concatenate([jnp.broadcast_to(w[r:r + 1, :], (128, 128)) for r in range(nr)], axis=0)
    return jnp.sum(jnp.where(lane == (row % 128), wb, 0.0), axis=1, keepdims=True)


def _moe_mix(y_ref, w0, w1):
    return (_token_columns(w0) * _unpack_bf16_pairs(y_ref[0])
            + _token_columns(w1) * _unpack_bf16_pairs(y_ref[1]))


def _nt_dot(a, b):
    return lax.dot_general(a, b, (((1,), (1,)), ((), ())), preferred_element_type=F32)


@functools.lru_cache(maxsize=None)
def _channel_dft_table():
    j = np.arange(FNET_GDIM)
    ang = 2.0 * np.pi * ((j[:, None] * j[None, :]) % FNET_GDIM) / FNET_GDIM
    c = np.cos(ang) / np.sqrt(FNET_GDIM)
    s = np.sin(ang) / np.sqrt(FNET_GDIM)
    out = np.zeros((FNET_WIDTH, 2 * FNET_WIDTH), np.float64)
    for g in range(FNET_GROUPS):
        sl = slice(g * FNET_GDIM, (g + 1) * FNET_GDIM)
        out[sl, sl] = c
        out[sl, FNET_WIDTH + g * FNET_GDIM:FNET_WIDTH + (g + 1) * FNET_GDIM] = s
    return out.astype(np.float32)


@functools.lru_cache(maxsize=None)
def _seq_dft_table(n):
    j = np.arange(n, dtype=np.int64)
    ang = 2.0 * np.pi * ((j[:, None] * j[None, :]) % n) / n
    return np.concatenate([np.cos(ang), -np.sin(ang)], axis=1).astype(np.float32) / np.float32(np.sqrt(n))


@functools.lru_cache(maxsize=None)
def _rope_tables(n):
    rows = n // GRID_W
    row = np.repeat(np.arange(rows), GRID_W).astype(np.float64)
    col = np.tile(np.arange(GRID_W), rows).astype(np.float64)
    inv = ROPE_THETA ** (-np.arange(ROPE_FREQS, dtype=np.float64) * 2.0 / (2 * ROPE_FREQS))
    ar = row[:, None] * inv[None, :]
    ac = col[:, None] * inv[None, :]
    cos = np.concatenate([np.cos(ar), np.cos(ar), np.cos(ac), np.cos(ac)], axis=1)
    sin = np.concatenate([-np.sin(ar), np.sin(ar), -np.sin(ac), np.sin(ac)], axis=1)
    return (np.tile(cos, (1, 2)).astype(np.float32), np.tile(sin, (1, 2)).astype(np.float32))


@functools.lru_cache(maxsize=None)
def _blockdiag_ones(width, blk):
    i = np.arange(width)
    return (i[:, None] // blk == i[None, :] // blk).astype(np.float32)


def _ada_kernel(cv_ref, w_ref, b_ref, o_ref):
    cv = cv_ref[...]
    a = (cv * _sigmoid(cv)).astype(BF16)
    o_ref[...] = jnp.dot(a, w_ref[...].astype(BF16), preferred_element_type=F32) + b_ref[...]


def _ada_call(cv, w_ada, b_ada):
    depth, d, d6 = w_ada.shape
    tn = 1536
    rows = cv.shape[0]
    return pl.pallas_call(
        _ada_kernel,
        grid=(depth, d6 // tn),
        in_specs=[
            pl.BlockSpec((rows, d), lambda l, j: (0, 0)),
            pl.BlockSpec((None, d, tn), lambda l, j: (l, 0, j)),
            pl.BlockSpec((None, 1, tn), lambda l, j: (l, 0, j)),
        ],
        out_specs=pl.BlockSpec((None, rows, tn), lambda l, j: (l, 0, j)),
        out_shape=jax.ShapeDtypeStruct((depth, rows, d6), F32),
        compiler_params=_cparams(("parallel", "parallel")),
        name="ada_mod",
    )(cv, w_ada, b_ada.reshape(depth, 1, d6))


def _swap16(x):
    lane = lax.broadcasted_iota(jnp.int32, x.shape, 1)
    first = (lane % 32) < 16
    return jnp.where(first, pltpu.roll(x, 112, 1), pltpu.roll(x, 16, 1))


def _head_rms(x, bd, w):
    ms = jnp.dot((x * x).astype(BF16), bd, preferred_element_type=F32) * (1.0 / ATT_HDIM)
    return x * lax.rsqrt(ms + EPS) * w


def _inproj_kernel(*refs, rope, pending_rows, names):
    refs = list(refs)
    x_ref, mod_ref, nw_ref, w_ref, cs_ref, wup_ref, bup_ref, qn_ref, kn_ref, bd_ref = refs[:10]
    del refs[:10]
    if rope:
        cos_ref, sin_ref = refs[:2]
        del refs[:2]
    if pending_rows is not None:
        y_ref, wt_ref, modp_ref = refs[:3]
        del refs[:3]
    out = dict(zip(names, refs))
    full = "q" in out
    x = x_ref[...]
    if pending_rows is not None:
        xnew_ref = out["x"]
        row0, rows_per_sample = pending_rows
        nr = x.shape[0] // 128
        r = row0 + pl.program_id(0) * rows_per_sample + pl.program_id(1) * nr
        sub = lax.rem(r, 8)
        w = [wt_ref[kk, 0:nr, :] for kk in range(2)]
        for blk in range(1, 8 // nr):
            w = [jnp.where(sub == blk * nr, wt_ref[kk, blk * nr:(blk + 1) * nr, :], w[kk]) for kk in range(2)]
        x = x + modp_ref[5:6, :] * _moe_mix(y_ref, w[0], w[1])
        xnew_ref[...] = x
    ms = jnp.mean(x * x, axis=-1, keepdims=True)
    y = x * lax.rsqrt(ms + EPS) * nw_ref[...]
    h = y * (1.0 + mod_ref[1:2, :]) + mod_ref[0:1, :]
    hb = h.astype(BF16)

    def proj(c0, width):
        return jnp.dot(hb, w_ref[:, c0:c0 + width], preferred_element_type=F32)

    if full:
        uab = jnp.dot(proj(C_U, FNET_WIDTH).astype(BF16), cs_ref[...], preferred_element_type=F32)
        out["ab"][0] = uab[:, :FNET_WIDTH].astype(BF16)
        out["ab"][1] = uab[:, FNET_WIDTH:].astype(BF16)

    out["gqk"][...] = proj(C_GQ, 2 * GLA_QK)
    gv = proj(C_GV, GLA_WIDTH)
    out["gv"][...] = gv.astype(BF16)
    out["gvt"][...] = gv.T.astype(BF16)
    if full:
        out["gg"][...] = proj(C_GG, GLA_WIDTH).astype(BF16)
    pre = jnp.dot(proj(C_GD, 128).astype(BF16), wup_ref[...], preferred_element_type=F32) + bup_ref[...]
    out["la"][...] = (jnp.minimum(pre, 0.0) - jnp.log1p(jnp.exp(-jnp.abs(pre)))) * (1.0 / GLA_GATE_NORM)

    bd = bd_ref[...]
    kv = proj(C_AK, 256)
    k = _head_rms(kv[:, :128], bd[:128, :128], kn_ref[...])
    if rope:
        cos = cos_ref[...]
        sin = sin_ref[...]
        k = k * cos + _swap16(k) * sin
    if full:
        q = _head_rms(proj(C_AQ, ATT_WIDTH), bd, qn_ref[...])
        if rope:
            q = jnp.concatenate(
                [q[:, s:s + 128] * cos + _swap16(q[:, s:s + 128]) * sin for s in range(0, ATT_WIDTH, 128)], axis=1)
        out["q"][...] = (q * (ATT_HDIM ** -0.5 * LOG2E)).astype(BF16)
    kt_ref, vd_ref = out["kt"], out["vd"]
    v = kv[:, 128:]
    lo = lax.broadcasted_iota(jnp.int32, k.shape, 1) < ATT_HDIM
    k_sw = pltpu.roll(k, ATT_HDIM, 1)
    v_sw = pltpu.roll(v, ATT_HDIM, 1)
    kt_ref[0] = jnp.where(lo, k, k_sw).T.astype(BF16)
    kt_ref[1] = jnp.where(lo, k_sw, k).T.astype(BF16)
    vd_ref[0] = jnp.where(lo, v, 1.0).astype(BF16)
    vd_ref[1] = jnp.where(lo, 1.0, v_sw).astype(BF16)
    vd_ref[2] = jnp.where(lo, v_sw, 1.0).astype(BF16)
    vd_ref[3] = jnp.where(lo, 1.0, v).astype(BF16)


def _inproj_call(x, mod_l, mod_row, nw, w_perm, cs, wup, bup, qn, kn, bd, rope_tabs, pending=None, full=True):
    b, n, d = x.shape
    tm = min(1024, n)
    nt = n // tm
    rope = rope_tabs is not None
    if mod_row is None:
        mod_map = lambda bi, i: (bi, 0, 0)
    else:
        mod_map = lambda bi, i: (mod_row, 0, 0)
    const = lambda bi, i: (0, 0)
    in_specs = [
        pl.BlockSpec((None, tm, d), lambda bi, i: (bi, i, 0)),
        pl.BlockSpec((None, 6, d), mod_map),
        pl.BlockSpec((1, d), const),
        pl.BlockSpec((None, d, W_IN_COLS), lambda bi, i: (w_perm[1], 0, 0)),
        pl.BlockSpec((FNET_WIDTH, 2 * FNET_WIDTH), const),
        pl.BlockSpec((128, 2 * GLA_QK), const),
        pl.BlockSpec((1, 2 * GLA_QK), const),
        pl.BlockSpec((1, ATT_WIDTH), const),
        pl.BlockSpec((1, 128), const),
        pl.BlockSpec((ATT_WIDTH, ATT_WIDTH), const),
    ]
    args = [x, mod_l, nw, w_perm[0], cs, wup, bup, qn, kn, bd]
    if rope:
        in_specs += [pl.BlockSpec((tm, 128), lambda bi, i: (i, 0)), pl.BlockSpec((tm, 128), lambda bi, i: (i, 0))]
        args += list(rope_tabs)
    pending_rows = None
    if pending is not None:
        y2, wts3, row0, mod_prev = pending
        nr = tm // 128
        rps = n // 128
        assert 8 % nr == 0 and row0 % nr == 0 and rps % nr == 0
        pending_rows = (row0, rps)
        in_specs += [
            pl.BlockSpec((2, None, tm, d // 2), lambda bi, i: (0, bi, i, 0)),
            pl.BlockSpec((2, 8, 128), lambda bi, i: (0, (row0 + bi * rps + i * nr) // 8, 0)),
            pl.BlockSpec((None, 6, d), mod_map),
        ]
        args += [y2.reshape(2, b, n, d // 2), wts3, mod_prev]
    tok = lambda w: pl.BlockSpec((None, tm, w), lambda bi, i: (bi, i, 0))
    outs = []
    if full:
        outs.append(("ab", jax.ShapeDtypeStruct((2, n, b * FNET_WIDTH), BF16),
                     pl.BlockSpec((2, tm, FNET_WIDTH), lambda bi, i: (0, i, bi))))
    outs += [
        ("gqk", jax.ShapeDtypeStruct((b, n, 2 * GLA_QK), F32), tok(2 * GLA_QK)),
        ("gv", jax.ShapeDtypeStruct((b, n, GLA_WIDTH), BF16), tok(GLA_WIDTH)),
        ("gvt", jax.ShapeDtypeStruct((b, GLA_WIDTH, n), BF16),
         pl.BlockSpec((None, GLA_WIDTH, tm), lambda bi, i: (bi, 0, i))),
    ]
    if full:
        outs.append(("gg", jax.ShapeDtypeStruct((b, n, GLA_WIDTH), BF16), tok(GLA_WIDTH)))
    outs.append(("la", jax.ShapeDtypeStruct((b, n, 2 * GLA_QK), F32), tok(2 * GLA_QK)))
    if full:
        outs.append(("q", jax.ShapeDtypeStruct((b, n, ATT_WIDTH), BF16), tok(ATT_WIDTH)))
    outs += [
        ("kt", jax.ShapeDtypeStruct((b, ATT_KV_HEADS, 128, n), BF16),
         pl.BlockSpec((None, ATT_KV_HEADS, 128, tm), lambda bi, i: (bi, 0, 0, i))),
        ("vd", jax.ShapeDtypeStruct((b, 2 * ATT_KV_HEADS, n, 128), BF16),
         pl.BlockSpec((None, 2 * ATT_KV_HEADS, tm, 128), lambda bi, i: (bi, 0, i, 0))),
    ]
    if pending is not None:
        outs.append(("x", jax.ShapeDtypeStruct((b, n, d), F32), tok(d)))
    names = tuple(o[0] for o in outs)
    res = pl.pallas_call(
        functools.partial(_inproj_kernel, rope=rope, pending_rows=pending_rows, names=names),
        grid=(b, nt),
        in_specs=in_specs,
        out_specs=tuple(o[2] for o in outs),
        out_shape=tuple(o[1] for o in outs),
        compiler_params=_cparams(("parallel", "parallel")),
        name="inproj_rope" if rope else "inproj_ctx",
    )(*args)
    return dict(zip(names, res))


def _seqdft_kernel(t_ref, ab_ref, o_ref):
    y = jnp.dot(t_ref[...], ab_ref[...], preferred_element_type=F32)
    for bb in range(o_ref.shape[0]):
        o_ref[bb] = y[:, bb * FNET_WIDTH:(bb + 1) * FNET_WIDTH].astype(BF16)


def _seqdft_call(table, ab, b):
    n = table.shape[0]
    tm = min(1024, n)
    nb = 4 if b % 4 == 0 else (2 if b % 2 == 0 else 1)
    return pl.pallas_call(
        _seqdft_kernel,
        grid=(b // nb, n // tm),
        in_specs=[
            pl.BlockSpec((tm, 2 * n), lambda c, i: (i, 0)),
            pl.BlockSpec((2 * n, nb * FNET_WIDTH), lambda c, i: (0, c)),
        ],
        out_specs=pl.BlockSpec((nb, tm, FNET_WIDTH), lambda c, i: (c, i, 0)),
        out_shape=jax.ShapeDtypeStruct((b, n, FNET_WIDTH), BF16),
        compiler_params=_cparams(("parallel", "parallel")),
        name="seq_dft",
    )(table, ab)


def _gla_dir(qk, v, vt, a, s_in, fwd):
    p = GLA_PAIR
    r = lax.broadcasted_iota(jnp.int32, (p, p), 0)
    c = lax.broadcasted_iota(jnp.int32, (p, p), 1)
    same = (r // GLA_CHUNK) == (c // GLA_CHUNK)
    tri = same & ((c <= r) if fwd else (c >= r))
    row_lo = r < GLA_CHUNK
    rin = r % GLA_CHUNK

    q = qk[:, :GLA_QK] * (GLA_DK ** -0.5)
    k = qk[:, GLA_QK:]
    cum = a
    sh = 1
    while sh < GLA_CHUNK:
        if fwd:
            cum = cum + jnp.where(rin >= sh, pltpu.roll(cum, sh, 0), 0.0)
        else:
            cum = cum + jnp.where(rin < GLA_CHUNK - sh, pltpu.roll(cum, p - sh, 0), 0.0)
        sh *= 2
    if fwd:
        last0, last1 = cum[GLA_CHUNK - 1:GLA_CHUNK, :], cum[p - 1:p, :]
    else:
        last0, last1 = cum[0:1, :], cum[GLA_CHUNK:GLA_CHUNK + 1, :]
    lastb = jnp.where(row_lo, last0, last1)
    qt = q * jnp.exp(cum)
    kt = k * jnp.exp(-cum)
    kd = k * jnp.exp(lastb - cum)

    kt_b = kt.astype(BF16)
    zk = jnp.zeros_like(kt_b)
    ks = jnp.concatenate([jnp.where((c // GLA_DK) == hh, kt_b, zk) for hh in range(GLA_HEADS)], axis=0)
    att = _nt_dot(qt.astype(BF16), ks)
    tri4 = jnp.concatenate([tri] * GLA_HEADS, axis=1)
    att = jnp.where(tri4, att, 0.0).astype(BF16)
    col = lax.broadcasted_iota(jnp.int32, (p, GLA_WIDTH), 1)
    zv = jnp.zeros_like(v)
    vs = jnp.concatenate([jnp.where((col // GLA_DV) == hh, v, zv) for hh in range(GLA_HEADS)], axis=0)
    o_intra = jnp.dot(att, vs, preferred_element_type=F32)

    sr = lax.broadcasted_iota(jnp.int32, (GLA_WIDTH, GLA_QK), 0)
    sc = lax.broadcasted_iota(jnp.int32, (GLA_WIDTH, GLA_QK), 1)
    bdm = (sr // GLA_DV) == (sc // GLA_DK)
    first, second = (0, 1) if fwd else (1, 0)
    lasts = (last0, last1)
    in_chunk = (row_lo, jnp.logical_not(row_lo))
    kd2 = jnp.concatenate([jnp.where(in_chunk[0], kd, 0.0), jnp.where(in_chunk[1], kd, 0.0)], axis=1).astype(BF16)
    kvt2 = jnp.dot(vt, kd2, preferred_element_type=F32)
    kvt = (kvt2[:, :GLA_QK], kvt2[:, GLA_QK:])
    s_a = s_in
    s_b = s_a * jnp.exp(lasts[first]) + jnp.where(bdm, kvt[first], 0.0)
    s_c = s_b * jnp.exp(lasts[second]) + jnp.where(bdm, kvt[second], 0.0)
    q2 = jnp.concatenate([jnp.where(in_chunk[first], qt, 0.0), jnp.where(in_chunk[second], qt, 0.0)], axis=1)
    s2 = jnp.concatenate([s_a, s_b], axis=1).astype(BF16)
    o_inter = _nt_dot(q2.astype(BF16), s2)
    return o_intra + o_inter, s_c


def _gla_kernel(qkf, vf, vtf, laf, qkb, vb, vtb, lab, s0_ref, of_ref, ob_ref, sfin_ref, s_scr):
    i = pl.program_id(1)

    @pl.when(i == 0)
    def _():
        s_scr[...] = s0_ref[...]

    pairs = qkf.shape[1] // GLA_PAIR
    for gi in range(qkf.shape[0]):
        sf = s_scr[gi, 0]
        sb = s_scr[gi, 1]
        for pi in range(pairs):
            rf = slice(pi * GLA_PAIR, (pi + 1) * GLA_PAIR)
            rb = slice((pairs - 1 - pi) * GLA_PAIR, (pairs - pi) * GLA_PAIR)
            o1, sf = _gla_dir(qkf[gi, rf, :], vf[gi, rf, :], vtf[gi, :, rf], laf[gi, rf, :], sf, True)
            o2, sb = _gla_dir(qkb[gi, rb, :], vb[gi, rb, :], vtb[gi, :, rb], lab[gi, rb, :], sb, False)
            of_ref[gi, rf, :] = o1.astype(of_ref.dtype)
            ob_ref[gi, rb, :] = o2.astype(ob_ref.dtype)
        s_scr[gi, 0] = sf
        s_scr[gi, 1] = sb

    @pl.when(i == pl.num_programs(1) - 1)
    def _():
        sfin_ref[...] = s_scr[...]


def _gla_call(gqk, gv, gvt, la, s0):
    b, n, _ = gqk.shape
    p = GLA_PAIR * (2 if n % (2 * GLA_PAIR) == 0 else 1)
    npair = n // p
    gb = 4 if b % 4 == 0 else (2 if b % 2 == 0 else 1)
    fw = lambda bi, i: (bi, i, 0)
    bw = lambda bi, i: (bi, npair - 1 - i, 0)
    in_specs = [
        pl.BlockSpec((gb, p, 2 * GLA_QK), fw),
        pl.BlockSpec((gb, p, GLA_WIDTH), fw),
        pl.BlockSpec((gb, GLA_WIDTH, p), lambda bi, i: (bi, 0, i)),
        pl.BlockSpec((gb, p, GLA_QK), fw),
        pl.BlockSpec((gb, p, 2 * GLA_QK), bw),
        pl.BlockSpec((gb, p, GLA_WIDTH), bw),
        pl.BlockSpec((gb, GLA_WIDTH, p), lambda bi, i: (bi, 0, npair - 1 - i)),
        pl.BlockSpec((gb, p, GLA_QK), lambda bi, i: (bi, npair - 1 - i, 1)),
        pl.BlockSpec((gb, 2, GLA_WIDTH, GLA_QK), lambda bi, i: (bi, 0, 0, 0)),
    ]
    out_specs = (
        pl.BlockSpec((gb, p, GLA_WIDTH), fw),
        pl.BlockSpec((gb, p, GLA_WIDTH), bw),
        pl.BlockSpec((gb, 2, GLA_WIDTH, GLA_QK), lambda bi, i: (bi, 0, 0, 0)),
    )
    out_shape = (
        jax.ShapeDtypeStruct((b, n, GLA_WIDTH), BF16),
        jax.ShapeDtypeStruct((b, n, GLA_WIDTH), BF16),
        jax.ShapeDtypeStruct((b, 2, GLA_WIDTH, GLA_QK), F32),
    )
    return pl.pallas_call(
        _gla_kernel,
        grid=(b // gb, npair),
        in_specs=in_specs,
        out_specs=out_specs,
        out_shape=out_shape,
        scratch_shapes=[pltpu.VMEM((gb, 2, GLA_WIDTH, GLA_QK), F32)],
        compiler_params=_cparams(("parallel", "arbitrary")),
        name="gla_scan",
    )(gqk, gv, gvt, la, gqk, gv, gvt, la, s0)


def _attn_heads(q_ref, parts):
    tq = q_ref.shape[0]
    lane = lax.broadcasted_iota(jnp.int32, (tq, 128), 1)
    lo = lane < ATT_HDIM
    blocks = []
    for j in range(ATT_HEADS // 2):
        q128 = q_ref[:, 128 * j:128 * (j + 1)]
        g = (2 * j) // (ATT_HEADS // ATT_KV_HEADS)
        outs = []
        for half in range(2):
            qm = jnp.where(lo if half == 0 else jnp.logical_not(lo), q128, jnp.zeros_like(q128))
            ss = [jnp.dot(qm, kt_ref[g], preferred_element_type=F32) for kt_ref, _ in parts]
            m = functools.reduce(jnp.maximum, [jnp.max(s, axis=-1, keepdims=True) for s in ss])
            ps = [jnp.exp2(s - m).astype(BF16) for s in ss]
            pv = functools.reduce(
                lambda u, w: u + w,
                [jnp.dot(pp, vd_ref[2 * g + half], preferred_element_type=F32) for pp, (_, vd_ref) in zip(ps, parts)])
            den = pv[:, ATT_HDIM:ATT_HDIM + 1] if half == 0 else pv[:, 0:1]
            outs.append(pv / den)
        blocks.append(jnp.where(lo, outs[0], outs[1]).astype(BF16))
    return jnp.concatenate(blocks, axis=1)


def _attn_outproj_kernel(*refs, nparts):
    q_ref = refs[0]
    parts = [(refs[1 + 2 * i], refs[2 + 2 * i]) for i in range(nparts)]
    (f_ref, of_ref, ob_ref, gg_ref, x_ref, mod_ref, w_ref, gn_ref, bd_ref, nf_ref, wr_ref,
     xn_ref, h2_ref, lg_ref) = refs[1 + 2 * nparts:]
    att = _attn_heads(q_ref, parts)
    o = of_ref[...].astype(F32) + ob_ref[...].astype(F32)
    ms = jnp.dot((o * o).astype(BF16), bd_ref[...], preferred_element_type=F32) * (1.0 / GLA_DV)
    on = o * lax.rsqrt(ms + EPS) * gn_ref[...]
    g = gg_ref[...].astype(F32)
    gl = (on * (g * _sigmoid(g))).astype(BF16)
    ox = (jnp.dot(f_ref[...], w_ref[0:FNET_WIDTH, :], preferred_element_type=F32)
          + jnp.dot(gl, w_ref[FNET_WIDTH:FNET_WIDTH + GLA_WIDTH, :], preferred_element_type=F32)
          + jnp.dot(att, w_ref[FNET_WIDTH + GLA_WIDTH:, :], preferred_element_type=F32))
    xn = x_ref[...] + mod_ref[2:3, :] * ox
    xn_ref[...] = xn
    ms2 = jnp.mean(xn * xn, axis=-1, keepdims=True)
    h2 = xn * lax.rsqrt(ms2 + EPS) * nf_ref[...] * (1.0 + mod_ref[4:5, :]) + mod_ref[3:4, :]
    h2_ref[...] = _pack_bf16_pairs(h2)
    lg2 = _nt_dot(wr_ref[...], h2.astype(BF16))
    lg_ref[...] = lg2[:N_EXPERTS, :] + lg2[N_EXPERTS:, :]


def _attn_outproj_call(q, kv_parts, f, of, ob, gg, x, mod_l, mod_row, w_out, gn, bd, nf, wr):
    b, n, d = x.shape
    tm = min(1024, n)
    nt = n // tm
    if mod_row is None:
        mod_map = lambda bi, i: (bi, 0, 0)
    else:
        mod_map = lambda bi, i: (mod_row, 0, 0)
    const = lambda bi, i: (0, 0)
    tok = lambda w: pl.BlockSpec((None, tm, w), lambda bi, i: (bi, i, 0))
    in_specs = [tok(ATT_WIDTH)]
    args = [q]
    for kt, vd in kv_parts:
        m = kt.shape[-1]
        in_specs.append(pl.BlockSpec((None, ATT_KV_HEADS, 128, m), lambda bi, i: (bi, 0, 0, 0)))
        in_specs.append(pl.BlockSpec((None, 2 * ATT_KV_HEADS, m, 128), lambda bi, i: (bi, 0, 0, 0)))
        args += [kt, vd]
    in_specs += [
        tok(FNET_WIDTH), tok(GLA_WIDTH), tok(GLA_WIDTH), tok(GLA_WIDTH), tok(d),
        pl.BlockSpec((None, 6, d), mod_map),
        pl.BlockSpec((None, d, d), lambda bi, i: (w_out[1], 0, 0)),
        pl.BlockSpec((1, GLA_WIDTH), const),
        pl.BlockSpec((GLA_WIDTH, GLA_WIDTH), const),
        pl.BlockSpec((1, d), const),
        pl.BlockSpec((2 * N_EXPERTS, d), const),
    ]
    args += [f, of, ob, gg, x, mod_l, w_out[0], gn, bd, nf, wr]
    out_specs = (
        tok(d), tok(d // 2),
        pl.BlockSpec((N_EXPERTS, tm), lambda bi, i: (0, bi * nt + i)),
    )
    out_shape = (
        jax.ShapeDtypeStruct((b, n, d), F32),
        jax.ShapeDtypeStruct((b, n, d // 2), jnp.int32),
        jax.ShapeDtypeStruct((N_EXPERTS, b * n), F32),
    )
    return pl.pallas_call(
        functools.partial(_attn_outproj_kernel, nparts=len(kv_parts)),
        grid=(b, nt),
        in_specs=in_specs,
        out_specs=out_specs,
        out_shape=out_shape,
        compiler_params=_cparams(("parallel", "parallel")),
        name="attn_outproj",
    )(*args)


MOE_TILE_SLOTS = 256
MOE_TILE_ROWS = 8

def _route_kernel(b_ref, lg_ref, pos_ref, wt_ref, te_ref, nv_ref, *, tm):
    r = lg_ref.shape[1]
    s = [_sigmoid(lg_ref[e]) for e in range(N_EXPERTS)]
    sel = [s[e] + b_ref[e] for e in range(N_EXPERTS)]
    grp = []
    for g in range(N_GROUPS):
        a, b, c, d = sel[4 * g:4 * g + 4]
        hi1, lo1 = jnp.maximum(a, b), jnp.minimum(a, b)
        hi2, lo2 = jnp.maximum(c, d), jnp.minimum(c, d)
        m1 = jnp.maximum(hi1, hi2)
        m2 = jnp.maximum(jnp.minimum(hi1, hi2), jnp.maximum(lo1, lo2))
        grp.append(m1 + m2)
    one = jnp.ones_like(s[0])
    zero = jnp.zeros_like(s[0])
    msk = []
    for g in range(N_GROUPS):
        isg = one
        for g2 in range(N_GROUPS):
            if g2 < g:
                isg = isg * jnp.where(grp[g] > grp[g2], one, zero)
            elif g2 > g:
                isg = isg * jnp.where(grp[g] >= grp[g2], one, zero)
        for li in range(EXPERTS_PER_GROUP):
            e = 4 * g + li
            rank = zero
            for lj in range(EXPERTS_PER_GROUP):
                ej = 4 * g + lj
                if lj < li:
                    rank = rank + jnp.where(sel[ej] >= sel[e], one, zero)
                elif lj > li:
                    rank = rank + jnp.where(sel[ej] > sel[e], one, zero)
            msk.append(jnp.where(rank < 2.0, isg, zero))
    den = functools.reduce(lambda u, v: u + v, [msk[e] * s[e] for e in range(N_EXPERTS)])

    li_ = lax.broadcasted_iota(jnp.int32, (128, 128), 0)
    lj_ = lax.broadcasted_iota(jnp.int32, (128, 128), 1)
    upper = jnp.where(li_ < lj_, 1.0, 0.0).astype(BF16)
    ri_ = lax.broadcasted_iota(jnp.int32, (r, r), 0)
    rj_ = lax.broadcasted_iota(jnp.int32, (r, r), 1)
    lower = jnp.where(rj_ < ri_, 1.0, 0.0).astype(BF16)
    tiles = (1, te_ref.shape[1])
    tile_start = lax.broadcasted_iota(jnp.int32, tiles, 1).astype(F32) * float(tm)
    te = jnp.zeros(tiles, F32)
    seg = jnp.zeros(tiles, F32)
    nonempty = []
    off = jnp.zeros((1, 1), F32)
    seen = zero
    pos = [zero, zero]
    wts = [zero, zero]
    for e in range(N_EXPERTS):
        mb = msk[e].astype(BF16)
        lane_pre = jnp.dot(mb, upper, preferred_element_type=F32)
        row_pre = jnp.sum(jnp.dot(lower, mb, preferred_element_type=F32), axis=1, keepdims=True)
        cnt = jnp.sum(jnp.sum(msk[e], axis=1, keepdims=True), axis=0, keepdims=True)
        p_e = off + row_pre + lane_pre
        g_e = s[e] / den
        for kk in range(2):
            hit = msk[e] * jnp.where(seen == float(kk), one, zero)
            pos[kk] = pos[kk] + hit * p_e
            wts[kk] = wts[kk] + hit * g_e
        seen = seen + msk[e]
        off = off + jnp.floor((cnt + float(tm - 1)) * (1.0 / tm)) * float(tm)
        passed = jnp.where(tile_start >= off, 1.0, 0.0)
        te = te + passed
        nonempty.append(jnp.where(cnt > 0.0, 1.0, 0.0))
        seg = seg + nonempty[e] * passed
    for kk in range(2):
        pos_ref[kk] = pos[kk].astype(jnp.int32)
        wt_ref[kk] = wts[kk]
    te = jnp.minimum(te, float(N_EXPERTS - 1))
    nxt = jnp.full(tiles, -1.0, F32)
    for e in reversed(range(N_EXPERTS)):
        nxt = jnp.where(jnp.logical_and(nonempty[e] > 0.0, te < float(e)), float(e), nxt)
    te_ref[0:1, :] = te.astype(jnp.int32)
    te_ref[1:2, :] = seg.astype(jnp.int32)
    te_ref[2:3, :] = nxt.astype(jnp.int32)
    te_ref[3:, :] = jnp.zeros((te_ref.shape[0] - 3, te_ref.shape[1]), jnp.int32)
    nv_ref[...] = jnp.broadcast_to(off * (1.0 / tm), nv_ref.shape).astype(jnp.int32)


def _route_call(lgt, b_router, tm):
    n_tok = lgt.shape[1]
    r = n_tok // 128
    assert r * 128 == n_tok and r % 8 == 0 and 2 * n_tok // tm + N_EXPERTS <= MOE_TILE_SLOTS
    lg3 = lgt.reshape(N_EXPERTS, r, 128)
    full3 = lambda k: pl.BlockSpec((k, r, 128), lambda: (0, 0, 0))
    pos, wts, te, nv = pl.pallas_call(
        functools.partial(_route_kernel, tm=tm),
        in_specs=[pl.BlockSpec(memory_space=pltpu.SMEM), full3(N_EXPERTS)],
        out_specs=(full3(2), full3(2), pl.BlockSpec((MOE_TILE_ROWS, MOE_TILE_SLOTS), lambda: (0, 0)), pl.BlockSpec((1, 128), lambda: (0, 0))),
        out_shape=(
            jax.ShapeDtypeStruct((2, r, 128), jnp.int32),
            jax.ShapeDtypeStruct((2, r, 128), F32),
            jax.ShapeDtypeStruct((MOE_TILE_ROWS, MOE_TILE_SLOTS), jnp.int32),
            jax.ShapeDtypeStruct((1, 128), jnp.int32),
        ),
        compiler_params=pltpu.CompilerParams(vmem_limit_bytes=VMEM_LIMIT),
        name="route",
    )(b_router, lg3)
    return pos.reshape(2, n_tok), wts, te[:3].reshape(3 * MOE_TILE_SLOTS), nv[0, :1]


SC_CORES = 2
SC_SUBCORES = 16
SC_WORKERS = SC_CORES * SC_SUBCORES
SC_CHUNK = 32


def _sc_mesh():
    return plsc.VectorSubcoreMesh(core_axis_name="c", subcore_axis_name="s",
                                  num_cores=SC_CORES, num_subcores=SC_SUBCORES)


def _sc_steps(n_rows):
    per_w = n_rows // SC_WORKERS
    steps = per_w // SC_CHUNK
    assert per_w * SC_WORKERS == n_rows and steps * SC_CHUNK == per_w and steps % 2 == 0, n_rows
    return per_w, steps


def _sc_gather_rows(table, idx):
    p = idx.shape[0]
    d = table.shape[1]
    per_w, steps = _sc_steps(p)
    idx3 = idx.reshape(SC_WORKERS, steps, SC_CHUNK)

    @functools.partial(
        pl.kernel, mesh=_sc_mesh(),
        out_type=jax.ShapeDtypeStruct((p, d), table.dtype),
        scratch_types=[
            pltpu.VMEM((steps, SC_CHUNK), jnp.int32),
            pltpu.VMEM((SC_CHUNK, d), table.dtype),
            pltpu.VMEM((SC_CHUNK, d), table.dtype),
            pltpu.SemaphoreType.DMA, pltpu.SemaphoreType.DMA,
            pltpu.SemaphoreType.DMA, pltpu.SemaphoreType.DMA,
        ],
        name="sc_gather_rows",
    )
    def k(table_hbm, idx_hbm, out_hbm, idx_v, buf0, buf1, g0, g1, w0, w1):
        wid = lax.axis_index("s") * SC_CORES + lax.axis_index("c")
        base = wid * per_w
        pltpu.sync_copy(idx_hbm.at[wid], idx_v)

        def gather(s, buf, sem):
            return pltpu.make_async_copy(table_hbm.at[idx_v.at[s]], buf, sem)

        def write(s, buf, sem):
            return pltpu.make_async_copy(buf, out_hbm.at[pl.ds(base + s * SC_CHUNK, SC_CHUNK)], sem)

        gather(0, buf0, g0).start()

        @pl.loop(0, steps, step=2)
        def _(s):
            gather(s + 1, buf1, g1).start()
            gather(s, buf0, g0).wait()
            write(s, buf0, w0).start()
            write(s, buf0, w0).wait()

            @pl.when(s + 2 < steps)
            def _():
                gather(s + 2, buf0, g0).start()

            gather(s + 1, buf1, g1).wait()
            write(s + 1, buf1, w1).start()
            write(s + 1, buf1, w1).wait()

    return k(table, idx3)


def _sc_dispatch(srcs, poss, p_rows):
    d = srcs[0].shape[1]
    dt = srcs[0].dtype
    plans = [_sc_steps(src.shape[0]) for src in srcs]
    idxs = [pos.reshape(2, SC_WORKERS, st, SC_CHUNK) for pos, (_, st) in zip(poss, plans)]
    nseg = len(srcs)
    scratch = [pltpu.VMEM((2, st, SC_CHUNK), jnp.int32) for _, st in plans]
    scratch += [pltpu.VMEM((SC_CHUNK, d), dt), pltpu.VMEM((SC_CHUNK, d), dt)]
    scratch += [pltpu.SemaphoreType.DMA] * 6

    @functools.partial(
        pl.kernel, mesh=_sc_mesh(),
        out_type=jax.ShapeDtypeStruct((p_rows, d), dt),
        scratch_types=scratch,
        name="sc_dispatch",
    )
    def k(*refs):
        src_hbm = refs[:nseg]
        idx_hbm = refs[nseg:2 * nseg]
        out_hbm = refs[2 * nseg]
        idx_v = refs[2 * nseg + 1:3 * nseg + 1]
        buf0, buf1, r0, r1, a0, a1, b0, b1 = refs[3 * nseg + 1:]
        wid = lax.axis_index("s") * SC_CORES + lax.axis_index("c")
        for seg in range(nseg):
            per_w, steps = plans[seg]
            base = wid * per_w
            for kk in range(2):
                pltpu.sync_copy(idx_hbm[seg].at[kk, wid], idx_v[seg].at[kk])

            def read(s, buf, sem, seg=seg, base=base):
                return pltpu.make_async_copy(src_hbm[seg].at[pl.ds(base + s * SC_CHUNK, SC_CHUNK)], buf, sem)

            def scat(kk, s, buf, sem, seg=seg):
                return pltpu.make_async_copy(buf, out_hbm.at[idx_v[seg].at[kk, s]], sem)

            read(0, buf0, r0).start()

            @pl.loop(0, steps, step=2)
            def _(s, read=read, scat=scat, steps=steps):
                read(s + 1, buf1, r1).start()
                read(s, buf0, r0).wait()
                scat(0, s, buf0, a0).start()
                scat(1, s, buf0, b0).start()
                scat(0, s, buf0, a0).wait()
                scat(1, s, buf0, b0).wait()

                @pl.when(s + 2 < steps)
                def _():
                    read(s + 2, buf0, r0).start()

                read(s + 1, buf1, r1).wait()
                scat(0, s + 1, buf1, a1).start()
                scat(1, s + 1, buf1, b1).start()
                scat(0, s + 1, buf1, a1).wait()
                scat(1, s + 1, buf1, b1).wait()

    return k(*srcs, *idxs)


MOE_TM = 512


def _experts_kernel(te_ref, nv_ref, xs_ref, wg_hbm, wu_hbm, wd_hbm, ys_ref, wg_v, wu_v, wd_v, sem, *, layer):
    i = pl.program_id(0)
    valid = i < nv_ref[0]
    e = te_ref[i]
    first = jnp.logical_or(i == 0, e != te_ref[jnp.maximum(i - 1, 0)])
    slot = lax.rem(te_ref[MOE_TILE_SLOTS + i], 2)
    nxt = te_ref[2 * MOE_TILE_SLOTS + i]

    def weight_copies(expert, s):
        return (pltpu.make_async_copy(wg_hbm.at[layer, expert], wg_v.at[s], sem.at[s, 0]),
                pltpu.make_async_copy(wu_hbm.at[layer, expert], wu_v.at[s], sem.at[s, 1]),
                pltpu.make_async_copy(wd_hbm.at[layer, expert], wd_v.at[s], sem.at[s, 2]))

    @pl.when(jnp.logical_and(valid, i == 0))
    def _():
        for cp in weight_copies(e, slot):
            cp.start()

    @pl.when(jnp.logical_and(valid, first))
    def _():
        for cp in weight_copies(e, slot):
            cp.wait()

        @pl.when(nxt >= 0)
        def _():
            for cp in weight_copies(nxt, 1 - slot):
                cp.start()

    @pl.when(valid)
    def _():
        h = _unpack_bf16_pairs(xs_ref[...]).astype(BF16)
        half = D_EXPERT // 2
        y = None
        for j in range(2):
            sl = slice(j * half, (j + 1) * half)
            a = jnp.dot(h, wg_v[slot, :, sl].astype(BF16), preferred_element_type=F32)
            u = jnp.dot(h, wu_v[slot, :, sl].astype(BF16), preferred_element_type=F32)
            t = ((a * _sigmoid(a)) * u).astype(BF16)
            yj = jnp.dot(t, wd_v[slot, sl, :].astype(BF16), preferred_element_type=F32)
            y = yj if y is None else y + yj
        ys_ref[...] = _pack_bf16_pairs(y)


def _experts_call(xs, te, nv, wg, wu, wd, layer):
    p_rows, dh = xs.shape
    d = 2 * dh
    tm = MOE_TM
    nt = p_rows // tm
    assert nt <= MOE_TILE_SLOTS and te.shape == (3 * MOE_TILE_SLOTS,)
    row = lambda i, te_r, nv_r: (jnp.minimum(i, nv_r[0] - 1), 0)
    grid_spec = pltpu.PrefetchScalarGridSpec(
        num_scalar_prefetch=2,
        grid=(nt,),
        in_specs=[
            pl.BlockSpec((tm, dh), row),
            pl.BlockSpec(memory_space=pl.ANY),
            pl.BlockSpec(memory_space=pl.ANY),
            pl.BlockSpec(memory_space=pl.ANY),
        ],
        out_specs=pl.BlockSpec((tm, dh), row),
        scratch_shapes=[
            pltpu.VMEM((2, d, D_EXPERT), F32),
            pltpu.VMEM((2, d, D_EXPERT), F32),
            pltpu.VMEM((2, D_EXPERT, d), F32),
            pltpu.SemaphoreType.DMA((2, 3)),
        ],
    )
    return pl.pallas_call(
        functools.partial(_experts_kernel, layer=layer),
        grid_spec=grid_spec,
        out_shape=jax.ShapeDtypeStruct((p_rows, dh), jnp.int32),
        compiler_params=_cparams(("arbitrary",)),
        name="moe_experts",
    )(te, nv, xs, wg, wu, wd)


COMBINE_TM = 1024


def _combine_kernel(y_ref, wt_ref, x_ref, mod_ref, fn_ref, o_ref, *, final):
    xo = x_ref[...] + mod_ref[5:6, :] * _moe_mix(y_ref, wt_ref[0], wt_ref[1])
    if final:
        ms = jnp.mean(xo * xo, axis=-1, keepdims=True)
        xo = xo * lax.rsqrt(ms + EPS) * fn_ref[...]
    o_ref[...] = xo


def _combine_call(y2, wts3, row0, x, mod_l, mod_row, fn, final):
    b, n, d = x.shape
    n_tok = b * n
    tm = COMBINE_TM
    assert n_tok % tm == 0 and row0 % 8 == 0 and (mod_row is not None or n % tm == 0)
    if mod_row is None:
        mod_map = lambda i: ((i * tm) // n, 0, 0)
    else:
        mod_map = lambda i: (mod_row, 0, 0)
    out = pl.pallas_call(
        functools.partial(_combine_kernel, final=final),
        grid=(n_tok // tm,),
        in_specs=[
            pl.BlockSpec((2, tm, d // 2), lambda i: (0, i, 0)),
            pl.BlockSpec((2, tm // 128, 128), lambda i: (0, row0 // 8 + i, 0)),
            pl.BlockSpec((tm, d), lambda i: (i, 0)),
            pl.BlockSpec((None, 6, d), mod_map),
            pl.BlockSpec((1, d), lambda i: (0, 0)),
        ],
        out_specs=pl.BlockSpec((tm, d), lambda i: (i, 0)),
        out_shape=jax.ShapeDtypeStruct((n_tok, d), F32),
        compiler_params=_cparams(("parallel",)),
        name="moe_combine",
    )(y2, wts3, x.reshape(n_tok, d), mod_l, fn)
    return out.reshape(b, n, d)


def _moe_sparse(h_list, lg_list, x_list, mod_l, mod_rows, b_router, wg, wu, wd, layer, fn, final):
    d = h_list[0].shape[-1]
    sizes = [h.shape[0] * h.shape[1] for h in h_list]
    n_tok = sum(sizes)
    lgt = lg_list[0] if len(lg_list) == 1 else jnp.concatenate(lg_list, axis=1)
    pos, wts, te, nv = _route_call(lgt, b_router, MOE_TM)
    p_rows = 2 * n_tok + N_EXPERTS * MOE_TM
    offs = np.cumsum([0] + sizes)
    poss = [pos[:, offs[i]:offs[i + 1]] for i in range(len(sizes))]
    xs = _sc_dispatch([h.reshape(-1, d) for h in h_list], poss, p_rows)
    ys = _experts_call(xs, te, nv, wg, wu, wd, layer)
    outs = []
    for i, x in enumerate(x_list):
        y2 = _sc_gather_rows(ys, poss[i].reshape(-1)).reshape(2, sizes[i], d)
        row0 = int(offs[i]) // 128
        if final:
            outs.append(_combine_call(y2, wts, row0, x, mod_l, mod_rows[i], fn, True))
        else:
            outs.append((y2, wts, row0, mod_l))
    return outs


def _winprep_kernel(wt_ref, o_ref):
    gd0 = C_AQ
    gdw = 2 * GLA_GATE_RANK
    tail = W_IN_REF_COLS - gd0 - gdw
    o_ref[:, 0:gd0] = wt_ref[0:gd0, :].T.astype(BF16)
    o_ref[:, gd0:gd0 + tail] = wt_ref[gd0 + gdw:W_IN_REF_COLS, :].T.astype(BF16)
    gd = wt_ref[gd0:gd0 + 128, :].T
    lane = lax.broadcasted_iota(jnp.int32, gd.shape, 1)
    o_ref[:, C_GD:] = jnp.where(lane < gdw, gd, 0.0).astype(BF16)


def _winprep_call(w_in):
    depth, d, cols = w_in.shape
    assert cols == W_IN_REF_COLS and C_GD == cols - 2 * GLA_GATE_RANK and W_IN_COLS - C_GD == 128
    return pl.pallas_call(
        _winprep_kernel,
        grid=(depth,),
        in_specs=[pl.BlockSpec((None, cols, d), lambda l: (l, 0, 0))],
        out_specs=pl.BlockSpec((None, d, W_IN_COLS), lambda l: (l, 0, 0)),
        out_shape=jax.ShapeDtypeStruct((depth, d, W_IN_COLS), BF16),
        compiler_params=_cparams(("parallel",)),
        name="w_in_prep",
    )(jnp.swapaxes(w_in, 1, 2))


def _gate_up_weights(w_up, b_up):
    z = jnp.zeros((GLA_GATE_RANK, GLA_QK), w_up.dtype)
    top = jnp.concatenate([w_up[0], z], axis=1)
    mid = jnp.concatenate([z, w_up[1]], axis=1)
    pad = jnp.zeros((128 - 2 * GLA_GATE_RANK, 2 * GLA_QK), w_up.dtype)
    return jnp.concatenate([top, mid, pad], axis=0).astype(BF16), b_up.reshape(1, 2 * GLA_QK)


def kernel(x, c, ctx, c_ctx, w_ada, b_ada, norm_mix, norm_ffn, w_in, w_gla_gate_up, b_gla_gate, gla_norm, q_norm,
           k_norm, w_out, w_router, b_router, w_exp_gate, w_exp_up, w_exp_down, final_norm):
    b, n, d = x.shape
    m = ctx.shape[1]
    depth = w_ada.shape[0]
    assert d == D_MODEL and n % GLA_PAIR == 0 and m % GLA_PAIR == 0 and n % GRID_W == 0

    rows = ((b + 1 + 7) // 8) * 8
    cv = jnp.concatenate([c, c_ctx[None, :], jnp.zeros((rows - b - 1, d), F32)], axis=0)
    mod = _ada_call(cv, w_ada, b_ada).reshape(depth, rows, 6, d)

    cs = jnp.asarray(_channel_dft_table()).astype(BF16)
    tab_x = jnp.asarray(_seq_dft_table(n)).astype(BF16)
    tab_c = jnp.asarray(_seq_dft_table(m)).astype(BF16)
    rope_tabs = tuple(jnp.asarray(t) for t in _rope_tables(n))
    bd512 = jnp.asarray(_blockdiag_ones(ATT_WIDTH, ATT_HDIM)).astype(BF16)
    bd256 = jnp.asarray(_blockdiag_ones(GLA_WIDTH, GLA_DV)).astype(BF16)
    wr_t = w_router.T
    wr_hi = wr_t.astype(BF16)
    wrh = jnp.concatenate([wr_hi, (wr_t - wr_hi.astype(F32)).astype(BF16)], axis=0)
    fn = final_norm.reshape(1, d)

    w_in_perm = _winprep_call(w_in)
    w_out_bf = w_out.astype(BF16)

    xc = ctx
    pend_x = pend_c = None
    for l in range(depth):
        ctx_out = l < depth - 1
        mod_l = mod[l]
        w_perm = (w_in_perm, l)
        wup, bup = _gate_up_weights(w_gla_gate_up[l], b_gla_gate[l])
        nw = norm_mix[l].reshape(1, d)
        nf = norm_ffn[l].reshape(1, d)
        qn = jnp.tile(q_norm[l], ATT_HEADS).reshape(1, ATT_WIDTH)
        kn = jnp.tile(k_norm[l], ATT_KV_HEADS).reshape(1, 128)
        gn = jnp.tile(gla_norm[l], GLA_HEADS).reshape(1, GLA_WIDTH)
        wo = (w_out_bf, l)

        pc = _inproj_call(xc, mod_l, b, nw, w_perm, cs, wup, bup, qn, kn, bd512, None, pend_c, full=ctx_out)
        px = _inproj_call(x, mod_l, None, nw, w_perm, cs, wup, bup, qn, kn, bd512, rope_tabs, pend_x)
        if pend_x is not None:
            x, xc = px["x"], pc["x"]

        s_zero = jnp.zeros((b, 2, GLA_WIDTH, GLA_QK), F32)
        of_c, ob_c, s_fin = _gla_call(pc["gqk"], pc["gv"], pc["gvt"], pc["la"], s_zero)
        of_x, ob_x, _ = _gla_call(px["gqk"], px["gv"], px["gvt"], px["la"], s_fin)

        kv_c = (pc["kt"], pc["vd"])
        f_x = _seqdft_call(tab_x, px["ab"].reshape(2 * n, b * FNET_WIDTH), b)
        x, h2_x, lg_x = _attn_outproj_call(px["q"], [kv_c, (px["kt"], px["vd"])], f_x, of_x, ob_x, px["gg"], x,
                                           mod_l, None, wo, gn, bd256, nf, wrh)

        if ctx_out:
            f_c = _seqdft_call(tab_c, pc["ab"].reshape(2 * m, b * FNET_WIDTH), b)
            xc, h2_c, lg_c = _attn_outproj_call(pc["q"], [kv_c], f_c, of_c, ob_c, pc["gg"], xc,
                                                mod_l, b, wo, gn, bd256, nf, wrh)

        final = l == depth - 1
        wexp = (w_exp_gate, w_exp_up, w_exp_down, l)
        if ctx_out:
            res = _moe_sparse([h2_x, h2_c], [lg_x, lg_c], [x, xc], mod_l, [None, b], b_router, *wexp, fn, final)
        else:
            res = _moe_sparse([h2_x], [lg_x], [x], mod_l, [None], b_router, *wexp, fn, final)
        if final:
            x = res[0]
        else:
            pend_x, pend_c = res
    return x
```

```python
import functools

import numpy as np
import jax
import jax.numpy as jnp
from jax import lax
from jax.experimental import pallas as pl
from jax.experimental.pallas import tpu as pltpu
from jax.experimental.pallas import tpu_sc as plsc

F32 = jnp.float32
BF16 = jnp.bfloat16

D_MODEL = 1024
GRID_W = 64
EPS = 1e-6
LOG2E = 1.4426950408889634

FNET_WIDTH = 256
FNET_GROUPS = 4
FNET_GDIM = 64

GLA_HEADS = 4
GLA_DV = 64
GLA_DK = 32
GLA_WIDTH = 256
GLA_QK = 128
GLA_GATE_RANK = 16
GLA_GATE_NORM = 16.0
GLA_CHUNK = 64
GLA_PAIR = 2 * GLA_CHUNK

ATT_HEADS = 8
ATT_KV_HEADS = 2
ATT_HDIM = 64
ATT_WIDTH = 512
ROPE_FREQS = 16
ROPE_THETA = 10000.0

N_EXPERTS = 16
N_GROUPS = 4
EXPERTS_PER_GROUP = 4
D_EXPERT = 512

C_U = 0
C_GQ = 256
C_GK = 384
C_GV = 512
C_GG = 768
C_AQ = 1024
C_AK = 1536
C_GD = 1792
W_IN_COLS = 1920
W_IN_REF_COLS = 1824

VMEM_LIMIT = 56 * 1024 * 1024


def _cparams(sem):
    return pltpu.CompilerParams(dimension_semantics=sem, vmem_limit_bytes=VMEM_LIMIT)


def _sigmoid(x):
    return 1.0 / (1.0 + jnp.exp(-x))


def _pack_bf16_pairs(x):
    blocks = []
    for t in range(x.shape[1] // 256):
        lo = lax.bitcast_convert_type(x[:, 256 * t:256 * t + 128].astype(BF16).astype(F32), jnp.uint32)
        hi = lax.bitcast_convert_type(x[:, 256 * t + 128:256 * t + 256].astype(BF16).astype(F32), jnp.uint32)
        blocks.append((lo >> 16) | (hi & jnp.uint32(0xFFFF0000)))
    return lax.bitcast_convert_type(jnp.concatenate(blocks, axis=1), jnp.int32)


def _unpack_bf16_pairs(p):
    u = lax.bitcast_convert_type(p, jnp.uint32)
    blocks = []
    for t in range(p.shape[1] // 128):
        word = u[:, 128 * t:128 * (t + 1)]
        blocks += [lax.bitcast_convert_type(word << 16, F32),
                   lax.bitcast_convert_type(word & jnp.uint32(0xFFFF0000), F32)]
    return jnp.concatenate(blocks, axis=1)


def _token_columns(w):
    nr = w.shape[0]
    tm = nr * 128
    lane = lax.broadcasted_iota(jnp.int32, (tm, 128), 1)
    row = lax.broadcasted_iota(jnp.int32, (tm, 128), 0)
    wb = jnp.concatenate([jnp.broadcast_to(w[r:r + 1, :], (128, 128)) for r in range(nr)], axis=0)
    return jnp.sum(jnp.where(lane == (row % 128), wb, 0.0), axis=1, keepdims=True)


def _moe_mix(y_ref, w0, w1):
    return (_token_columns(w0) * _unpack_bf16_pairs(y_ref[0])
            + _token_columns(w1) * _unpack_bf16_pairs(y_ref[1]))


def _nt_dot(a, b):
    return lax.dot_general(a, b, (((1,), (1,)), ((), ())), preferred_element_type=F32)


@functools.lru_cache(maxsize=None)
def _channel_dft_table():
    j = np.arange(FNET_GDIM)
    ang = 2.0 * np.pi * ((j[:, None] * j[None, :]) % FNET_GDIM) / FNET_GDIM
    c = np.cos(ang) / np.sqrt(FNET_GDIM)
    s = np.sin(ang) / np.sqrt(FNET_GDIM)
    out = np.zeros((FNET_WIDTH, 2 * FNET_WIDTH), np.float64)
    for g in range(FNET_GROUPS):
        sl = slice(g * FNET_GDIM, (g + 1) * FNET_GDIM)
        out[sl, sl] = c
        out[sl, FNET_WIDTH + g * FNET_GDIM:FNET_WIDTH + (g + 1) * FNET_GDIM] = s
    return out.astype(np.float32)


@functools.lru_cache(maxsize=None)
def _seq_dft_table(n):
    j = np.arange(n, dtype=np.int64)
    ang = 2.0 * np.pi * ((j[:, None] * j[None, :]) % n) / n
    return np.concatenate([np.cos(ang), -np.sin(ang)], axis=1).astype(np.float32) / np.float32(np.sqrt(n))


@functools.lru_cache(maxsize=None)
def _rope_tables(n):
    rows = n // GRID_W
    row = np.repeat(np.arange(rows), GRID_W).astype(np.float64)
    col = np.tile(np.arange(GRID_W), rows).astype(np.float64)
    inv = ROPE_THETA ** (-np.arange(ROPE_FREQS, dtype=np.float64) * 2.0 / (2 * ROPE_FREQS))
    ar = row[:, None] * inv[None, :]
    ac = col[:, None] * inv[None, :]
    cos = np.concatenate([np.cos(ar), np.cos(ar), np.cos(ac), np.cos(ac)], axis=1)
    sin = np.concatenate([-np.sin(ar), np.sin(ar), -np.sin(ac), np.sin(ac)], axis=1)
    return (np.tile(cos, (1, 2)).astype(np.float32), np.tile(sin, (1, 2)).astype(np.float32))


@functools.lru_cache(maxsize=None)
def _blockdiag_ones(width, blk):
    i = np.arange(width)
    return (i[:, None] // blk == i[None, :] // blk).astype(np.float32)


def _ada_kernel(cv_ref, w_ref, b_ref, o_ref):
    cv = cv_ref[...]
    a = (cv * _sigmoid(cv)).astype(BF16)
    o_ref[...] = jnp.dot(a, w_ref[...].astype(BF16), preferred_element_type=F32) + b_ref[...]


def _ada_call(cv, w_ada, b_ada):
    depth, d, d6 = w_ada.shape
    tn = 1536
    rows = cv.shape[0]
    return pl.pallas_call(
        _ada_kernel,
        grid=(depth, d6 // tn),
        in_specs=[
            pl.BlockSpec((rows, d), lambda l, j: (0, 0)),
            pl.BlockSpec((None, d, tn), lambda l, j: (l, 0, j)),
            pl.BlockSpec((None, 1, tn), lambda l, j: (l, 0, j)),
        ],
        out_specs=pl.BlockSpec((None, rows, tn), lambda l, j: (l, 0, j)),
        out_shape=jax.ShapeDtypeStruct((depth, rows, d6), F32),
        compiler_params=_cparams(("parallel", "parallel")),
        name="ada_mod",
    )(cv, w_ada, b_ada.reshape(depth, 1, d6))


def _swap16(x):
    lane = lax.broadcasted_iota(jnp.int32, x.shape, 1)
    first = (lane % 32) < 16
    return jnp.where(first, pltpu.roll(x, 112, 1), pltpu.roll(x, 16, 1))


def _head_rms(x, bd, w):
    ms = jnp.dot((x * x).astype(BF16), bd, preferred_element_type=F32) * (1.0 / ATT_HDIM)
    return x * lax.rsqrt(ms + EPS) * w


def _inproj_kernel(*refs, rope, pending_rows, names):
    refs = list(refs)
    x_ref, mod_ref, nw_ref, w_ref, cs_ref, wup_ref, bup_ref, qn_ref, kn_ref, bd_ref = refs[:10]
    del refs[:10]
    if rope:
        cos_ref, sin_ref = refs[:2]
        del refs[:2]
    if pending_rows is not None:
        y_ref, wt_ref, modp_ref = refs[:3]
        del refs[:3]
    out = dict(zip(names, refs))
    full = "q" in out
    x = x_ref[...]
    if pending_rows is not None:
        xnew_ref = out["x"]
        row0, rows_per_sample = pending_rows
        nr = x.shape[0] // 128
        r = row0 + pl.program_id(0) * rows_per_sample + pl.program_id(1) * nr
        sub = lax.rem(r, 8)
        w = [wt_ref[kk, 0:nr, :] for kk in range(2)]
        for blk in range(1, 8 // nr):
            w = [jnp.where(sub == blk * nr, wt_ref[kk, blk * nr:(blk + 1) * nr, :], w[kk]) for kk in range(2)]
        x = x + modp_ref[5:6, :] * _moe_mix(y_ref, w[0], w[1])
        xnew_ref[...] = x
    ms = jnp.mean(x * x, axis=-1, keepdims=True)
    y = x * lax.rsqrt(ms + EPS) * nw_ref[...]
    h = y * (1.0 + mod_ref[1:2, :]) + mod_ref[0:1, :]
    hb = h.astype(BF16)

    def proj(c0, width):
        return jnp.dot(hb, w_ref[:, c0:c0 + width], preferred_element_type=F32)

    if full:
        uab = jnp.dot(proj(C_U, FNET_WIDTH).astype(BF16), cs_ref[...], preferred_element_type=F32)
        out["ab"][0] = uab[:, :FNET_WIDTH].astype(BF16)
        out["ab"][1] = uab[:, FNET_WIDTH:].astype(BF16)

    out["gqk"][...] = proj(C_GQ, 2 * GLA_QK)
    gv = proj(C_GV, GLA_WIDTH)
    out["gv"][...] = gv.astype(BF16)
    out["gvt"][...] = gv.T.astype(BF16)
    if full:
        out["gg"][...] = proj(C_GG, GLA_WIDTH).astype(BF16)
    pre = jnp.dot(proj(C_GD, 128).astype(BF16), wup_ref[...], preferred_element_type=F32) + bup_ref[...]
    out["la"][...] = (jnp.minimum(pre, 0.0) - jnp.log1p(jnp.exp(-jnp.abs(pre)))) * (1.0 / GLA_GATE_NORM)

    bd = bd_ref[...]
    kv = proj(C_AK, 256)
    k = _head_rms(kv[:, :128], bd[:128, :128], kn_ref[...])
    if rope:
        cos = cos_ref[...]
        sin = sin_ref[...]
        k = k * cos + _swap16(k) * sin
    if full:
        q = _head_rms(proj(C_AQ, ATT_WIDTH), bd, qn_ref[...])
        if rope:
            q = jnp.concatenate(
                [q[:, s:s + 128] * cos + _swap16(q[:, s:s + 128]) * sin for s in range(0, ATT_WIDTH, 128)], axis=1)
        out["q"][...] = (q * (ATT_HDIM ** -0.5 * LOG2E)).astype(BF16)
    kt_ref, vd_ref = out["kt"], out["vd"]
    v = kv[:, 128:]
    lo = lax.broadcasted_iota(jnp.int32, k.shape, 1) < ATT_HDIM
    k_sw = pltpu.roll(k, ATT_HDIM, 1)
    v_sw = pltpu.roll(v, ATT_HDIM, 1)
    kt_ref[0] = jnp.where(lo, k, k_sw).T.astype(BF16)
    kt_ref[1] = jnp.where(lo, k_sw, k).T.astype(BF16)
    vd_ref[0] = jnp.where(lo, v, 1.0).astype(BF16)
    vd_ref[1] = jnp.where(lo, 1.0, v_sw).astype(BF16)
    vd_ref[2] = jnp.where(lo, v_sw, 1.0).astype(BF16)
    vd_ref[3] = jnp.where(lo, 1.0, v).astype(BF16)


def _inproj_call(x, mod_l, mod_row, nw, w_perm, cs, wup, bup, qn, kn, bd, rope_tabs, pending=None, full=True):
    b, n, d = x.shape
    tm = min(1024, n)
    nt = n // tm
    rope = rope_tabs is not None
    if mod_row is None:
        mod_map = lambda bi, i: (bi, 0, 0)
    else:
        mod_map = lambda bi, i: (mod_row, 0, 0)
    const = lambda bi, i: (0, 0)
    in_specs = [
        pl.BlockSpec((None, tm, d), lambda bi, i: (bi, i, 0)),
        pl.BlockSpec((None, 6, d), mod_map),
        pl.BlockSpec((1, d), const),
        pl.BlockSpec((None, d, W_IN_COLS), lambda bi, i: (w_perm[1], 0, 0)),
        pl.BlockSpec((FNET_WIDTH, 2 * FNET_WIDTH), const),
        pl.BlockSpec((128, 2 * GLA_QK), const),
        pl.BlockSpec((1, 2 * GLA_QK), const),
        pl.BlockSpec((1, ATT_WIDTH), const),
        pl.BlockSpec((1, 128), const),
        pl.BlockSpec((ATT_WIDTH, ATT_WIDTH), const),
    ]
    args = [x, mod_l, nw, w_perm[0], cs, wup, bup, qn, kn, bd]
    if rope:
        in_specs += [pl.BlockSpec((tm, 128), lambda bi, i: (i, 0)), pl.BlockSpec((tm, 128), lambda bi, i: (i, 0))]
        args += list(rope_tabs)
    pending_rows = None
    if pending is not None:
        y2, wts3, row0, mod_prev = pending
        nr = tm // 128
        rps = n // 128
        assert 8 % nr == 0 and row0 % nr == 0 and rps % nr == 0
        pending_rows = (row0, rps)
        in_specs += [
            pl.BlockSpec((2, None, tm, d // 2), lambda bi, i: (0, bi, i, 0)),
            pl.BlockSpec((2, 8, 128), lambda bi, i: (0, (row0 + bi * rps + i * nr) // 8, 0)),
            pl.BlockSpec((None, 6, d), mod_map),
        ]
        args += [y2.reshape(2, b, n, d // 2), wts3, mod_prev]
    tok = lambda w: pl.BlockSpec((None, tm, w), lambda bi, i: (bi, i, 0))
    outs = []
    if full:
        outs.append(("ab", jax.ShapeDtypeStruct((2, n, b * FNET_WIDTH), BF16),
                     pl.BlockSpec((2, tm, FNET_WIDTH), lambda bi, i: (0, i, bi))))
    outs += [
        ("gqk", jax.ShapeDtypeStruct((b, n, 2 * GLA_QK), F32), tok(2 * GLA_QK)),
        ("gv", jax.ShapeDtypeStruct((b, n, GLA_WIDTH), BF16), tok(GLA_WIDTH)),
        ("gvt", jax.ShapeDtypeStruct((b, GLA_WIDTH, n), BF16),
         pl.BlockSpec((None, GLA_WIDTH, tm), lambda bi, i: (bi, 0, i))),
    ]
    if full:
        outs.append(("gg", jax.ShapeDtypeStruct((b, n, GLA_WIDTH), BF16), tok(GLA_WIDTH)))
    outs.append(("la", jax.ShapeDtypeStruct((b, n, 2 * GLA_QK), F32), tok(2 * GLA_QK)))
    if full:
        outs.append(("q", jax.ShapeDtypeStruct((b, n, ATT_WIDTH), BF16), tok(ATT_WIDTH)))
    outs += [
        ("kt", jax.ShapeDtypeStruct((b, ATT_KV_HEADS, 128, n), BF16),
         pl.BlockSpec((None, ATT_KV_HEADS, 128, tm), lambda bi, i: (bi, 0, 0, i))),
        ("vd", jax.ShapeDtypeStruct((b, 2 * ATT_KV_HEADS, n, 128), BF16),
         pl.BlockSpec((None, 2 * ATT_KV_HEADS, tm, 128), lambda bi, i: (bi, 0, i, 0))),
    ]
    if pending is not None:
        outs.append(("x", jax.ShapeDtypeStruct((b, n, d), F32), tok(d)))
    names = tuple(o[0] for o in outs)
    res = pl.pallas_call(
        functools.partial(_inproj_kernel, rope=rope, pending_rows=pending_rows, names=names),
        grid=(b, nt),
        in_specs=in_specs,
        out_specs=tuple(o[2] for o in outs),
        out_shape=tuple(o[1] for o in outs),
        compiler_params=_cparams(("parallel", "parallel")),
        name="inproj_rope" if rope else "inproj_ctx",
    )(*args)
    return dict(zip(names, res))


def _seqdft_kernel(t_ref, ab_ref, o_ref):
    y = jnp.dot(t_ref[...], ab_ref[...], preferred_element_type=F32)
    for bb in range(o_ref.shape[0]):
        o_ref[bb] = y[:, bb * FNET_WIDTH:(bb + 1) * FNET_WIDTH].astype(BF16)


def _seqdft_call(table, ab, b):
    n = table.shape[0]
    tm = min(512, n)
    nb = 4 if b % 4 == 0 else (2 if b % 2 == 0 else 1)
    return pl.pallas_call(
        _seqdft_kernel,
        grid=(b // nb, n // tm),
        in_specs=[
            pl.BlockSpec((tm, 2 * n), lambda c, i: (i, 0)),
            pl.BlockSpec((2 * n, nb * FNET_WIDTH), lambda c, i: (0, c)),
        ],
        out_specs=pl.BlockSpec((nb, tm, FNET_WIDTH), lambda c, i: (c, i, 0)),
        out_shape=jax.ShapeDtypeStruct((b, n, FNET_WIDTH), BF16),
        compiler_params=_cparams(("parallel", "parallel")),
        name="seq_dft",
    )(table, ab)


def _gla_dir(qk, v, vt, a, s_in, fwd):
    p = GLA_PAIR
    r = lax.broadcasted_iota(jnp.int32, (p, p), 0)
    c = lax.broadcasted_iota(jnp.int32, (p, p), 1)
    same = (r // GLA_CHUNK) == (c // GLA_CHUNK)
    tri = same & ((c <= r) if fwd else (c >= r))
    row_lo = r < GLA_CHUNK
    rin = r % GLA_CHUNK

    q = qk[:, :GLA_QK] * (GLA_DK ** -0.5)
    k = qk[:, GLA_QK:]
    cum = a
    sh = 1
    while sh < GLA_CHUNK:
        if fwd:
            cum = cum + jnp.where(rin >= sh, pltpu.roll(cum, sh, 0), 0.0)
        else:
            cum = cum + jnp.where(rin < GLA_CHUNK - sh, pltpu.roll(cum, p - sh, 0), 0.0)
        sh *= 2
    if fwd:
        last0, last1 = cum[GLA_CHUNK - 1:GLA_CHUNK, :], cum[p - 1:p, :]
    else:
        last0, last1 = cum[0:1, :], cum[GLA_CHUNK:GLA_CHUNK + 1, :]
    lastb = jnp.where(row_lo, last0, last1)
    qt = q * jnp.exp(cum)
    kt = k * jnp.exp(-cum)
    kd = k * jnp.exp(lastb - cum)

    kt_b = kt.astype(BF16)
    zk = jnp.zeros_like(kt_b)
    ks = jnp.concatenate([jnp.where((c // GLA_DK) == hh, kt_b, zk) for hh in range(GLA_HEADS)], axis=0)
    att = _nt_dot(qt.astype(BF16), ks)
    tri4 = jnp.concatenate([tri] * GLA_HEADS, axis=1)
    att = jnp.where(tri4, att, 0.0).astype(BF16)
    col = lax.broadcasted_iota(jnp.int32, (p, GLA_WIDTH), 1)
    zv = jnp.zeros_like(v)
    vs = jnp.concatenate([jnp.where((col // GLA_DV) == hh, v, zv) for hh in range(GLA_HEADS)], axis=0)
    o_intra = jnp.dot(att, vs, preferred_element_type=F32)

    sr = lax.broadcasted_iota(jnp.int32, (GLA_WIDTH, GLA_QK), 0)
    sc = lax.broadcasted_iota(jnp.int32, (GLA_WIDTH, GLA_QK), 1)
    bdm = (sr // GLA_DV) == (sc // GLA_DK)
    first, second = (0, 1) if fwd else (1, 0)
    lasts = (last0, last1)
    in_chunk = (row_lo, jnp.logical_not(row_lo))
    kd2 = jnp.concatenate([jnp.where(in_chunk[0], kd, 0.0), jnp.where(in_chunk[1], kd, 0.0)], axis=1).astype(BF16)
    kvt2 = jnp.dot(vt, kd2, preferred_element_type=F32)
    kvt = (kvt2[:, :GLA_QK], kvt2[:, GLA_QK:])
    s_a = s_in
    s_b = s_a * jnp.exp(lasts[first]) + jnp.where(bdm, kvt[first], 0.0)
    s_c = s_b * jnp.exp(lasts[second]) + jnp.where(bdm, kvt[second], 0.0)
    q2 = jnp.concatenate([jnp.where(in_chunk[first], qt, 0.0), jnp.where(in_chunk[second], qt, 0.0)], axis=1)
    s2 = jnp.concatenate([s_a, s_b], axis=1).astype(BF16)
    o_inter = _nt_dot(q2.astype(BF16), s2)
    return o_intra + o_inter, s_c


def _gla_kernel(qkf, vf, vtf, laf, qkb, vb, vtb, lab, s0_ref, of_ref, ob_ref, sfin_ref, s_scr):
    i = pl.program_id(1)

    @pl.when(i == 0)
    def _():
        s_scr[...] = s0_ref[...]

    pairs = qkf.shape[1] // GLA_PAIR
    for gi in range(qkf.shape[0]):
        sf = s_scr[gi, 0]
        sb = s_scr[gi, 1]
        for pi in range(pairs):
            rf = slice(pi * GLA_PAIR, (pi + 1) * GLA_PAIR)
            rb = slice((pairs - 1 - pi) * GLA_PAIR, (pairs - pi) * GLA_PAIR)
            o1, sf = _gla_dir(qkf[gi, rf, :], vf[gi, rf, :], vtf[gi, :, rf], laf[gi, rf, :], sf, True)
            o2, sb = _gla_dir(qkb[gi, rb, :], vb[gi, rb, :], vtb[gi, :, rb], lab[gi, rb, :], sb, False)
            of_ref[gi, rf, :] = o1.astype(of_ref.dtype)
            ob_ref[gi, rb, :] = o2.astype(ob_ref.dtype)
        s_scr[gi, 0] = sf
        s_scr[gi, 1] = sb

    @pl.when(i == pl.num_programs(1) - 1)
    def _():
        sfin_ref[...] = s_scr[...]


def _gla_call(gqk, gv, gvt, la, s0):
    b, n, _ = gqk.shape
    p = GLA_PAIR * (2 if n % (2 * GLA_PAIR) == 0 else 1)
    npair = n // p
    gb = 4 if b % 4 == 0 else (2 if b % 2 == 0 else 1)
    fw = lambda bi, i: (bi, i, 0)
    bw = lambda bi, i: (bi, npair - 1 - i, 0)
    in_specs = [
        pl.BlockSpec((gb, p, 2 * GLA_QK), fw),
        pl.BlockSpec((gb, p, GLA_WIDTH), fw),
        pl.BlockSpec((gb, GLA_WIDTH, p), lambda bi, i: (bi, 0, i)),
        pl.BlockSpec((gb, p, GLA_QK), fw),
        pl.BlockSpec((gb, p, 2 * GLA_QK), bw),
        pl.BlockSpec((gb, p, GLA_WIDTH), bw),
        pl.BlockSpec((gb, GLA_WIDTH, p), lambda bi, i: (bi, 0, npair - 1 - i)),
        pl.BlockSpec((gb, p, GLA_QK), lambda bi, i: (bi, npair - 1 - i, 1)),
        pl.BlockSpec((gb, 2, GLA_WIDTH, GLA_QK), lambda bi, i: (bi, 0, 0, 0)),
    ]
    out_specs = (
        pl.BlockSpec((gb, p, GLA_WIDTH), fw),
        pl.BlockSpec((gb, p, GLA_WIDTH), bw),
        pl.BlockSpec((gb, 2, GLA_WIDTH, GLA_QK), lambda bi, i: (bi, 0, 0, 0)),
    )
    out_shape = (
        jax.ShapeDtypeStruct((b, n, GLA_WIDTH), BF16),
        jax.ShapeDtypeStruct((b, n, GLA_WIDTH), BF16),
        jax.ShapeDtypeStruct((b, 2, GLA_WIDTH, GLA_QK), F32),
    )
    return pl.pallas_call(
        _gla_kernel,
        grid=(b // gb, npair),
        in_specs=in_specs,
        out_specs=out_specs,
        out_shape=out_shape,
        scratch_shapes=[pltpu.VMEM((gb, 2, GLA_WIDTH, GLA_QK), F32)],
        compiler_params=_cparams(("parallel", "arbitrary")),
        name="gla_scan",
    )(gqk, gv, gvt, la, gqk, gv, gvt, la, s0)


def _attn_heads(q_ref, parts):
    tq = q_ref.shape[0]
    lane = lax.broadcasted_iota(jnp.int32, (tq, 128), 1)
    lo = lane < ATT_HDIM
    blocks = []
    for j in range(ATT_HEADS // 2):
        q128 = q_ref[:, 128 * j:128 * (j + 1)]
        g = (2 * j) // (ATT_HEADS // ATT_KV_HEADS)
        outs = []
        for half in range(2):
            qm = jnp.where(lo if half == 0 else jnp.logical_not(lo), q128, jnp.zeros_like(q128))
            ss = [jnp.dot(qm, kt_ref[g], preferred_element_type=F32) for kt_ref, _ in parts]
            m = functools.reduce(jnp.maximum, [jnp.max(s, axis=-1, keepdims=True) for s in ss])
            ps = [jnp.exp2(s - m).astype(BF16) for s in ss]
            pv = functools.reduce(
                lambda u, w: u + w,
                [jnp.dot(pp, vd_ref[2 * g + half], preferred_element_type=F32) for pp, (_, vd_ref) in zip(ps, parts)])
            den = pv[:, ATT_HDIM:ATT_HDIM + 1] if half == 0 else pv[:, 0:1]
            outs.append(pv / den)
        blocks.append(jnp.where(lo, outs[0], outs[1]).astype(BF16))
    return jnp.concatenate(blocks, axis=1)


def _attn_outproj_kernel(*refs, nparts):
    q_ref = refs[0]
    parts = [(refs[1 + 2 * i], refs[2 + 2 * i]) for i in range(nparts)]
    (f_ref, of_ref, ob_ref, gg_ref, x_ref, mod_ref, w_ref, gn_ref, bd_ref, nf_ref, wr_ref,
     xn_ref, h2_ref, lg_ref) = refs[1 + 2 * nparts:]
    att = _attn_heads(q_ref, parts)
    o = of_ref[...].astype(F32) + ob_ref[...].astype(F32)
    ms = jnp.dot((o * o).astype(BF16), bd_ref[...], preferred_element_type=F32) * (1.0 / GLA_DV)
    on = o * lax.rsqrt(ms + EPS) * gn_ref[...]
    g = gg_ref[...].astype(F32)
    gl = (on * (g * _sigmoid(g))).astype(BF16)
    ox = (jnp.dot(f_ref[...], w_ref[0:FNET_WIDTH, :], preferred_element_type=F32)
          + jnp.dot(gl, w_ref[FNET_WIDTH:FNET_WIDTH + GLA_WIDTH, :], preferred_element_type=F32)
          + jnp.dot(att, w_ref[FNET_WIDTH + GLA_WIDTH:, :], preferred_element_type=F32))
    xn = x_ref[...] + mod_ref[2:3, :] * ox
    xn_ref[...] = xn
    ms2 = jnp.mean(xn * xn, axis=-1, keepdims=True)
    h2 = xn * lax.rsqrt(ms2 + EPS) * nf_ref[...] * (1.0 + mod_ref[4:5, :]) + mod_ref[3:4, :]
    h2_ref[...] = _pack_bf16_pairs(h2)
    lg2 = _nt_dot(wr_ref[...], h2.astype(BF16))
    lg_ref[...] = lg2[:N_EXPERTS, :] + lg2[N_EXPERTS:, :]


def _attn_outproj_call(q, kv_parts, f, of, ob, gg, x, mod_l, mod_row, w_out, gn, bd, nf, wr):
    b, n, d = x.shape
    tm = min(1024, n)
    nt = n // tm
    if mod_row is None:
        mod_map = lambda bi, i: (bi, 0, 0)
    else:
        mod_map = lambda bi, i: (mod_row, 0, 0)
    const = lambda bi, i: (0, 0)
    tok = lambda w: pl.BlockSpec((None, tm, w), lambda bi, i: (bi, i, 0))
    in_specs = [tok(ATT_WIDTH)]
    args = [q]
    for kt, vd in kv_parts:
        m = kt.shape[-1]
        in_specs.append(pl.BlockSpec((None, ATT_KV_HEADS, 128, m), lambda bi, i: (bi, 0, 0, 0)))
        in_specs.append(pl.BlockSpec((None, 2 * ATT_KV_HEADS, m, 128), lambda bi, i: (bi, 0, 0, 0)))
        args += [kt, vd]
    in_specs += [
        tok(FNET_WIDTH), tok(GLA_WIDTH), tok(GLA_WIDTH), tok(GLA_WIDTH), tok(d),
        pl.BlockSpec((None, 6, d), mod_map),
        pl.BlockSpec((None, d, d), lambda bi, i: (w_out[1], 0, 0)),
        pl.BlockSpec((1, GLA_WIDTH), const),
        pl.BlockSpec((GLA_WIDTH, GLA_WIDTH), const),
        pl.BlockSpec((1, d), const),
        pl.BlockSpec((2 * N_EXPERTS, d), const),
    ]
    args += [f, of, ob, gg, x, mod_l, w_out[0], gn, bd, nf, wr]
    out_specs = (
        tok(d), tok(d // 2),
        pl.BlockSpec((N_EXPERTS, tm), lambda bi, i: (0, bi * nt + i)),
    )
    out_shape = (
        jax.ShapeDtypeStruct((b, n, d), F32),
        jax.ShapeDtypeStruct((b, n, d // 2), jnp.int32),
        jax.ShapeDtypeStruct((N_EXPERTS, b * n), F32),
    )
    return pl.pallas_call(
        functools.partial(_attn_outproj_kernel, nparts=len(kv_parts)),
        grid=(b, nt),
        in_specs=in_specs,
        out_specs=out_specs,
        out_shape=out_shape,
        compiler_params=_cparams(("parallel", "parallel")),
        name="attn_outproj",
    )(*args)


MOE_TILE_SLOTS = 256
MOE_TILE_ROWS = 8
MOE_META = 4

def _route_kernel(b_ref, lg_ref, pos_ref, wt_ref, te_ref, nv_ref, *, tm):
    r = lg_ref.shape[1]
    s = [_sigmoid(lg_ref[e]) for e in range(N_EXPERTS)]
    sel = [s[e] + b_ref[e] for e in range(N_EXPERTS)]
    grp = []
    for g in range(N_GROUPS):
        a, b, c, d = sel[4 * g:4 * g + 4]
        hi1, lo1 = jnp.maximum(a, b), jnp.minimum(a, b)
        hi2, lo2 = jnp.maximum(c, d), jnp.minimum(c, d)
        m1 = jnp.maximum(hi1, hi2)
        m2 = jnp.maximum(jnp.minimum(hi1, hi2), jnp.maximum(lo1, lo2))
        grp.append(m1 + m2)
    one = jnp.ones_like(s[0])
    zero = jnp.zeros_like(s[0])
    msk = []
    for g in range(N_GROUPS):
        isg = one
        for g2 in range(N_GROUPS):
            if g2 < g:
                isg = isg * jnp.where(grp[g] > grp[g2], one, zero)
            elif g2 > g:
                isg = isg * jnp.where(grp[g] >= grp[g2], one, zero)
        for li in range(EXPERTS_PER_GROUP):
            e = 4 * g + li
            rank = zero
            for lj in range(EXPERTS_PER_GROUP):
                ej = 4 * g + lj
                if lj < li:
                    rank = rank + jnp.where(sel[ej] >= sel[e], one, zero)
                elif lj > li:
                    rank = rank + jnp.where(sel[ej] > sel[e], one, zero)
            msk.append(jnp.where(rank < 2.0, isg, zero))
    den = functools.reduce(lambda u, v: u + v, [msk[e] * s[e] for e in range(N_EXPERTS)])

    li_ = lax.broadcasted_iota(jnp.int32, (128, 128), 0)
    lj_ = lax.broadcasted_iota(jnp.int32, (128, 128), 1)
    upper = jnp.where(li_ < lj_, 1.0, 0.0).astype(BF16)
    ri_ = lax.broadcasted_iota(jnp.int32, (r, r), 0)
    rj_ = lax.broadcasted_iota(jnp.int32, (r, r), 1)
    lower = jnp.where(rj_ < ri_, 1.0, 0.0).astype(BF16)
    tiles = (1, te_ref.shape[1])
    tile_start = lax.broadcasted_iota(jnp.int32, tiles, 1).astype(F32) * float(tm)
    te = jnp.zeros(tiles, F32)
    seg = jnp.zeros(tiles, F32)
    nonempty = []
    off = jnp.zeros((1, 1), F32)
    seen = zero
    pos = [zero, zero]
    wts = [zero, zero]
    for e in range(N_EXPERTS):
        mb = msk[e].astype(BF16)
        lane_pre = jnp.dot(mb, upper, preferred_element_type=F32)
        row_pre = jnp.sum(jnp.dot(lower, mb, preferred_element_type=F32), axis=1, keepdims=True)
        cnt = jnp.sum(jnp.sum(msk[e], axis=1, keepdims=True), axis=0, keepdims=True)
        p_e = off + row_pre + lane_pre
        g_e = s[e] / den
        for kk in range(2):
            hit = msk[e] * jnp.where(seen == float(kk), one, zero)
            pos[kk] = pos[kk] + hit * p_e
            wts[kk] = wts[kk] + hit * g_e
        seen = seen + msk[e]
        off = off + jnp.floor((cnt + float(tm - 1)) * (1.0 / tm)) * float(tm)
        passed = jnp.where(tile_start >= off, 1.0, 0.0)
        te = te + passed
        nonempty.append(jnp.where(cnt > 0.0, 1.0, 0.0))
        seg = seg + nonempty[e] * passed
    for kk in range(2):
        pos_ref[kk] = pos[kk].astype(jnp.int32)
        wt_ref[kk] = wts[kk]
    te = jnp.minimum(te, float(N_EXPERTS - 1))
    nxt = jnp.full(tiles, -1.0, F32)
    for e in reversed(range(N_EXPERTS)):
        nxt = jnp.where(jnp.logical_and(nonempty[e] > 0.0, te < float(e)), float(e), nxt)
    nxt2 = jnp.full(tiles, -1.0, F32)
    for e in reversed(range(N_EXPERTS)):
        later = jnp.logical_and(nxt >= 0.0, nxt < float(e))
        nxt2 = jnp.where(jnp.logical_and(nonempty[e] > 0.0, later), float(e), nxt2)
    te_ref[0:1, :] = te.astype(jnp.int32)
    te_ref[1:2, :] = seg.astype(jnp.int32)
    te_ref[2:3, :] = nxt.astype(jnp.int32)
    te_ref[3:4, :] = nxt2.astype(jnp.int32)
    te_ref[4:, :] = jnp.zeros((te_ref.shape[0] - 4, te_ref.shape[1]), jnp.int32)
    nv_ref[...] = jnp.broadcast_to(off * (1.0 / tm), nv_ref.shape).astype(jnp.int32)


def _route_call(lgt, b_router, tm):
    n_tok = lgt.shape[1]
    r = n_tok // 128
    assert r * 128 == n_tok and r % 8 == 0 and 2 * n_tok // tm + N_EXPERTS <= MOE_TILE_SLOTS
    lg3 = lgt.reshape(N_EXPERTS, r, 128)
    full3 = lambda k: pl.BlockSpec((k, r, 128), lambda: (0, 0, 0))
    pos, wts, te, nv = pl.pallas_call(
        functools.partial(_route_kernel, tm=tm),
        in_specs=[pl.BlockSpec(memory_space=pltpu.SMEM), full3(N_EXPERTS)],
        out_specs=(full3(2), full3(2), pl.BlockSpec((MOE_TILE_ROWS, MOE_TILE_SLOTS), lambda: (0, 0)), pl.BlockSpec((1, 128), lambda: (0, 0))),
        out_shape=(
            jax.ShapeDtypeStruct((2, r, 128), jnp.int32),
            jax.ShapeDtypeStruct((2, r, 128), F32),
            jax.ShapeDtypeStruct((MOE_TILE_ROWS, MOE_TILE_SLOTS), jnp.int32),
            jax.ShapeDtypeStruct((1, 128), jnp.int32),
        ),
        compiler_params=pltpu.CompilerParams(vmem_limit_bytes=VMEM_LIMIT),
        name="route",
    )(b_router, lg3)
    return pos.reshape(2, n_tok), wts, te[:MOE_META].reshape(MOE_META * MOE_TILE_SLOTS), nv[0, :1]


SC_CORES = 2
SC_SUBCORES = 16
SC_WORKERS = SC_CORES * SC_SUBCORES
SC_CHUNK = 32


def _sc_mesh():
    return plsc.VectorSubcoreMesh(core_axis_name="c", subcore_axis_name="s",
                                  num_cores=SC_CORES, num_subcores=SC_SUBCORES)


def _sc_steps(n_rows):
    per_w = n_rows // SC_WORKERS
    steps = per_w // SC_CHUNK
    assert per_w * SC_WORKERS == n_rows and steps * SC_CHUNK == per_w and steps % 2 == 0, n_rows
    return per_w, steps


def _sc_gather_rows(table, idx):
    p = idx.shape[0]
    d = table.shape[1]
    per_w, steps = _sc_steps(p)
    idx3 = idx.reshape(SC_WORKERS, steps, SC_CHUNK)

    @functools.partial(
        pl.kernel, mesh=_sc_mesh(),
        out_type=jax.ShapeDtypeStruct((p, d), table.dtype),
        scratch_types=[
            pltpu.VMEM((steps, SC_CHUNK), jnp.int32),
            pltpu.VMEM((SC_CHUNK, d), table.dtype),
            pltpu.VMEM((SC_CHUNK, d), table.dtype),
            pltpu.SemaphoreType.DMA, pltpu.SemaphoreType.DMA,
            pltpu.SemaphoreType.DMA, pltpu.SemaphoreType.DMA,
        ],
        name="sc_gather_rows",
    )
    def k(table_hbm, idx_hbm, out_hbm, idx_v, buf0, buf1, g0, g1, w0, w1):
        wid = lax.axis_index("s") * SC_CORES + lax.axis_index("c")
        base = wid * per_w
        pltpu.sync_copy(idx_hbm.at[wid], idx_v)

        def gather(s, buf, sem):
            return pltpu.make_async_copy(table_hbm.at[idx_v.at[s]], buf, sem)

        def write(s, buf, sem):
            return pltpu.make_async_copy(buf, out_hbm.at[pl.ds(base + s * SC_CHUNK, SC_CHUNK)], sem)

        gather(0, buf0, g0).start()

        @pl.loop(0, steps, step=2)
        def _(s):
            gather(s + 1, buf1, g1).start()
            gather(s, buf0, g0).wait()
            write(s, buf0, w0).start()
            write(s, buf0, w0).wait()

            @pl.when(s + 2 < steps)
            def _():
                gather(s + 2, buf0, g0).start()

            gather(s + 1, buf1, g1).wait()
            write(s + 1, buf1, w1).start()
            write(s + 1, buf1, w1).wait()

    return k(table, idx3)


def _sc_dispatch(srcs, poss, p_rows):
    d = srcs[0].shape[1]
    dt = srcs[0].dtype
    plans = [_sc_steps(src.shape[0]) for src in srcs]
    idxs = [pos.reshape(2, SC_WORKERS, st, SC_CHUNK) for pos, (_, st) in zip(poss, plans)]
    nseg = len(srcs)
    scratch = [pltpu.VMEM((2, st, SC_CHUNK), jnp.int32) for _, st in plans]
    scratch += [pltpu.VMEM((SC_CHUNK, d), dt), pltpu.VMEM((SC_CHUNK, d), dt)]
    scratch += [pltpu.SemaphoreType.DMA] * 6

    @functools.partial(
        pl.kernel, mesh=_sc_mesh(),
        out_type=jax.ShapeDtypeStruct((p_rows, d), dt),
        scratch_types=scratch,
        name="sc_dispatch",
    )
    def k(*refs):
        src_hbm = refs[:nseg]
        idx_hbm = refs[nseg:2 * nseg]
        out_hbm = refs[2 * nseg]
        idx_v = refs[2 * nseg + 1:3 * nseg + 1]
        buf0, buf1, r0, r1, a0, a1, b0, b1 = refs[3 * nseg + 1:]
        wid = lax.axis_index("s") * SC_CORES + lax.axis_index("c")
        for seg in range(nseg):
            per_w, steps = plans[seg]
            base = wid * per_w
            for kk in range(2):
                pltpu.sync_copy(idx_hbm[seg].at[kk, wid], idx_v[seg].at[kk])

            def read(s, buf, sem, seg=seg, base=base):
                return pltpu.make_async_copy(src_hbm[seg].at[pl.ds(base + s * SC_CHUNK, SC_CHUNK)], buf, sem)

            def scat(kk, s, buf, sem, seg=seg):
                return pltpu.make_async_copy(buf, out_hbm.at[idx_v[seg].at[kk, s]], sem)

            read(0, buf0, r0).start()

            @pl.loop(0, steps, step=2)
            def _(s, read=read, scat=scat, steps=steps):
                read(s + 1, buf1, r1).start()
                read(s, buf0, r0).wait()
                scat(0, s, buf0, a0).start()
                scat(1, s, buf0, b0).start()
                scat(0, s, buf0, a0).wait()
                scat(1, s, buf0, b0).wait()

                @pl.when(s + 2 < steps)
                def _():
                    read(s + 2, buf0, r0).start()

                read(s + 1, buf1, r1).wait()
                scat(0, s + 1, buf1, a1).start()
                scat(1, s + 1, buf1, b1).start()
                scat(0, s + 1, buf1, a1).wait()
                scat(1, s + 1, buf1, b1).wait()

    return k(*srcs, *idxs)


MOE_TM = 512


MOE_WSLOTS = 4


def _experts_kernel(te_ref, nv_ref, xs_ref, wg_hbm, wu_hbm, wd_hbm, ys_ref, wg_v, wu_v, wd_v, sem, *, layer):
    i = pl.program_id(0)
    nv = nv_ref[0]
    rows = MOE_TM

    def weight_copies(expert, s):
        return (pltpu.make_async_copy(wg_hbm.at[layer, expert], wg_v.at[s], sem.at[s, 0]),
                pltpu.make_async_copy(wu_hbm.at[layer, expert], wu_v.at[s], sem.at[s, 1]),
                pltpu.make_async_copy(wd_hbm.at[layer, expert], wd_v.at[s], sem.at[s, 2]))

    def start(expert, s):
        for cp in weight_copies(expert, s):
            cp.start()

    experts, slots, valids = [], [], []
    for k in range(2):
        t = 2 * i + k
        e = te_ref[t]
        run = te_ref[MOE_TILE_SLOTS + t]
        nxt1 = te_ref[2 * MOE_TILE_SLOTS + t]
        nxt2 = te_ref[3 * MOE_TILE_SLOTS + t]
        slot = lax.rem(run, MOE_WSLOTS)
        valid = t < nv
        first = jnp.logical_or(t == 0, e != te_ref[jnp.maximum(t - 1, 0)])
        if k == 0:
            @pl.when(jnp.logical_and(valid, i == 0))
            def _(e=e, nxt1=nxt1):
                start(e, 0)

                @pl.when(nxt1 >= 0)
                def _():
                    start(nxt1, 1)

        @pl.when(jnp.logical_and(valid, first))
        def _(e=e, slot=slot, run=run, nxt2=nxt2):
            for cp in weight_copies(e, slot):
                cp.wait()

            @pl.when(nxt2 >= 0)
            def _():
                start(nxt2, lax.rem(run + 2, MOE_WSLOTS))

        experts.append(e)
        slots.append(slot)
        valids.append(valid)

    def mlp(packed, slot):
        h = _unpack_bf16_pairs(packed).astype(BF16)
        half = D_EXPERT // 2
        y = None
        for j in range(2):
            sl = slice(j * half, (j + 1) * half)
            a = jnp.dot(h, wg_v[slot, :, sl].astype(BF16), preferred_element_type=F32)
            u = jnp.dot(h, wu_v[slot, :, sl].astype(BF16), preferred_element_type=F32)
            t = ((a * _sigmoid(a)) * u).astype(BF16)
            yj = jnp.dot(t, wd_v[slot, sl, :].astype(BF16), preferred_element_type=F32)
            y = yj if y is None else y + yj
        return _pack_bf16_pairs(y)

    same = jnp.logical_and(valids[1], experts[0] == experts[1])

    @pl.when(jnp.logical_and(valids[0], same))
    def _():
        ys_ref[...] = mlp(xs_ref[...], slots[0])

    @pl.when(jnp.logical_and(valids[0], jnp.logical_not(same)))
    def _():
        ys_ref[0:rows, :] = mlp(xs_ref[0:rows, :], slots[0])

    @pl.when(jnp.logical_and(valids[1], jnp.logical_not(same)))
    def _():
        ys_ref[rows:, :] = mlp(xs_ref[rows:, :], slots[1])


def _experts_call(xs, te, nv, wg, wu, wd, layer):
    p_rows, dh = xs.shape
    d = 2 * dh
    tm = 2 * MOE_TM
    nt = p_rows // tm
    assert nt * tm == p_rows and 2 * nt <= MOE_TILE_SLOTS and te.shape == (MOE_META * MOE_TILE_SLOTS,)
    row = lambda i, te_r, nv_r: (jnp.minimum(i, (nv_r[0] + 1) // 2 - 1), 0)
    grid_spec = pltpu.PrefetchScalarGridSpec(
        num_scalar_prefetch=2,
        grid=(nt,),
        in_specs=[
            pl.BlockSpec((tm, dh), row),
            pl.BlockSpec(memory_space=pl.ANY),
            pl.BlockSpec(memory_space=pl.ANY),
            pl.BlockSpec(memory_space=pl.ANY),
        ],
        out_specs=pl.BlockSpec((tm, dh), row),
        scratch_shapes=[
            pltpu.VMEM((MOE_WSLOTS, d, D_EXPERT), F32),
            pltpu.VMEM((MOE_WSLOTS, d, D_EXPERT), F32),
            pltpu.VMEM((MOE_WSLOTS, D_EXPERT, d), F32),
            pltpu.SemaphoreType.DMA((MOE_WSLOTS, 3)),
        ],
    )
    return pl.pallas_call(
        functools.partial(_experts_kernel, layer=layer),
        grid_spec=grid_spec,
        out_shape=jax.ShapeDtypeStruct((p_rows, dh), jnp.int32),
        compiler_params=_cparams(("arbitrary",)),
        name="moe_experts",
    )(te, nv, xs, wg, wu, wd)


COMBINE_TM = 1024


def _combine_kernel(y_ref, wt_ref, x_ref, mod_ref, fn_ref, o_ref, *, final):
    xo = x_ref[...] + mod_ref[5:6, :] * _moe_mix(y_ref, wt_ref[0], wt_ref[1])
    if final:
        ms = jnp.mean(xo * xo, axis=-1, keepdims=True)
        xo = xo * lax.rsqrt(ms + EPS) * fn_ref[...]
    o_ref[...] = xo


def _combine_call(y2, wts3, row0, x, mod_l, mod_row, fn, final):
    b, n, d = x.shape
    n_tok = b * n
    tm = COMBINE_TM
    assert n_tok % tm == 0 and row0 % 8 == 0 and (mod_row is not None or n % tm == 0)
    if mod_row is None:
        mod_map = lambda i: ((i * tm) // n, 0, 0)
    else:
        mod_map = lambda i: (mod_row, 0, 0)
    out = pl.pallas_call(
        functools.partial(_combine_kernel, final=final),
        grid=(n_tok // tm,),
        in_specs=[
            pl.BlockSpec((2, tm, d // 2), lambda i: (0, i, 0)),
            pl.BlockSpec((2, tm // 128, 128), lambda i: (0, row0 // 8 + i, 0)),
            pl.BlockSpec((tm, d), lambda i: (i, 0)),
            pl.BlockSpec((None, 6, d), mod_map),
            pl.BlockSpec((1, d), lambda i: (0, 0)),
        ],
        out_specs=pl.BlockSpec((tm, d), lambda i: (i, 0)),
        out_shape=jax.ShapeDtypeStruct((n_tok, d), F32),
        compiler_params=_cparams(("parallel",)),
        name="moe_combine",
    )(y2, wts3, x.reshape(n_tok, d), mod_l, fn)
    return out.reshape(b, n, d)


def _moe_sparse(h_list, lg_list, x_list, mod_l, mod_rows, b_router, wg, wu, wd, layer, fn, final):
    d = h_list[0].shape[-1]
    sizes = [h.shape[0] * h.shape[1] for h in h_list]
    n_tok = sum(sizes)
    lgt = lg_list[0] if len(lg_list) == 1 else jnp.concatenate(lg_list, axis=1)
    pos, wts, te, nv = _route_call(lgt, b_router, MOE_TM)
    p_rows = 2 * n_tok + N_EXPERTS * MOE_TM
    offs = np.cumsum([0] + sizes)
    poss = [pos[:, offs[i]:offs[i + 1]] for i in range(len(sizes))]
    xs = _sc_dispatch([h.reshape(-1, d) for h in h_list], poss, p_rows)
    ys = _experts_call(xs, te, nv, wg, wu, wd, layer)
    outs = []
    for i, x in enumerate(x_list):
        y2 = _sc_gather_rows(ys, poss[i].reshape(-1)).reshape(2, sizes[i], d)
        row0 = int(offs[i]) // 128
        if final:
            outs.append(_combine_call(y2, wts, row0, x, mod_l, mod_rows[i], fn, True))
        else:
            outs.append((y2, wts, row0, mod_l))
    return outs


def _winprep_kernel(wt_ref, o_ref):
    gd0 = C_AQ
    gdw = 2 * GLA_GATE_RANK
    tail = W_IN_REF_COLS - gd0 - gdw
    o_ref[:, 0:gd0] = wt_ref[0:gd0, :].T.astype(BF16)
    o_ref[:, gd0:gd0 + tail] = wt_ref[gd0 + gdw:W_IN_REF_COLS, :].T.astype(BF16)
    gd = wt_ref[gd0:gd0 + 128, :].T
    lane = lax.broadcasted_iota(jnp.int32, gd.shape, 1)
    o_ref[:, C_GD:] = jnp.where(lane < gdw, gd, 0.0).astype(BF16)


def _winprep_call(w_in):
    depth, d, cols = w_in.shape
    assert cols == W_IN_REF_COLS and C_GD == cols - 2 * GLA_GATE_RANK and W_IN_COLS - C_GD == 128
    return pl.pallas_call(
        _winprep_kernel,
        grid=(depth,),
        in_specs=[pl.BlockSpec((None, cols, d), lambda l: (l, 0, 0))],
        out_specs=pl.BlockSpec((None, d, W_IN_COLS), lambda l: (l, 0, 0)),
        out_shape=jax.ShapeDtypeStruct((depth, d, W_IN_COLS), BF16),
        compiler_params=_cparams(("parallel",)),
        name="w_in_prep",
    )(jnp.swapaxes(w_in, 1, 2))


def _gate_up_weights(w_up, b_up):
    z = jnp.zeros((GLA_GATE_RANK, GLA_QK), w_up.dtype)
    top = jnp.concatenate([w_up[0], z], axis=1)
    mid = jnp.concatenate([z, w_up[1]], axis=1)
    pad = jnp.zeros((128 - 2 * GLA_GATE_RANK, 2 * GLA_QK), w_up.dtype)
    return jnp.concatenate([top, mid, pad], axis=0).astype(BF16), b_up.reshape(1, 2 * GLA_QK)


def kernel(x, c, ctx, c_ctx, w_ada, b_ada, norm_mix, norm_ffn, w_in, w_gla_gate_up, b_gla_gate, gla_norm, q_norm,
           k_norm, w_out, w_router, b_router, w_exp_gate, w_exp_up, w_exp_down, final_norm):
    b, n, d = x.shape
    m = ctx.shape[1]
    depth = w_ada.shape[0]
    assert d == D_MODEL and n % GLA_PAIR == 0 and m % GLA_PAIR == 0 and n % GRID_W == 0

    rows = ((b + 1 + 7) // 8) * 8
    cv = jnp.concatenate([c, c_ctx[None, :], jnp.zeros((rows - b - 1, d), F32)], axis=0)
    mod = _ada_call(cv, w_ada, b_ada).reshape(depth, rows, 6, d)

    cs = jnp.asarray(_channel_dft_table()).astype(BF16)
    tab_x = jnp.asarray(_seq_dft_table(n)).astype(BF16)
    tab_c = jnp.asarray(_seq_dft_table(m)).astype(BF16)
    rope_tabs = tuple(jnp.asarray(t) for t in _rope_tables(n))
    bd512 = jnp.asarray(_blockdiag_ones(ATT_WIDTH, ATT_HDIM)).astype(BF16)
    bd256 = jnp.asarray(_blockdiag_ones(GLA_WIDTH, GLA_DV)).astype(BF16)
    wr_t = w_router.T
    wr_hi = wr_t.astype(BF16)
    wrh = jnp.concatenate([wr_hi, (wr_t - wr_hi.astype(F32)).astype(BF16)], axis=0)
    fn = final_norm.reshape(1, d)

    w_in_perm = _winprep_call(w_in)
    w_out_bf = w_out.astype(BF16)

    xc = ctx
    pend_x = pend_c = None
    for l in range(depth):
        ctx_out = l < depth - 1
        mod_l = mod[l]
        w_perm = (w_in_perm, l)
        wup, bup = _gate_up_weights(w_gla_gate_up[l], b_gla_gate[l])
        nw = norm_mix[l].reshape(1, d)
        nf = norm_ffn[l].reshape(1, d)
        qn = jnp.tile(q_norm[l], ATT_HEADS).reshape(1, ATT_WIDTH)
        kn = jnp.tile(k_norm[l], ATT_KV_HEADS).reshape(1, 128)
        gn = jnp.tile(gla_norm[l], GLA_HEADS).reshape(1, GLA_WIDTH)
        wo = (w_out_bf, l)

        pc = _inproj_call(xc, mod_l, b, nw, w_perm, cs, wup, bup, qn, kn, bd512, None, pend_c, full=ctx_out)
        px = _inproj_call(x, mod_l, None, nw, w_perm, cs, wup, bup, qn, kn, bd512, rope_tabs, pend_x)
        if pend_x is not None:
            x, xc = px["x"], pc["x"]

        s_zero = jnp.zeros((b, 2, GLA_WIDTH, GLA_QK), F32)
        of_c, ob_c, s_fin = _gla_call(pc["gqk"], pc["gv"], pc["gvt"], pc["la"], s_zero)
        of_x, ob_x, _ = _gla_call(px["gqk"], px["gv"], px["gvt"], px["la"], s_fin)

        kv_c = (pc["kt"], pc["vd"])
        f_x = _seqdft_call(tab_x, px["ab"].reshape(2 * n, b * FNET_WIDTH), b)
        x, h2_x, lg_x = _attn_outproj_call(px["q"], [kv_c, (px["kt"], px["vd"])], f_x, of_x, ob_x, px["gg"], x,
                                           mod_l, None, wo, gn, bd256, nf, wrh)

        if ctx_out:
            f_c = _seqdft_call(tab_c, pc["ab"].reshape(2 * m, b * FNET_WIDTH), b)
            xc, h2_c, lg_c = _attn_outproj_call(pc["q"], [kv_c], f_c, of_c, ob_c, pc["gg"], xc,
                                                mod_l, b, wo, gn, bd256, nf, wrh)

        final = l == depth - 1
        wexp = (w_exp_gate, w_exp_up, w_exp_down, l)
        if ctx_out:
            res = _moe_sparse([h2_x, h2_c], [lg_x, lg_c], [x, xc], mod_l, [None, b], b_router, *wexp, fn, final)
        else:
            res = _moe_sparse([h2_x], [lg_x], [x], mod_l, [None], b_router, *wexp, fn, final)
        if final:
            x = res[0]
        else:
            pend_x, pend_c = res
    return x
```

```python
import functools

import numpy as np
import jax
import jax.numpy as jnp
from jax import lax
from jax.experimental import pallas as pl
from jax.experimental.pallas import tpu as pltpu
from jax.experimental.pallas import tpu_sc as plsc

F32 = jnp.float32
BF16 = jnp.bfloat16

D_MODEL = 1024
GRID_W = 64
EPS = 1e-6
LOG2E = 1.4426950408889634

FNET_WIDTH = 256
FNET_GROUPS = 4
FNET_GDIM = 64

GLA_HEADS = 4
GLA_DV = 64
GLA_DK = 32
GLA_WIDTH = 256
GLA_QK = 128
GLA_GATE_RANK = 16
GLA_GATE_NORM = 16.0
GLA_CHUNK = 64
GLA_PAIR = 2 * GLA_CHUNK

ATT_HEADS = 8
ATT_KV_HEADS = 2
ATT_HDIM = 64
ATT_WIDTH = 512
ROPE_FREQS = 16
ROPE_THETA = 10000.0

N_EXPERTS = 16
N_GROUPS = 4
EXPERTS_PER_GROUP = 4
D_EXPERT = 512

C_U = 0
C_GQ = 256
C_GK = 384
C_GV = 512
C_GG = 768
C_AQ = 1024
C_AK = 1536
C_GD = 1792
W_IN_COLS = 1920
W_IN_REF_COLS = 1824

VMEM_LIMIT = 56 * 1024 * 1024


def _cparams(sem):
    return pltpu.CompilerParams(dimension_semantics=sem, vmem_limit_bytes=VMEM_LIMIT)


def _sigmoid(x):
    return 1.0 / (1.0 + jnp.exp(-x))


def _pack_bf16_pairs(x):
    blocks = []
    for t in range(x.shape[1] // 256):
        lo = lax.bitcast_convert_type(x[:, 256 * t:256 * t + 128].astype(BF16).astype(F32), jnp.uint32)
        hi = lax.bitcast_convert_type(x[:, 256 * t + 128:256 * t + 256].astype(BF16).astype(F32), jnp.uint32)
        blocks.append((lo >> 16) | (hi & jnp.uint32(0xFFFF0000)))
    return lax.bitcast_convert_type(jnp.concatenate(blocks, axis=1), jnp.int32)


def _unpack_bf16_pairs(p):
    u = lax.bitcast_convert_type(p, jnp.uint32)
    blocks = []
    for t in range(p.shape[1] // 128):
        word = u[:, 128 * t:128 * (t + 1)]
        blocks += [lax.bitcast_convert_type(word << 16, F32),
                   lax.bitcast_convert_type(word & jnp.uint32(0xFFFF0000), F32)]
    return jnp.concatenate(blocks, axis=1)


def _token_columns(w):
    nr = w.shape[0]
    tm = nr * 128
    lane = lax.broadcasted_iota(jnp.int32, (tm, 128), 1)
    row = lax.broadcasted_iota(jnp.int32, (tm, 128), 0)
    wb = jnp.concatenate([jnp.broadcast_to(w[r:r + 1, :], (128, 128)) for r in range(nr)], axis=0)
    return jnp.sum(jnp.where(lane == (row % 128), wb, 0.0), axis=1, keepdims=True)


def _moe_mix(y_ref, w0, w1):
    return (_token_columns(w0) * _unpack_bf16_pairs(y_ref[0])
            + _token_columns(w1) * _unpack_bf16_pairs(y_ref[1]))


def _nt_dot(a, b):
    return lax.dot_general(a, b, (((1,), (1,)), ((), ())), preferred_element_type=F32)


@functools.lru_cache(maxsize=None)
def _channel_dft_table():
    j = np.arange(FNET_GDIM)
    ang = 2.0 * np.pi * ((j[:, None] * j[None, :]) % FNET_GDIM) / FNET_GDIM
    c = np.cos(ang) / np.sqrt(FNET_GDIM)
    s = np.sin(ang) / np.sqrt(FNET_GDIM)
    out = np.zeros((FNET_WIDTH, 2 * FNET_WIDTH), np.float64)
    for g in range(FNET_GROUPS):
        sl = slice(g * FNET_GDIM, (g + 1) * FNET_GDIM)
        out[sl, sl] = c
        out[sl, FNET_WIDTH + g * FNET_GDIM:FNET_WIDTH + (g + 1) * FNET_GDIM] = s
    return out.astype(np.float32)


@functools.lru_cache(maxsize=None)
def _seq_dft_table(n):
    j = np.arange(n, dtype=np.int64)
    ang = 2.0 * np.pi * ((j[:, None] * j[None, :]) % n) / n
    return np.concatenate([np.cos(ang), -np.sin(ang)], axis=1).astype(np.float32) / np.float32(np.sqrt(n))


@functools.lru_cache(maxsize=None)
def _rope_tables(n):
    rows = n // GRID_W
    row = np.repeat(np.arange(rows), GRID_W).astype(np.float64)
    col = np.tile(np.arange(GRID_W), rows).astype(np.float64)
    inv = ROPE_THETA ** (-np.arange(ROPE_FREQS, dtype=np.float64) * 2.0 / (2 * ROPE_FREQS))
    ar = row[:, None] * inv[None, :]
    ac = col[:, None] * inv[None, :]
    cos = np.concatenate([np.cos(ar), np.cos(ar), np.cos(ac), np.cos(ac)], axis=1)
    sin = np.concatenate([-np.sin(ar), np.sin(ar), -np.sin(ac), np.sin(ac)], axis=1)
    return (np.tile(cos, (1, 2)).astype(np.float32), np.tile(sin, (1, 2)).astype(np.float32))


@functools.lru_cache(maxsize=None)
def _blockdiag_ones(width, blk):
    i = np.arange(width)
    return (i[:, None] // blk == i[None, :] // blk).astype(np.float32)


def _ada_kernel(cv_ref, w_ref, b_ref, o_ref):
    cv = cv_ref[...]
    a = (cv * _sigmoid(cv)).astype(BF16)
    o_ref[...] = jnp.dot(a, w_ref[...].astype(BF16), preferred_element_type=F32) + b_ref[...]


def _ada_call(cv, w_ada, b_ada):
    depth, d, d6 = w_ada.shape
    tn = 1536
    rows = cv.shape[0]
    return pl.pallas_call(
        _ada_kernel,
        grid=(depth, d6 // tn),
        in_specs=[
            pl.BlockSpec((rows, d), lambda l, j: (0, 0)),
            pl.BlockSpec((None, d, tn), lambda l, j: (l, 0, j)),
            pl.BlockSpec((None, 1, tn), lambda l, j: (l, 0, j)),
        ],
        out_specs=pl.BlockSpec((None, rows, tn), lambda l, j: (l, 0, j)),
        out_shape=jax.ShapeDtypeStruct((depth, rows, d6), F32),
        compiler_params=_cparams(("parallel", "parallel")),
        name="ada_mod",
    )(cv, w_ada, b_ada.reshape(depth, 1, d6))


def _swap16(x):
    lane = lax.broadcasted_iota(jnp.int32, x.shape, 1)
    first = (lane % 32) < 16
    return jnp.where(first, pltpu.roll(x, 112, 1), pltpu.roll(x, 16, 1))


def _head_rms(x, bd, w):
    ms = jnp.dot((x * x).astype(BF16), bd, preferred_element_type=F32) * (1.0 / ATT_HDIM)
    return x * lax.rsqrt(ms + EPS) * w


def _inproj_kernel(*refs, rope, pending_rows, names):
    refs = list(refs)
    x_ref, mod_ref, nw_ref, w_ref, cs_ref, wup_ref, bup_ref, qn_ref, kn_ref, bd_ref = refs[:10]
    del refs[:10]
    if rope:
        cos_ref, sin_ref = refs[:2]
        del refs[:2]
    if pending_rows is not None:
        y_ref, wt_ref, modp_ref = refs[:3]
        del refs[:3]
    out = dict(zip(names, refs))
    full = "q" in out
    x = x_ref[...]
    if pending_rows is not None:
        xnew_ref = out["x"]
        row0, rows_per_sample = pending_rows
        nr = x.shape[0] // 128
        r = row0 + pl.program_id(0) * rows_per_sample + pl.program_id(1) * nr
        sub = lax.rem(r, 8)
        w = [wt_ref[kk, 0:nr, :] for kk in range(2)]
        for blk in range(1, 8 // nr):
            w = [jnp.where(sub == blk * nr, wt_ref[kk, blk * nr:(blk + 1) * nr, :], w[kk]) for kk in range(2)]
        x = x + modp_ref[5:6, :] * _moe_mix(y_ref, w[0], w[1])
        xnew_ref[...] = x
    ms = jnp.mean(x * x, axis=-1, keepdims=True)
    y = x * lax.rsqrt(ms + EPS) * nw_ref[...]
    h = y * (1.0 + mod_ref[1:2, :]) + mod_ref[0:1, :]
    hb = h.astype(BF16)

    def proj(c0, width):
        return jnp.dot(hb, w_ref[:, c0:c0 + width], preferred_element_type=F32)

    if full:
        uab = jnp.dot(proj(C_U, FNET_WIDTH).astype(BF16), cs_ref[...], preferred_element_type=F32)
        out["ab"][0] = uab[:, :FNET_WIDTH].astype(BF16)
        out["ab"][1] = uab[:, FNET_WIDTH:].astype(BF16)

    out["gqk"][...] = proj(C_GQ, 2 * GLA_QK)
    gv = proj(C_GV, GLA_WIDTH)
    out["gv"][...] = gv.astype(BF16)
    out["gvt"][...] = gv.T.astype(BF16)
    if full:
        out["gg"][...] = proj(C_GG, GLA_WIDTH).astype(BF16)
    pre = jnp.dot(proj(C_GD, 128).astype(BF16), wup_ref[...], preferred_element_type=F32) + bup_ref[...]
    out["la"][...] = (jnp.minimum(pre, 0.0) - jnp.log1p(jnp.exp(-jnp.abs(pre)))) * (1.0 / GLA_GATE_NORM)

    bd = bd_ref[...]
    kv = proj(C_AK, 256)
    k = _head_rms(kv[:, :128], bd[:128, :128], kn_ref[...])
    if rope:
        cos = cos_ref[...]
        sin = sin_ref[...]
        k = k * cos + _swap16(k) * sin
    if full:
        q = _head_rms(proj(C_AQ, ATT_WIDTH), bd, qn_ref[...])
        if rope:
            q = jnp.concatenate(
                [q[:, s:s + 128] * cos + _swap16(q[:, s:s + 128]) * sin for s in range(0, ATT_WIDTH, 128)], axis=1)
        out["q"][...] = (q * (ATT_HDIM ** -0.5 * LOG2E)).astype(BF16)
    kt_ref, vd_ref = out["kt"], out["vd"]
    v = kv[:, 128:]
    lo = lax.broadcasted_iota(jnp.int32, k.shape, 1) < ATT_HDIM
    k_sw = pltpu.roll(k, ATT_HDIM, 1)
    v_sw = pltpu.roll(v, ATT_HDIM, 1)
    kt_ref[0] = jnp.where(lo, k, k_sw).T.astype(BF16)
    kt_ref[1] = jnp.where(lo, k_sw, k).T.astype(BF16)
    vd_ref[0] = jnp.where(lo, v, 1.0).astype(BF16)
    vd_ref[1] = jnp.where(lo, 1.0, v_sw).astype(BF16)
    vd_ref[2] = jnp.where(lo, v_sw, 1.0).astype(BF16)
    vd_ref[3] = jnp.where(lo, 1.0, v).astype(BF16)


def _inproj_call(x, mod_l, mod_row, nw, w_perm, cs, wup, bup, qn, kn, bd, rope_tabs, pending=None, full=True):
    b, n, d = x.shape
    tm = min(1024, n)
    nt = n // tm
    rope = rope_tabs is not None
    if mod_row is None:
        mod_map = lambda bi, i: (bi, 0, 0)
    else:
        mod_map = lambda bi, i: (mod_row, 0, 0)
    const = lambda bi, i: (0, 0)
    in_specs = [
        pl.BlockSpec((None, tm, d), lambda bi, i: (bi, i, 0)),
        pl.BlockSpec((None, 6, d), mod_map),
        pl.BlockSpec((1, d), const),
        pl.BlockSpec((None, d, W_IN_COLS), lambda bi, i: (w_perm[1], 0, 0)),
        pl.BlockSpec((FNET_WIDTH, 2 * FNET_WIDTH), const),
        pl.BlockSpec((128, 2 * GLA_QK), const),
        pl.BlockSpec((1, 2 * GLA_QK), const),
        pl.BlockSpec((1, ATT_WIDTH), const),
        pl.BlockSpec((1, 128), const),
        pl.BlockSpec((ATT_WIDTH, ATT_WIDTH), const),
    ]
    args = [x, mod_l, nw, w_perm[0], cs, wup, bup, qn, kn, bd]
    if rope:
        in_specs += [pl.BlockSpec((tm, 128), lambda bi, i: (i, 0)), pl.BlockSpec((tm, 128), lambda bi, i: (i, 0))]
        args += list(rope_tabs)
    pending_rows = None
    if pending is not None:
        y2, wts3, row0, mod_prev = pending
        nr = tm // 128
        rps = n // 128
        assert 8 % nr == 0 and row0 % nr == 0 and rps % nr == 0
        pending_rows = (row0, rps)
        in_specs += [
            pl.BlockSpec((2, None, tm, d // 2), lambda bi, i: (0, bi, i, 0)),
            pl.BlockSpec((2, 8, 128), lambda bi, i: (0, (row0 + bi * rps + i * nr) // 8, 0)),
            pl.BlockSpec((None, 6, d), mod_map),
        ]
        args += [y2.reshape(2, b, n, d // 2), wts3, mod_prev]
    tok = lambda w: pl.BlockSpec((None, tm, w), lambda bi, i: (bi, i, 0))
    outs = []
    if full:
        outs.append(("ab", jax.ShapeDtypeStruct((2, n, b * FNET_WIDTH), BF16),
                     pl.BlockSpec((2, tm, FNET_WIDTH), lambda bi, i: (0, i, bi))))
    outs += [
        ("gqk", jax.ShapeDtypeStruct((b, n, 2 * GLA_QK), F32), tok(2 * GLA_QK)),
        ("gv", jax.ShapeDtypeStruct((b, n, GLA_WIDTH), BF16), tok(GLA_WIDTH)),
        ("gvt", jax.ShapeDtypeStruct((b, GLA_WIDTH, n), BF16),
         pl.BlockSpec((None, GLA_WIDTH, tm), lambda bi, i: (bi, 0, i))),
    ]
    if full:
        outs.append(("gg", jax.ShapeDtypeStruct((b, n, GLA_WIDTH), BF16), tok(GLA_WIDTH)))
    outs.append(("la", jax.ShapeDtypeStruct((b, n, 2 * GLA_QK), F32), tok(2 * GLA_QK)))
    if full:
        outs.append(("q", jax.ShapeDtypeStruct((b, n, ATT_WIDTH), BF16), tok(ATT_WIDTH)))
    outs += [
        ("kt", jax.ShapeDtypeStruct((b, ATT_KV_HEADS, 128, n), BF16),
         pl.BlockSpec((None, ATT_KV_HEADS, 128, tm), lambda bi, i: (bi, 0, 0, i))),
        ("vd", jax.ShapeDtypeStruct((b, 2 * ATT_KV_HEADS, n, 128), BF16),
         pl.BlockSpec((None, 2 * ATT_KV_HEADS, tm, 128), lambda bi, i: (bi, 0, i, 0))),
    ]
    if pending is not None:
        outs.append(("x", jax.ShapeDtypeStruct((b, n, d), F32), tok(d)))
    names = tuple(o[0] for o in outs)
    res = pl.pallas_call(
        functools.partial(_inproj_kernel, rope=rope, pending_rows=pending_rows, names=names),
        grid=(b, nt),
        in_specs=in_specs,
        out_specs=tuple(o[2] for o in outs),
        out_shape=tuple(o[1] for o in outs),
        compiler_params=_cparams(("parallel", "parallel")),
        name="inproj_rope" if rope else "inproj_ctx",
    )(*args)
    return dict(zip(names, res))


def _seqdft_kernel(t_ref, ab_ref, o_ref):
    y = jnp.dot(t_ref[...], ab_ref[...], preferred_element_type=F32)
    for bb in range(o_ref.shape[0]):
        o_ref[bb] = y[:, bb * FNET_WIDTH:(bb + 1) * FNET_WIDTH].astype(BF16)


def _seqdft_call(table, ab, b):
    n = table.shape[0]
    tm = min(512, n)
    nb = 4 if b % 4 == 0 else (2 if b % 2 == 0 else 1)
    return pl.pallas_call(
        _seqdft_kernel,
        grid=(b // nb, n // tm),
        in_specs=[
            pl.BlockSpec((tm, 2 * n), lambda c, i: (i, 0)),
            pl.BlockSpec((2 * n, nb * FNET_WIDTH), lambda c, i: (0, c)),
        ],
        out_specs=pl.BlockSpec((nb, tm, FNET_WIDTH), lambda c, i: (c, i, 0)),
        out_shape=jax.ShapeDtypeStruct((b, n, FNET_WIDTH), BF16),
        compiler_params=_cparams(("parallel", "parallel")),
        name="seq_dft",
    )(table, ab)


def _gla_dir(qk, v, vt, a, s_in, fwd):
    p = GLA_PAIR
    r = lax.broadcasted_iota(jnp.int32, (p, p), 0)
    c = lax.broadcasted_iota(jnp.int32, (p, p), 1)
    same = (r // GLA_CHUNK) == (c // GLA_CHUNK)
    tri = same & ((c <= r) if fwd else (c >= r))
    row_lo = r < GLA_CHUNK
    rin = r % GLA_CHUNK

    q = qk[:, :GLA_QK] * (GLA_DK ** -0.5)
    k = qk[:, GLA_QK:]
    cum = a
    sh = 1
    while sh < GLA_CHUNK:
        if fwd:
            cum = cum + jnp.where(rin >= sh, pltpu.roll(cum, sh, 0), 0.0)
        else:
            cum = cum + jnp.where(rin < GLA_CHUNK - sh, pltpu.roll(cum, p - sh, 0), 0.0)
        sh *= 2
    if fwd:
        last0, last1 = cum[GLA_CHUNK - 1:GLA_CHUNK, :], cum[p - 1:p, :]
    else:
        last0, last1 = cum[0:1, :], cum[GLA_CHUNK:GLA_CHUNK + 1, :]
    lastb = jnp.where(row_lo, last0, last1)
    qt = q * jnp.exp(cum)
    kt = k * jnp.exp(-cum)
    kd = k * jnp.exp(lastb - cum)

    kt_b = kt.astype(BF16)
    zk = jnp.zeros_like(kt_b)
    ks = jnp.concatenate([jnp.where((c // GLA_DK) == hh, kt_b, zk) for hh in range(GLA_HEADS)], axis=0)
    att = _nt_dot(qt.astype(BF16), ks)
    tri4 = jnp.concatenate([tri] * GLA_HEADS, axis=1)
    att = jnp.where(tri4, att, 0.0).astype(BF16)
    col = lax.broadcasted_iota(jnp.int32, (p, GLA_WIDTH), 1)
    zv = jnp.zeros_like(v)
    vs = jnp.concatenate([jnp.where((col // GLA_DV) == hh, v, zv) for hh in range(GLA_HEADS)], axis=0)
    o_intra = jnp.dot(att, vs, preferred_element_type=F32)

    sr = lax.broadcasted_iota(jnp.int32, (GLA_WIDTH, GLA_QK), 0)
    sc = lax.broadcasted_iota(jnp.int32, (GLA_WIDTH, GLA_QK), 1)
    bdm = (sr // GLA_DV) == (sc // GLA_DK)
    first, second = (0, 1) if fwd else (1, 0)
    lasts = (last0, last1)
    in_chunk = (row_lo, jnp.logical_not(row_lo))
    kd2 = jnp.concatenate([jnp.where(in_chunk[0], kd, 0.0), jnp.where(in_chunk[1], kd, 0.0)], axis=1).astype(BF16)
    kvt2 = jnp.dot(vt, kd2, preferred_element_type=F32)
    kvt = (kvt2[:, :GLA_QK], kvt2[:, GLA_QK:])
    s_a = s_in
    s_b = s_a * jnp.exp(lasts[first]) + jnp.where(bdm, kvt[first], 0.0)
    s_c = s_b * jnp.exp(lasts[second]) + jnp.where(bdm, kvt[second], 0.0)
    q2 = jnp.concatenate([jnp.where(in_chunk[first], qt, 0.0), jnp.where(in_chunk[second], qt, 0.0)], axis=1)
    s2 = jnp.concatenate([s_a, s_b], axis=1).astype(BF16)
    o_inter = _nt_dot(q2.astype(BF16), s2)
    return o_intra + o_inter, s_c


def _gla_kernel(qkf, vf, vtf, laf, qkb, vb, vtb, lab, s0_ref, of_ref, ob_ref, sfin_ref, s_scr):
    i = pl.program_id(1)

    @pl.when(i == 0)
    def _():
        s_scr[...] = s0_ref[...]

    pairs = qkf.shape[1] // GLA_PAIR
    for gi in range(qkf.shape[0]):
        sf = s_scr[gi, 0]
        sb = s_scr[gi, 1]
        for pi in range(pairs):
            rf = slice(pi * GLA_PAIR, (pi + 1) * GLA_PAIR)
            rb = slice((pairs - 1 - pi) * GLA_PAIR, (pairs - pi) * GLA_PAIR)
            o1, sf = _gla_dir(qkf[gi, rf, :], vf[gi, rf, :], vtf[gi, :, rf], laf[gi, rf, :], sf, True)
            o2, sb = _gla_dir(qkb[gi, rb, :], vb[gi, rb, :], vtb[gi, :, rb], lab[gi, rb, :], sb, False)
            of_ref[gi, rf, :] = o1.astype(of_ref.dtype)
            ob_ref[gi, rb, :] = o2.astype(ob_ref.dtype)
        s_scr[gi, 0] = sf
        s_scr[gi, 1] = sb

    @pl.when(i == pl.num_programs(1) - 1)
    def _():
        sfin_ref[...] = s_scr[...]


def _gla_call(gqk, gv, gvt, la, s0):
    b, n, _ = gqk.shape
    p = GLA_PAIR * max(k for k in (1, 2, 4) if n % (k * GLA_PAIR) == 0)
    npair = n // p
    gb = 4 if b % 4 == 0 else (2 if b % 2 == 0 else 1)
    fw = lambda bi, i: (bi, i, 0)
    bw = lambda bi, i: (bi, npair - 1 - i, 0)
    in_specs = [
        pl.BlockSpec((gb, p, 2 * GLA_QK), fw),
        pl.BlockSpec((gb, p, GLA_WIDTH), fw),
        pl.BlockSpec((gb, GLA_WIDTH, p), lambda bi, i: (bi, 0, i)),
        pl.BlockSpec((gb, p, GLA_QK), fw),
        pl.BlockSpec((gb, p, 2 * GLA_QK), bw),
        pl.BlockSpec((gb, p, GLA_WIDTH), bw),
        pl.BlockSpec((gb, GLA_WIDTH, p), lambda bi, i: (bi, 0, npair - 1 - i)),
        pl.BlockSpec((gb, p, GLA_QK), lambda bi, i: (bi, npair - 1 - i, 1)),
        pl.BlockSpec((gb, 2, GLA_WIDTH, GLA_QK), lambda bi, i: (bi, 0, 0, 0)),
    ]
    out_specs = (
        pl.BlockSpec((gb, p, GLA_WIDTH), fw),
        pl.BlockSpec((gb, p, GLA_WIDTH), bw),
        pl.BlockSpec((gb, 2, GLA_WIDTH, GLA_QK), lambda bi, i: (bi, 0, 0, 0)),
    )
    out_shape = (
        jax.ShapeDtypeStruct((b, n, GLA_WIDTH), BF16),
        jax.ShapeDtypeStruct((b, n, GLA_WIDTH), BF16),
        jax.ShapeDtypeStruct((b, 2, GLA_WIDTH, GLA_QK), F32),
    )
    return pl.pallas_call(
        _gla_kernel,
        grid=(b // gb, npair),
        in_specs=in_specs,
        out_specs=out_specs,
        out_shape=out_shape,
        scratch_shapes=[pltpu.VMEM((gb, 2, GLA_WIDTH, GLA_QK), F32)],
        compiler_params=_cparams(("parallel", "arbitrary")),
        name="gla_scan",
    )(gqk, gv, gvt, la, gqk, gv, gvt, la, s0)


def _attn_heads(q_ref, parts):
    tq = q_ref.shape[0]
    lane = lax.broadcasted_iota(jnp.int32, (tq, 128), 1)
    lo = lane < ATT_HDIM
    blocks = []
    for j in range(ATT_HEADS // 2):
        q128 = q_ref[:, 128 * j:128 * (j + 1)]
        g = (2 * j) // (ATT_HEADS // ATT_KV_HEADS)
        outs = []
        for half in range(2):
            qm = jnp.where(lo if half == 0 else jnp.logical_not(lo), q128, jnp.zeros_like(q128))
            ss = [jnp.dot(qm, kt_ref[g], preferred_element_type=F32) for kt_ref, _ in parts]
            m = functools.reduce(jnp.maximum, [jnp.max(s, axis=-1, keepdims=True) for s in ss])
            ps = [jnp.exp2(s - m).astype(BF16) for s in ss]
            pv = functools.reduce(
                lambda u, w: u + w,
                [jnp.dot(pp, vd_ref[2 * g + half], preferred_element_type=F32) for pp, (_, vd_ref) in zip(ps, parts)])
            den = pv[:, ATT_HDIM:ATT_HDIM + 1] if half == 0 else pv[:, 0:1]
            outs.append(pv / den)
        blocks.append(jnp.where(lo, outs[0], outs[1]).astype(BF16))
    return jnp.concatenate(blocks, axis=1)


def _attn_outproj_kernel(*refs, nparts):
    q_ref = refs[0]
    parts = [(refs[1 + 2 * i], refs[2 + 2 * i]) for i in range(nparts)]
    (f_ref, of_ref, ob_ref, gg_ref, x_ref, mod_ref, w_ref, gn_ref, bd_ref, nf_ref, wr_ref,
     xn_ref, h2_ref, lg_ref) = refs[1 + 2 * nparts:]
    att = _attn_heads(q_ref, parts)
    o = of_ref[...].astype(F32) + ob_ref[...].astype(F32)
    ms = jnp.dot((o * o).astype(BF16), bd_ref[...], preferred_element_type=F32) * (1.0 / GLA_DV)
    on = o * lax.rsqrt(ms + EPS) * gn_ref[...]
    g = gg_ref[...].astype(F32)
    gl = (on * (g * _sigmoid(g))).astype(BF16)
    ox = (jnp.dot(f_ref[...], w_ref[0:FNET_WIDTH, :], preferred_element_type=F32)
          + jnp.dot(gl, w_ref[FNET_WIDTH:FNET_WIDTH + GLA_WIDTH, :], preferred_element_type=F32)
          + jnp.dot(att, w_ref[FNET_WIDTH + GLA_WIDTH:, :], preferred_element_type=F32))
    xn = x_ref[...] + mod_ref[2:3, :] * ox
    xn_ref[...] = xn
    ms2 = jnp.mean(xn * xn, axis=-1, keepdims=True)
    h2 = xn * lax.rsqrt(ms2 + EPS) * nf_ref[...] * (1.0 + mod_ref[4:5, :]) + mod_ref[3:4, :]
    h2_ref[...] = _pack_bf16_pairs(h2)
    lg2 = _nt_dot(wr_ref[...], h2.astype(BF16))
    lg_ref[...] = lg2[:N_EXPERTS, :] + lg2[N_EXPERTS:, :]


def _attn_outproj_call(q, kv_parts, f, of, ob, gg, x, mod_l, mod_row, w_out, gn, bd, nf, wr):
    b, n, d = x.shape
    tm = min(1024, n)
    nt = n // tm
    if mod_row is None:
        mod_map = lambda bi, i: (bi, 0, 0)
    else:
        mod_map = lambda bi, i: (mod_row, 0, 0)
    const = lambda bi, i: (0, 0)
    tok = lambda w: pl.BlockSpec((None, tm, w), lambda bi, i: (bi, i, 0))
    in_specs = [tok(ATT_WIDTH)]
    args = [q]
    for kt, vd in kv_parts:
        m = kt.shape[-1]
        in_specs.append(pl.BlockSpec((None, ATT_KV_HEADS, 128, m), lambda bi, i: (bi, 0, 0, 0)))
        in_specs.append(pl.BlockSpec((None, 2 * ATT_KV_HEADS, m, 128), lambda bi, i: (bi, 0, 0, 0)))
        args += [kt, vd]
    in_specs += [
        tok(FNET_WIDTH), tok(GLA_WIDTH), tok(GLA_WIDTH), tok(GLA_WIDTH), tok(d),
        pl.BlockSpec((None, 6, d), mod_map),
        pl.BlockSpec((None, d, d), lambda bi, i: (w_out[1], 0, 0)),
        pl.BlockSpec((1, GLA_WIDTH), const),
        pl.BlockSpec((GLA_WIDTH, GLA_WIDTH), const),
        pl.BlockSpec((1, d), const),
        pl.BlockSpec((2 * N_EXPERTS, d), const),
    ]
    args += [f, of, ob, gg, x, mod_l, w_out[0], gn, bd, nf, wr]
    out_specs = (
        tok(d), tok(d // 2),
        pl.BlockSpec((N_EXPERTS, tm), lambda bi, i: (0, bi * nt + i)),
    )
    out_shape = (
        jax.ShapeDtypeStruct((b, n, d), F32),
        jax.ShapeDtypeStruct((b, n, d // 2), jnp.int32),
        jax.ShapeDtypeStruct((N_EXPERTS, b * n), F32),
    )
    return pl.pallas_call(
        functools.partial(_attn_outproj_kernel, nparts=len(kv_parts)),
        grid=(b, nt),
        in_specs=in_specs,
        out_specs=out_specs,
        out_shape=out_shape,
        compiler_params=_cparams(("parallel", "parallel")),
        name="attn_outproj",
    )(*args)


MOE_TILE_SLOTS = 256
MOE_TILE_ROWS = 8
MOE_META = 4

def _route_kernel(b_ref, lg_ref, pos_ref, wt_ref, te_ref, nv_ref, *, tm):
    r = lg_ref.shape[1]
    s = [_sigmoid(lg_ref[e]) for e in range(N_EXPERTS)]
    sel = [s[e] + b_ref[e] for e in range(N_EXPERTS)]
    grp = []
    for g in range(N_GROUPS):
        a, b, c, d = sel[4 * g:4 * g + 4]
        hi1, lo1 = jnp.maximum(a, b), jnp.minimum(a, b)
        hi2, lo2 = jnp.maximum(c, d), jnp.minimum(c, d)
        m1 = jnp.maximum(hi1, hi2)
        m2 = jnp.maximum(jnp.minimum(hi1, hi2), jnp.maximum(lo1, lo2))
        grp.append(m1 + m2)
    one = jnp.ones_like(s[0])
    zero = jnp.zeros_like(s[0])
    msk = []
    for g in range(N_GROUPS):
        isg = one
        for g2 in range(N_GROUPS):
            if g2 < g:
                isg = isg * jnp.where(grp[g] > grp[g2], one, zero)
            elif g2 > g:
                isg = isg * jnp.where(grp[g] >= grp[g2], one, zero)
        for li in range(EXPERTS_PER_GROUP):
            e = 4 * g + li
            rank = zero
            for lj in range(EXPERTS_PER_GROUP):
                ej = 4 * g + lj
                if lj < li:
                    rank = rank + jnp.where(sel[ej] >= sel[e], one, zero)
                elif lj > li:
                    rank = rank + jnp.where(sel[ej] > sel[e], one, zero)
            msk.append(jnp.where(rank < 2.0, isg, zero))
    den = functools.reduce(lambda u, v: u + v, [msk[e] * s[e] for e in range(N_EXPERTS)])

    li_ = lax.broadcasted_iota(jnp.int32, (128, 128), 0)
    lj_ = lax.broadcasted_iota(jnp.int32, (128, 128), 1)
    upper = jnp.where(li_ < lj_, 1.0, 0.0).astype(BF16)
    ri_ = lax.broadcasted_iota(jnp.int32, (r, r), 0)
    rj_ = lax.broadcasted_iota(jnp.int32, (r, r), 1)
    lower = jnp.where(rj_ < ri_, 1.0, 0.0).astype(BF16)
    tiles = (1, te_ref.shape[1])
    tile_start = lax.broadcasted_iota(jnp.int32, tiles, 1).astype(F32) * float(tm)
    te = jnp.zeros(tiles, F32)
    seg = jnp.zeros(tiles, F32)
    nonempty = []
    off = jnp.zeros((1, 1), F32)
    seen = zero
    pos = [zero, zero]
    wts = [zero, zero]
    for e in range(N_EXPERTS):
        mb = msk[e].astype(BF16)
        lane_pre = jnp.dot(mb, upper, preferred_element_type=F32)
        row_pre = jnp.sum(jnp.dot(lower, mb, preferred_element_type=F32), axis=1, keepdims=True)
        cnt = jnp.sum(jnp.sum(msk[e], axis=1, keepdims=True), axis=0, keepdims=True)
        p_e = off + row_pre + lane_pre
        g_e = s[e] / den
        for kk in range(2):
            hit = msk[e] * jnp.where(seen == float(kk), one, zero)
            pos[kk] = pos[kk] + hit * p_e
            wts[kk] = wts[kk] + hit * g_e
        seen = seen + msk[e]
        off = off + jnp.floor((cnt + float(tm - 1)) * (1.0 / tm)) * float(tm)
        passed = jnp.where(tile_start >= off, 1.0, 0.0)
        te = te + passed
        nonempty.append(jnp.where(cnt > 0.0, 1.0, 0.0))
        seg = seg + nonempty[e] * passed
    for kk in range(2):
        pos_ref[kk] = pos[kk].astype(jnp.int32)
        wt_ref[kk] = wts[kk]
    te = jnp.minimum(te, float(N_EXPERTS - 1))
    nxt = jnp.full(tiles, -1.0, F32)
    for e in reversed(range(N_EXPERTS)):
        nxt = jnp.where(jnp.logical_and(nonempty[e] > 0.0, te < float(e)), float(e), nxt)
    nxt2 = jnp.full(tiles, -1.0, F32)
    for e in reversed(range(N_EXPERTS)):
        later = jnp.logical_and(nxt >= 0.0, nxt < float(e))
        nxt2 = jnp.where(jnp.logical_and(nonempty[e] > 0.0, later), float(e), nxt2)
    te_ref[0:1, :] = te.astype(jnp.int32)
    te_ref[1:2, :] = seg.astype(jnp.int32)
    te_ref[2:3, :] = nxt.astype(jnp.int32)
    te_ref[3:4, :] = nxt2.astype(jnp.int32)
    te_ref[4:, :] = jnp.zeros((te_ref.shape[0] - 4, te_ref.shape[1]), jnp.int32)
    nv_ref[...] = jnp.broadcast_to(off * (1.0 / tm), nv_ref.shape).astype(jnp.int32)


def _route_call(lgt, b_router, tm):
    n_tok = lgt.shape[1]
    r = n_tok // 128
    assert r * 128 == n_tok and r % 8 == 0 and 2 * n_tok // tm + N_EXPERTS <= MOE_TILE_SLOTS
    lg3 = lgt.reshape(N_EXPERTS, r, 128)
    full3 = lambda k: pl.BlockSpec((k, r, 128), lambda: (0, 0, 0))
    pos, wts, te, nv = pl.pallas_call(
        functools.partial(_route_kernel, tm=tm),
        in_specs=[pl.BlockSpec(memory_space=pltpu.SMEM), full3(N_EXPERTS)],
        out_specs=(full3(2), full3(2), pl.BlockSpec((MOE_TILE_ROWS, MOE_TILE_SLOTS), lambda: (0, 0)), pl.BlockSpec((1, 128), lambda: (0, 0))),
        out_shape=(
            jax.ShapeDtypeStruct((2, r, 128), jnp.int32),
            jax.ShapeDtypeStruct((2, r, 128), F32),
            jax.ShapeDtypeStruct((MOE_TILE_ROWS, MOE_TILE_SLOTS), jnp.int32),
            jax.ShapeDtypeStruct((1, 128), jnp.int32),
        ),
        compiler_params=pltpu.CompilerParams(vmem_limit_bytes=VMEM_LIMIT),
        name="route",
    )(b_router, lg3)
    return pos.reshape(2, n_tok), wts, te[:MOE_META].reshape(MOE_META * MOE_TILE_SLOTS), nv[0, :1]


SC_CORES = 2
SC_SUBCORES = 16
SC_WORKERS = SC_CORES * SC_SUBCORES
SC_CHUNK = 32


def _sc_mesh():
    return plsc.VectorSubcoreMesh(core_axis_name="c", subcore_axis_name="s",
                                  num_cores=SC_CORES, num_subcores=SC_SUBCORES)


def _sc_steps(n_rows):
    per_w = n_rows // SC_WORKERS
    steps = per_w // SC_CHUNK
    assert per_w * SC_WORKERS == n_rows and steps * SC_CHUNK == per_w and steps % 2 == 0, n_rows
    return per_w, steps


def _sc_gather_rows(table, idx):
    p = idx.shape[0]
    d = table.shape[1]
    per_w, steps = _sc_steps(p)
    idx3 = idx.reshape(SC_WORKERS, steps, SC_CHUNK)

    @functools.partial(
        pl.kernel, mesh=_sc_mesh(),
        out_type=jax.ShapeDtypeStruct((p, d), table.dtype),
        scratch_types=[
            pltpu.VMEM((steps, SC_CHUNK), jnp.int32),
            pltpu.VMEM((SC_CHUNK, d), table.dtype),
            pltpu.VMEM((SC_CHUNK, d), table.dtype),
            pltpu.SemaphoreType.DMA, pltpu.SemaphoreType.DMA,
            pltpu.SemaphoreType.DMA, pltpu.SemaphoreType.DMA,
        ],
        name="sc_gather_rows",
    )
    def k(table_hbm, idx_hbm, out_hbm, idx_v, buf0, buf1, g0, g1, w0, w1):
        wid = lax.axis_index("s") * SC_CORES + lax.axis_index("c")
        base = wid * per_w
        pltpu.sync_copy(idx_hbm.at[wid], idx_v)

        def gather(s, buf, sem):
            return pltpu.make_async_copy(table_hbm.at[idx_v.at[s]], buf, sem)

        def write(s, buf, sem):
            return pltpu.make_async_copy(buf, out_hbm.at[pl.ds(base + s * SC_CHUNK, SC_CHUNK)], sem)

        gather(0, buf0, g0).start()

        @pl.loop(0, steps, step=2)
        def _(s):
            gather(s + 1, buf1, g1).start()
            gather(s, buf0, g0).wait()
            write(s, buf0, w0).start()
            write(s, buf0, w0).wait()

            @pl.when(s + 2 < steps)
            def _():
                gather(s + 2, buf0, g0).start()

            gather(s + 1, buf1, g1).wait()
            write(s + 1, buf1, w1).start()
            write(s + 1, buf1, w1).wait()

    return k(table, idx3)


def _sc_dispatch(srcs, poss, p_rows):
    d = srcs[0].shape[1]
    dt = srcs[0].dtype
    plans = [_sc_steps(src.shape[0]) for src in srcs]
    idxs = [pos.reshape(2, SC_WORKERS, st, SC_CHUNK) for pos, (_, st) in zip(poss, plans)]
    nseg = len(srcs)
    scratch = [pltpu.VMEM((2, st, SC_CHUNK), jnp.int32) for _, st in plans]
    scratch += [pltpu.VMEM((SC_CHUNK, d), dt), pltpu.VMEM((SC_CHUNK, d), dt)]
    scratch += [pltpu.SemaphoreType.DMA] * 6

    @functools.partial(
        pl.kernel, mesh=_sc_mesh(),
        out_type=jax.ShapeDtypeStruct((p_rows, d), dt),
        scratch_types=scratch,
        name="sc_dispatch",
    )
    def k(*refs):
        src_hbm = refs[:nseg]
        idx_hbm = refs[nseg:2 * nseg]
        out_hbm = refs[2 * nseg]
        idx_v = refs[2 * nseg + 1:3 * nseg + 1]
        buf0, buf1, r0, r1, a0, a1, b0, b1 = refs[3 * nseg + 1:]
        wid = lax.axis_index("s") * SC_CORES + lax.axis_index("c")
        for seg in range(nseg):
            per_w, steps = plans[seg]
            base = wid * per_w
            for kk in range(2):
                pltpu.sync_copy(idx_hbm[seg].at[kk, wid], idx_v[seg].at[kk])

            def read(s, buf, sem, seg=seg, base=base):
                return pltpu.make_async_copy(src_hbm[seg].at[pl.ds(base + s * SC_CHUNK, SC_CHUNK)], buf, sem)

            def scat(kk, s, buf, sem, seg=seg):
                return pltpu.make_async_copy(buf, out_hbm.at[idx_v[seg].at[kk, s]], sem)

            read(0, buf0, r0).start()

            @pl.loop(0, steps, step=2)
            def _(s, read=read, scat=scat, steps=steps):
                read(s + 1, buf1, r1).start()
                read(s, buf0, r0).wait()
                scat(0, s, buf0, a0).start()
                scat(1, s, buf0, b0).start()
                scat(0, s, buf0, a0).wait()
                scat(1, s, buf0, b0).wait()

                @pl.when(s + 2 < steps)
                def _():
                    read(s + 2, buf0, r0).start()

                read(s + 1, buf1, r1).wait()
                scat(0, s + 1, buf1, a1).start()
                scat(1, s + 1, buf1, b1).start()
                scat(0, s + 1, buf1, a1).wait()
                scat(1, s + 1, buf1, b1).wait()

    return k(*srcs, *idxs)


MOE_TM = 512


MOE_WSLOTS = 4


def _experts_kernel(te_ref, nv_ref, xs_ref, wg_hbm, wu_hbm, wd_hbm, ys_ref, wg_v, wu_v, wd_v, sem, *, layer):
    i = pl.program_id(0)
    nv = nv_ref[0]
    rows = MOE_TM

    def weight_copies(expert, s):
        return (pltpu.make_async_copy(wg_hbm.at[layer, expert], wg_v.at[s], sem.at[s, 0]),
                pltpu.make_async_copy(wu_hbm.at[layer, expert], wu_v.at[s], sem.at[s, 1]),
                pltpu.make_async_copy(wd_hbm.at[layer, expert], wd_v.at[s], sem.at[s, 2]))

    def start(expert, s):
        for cp in weight_copies(expert, s):
            cp.start()

    experts, slots, valids = [], [], []
    for k in range(2):
        t = 2 * i + k
        e = te_ref[t]
        run = te_ref[MOE_TILE_SLOTS + t]
        nxt1 = te_ref[2 * MOE_TILE_SLOTS + t]
        nxt2 = te_ref[3 * MOE_TILE_SLOTS + t]
        slot = lax.rem(run, MOE_WSLOTS)
        valid = t < nv
        first = jnp.logical_or(t == 0, e != te_ref[jnp.maximum(t - 1, 0)])
        if k == 0:
            @pl.when(jnp.logical_and(valid, i == 0))
            def _(e=e, nxt1=nxt1):
                start(e, 0)

                @pl.when(nxt1 >= 0)
                def _():
                    start(nxt1, 1)

        @pl.when(jnp.logical_and(valid, first))
        def _(e=e, slot=slot, run=run, nxt2=nxt2):
            for cp in weight_copies(e, slot):
                cp.wait()

            @pl.when(nxt2 >= 0)
            def _():
                start(nxt2, lax.rem(run + 2, MOE_WSLOTS))

        experts.append(e)
        slots.append(slot)
        valids.append(valid)

    def mlp(packed, slot):
        h = _unpack_bf16_pairs(packed).astype(BF16)
        half = D_EXPERT // 2
        y = None
        for j in range(2):
            sl = slice(j * half, (j + 1) * half)
            a = jnp.dot(h, wg_v[slot, :, sl].astype(BF16), preferred_element_type=F32)
            u = jnp.dot(h, wu_v[slot, :, sl].astype(BF16), preferred_element_type=F32)
            t = ((a * _sigmoid(a)) * u).astype(BF16)
            yj = jnp.dot(t, wd_v[slot, sl, :].astype(BF16), preferred_element_type=F32)
            y = yj if y is None else y + yj
        return _pack_bf16_pairs(y)

    same = jnp.logical_and(valids[1], experts[0] == experts[1])

    @pl.when(jnp.logical_and(valids[0], same))
    def _():
        ys_ref[...] = mlp(xs_ref[...], slots[0])

    @pl.when(jnp.logical_and(valids[0], jnp.logical_not(same)))
    def _():
        ys_ref[0:rows, :] = mlp(xs_ref[0:rows, :], slots[0])

    @pl.when(jnp.logical_and(valids[1], jnp.logical_not(same)))
    def _():
        ys_ref[rows:, :] = mlp(xs_ref[rows:, :], slots[1])


def _experts_call(xs, te, nv, wg, wu, wd, layer):
    p_rows, dh = xs.shape
    d = 2 * dh
    tm = 2 * MOE_TM
    nt = p_rows // tm
    assert nt * tm == p_rows and 2 * nt <= MOE_TILE_SLOTS and te.shape == (MOE_META * MOE_TILE_SLOTS,)
    row = lambda i, te_r, nv_r: (jnp.minimum(i, (nv_r[0] + 1) // 2 - 1), 0)
    grid_spec = pltpu.PrefetchScalarGridSpec(
        num_scalar_prefetch=2,
        grid=(nt,),
        in_specs=[
            pl.BlockSpec((tm, dh), row),
            pl.BlockSpec(memory_space=pl.ANY),
            pl.BlockSpec(memory_space=pl.ANY),
            pl.BlockSpec(memory_space=pl.ANY),
        ],
        out_specs=pl.BlockSpec((tm, dh), row),
        scratch_shapes=[
            pltpu.VMEM((MOE_WSLOTS, d, D_EXPERT), F32),
            pltpu.VMEM((MOE_WSLOTS, d, D_EXPERT), F32),
            pltpu.VMEM((MOE_WSLOTS, D_EXPERT, d), F32),
            pltpu.SemaphoreType.DMA((MOE_WSLOTS, 3)),
        ],
    )
    return pl.pallas_call(
        functools.partial(_experts_kernel, layer=layer),
        grid_spec=grid_spec,
        out_shape=jax.ShapeDtypeStruct((p_rows, dh), jnp.int32),
        compiler_params=_cparams(("arbitrary",)),
        name="moe_experts",
    )(te, nv, xs, wg, wu, wd)


COMBINE_TM = 1024


def _combine_kernel(y_ref, wt_ref, x_ref, mod_ref, fn_ref, o_ref, *, final):
    xo = x_ref[...] + mod_ref[5:6, :] * _moe_mix(y_ref, wt_ref[0], wt_ref[1])
    if final:
        ms = jnp.mean(xo * xo, axis=-1, keepdims=True)
        xo = xo * lax.rsqrt(ms + EPS) * fn_ref[...]
    o_ref[...] = xo


def _combine_call(y2, wts3, row0, x, mod_l, mod_row, fn, final):
    b, n, d = x.shape
    n_tok = b * n
    tm = COMBINE_TM
    assert n_tok % tm == 0 and row0 % 8 == 0 and (mod_row is not None or n % tm == 0)
    if mod_row is None:
        mod_map = lambda i: ((i * tm) // n, 0, 0)
    else:
        mod_map = lambda i: (mod_row, 0, 0)
    out = pl.pallas_call(
        functools.partial(_combine_kernel, final=final),
        grid=(n_tok // tm,),
        in_specs=[
            pl.BlockSpec((2, tm, d // 2), lambda i: (0, i, 0)),
            pl.BlockSpec((2, tm // 128, 128), lambda i: (0, row0 // 8 + i, 0)),
            pl.BlockSpec((tm, d), lambda i: (i, 0)),
            pl.BlockSpec((None, 6, d), mod_map),
            pl.BlockSpec((1, d), lambda i: (0, 0)),
        ],
        out_specs=pl.BlockSpec((tm, d), lambda i: (i, 0)),
        out_shape=jax.ShapeDtypeStruct((n_tok, d), F32),
        compiler_params=_cparams(("parallel",)),
        name="moe_combine",
    )(y2, wts3, x.reshape(n_tok, d), mod_l, fn)
    return out.reshape(b, n, d)


def _moe_sparse(h_list, lg_list, x_list, mod_l, mod_rows, b_router, wg, wu, wd, layer, fn, final):
    d = h_list[0].shape[-1]
    sizes = [h.shape[0] * h.shape[1] for h in h_list]
    n_tok = sum(sizes)
    lgt = lg_list[0] if len(lg_list) == 1 else jnp.concatenate(lg_list, axis=1)
    pos, wts, te, nv = _route_call(lgt, b_router, MOE_TM)
    p_rows = 2 * n_tok + N_EXPERTS * MOE_TM
    offs = np.cumsum([0] + sizes)
    poss = [pos[:, offs[i]:offs[i + 1]] for i in range(len(sizes))]
    xs = _sc_dispatch([h.reshape(-1, d) for h in h_list], poss, p_rows)
    ys = _experts_call(xs, te, nv, wg, wu, wd, layer)
    outs = []
    for i, x in enumerate(x_list):
        y2 = _sc_gather_rows(ys, poss[i].reshape(-1)).reshape(2, sizes[i], d)
        row0 = int(offs[i]) // 128
        if final:
            outs.append(_combine_call(y2, wts, row0, x, mod_l, mod_rows[i], fn, True))
        else:
            outs.append((y2, wts, row0, mod_l))
    return outs


def _winprep_kernel(wt_ref, o_ref):
    gd0 = C_AQ
    gdw = 2 * GLA_GATE_RANK
    tail = W_IN_REF_COLS - gd0 - gdw
    o_ref[:, 0:gd0] = wt_ref[0:gd0, :].T.astype(BF16)
    o_ref[:, gd0:gd0 + tail] = wt_ref[gd0 + gdw:W_IN_REF_COLS, :].T.astype(BF16)
    gd = wt_ref[gd0:gd0 + 128, :].T
    lane = lax.broadcasted_iota(jnp.int32, gd.shape, 1)
    o_ref[:, C_GD:] = jnp.where(lane < gdw, gd, 0.0).astype(BF16)


def _winprep_call(w_in):
    depth, d, cols = w_in.shape
    assert cols == W_IN_REF_COLS and C_GD == cols - 2 * GLA_GATE_RANK and W_IN_COLS - C_GD == 128
    return pl.pallas_call(
        _winprep_kernel,
        grid=(depth,),
        in_specs=[pl.BlockSpec((None, cols, d), lambda l: (l, 0, 0))],
        out_specs=pl.BlockSpec((None, d, W_IN_COLS), lambda l: (l, 0, 0)),
        out_shape=jax.ShapeDtypeStruct((depth, d, W_IN_COLS), BF16),
        compiler_params=_cparams(("parallel",)),
        name="w_in_prep",
    )(jnp.swapaxes(w_in, 1, 2))


def _gate_up_weights(w_up, b_up):
    z = jnp.zeros((GLA_GATE_RANK, GLA_QK), w_up.dtype)
    top = jnp.concatenate([w_up[0], z], axis=1)
    mid = jnp.concatenate([z, w_up[1]], axis=1)
    pad = jnp.zeros((128 - 2 * GLA_GATE_RANK, 2 * GLA_QK), w_up.dtype)
    return jnp.concatenate([top, mid, pad], axis=0).astype(BF16), b_up.reshape(1, 2 * GLA_QK)


def kernel(x, c, ctx, c_ctx, w_ada, b_ada, norm_mix, norm_ffn, w_in, w_gla_gate_up, b_gla_gate, gla_norm, q_norm,
           k_norm, w_out, w_router, b_router, w_exp_gate, w_exp_up, w_exp_down, final_norm):
    b, n, d = x.shape
    m = ctx.shape[1]
    depth = w_ada.shape[0]
    assert d == D_MODEL and n % GLA_PAIR == 0 and m % GLA_PAIR == 0 and n % GRID_W == 0

    rows = ((b + 1 + 7) // 8) * 8
    cv = jnp.concatenate([c, c_ctx[None, :], jnp.zeros((rows - b - 1, d), F32)], axis=0)
    mod = _ada_call(cv, w_ada, b_ada).reshape(depth, rows, 6, d)

    cs = jnp.asarray(_channel_dft_table()).astype(BF16)
    tab_x = jnp.asarray(_seq_dft_table(n)).astype(BF16)
    tab_c = jnp.asarray(_seq_dft_table(m)).astype(BF16)
    rope_tabs = tuple(jnp.asarray(t) for t in _rope_tables(n))
    bd512 = jnp.asarray(_blockdiag_ones(ATT_WIDTH, ATT_HDIM)).astype(BF16)
    bd256 = jnp.asarray(_blockdiag_ones(GLA_WIDTH, GLA_DV)).astype(BF16)
    wr_t = w_router.T
    wr_hi = wr_t.astype(BF16)
    wrh = jnp.concatenate([wr_hi, (wr_t - wr_hi.astype(F32)).astype(BF16)], axis=0)
    fn = final_norm.reshape(1, d)

    w_in_perm = _winprep_call(w_in)
    w_out_bf = w_out.astype(BF16)

    xc = ctx
    pend_x = pend_c = None
    for l in range(depth):
        ctx_out = l < depth - 1
        mod_l = mod[l]
        w_perm = (w_in_perm, l)
        wup, bup = _gate_up_weights(w_gla_gate_up[l], b_gla_gate[l])
        nw = norm_mix[l].reshape(1, d)
        nf = norm_ffn[l].reshape(1, d)
        qn = jnp.tile(q_norm[l], ATT_HEADS).reshape(1, ATT_WIDTH)
        kn = jnp.tile(k_norm[l], ATT_KV_HEADS).reshape(1, 128)
        gn = jnp.tile(gla_norm[l], GLA_HEADS).reshape(1, GLA_WIDTH)
        wo = (w_out_bf, l)

        pc = _inproj_call(xc, mod_l, b, nw, w_perm, cs, wup, bup, qn, kn, bd512, None, pend_c, full=ctx_out)
        px = _inproj_call(x, mod_l, None, nw, w_perm, cs, wup, bup, qn, kn, bd512, rope_tabs, pend_x)
        if pend_x is not None:
            x, xc = px["x"], pc["x"]

        s_zero = jnp.zeros((b, 2, GLA_WIDTH, GLA_QK), F32)
        of_c, ob_c, s_fin = _gla_call(pc["gqk"], pc["gv"], pc["gvt"], pc["la"], s_zero)
        of_x, ob_x, _ = _gla_call(px["gqk"], px["gv"], px["gvt"], px["la"], s_fin)

        kv_c = (pc["kt"], pc["vd"])
        f_x = _seqdft_call(tab_x, px["ab"].reshape(2 * n, b * FNET_WIDTH), b)
        x, h2_x, lg_x = _attn_outproj_call(px["q"], [kv_c, (px["kt"], px["vd"])], f_x, of_x, ob_x, px["gg"], x,
                                           mod_l, None, wo, gn, bd256, nf, wrh)

        if ctx_out:
            f_c = _seqdft_call(tab_c, pc["ab"].reshape(2 * m, b * FNET_WIDTH), b)
            xc, h2_c, lg_c = _attn_outproj_call(pc["q"], [kv_c], f_c, of_c, ob_c, pc["gg"], xc,
                                                mod_l, b, wo, gn, bd256, nf, wrh)

        final = l == depth - 1
        wexp = (w_exp_gate, w_exp_up, w_exp_down, l)
        if ctx_out:
            res = _moe_sparse([h2_x, h2_c], [lg_x, lg_c], [x, xc], mod_l, [None, b], b_router, *wexp, fn, final)
        else:
            res = _moe_sparse([h2_x], [lg_x], [x], mod_l, [None], b_router, *wexp, fn, final)
        if final:
            x = res[0]
        else:
            pend_x, pend_c = res
    return x
```

```python
import functools

import numpy as np
import jax
import jax.numpy as jnp
from jax import lax
from jax.experimental import pallas as pl
from jax.experimental.pallas import tpu as pltpu
from jax.experimental.pallas import tpu_sc as plsc

F32 = jnp.float32
BF16 = jnp.bfloat16

D_MODEL = 1024
GRID_W = 64
EPS = 1e-6
LOG2E = 1.4426950408889634

FNET_WIDTH = 256
FNET_GROUPS = 4
FNET_GDIM = 64

GLA_HEADS = 4
GLA_DV = 64
GLA_DK = 32
GLA_WIDTH = 256
GLA_QK = 128
GLA_GATE_RANK = 16
GLA_GATE_NORM = 16.0
GLA_CHUNK = 64
GLA_PAIR = 2 * GLA_CHUNK

ATT_HEADS = 8
ATT_KV_HEADS = 2
ATT_HDIM = 64
ATT_WIDTH = 512
ROPE_FREQS = 16
ROPE_THETA = 10000.0

N_EXPERTS = 16
N_GROUPS = 4
EXPERTS_PER_GROUP = 4
D_EXPERT = 512

C_U = 0
C_GQ = 256
C_GK = 384
C_GV = 512
C_GG = 768
C_AQ = 1024
C_AK = 1536
C_GD = 1792
W_IN_COLS = 1920
W_IN_REF_COLS = 1824

VMEM_LIMIT = 56 * 1024 * 1024


def _cparams(sem):
    return pltpu.CompilerParams(dimension_semantics=sem, vmem_limit_bytes=VMEM_LIMIT)


def _sigmoid(x):
    return 1.0 / (1.0 + jnp.exp(-x))


def _pack_bf16_pairs(x):
    blocks = []
    for t in range(x.shape[1] // 256):
        lo = lax.bitcast_convert_type(x[:, 256 * t:256 * t + 128].astype(BF16).astype(F32), jnp.uint32)
        hi = lax.bitcast_convert_type(x[:, 256 * t + 128:256 * t + 256].astype(BF16).astype(F32), jnp.uint32)
        blocks.append((lo >> 16) | (hi & jnp.uint32(0xFFFF0000)))
    return lax.bitcast_convert_type(jnp.concatenate(blocks, axis=1), jnp.int32)


def _unpack_bf16_pairs(p):
    u = lax.bitcast_convert_type(p, jnp.uint32)
    blocks = []
    for t in range(p.shape[1] // 128):
        word = u[:, 128 * t:128 * (t + 1)]
        blocks += [lax.bitcast_convert_type(word << 16, F32),
                   lax.bitcast_convert_type(word & jnp.uint32(0xFFFF0000), F32)]
    return jnp.concatenate(blocks, axis=1)


def _token_columns(w):
    nr = w.shape[0]
    tm = nr * 128
    lane = lax.broadcasted_iota(jnp.int32, (tm, 128), 1)
    row = lax.broadcasted_iota(jnp.int32, (tm, 128), 0)
    wb = jnp.concatenate([jnp.broadcast_to(w[r:r + 1, :], (128, 128)) for r in range(nr)], axis=0)
    return jnp.sum(jnp.where(lane == (row % 128), wb, 0.0), axis=1, keepdims=True)


def _moe_mix(y_ref, w0, w1):
    return (_token_columns(w0) * _unpack_bf16_pairs(y_ref[0])
            + _token_columns(w1) * _unpack_bf16_pairs(y_ref[1]))


def _nt_dot(a, b):
    return lax.dot_general(a, b, (((1,), (1,)), ((), ())), preferred_element_type=F32)


@functools.lru_cache(maxsize=None)
def _channel_dft_table():
    j = np.arange(FNET_GDIM)
    ang = 2.0 * np.pi * ((j[:, None] * j[None, :]) % FNET_GDIM) / FNET_GDIM
    c = np.cos(ang) / np.sqrt(FNET_GDIM)
    s = np.sin(ang) / np.sqrt(FNET_GDIM)
    out = np.zeros((FNET_WIDTH, 2 * FNET_WIDTH), np.float64)
    for g in range(FNET_GROUPS):
        sl = slice(g * FNET_GDIM, (g + 1) * FNET_GDIM)
        out[sl, sl] = c
        out[sl, FNET_WIDTH + g * FNET_GDIM:FNET_WIDTH + (g + 1) * FNET_GDIM] = s
    return out.astype(np.float32)


@functools.lru_cache(maxsize=None)
def _seq_dft_table(n):
    j = np.arange(n, dtype=np.int64)
    ang = 2.0 * np.pi * ((j[:, None] * j[None, :]) % n) / n
    return np.concatenate([np.cos(ang), -np.sin(ang)], axis=1).astype(np.float32) / np.float32(np.sqrt(n))


@functools.lru_cache(maxsize=None)
def _rope_tables(n):
    rows = n // GRID_W
    row = np.repeat(np.arange(rows), GRID_W).astype(np.float64)
    col = np.tile(np.arange(GRID_W), rows).astype(np.float64)
    inv = ROPE_THETA ** (-np.arange(ROPE_FREQS, dtype=np.float64) * 2.0 / (2 * ROPE_FREQS))
    ar = row[:, None] * inv[None, :]
    ac = col[:, None] * inv[None, :]
    cos = np.concatenate([np.cos(ar), np.cos(ar), np.cos(ac), np.cos(ac)], axis=1)
    sin = np.concatenate([-np.sin(ar), np.sin(ar), -np.sin(ac), np.sin(ac)], axis=1)
    return (np.tile(cos, (1, 2)).astype(np.float32), np.tile(sin, (1, 2)).astype(np.float32))


@functools.lru_cache(maxsize=None)
def _blockdiag_ones(width, blk):
    i = np.arange(width)
    return (i[:, None] // blk == i[None, :] // blk).astype(np.float32)


def _ada_kernel(cv_ref, w_ref, b_ref, o_ref):
    cv = cv_ref[...]
    a = (cv * _sigmoid(cv)).astype(BF16)
    o_ref[...] = jnp.dot(a, w_ref[...].astype(BF16), preferred_element_type=F32) + b_ref[...]


def _ada_call(cv, w_ada, b_ada):
    depth, d, d6 = w_ada.shape
    tn = 1536
    rows = cv.shape[0]
    return pl.pallas_call(
        _ada_kernel,
        grid=(depth, d6 // tn),
        in_specs=[
            pl.BlockSpec((rows, d), lambda l, j: (0, 0)),
            pl.BlockSpec((None, d, tn), lambda l, j: (l, 0, j)),
            pl.BlockSpec((None, 1, tn), lambda l, j: (l, 0, j)),
        ],
        out_specs=pl.BlockSpec((None, rows, tn), lambda l, j: (l, 0, j)),
        out_shape=jax.ShapeDtypeStruct((depth, rows, d6), F32),
        compiler_params=_cparams(("parallel", "parallel")),
        name="ada_mod",
    )(cv, w_ada, b_ada.reshape(depth, 1, d6))


def _swap16(x):
    lane = lax.broadcasted_iota(jnp.int32, x.shape, 1)
    first = (lane % 32) < 16
    return jnp.where(first, pltpu.roll(x, 112, 1), pltpu.roll(x, 16, 1))


def _head_rms(x, bd, w):
    ms = jnp.dot((x * x).astype(BF16), bd, preferred_element_type=F32) * (1.0 / ATT_HDIM)
    return x * lax.rsqrt(ms + EPS) * w


def _inproj_kernel(*refs, rope, pending_rows, names):
    refs = list(refs)
    x_ref, mod_ref, nw_ref, w_ref, cs_ref, wup_ref, bup_ref, qn_ref, kn_ref, bd_ref = refs[:10]
    del refs[:10]
    if rope:
        cos_ref, sin_ref = refs[:2]
        del refs[:2]
    if pending_rows is not None:
        y_ref, wt_ref, modp_ref = refs[:3]
        del refs[:3]
    out = dict(zip(names, refs))
    full = "q" in out
    x = x_ref[...]
    if pending_rows is not None:
        xnew_ref = out["x"]
        row0, rows_per_sample = pending_rows
        nr = x.shape[0] // 128
        r = row0 + pl.program_id(0) * rows_per_sample + pl.program_id(1) * nr
        sub = lax.rem(r, 8)
        w = [wt_ref[kk, 0:nr, :] for kk in range(2)]
        for blk in range(1, 8 // nr):
            w = [jnp.where(sub == blk * nr, wt_ref[kk, blk * nr:(blk + 1) * nr, :], w[kk]) for kk in range(2)]
        x = x + modp_ref[5:6, :] * _moe_mix(y_ref, w[0], w[1])
        xnew_ref[...] = x
    ms = jnp.mean(x * x, axis=-1, keepdims=True)
    y = x * lax.rsqrt(ms + EPS) * nw_ref[...]
    h = y * (1.0 + mod_ref[1:2, :]) + mod_ref[0:1, :]
    hb = h.astype(BF16)

    def proj(c0, width):
        return jnp.dot(hb, w_ref[:, c0:c0 + width], preferred_element_type=F32)

    if full:
        uab = jnp.dot(proj(C_U, FNET_WIDTH).astype(BF16), cs_ref[...], preferred_element_type=F32)
        out["ab"][0] = uab[:, :FNET_WIDTH].astype(BF16)
        out["ab"][1] = uab[:, FNET_WIDTH:].astype(BF16)

    out["gqk"][...] = proj(C_GQ, 2 * GLA_QK)
    gv = proj(C_GV, GLA_WIDTH)
    out["gv"][...] = gv.astype(BF16)
    out["gvt"][...] = gv.T.astype(BF16)
    if full:
        out["gg"][...] = proj(C_GG, GLA_WIDTH).astype(BF16)
    pre = jnp.dot(proj(C_GD, 128).astype(BF16), wup_ref[...], preferred_element_type=F32) + bup_ref[...]
    out["la"][...] = (jnp.minimum(pre, 0.0) - jnp.log1p(jnp.exp(-jnp.abs(pre)))) * (1.0 / GLA_GATE_NORM)

    bd = bd_ref[...]
    kv = proj(C_AK, 256)
    k = _head_rms(kv[:, :128], bd[:128, :128], kn_ref[...])
    if rope:
        cos = cos_ref[...]
        sin = sin_ref[...]
        k = k * cos + _swap16(k) * sin
    if full:
        q = _head_rms(proj(C_AQ, ATT_WIDTH), bd, qn_ref[...])
        if rope:
            q = jnp.concatenate(
                [q[:, s:s + 128] * cos + _swap16(q[:, s:s + 128]) * sin for s in range(0, ATT_WIDTH, 128)], axis=1)
        out["q"][...] = (q * (ATT_HDIM ** -0.5 * LOG2E)).astype(BF16)
    kt_ref, vd_ref = out["kt"], out["vd"]
    v = kv[:, 128:]
    lo = lax.broadcasted_iota(jnp.int32, k.shape, 1) < ATT_HDIM
    k_sw = pltpu.roll(k, ATT_HDIM, 1)
    v_sw = pltpu.roll(v, ATT_HDIM, 1)
    kt_ref[0] = jnp.where(lo, k, k_sw).T.astype(BF16)
    kt_ref[1] = jnp.where(lo, k_sw, k).T.astype(BF16)
    vd_ref[0] = jnp.where(lo, v, 1.0).astype(BF16)
    vd_ref[1] = jnp.where(lo, 1.0, v_sw).astype(BF16)
    vd_ref[2] = jnp.where(lo, v_sw, 1.0).astype(BF16)
    vd_ref[3] = jnp.where(lo, 1.0, v).astype(BF16)


def _inproj_call(x, mod_l, mod_row, nw, w_perm, cs, wup, bup, qn, kn, bd, rope_tabs, pending=None, full=True):
    b, n, d = x.shape
    tm = min(1024, n)
    nt = n // tm
    rope = rope_tabs is not None
    if mod_row is None:
        mod_map = lambda bi, i: (bi, 0, 0)
    else:
        mod_map = lambda bi, i: (mod_row, 0, 0)
    const = lambda bi, i: (0, 0)
    in_specs = [
        pl.BlockSpec((None, tm, d), lambda bi, i: (bi, i, 0)),
        pl.BlockSpec((None, 6, d), mod_map),
        pl.BlockSpec((1, d), const),
        pl.BlockSpec((None, d, W_IN_COLS), lambda bi, i: (w_perm[1], 0, 0)),
        pl.BlockSpec((FNET_WIDTH, 2 * FNET_WIDTH), const),
        pl.BlockSpec((128, 2 * GLA_QK), const),
        pl.BlockSpec((1, 2 * GLA_QK), const),
        pl.BlockSpec((1, ATT_WIDTH), const),
        pl.BlockSpec((1, 128), const),
        pl.BlockSpec((ATT_WIDTH, ATT_WIDTH), const),
    ]
    args = [x, mod_l, nw, w_perm[0], cs, wup, bup, qn, kn, bd]
    if rope:
        in_specs += [pl.BlockSpec((tm, 128), lambda bi, i: (i, 0)), pl.BlockSpec((tm, 128), lambda bi, i: (i, 0))]
        args += list(rope_tabs)
    pending_rows = None
    if pending is not None:
        y2, wts3, row0, mod_prev = pending
        nr = tm // 128
        rps = n // 128
        assert 8 % nr == 0 and row0 % nr == 0 and rps % nr == 0
        pending_rows = (row0, rps)
        in_specs += [
            pl.BlockSpec((2, None, tm, d // 2), lambda bi, i: (0, bi, i, 0)),
            pl.BlockSpec((2, 8, 128), lambda bi, i: (0, (row0 + bi * rps + i * nr) // 8, 0)),
            pl.BlockSpec((None, 6, d), mod_map),
        ]
        args += [y2.reshape(2, b, n, d // 2), wts3, mod_prev]
    tok = lambda w: pl.BlockSpec((None, tm, w), lambda bi, i: (bi, i, 0))
    outs = []
    if full:
        outs.append(("ab", jax.ShapeDtypeStruct((2, n, b * FNET_WIDTH), BF16),
                     pl.BlockSpec((2, tm, FNET_WIDTH), lambda bi, i: (0, i, bi))))
    outs += [
        ("gqk", jax.ShapeDtypeStruct((b, n, 2 * GLA_QK), F32), tok(2 * GLA_QK)),
        ("gv", jax.ShapeDtypeStruct((b, n, GLA_WIDTH), BF16), tok(GLA_WIDTH)),
        ("gvt", jax.ShapeDtypeStruct((b, GLA_WIDTH, n), BF16),
         pl.BlockSpec((None, GLA_WIDTH, tm), lambda bi, i: (bi, 0, i))),
    ]
    if full:
        outs.append(("gg", jax.ShapeDtypeStruct((b, n, GLA_WIDTH), BF16), tok(GLA_WIDTH)))
    outs.append(("la", jax.ShapeDtypeStruct((b, n, 2 * GLA_QK), F32), tok(2 * GLA_QK)))
    if full:
        outs.append(("q", jax.ShapeDtypeStruct((b, n, ATT_WIDTH), BF16), tok(ATT_WIDTH)))
    outs += [
        ("kt", jax.ShapeDtypeStruct((b, ATT_KV_HEADS, 128, n), BF16),
         pl.BlockSpec((None, ATT_KV_HEADS, 128, tm), lambda bi, i: (bi, 0, 0, i))),
        ("vd", jax.ShapeDtypeStruct((b, 2 * ATT_KV_HEADS, n, 128), BF16),
         pl.BlockSpec((None, 2 * ATT_KV_HEADS, tm, 128), lambda bi, i: (bi, 0, i, 0))),
    ]
    if pending is not None:
        outs.append(("x", jax.ShapeDtypeStruct((b, n, d), F32), tok(d)))
    names = tuple(o[0] for o in outs)
    res = pl.pallas_call(
        functools.partial(_inproj_kernel, rope=rope, pending_rows=pending_rows, names=names),
        grid=(b, nt),
        in_specs=in_specs,
        out_specs=tuple(o[2] for o in outs),
        out_shape=tuple(o[1] for o in outs),
        compiler_params=_cparams(("parallel", "parallel")),
        name="inproj_rope" if rope else "inproj_ctx",
    )(*args)
    return dict(zip(names, res))


def _seqdft_kernel(t_ref, ab_ref, o_ref):
    y = jnp.dot(t_ref[...], ab_ref[...], preferred_element_type=F32)
    for bb in range(o_ref.shape[0]):
        o_ref[bb] = y[:, bb * FNET_WIDTH:(bb + 1) * FNET_WIDTH].astype(BF16)


def _seqdft_call(table, ab, b):
    n = table.shape[0]
    tm = min(512, n)
    nb = 4 if b % 4 == 0 else (2 if b % 2 == 0 else 1)
    return pl.pallas_call(
        _seqdft_kernel,
        grid=(b // nb, n // tm),
        in_specs=[
            pl.BlockSpec((tm, 2 * n), lambda c, i: (i, 0)),
            pl.BlockSpec((2 * n, nb * FNET_WIDTH), lambda c, i: (0, c)),
        ],
        out_specs=pl.BlockSpec((nb, tm, FNET_WIDTH), lambda c, i: (c, i, 0)),
        out_shape=jax.ShapeDtypeStruct((b, n, FNET_WIDTH), BF16),
        compiler_params=_cparams(("parallel", "parallel")),
        name="seq_dft",
    )(table, ab)


def _gla_dir(qk, v, vt, a, s_in, fwd):
    p = GLA_PAIR
    r = lax.broadcasted_iota(jnp.int32, (p, p), 0)
    c = lax.broadcasted_iota(jnp.int32, (p, p), 1)
    same = (r // GLA_CHUNK) == (c // GLA_CHUNK)
    tri = same & ((c <= r) if fwd else (c >= r))
    row_lo = r < GLA_CHUNK
    rin = r % GLA_CHUNK

    q = qk[:, :GLA_QK] * (GLA_DK ** -0.5)
    k = qk[:, GLA_QK:]
    cum = a
    sh = 1
    while sh < GLA_CHUNK:
        if fwd:
            cum = cum + jnp.where(rin >= sh, pltpu.roll(cum, sh, 0), 0.0)
        else:
            cum = cum + jnp.where(rin < GLA_CHUNK - sh, pltpu.roll(cum, p - sh, 0), 0.0)
        sh *= 2
    if fwd:
        last0, last1 = cum[GLA_CHUNK - 1:GLA_CHUNK, :], cum[p - 1:p, :]
    else:
        last0, last1 = cum[0:1, :], cum[GLA_CHUNK:GLA_CHUNK + 1, :]
    lastb = jnp.where(row_lo, last0, last1)
    qt = q * jnp.exp(cum)
    kt = k * jnp.exp(-cum)
    kd = k * jnp.exp(lastb - cum)

    kt_b = kt.astype(BF16)
    zk = jnp.zeros_like(kt_b)
    ks = jnp.concatenate([jnp.where((c // GLA_DK) == hh, kt_b, zk) for hh in range(GLA_HEADS)], axis=0)
    att = _nt_dot(qt.astype(BF16), ks)
    tri4 = jnp.concatenate([tri] * GLA_HEADS, axis=1)
    att = jnp.where(tri4, att, 0.0).astype(BF16)
    col = lax.broadcasted_iota(jnp.int32, (p, GLA_WIDTH), 1)
    zv = jnp.zeros_like(v)
    vs = jnp.concatenate([jnp.where((col // GLA_DV) == hh, v, zv) for hh in range(GLA_HEADS)], axis=0)
    o_intra = jnp.dot(att, vs, preferred_element_type=F32)

    sr = lax.broadcasted_iota(jnp.int32, (GLA_WIDTH, GLA_QK), 0)
    sc = lax.broadcasted_iota(jnp.int32, (GLA_WIDTH, GLA_QK), 1)
    bdm = (sr // GLA_DV) == (sc // GLA_DK)
    first, second = (0, 1) if fwd else (1, 0)
    lasts = (last0, last1)
    in_chunk = (row_lo, jnp.logical_not(row_lo))
    kd2 = jnp.concatenate([jnp.where(in_chunk[0], kd, 0.0), jnp.where(in_chunk[1], kd, 0.0)], axis=1).astype(BF16)
    kvt2 = jnp.dot(vt, kd2, preferred_element_type=F32)
    kvt = (kvt2[:, :GLA_QK], kvt2[:, GLA_QK:])
    s_a = s_in
    s_b = s_a * jnp.exp(lasts[first]) + jnp.where(bdm, kvt[first], 0.0)
    s_c = s_b * jnp.exp(lasts[second]) + jnp.where(bdm, kvt[second], 0.0)
    q2 = jnp.concatenate([jnp.where(in_chunk[first], qt, 0.0), jnp.where(in_chunk[second], qt, 0.0)], axis=1)
    s2 = jnp.concatenate([s_a, s_b], axis=1).astype(BF16)
    o_inter = _nt_dot(q2.astype(BF16), s2)
    return o_intra + o_inter, s_c


def _gla_kernel(qkf, vf, vtf, laf, qkb, vb, vtb, lab, s0_ref, of_ref, ob_ref, sfin_ref, s_scr):
    i = pl.program_id(1)

    @pl.when(i == 0)
    def _():
        s_scr[...] = s0_ref[...]

    pairs = qkf.shape[1] // GLA_PAIR
    for gi in range(qkf.shape[0]):
        sf = s_scr[gi, 0]
        sb = s_scr[gi, 1]
        for pi in range(pairs):
            rf = slice(pi * GLA_PAIR, (pi + 1) * GLA_PAIR)
            rb = slice((pairs - 1 - pi) * GLA_PAIR, (pairs - pi) * GLA_PAIR)
            o1, sf = _gla_dir(qkf[gi, rf, :], vf[gi, rf, :], vtf[gi, :, rf], laf[gi, rf, :], sf, True)
            o2, sb = _gla_dir(qkb[gi, rb, :], vb[gi, rb, :], vtb[gi, :, rb], lab[gi, rb, :], sb, False)
            of_ref[gi, rf, :] = o1.astype(of_ref.dtype)
            ob_ref[gi, rb, :] = o2.astype(ob_ref.dtype)
        s_scr[gi, 0] = sf
        s_scr[gi, 1] = sb

    @pl.when(i == pl.num_programs(1) - 1)
    def _():
        sfin_ref[...] = s_scr[...]


def _gla_call(gqk, gv, gvt, la, s0):
    b, n, _ = gqk.shape
    p = GLA_PAIR * max(k for k in (1, 2, 4) if n % (k * GLA_PAIR) == 0)
    npair = n // p
    gb = 4 if b % 4 == 0 else (2 if b % 2 == 0 else 1)
    fw = lambda bi, i: (bi, i, 0)
    bw = lambda bi, i: (bi, npair - 1 - i, 0)
    in_specs = [
        pl.BlockSpec((gb, p, 2 * GLA_QK), fw),
        pl.BlockSpec((gb, p, GLA_WIDTH), fw),
        pl.BlockSpec((gb, GLA_WIDTH, p), lambda bi, i: (bi, 0, i)),
        pl.BlockSpec((gb, p, GLA_QK), fw),
        pl.BlockSpec((gb, p, 2 * GLA_QK), bw),
        pl.BlockSpec((gb, p, GLA_WIDTH), bw),
        pl.BlockSpec((gb, GLA_WIDTH, p), lambda bi, i: (bi, 0, npair - 1 - i)),
        pl.BlockSpec((gb, p, GLA_QK), lambda bi, i: (bi, npair - 1 - i, 1)),
        pl.BlockSpec((gb, 2, GLA_WIDTH, GLA_QK), lambda bi, i: (bi, 0, 0, 0)),
    ]
    out_specs = (
        pl.BlockSpec((gb, p, GLA_WIDTH), fw),
        pl.BlockSpec((gb, p, GLA_WIDTH), bw),
        pl.BlockSpec((gb, 2, GLA_WIDTH, GLA_QK), lambda bi, i: (bi, 0, 0, 0)),
    )
    out_shape = (
        jax.ShapeDtypeStruct((b, n, GLA_WIDTH), BF16),
        jax.ShapeDtypeStruct((b, n, GLA_WIDTH), BF16),
        jax.ShapeDtypeStruct((b, 2, GLA_WIDTH, GLA_QK), F32),
    )
    return pl.pallas_call(
        _gla_kernel,
        grid=(b // gb, npair),
        in_specs=in_specs,
        out_specs=out_specs,
        out_shape=out_shape,
        scratch_shapes=[pltpu.VMEM((gb, 2, GLA_WIDTH, GLA_QK), F32)],
        compiler_params=_cparams(("parallel", "arbitrary")),
        name="gla_scan",
    )(gqk, gv, gvt, la, gqk, gv, gvt, la, s0)


def _attn_heads(q_ref, parts):
    tq = q_ref.shape[0]
    lane = lax.broadcasted_iota(jnp.int32, (tq, 128), 1)
    lo = lane < ATT_HDIM
    blocks = []
    for j in range(ATT_HEADS // 2):
        q128 = q_ref[:, 128 * j:128 * (j + 1)]
        g = (2 * j) // (ATT_HEADS // ATT_KV_HEADS)
        outs = []
        for half in range(2):
            qm = jnp.where(lo if half == 0 else jnp.logical_not(lo), q128, jnp.zeros_like(q128))
            ss = [jnp.dot(qm, kt_ref[g], preferred_element_type=F32) for kt_ref, _ in parts]
            m = functools.reduce(jnp.maximum, [jnp.max(s, axis=-1, keepdims=True) for s in ss])
            ps = [jnp.exp2(s - m).astype(BF16) for s in ss]
            pv = functools.reduce(
                lambda u, w: u + w,
                [jnp.dot(pp, vd_ref[2 * g + half], preferred_element_type=F32) for pp, (_, vd_ref) in zip(ps, parts)])
            den = pv[:, ATT_HDIM:ATT_HDIM + 1] if half == 0 else pv[:, 0:1]
            outs.append(pv / den)
        blocks.append(jnp.where(lo, outs[0], outs[1]).astype(BF16))
    return jnp.concatenate(blocks, axis=1)


def _attn_outproj_kernel(*refs, nparts):
    q_ref = refs[0]
    parts = [(refs[1 + 2 * i], refs[2 + 2 * i]) for i in range(nparts)]
    (f_ref, of_ref, ob_ref, gg_ref, x_ref, mod_ref, w_ref, gn_ref, bd_ref, nf_ref, wr_ref,
     xn_ref, h2_ref, lg_ref) = refs[1 + 2 * nparts:]
    att = _attn_heads(q_ref, parts)
    o = of_ref[...].astype(F32) + ob_ref[...].astype(F32)
    ms = jnp.dot((o * o).astype(BF16), bd_ref[...], preferred_element_type=F32) * (1.0 / GLA_DV)
    on = o * lax.rsqrt(ms + EPS) * gn_ref[...]
    g = gg_ref[...].astype(F32)
    gl = (on * (g * _sigmoid(g))).astype(BF16)
    ox = (jnp.dot(f_ref[...], w_ref[0:FNET_WIDTH, :], preferred_element_type=F32)
          + jnp.dot(gl, w_ref[FNET_WIDTH:FNET_WIDTH + GLA_WIDTH, :], preferred_element_type=F32)
          + jnp.dot(att, w_ref[FNET_WIDTH + GLA_WIDTH:, :], preferred_element_type=F32))
    xn = x_ref[...] + mod_ref[2:3, :] * ox
    xn_ref[...] = xn
    ms2 = jnp.mean(xn * xn, axis=-1, keepdims=True)
    h2 = xn * lax.rsqrt(ms2 + EPS) * nf_ref[...] * (1.0 + mod_ref[4:5, :]) + mod_ref[3:4, :]
    h2_ref[...] = _pack_bf16_pairs(h2)
    lg2 = _nt_dot(wr_ref[...], h2.astype(BF16))
    lg_ref[...] = lg2[:N_EXPERTS, :] + lg2[N_EXPERTS:, :]


def _attn_outproj_call(q, kv_parts, f, of, ob, gg, x, mod_l, mod_row, w_out, gn, bd, nf, wr):
    b, n, d = x.shape
    tm = min(1024, n)
    nt = n // tm
    if mod_row is None:
        mod_map = lambda bi, i: (bi, 0, 0)
    else:
        mod_map = lambda bi, i: (mod_row, 0, 0)
    const = lambda bi, i: (0, 0)
    tok = lambda w: pl.BlockSpec((None, tm, w), lambda bi, i: (bi, i, 0))
    in_specs = [tok(ATT_WIDTH)]
    args = [q]
    for kt, vd in kv_parts:
        m = kt.shape[-1]
        in_specs.append(pl.BlockSpec((None, ATT_KV_HEADS, 128, m), lambda bi, i: (bi, 0, 0, 0)))
        in_specs.append(pl.BlockSpec((None, 2 * ATT_KV_HEADS, m, 128), lambda bi, i: (bi, 0, 0, 0)))
        args += [kt, vd]
    in_specs += [
        tok(FNET_WIDTH), tok(GLA_WIDTH), tok(GLA_WIDTH), tok(GLA_WIDTH), tok(d),
        pl.BlockSpec((None, 6, d), mod_map),
        pl.BlockSpec((None, d, d), lambda bi, i: (w_out[1], 0, 0)),
        pl.BlockSpec((1, GLA_WIDTH), const),
        pl.BlockSpec((GLA_WIDTH, GLA_WIDTH), const),
        pl.BlockSpec((1, d), const),
        pl.BlockSpec((2 * N_EXPERTS, d), const),
    ]
    args += [f, of, ob, gg, x, mod_l, w_out[0], gn, bd, nf, wr]
    out_specs = (
        tok(d), tok(d // 2),
        pl.BlockSpec((N_EXPERTS, tm), lambda bi, i: (0, bi * nt + i)),
    )
    out_shape = (
        jax.ShapeDtypeStruct((b, n, d), F32),
        jax.ShapeDtypeStruct((b, n, d // 2), jnp.int32),
        jax.ShapeDtypeStruct((N_EXPERTS, b * n), F32),
    )
    return pl.pallas_call(
        functools.partial(_attn_outproj_kernel, nparts=len(kv_parts)),
        grid=(b, nt),
        in_specs=in_specs,
        out_specs=out_specs,
        out_shape=out_shape,
        compiler_params=_cparams(("parallel", "parallel")),
        name="attn_outproj",
    )(*args)


MOE_TILE_SLOTS = 256
MOE_TILE_ROWS = 8
MOE_META = 4

def _route_kernel(b_ref, lg_ref, pos_ref, wt_ref, te_ref, nv_ref, *, tm):
    r = lg_ref.shape[1]
    s = [_sigmoid(lg_ref[e]) for e in range(N_EXPERTS)]
    sel = [s[e] + b_ref[e] for e in range(N_EXPERTS)]
    grp = []
    for g in range(N_GROUPS):
        a, b, c, d = sel[4 * g:4 * g + 4]
        hi1, lo1 = jnp.maximum(a, b), jnp.minimum(a, b)
        hi2, lo2 = jnp.maximum(c, d), jnp.minimum(c, d)
        m1 = jnp.maximum(hi1, hi2)
        m2 = jnp.maximum(jnp.minimum(hi1, hi2), jnp.maximum(lo1, lo2))
        grp.append(m1 + m2)
    one = jnp.ones_like(s[0])
    zero = jnp.zeros_like(s[0])
    msk = []
    for g in range(N_GROUPS):
        isg = one
        for g2 in range(N_GROUPS):
            if g2 < g:
                isg = isg * jnp.where(grp[g] > grp[g2], one, zero)
            elif g2 > g:
                isg = isg * jnp.where(grp[g] >= grp[g2], one, zero)
        for li in range(EXPERTS_PER_GROUP):
            e = 4 * g + li
            rank = zero
            for lj in range(EXPERTS_PER_GROUP):
                ej = 4 * g + lj
                if lj < li:
                    rank = rank + jnp.where(sel[ej] >= sel[e], one, zero)
                elif lj > li:
                    rank = rank + jnp.where(sel[ej] > sel[e], one, zero)
            msk.append(jnp.where(rank < 2.0, isg, zero))
    den = functools.reduce(lambda u, v: u + v, [msk[e] * s[e] for e in range(N_EXPERTS)])

    li_ = lax.broadcasted_iota(jnp.int32, (128, 128), 0)
    lj_ = lax.broadcasted_iota(jnp.int32, (128, 128), 1)
    upper = jnp.where(li_ < lj_, 1.0, 0.0).astype(BF16)
    ri_ = lax.broadcasted_iota(jnp.int32, (r, r), 0)
    rj_ = lax.broadcasted_iota(jnp.int32, (r, r), 1)
    lower = jnp.where(rj_ < ri_, 1.0, 0.0).astype(BF16)
    tiles = (1, te_ref.shape[1])
    tile_start = lax.broadcasted_iota(jnp.int32, tiles, 1).astype(F32) * float(tm)
    te = jnp.zeros(tiles, F32)
    seg = jnp.zeros(tiles, F32)
    nonempty = []
    off = jnp.zeros((1, 1), F32)
    seen = zero
    pos = [zero, zero]
    wts = [zero, zero]
    for e in range(N_EXPERTS):
        mb = msk[e].astype(BF16)
        lane_pre = jnp.dot(mb, upper, preferred_element_type=F32)
        row_pre = jnp.sum(jnp.dot(lower, mb, preferred_element_type=F32), axis=1, keepdims=True)
        cnt = jnp.sum(jnp.sum(msk[e], axis=1, keepdims=True), axis=0, keepdims=True)
        p_e = off + row_pre + lane_pre
        g_e = s[e] / den
        for kk in range(2):
            hit = msk[e] * jnp.where(seen == float(kk), one, zero)
            pos[kk] = pos[kk] + hit * p_e
            wts[kk] = wts[kk] + hit * g_e
        seen = seen + msk[e]
        off = off + jnp.floor((cnt + float(tm - 1)) * (1.0 / tm)) * float(tm)
        passed = jnp.where(tile_start >= off, 1.0, 0.0)
        te = te + passed
        nonempty.append(jnp.where(cnt > 0.0, 1.0, 0.0))
        seg = seg + nonempty[e] * passed
    for kk in range(2):
        pos_ref[kk] = pos[kk].astype(jnp.int32)
        wt_ref[kk] = wts[kk]
    te = jnp.minimum(te, float(N_EXPERTS - 1))
    nxt = jnp.full(tiles, -1.0, F32)
    for e in reversed(range(N_EXPERTS)):
        nxt = jnp.where(jnp.logical_and(nonempty[e] > 0.0, te < float(e)), float(e), nxt)
    nxt2 = jnp.full(tiles, -1.0, F32)
    for e in reversed(range(N_EXPERTS)):
        later = jnp.logical_and(nxt >= 0.0, nxt < float(e))
        nxt2 = jnp.where(jnp.logical_and(nonempty[e] > 0.0, later), float(e), nxt2)
    te_ref[0:1, :] = te.astype(jnp.int32)
    te_ref[1:2, :] = seg.astype(jnp.int32)
    te_ref[2:3, :] = nxt.astype(jnp.int32)
    te_ref[3:4, :] = nxt2.astype(jnp.int32)
    te_ref[4:, :] = jnp.zeros((te_ref.shape[0] - 4, te_ref.shape[1]), jnp.int32)
    nv_ref[...] = jnp.broadcast_to(off * (1.0 / tm), nv_ref.shape).astype(jnp.int32)


def _route_call(lgt, b_router, tm):
    n_tok = lgt.shape[1]
    r = n_tok // 128
    assert r * 128 == n_tok and r % 8 == 0 and 2 * n_tok // tm + N_EXPERTS <= MOE_TILE_SLOTS
    lg3 = lgt.reshape(N_EXPERTS, r, 128)
    full3 = lambda k: pl.BlockSpec((k, r, 128), lambda: (0, 0, 0))
    pos, wts, te, nv = pl.pallas_call(
        functools.partial(_route_kernel, tm=tm),
        in_specs=[pl.BlockSpec(memory_space=pltpu.SMEM), full3(N_EXPERTS)],
        out_specs=(full3(2), full3(2), pl.BlockSpec((MOE_TILE_ROWS, MOE_TILE_SLOTS), lambda: (0, 0)), pl.BlockSpec((1, 128), lambda: (0, 0))),
        out_shape=(
            jax.ShapeDtypeStruct((2, r, 128), jnp.int32),
            jax.ShapeDtypeStruct((2, r, 128), F32),
            jax.ShapeDtypeStruct((MOE_TILE_ROWS, MOE_TILE_SLOTS), jnp.int32),
            jax.ShapeDtypeStruct((1, 128), jnp.int32),
        ),
        compiler_params=pltpu.CompilerParams(vmem_limit_bytes=VMEM_LIMIT),
        name="route",
    )(b_router, lg3)
    return pos.reshape(2, n_tok), wts, te[:MOE_META].reshape(MOE_META * MOE_TILE_SLOTS), nv[0, :1]


SC_CORES = 2
SC_SUBCORES = 16
SC_WORKERS = SC_CORES * SC_SUBCORES
SC_CHUNK = 32


def _sc_mesh():
    return plsc.VectorSubcoreMesh(core_axis_name="c", subcore_axis_name="s",
                                  num_cores=SC_CORES, num_subcores=SC_SUBCORES)


def _sc_steps(n_rows):
    per_w = n_rows // SC_WORKERS
    steps = per_w // SC_CHUNK
    assert per_w * SC_WORKERS == n_rows and steps * SC_CHUNK == per_w and steps % 2 == 0, n_rows
    return per_w, steps


def _sc_gather_rows(table, idx):
    p = idx.shape[0]
    d = table.shape[1]
    per_w, steps = _sc_steps(p)
    idx3 = idx.reshape(SC_WORKERS, steps, SC_CHUNK)

    @functools.partial(
        pl.kernel, mesh=_sc_mesh(),
        out_type=jax.ShapeDtypeStruct((p, d), table.dtype),
        scratch_types=[
            pltpu.VMEM((steps, SC_CHUNK), jnp.int32),
            pltpu.VMEM((SC_CHUNK, d), table.dtype),
            pltpu.VMEM((SC_CHUNK, d), table.dtype),
            pltpu.SemaphoreType.DMA, pltpu.SemaphoreType.DMA,
            pltpu.SemaphoreType.DMA, pltpu.SemaphoreType.DMA,
        ],
        name="sc_gather_rows",
    )
    def k(table_hbm, idx_hbm, out_hbm, idx_v, buf0, buf1, g0, g1, w0, w1):
        wid = lax.axis_index("s") * SC_CORES + lax.axis_index("c")
        base = wid * per_w
        pltpu.sync_copy(idx_hbm.at[wid], idx_v)

        def gather(s, buf, sem):
            return pltpu.make_async_copy(table_hbm.at[idx_v.at[s]], buf, sem)

        def write(s, buf, sem):
            return pltpu.make_async_copy(buf, out_hbm.at[pl.ds(base + s * SC_CHUNK, SC_CHUNK)], sem)

        gather(0, buf0, g0).start()

        @pl.loop(0, steps, step=2)
        def _(s):
            gather(s + 1, buf1, g1).start()
            gather(s, buf0, g0).wait()
            write(s, buf0, w0).start()
            write(s, buf0, w0).wait()

            @pl.when(s + 2 < steps)
            def _():
                gather(s + 2, buf0, g0).start()

            gather(s + 1, buf1, g1).wait()
            write(s + 1, buf1, w1).start()
            write(s + 1, buf1, w1).wait()

    return k(table, idx3)


def _sc_dispatch(srcs, poss, p_rows):
    d = srcs[0].shape[1]
    dt = srcs[0].dtype
    plans = [_sc_steps(src.shape[0]) for src in srcs]
    idxs = [pos.reshape(2, SC_WORKERS, st, SC_CHUNK) for pos, (_, st) in zip(poss, plans)]
    nseg = len(srcs)
    scratch = [pltpu.VMEM((2, st, SC_CHUNK), jnp.int32) for _, st in plans]
    scratch += [pltpu.VMEM((SC_CHUNK, d), dt), pltpu.VMEM((SC_CHUNK, d), dt)]
    scratch += [pltpu.SemaphoreType.DMA] * 6

    @functools.partial(
        pl.kernel, mesh=_sc_mesh(),
        out_type=jax.ShapeDtypeStruct((p_rows, d), dt),
        scratch_types=scratch,
        name="sc_dispatch",
    )
    def k(*refs):
        src_hbm = refs[:nseg]
        idx_hbm = refs[nseg:2 * nseg]
        out_hbm = refs[2 * nseg]
        idx_v = refs[2 * nseg + 1:3 * nseg + 1]
        buf0, buf1, r0, r1, a0, a1, b0, b1 = refs[3 * nseg + 1:]
        wid = lax.axis_index("s") * SC_CORES + lax.axis_index("c")
        for seg in range(nseg):
            per_w, steps = plans[seg]
            base = wid * per_w
            for kk in range(2):
                pltpu.sync_copy(idx_hbm[seg].at[kk, wid], idx_v[seg].at[kk])

            def read(s, buf, sem, seg=seg, base=base):
                return pltpu.make_async_copy(src_hbm[seg].at[pl.ds(base + s * SC_CHUNK, SC_CHUNK)], buf, sem)

            def scat(kk, s, buf, sem, seg=seg):
                return pltpu.make_async_copy(buf, out_hbm.at[idx_v[seg].at[kk, s]], sem)

            read(0, buf0, r0).start()

            @pl.loop(0, steps, step=2)
            def _(s, read=read, scat=scat, steps=steps):
                read(s + 1, buf1, r1).start()
                read(s, buf0, r0).wait()
                scat(0, s, buf0, a0).start()
                scat(1, s, buf0, b0).start()
                scat(0, s, buf0, a0).wait()
                scat(1, s, buf0, b0).wait()

                @pl.when(s + 2 < steps)
                def _():
                    read(s + 2, buf0, r0).start()

                read(s + 1, buf1, r1).wait()
                scat(0, s + 1, buf1, a1).start()
                scat(1, s + 1, buf1, b1).start()
                scat(0, s + 1, buf1, a1).wait()
                scat(1, s + 1, buf1, b1).wait()

    return k(*srcs, *idxs)


MOE_TM = 512


MOE_WSLOTS = 4


def _experts_kernel(te_ref, nv_ref, xs_ref, wg_hbm, wu_hbm, wd_hbm, ys_ref, wg_v, wu_v, wd_v, sem, *, layer):
    i = pl.program_id(0)
    nv = nv_ref[0]
    rows = MOE_TM

    def weight_copies(expert, s):
        return (pltpu.make_async_copy(wg_hbm.at[layer, expert], wg_v.at[s], sem.at[s, 0]),
                pltpu.make_async_copy(wu_hbm.at[layer, expert], wu_v.at[s], sem.at[s, 1]),
                pltpu.make_async_copy(wd_hbm.at[layer, expert], wd_v.at[s], sem.at[s, 2]))

    def start(expert, s):
        for cp in weight_copies(expert, s):
            cp.start(priority=1)

    experts, slots, valids = [], [], []
    for k in range(2):
        t = 2 * i + k
        e = te_ref[t]
        run = te_ref[MOE_TILE_SLOTS + t]
        nxt1 = te_ref[2 * MOE_TILE_SLOTS + t]
        nxt2 = te_ref[3 * MOE_TILE_SLOTS + t]
        slot = lax.rem(run, MOE_WSLOTS)
        valid = t < nv
        first = jnp.logical_or(t == 0, e != te_ref[jnp.maximum(t - 1, 0)])
        if k == 0:
            @pl.when(jnp.logical_and(valid, i == 0))
            def _(e=e, nxt1=nxt1):
                start(e, 0)

                @pl.when(nxt1 >= 0)
                def _():
                    start(nxt1, 1)

        @pl.when(jnp.logical_and(valid, first))
        def _(e=e, slot=slot, run=run, nxt2=nxt2):
            for cp in weight_copies(e, slot):
                cp.wait()

            @pl.when(nxt2 >= 0)
            def _():
                start(nxt2, lax.rem(run + 2, MOE_WSLOTS))

        experts.append(e)
        slots.append(slot)
        valids.append(valid)

    def mlp(packed, slot):
        h = _unpack_bf16_pairs(packed).astype(BF16)
        half = D_EXPERT // 2
        y = None
        for j in range(2):
            sl = slice(j * half, (j + 1) * half)
            a = jnp.dot(h, wg_v[slot, :, sl].astype(BF16), preferred_element_type=F32)
            u = jnp.dot(h, wu_v[slot, :, sl].astype(BF16), preferred_element_type=F32)
            t = ((a * _sigmoid(a)) * u).astype(BF16)
            yj = jnp.dot(t, wd_v[slot, sl, :].astype(BF16), preferred_element_type=F32)
            y = yj if y is None else y + yj
        return _pack_bf16_pairs(y)

    same = jnp.logical_and(valids[1], experts[0] == experts[1])

    @pl.when(jnp.logical_and(valids[0], same))
    def _():
        ys_ref[...] = mlp(xs_ref[...], slots[0])

    @pl.when(jnp.logical_and(valids[0], jnp.logical_not(same)))
    def _():
        ys_ref[0:rows, :] = mlp(xs_ref[0:rows, :], slots[0])

    @pl.when(jnp.logical_and(valids[1], jnp.logical_not(same)))
    def _():
        ys_ref[rows:, :] = mlp(xs_ref[rows:, :], slots[1])


def _experts_call(xs, te, nv, wg, wu, wd, layer):
    p_rows, dh = xs.shape
    d = 2 * dh
    tm = 2 * MOE_TM
    nt = p_rows // tm
    assert nt * tm == p_rows and 2 * nt <= MOE_TILE_SLOTS and te.shape == (MOE_META * MOE_TILE_SLOTS,)
    row = lambda i, te_r, nv_r: (jnp.minimum(i, (nv_r[0] + 1) // 2 - 1), 0)
    grid_spec = pltpu.PrefetchScalarGridSpec(
        num_scalar_prefetch=2,
        grid=(nt,),
        in_specs=[
            pl.BlockSpec((tm, dh), row),
            pl.BlockSpec(memory_space=pl.ANY),
            pl.BlockSpec(memory_space=pl.ANY),
            pl.BlockSpec(memory_space=pl.ANY),
        ],
        out_specs=pl.BlockSpec((tm, dh), row),
        scratch_shapes=[
            pltpu.VMEM((MOE_WSLOTS, d, D_EXPERT), F32),
            pltpu.VMEM((MOE_WSLOTS, d, D_EXPERT), F32),
            pltpu.VMEM((MOE_WSLOTS, D_EXPERT, d), F32),
            pltpu.SemaphoreType.DMA((MOE_WSLOTS, 3)),
        ],
    )
    return pl.pallas_call(
        functools.partial(_experts_kernel, layer=layer),
        grid_spec=grid_spec,
        out_shape=jax.ShapeDtypeStruct((p_rows, dh), jnp.int32),
        compiler_params=_cparams(("arbitrary",)),
        name="moe_experts",
    )(te, nv, xs, wg, wu, wd)


COMBINE_TM = 1024


def _combine_kernel(y_ref, wt_ref, x_ref, mod_ref, fn_ref, o_ref, *, final):
    xo = x_ref[...] + mod_ref[5:6, :] * _moe_mix(y_ref, wt_ref[0], wt_ref[1])
    if final:
        ms = jnp.mean(xo * xo, axis=-1, keepdims=True)
        xo = xo * lax.rsqrt(ms + EPS) * fn_ref[...]
    o_ref[...] = xo


def _combine_call(y2, wts3, row0, x, mod_l, mod_row, fn, final):
    b, n, d = x.shape
    n_tok = b * n
    tm = COMBINE_TM
    assert n_tok % tm == 0 and row0 % 8 == 0 and (mod_row is not None or n % tm == 0)
    if mod_row is None:
        mod_map = lambda i: ((i * tm) // n, 0, 0)
    else:
        mod_map = lambda i: (mod_row, 0, 0)
    out = pl.pallas_call(
        functools.partial(_combine_kernel, final=final),
        grid=(n_tok // tm,),
        in_specs=[
            pl.BlockSpec((2, tm, d // 2), lambda i: (0, i, 0)),
            pl.BlockSpec((2, tm // 128, 128), lambda i: (0, row0 // 8 + i, 0)),
            pl.BlockSpec((tm, d), lambda i: (i, 0)),
            pl.BlockSpec((None, 6, d), mod_map),
            pl.BlockSpec((1, d), lambda i: (0, 0)),
        ],
        out_specs=pl.BlockSpec((tm, d), lambda i: (i, 0)),
        out_shape=jax.ShapeDtypeStruct((n_tok, d), F32),
        compiler_params=_cparams(("parallel",)),
        name="moe_combine",
    )(y2, wts3, x.reshape(n_tok, d), mod_l, fn)
    return out.reshape(b, n, d)


def _moe_sparse(h_list, lg_list, x_list, mod_l, mod_rows, b_router, wg, wu, wd, layer, fn, final):
    d = h_list[0].shape[-1]
    sizes = [h.shape[0] * h.shape[1] for h in h_list]
    n_tok = sum(sizes)
    lgt = lg_list[0] if len(lg_list) == 1 else jnp.concatenate(lg_list, axis=1)
    pos, wts, te, nv = _route_call(lgt, b_router, MOE_TM)
    p_rows = 2 * n_tok + N_EXPERTS * MOE_TM
    offs = np.cumsum([0] + sizes)
    poss = [pos[:, offs[i]:offs[i + 1]] for i in range(len(sizes))]
    xs = _sc_dispatch([h.reshape(-1, d) for h in h_list], poss, p_rows)
    ys = _experts_call(xs, te, nv, wg, wu, wd, layer)
    outs = []
    for i, x in enumerate(x_list):
        y2 = _sc_gather_rows(ys, poss[i].reshape(-1)).reshape(2, sizes[i], d)
        row0 = int(offs[i]) // 128
        if final:
            outs.append(_combine_call(y2, wts, row0, x, mod_l, mod_rows[i], fn, True))
        else:
            outs.append((y2, wts, row0, mod_l))
    return outs


def _winprep_kernel(wt_ref, o_ref):
    gd0 = C_AQ
    gdw = 2 * GLA_GATE_RANK
    tail = W_IN_REF_COLS - gd0 - gdw
    o_ref[:, 0:gd0] = wt_ref[0:gd0, :].T.astype(BF16)
    o_ref[:, gd0:gd0 + tail] = wt_ref[gd0 + gdw:W_IN_REF_COLS, :].T.astype(BF16)
    gd = wt_ref[gd0:gd0 + 128, :].T
    lane = lax.broadcasted_iota(jnp.int32, gd.shape, 1)
    o_ref[:, C_GD:] = jnp.where(lane < gdw, gd, 0.0).astype(BF16)


def _winprep_call(w_in):
    depth, d, cols = w_in.shape
    assert cols == W_IN_REF_COLS and C_GD == cols - 2 * GLA_GATE_RANK and W_IN_COLS - C_GD == 128
    return pl.pallas_call(
        _winprep_kernel,
        grid=(depth,),
        in_specs=[pl.BlockSpec((None, cols, d), lambda l: (l, 0, 0))],
        out_specs=pl.BlockSpec((None, d, W_IN_COLS), lambda l: (l, 0, 0)),
        out_shape=jax.ShapeDtypeStruct((depth, d, W_IN_COLS), BF16),
        compiler_params=_cparams(("parallel",)),
        name="w_in_prep",
    )(jnp.swapaxes(w_in, 1, 2))


def _gate_up_weights(w_up, b_up):
    z = jnp.zeros((GLA_GATE_RANK, GLA_QK), w_up.dtype)
    top = jnp.concatenate([w_up[0], z], axis=1)
    mid = jnp.concatenate([z, w_up[1]], axis=1)
    pad = jnp.zeros((128 - 2 * GLA_GATE_RANK, 2 * GLA_QK), w_up.dtype)
    return jnp.concatenate([top, mid, pad], axis=0).astype(BF16), b_up.reshape(1, 2 * GLA_QK)


def kernel(x, c, ctx, c_ctx, w_ada, b_ada, norm_mix, norm_ffn, w_in, w_gla_gate_up, b_gla_gate, gla_norm, q_norm,
           k_norm, w_out, w_router, b_router, w_exp_gate, w_exp_up, w_exp_down, final_norm):
    b, n, d = x.shape
    m = ctx.shape[1]
    depth = w_ada.shape[0]
    assert d == D_MODEL and n % GLA_PAIR == 0 and m % GLA_PAIR == 0 and n % GRID_W == 0

    rows = ((b + 1 + 7) // 8) * 8
    cv = jnp.concatenate([c, c_ctx[None, :], jnp.zeros((rows - b - 1, d), F32)], axis=0)
    mod = _ada_call(cv, w_ada, b_ada).reshape(depth, rows, 6, d)

    cs = jnp.asarray(_channel_dft_table()).astype(BF16)
    tab_x = jnp.asarray(_seq_dft_table(n)).astype(BF16)
    tab_c = jnp.asarray(_seq_dft_table(m)).astype(BF16)
    rope_tabs = tuple(jnp.asarray(t) for t in _rope_tables(n))
    bd512 = jnp.asarray(_blockdiag_ones(ATT_WIDTH, ATT_HDIM)).astype(BF16)
    bd256 = jnp.asarray(_blockdiag_ones(GLA_WIDTH, GLA_DV)).astype(BF16)
    wr_t = w_router.T
    wr_hi = wr_t.astype(BF16)
    wrh = jnp.concatenate([wr_hi, (wr_t - wr_hi.astype(F32)).astype(BF16)], axis=0)
    fn = final_norm.reshape(1, d)

    w_in_perm = _winprep_call(w_in)
    w_out_bf = w_out.astype(BF16)

    xc = ctx
    pend_x = pend_c = None
    for l in range(depth):
        ctx_out = l < depth - 1
        mod_l = mod[l]
        w_perm = (w_in_perm, l)
        wup, bup = _gate_up_weights(w_gla_gate_up[l], b_gla_gate[l])
        nw = norm_mix[l].reshape(1, d)
        nf = norm_ffn[l].reshape(1, d)
        qn = jnp.tile(q_norm[l], ATT_HEADS).reshape(1, ATT_WIDTH)
        kn = jnp.tile(k_norm[l], ATT_KV_HEADS).reshape(1, 128)
        gn = jnp.tile(gla_norm[l], GLA_HEADS).reshape(1, GLA_WIDTH)
        wo = (w_out_bf, l)

        pc = _inproj_call(xc, mod_l, b, nw, w_perm, cs, wup, bup, qn, kn, bd512, None, pend_c, full=ctx_out)
        px = _inproj_call(x, mod_l, None, nw, w_perm, cs, wup, bup, qn, kn, bd512, rope_tabs, pend_x)
        if pend_x is not None:
            x, xc = px["x"], pc["x"]

        s_zero = jnp.zeros((b, 2, GLA_WIDTH, GLA_QK), F32)
        of_c, ob_c, s_fin = _gla_call(pc["gqk"], pc["gv"], pc["gvt"], pc["la"], s_zero)
        of_x, ob_x, _ = _gla_call(px["gqk"], px["gv"], px["gvt"], px["la"], s_fin)

        kv_c = (pc["kt"], pc["vd"])
        f_x = _seqdft_call(tab_x, px["ab"].reshape(2 * n, b * FNET_WIDTH), b)
        x, h2_x, lg_x = _attn_outproj_call(px["q"], [kv_c, (px["kt"], px["vd"])], f_x, of_x, ob_x, px["gg"], x,
                                           mod_l, None, wo, gn, bd256, nf, wrh)

        if ctx_out:
            f_c = _seqdft_call(tab_c, pc["ab"].reshape(2 * m, b * FNET_WIDTH), b)
            xc, h2_c, lg_c = _attn_outproj_call(pc["q"], [kv_c], f_c, of_c, ob_c, pc["gg"], xc,
                                                mod_l, b, wo, gn, bd256, nf, wrh)

        final = l == depth - 1
        wexp = (w_exp_gate, w_exp_up, w_exp_down, l)
        if ctx_out:
            res = _moe_sparse([h2_x, h2_c], [lg_x, lg_c], [x, xc], mod_l, [None, b], b_router, *wexp, fn, final)
        else:
            res = _moe_sparse([h2_x], [lg_x], [x], mod_l, [None], b_router, *wexp, fn, final)
        if final:
            x = res[0]
        else:
            pend_x, pend_c = res
    return x
```
